```python
import jax, jax.numpy as jnp
from jax import lax
import numpy as np

D_MODEL = 1024
BATCH = 32
SEQ = 2048
DEPTH = 1

CHUNK = 64
Q_BLOCK = 128
SB_HEADS = 8
SB_HEAD_DIM = 64
DSA_HEADS = 8
DSA_HEAD_DIM = 64
IDX_HEADS = 4
IDX_HEAD_DIM = 64
DSA_TOPK_MAX = 256
MEM_LEN = 256
MEM_HEADS = 4
MEM_HEAD_DIM = 128
N_BRANCH = 3
N_EXPERTS = 32
TOP_K = 4
D_EXPERT = D_MODEL
SWIGLU_LIMIT = 7.0
SWIGLU_ALPHA = 1.702
ROPE_THETA = 10000.0
EPS = 1e-6
MOE_BLOCK = 512

SB_W = SB_HEADS * SB_HEAD_DIM
DSA_W = DSA_HEADS * DSA_HEAD_DIM
IDX_QW = IDX_HEADS * IDX_HEAD_DIM
MEM_W = MEM_HEADS * MEM_HEAD_DIM
BRANCH_WIDTH = SB_W
IN_SIZES = (SB_W, SB_W, SB_W, DSA_W, DSA_HEAD_DIM, DSA_HEAD_DIM, IDX_QW, IDX_HEAD_DIM, IDX_HEADS, MEM_W)
IN_WIDTH = sum(IN_SIZES)

kernel_name = "hybrid_stickbreak_dsa_memory_moe_layer"


def rms_norm(x, g):
    xf = x.astype(jnp.float32)
    y = xf * lax.rsqrt(jnp.mean(xf * xf, axis=-1, keepdims=True) + EPS)
    return (y * g.astype(jnp.float32)).astype(x.dtype)


def rope(x, pos):
    half = x.shape[-1] // 2
    inv = ROPE_THETA ** (-jnp.arange(half, dtype=jnp.float32) / half)
    ang = pos.astype(jnp.float32)[:, None] * inv[None, :]
    cos = jnp.cos(ang)[:, None, :]
    sin = jnp.sin(ang)[:, None, :]
    xf = x.astype(jnp.float32)
    x1, x2 = xf[..., :half], xf[..., half:]
    return jnp.concatenate([x1 * cos - x2 * sin, x1 * sin + x2 * cos], axis=-1).astype(x.dtype)


def to_blocks(a):
    return jnp.swapaxes(a.reshape(a.shape[0], -1, Q_BLOCK, *a.shape[2:]), 0, 1)


def from_blocks(o):
    o = jnp.swapaxes(o, 0, 1)
    return o.reshape(o.shape[0], -1, *o.shape[3:])


def stick_breaking_attention(q, k, v):
    S, d = q.shape[1], q.shape[-1]
    scale = d ** -0.5
    kpos = jnp.arange(S)

    def block(args):
        qb, start = args
        qpos = start + jnp.arange(Q_BLOCK)
        z = jnp.einsum('bqhd,bshd->bhqs', qb, k, preferred_element_type=jnp.float32) * scale
        earlier = (kpos[None, :] < qpos[:, None])[None, None]
        log_keep = jnp.where(earlier, jax.nn.log_sigmoid(-z), 0.0)
        between = lax.cumsum(log_keep, axis=3, reverse=True) - log_keep
        w = jnp.where(earlier, jnp.exp(jax.nn.log_sigmoid(z) + between), 0.0)
        return jnp.einsum('bhqs,bshd->bqhd', w.astype(v.dtype), v)

    starts = jnp.arange(S // Q_BLOCK) * Q_BLOCK
    return from_blocks(lax.map(block, (to_blocks(q), starts)))


def dsa_attention(q, k, v, qi, ki, wi, topk):
    S, d = q.shape[1], q.shape[-1]
    scale = d ** -0.5
    kchunk = jnp.arange(S) // CHUNK

    def block(args):
        qb, qib, wib, start = args
        qchunk = (start + jnp.arange(Q_BLOCK)) // CHUNK
        admissible = kchunk[None, :] <= qchunk[:, None]
        idx_logits = jnp.einsum('bqhd,bsd->bhqs', qib, ki,
                                preferred_element_type=jnp.float32) * (IDX_HEAD_DIM ** -0.5)
        score = jnp.einsum('bqh,bhqs->bqs', wib.astype(jnp.float32) * (IDX_HEADS ** -0.5),
                           jax.nn.relu(idx_logits))
        score = jnp.where(admissible[None], score, -jnp.inf)
        _, sel = lax.top_k(score, topk)
        valid = (sel // CHUNK) <= qchunk[None, :, None]
        k_sel = jax.vmap(lambda kb, ib: kb[ib])(k, sel)
        v_sel = jax.vmap(lambda vb, ib: vb[ib])(v, sel)
        s = jnp.einsum('bqhd,bqkd->bhqk', qb, k_sel, preferred_element_type=jnp.float32) * scale
        s = jnp.where(valid[:, None], s, -jnp.inf)
        p = jax.nn.softmax(s, axis=-1)
        return jnp.einsum('bhqk,bqkd->bqhd', p.astype(v.dtype), v_sel)

    starts = jnp.arange(S // Q_BLOCK) * Q_BLOCK
    out = lax.map(block, (to_blocks(q), to_blocks(qi), to_blocks(wi), starts))
    return from_blocks(out)


def memory_attention(q, mk, mv):
    scale = q.shape[-1] ** -0.5
    s = jnp.einsum('bshd,bmhd->bhsm', q, mk, preferred_element_type=jnp.float32) * scale
    p = jax.nn.softmax(s, axis=-1)
    return jnp.einsum('bhsm,bmhd->bshd', p.astype(mv.dtype), mv)


def moe_ffn(h, w_router, b_router, w_e_in, b_e_in, w_e_out, b_e_out):
    B, S, D = h.shape
    N = B * S
    xt = h.reshape(N, D)
    logits = jnp.dot(xt, w_router, preferred_element_type=jnp.float32) + b_router.astype(jnp.float32)
    top_val, top_idx = lax.top_k(logits, TOP_K)
    gate = jax.nn.softmax(top_val, axis=-1)
    A = N * TOP_K
    flat_e = top_idx.reshape(A)
    flat_tok = jnp.arange(A, dtype=jnp.int32) // TOP_K
    flat_w = gate.reshape(A)
    order = jnp.argsort(flat_e)
    se = flat_e[order]
    counts = jnp.bincount(flat_e, length=N_EXPERTS)
    padded = (counts + MOE_BLOCK - 1) // MOE_BLOCK * MOE_BLOCK
    start = jnp.cumsum(counts) - counts
    pend = jnp.cumsum(padded)
    pstart = pend - padded
    dest = pstart[se] + jnp.arange(A, dtype=jnp.int32) - start[se]
    nblk = -(-A // MOE_BLOCK) + N_EXPERTS
    P = nblk * MOE_BLOCK
    slot_tok = jnp.full((P,), N, jnp.int32).at[dest].set(flat_tok[order])
    slot_w = jnp.zeros((P,), jnp.float32).at[dest].set(flat_w[order])
    blk_e = jnp.minimum(jnp.searchsorted(pend, jnp.arange(nblk) * MOE_BLOCK, side='right'), N_EXPERTS - 1)
    x_pad = jnp.concatenate([xt, jnp.zeros((1, D), xt.dtype)], axis=0)

    def expert_block(args):
        tok, e = args
        xb = x_pad[tok]
        hb = jnp.dot(xb, w_e_in[e], preferred_element_type=jnp.float32) + b_e_in[e].astype(jnp.float32)
        g = jnp.minimum(hb[:, :D_EXPERT], SWIGLU_LIMIT)
        u = jnp.clip(hb[:, D_EXPERT:], -SWIGLU_LIMIT, SWIGLU_LIMIT)
        act = (u + 1.0) * (g * jax.nn.sigmoid(SWIGLU_ALPHA * g))
        return jnp.dot(act.astype(xt.dtype), w_e_out[e], preferred_element_type=jnp.float32) \
            + b_e_out[e].astype(jnp.float32)

    yb = lax.map(expert_block, (slot_tok.reshape(nblk, MOE_BLOCK), blk_e))
    y = yb.reshape(P, D) * slot_w[:, None]
    out = jax.ops.segment_sum(y, slot_tok, num_segments=N + 1)[:N]
    return out.reshape(B, S, D).astype(h.dtype)


def hybrid_layer(x, mem, pos, topk, g_mix, w_in, g_q_dsa, g_k_dsa, g_q_mem, g_k_mem, g_mem,
                 w_mem_kv, w_gate, b_gate, w_branch, w_out, g_ffn, w_router, b_router,
                 w_e_in, b_e_in, w_e_out, b_e_out):
    B, S, _ = x.shape
    h = rms_norm(x, g_mix)
    proj = jnp.dot(h, w_in)
    (sq, sk, sv, dq, dk, dv, iq, ik, iw, mq) = jnp.split(proj, [int(c) for c in np.cumsum(IN_SIZES)[:-1]], axis=-1)

    o_sb = stick_breaking_attention(sq.reshape(B, S, SB_HEADS, SB_HEAD_DIM),
                                    sk.reshape(B, S, SB_HEADS, SB_HEAD_DIM),
                                    sv.reshape(B, S, SB_HEADS, SB_HEAD_DIM)).reshape(B, S, SB_W)

    dq = rope(rms_norm(dq.reshape(B, S, DSA_HEADS, DSA_HEAD_DIM), g_q_dsa), pos)
    dk = rope(rms_norm(dk, g_k_dsa)[:, :, None, :], pos)[:, :, 0, :]
    iq = rope(iq.reshape(B, S, IDX_HEADS, IDX_HEAD_DIM), pos)
    ik = rope(ik[:, :, None, :], pos)[:, :, 0, :]
    o_dsa = dsa_attention(dq, dk, dv, iq, ik, iw, topk).reshape(B, S, DSA_W)

    mkv = jnp.dot(rms_norm(mem, g_mem), w_mem_kv)
    M = mem.shape[1]
    mk = rms_norm(mkv[..., :MEM_W].reshape(B, M, MEM_HEADS, MEM_HEAD_DIM), g_k_mem)
    mv = mkv[..., MEM_W:].reshape(B, M, MEM_HEADS, MEM_HEAD_DIM)
    mq = rms_norm(mq.reshape(B, S, MEM_HEADS, MEM_HEAD_DIM), g_q_mem)
    o_mem = memory_attention(mq, mk, mv).reshape(B, S, MEM_W)

    merged = None
    for n, o in enumerate((o_sb, o_dsa, o_mem)):
        gate = jax.nn.sigmoid((jnp.dot(h, w_gate[n]) + b_gate[n]).astype(jnp.float32))
        term = gate * jnp.dot(o, w_branch[n]).astype(jnp.float32)
        merged = term if merged is None else merged + term
    x = x + jnp.dot(merged.astype(x.dtype), w_out)

    x = x + moe_ffn(rms_norm(x, g_ffn), w_router, b_router, w_e_in, b_e_in, w_e_out, b_e_out)
    return x


def setup_inputs(seed: int = 0) -> dict:
    key = jax.random.key(seed)
    ks = jax.random.split(key, 21)
    f32 = jnp.float32
    L = DEPTH

    def nrm(k, shape, scale):
        return jax.random.normal(k, shape, f32) * scale

    def gain(k, shape):
        return 1.0 + 0.02 * jax.random.normal(k, shape, f32)

    return {
        "x": nrm(ks[0], (BATCH, SEQ, D_MODEL), 1.0),
        "mem": nrm(ks[1], (BATCH, MEM_LEN, D_MODEL), 1.0),
        "g_mix": gain(ks[2], (L, D_MODEL)),
        "w_in": nrm(ks[3], (L, D_MODEL, IN_WIDTH), D_MODEL ** -0.5),
        "g_q_dsa": gain(ks[4], (L, DSA_HEAD_DIM)),
        "g_k_dsa": gain(ks[5], (L, DSA_HEAD_DIM)),
        "g_q_mem": gain(ks[6], (L, MEM_HEAD_DIM)),
        "g_k_mem": gain(ks[7], (L, MEM_HEAD_DIM)),
        "g_mem": gain(ks[8], (L, D_MODEL)),
        "w_mem_kv": nrm(ks[9], (L, D_MODEL, 2 * MEM_W), D_MODEL ** -0.5),
        "w_gate": nrm(ks[10], (L, N_BRANCH, D_MODEL, D_MODEL), D_MODEL ** -0.5),
        "b_gate": nrm(ks[11], (L, N_BRANCH, D_MODEL), 0.02),
        "w_branch": nrm(ks[12], (L, N_BRANCH, BRANCH_WIDTH, D_MODEL), BRANCH_WIDTH ** -0.5),
        "w_out": nrm(ks[13], (L, D_MODEL, D_MODEL), D_MODEL ** -0.5),
        "g_ffn": gain(ks[14], (L, D_MODEL)),
        "w_router": nrm(ks[15], (L, D_MODEL, N_EXPERTS), D_MODEL ** -0.5),
        "b_router": nrm(ks[16], (L, N_EXPERTS), 0.01),
        "w_e_in": nrm(ks[17], (L, N_EXPERTS, D_MODEL, 2 * D_EXPERT), D_MODEL ** -0.5),
        "b_e_in": nrm(ks[18], (L, N_EXPERTS, 2 * D_EXPERT), 0.02),
        "w_e_out": nrm(ks[19], (L, N_EXPERTS, D_EXPERT, D_MODEL), D_EXPERT ** -0.5),
        "b_e_out": nrm(ks[20], (L, N_EXPERTS, D_MODEL), 0.02),
    }


def reference(x, mem, g_mix, w_in, g_q_dsa, g_k_dsa, g_q_mem, g_k_mem, g_mem, w_mem_kv,
              w_gate, b_gate, w_branch, w_out, g_ffn, w_router, b_router,
              w_e_in, b_e_in, w_e_out, b_e_out):
    S = x.shape[1]
    pos = jnp.arange(S)
    topk = min(DSA_TOPK_MAX, S // 4)
    for l in range(DEPTH):
        x = hybrid_layer(x, mem, pos, topk, g_mix[l], w_in[l], g_q_dsa[l], g_k_dsa[l],
                         g_q_mem[l], g_k_mem[l], g_mem[l], w_mem_kv[l], w_gate[l], b_gate[l],
                         w_branch[l], w_out[l], g_ffn[l], w_router[l], b_router[l],
                         w_e_in[l], b_e_in[l], w_e_out[l], b_e_out[l])
    return x
```

```python
import functools

import numpy as np
import jax
import jax.numpy as jnp
from jax import lax
from jax.experimental import pallas as pl
from jax.experimental.pallas import tpu as pltpu

F32 = jnp.float32
BF16 = jnp.bfloat16
I32 = jnp.int32

D_MODEL = 1024
CHUNK = 64
SB_HEADS = 8
DSA_HEADS = 8
HEAD_DIM = 64
IDX_HEADS = 4
DSA_TOPK_MAX = 256
MEM_HEADS = 4
MEM_HEAD_DIM = 128
N_EXPERTS = 32
TOP_K = 4
D_EXPERT = D_MODEL
SWIGLU_LIMIT = 7.0
SWIGLU_ALPHA = 1.702
ROPE_THETA = 10000.0
EPS = 1e-6
MOE_BLOCK = 512

BRANCH_W = 512
IN_SIZES = (512, 512, 512, 512, 64, 64, 256, 64, 4, 512)
C_SQ, C_SK, C_SV, C_DQ, C_MQ, C_IQ, C_SMALL, C_END = 0, 512, 1024, 1536, 2048, 2560, 2816, 3072
IW_LANE = 64

LANES = 128
NEG_BIG = -1e30
SB_CUTOFF = 110.0
KEY_NEG_INF = int(np.array(-np.inf, np.float32).view(np.int32)) ^ 0x7FFFFFFF
INT_MIN = -(2 ** 31)

VMEM_LIMIT = 56 * 1024 * 1024


def _cparams(sem):
    return pltpu.CompilerParams(dimension_semantics=sem, vmem_limit_bytes=VMEM_LIMIT)


def _dot(a, b):
    return jnp.dot(a, b, preferred_element_type=F32)


def _dot_nt(a, b):
    return lax.dot_general(a, b, (((1,), (1,)), ((), ())), preferred_element_type=F32)


def _split_bf16(x):
    hi = x.astype(BF16)
    lo = (x - hi.astype(F32)).astype(BF16)
    return hi, lo


def _dot_split(x, m_bf16):
    hi, lo = _split_bf16(x)
    return _dot(hi, m_bf16) + _dot(lo, m_bf16)


def _rot_half_unsigned(y):
    w = y.shape[1]
    lane = lax.broadcasted_iota(I32, y.shape, 1)
    return jnp.where((lane & 32) == 0, pltpu.roll(y, w - 32, 1), pltpu.roll(y, 32, 1))


def _inproj_kernel(x_ref, g_ref, w_ref, cos_ref, sin_ref, coss_ref, sins_ref, gq_ref, gks_ref,
                   gm_ref, bd64_ref, bd128_ref,
                   sq_ref, sk_ref, sv_ref, dq_ref, iq_ref, kvi_ref, iw_ref, mq_ref):
    x = x_ref[...]
    ms = jnp.mean(x * x, axis=-1, keepdims=True)
    h = (x * lax.rsqrt(ms + EPS) * g_ref[...]).astype(BF16)

    def seg(a, b):
        return _dot(h, w_ref[:, a:b])

    def put_heads(ref, y):
        for hd in range(y.shape[1] // HEAD_DIM):
            ref[0, hd] = y[:, hd * HEAD_DIM:(hd + 1) * HEAD_DIM].astype(BF16)

    put_heads(sq_ref, seg(C_SQ, C_SK) * (HEAD_DIM ** -0.5))
    put_heads(sk_ref, seg(C_SK, C_SV))
    put_heads(sv_ref, seg(C_SV, C_DQ))

    y = seg(C_DQ, C_MQ)
    msq = _dot_split(y * y, bd64_ref[...]) * (1.0 / HEAD_DIM)
    y = y * lax.rsqrt(msq + EPS) * gq_ref[...]
    y = y * cos_ref[...] + _rot_half_unsigned(y) * sin_ref[...]
    put_heads(dq_ref, y * (HEAD_DIM ** -0.5))

    y = seg(C_IQ, C_SMALL)
    y = y * cos_ref[:, :256] + _rot_half_unsigned(y) * sin_ref[:, :256]
    iq_ref[...] = (y * (HEAD_DIM ** -0.5)).astype(BF16)

    y = seg(C_SMALL, C_END)
    lane = lax.broadcasted_iota(I32, y.shape, 1)
    is_k = lane < HEAD_DIM
    msk = jnp.sum(jnp.where(is_k, y * y, 0.0), axis=-1, keepdims=True) * (1.0 / HEAD_DIM)
    y = y * jnp.where(is_k, lax.rsqrt(msk + EPS) * gks_ref[...], 1.0)
    y = y * coss_ref[...] + _rot_half_unsigned(y) * sins_ref[...]
    kvi_ref[...] = y.astype(BF16)
    iw_ref[...] = y[:, 128:256]

    y = seg(C_MQ, C_IQ)
    msm = _dot_split(y * y, bd128_ref[...]) * (1.0 / MEM_HEAD_DIM)
    mq_ref[...] = (y * lax.rsqrt(msm + EPS) * gm_ref[...]).astype(BF16)


def _rope_tables(seq):
    half = HEAD_DIM // 2
    inv = ROPE_THETA ** (-np.arange(half, dtype=np.float32) / half)
    ang = np.arange(seq, dtype=np.float32)[:, None] * inv[None, :]
    cos = np.cos(ang).astype(np.float32)
    sin = np.sin(ang).astype(np.float32)
    cos64 = np.concatenate([cos, cos], axis=1)
    sin64 = np.concatenate([-sin, sin], axis=1)
    one = np.ones_like(cos64)
    zero = np.zeros_like(cos64)
    cosq = np.tile(cos64, (1, 8))
    sinq = np.tile(sin64, (1, 8))
    coss = np.concatenate([cos64, one, cos64, one], axis=1)
    sins = np.concatenate([sin64, zero, sin64, zero], axis=1)
    return cosq, sinq, coss, sins


def _block_diag_ones(width, group):
    idx = np.arange(width) // group
    return jnp.asarray((idx[:, None] == idx[None, :]).astype(np.float32), dtype=BF16)


def _inproj(x2d, batch, seq, g_mix, w_in, g_q_dsa, g_k_dsa, g_q_mem, tm):
    n = x2d.shape[0]
    sizes = np.cumsum((0,) + IN_SIZES)
    col = {name: (int(sizes[i]), int(sizes[i + 1])) for i, name in enumerate(
        ("sq", "sk", "sv", "dq", "dk", "dv", "iq", "ik", "iw", "mq"))}
    order = ("sq", "sk", "sv", "dq", "mq", "iq", "dk", "dv", "ik", "iw")
    w = jnp.concatenate([w_in[:, col[k][0]:col[k][1]] for k in order]
                        + [jnp.zeros((D_MODEL, C_END - sum(IN_SIZES)), w_in.dtype)], axis=1).astype(BF16)
    cosq, sinq, coss, sins = (jnp.asarray(t) for t in _rope_tables(seq))
    gq = jnp.tile(g_q_dsa, 8)[None, :]
    gks = jnp.concatenate([g_k_dsa, jnp.ones((256 - HEAD_DIM,), F32)])[None, :]
    gm = jnp.tile(g_q_mem, MEM_HEADS)[None, :]
    spb = seq // tm
    row = lambda i: (i, 0)
    const = lambda i: (0, 0)
    pos = lambda i: (i % spb, 0)
    heads = lambda i: (i // spb, 0, i % spb, 0)
    head_shape = jax.ShapeDtypeStruct((batch, 8, seq, HEAD_DIM), BF16)
    head_spec = pl.BlockSpec((1, 8, tm, HEAD_DIM), heads)
    return pl.pallas_call(
        _inproj_kernel,
        grid=(n // tm,),
        in_specs=[
            pl.BlockSpec((tm, D_MODEL), row),
            pl.BlockSpec((1, D_MODEL), const),
            pl.BlockSpec((D_MODEL, C_END), const),
            pl.BlockSpec((tm, 512), pos), pl.BlockSpec((tm, 512), pos),
            pl.BlockSpec((tm, 256), pos), pl.BlockSpec((tm, 256), pos),
            pl.BlockSpec((1, 512), const), pl.BlockSpec((1, 256), const), pl.BlockSpec((1, 512), const),
            pl.BlockSpec((512, 512), const), pl.BlockSpec((512, 512), const),
        ],
        out_specs=[head_spec, head_spec, head_spec, head_spec,
                   pl.BlockSpec((tm, 256), row), pl.BlockSpec((tm, 256), row),
                   pl.BlockSpec((tm, 128), row), pl.BlockSpec((tm, 512), row)],
        out_shape=[head_shape, head_shape, head_shape, head_shape,
                   jax.ShapeDtypeStruct((n, 256), BF16), jax.ShapeDtypeStruct((n, 256), BF16),
                   jax.ShapeDtypeStruct((n, 128), F32), jax.ShapeDtypeStruct((n, 512), BF16)],
        compiler_params=_cparams(("parallel",)),
        name="inproj",
    )(x2d, g_mix[None, :], w, cosq, sinq, coss, sins, gq, gks, gm,
      _block_diag_ones(512, HEAD_DIM), _block_diag_ones(512, MEM_HEAD_DIM))


def _sb_kernel(q_ref, k_ref, v_ref, u_ref, o_ref, acc_ref, car_ref, out_ref, *, tq):
    qi = pl.program_id(1)
    rows = lax.broadcasted_iota(I32, (tq, tq), 0)
    cols = lax.broadcasted_iota(I32, (tq, tq), 1)
    dif = cols - rows
    u = u_ref[...]

    def head(hd, carry):
        q = q_ref[0, hd]
        acc_ref[...] = jnp.zeros_like(acc_ref)
        car_ref[...] = jnp.zeros_like(car_ref)

        def cond(c):
            kb, mx = c
            return jnp.logical_and(kb >= 0, mx > -SB_CUTOFF)

        def body(c):
            kb, _ = c
            ks = pl.multiple_of(kb * tq, tq)
            k = k_ref[0, hd, pl.ds(ks, tq), :]
            v = v_ref[0, hd, pl.ds(ks, tq), :]
            z = _dot_nt(q, k)
            earlier = dif < (qi - kb) * tq
            sp = jnp.maximum(z, 0.0) + jnp.log(1.0 + jnp.exp(-jnp.abs(z)))
            lk = jnp.where(earlier, -sp, 0.0)
            between = _dot_split(lk, u)
            car = car_ref[...]
            w = jnp.where(earlier, jnp.exp((z - sp) + between + car), 0.0)
            acc_ref[...] += _dot(w.astype(BF16), v)
            car_new = car + jnp.sum(lk, axis=-1, keepdims=True)
            car_ref[...] = car_new
            return kb - 1, jnp.max(car_new)

        lax.while_loop(cond, body, (qi, jnp.float32(0.0)))
        out_ref[hd] = acc_ref[...]
        return carry

    lax.fori_loop(0, SB_HEADS, head, 0)
    for hd in range(SB_HEADS):
        o_ref[0, :, hd * HEAD_DIM:(hd + 1) * HEAD_DIM] = out_ref[hd].astype(BF16)


def _strict_tri(n, lower):
    i = np.arange(n)
    m = (i[:, None] > i[None, :]) if lower else (i[:, None] < i[None, :])
    return jnp.asarray(m.astype(np.float32), dtype=BF16)


def _sb_attention(sq, sk, sv, tq):
    batch, _, seq, _ = sq.shape
    qspec = pl.BlockSpec((1, SB_HEADS, tq, HEAD_DIM), lambda b, i: (b, 0, i, 0))
    kspec = pl.BlockSpec((1, SB_HEADS, seq, HEAD_DIM), lambda b, i: (b, 0, 0, 0))
    return pl.pallas_call(
        functools.partial(_sb_kernel, tq=tq),
        grid=(batch, seq // tq),
        in_specs=[qspec, kspec, kspec, pl.BlockSpec((tq, tq), lambda b, i: (0, 0))],
        out_specs=pl.BlockSpec((1, tq, BRANCH_W), lambda b, i: (b, i, 0)),
        out_shape=jax.ShapeDtypeStruct((batch, seq, BRANCH_W), BF16),
        scratch_shapes=[pltpu.VMEM((tq, HEAD_DIM), F32), pltpu.VMEM((tq, 1), F32),
                        pltpu.VMEM((SB_HEADS, tq, HEAD_DIM), F32)],
        compiler_params=_cparams(("parallel", "parallel")),
        name="sb_attention",
    )(sq, sk, sv, _strict_tri(tq, lower=True))


DSA_SEG = 512
DSA_KB = 256


def _dsa_kernel(dq_ref, iq_ref, iw_ref, kvi_ref, u_ref, o_ref,
                key_ref, sel_ref, m_ref, l_ref, acc_ref, *, tq, topk):
    qi = pl.program_id(1)
    qs = qi * tq
    nseg = qs // DSA_SEG + 1
    nchunk = nseg * (DSA_SEG // LANES)
    iq = iq_ref[0]
    wi = iw_ref[0][:, IW_LANE:IW_LANE + IDX_HEADS] * (IDX_HEADS ** -0.5)
    wib = [jnp.broadcast_to(wi[:, h:h + 1], (tq, LANES)) for h in range(IDX_HEADS)]
    row_chunk = (qs + lax.broadcasted_iota(I32, (tq, LANES), 0)) // CHUNK
    col_chunk = lax.broadcasted_iota(I32, (tq, LANES), 1) // CHUNK

    def score_chunk(c, carry):
        cs = pl.multiple_of(c * LANES, LANES)
        ik = kvi_ref[0, pl.ds(cs, LANES), 128:192]
        sc = jnp.zeros((tq, LANES), F32)
        for h in range(IDX_HEADS):
            lg = _dot_nt(iq[:, h * HEAD_DIM:(h + 1) * HEAD_DIM], ik)
            sc = sc + wib[h] * jnp.maximum(lg, 0.0)
        admissible = (c * (LANES // CHUNK) + col_chunk) <= row_chunk
        sc = jnp.where(admissible, sc, -jnp.inf)
        bits = lax.bitcast_convert_type(sc, I32)
        key_ref[:, pl.ds(cs, LANES)] = jnp.where(bits >= 0, bits, bits ^ 0x7FFFFFFF)
        return carry

    lax.fori_loop(0, nchunk, score_chunk, 0)

    def count(pred_fn):
        def seg_body(s, a):
            for j in range(DSA_SEG // LANES):
                cs = pl.multiple_of(s * DSA_SEG + j * LANES, LANES)
                a = a + jnp.where(pred_fn(key_ref[:, pl.ds(cs, LANES)]), 1.0, 0.0)
            return a
        a = lax.fori_loop(0, nseg, seg_body, jnp.zeros((tq, LANES), F32))
        return jnp.sum(a, axis=-1, keepdims=True)

    kf = jnp.float32(topk)
    zero_b = jnp.zeros((tq, LANES), I32)
    t0 = jnp.where(count(lambda kk: kk >= zero_b) >= kf, 0, INT_MIN).astype(I32)

    def bit_step(i, t):
        cand = t + lax.shift_left(jnp.int32(1), 30 - i)
        cand_b = jnp.broadcast_to(cand, (tq, LANES))
        return jnp.where(count(lambda kk: kk >= cand_b) >= kf, cand, t)

    thr = lax.fori_loop(0, 31, bit_step, t0)
    thr_b = jnp.broadcast_to(thr, (tq, LANES))
    need = kf - count(lambda kk: kk > thr_b)
    need_b = jnp.broadcast_to(need, (tq, LANES))
    u = u_ref[...]

    def select_chunk(c, prefix):
        cs = pl.multiple_of(c * LANES, LANES)
        kk = key_ref[:, pl.ds(cs, LANES)]
        eqf = jnp.where(kk == thr_b, 1.0, 0.0)
        rank = _dot(eqf.astype(BF16), u) + prefix
        tie = jnp.where(rank < need_b, eqf, 0.0)
        sel = jnp.where(kk > thr_b, 1.0, tie)
        sel_ref[:, pl.ds(cs, LANES)] = jnp.where(kk > KEY_NEG_INF, sel, 0.0)
        return prefix + jnp.sum(eqf, axis=-1, keepdims=True)

    lax.fori_loop(0, nchunk, select_chunk, jnp.zeros((tq, 1), F32))

    q8 = dq_ref[0].reshape(DSA_HEADS * tq, HEAD_DIM)
    m_ref[...] = jnp.full_like(m_ref, NEG_BIG)
    l_ref[...] = jnp.zeros_like(l_ref)
    acc_ref[...] = jnp.zeros_like(acc_ref)

    def attn_block(c, carry):
        cs = pl.multiple_of(c * DSA_KB, DSA_KB)
        k = kvi_ref[0, pl.ds(cs, DSA_KB), 0:HEAD_DIM]
        v = kvi_ref[0, pl.ds(cs, DSA_KB), HEAD_DIM:2 * HEAD_DIM]
        s = _dot_nt(q8, k).reshape(DSA_HEADS, tq, DSA_KB)
        sel = (sel_ref[:, pl.ds(cs, DSA_KB)] > 0.5)[None]
        m_old = m_ref[...]
        m_new = jnp.maximum(m_old, jnp.max(jnp.where(sel, s, NEG_BIG), axis=-1, keepdims=True))
        p = jnp.where(sel, jnp.exp(s - m_new), 0.0)
        alpha = jnp.exp(m_old - m_new)
        l_ref[...] = alpha * l_ref[...] + jnp.sum(p, axis=-1, keepdims=True)
        pv = _dot(p.reshape(DSA_HEADS * tq, DSA_KB).astype(BF16), v)
        acc_ref[...] = alpha * acc_ref[...] + pv.reshape(DSA_HEADS, tq, HEAD_DIM)
        m_ref[...] = m_new
        return carry

    lax.fori_loop(0, nseg * (DSA_SEG // DSA_KB), attn_block, 0)
    o = acc_ref[...] / l_ref[...]
    for hd in range(DSA_HEADS):
        o_ref[0, :, hd * HEAD_DIM:(hd + 1) * HEAD_DIM] = o[hd].astype(BF16)


def _dsa_attention(dq, iq, iw, kvi, tq, topk):
    batch, _, seq, _ = dq.shape
    return pl.pallas_call(
        functools.partial(_dsa_kernel, tq=tq, topk=topk),
        grid=(batch, seq // tq),
        in_specs=[
            pl.BlockSpec((1, DSA_HEADS, tq, HEAD_DIM), lambda b, i: (b, 0, i, 0)),
            pl.BlockSpec((1, tq, 256), lambda b, i: (b, i, 0)),
            pl.BlockSpec((1, tq, 128), lambda b, i: (b, i, 0)),
            pl.BlockSpec((1, seq, 256), lambda b, i: (b, 0, 0)),
            pl.BlockSpec((LANES, LANES), lambda b, i: (0, 0)),
        ],
        out_specs=pl.BlockSpec((1, tq, BRANCH_W), lambda b, i: (b, i, 0)),
        out_shape=jax.ShapeDtypeStruct((batch, seq, BRANCH_W), BF16),
        scratch_shapes=[pltpu.VMEM((tq, seq), I32), pltpu.VMEM((tq, seq), F32),
                        pltpu.VMEM((DSA_HEADS, tq, 1), F32), pltpu.VMEM((DSA_HEADS, tq, 1), F32),
                        pltpu.VMEM((DSA_HEADS, tq, HEAD_DIM), F32)],
        compiler_params=_cparams(("parallel", "parallel")),
        name="dsa_attention",
    )(dq, iq.reshape(batch, seq, 256), iw.reshape(batch, seq, 128), kvi.reshape(batch, seq, 256),
      _strict_tri(LANES, lower=False))


def _memkv_kernel(m_ref, g_ref, w_ref, gk_ref, bd_ref, mk_ref, mv_ref):
    x = m_ref[...]
    ms = jnp.mean(x * x, axis=-1, keepdims=True)
    h = (x * lax.rsqrt(ms + EPS) * g_ref[...]).astype(BF16)
    mw = MEM_HEADS * MEM_HEAD_DIM
    k = _dot(h, w_ref[:, :mw])
    msk = _dot_split(k * k, bd_ref[...]) * (1.0 / MEM_HEAD_DIM)
    mk_ref[...] = (k * lax.rsqrt(msk + EPS) * gk_ref[...]).astype(BF16)
    mv_ref[...] = _dot(h, w_ref[:, mw:]).astype(BF16)


def _mem_kv(mem2d, g_mem, w_mem_kv, g_k_mem, tm):
    n = mem2d.shape[0]
    mw = MEM_HEADS * MEM_HEAD_DIM
    row = lambda i: (i, 0)
    const = lambda i: (0, 0)
    return pl.pallas_call(
        _memkv_kernel,
        grid=(n // tm,),
        in_specs=[pl.BlockSpec((tm, D_MODEL), row), pl.BlockSpec((1, D_MODEL), const),
                  pl.BlockSpec((D_MODEL, 2 * mw), const), pl.BlockSpec((1, mw), const),
                  pl.BlockSpec((mw, mw), const)],
        out_specs=[pl.BlockSpec((tm, mw), row), pl.BlockSpec((tm, mw), row)],
        out_shape=[jax.ShapeDtypeStruct((n, mw), BF16), jax.ShapeDtypeStruct((n, mw), BF16)],
        compiler_params=_cparams(("parallel",)),
        name="mem_kv",
    )(mem2d, g_mem[None, :], w_mem_kv.astype(BF16), jnp.tile(g_k_mem, MEM_HEADS)[None, :],
      _block_diag_ones(mw, MEM_HEAD_DIM))


def _memattn_kernel(q_ref, k_ref, v_ref, o_ref):
    for hd in range(MEM_HEADS):
        sl = slice(hd * MEM_HEAD_DIM, (hd + 1) * MEM_HEAD_DIM)
        s = _dot_nt(q_ref[0, :, sl], k_ref[0, :, sl]) * (MEM_HEAD_DIM ** -0.5)
        p = jnp.exp(s - jnp.max(s, axis=-1, keepdims=True))
        o = _dot(p.astype(BF16), v_ref[0, :, sl]) / jnp.sum(p, axis=-1, keepdims=True)
        o_ref[0, :, sl] = o.astype(BF16)


def _mem_attention(mq, mk, mv, tq):
    batch, seq, mw = mq.shape
    mlen = mk.shape[1]
    kspec = pl.BlockSpec((1, mlen, mw), lambda b, i: (b, 0, 0))
    return pl.pallas_call(
        _memattn_kernel,
        grid=(batch, seq // tq),
        in_specs=[pl.BlockSpec((1, tq, mw), lambda b, i: (b, i, 0)), kspec, kspec],
        out_specs=pl.BlockSpec((1, tq, mw), lambda b, i: (b, i, 0)),
        out_shape=jax.ShapeDtypeStruct((batch, seq, mw), BF16),
        compiler_params=_cparams(("parallel", "parallel")),
        name="mem_attention",
    )(mq, mk, mv)


def _merge_kernel(x_ref, g_ref, osb_ref, odsa_ref, omem_ref, wg_ref, bg_ref, wb_ref, wo_ref,
                  gf_ref, wrh_ref, wrl_ref, br_ref,
                  x2_ref, h2_ref, idx_ref, gate_ref):
    x = x_ref[...]
    ms = jnp.mean(x * x, axis=-1, keepdims=True)
    h = (x * lax.rsqrt(ms + EPS) * g_ref[...]).astype(BF16)
    merged = None
    for n, o_ref in enumerate((osb_ref, odsa_ref, omem_ref)):
        gate = jax.nn.sigmoid(_dot(h, wg_ref[n]) + bg_ref[n])
        term = gate * _dot(o_ref[...], wb_ref[n])
        merged = term if merged is None else merged + term
    x2 = x + _dot(merged.astype(BF16), wo_ref[...])
    x2_ref[...] = x2

    ms2 = jnp.mean(x2 * x2, axis=-1, keepdims=True)
    h2 = x2 * lax.rsqrt(ms2 + EPS) * gf_ref[...]
    h2_ref[...] = h2
    hi, lo = _split_bf16(h2)
    logits = _dot(hi, wrh_ref[...]) + _dot(hi, wrl_ref[...]) + _dot(lo, wrh_ref[...]) + br_ref[...]

    lane = lax.broadcasted_iota(I32, logits.shape, 1)
    vals = logits
    top_v, top_i = [], []
    for _ in range(TOP_K):
        mx = jnp.max(vals, axis=-1, keepdims=True)
        ix = jnp.min(jnp.where(vals == mx, lane, N_EXPERTS), axis=-1, keepdims=True)
        top_v.append(mx)
        top_i.append(ix)
        vals = jnp.where(lane == ix, -jnp.inf, vals)
    ex = [jnp.exp(v - top_v[0]) for v in top_v]
    den = ex[0] + ex[1] + ex[2] + ex[3]
    for k in range(TOP_K):
        idx_ref[:, k:k + 1] = top_i[k]
        gate_ref[:, k:k + 1] = ex[k] / den


def _merge_route(x2d, g_mix, o_sb, o_dsa, o_mem, w_gate, b_gate, w_branch, w_out, g_ffn, w_router, b_router, tm):
    n = x2d.shape[0]
    row = lambda i: (i, 0)
    c2 = lambda i: (0, 0)
    c3 = lambda i: (0, 0, 0)
    wr_hi = w_router.astype(BF16)
    wr_lo = (w_router - wr_hi.astype(F32)).astype(BF16)
    return pl.pallas_call(
        _merge_kernel,
        grid=(n // tm,),
        in_specs=[pl.BlockSpec((tm, D_MODEL), row), pl.BlockSpec((1, D_MODEL), c2),
                  pl.BlockSpec((tm, BRANCH_W), row), pl.BlockSpec((tm, BRANCH_W), row),
                  pl.BlockSpec((tm, BRANCH_W), row),
                  pl.BlockSpec((3, D_MODEL, D_MODEL), c3), pl.BlockSpec((3, 1, D_MODEL), c3),
                  pl.BlockSpec((3, BRANCH_W, D_MODEL), c3), pl.BlockSpec((D_MODEL, D_MODEL), c2),
                  pl.BlockSpec((1, D_MODEL), c2), pl.BlockSpec((D_MODEL, N_EXPERTS), c2),
                  pl.BlockSpec((D_MODEL, N_EXPERTS), c2), pl.BlockSpec((1, N_EXPERTS), c2)],
        out_specs=[pl.BlockSpec((tm, D_MODEL), row), pl.BlockSpec((tm, D_MODEL), row),
                   pl.BlockSpec((tm, TOP_K), row), pl.BlockSpec((tm, TOP_K), row)],
        out_shape=[jax.ShapeDtypeStruct((n, D_MODEL), F32), jax.ShapeDtypeStruct((n, D_MODEL), F32),
                   jax.ShapeDtypeStruct((n, TOP_K), I32), jax.ShapeDtypeStruct((n, TOP_K), F32)],
        compiler_params=_cparams(("parallel",)),
        name="merge_route",
    )(x2d, g_mix[None, :], o_sb, o_dsa, o_mem, w_gate.astype(BF16), b_gate[:, None, :],
      w_branch.astype(BF16), w_out.astype(BF16), g_ffn[None, :], wr_hi, wr_lo, b_router[None, :])


def _rank_kernel(idx_ref, l_ref, rank_ref, cnt_ref, carry_ref):
    @pl.when(pl.program_id(0) == 0)
    def _():
        carry_ref[...] = jnp.zeros_like(carry_ref)

    idx = idx_ref[...]
    tm = idx.shape[0]
    lane = lax.broadcasted_iota(I32, (tm, LANES), 1)
    hits = [lane == idx[:, k:k + 1] for k in range(TOP_K)]
    onehot = sum(jnp.where(hk, 1.0, 0.0) for hk in hits)
    before = _dot(l_ref[...], onehot.astype(BF16)) + carry_ref[...]
    for k in range(TOP_K):
        rank_ref[:, k:k + 1] = jnp.sum(jnp.where(hits[k], before, 0.0), axis=-1, keepdims=True).astype(I32)
    carry_ref[...] += jnp.sum(onehot, axis=0, keepdims=True)
    cnt_ref[...] = carry_ref[...]


def _expert_rank(top_idx, tm):
    n = top_idx.shape[0]
    return pl.pallas_call(
        _rank_kernel,
        grid=(n // tm,),
        in_specs=[pl.BlockSpec((tm, TOP_K), lambda i: (i, 0)), pl.BlockSpec((tm, tm), lambda i: (0, 0))],
        out_specs=[pl.BlockSpec((tm, TOP_K), lambda i: (i, 0)), pl.BlockSpec((1, LANES), lambda i: (0, 0))],
        out_shape=[jax.ShapeDtypeStruct((n, TOP_K), I32), jax.ShapeDtypeStruct((1, LANES), F32)],
        scratch_shapes=[pltpu.VMEM((1, LANES), F32)],
        compiler_params=_cparams(("arbitrary",)),
        name="expert_rank",
    )(top_idx, _strict_tri(tm, lower=True))


DISPATCH_TOKENS = 256


def _dispatch_kernel(dest_ref, h_ref, xs_in_ref, xs_ref, sem):
    del xs_in_ref
    n_copies = DISPATCH_TOKENS * TOP_K

    def copy(a):
        return pltpu.make_async_copy(h_ref.at[pl.ds(a // TOP_K, 1)],
                                     xs_ref.at[pl.ds(dest_ref[0, 0, a], 1)], sem)

    def start(a, c):
        copy(a).start()
        return c

    def wait(a, c):
        copy(a).wait()
        return c

    lax.fori_loop(0, n_copies, start, 0)
    lax.fori_loop(0, n_copies, wait, 0)


def _dispatch(dest, h2, n_slots):
    n = h2.shape[0]
    per = DISPATCH_TOKENS * TOP_K
    return pl.pallas_call(
        _dispatch_kernel,
        grid=(n // DISPATCH_TOKENS,),
        in_specs=[pl.BlockSpec((1, 1, per), lambda i: (i, 0, 0), memory_space=pltpu.SMEM),
                  pl.BlockSpec((DISPATCH_TOKENS, D_MODEL), lambda i: (i, 0)),
                  pl.BlockSpec(memory_space=pl.ANY)],
        out_specs=pl.BlockSpec(memory_space=pl.ANY),
        out_shape=jax.ShapeDtypeStruct((n_slots, D_MODEL), F32),
        scratch_shapes=[pltpu.SemaphoreType.DMA(())],
        input_output_aliases={2: 0},
        compiler_params=_cparams(("arbitrary",)),
        name="moe_dispatch",
    )(dest.reshape(n // DISPATCH_TOKENS, 1, per), h2, jnp.zeros((n_slots, D_MODEL), F32))


def _expert_kernel(blk_e_ref, nused_ref, x_ref, w1_ref, b1_ref, w2_ref, b2_ref, y_ref):
    del blk_e_ref

    @pl.when(pl.program_id(0) < nused_ref[0])
    def _():
        hb = _dot(x_ref[...].astype(BF16), w1_ref[0]) + b1_ref[0]
        g = jnp.minimum(hb[:, :D_EXPERT], SWIGLU_LIMIT)
        u = jnp.clip(hb[:, D_EXPERT:], -SWIGLU_LIMIT, SWIGLU_LIMIT)
        act = (u + 1.0) * (g * jax.nn.sigmoid(SWIGLU_ALPHA * g))
        y_ref[...] = _dot(act.astype(BF16), w2_ref[0]) + b2_ref[0]

    @pl.when(pl.program_id(0) >= nused_ref[0])
    def _():
        y_ref[...] = jnp.zeros_like(y_ref)


def _experts(blk_e, n_used, xs, w_e_in, b_e_in, w_e_out, b_e_out):
    n_slots = xs.shape[0]
    nblk = n_slots // MOE_BLOCK
    grid_spec = pltpu.PrefetchScalarGridSpec(
        num_scalar_prefetch=2,
        grid=(nblk,),
        in_specs=[pl.BlockSpec((MOE_BLOCK, D_MODEL), lambda i, be, nu: (i, 0)),
                  pl.BlockSpec((1, D_MODEL, 2 * D_EXPERT), lambda i, be, nu: (be[i], 0, 0)),
                  pl.BlockSpec((1, 1, 2 * D_EXPERT), lambda i, be, nu: (be[i], 0, 0)),
                  pl.BlockSpec((1, D_EXPERT, D_MODEL), lambda i, be, nu: (be[i], 0, 0)),
                  pl.BlockSpec((1, 1, D_MODEL), lambda i, be, nu: (be[i], 0, 0))],
        out_specs=pl.BlockSpec((MOE_BLOCK, D_MODEL), lambda i, be, nu: (i, 0)),
    )
    return pl.pallas_call(
        _expert_kernel,
        grid_spec=grid_spec,
        out_shape=jax.ShapeDtypeStruct((n_slots, D_MODEL), F32),
        compiler_params=_cparams(("arbitrary",)),
        name="moe_experts",
    )(blk_e, n_used, xs, w_e_in.astype(BF16), b_e_in[:, None, :], w_e_out.astype(BF16), b_e_out[:, None, :])


def _combine_kernel(dest_ref, x_ref, gate_ref, y_ref, o_ref, buf_ref, sem):
    n_copies = DISPATCH_TOKENS * TOP_K

    def copy(a):
        return pltpu.make_async_copy(y_ref.at[pl.ds(dest_ref[0, 0, a], 1)],
                                     buf_ref.at[a % TOP_K, pl.ds(a // TOP_K, 1)], sem)

    def start(a, c):
        copy(a).start()
        return c

    def wait(a, c):
        copy(a).wait()
        return c

    lax.fori_loop(0, n_copies, start, 0)
    lax.fori_loop(0, n_copies, wait, 0)
    gate = gate_ref[...]
    out = x_ref[...]
    for k in range(TOP_K):
        out = out + gate[:, k:k + 1] * buf_ref[k]
    o_ref[...] = out


def _combine(dest, x2, gate, ys):
    n = x2.shape[0]
    per = DISPATCH_TOKENS * TOP_K
    return pl.pallas_call(
        _combine_kernel,
        grid=(n // DISPATCH_TOKENS,),
        in_specs=[pl.BlockSpec((1, 1, per), lambda i: (i, 0, 0), memory_space=pltpu.SMEM),
                  pl.BlockSpec((DISPATCH_TOKENS, D_MODEL), lambda i: (i, 0)),
                  pl.BlockSpec((DISPATCH_TOKENS, TOP_K), lambda i: (i, 0)),
                  pl.BlockSpec(memory_space=pl.ANY)],
        out_specs=pl.BlockSpec((DISPATCH_TOKENS, D_MODEL), lambda i: (i, 0)),
        out_shape=jax.ShapeDtypeStruct((n, D_MODEL), F32),
        scratch_shapes=[pltpu.VMEM((TOP_K, DISPATCH_TOKENS, D_MODEL), F32), pltpu.SemaphoreType.DMA(())],
        compiler_params=_cparams(("arbitrary",)),
        name="moe_combine",
    )(dest.reshape(n // DISPATCH_TOKENS, 1, per), x2, gate, ys)


def _moe(x2, h2, top_idx, gate, w_e_in, b_e_in, w_e_out, b_e_out):
    n = x2.shape[0]
    rank, counts = _expert_rank(top_idx, 512)
    counts = counts[0, :N_EXPERTS].astype(I32)
    padded = (counts + MOE_BLOCK - 1) // MOE_BLOCK * MOE_BLOCK
    pend = jnp.cumsum(padded)
    pstart = pend - padded
    nblk = -(-(n * TOP_K) // MOE_BLOCK) + N_EXPERTS
    blk_e = jnp.minimum(jnp.searchsorted(pend, jnp.arange(nblk, dtype=I32) * MOE_BLOCK, side="right"),
                        N_EXPERTS - 1).astype(I32)
    n_used = (pend[-1:] // MOE_BLOCK).astype(I32)
    onehot = top_idx[:, :, None] == jnp.arange(N_EXPERTS, dtype=I32)[None, None, :]
    dest = rank + jnp.sum(jnp.where(onehot, pstart[None, None, :], 0), axis=-1)
    xs = _dispatch(dest, h2, nblk * MOE_BLOCK)
    ys = _experts(blk_e, n_used, xs, w_e_in, b_e_in, w_e_out, b_e_out)
    return _combine(dest, x2, gate, ys)


def _layer(x, mem, g_mix, w_in, g_q_dsa, g_k_dsa, g_q_mem, g_k_mem, g_mem, w_mem_kv, w_gate, b_gate,
           w_branch, w_out, g_ffn, w_router, b_router, w_e_in, b_e_in, w_e_out, b_e_out):
    batch, seq, _ = x.shape
    n = batch * seq
    topk = min(DSA_TOPK_MAX, seq // 4)
    x2d = x.reshape(n, D_MODEL)
    tm = min(512, seq)
    sq, sk, sv, dq, iq, kvi, iw, mq = _inproj(x2d, batch, seq, g_mix, w_in, g_q_dsa, g_k_dsa, g_q_mem, tm)
    o_sb = _sb_attention(sq, sk, sv, min(256, seq))
    o_dsa = _dsa_attention(dq, iq, iw, kvi, 128, topk)
    mlen = mem.shape[1]
    mk, mv = _mem_kv(mem.reshape(batch * mlen, D_MODEL), g_mem, w_mem_kv, g_k_mem, min(512, batch * mlen))
    mw = MEM_HEADS * MEM_HEAD_DIM
    o_mem = _mem_attention(mq.reshape(batch, seq, mw), mk.reshape(batch, mlen, mw), mv.reshape(batch, mlen, mw), tm)
    x2, h2, top_idx, gate = _merge_route(
        x2d, g_mix, o_sb.reshape(n, BRANCH_W), o_dsa.reshape(n, BRANCH_W), o_mem.reshape(n, mw),
        w_gate, b_gate, w_branch, w_out, g_ffn, w_router, b_router, tm)
    out = _moe(x2, h2, top_idx, gate, w_e_in, b_e_in, w_e_out, b_e_out)
    return out.reshape(batch, seq, D_MODEL)


def kernel(x, mem, g_mix, w_in, g_q_dsa, g_k_dsa, g_q_mem, g_k_mem, g_mem, w_mem_kv, w_gate, b_gate, w_branch, w_out, g_ffn, w_router, b_router, w_e_in, b_e_in, w_e_out, b_e_out):
    for l in range(g_mix.shape[0]):
        x = _layer(x, mem, g_mix[l], w_in[l], g_q_dsa[l], g_k_dsa[l], g_q_mem[l], g_k_mem[l], g_mem[l],
                   w_mem_kv[l], w_gate[l], b_gate[l], w_branch[l], w_out[l], g_ffn[l], w_router[l],
                   b_router[l], w_e_in[l], b_e_in[l], w_e_out[l], b_e_out[l])
    return x
```

```python
import functools

import numpy as np
import jax
import jax.numpy as jnp
from jax import lax
from jax.experimental import pallas as pl
from jax.experimental.pallas import tpu as pltpu

F32 = jnp.float32
BF16 = jnp.bfloat16
I32 = jnp.int32

D_MODEL = 1024
CHUNK = 64
SB_HEADS = 8
DSA_HEADS = 8
HEAD_DIM = 64
IDX_HEADS = 4
DSA_TOPK_MAX = 256
MEM_HEADS = 4
MEM_HEAD_DIM = 128
N_EXPERTS = 32
TOP_K = 4
D_EXPERT = D_MODEL
SWIGLU_LIMIT = 7.0
SWIGLU_ALPHA = 1.702
ROPE_THETA = 10000.0
EPS = 1e-6
MOE_BLOCK = 512

BRANCH_W = 512
IN_SIZES = (512, 512, 512, 512, 64, 64, 256, 64, 4, 512)
C_SQ, C_SK, C_SV, C_DQ, C_MQ, C_IQ, C_SMALL, C_END = 0, 512, 1024, 1536, 2048, 2560, 2816, 3072
IW_LANE = 64

LANES = 128
NEG_BIG = -1e30
SB_CUTOFF = 110.0
KEY_NEG_INF = int(np.array(-np.inf, np.float32).view(np.int32)) ^ 0x7FFFFFFF
INT_MIN = -(2 ** 31)

VMEM_LIMIT = 56 * 1024 * 1024


def _cparams(sem):
    return pltpu.CompilerParams(dimension_semantics=sem, vmem_limit_bytes=VMEM_LIMIT)


def _dot(a, b):
    return jnp.dot(a, b, preferred_element_type=F32)


def _dot_nt(a, b):
    return lax.dot_general(a, b, (((1,), (1,)), ((), ())), preferred_element_type=F32)


def _split_bf16(x):
    hi = x.astype(BF16)
    lo = (x - hi.astype(F32)).astype(BF16)
    return hi, lo


def _dot_split(x, m_bf16):
    hi, lo = _split_bf16(x)
    return _dot(hi, m_bf16) + _dot(lo, m_bf16)


def _rot_half_unsigned(y):
    w = y.shape[1]
    lane = lax.broadcasted_iota(I32, y.shape, 1)
    return jnp.where((lane & 32) == 0, pltpu.roll(y, w - 32, 1), pltpu.roll(y, 32, 1))


def _inproj_kernel(x_ref, g_ref, w_ref, cos_ref, sin_ref, coss_ref, sins_ref, gq_ref, gks_ref,
                   gm_ref, bd64_ref, bd128_ref,
                   sq_ref, sk_ref, sv_ref, dq_ref, iq_ref, kvi_ref, iw_ref, mq_ref, vt_ref):
    x = x_ref[...]
    ms = jnp.mean(x * x, axis=-1, keepdims=True)
    h = (x * lax.rsqrt(ms + EPS) * g_ref[...]).astype(BF16)

    def seg(a, b):
        return _dot(h, w_ref[:, a:b])

    def put_heads(ref, y):
        for hd in range(y.shape[1] // HEAD_DIM):
            ref[0, hd] = y[:, hd * HEAD_DIM:(hd + 1) * HEAD_DIM].astype(BF16)

    put_heads(sq_ref, seg(C_SQ, C_SK) * (HEAD_DIM ** -0.5))
    put_heads(sk_ref, seg(C_SK, C_SV))
    put_heads(sv_ref, seg(C_SV, C_DQ))

    y = seg(C_DQ, C_MQ)
    msq = _dot_split(y * y, bd64_ref[...]) * (1.0 / HEAD_DIM)
    y = y * lax.rsqrt(msq + EPS) * gq_ref[...]
    y = y * cos_ref[...] + _rot_half_unsigned(y) * sin_ref[...]
    put_heads(dq_ref, y * (HEAD_DIM ** -0.5))

    y = seg(C_IQ, C_SMALL)
    y = y * cos_ref[:, :256] + _rot_half_unsigned(y) * sin_ref[:, :256]
    iq_ref[...] = (y * (HEAD_DIM ** -0.5)).astype(BF16)

    y = seg(C_SMALL, C_END)
    lane = lax.broadcasted_iota(I32, y.shape, 1)
    is_k = lane < HEAD_DIM
    msk = jnp.sum(jnp.where(is_k, y * y, 0.0), axis=-1, keepdims=True) * (1.0 / HEAD_DIM)
    y = y * jnp.where(is_k, lax.rsqrt(msk + EPS) * gks_ref[...], 1.0)
    y = y * coss_ref[...] + _rot_half_unsigned(y) * sins_ref[...]
    kvi_ref[...] = y.astype(BF16)
    iw_ref[...] = y[:, 128:256]
    kv_t = y[:, 0:128].T
    row_t = lax.broadcasted_iota(I32, kv_t.shape, 0)
    vt_ref[0] = jnp.where(row_t < HEAD_DIM, 1.0, kv_t).astype(BF16)

    y = seg(C_MQ, C_IQ)
    msm = _dot_split(y * y, bd128_ref[...]) * (1.0 / MEM_HEAD_DIM)
    mq_ref[...] = (y * lax.rsqrt(msm + EPS) * gm_ref[...]).astype(BF16)


def _rope_tables(seq):
    half = HEAD_DIM // 2
    inv = ROPE_THETA ** (-jnp.arange(half, dtype=F32) / half)
    ang = jnp.arange(seq).astype(F32)[:, None] * inv[None, :]
    cos = jnp.cos(ang)
    sin = jnp.sin(ang)
    cos64 = jnp.concatenate([cos, cos], axis=1)
    sin64 = jnp.concatenate([-sin, sin], axis=1)
    one = jnp.ones_like(cos64)
    zero = jnp.zeros_like(cos64)
    cosq = jnp.tile(cos64, (1, 8))
    sinq = jnp.tile(sin64, (1, 8))
    coss = jnp.concatenate([cos64, one, cos64, one], axis=1)
    sins = jnp.concatenate([sin64, zero, sin64, zero], axis=1)
    return cosq, sinq, coss, sins


def _block_diag_ones(width, group):
    idx = np.arange(width) // group
    return jnp.asarray((idx[:, None] == idx[None, :]).astype(np.float32), dtype=BF16)


def _inproj(x2d, batch, seq, g_mix, w_in, g_q_dsa, g_k_dsa, g_q_mem, tm):
    n = x2d.shape[0]
    sizes = np.cumsum((0,) + IN_SIZES)
    col = {name: (int(sizes[i]), int(sizes[i + 1])) for i, name in enumerate(
        ("sq", "sk", "sv", "dq", "dk", "dv", "iq", "ik", "iw", "mq"))}
    order = ("sq", "sk", "sv", "dq", "mq", "iq", "dk", "dv", "ik", "iw")
    w = jnp.concatenate([w_in[:, col[k][0]:col[k][1]] for k in order]
                        + [jnp.zeros((D_MODEL, C_END - sum(IN_SIZES)), w_in.dtype)], axis=1).astype(BF16)
    cosq, sinq, coss, sins = (jnp.asarray(t) for t in _rope_tables(seq))
    gq = jnp.tile(g_q_dsa, 8)[None, :]
    gks = jnp.concatenate([g_k_dsa, jnp.ones((256 - HEAD_DIM,), F32)])[None, :]
    gm = jnp.tile(g_q_mem, MEM_HEADS)[None, :]
    spb = seq // tm
    row = lambda i: (i, 0)
    const = lambda i: (0, 0)
    pos = lambda i: (i % spb, 0)
    heads = lambda i: (i // spb, 0, i % spb, 0)
    head_shape = jax.ShapeDtypeStruct((batch, 8, seq, HEAD_DIM), BF16)
    head_spec = pl.BlockSpec((1, 8, tm, HEAD_DIM), heads)
    return pl.pallas_call(
        _inproj_kernel,
        grid=(n // tm,),
        in_specs=[
            pl.BlockSpec((tm, D_MODEL), row),
            pl.BlockSpec((1, D_MODEL), const),
            pl.BlockSpec((D_MODEL, C_END), const),
            pl.BlockSpec((tm, 512), pos), pl.BlockSpec((tm, 512), pos),
            pl.BlockSpec((tm, 256), pos), pl.BlockSpec((tm, 256), pos),
            pl.BlockSpec((1, 512), const), pl.BlockSpec((1, 256), const), pl.BlockSpec((1, 512), const),
            pl.BlockSpec((512, 512), const), pl.BlockSpec((512, 512), const),
        ],
        out_specs=[head_spec, head_spec, head_spec, head_spec,
                   pl.BlockSpec((tm, 256), row), pl.BlockSpec((tm, 256), row),
                   pl.BlockSpec((tm, 128), row), pl.BlockSpec((tm, 512), row),
                   pl.BlockSpec((1, 128, tm), lambda i: (i // spb, 0, i % spb))],
        out_shape=[head_shape, head_shape, head_shape, head_shape,
                   jax.ShapeDtypeStruct((n, 256), BF16), jax.ShapeDtypeStruct((n, 256), BF16),
                   jax.ShapeDtypeStruct((n, 128), F32), jax.ShapeDtypeStruct((n, 512), BF16),
                   jax.ShapeDtypeStruct((batch, 128, seq), BF16)],
        compiler_params=_cparams(("parallel",)),
        name="inproj",
    )(x2d, g_mix[None, :], w, cosq, sinq, coss, sins, gq, gks, gm,
      _block_diag_ones(512, HEAD_DIM), _block_diag_ones(512, MEM_HEAD_DIM))


def _sb_kernel(q_ref, k_ref, v_ref, u_ref, o_ref, acc_ref, car_ref, out_ref, *, tq):
    qi = pl.program_id(1)
    rows = lax.broadcasted_iota(I32, (tq, tq), 0)
    cols = lax.broadcasted_iota(I32, (tq, tq), 1)
    dif = cols - rows
    u = u_ref[...]

    def head(hd, carry):
        q = q_ref[0, hd]
        acc_ref[...] = jnp.zeros_like(acc_ref)
        car_ref[...] = jnp.zeros_like(car_ref)

        def cond(c):
            kb, mx = c
            return jnp.logical_and(kb >= 0, mx > -SB_CUTOFF)

        def body(c):
            kb, _ = c
            ks = pl.multiple_of(kb * tq, tq)
            k = k_ref[0, hd, pl.ds(ks, tq), :]
            v = v_ref[0, hd, pl.ds(ks, tq), :]
            z = _dot_nt(q, k)
            earlier = dif < (qi - kb) * tq
            sp = jnp.maximum(z, 0.0) + jnp.log(1.0 + jnp.exp(-jnp.abs(z)))
            lk = jnp.where(earlier, -sp, 0.0)
            between = _dot_split(lk, u)
            car = car_ref[...]
            w = jnp.where(earlier, jnp.exp((z - sp) + between + car), 0.0)
            acc_ref[...] += _dot(w.astype(BF16), v)
            car_new = car + jnp.sum(lk, axis=-1, keepdims=True)
            car_ref[...] = car_new
            return kb - 1, jnp.max(car_new)

        lax.while_loop(cond, body, (qi, jnp.float32(0.0)))
        out_ref[hd] = acc_ref[...]
        return carry

    lax.fori_loop(0, SB_HEADS, head, 0)
    for hd in range(SB_HEADS):
        o_ref[0, :, hd * HEAD_DIM:(hd + 1) * HEAD_DIM] = out_ref[hd].astype(BF16)


def _strict_tri(n, lower):
    i = np.arange(n)
    m = (i[:, None] > i[None, :]) if lower else (i[:, None] < i[None, :])
    return jnp.asarray(m.astype(np.float32), dtype=BF16)


def _sb_attention(sq, sk, sv, tq):
    batch, _, seq, _ = sq.shape
    qspec = pl.BlockSpec((1, SB_HEADS, tq, HEAD_DIM), lambda b, i: (b, 0, i, 0))
    kspec = pl.BlockSpec((1, SB_HEADS, seq, HEAD_DIM), lambda b, i: (b, 0, 0, 0))
    return pl.pallas_call(
        functools.partial(_sb_kernel, tq=tq),
        grid=(batch, seq // tq),
        in_specs=[qspec, kspec, kspec, pl.BlockSpec((tq, tq), lambda b, i: (0, 0))],
        out_specs=pl.BlockSpec((1, tq, BRANCH_W), lambda b, i: (b, i, 0)),
        out_shape=jax.ShapeDtypeStruct((batch, seq, BRANCH_W), BF16),
        scratch_shapes=[pltpu.VMEM((tq, HEAD_DIM), F32), pltpu.VMEM((tq, 1), F32),
                        pltpu.VMEM((SB_HEADS, tq, HEAD_DIM), F32)],
        compiler_params=_cparams(("parallel", "parallel")),
        name="sb_attention",
    )(sq, sk, sv, _strict_tri(tq, lower=True))


DSA_SEG = 512
DSA_KB = 256


def _dsa_kernel(dq_ref, iq_ref, iw_ref, kvi_ref, vt_ref, tri_ref, o_ref,
                sc_ref, bias_ref, m_ref, acc_ref, *, tq, topk):
    qi = pl.program_id(1)
    qs = qi * tq
    nseg = qs // DSA_SEG + 1
    nblk = nseg * (DSA_SEG // DSA_KB)
    iq = iq_ref[0]
    w_t = iw_ref[0].T
    w_row = [w_t[IW_LANE + h:IW_LANE + h + 1, :] * (IDX_HEADS ** -0.5) for h in range(IDX_HEADS)]
    q_chunk = (qs + lax.broadcasted_iota(I32, (DSA_KB, tq), 1)) // CHUNK
    k_chunk = lax.broadcasted_iota(I32, (DSA_KB, tq), 0) // CHUNK

    def score_block(c, carry):
        cs = pl.multiple_of(c * DSA_KB, DSA_KB)
        ik = kvi_ref[0, pl.ds(cs, DSA_KB), 128:192]
        sc = jnp.zeros((DSA_KB, tq), F32)
        for h in range(IDX_HEADS):
            lg = _dot_nt(ik, iq[:, h * HEAD_DIM:(h + 1) * HEAD_DIM])
            sc = sc + w_row[h] * jnp.maximum(lg, 0.0)
        admissible = (c * (DSA_KB // CHUNK) + k_chunk) <= q_chunk
        sc_ref[pl.ds(cs, DSA_KB), :] = jnp.where(admissible, sc, -jnp.inf)
        return carry

    lax.fori_loop(0, nblk, score_block, 0)

    def count(pred_fn):
        sub = DSA_SEG // 8

        def seg_body(s, a):
            base = pl.multiple_of(s * DSA_SEG, DSA_SEG)
            parts = [jnp.where(pred_fn(sc_ref[pl.ds(base + j * sub, sub), :]), 1.0, 0.0) for j in range(8)]
            return a + ((parts[0] + parts[1]) + (parts[2] + parts[3])) + ((parts[4] + parts[5]) + (parts[6] + parts[7]))
        a = lax.fori_loop(0, nseg, seg_body, jnp.zeros((sub, tq), F32))
        return jnp.sum(a, axis=0, keepdims=True)

    def key_to_float(key):
        return lax.bitcast_convert_type(jnp.where(key >= 0, key, key ^ 0x7FFFFFFF), F32)

    kf = jnp.float32(topk)
    n_rows = (nseg * DSA_SEG).astype(F32)
    t0 = jnp.where(count(lambda s: s >= 0.0) >= kf, 0, INT_MIN).astype(I32)

    def bit_step(i, t):
        cand = t + lax.shift_left(jnp.int32(1), 30 - i)
        cand_f = key_to_float(cand)
        cnt = jnp.where(cand <= KEY_NEG_INF, n_rows, count(lambda s: s >= cand_f))
        return jnp.where(cnt >= kf, cand, t)

    thr_key = lax.fori_loop(0, 31, bit_step, t0)
    thr = jnp.where(thr_key <= KEY_NEG_INF, -jnp.inf, key_to_float(thr_key))
    need = kf - count(lambda s: s > thr)
    tri = tri_ref[...]

    def select_block(c, prefix):
        cs = pl.multiple_of(c * DSA_KB, DSA_KB)
        sc = sc_ref[pl.ds(cs, DSA_KB), :]
        eqf = jnp.where(sc == thr, 1.0, 0.0)
        rank = _dot(tri, eqf.astype(BF16)) + prefix
        tie = jnp.where(rank < need, eqf, 0.0)
        sel = jnp.where(sc > thr, 1.0, tie)
        bias_ref[pl.ds(cs, DSA_KB), :] = jnp.where(sc > -jnp.inf, (sel - 1.0) * (-NEG_BIG), NEG_BIG)
        return prefix + jnp.sum(eqf, axis=0, keepdims=True)

    lax.fori_loop(0, nblk, select_block, jnp.zeros((1, tq), F32))

    q8 = dq_ref[0].reshape(DSA_HEADS * tq, HEAD_DIM)
    m_ref[...] = jnp.full_like(m_ref, NEG_BIG)
    acc_ref[...] = jnp.zeros_like(acc_ref)

    def attn_block(c, carry):
        cs = pl.multiple_of(c * DSA_KB, DSA_KB)
        k = kvi_ref[0, pl.ds(cs, DSA_KB), 0:HEAD_DIM]
        b = bias_ref[pl.ds(cs, DSA_KB), :]
        s = _dot_nt(k, q8) + jnp.concatenate([b] * DSA_HEADS, axis=1)
        m_old = m_ref[...]
        m_new = jnp.maximum(m_old, jnp.max(s, axis=0, keepdims=True))
        p = jnp.exp(s - m_new)
        alpha = jnp.exp(m_old - m_new)
        vt = vt_ref[0, :, pl.ds(cs, DSA_KB)]
        acc_ref[...] = alpha * acc_ref[...] + _dot(vt, p.astype(BF16))
        m_ref[...] = m_new
        return carry

    lax.fori_loop(0, nblk, attn_block, 0)
    for hd in range(DSA_HEADS):
        a = acc_ref[:, hd * tq:(hd + 1) * tq]
        o = (a / a[0:1, :]).T
        o_ref[0, :, hd * HEAD_DIM:(hd + 1) * HEAD_DIM] = o[:, HEAD_DIM:].astype(BF16)


def _dsa_attention(dq, iq, iw, kvi, vt, tq, topk):
    batch, _, seq, _ = dq.shape
    return pl.pallas_call(
        functools.partial(_dsa_kernel, tq=tq, topk=topk),
        grid=(batch, seq // tq),
        in_specs=[
            pl.BlockSpec((1, DSA_HEADS, tq, HEAD_DIM), lambda b, i: (b, 0, i, 0)),
            pl.BlockSpec((1, tq, 256), lambda b, i: (b, i, 0)),
            pl.BlockSpec((1, tq, 128), lambda b, i: (b, i, 0)),
            pl.BlockSpec((1, seq, 256), lambda b, i: (b, 0, 0)),
            pl.BlockSpec((1, 128, seq), lambda b, i: (b, 0, 0)),
            pl.BlockSpec((DSA_KB, DSA_KB), lambda b, i: (0, 0)),
        ],
        out_specs=pl.BlockSpec((1, tq, BRANCH_W), lambda b, i: (b, i, 0)),
        out_shape=jax.ShapeDtypeStruct((batch, seq, BRANCH_W), BF16),
        scratch_shapes=[pltpu.VMEM((seq, tq), F32), pltpu.VMEM((seq, tq), F32),
                        pltpu.VMEM((1, DSA_HEADS * tq), F32), pltpu.VMEM((128, DSA_HEADS * tq), F32)],
        compiler_params=_cparams(("parallel", "parallel")),
        name="dsa_attention",
    )(dq, iq.reshape(batch, seq, 256), iw.reshape(batch, seq, 128), kvi.reshape(batch, seq, 256), vt,
      _strict_tri(DSA_KB, lower=True))


def _memkv_kernel(m_ref, g_ref, w_ref, gk_ref, bd_ref, mk_ref, mv_ref):
    x = m_ref[...]
    ms = jnp.mean(x * x, axis=-1, keepdims=True)
    h = (x * lax.rsqrt(ms + EPS) * g_ref[...]).astype(BF16)
    mw = MEM_HEADS * MEM_HEAD_DIM
    k = _dot(h, w_ref[:, :mw])
    msk = _dot_split(k * k, bd_ref[...]) * (1.0 / MEM_HEAD_DIM)
    mk_ref[...] = (k * lax.rsqrt(msk + EPS) * gk_ref[...]).astype(BF16)
    mv_ref[...] = _dot(h, w_ref[:, mw:]).astype(BF16)


def _mem_kv(mem2d, g_mem, w_mem_kv, g_k_mem, tm):
    n = mem2d.shape[0]
    mw = MEM_HEADS * MEM_HEAD_DIM
    row = lambda i: (i, 0)
    const = lambda i: (0, 0)
    return pl.pallas_call(
        _memkv_kernel,
        grid=(n // tm,),
        in_specs=[pl.BlockSpec((tm, D_MODEL), row), pl.BlockSpec((1, D_MODEL), const),
                  pl.BlockSpec((D_MODEL, 2 * mw), const), pl.BlockSpec((1, mw), const),
                  pl.BlockSpec((mw, mw), const)],
        out_specs=[pl.BlockSpec((tm, mw), row), pl.BlockSpec((tm, mw), row)],
        out_shape=[jax.ShapeDtypeStruct((n, mw), BF16), jax.ShapeDtypeStruct((n, mw), BF16)],
        compiler_params=_cparams(("parallel",)),
        name="mem_kv",
    )(mem2d, g_mem[None, :], w_mem_kv.astype(BF16), jnp.tile(g_k_mem, MEM_HEADS)[None, :],
      _block_diag_ones(mw, MEM_HEAD_DIM))


def _memattn_kernel(q_ref, k_ref, v_ref, o_ref):
    for hd in range(MEM_HEADS):
        sl = slice(hd * MEM_HEAD_DIM, (hd + 1) * MEM_HEAD_DIM)
        s = _dot_nt(q_ref[0, :, sl], k_ref[0, :, sl]) * (MEM_HEAD_DIM ** -0.5)
        p = jnp.exp(s - jnp.max(s, axis=-1, keepdims=True))
        o = _dot(p.astype(BF16), v_ref[0, :, sl]) / jnp.sum(p, axis=-1, keepdims=True)
        o_ref[0, :, sl] = o.astype(BF16)


def _mem_attention(mq, mk, mv, tq):
    batch, seq, mw = mq.shape
    mlen = mk.shape[1]
    kspec = pl.BlockSpec((1, mlen, mw), lambda b, i: (b, 0, 0))
    return pl.pallas_call(
        _memattn_kernel,
        grid=(batch, seq // tq),
        in_specs=[pl.BlockSpec((1, tq, mw), lambda b, i: (b, i, 0)), kspec, kspec],
        out_specs=pl.BlockSpec((1, tq, mw), lambda b, i: (b, i, 0)),
        out_shape=jax.ShapeDtypeStruct((batch, seq, mw), BF16),
        compiler_params=_cparams(("parallel", "parallel")),
        name="mem_attention",
    )(mq, mk, mv)


def _merge_kernel(x_ref, g_ref, osb_ref, odsa_ref, omem_ref, wg_ref, bg_ref, wb_ref, wo_ref,
                  gf_ref, wrh_ref, wrl_ref, br_ref,
                  x2_ref, h2_ref, idx_ref, gate_ref):
    x = x_ref[...]
    ms = jnp.mean(x * x, axis=-1, keepdims=True)
    h = (x * lax.rsqrt(ms + EPS) * g_ref[...]).astype(BF16)
    merged = None
    for n, o_ref in enumerate((osb_ref, odsa_ref, omem_ref)):
        gate = jax.nn.sigmoid(_dot(h, wg_ref[n]) + bg_ref[n])
        term = gate * _dot(o_ref[...], wb_ref[n])
        merged = term if merged is None else merged + term
    x2 = x + _dot(merged.astype(BF16), wo_ref[...])
    x2_ref[...] = x2

    ms2 = jnp.mean(x2 * x2, axis=-1, keepdims=True)
    h2 = x2 * lax.rsqrt(ms2 + EPS) * gf_ref[...]
    h2_ref[...] = h2
    hi, lo = _split_bf16(h2)
    logits = _dot(hi, wrh_ref[...]) + _dot(hi, wrl_ref[...]) + _dot(lo, wrh_ref[...]) + br_ref[...]

    lane = lax.broadcasted_iota(I32, logits.shape, 1)
    vals = logits
    top_v, top_i = [], []
    for _ in range(TOP_K):
        mx = jnp.max(vals, axis=-1, keepdims=True)
        ix = jnp.min(jnp.where(vals == mx, lane, N_EXPERTS), axis=-1, keepdims=True)
        top_v.append(mx)
        top_i.append(ix)
        vals = jnp.where(lane == ix, -jnp.inf, vals)
    ex = [jnp.exp(v - top_v[0]) for v in top_v]
    den = ex[0] + ex[1] + ex[2] + ex[3]
    for k in range(TOP_K):
        idx_ref[:, k:k + 1] = top_i[k]
        gate_ref[:, k:k + 1] = ex[k] / den


def _merge_route(x2d, g_mix, o_sb, o_dsa, o_mem, w_gate, b_gate, w_branch, w_out, g_ffn, w_router, b_router, tm):
    n = x2d.shape[0]
    row = lambda i: (i, 0)
    c2 = lambda i: (0, 0)
    c3 = lambda i: (0, 0, 0)
    wr_hi = w_router.astype(BF16)
    wr_lo = (w_router - wr_hi.astype(F32)).astype(BF16)
    return pl.pallas_call(
        _merge_kernel,
        grid=(n // tm,),
        in_specs=[pl.BlockSpec((tm, D_MODEL), row), pl.BlockSpec((1, D_MODEL), c2),
                  pl.BlockSpec((tm, BRANCH_W), row), pl.BlockSpec((tm, BRANCH_W), row),
                  pl.BlockSpec((tm, BRANCH_W), row),
                  pl.BlockSpec((3, D_MODEL, D_MODEL), c3), pl.BlockSpec((3, 1, D_MODEL), c3),
                  pl.BlockSpec((3, BRANCH_W, D_MODEL), c3), pl.BlockSpec((D_MODEL, D_MODEL), c2),
                  pl.BlockSpec((1, D_MODEL), c2), pl.BlockSpec((D_MODEL, N_EXPERTS), c2),
                  pl.BlockSpec((D_MODEL, N_EXPERTS), c2), pl.BlockSpec((1, N_EXPERTS), c2)],
        out_specs=[pl.BlockSpec((tm, D_MODEL), row), pl.BlockSpec((tm, D_MODEL), row),
                   pl.BlockSpec((tm, TOP_K), row), pl.BlockSpec((tm, TOP_K), row)],
        out_shape=[jax.ShapeDtypeStruct((n, D_MODEL), F32), jax.ShapeDtypeStruct((n, D_MODEL), F32),
                   jax.ShapeDtypeStruct((n, TOP_K), I32), jax.ShapeDtypeStruct((n, TOP_K), F32)],
        compiler_params=_cparams(("parallel",)),
        name="merge_route",
    )(x2d, g_mix[None, :], o_sb, o_dsa, o_mem, w_gate.astype(BF16), b_gate[:, None, :],
      w_branch.astype(BF16), w_out.astype(BF16), g_ffn[None, :], wr_hi, wr_lo, b_router[None, :])


def _rank_kernel(idx_ref, l_ref, rank_ref, cnt_ref, carry_ref):
    @pl.when(pl.program_id(0) == 0)
    def _():
        carry_ref[...] = jnp.zeros_like(carry_ref)

    idx = idx_ref[...]
    tm = idx.shape[0]
    lane = lax.broadcasted_iota(I32, (tm, LANES), 1)
    hits = [lane == idx[:, k:k + 1] for k in range(TOP_K)]
    onehot = sum(jnp.where(hk, 1.0, 0.0) for hk in hits)
    before = _dot(l_ref[...], onehot.astype(BF16)) + carry_ref[...]
    for k in range(TOP_K):
        rank_ref[:, k:k + 1] = jnp.sum(jnp.where(hits[k], before, 0.0), axis=-1, keepdims=True).astype(I32)
    carry_ref[...] += jnp.sum(onehot, axis=0, keepdims=True)
    cnt_ref[...] = carry_ref[...]


def _expert_rank(top_idx, tm):
    n = top_idx.shape[0]
    return pl.pallas_call(
        _rank_kernel,
        grid=(n // tm,),
        in_specs=[pl.BlockSpec((tm, TOP_K), lambda i: (i, 0)), pl.BlockSpec((tm, tm), lambda i: (0, 0))],
        out_specs=[pl.BlockSpec((tm, TOP_K), lambda i: (i, 0)), pl.BlockSpec((1, LANES), lambda i: (0, 0))],
        out_shape=[jax.ShapeDtypeStruct((n, TOP_K), I32), jax.ShapeDtypeStruct((1, LANES), F32)],
        scratch_shapes=[pltpu.VMEM((1, LANES), F32)],
        compiler_params=_cparams(("arbitrary",)),
        name="expert_rank",
    )(top_idx, _strict_tri(tm, lower=True))


DISPATCH_TOKENS = 256


def _dispatch_kernel(dest_ref, h_ref, xs_in_ref, xs_ref, sem):
    del xs_in_ref
    n_copies = DISPATCH_TOKENS * TOP_K

    def copy(a):
        return pltpu.make_async_copy(h_ref.at[pl.ds(a // TOP_K, 1)],
                                     xs_ref.at[pl.ds(dest_ref[0, 0, a], 1)], sem)

    def start(a, c):
        copy(a).start()
        return c

    lax.fori_loop(0, n_copies, start, 0, unroll=8)
    pltpu.make_async_copy(xs_ref.at[pl.ds(0, n_copies)], xs_ref.at[pl.ds(0, n_copies)], sem).wait()


def _dispatch(dest, h2, n_slots):
    n = h2.shape[0]
    per = DISPATCH_TOKENS * TOP_K
    return pl.pallas_call(
        _dispatch_kernel,
        grid=(n // DISPATCH_TOKENS,),
        in_specs=[pl.BlockSpec((1, 1, per), lambda i: (i, 0, 0), memory_space=pltpu.SMEM),
                  pl.BlockSpec((DISPATCH_TOKENS, D_MODEL), lambda i: (i, 0)),
                  pl.BlockSpec(memory_space=pl.ANY)],
        out_specs=pl.BlockSpec(memory_space=pl.ANY),
        out_shape=jax.ShapeDtypeStruct((n_slots, D_MODEL), F32),
        scratch_shapes=[pltpu.SemaphoreType.DMA(())],
        input_output_aliases={2: 0},
        compiler_params=_cparams(("arbitrary",)),
        name="moe_dispatch",
    )(dest.reshape(n // DISPATCH_TOKENS, 1, per), h2, jnp.zeros((n_slots, D_MODEL), F32))


def _expert_kernel(blk_e_ref, nused_ref, x_ref, w1_ref, b1_ref, w2_ref, b2_ref, y_ref):
    del blk_e_ref

    @pl.when(pl.program_id(0) < nused_ref[0])
    def _():
        hb = _dot(x_ref[...].astype(BF16), w1_ref[0]) + b1_ref[0]
        g = jnp.minimum(hb[:, :D_EXPERT], SWIGLU_LIMIT)
        u = jnp.clip(hb[:, D_EXPERT:], -SWIGLU_LIMIT, SWIGLU_LIMIT)
        act = (u + 1.0) * (g * jax.nn.sigmoid(SWIGLU_ALPHA * g))
        y_ref[...] = _dot(act.astype(BF16), w2_ref[0]) + b2_ref[0]

    @pl.when(pl.program_id(0) >= nused_ref[0])
    def _():
        y_ref[...] = jnp.zeros_like(y_ref)


def _experts(blk_e, n_used, xs, w_e_in, b_e_in, w_e_out, b_e_out):
    n_slots = xs.shape[0]
    nblk = n_slots // MOE_BLOCK
    grid_spec = pltpu.PrefetchScalarGridSpec(
        num_scalar_prefetch=2,
        grid=(nblk,),
        in_specs=[pl.BlockSpec((MOE_BLOCK, D_MODEL), lambda i, be, nu: (i, 0)),
                  pl.BlockSpec((1, D_MODEL, 2 * D_EXPERT), lambda i, be, nu: (be[i], 0, 0)),
                  pl.BlockSpec((1, 1, 2 * D_EXPERT), lambda i, be, nu: (be[i], 0, 0)),
                  pl.BlockSpec((1, D_EXPERT, D_MODEL), lambda i, be, nu: (be[i], 0, 0)),
                  pl.BlockSpec((1, 1, D_MODEL), lambda i, be, nu: (be[i], 0, 0))],
        out_specs=pl.BlockSpec((MOE_BLOCK, D_MODEL), lambda i, be, nu: (i, 0)),
    )
    return pl.pallas_call(
        _expert_kernel,
        grid_spec=grid_spec,
        out_shape=jax.ShapeDtypeStruct((n_slots, D_MODEL), F32),
        compiler_params=_cparams(("arbitrary",)),
        name="moe_experts",
    )(blk_e, n_used, xs, w_e_in.astype(BF16), b_e_in[:, None, :], w_e_out.astype(BF16), b_e_out[:, None, :])


def _combine_kernel(dest_ref, x_ref, gate_ref, y_ref, o_ref, buf_ref, sem):
    n_copies = DISPATCH_TOKENS * TOP_K

    def start(a, c):
        row = (a % TOP_K) * DISPATCH_TOKENS + a // TOP_K
        pltpu.make_async_copy(y_ref.at[pl.ds(dest_ref[0, 0, a], 1)], buf_ref.at[pl.ds(row, 1)], sem).start()
        return c

    lax.fori_loop(0, n_copies, start, 0, unroll=8)
    pltpu.make_async_copy(y_ref.at[pl.ds(0, n_copies)], buf_ref, sem).wait()
    gate = gate_ref[...]
    out = x_ref[...]
    for k in range(TOP_K):
        out = out + gate[:, k:k + 1] * buf_ref[k * DISPATCH_TOKENS:(k + 1) * DISPATCH_TOKENS]
    o_ref[...] = out


def _combine(dest, x2, gate, ys):
    n = x2.shape[0]
    per = DISPATCH_TOKENS * TOP_K
    return pl.pallas_call(
        _combine_kernel,
        grid=(n // DISPATCH_TOKENS,),
        in_specs=[pl.BlockSpec((1, 1, per), lambda i: (i, 0, 0), memory_space=pltpu.SMEM),
                  pl.BlockSpec((DISPATCH_TOKENS, D_MODEL), lambda i: (i, 0)),
                  pl.BlockSpec((DISPATCH_TOKENS, TOP_K), lambda i: (i, 0)),
                  pl.BlockSpec(memory_space=pl.ANY)],
        out_specs=pl.BlockSpec((DISPATCH_TOKENS, D_MODEL), lambda i: (i, 0)),
        out_shape=jax.ShapeDtypeStruct((n, D_MODEL), F32),
        scratch_shapes=[pltpu.VMEM((TOP_K * DISPATCH_TOKENS, D_MODEL), F32), pltpu.SemaphoreType.DMA(())],
        compiler_params=_cparams(("arbitrary",)),
        name="moe_combine",
    )(dest.reshape(n // DISPATCH_TOKENS, 1, per), x2, gate, ys)


def _moe(x2, h2, top_idx, gate, w_e_in, b_e_in, w_e_out, b_e_out):
    n = x2.shape[0]
    rank, counts = _expert_rank(top_idx, 512)
    counts = counts[0, :N_EXPERTS].astype(I32)
    padded = (counts + MOE_BLOCK - 1) // MOE_BLOCK * MOE_BLOCK
    pend = jnp.cumsum(padded)
    pstart = pend - padded
    nblk = -(-(n * TOP_K) // MOE_BLOCK) + N_EXPERTS
    blk_start = jnp.arange(nblk, dtype=I32) * MOE_BLOCK
    blk_e = jnp.minimum(jnp.sum((pend[None, :] <= blk_start[:, None]).astype(I32), axis=1), N_EXPERTS - 1)
    n_used = (pend[-1:] // MOE_BLOCK).astype(I32)
    onehot = top_idx[:, :, None] == jnp.arange(N_EXPERTS, dtype=I32)[None, None, :]
    dest = rank + jnp.sum(jnp.where(onehot, pstart[None, None, :], 0), axis=-1)
    xs = _dispatch(dest, h2, nblk * MOE_BLOCK)
    ys = _experts(blk_e, n_used, xs, w_e_in, b_e_in, w_e_out, b_e_out)
    return _combine(dest, x2, gate, ys)


def _layer(x, mem, g_mix, w_in, g_q_dsa, g_k_dsa, g_q_mem, g_k_mem, g_mem, w_mem_kv, w_gate, b_gate,
           w_branch, w_out, g_ffn, w_router, b_router, w_e_in, b_e_in, w_e_out, b_e_out):
    batch, seq, _ = x.shape
    n = batch * seq
    topk = min(DSA_TOPK_MAX, seq // 4)
    x2d = x.reshape(n, D_MODEL)
    tm = min(512, seq)
    sq, sk, sv, dq, iq, kvi, iw, mq, vt = _inproj(x2d, batch, seq, g_mix, w_in, g_q_dsa, g_k_dsa, g_q_mem, tm)
    o_sb = _sb_attention(sq, sk, sv, min(256, seq))
    o_dsa = _dsa_attention(dq, iq, iw, kvi, vt, 128, topk)
    mlen = mem.shape[1]
    mk, mv = _mem_kv(mem.reshape(batch * mlen, D_MODEL), g_mem, w_mem_kv, g_k_mem, min(512, batch * mlen))
    mw = MEM_HEADS * MEM_HEAD_DIM
    o_mem = _mem_attention(mq.reshape(batch, seq, mw), mk.reshape(batch, mlen, mw), mv.reshape(batch, mlen, mw), tm)
    x2, h2, top_idx, gate = _merge_route(
        x2d, g_mix, o_sb.reshape(n, BRANCH_W), o_dsa.reshape(n, BRANCH_W), o_mem.reshape(n, mw),
        w_gate, b_gate, w_branch, w_out, g_ffn, w_router, b_router, tm)
    out = _moe(x2, h2, top_idx, gate, w_e_in, b_e_in, w_e_out, b_e_out)
    return out.reshape(batch, seq, D_MODEL)


def kernel(x, mem, g_mix, w_in, g_q_dsa, g_k_dsa, g_q_mem, g_k_mem, g_mem, w_mem_kv, w_gate, b_gate, w_branch, w_out, g_ffn, w_router, b_router, w_e_in, b_e_in, w_e_out, b_e_out):
    for l in range(g_mix.shape[0]):
        x = _layer(x, mem, g_mix[l], w_in[l], g_q_dsa[l], g_k_dsa[l], g_q_mem[l], g_k_mem[l], g_mem[l],
                   w_mem_kv[l], w_gate[l], b_gate[l], w_branch[l], w_out[l], g_ffn[l], w_router[l],
                   b_router[l], w_e_in[l], b_e_in[l], w_e_out[l], b_e_out[l])
    return x
```

```python
import functools

import numpy as np
import jax
import jax.numpy as jnp
from jax import lax
from jax.experimental import pallas as pl
from jax.experimental.pallas import tpu as pltpu

F32 = jnp.float32
BF16 = jnp.bfloat16
I32 = jnp.int32

D_MODEL = 1024
CHUNK = 64
SB_HEADS = 8
DSA_HEADS = 8
HEAD_DIM = 64
IDX_HEADS = 4
DSA_TOPK_MAX = 256
MEM_HEADS = 4
MEM_HEAD_DIM = 128
N_EXPERTS = 32
TOP_K = 4
D_EXPERT = D_MODEL
SWIGLU_LIMIT = 7.0
SWIGLU_ALPHA = 1.702
ROPE_THETA = 10000.0
EPS = 1e-6
MOE_BLOCK = 512

BRANCH_W = 512
IN_SIZES = (512, 512, 512, 512, 64, 64, 256, 64, 4, 512)
C_SQ, C_SK, C_SV, C_DQ, C_MQ, C_IQ, C_SMALL, C_END = 0, 512, 1024, 1536, 2048, 2560, 2816, 3072
IW_LANE = 64

LANES = 128
NEG_BIG = -1e30
SB_CUTOFF = 110.0
KEY_NEG_INF = int(np.array(-np.inf, np.float32).view(np.int32)) ^ 0x7FFFFFFF
INT_MIN = -(2 ** 31)

VMEM_LIMIT = 56 * 1024 * 1024


def _cparams(sem):
    return pltpu.CompilerParams(dimension_semantics=sem, vmem_limit_bytes=VMEM_LIMIT)


def _dot(a, b):
    return jnp.dot(a, b, preferred_element_type=F32)


def _dot_nt(a, b):
    return lax.dot_general(a, b, (((1,), (1,)), ((), ())), preferred_element_type=F32)


def _split_bf16(x):
    hi = x.astype(BF16)
    lo = (x - hi.astype(F32)).astype(BF16)
    return hi, lo


def _dot_split(x, m_bf16):
    hi, lo = _split_bf16(x)
    return _dot(hi, m_bf16) + _dot(lo, m_bf16)


TOKEN_TILE = D_MODEL // LANES


def _store_token_tiles(ref, y):
    rows = y.shape[0]
    for s in range(TOKEN_TILE):
        ref[pl.ds(s, rows, stride=TOKEN_TILE), :] = y[:, s * LANES:(s + 1) * LANES]


def _load_token_tiles(ref, start_row, rows, s):
    return ref[pl.ds(start_row * TOKEN_TILE + s, rows, stride=TOKEN_TILE), :]


def _rot_half_unsigned(y):
    w = y.shape[1]
    lane = lax.broadcasted_iota(I32, y.shape, 1)
    return jnp.where((lane & 32) == 0, pltpu.roll(y, w - 32, 1), pltpu.roll(y, 32, 1))


def _inproj_kernel(x_ref, g_ref, w_ref, cos_ref, sin_ref, coss_ref, sins_ref, gq_ref, gks_ref,
                   gm_ref, bd64_ref, bd128_ref,
                   sq_ref, sk_ref, sv_ref, dq_ref, iq_ref, kvi_ref, iw_ref, mq_ref, vt_ref):
    x = x_ref[...]
    ms = jnp.mean(x * x, axis=-1, keepdims=True)
    h = (x * lax.rsqrt(ms + EPS) * g_ref[...]).astype(BF16)

    def seg(a, b):
        return _dot(h, w_ref[:, a:b])

    def put_heads(ref, y):
        for hd in range(y.shape[1] // HEAD_DIM):
            ref[0, hd] = y[:, hd * HEAD_DIM:(hd + 1) * HEAD_DIM].astype(BF16)

    put_heads(sq_ref, seg(C_SQ, C_SK) * (HEAD_DIM ** -0.5))
    put_heads(sk_ref, seg(C_SK, C_SV))
    put_heads(sv_ref, seg(C_SV, C_DQ))

    y = seg(C_DQ, C_MQ)
    msq = _dot_split(y * y, bd64_ref[...]) * (1.0 / HEAD_DIM)
    y = y * lax.rsqrt(msq + EPS) * gq_ref[...]
    y = y * cos_ref[...] + _rot_half_unsigned(y) * sin_ref[...]
    put_heads(dq_ref, y * (HEAD_DIM ** -0.5))

    y = seg(C_IQ, C_SMALL)
    y = y * cos_ref[:, :256] + _rot_half_unsigned(y) * sin_ref[:, :256]
    iq_ref[...] = (y * (HEAD_DIM ** -0.5)).astype(BF16)

    y = seg(C_SMALL, C_END)
    lane = lax.broadcasted_iota(I32, y.shape, 1)
    is_k = lane < HEAD_DIM
    msk = jnp.sum(jnp.where(is_k, y * y, 0.0), axis=-1, keepdims=True) * (1.0 / HEAD_DIM)
    y = y * jnp.where(is_k, lax.rsqrt(msk + EPS) * gks_ref[...], 1.0)
    y = y * coss_ref[...] + _rot_half_unsigned(y) * sins_ref[...]
    kvi_ref[...] = y.astype(BF16)
    iw_ref[...] = y[:, 128:256]
    kv_t = y[:, 0:128].T
    row_t = lax.broadcasted_iota(I32, kv_t.shape, 0)
    vt_ref[0] = jnp.where(row_t < HEAD_DIM, 1.0, kv_t).astype(BF16)

    y = seg(C_MQ, C_IQ)
    msm = _dot_split(y * y, bd128_ref[...]) * (1.0 / MEM_HEAD_DIM)
    mq_ref[...] = (y * lax.rsqrt(msm + EPS) * gm_ref[...]).astype(BF16)


def _rope_tables(seq):
    half = HEAD_DIM // 2
    inv = ROPE_THETA ** (-jnp.arange(half, dtype=F32) / half)
    ang = jnp.arange(seq).astype(F32)[:, None] * inv[None, :]
    cos = jnp.cos(ang)
    sin = jnp.sin(ang)
    cos64 = jnp.concatenate([cos, cos], axis=1)
    sin64 = jnp.concatenate([-sin, sin], axis=1)
    one = jnp.ones_like(cos64)
    zero = jnp.zeros_like(cos64)
    cosq = jnp.tile(cos64, (1, 8))
    sinq = jnp.tile(sin64, (1, 8))
    coss = jnp.concatenate([cos64, one, cos64, one], axis=1)
    sins = jnp.concatenate([sin64, zero, sin64, zero], axis=1)
    return cosq, sinq, coss, sins


def _block_diag_ones(width, group):
    idx = np.arange(width) // group
    return jnp.asarray((idx[:, None] == idx[None, :]).astype(np.float32), dtype=BF16)


def _inproj(x2d, batch, seq, g_mix, w_in, g_q_dsa, g_k_dsa, g_q_mem, tm):
    n = x2d.shape[0]
    sizes = np.cumsum((0,) + IN_SIZES)
    col = {name: (int(sizes[i]), int(sizes[i + 1])) for i, name in enumerate(
        ("sq", "sk", "sv", "dq", "dk", "dv", "iq", "ik", "iw", "mq"))}
    order = ("sq", "sk", "sv", "dq", "mq", "iq", "dk", "dv", "ik", "iw")
    w = jnp.concatenate([w_in[:, col[k][0]:col[k][1]] for k in order]
                        + [jnp.zeros((D_MODEL, C_END - sum(IN_SIZES)), w_in.dtype)], axis=1).astype(BF16)
    cosq, sinq, coss, sins = (jnp.asarray(t) for t in _rope_tables(seq))
    gq = jnp.tile(g_q_dsa, 8)[None, :]
    gks = jnp.concatenate([g_k_dsa, jnp.ones((256 - HEAD_DIM,), F32)])[None, :]
    gm = jnp.tile(g_q_mem, MEM_HEADS)[None, :]
    spb = seq // tm
    row = lambda i: (i, 0)
    const = lambda i: (0, 0)
    pos = lambda i: (i % spb, 0)
    heads = lambda i: (i // spb, 0, i % spb, 0)
    head_shape = jax.ShapeDtypeStruct((batch, 8, seq, HEAD_DIM), BF16)
    head_spec = pl.BlockSpec((1, 8, tm, HEAD_DIM), heads)
    return pl.pallas_call(
        _inproj_kernel,
        grid=(n // tm,),
        in_specs=[
            pl.BlockSpec((tm, D_MODEL), row),
            pl.BlockSpec((1, D_MODEL), const),
            pl.BlockSpec((D_MODEL, C_END), const),
            pl.BlockSpec((tm, 512), pos), pl.BlockSpec((tm, 512), pos),
            pl.BlockSpec((tm, 256), pos), pl.BlockSpec((tm, 256), pos),
            pl.BlockSpec((1, 512), const), pl.BlockSpec((1, 256), const), pl.BlockSpec((1, 512), const),
            pl.BlockSpec((512, 512), const), pl.BlockSpec((512, 512), const),
        ],
        out_specs=[head_spec, head_spec, head_spec, head_spec,
                   pl.BlockSpec((tm, 256), row), pl.BlockSpec((tm, 256), row),
                   pl.BlockSpec((tm, 128), row), pl.BlockSpec((tm, 512), row),
                   pl.BlockSpec((1, 128, tm), lambda i: (i // spb, 0, i % spb))],
        out_shape=[head_shape, head_shape, head_shape, head_shape,
                   jax.ShapeDtypeStruct((n, 256), BF16), jax.ShapeDtypeStruct((n, 256), BF16),
                   jax.ShapeDtypeStruct((n, 128), F32), jax.ShapeDtypeStruct((n, 512), BF16),
                   jax.ShapeDtypeStruct((batch, 128, seq), BF16)],
        compiler_params=_cparams(("parallel",)),
        name="inproj",
    )(x2d, g_mix[None, :], w, cosq, sinq, coss, sins, gq, gks, gm,
      _block_diag_ones(512, HEAD_DIM), _block_diag_ones(512, MEM_HEAD_DIM))


def _sb_kernel(q_ref, k_ref, v_ref, u_ref, o_ref, acc_ref, car_ref, out_ref, *, tq):
    qi = pl.program_id(1)
    rows = lax.broadcasted_iota(I32, (tq, tq), 0)
    cols = lax.broadcasted_iota(I32, (tq, tq), 1)
    dif = cols - rows
    u = u_ref[...]

    def head(hd, carry):
        q = q_ref[0, hd]
        acc_ref[...] = jnp.zeros_like(acc_ref)
        car_ref[...] = jnp.zeros_like(car_ref)

        def cond(c):
            kb, mx = c
            return jnp.logical_and(kb >= 0, mx > -SB_CUTOFF)

        def body(c):
            kb, _ = c
            ks = pl.multiple_of(kb * tq, tq)
            k = k_ref[0, hd, pl.ds(ks, tq), :]
            v = v_ref[0, hd, pl.ds(ks, tq), :]
            z = _dot_nt(q, k)
            earlier = dif < (qi - kb) * tq
            sp = jnp.maximum(z, 0.0) + jnp.log(1.0 + jnp.exp(-jnp.abs(z)))
            lk = jnp.where(earlier, -sp, 0.0)
            between = _dot_split(lk, u)
            car = car_ref[...]
            w = jnp.where(earlier, jnp.exp((z - sp) + between + car), 0.0)
            acc_ref[...] += _dot(w.astype(BF16), v)
            car_new = car + jnp.sum(lk, axis=-1, keepdims=True)
            car_ref[...] = car_new
            return kb - 1, jnp.max(car_new)

        lax.while_loop(cond, body, (qi, jnp.float32(0.0)))
        out_ref[hd] = acc_ref[...]
        return carry

    lax.fori_loop(0, SB_HEADS, head, 0)
    for hd in range(SB_HEADS):
        o_ref[0, :, hd * HEAD_DIM:(hd + 1) * HEAD_DIM] = out_ref[hd].astype(BF16)


def _strict_tri(n, lower):
    i = np.arange(n)
    m = (i[:, None] > i[None, :]) if lower else (i[:, None] < i[None, :])
    return jnp.asarray(m.astype(np.float32), dtype=BF16)


def _sb_attention(sq, sk, sv, tq):
    batch, _, seq, _ = sq.shape
    qspec = pl.BlockSpec((1, SB_HEADS, tq, HEAD_DIM), lambda b, i: (b, 0, i, 0))
    kspec = pl.BlockSpec((1, SB_HEADS, seq, HEAD_DIM), lambda b, i: (b, 0, 0, 0))
    return pl.pallas_call(
        functools.partial(_sb_kernel, tq=tq),
        grid=(batch, seq // tq),
        in_specs=[qspec, kspec, kspec, pl.BlockSpec((tq, tq), lambda b, i: (0, 0))],
        out_specs=pl.BlockSpec((1, tq, BRANCH_W), lambda b, i: (b, i, 0)),
        out_shape=jax.ShapeDtypeStruct((batch, seq, BRANCH_W), BF16),
        scratch_shapes=[pltpu.VMEM((tq, HEAD_DIM), F32), pltpu.VMEM((tq, 1), F32),
                        pltpu.VMEM((SB_HEADS, tq, HEAD_DIM), F32)],
        compiler_params=_cparams(("parallel", "parallel")),
        name="sb_attention",
    )(sq, sk, sv, _strict_tri(tq, lower=True))


DSA_SEG = 512
DSA_KB = 256


def _dsa_kernel(dq_ref, iq_ref, iw_ref, kvi_ref, vt_ref, tri_ref, o_ref,
                sc_ref, bias_ref, m_ref, acc_ref, *, tq, topk):
    qi = pl.program_id(1)
    qs = qi * tq
    nseg = qs // DSA_SEG + 1
    nblk = nseg * (DSA_SEG // DSA_KB)
    iq = iq_ref[0]
    w_t = iw_ref[0].T
    w_row = [w_t[IW_LANE + h:IW_LANE + h + 1, :] * (IDX_HEADS ** -0.5) for h in range(IDX_HEADS)]
    q_chunk = (qs + lax.broadcasted_iota(I32, (DSA_KB, tq), 1)) // CHUNK
    k_chunk = lax.broadcasted_iota(I32, (DSA_KB, tq), 0) // CHUNK

    def score_block(c, carry):
        cs = pl.multiple_of(c * DSA_KB, DSA_KB)
        ik = kvi_ref[0, pl.ds(cs, DSA_KB), 128:192]
        sc = jnp.zeros((DSA_KB, tq), F32)
        for h in range(IDX_HEADS):
            lg = _dot_nt(ik, iq[:, h * HEAD_DIM:(h + 1) * HEAD_DIM])
            sc = sc + w_row[h] * jnp.maximum(lg, 0.0)
        admissible = (c * (DSA_KB // CHUNK) + k_chunk) <= q_chunk
        sc_ref[pl.ds(cs, DSA_KB), :] = jnp.where(admissible, sc, -jnp.inf)
        return carry

    lax.fori_loop(0, nblk, score_block, 0)

    def count(pred_fn):
        sub = DSA_SEG // 8

        def seg_body(s, a):
            base = pl.multiple_of(s * DSA_SEG, DSA_SEG)
            parts = [jnp.where(pred_fn(sc_ref[pl.ds(base + j * sub, sub), :]), 1.0, 0.0) for j in range(8)]
            return a + ((parts[0] + parts[1]) + (parts[2] + parts[3])) + ((parts[4] + parts[5]) + (parts[6] + parts[7]))
        a = lax.fori_loop(0, nseg, seg_body, jnp.zeros((sub, tq), F32))
        return jnp.sum(a, axis=0, keepdims=True)

    def key_to_float(key):
        return lax.bitcast_convert_type(jnp.where(key >= 0, key, key ^ 0x7FFFFFFF), F32)

    kf = jnp.float32(topk)
    n_rows = (nseg * DSA_SEG).astype(F32)
    t0 = jnp.where(count(lambda s: s >= 0.0) >= kf, 0, INT_MIN).astype(I32)

    def bit_step(i, t):
        cand = t + lax.shift_left(jnp.int32(1), 30 - i)
        cand_f = key_to_float(cand)
        cnt = jnp.where(cand <= KEY_NEG_INF, n_rows, count(lambda s: s >= cand_f))
        return jnp.where(cnt >= kf, cand, t)

    thr_key = lax.fori_loop(0, 31, bit_step, t0)
    thr = jnp.where(thr_key <= KEY_NEG_INF, -jnp.inf, key_to_float(thr_key))
    need = kf - count(lambda s: s > thr)
    tri = tri_ref[...]

    def select_block(c, prefix):
        cs = pl.multiple_of(c * DSA_KB, DSA_KB)
        sc = sc_ref[pl.ds(cs, DSA_KB), :]
        eqf = jnp.where(sc == thr, 1.0, 0.0)
        rank = _dot(tri, eqf.astype(BF16)) + prefix
        tie = jnp.where(rank < need, eqf, 0.0)
        sel = jnp.where(sc > thr, 1.0, tie)
        bias_ref[pl.ds(cs, DSA_KB), :] = jnp.where(sc > -jnp.inf, (sel - 1.0) * (-NEG_BIG), NEG_BIG)
        return prefix + jnp.sum(eqf, axis=0, keepdims=True)

    lax.fori_loop(0, nblk, select_block, jnp.zeros((1, tq), F32))

    q8 = dq_ref[0].reshape(DSA_HEADS * tq, HEAD_DIM)
    m_ref[...] = jnp.full_like(m_ref, NEG_BIG)
    acc_ref[...] = jnp.zeros_like(acc_ref)

    def attn_block(c, carry):
        cs = pl.multiple_of(c * DSA_KB, DSA_KB)
        k = kvi_ref[0, pl.ds(cs, DSA_KB), 0:HEAD_DIM]
        b = bias_ref[pl.ds(cs, DSA_KB), :]
        s = _dot_nt(k, q8) + jnp.concatenate([b] * DSA_HEADS, axis=1)
        m_old = m_ref[...]
        m_new = jnp.maximum(m_old, jnp.max(s, axis=0, keepdims=True))
        p = jnp.exp(s - m_new)
        alpha = jnp.exp(m_old - m_new)
        vt = vt_ref[0, :, pl.ds(cs, DSA_KB)]
        acc_ref[...] = alpha * acc_ref[...] + _dot(vt, p.astype(BF16))
        m_ref[...] = m_new
        return carry

    lax.fori_loop(0, nblk, attn_block, 0)
    for hd in range(DSA_HEADS):
        a = acc_ref[:, hd * tq:(hd + 1) * tq]
        o = (a / a[0:1, :]).T
        o_ref[0, :, hd * HEAD_DIM:(hd + 1) * HEAD_DIM] = o[:, HEAD_DIM:].astype(BF16)


def _dsa_attention(dq, iq, iw, kvi, vt, tq, topk):
    batch, _, seq, _ = dq.shape
    return pl.pallas_call(
        functools.partial(_dsa_kernel, tq=tq, topk=topk),
        grid=(batch, seq // tq),
        in_specs=[
            pl.BlockSpec((1, DSA_HEADS, tq, HEAD_DIM), lambda b, i: (b, 0, i, 0)),
            pl.BlockSpec((1, tq, 256), lambda b, i: (b, i, 0)),
            pl.BlockSpec((1, tq, 128), lambda b, i: (b, i, 0)),
            pl.BlockSpec((1, seq, 256), lambda b, i: (b, 0, 0)),
            pl.BlockSpec((1, 128, seq), lambda b, i: (b, 0, 0)),
            pl.BlockSpec((DSA_KB, DSA_KB), lambda b, i: (0, 0)),
        ],
        out_specs=pl.BlockSpec((1, tq, BRANCH_W), lambda b, i: (b, i, 0)),
        out_shape=jax.ShapeDtypeStruct((batch, seq, BRANCH_W), BF16),
        scratch_shapes=[pltpu.VMEM((seq, tq), F32), pltpu.VMEM((seq, tq), F32),
                        pltpu.VMEM((1, DSA_HEADS * tq), F32), pltpu.VMEM((128, DSA_HEADS * tq), F32)],
        compiler_params=_cparams(("parallel", "parallel")),
        name="dsa_attention",
    )(dq, iq.reshape(batch, seq, 256), iw.reshape(batch, seq, 128), kvi.reshape(batch, seq, 256), vt,
      _strict_tri(DSA_KB, lower=True))


def _memkv_kernel(m_ref, g_ref, w_ref, gk_ref, bd_ref, mk_ref, mv_ref):
    x = m_ref[...]
    ms = jnp.mean(x * x, axis=-1, keepdims=True)
    h = (x * lax.rsqrt(ms + EPS) * g_ref[...]).astype(BF16)
    mw = MEM_HEADS * MEM_HEAD_DIM
    k = _dot(h, w_ref[:, :mw])
    msk = _dot_split(k * k, bd_ref[...]) * (1.0 / MEM_HEAD_DIM)
    mk_ref[...] = (k * lax.rsqrt(msk + EPS) * gk_ref[...]).astype(BF16)
    mv_ref[...] = _dot(h, w_ref[:, mw:]).astype(BF16)


def _mem_kv(mem2d, g_mem, w_mem_kv, g_k_mem, tm):
    n = mem2d.shape[0]
    mw = MEM_HEADS * MEM_HEAD_DIM
    row = lambda i: (i, 0)
    const = lambda i: (0, 0)
    return pl.pallas_call(
        _memkv_kernel,
        grid=(n // tm,),
        in_specs=[pl.BlockSpec((tm, D_MODEL), row), pl.BlockSpec((1, D_MODEL), const),
                  pl.BlockSpec((D_MODEL, 2 * mw), const), pl.BlockSpec((1, mw), const),
                  pl.BlockSpec((mw, mw), const)],
        out_specs=[pl.BlockSpec((tm, mw), row), pl.BlockSpec((tm, mw), row)],
        out_shape=[jax.ShapeDtypeStruct((n, mw), BF16), jax.ShapeDtypeStruct((n, mw), BF16)],
        compiler_params=_cparams(("parallel",)),
        name="mem_kv",
    )(mem2d, g_mem[None, :], w_mem_kv.astype(BF16), jnp.tile(g_k_mem, MEM_HEADS)[None, :],
      _block_diag_ones(mw, MEM_HEAD_DIM))


def _memattn_kernel(q_ref, k_ref, v_ref, o_ref):
    for hd in range(MEM_HEADS):
        sl = slice(hd * MEM_HEAD_DIM, (hd + 1) * MEM_HEAD_DIM)
        s = _dot_nt(q_ref[0, :, sl], k_ref[0, :, sl]) * (MEM_HEAD_DIM ** -0.5)
        p = jnp.exp(s - jnp.max(s, axis=-1, keepdims=True))
        o = _dot(p.astype(BF16), v_ref[0, :, sl]) / jnp.sum(p, axis=-1, keepdims=True)
        o_ref[0, :, sl] = o.astype(BF16)


def _mem_attention(mq, mk, mv, tq):
    batch, seq, mw = mq.shape
    mlen = mk.shape[1]
    kspec = pl.BlockSpec((1, mlen, mw), lambda b, i: (b, 0, 0))
    return pl.pallas_call(
        _memattn_kernel,
        grid=(batch, seq // tq),
        in_specs=[pl.BlockSpec((1, tq, mw), lambda b, i: (b, i, 0)), kspec, kspec],
        out_specs=pl.BlockSpec((1, tq, mw), lambda b, i: (b, i, 0)),
        out_shape=jax.ShapeDtypeStruct((batch, seq, mw), BF16),
        compiler_params=_cparams(("parallel", "parallel")),
        name="mem_attention",
    )(mq, mk, mv)


def _merge_kernel(x_ref, g_ref, osb_ref, odsa_ref, omem_ref, wg_ref, bg_ref, wb_ref, wo_ref,
                  gf_ref, wrh_ref, wrl_ref, br_ref,
                  x2_ref, h2_ref, idx_ref, gate_ref):
    x = x_ref[...]
    ms = jnp.mean(x * x, axis=-1, keepdims=True)
    h = (x * lax.rsqrt(ms + EPS) * g_ref[...]).astype(BF16)
    merged = None
    for n, o_ref in enumerate((osb_ref, odsa_ref, omem_ref)):
        gate = jax.nn.sigmoid(_dot(h, wg_ref[n]) + bg_ref[n])
        term = gate * _dot(o_ref[...], wb_ref[n])
        merged = term if merged is None else merged + term
    x2 = x + _dot(merged.astype(BF16), wo_ref[...])
    x2_ref[...] = x2

    ms2 = jnp.mean(x2 * x2, axis=-1, keepdims=True)
    h2 = x2 * lax.rsqrt(ms2 + EPS) * gf_ref[...]
    _store_token_tiles(h2_ref, h2)
    hi, lo = _split_bf16(h2)
    logits = _dot(hi, wrh_ref[...]) + _dot(hi, wrl_ref[...]) + _dot(lo, wrh_ref[...]) + br_ref[...]

    lane = lax.broadcasted_iota(I32, logits.shape, 1)
    vals = logits
    top_v, top_i = [], []
    for _ in range(TOP_K):
        mx = jnp.max(vals, axis=-1, keepdims=True)
        ix = jnp.min(jnp.where(vals == mx, lane, N_EXPERTS), axis=-1, keepdims=True)
        top_v.append(mx)
        top_i.append(ix)
        vals = jnp.where(lane == ix, -jnp.inf, vals)
    ex = [jnp.exp(v - top_v[0]) for v in top_v]
    den = ex[0] + ex[1] + ex[2] + ex[3]
    for k in range(TOP_K):
        idx_ref[:, k:k + 1] = top_i[k]
        gate_ref[:, k:k + 1] = ex[k] / den


def _merge_route(x2d, g_mix, o_sb, o_dsa, o_mem, w_gate, b_gate, w_branch, w_out, g_ffn, w_router, b_router, tm):
    n = x2d.shape[0]
    row = lambda i: (i, 0)
    c2 = lambda i: (0, 0)
    c3 = lambda i: (0, 0, 0)
    wr_hi = w_router.astype(BF16)
    wr_lo = (w_router - wr_hi.astype(F32)).astype(BF16)
    return pl.pallas_call(
        _merge_kernel,
        grid=(n // tm,),
        in_specs=[pl.BlockSpec((tm, D_MODEL), row), pl.BlockSpec((1, D_MODEL), c2),
                  pl.BlockSpec((tm, BRANCH_W), row), pl.BlockSpec((tm, BRANCH_W), row),
                  pl.BlockSpec((tm, BRANCH_W), row),
                  pl.BlockSpec((3, D_MODEL, D_MODEL), c3), pl.BlockSpec((3, 1, D_MODEL), c3),
                  pl.BlockSpec((3, BRANCH_W, D_MODEL), c3), pl.BlockSpec((D_MODEL, D_MODEL), c2),
                  pl.BlockSpec((1, D_MODEL), c2), pl.BlockSpec((D_MODEL, N_EXPERTS), c2),
                  pl.BlockSpec((D_MODEL, N_EXPERTS), c2), pl.BlockSpec((1, N_EXPERTS), c2)],
        out_specs=[pl.BlockSpec((tm, D_MODEL), row), pl.BlockSpec((tm * TOKEN_TILE, LANES), row),
                   pl.BlockSpec((tm, TOP_K), row), pl.BlockSpec((tm, TOP_K), row)],
        out_shape=[jax.ShapeDtypeStruct((n, D_MODEL), F32), jax.ShapeDtypeStruct((n * TOKEN_TILE, LANES), F32),
                   jax.ShapeDtypeStruct((n, TOP_K), I32), jax.ShapeDtypeStruct((n, TOP_K), F32)],
        compiler_params=_cparams(("parallel",)),
        name="merge_route",
    )(x2d, g_mix[None, :], o_sb, o_dsa, o_mem, w_gate.astype(BF16), b_gate[:, None, :],
      w_branch.astype(BF16), w_out.astype(BF16), g_ffn[None, :], wr_hi, wr_lo, b_router[None, :])


def _rank_kernel(idx_ref, l_ref, rank_ref, cnt_ref, carry_ref):
    @pl.when(pl.program_id(0) == 0)
    def _():
        carry_ref[...] = jnp.zeros_like(carry_ref)

    idx = idx_ref[...]
    tm = idx.shape[0]
    lane = lax.broadcasted_iota(I32, (tm, LANES), 1)
    hits = [lane == idx[:, k:k + 1] for k in range(TOP_K)]
    onehot = sum(jnp.where(hk, 1.0, 0.0) for hk in hits)
    before = _dot(l_ref[...], onehot.astype(BF16)) + carry_ref[...]
    for k in range(TOP_K):
        rank_ref[:, k:k + 1] = jnp.sum(jnp.where(hits[k], before, 0.0), axis=-1, keepdims=True).astype(I32)
    carry_ref[...] += jnp.sum(onehot, axis=0, keepdims=True)
    cnt_ref[...] = carry_ref[...]


def _expert_rank(top_idx, tm):
    n = top_idx.shape[0]
    return pl.pallas_call(
        _rank_kernel,
        grid=(n // tm,),
        in_specs=[pl.BlockSpec((tm, TOP_K), lambda i: (i, 0)), pl.BlockSpec((tm, tm), lambda i: (0, 0))],
        out_specs=[pl.BlockSpec((tm, TOP_K), lambda i: (i, 0)), pl.BlockSpec((1, LANES), lambda i: (0, 0))],
        out_shape=[jax.ShapeDtypeStruct((n, TOP_K), I32), jax.ShapeDtypeStruct((1, LANES), F32)],
        scratch_shapes=[pltpu.VMEM((1, LANES), F32)],
        compiler_params=_cparams(("arbitrary",)),
        name="expert_rank",
    )(top_idx, _strict_tri(tm, lower=True))


DISPATCH_TOKENS = 256


def _dispatch_kernel(dest_ref, h_ref, xs_in_ref, xs_ref, sem):
    del xs_in_ref
    n_copies = DISPATCH_TOKENS * TOP_K

    def start(t, c):
        src = pl.multiple_of(t * TOKEN_TILE, TOKEN_TILE)
        for k in range(TOP_K):
            dst = pl.multiple_of(dest_ref[0, 0, t * TOP_K + k] * TOKEN_TILE, TOKEN_TILE)
            pltpu.make_async_copy(h_ref.at[pl.ds(src, TOKEN_TILE)], xs_ref.at[pl.ds(dst, TOKEN_TILE)], sem).start()
        return c

    lax.fori_loop(0, DISPATCH_TOKENS, start, 0, unroll=2)
    total = n_copies * TOKEN_TILE
    pltpu.make_async_copy(xs_ref.at[pl.ds(0, total)], xs_ref.at[pl.ds(0, total)], sem).wait()


def _dispatch(dest, h2t, n_slots):
    n = h2t.shape[0] // TOKEN_TILE
    per = DISPATCH_TOKENS * TOP_K
    return pl.pallas_call(
        _dispatch_kernel,
        grid=(n // DISPATCH_TOKENS,),
        in_specs=[pl.BlockSpec((1, 1, per), lambda i: (i, 0, 0), memory_space=pltpu.SMEM),
                  pl.BlockSpec((DISPATCH_TOKENS * TOKEN_TILE, LANES), lambda i: (i, 0)),
                  pl.BlockSpec(memory_space=pl.ANY)],
        out_specs=pl.BlockSpec(memory_space=pl.ANY),
        out_shape=jax.ShapeDtypeStruct((n_slots * TOKEN_TILE, LANES), F32),
        scratch_shapes=[pltpu.SemaphoreType.DMA(())],
        input_output_aliases={2: 0},
        compiler_params=_cparams(("arbitrary",)),
        name="moe_dispatch",
    )(dest.reshape(n // DISPATCH_TOKENS, 1, per), h2t, jnp.zeros((n_slots * TOKEN_TILE, LANES), F32))


def _expert_kernel(blk_e_ref, nused_ref, x_ref, w1_ref, b1_ref, w2_ref, b2_ref, y_ref, xb_ref):
    del blk_e_ref

    @pl.when(pl.program_id(0) < nused_ref[0])
    def _():
        for s in range(TOKEN_TILE):
            xb_ref[:, s * LANES:(s + 1) * LANES] = _load_token_tiles(x_ref, 0, MOE_BLOCK, s).astype(BF16)
        hb = _dot(xb_ref[...], w1_ref[0]) + b1_ref[0]
        g = jnp.minimum(hb[:, :D_EXPERT], SWIGLU_LIMIT)
        u = jnp.clip(hb[:, D_EXPERT:], -SWIGLU_LIMIT, SWIGLU_LIMIT)
        act = (u + 1.0) * (g * jax.nn.sigmoid(SWIGLU_ALPHA * g))
        _store_token_tiles(y_ref, _dot(act.astype(BF16), w2_ref[0]) + b2_ref[0])

    @pl.when(pl.program_id(0) >= nused_ref[0])
    def _():
        y_ref[...] = jnp.zeros_like(y_ref)


def _experts(blk_e, n_used, xs, w_e_in, b_e_in, w_e_out, b_e_out):
    n_slots = xs.shape[0] // TOKEN_TILE
    nblk = n_slots // MOE_BLOCK
    slot_block = pl.BlockSpec((MOE_BLOCK * TOKEN_TILE, LANES), lambda i, be, nu: (i, 0))
    grid_spec = pltpu.PrefetchScalarGridSpec(
        num_scalar_prefetch=2,
        grid=(nblk,),
        in_specs=[slot_block,
                  pl.BlockSpec((1, D_MODEL, 2 * D_EXPERT), lambda i, be, nu: (be[i], 0, 0)),
                  pl.BlockSpec((1, 1, 2 * D_EXPERT), lambda i, be, nu: (be[i], 0, 0)),
                  pl.BlockSpec((1, D_EXPERT, D_MODEL), lambda i, be, nu: (be[i], 0, 0)),
                  pl.BlockSpec((1, 1, D_MODEL), lambda i, be, nu: (be[i], 0, 0))],
        out_specs=slot_block,
        scratch_shapes=[pltpu.VMEM((MOE_BLOCK, D_MODEL), BF16)],
    )
    return pl.pallas_call(
        _expert_kernel,
        grid_spec=grid_spec,
        out_shape=jax.ShapeDtypeStruct((n_slots * TOKEN_TILE, LANES), F32),
        compiler_params=_cparams(("arbitrary",)),
        name="moe_experts",
    )(blk_e, n_used, xs, w_e_in.astype(BF16), b_e_in[:, None, :], w_e_out.astype(BF16), b_e_out[:, None, :])


def _combine_kernel(dest_ref, x_ref, gate_ref, y_ref, o_ref, buf_ref, sem):
    n_copies = DISPATCH_TOKENS * TOP_K

    def start(t, c):
        for k in range(TOP_K):
            src = pl.multiple_of(dest_ref[0, 0, t * TOP_K + k] * TOKEN_TILE, TOKEN_TILE)
            dst = pl.multiple_of((k * DISPATCH_TOKENS + t) * TOKEN_TILE, TOKEN_TILE)
            pltpu.make_async_copy(y_ref.at[pl.ds(src, TOKEN_TILE)], buf_ref.at[pl.ds(dst, TOKEN_TILE)], sem).start()
        return c

    lax.fori_loop(0, DISPATCH_TOKENS, start, 0, unroll=2)
    pltpu.make_async_copy(y_ref.at[pl.ds(0, n_copies * TOKEN_TILE)], buf_ref, sem).wait()
    gate = gate_ref[...]
    gates = [jnp.broadcast_to(gate[:, k:k + 1], (DISPATCH_TOKENS, LANES)) for k in range(TOP_K)]
    for s in range(TOKEN_TILE):
        out = x_ref[:, s * LANES:(s + 1) * LANES]
        for k in range(TOP_K):
            out = out + gates[k] * _load_token_tiles(buf_ref, k * DISPATCH_TOKENS, DISPATCH_TOKENS, s)
        o_ref[:, s * LANES:(s + 1) * LANES] = out


def _combine(dest, x2, gate, ys):
    n = x2.shape[0]
    per = DISPATCH_TOKENS * TOP_K
    return pl.pallas_call(
        _combine_kernel,
        grid=(n // DISPATCH_TOKENS,),
        in_specs=[pl.BlockSpec((1, 1, per), lambda i: (i, 0, 0), memory_space=pltpu.SMEM),
                  pl.BlockSpec((DISPATCH_TOKENS, D_MODEL), lambda i: (i, 0)),
                  pl.BlockSpec((DISPATCH_TOKENS, TOP_K), lambda i: (i, 0)),
                  pl.BlockSpec(memory_space=pl.ANY)],
        out_specs=pl.BlockSpec((DISPATCH_TOKENS, D_MODEL), lambda i: (i, 0)),
        out_shape=jax.ShapeDtypeStruct((n, D_MODEL), F32),
        scratch_shapes=[pltpu.VMEM((TOP_K * DISPATCH_TOKENS * TOKEN_TILE, LANES), F32),
                        pltpu.SemaphoreType.DMA(())],
        compiler_params=_cparams(("arbitrary",)),
        name="moe_combine",
    )(dest.reshape(n // DISPATCH_TOKENS, 1, per), x2, gate, ys)


def _moe(x2, h2, top_idx, gate, w_e_in, b_e_in, w_e_out, b_e_out):
    n = x2.shape[0]
    rank, counts = _expert_rank(top_idx, 512)
    counts = counts[0, :N_EXPERTS].astype(I32)
    padded = (counts + MOE_BLOCK - 1) // MOE_BLOCK * MOE_BLOCK
    pend = jnp.cumsum(padded)
    pstart = pend - padded
    nblk = -(-(n * TOP_K) // MOE_BLOCK) + N_EXPERTS
    blk_start = jnp.arange(nblk, dtype=I32) * MOE_BLOCK
    blk_e = jnp.minimum(jnp.sum((pend[None, :] <= blk_start[:, None]).astype(I32), axis=1), N_EXPERTS - 1)
    n_used = (pend[-1:] // MOE_BLOCK).astype(I32)
    onehot = top_idx[:, :, None] == jnp.arange(N_EXPERTS, dtype=I32)[None, None, :]
    dest = rank + jnp.sum(jnp.where(onehot, pstart[None, None, :], 0), axis=-1)
    xs = _dispatch(dest, h2, nblk * MOE_BLOCK)
    ys = _experts(blk_e, n_used, xs, w_e_in, b_e_in, w_e_out, b_e_out)
    return _combine(dest, x2, gate, ys)


def _layer(x, mem, g_mix, w_in, g_q_dsa, g_k_dsa, g_q_mem, g_k_mem, g_mem, w_mem_kv, w_gate, b_gate,
           w_branch, w_out, g_ffn, w_router, b_router, w_e_in, b_e_in, w_e_out, b_e_out):
    batch, seq, _ = x.shape
    n = batch * seq
    topk = min(DSA_TOPK_MAX, seq // 4)
    x2d = x.reshape(n, D_MODEL)
    tm = min(512, seq)
    sq, sk, sv, dq, iq, kvi, iw, mq, vt = _inproj(x2d, batch, seq, g_mix, w_in, g_q_dsa, g_k_dsa, g_q_mem, tm)
    o_sb = _sb_attention(sq, sk, sv, min(256, seq))
    o_dsa = _dsa_attention(dq, iq, iw, kvi, vt, 128, topk)
    mlen = mem.shape[1]
    mk, mv = _mem_kv(mem.reshape(batch * mlen, D_MODEL), g_mem, w_mem_kv, g_k_mem, min(512, batch * mlen))
    mw = MEM_HEADS * MEM_HEAD_DIM
    o_mem = _mem_attention(mq.reshape(batch, seq, mw), mk.reshape(batch, mlen, mw), mv.reshape(batch, mlen, mw), tm)
    x2, h2, top_idx, gate = _merge_route(
        x2d, g_mix, o_sb.reshape(n, BRANCH_W), o_dsa.reshape(n, BRANCH_W), o_mem.reshape(n, mw),
        w_gate, b_gate, w_branch, w_out, g_ffn, w_router, b_router, tm)
    out = _moe(x2, h2, top_idx, gate, w_e_in, b_e_in, w_e_out, b_e_out)
    return out.reshape(batch, seq, D_MODEL)


def kernel(x, mem, g_mix, w_in, g_q_dsa, g_k_dsa, g_q_mem, g_k_mem, g_mem, w_mem_kv, w_gate, b_gate, w_branch, w_out, g_ffn, w_router, b_router, w_e_in, b_e_in, w_e_out, b_e_out):
    for l in range(g_mix.shape[0]):
        x = _layer(x, mem, g_mix[l], w_in[l], g_q_dsa[l], g_k_dsa[l], g_q_mem[l], g_k_mem[l], g_mem[l],
                   w_mem_kv[l], w_gate[l], b_gate[l], w_branch[l], w_out[l], g_ffn[l], w_router[l],
                   b_router[l], w_e_in[l], b_e_in[l], w_e_out[l], b_e_out[l])
    return x
```

```python
import functools

import numpy as np
import jax
import jax.numpy as jnp
from jax import lax
from jax.experimental import pallas as pl
from jax.experimental.pallas import tpu as pltpu

F32 = jnp.float32
BF16 = jnp.bfloat16
I32 = jnp.int32

D_MODEL = 1024
CHUNK = 64
SB_HEADS = 8
DSA_HEADS = 8
HEAD_DIM = 64
IDX_HEADS = 4
DSA_TOPK_MAX = 256
MEM_HEADS = 4
MEM_HEAD_DIM = 128
N_EXPERTS = 32
TOP_K = 4
D_EXPERT = D_MODEL
SWIGLU_LIMIT = 7.0
SWIGLU_ALPHA = 1.702
ROPE_THETA = 10000.0
EPS = 1e-6
MOE_BLOCK = 512

BRANCH_W = 512
IN_SIZES = (512, 512, 512, 512, 64, 64, 256, 64, 4, 512)
C_SQ, C_SK, C_SV, C_DQ, C_MQ, C_IQ, C_SMALL, C_END = 0, 512, 1024, 1536, 2048, 2560, 2816, 3072
IW_LANE = 64

LANES = 128
NEG_BIG = -1e30
SB_CUTOFF = 110.0
KEY_NEG_INF = int(np.array(-np.inf, np.float32).view(np.int32)) ^ 0x7FFFFFFF
INT_MIN = -(2 ** 31)

VMEM_LIMIT = 56 * 1024 * 1024


def _cparams(sem):
    return pltpu.CompilerParams(dimension_semantics=sem, vmem_limit_bytes=VMEM_LIMIT)


def _dot(a, b):
    return jnp.dot(a, b, preferred_element_type=F32)


def _dot_nt(a, b):
    return lax.dot_general(a, b, (((1,), (1,)), ((), ())), preferred_element_type=F32)


def _split_bf16(x):
    hi = x.astype(BF16)
    lo = (x - hi.astype(F32)).astype(BF16)
    return hi, lo


def _dot_split(x, m_bf16):
    hi, lo = _split_bf16(x)
    return _dot(hi, m_bf16) + _dot(lo, m_bf16)


TOKEN_TILE = D_MODEL // LANES


def _store_token_tiles(ref, y):
    rows = y.shape[0]
    for s in range(TOKEN_TILE):
        ref[pl.ds(s, rows, stride=TOKEN_TILE), :] = y[:, s * LANES:(s + 1) * LANES]


def _load_token_tiles(ref, start_row, rows, s):
    return ref[pl.ds(start_row * TOKEN_TILE + s, rows, stride=TOKEN_TILE), :]


def _rot_half_unsigned(y):
    w = y.shape[1]
    lane = lax.broadcasted_iota(I32, y.shape, 1)
    return jnp.where((lane & 32) == 0, pltpu.roll(y, w - 32, 1), pltpu.roll(y, 32, 1))


def _inproj_kernel(x_ref, g_ref, w_ref, wsvt_ref, cos_ref, sin_ref, coss_ref, sins_ref, gq_ref, gks_ref,
                   gm_ref, bd64_ref, bd128_ref,
                   sq_ref, sk_ref, sv_ref, dq_ref, iq_ref, kvi_ref, iw_ref, mq_ref, vt_ref):
    x = x_ref[...]
    ms = jnp.mean(x * x, axis=-1, keepdims=True)
    h = (x * lax.rsqrt(ms + EPS) * g_ref[...]).astype(BF16)

    def seg(a, b):
        return _dot(h, w_ref[:, a:b])

    def put_heads(ref, y):
        for hd in range(y.shape[1] // HEAD_DIM):
            ref[0, hd] = y[:, hd * HEAD_DIM:(hd + 1) * HEAD_DIM].astype(BF16)

    put_heads(sq_ref, seg(C_SQ, C_SK) * (HEAD_DIM ** -0.5))
    put_heads(sk_ref, seg(C_SK, C_SV))
    sv_ref[0] = _dot_nt(wsvt_ref[...], h).astype(BF16)

    y = seg(C_DQ, C_MQ)
    msq = _dot_split(y * y, bd64_ref[...]) * (1.0 / HEAD_DIM)
    y = y * lax.rsqrt(msq + EPS) * gq_ref[...]
    y = y * cos_ref[...] + _rot_half_unsigned(y) * sin_ref[...]
    put_heads(dq_ref, y * (HEAD_DIM ** -0.5))

    y = seg(C_IQ, C_SMALL)
    y = y * cos_ref[:, :256] + _rot_half_unsigned(y) * sin_ref[:, :256]
    iq_ref[...] = (y * (HEAD_DIM ** -0.5)).astype(BF16)

    y = seg(C_SMALL, C_END)
    lane = lax.broadcasted_iota(I32, y.shape, 1)
    is_k = lane < HEAD_DIM
    msk = jnp.sum(jnp.where(is_k, y * y, 0.0), axis=-1, keepdims=True) * (1.0 / HEAD_DIM)
    y = y * jnp.where(is_k, lax.rsqrt(msk + EPS) * gks_ref[...], 1.0)
    y = y * coss_ref[...] + _rot_half_unsigned(y) * sins_ref[...]
    kvi_ref[...] = y.astype(BF16)
    iw_ref[...] = y[:, 128:256]
    kv_t = y[:, 0:128].T
    row_t = lax.broadcasted_iota(I32, kv_t.shape, 0)
    vt_ref[0] = jnp.where(row_t < HEAD_DIM, 1.0, kv_t).astype(BF16)

    y = seg(C_MQ, C_IQ)
    msm = _dot_split(y * y, bd128_ref[...]) * (1.0 / MEM_HEAD_DIM)
    mq_ref[...] = (y * lax.rsqrt(msm + EPS) * gm_ref[...]).astype(BF16)


def _rope_tables(seq):
    half = HEAD_DIM // 2
    inv = ROPE_THETA ** (-jnp.arange(half, dtype=F32) / half)
    ang = jnp.arange(seq).astype(F32)[:, None] * inv[None, :]
    cos = jnp.cos(ang)
    sin = jnp.sin(ang)
    cos64 = jnp.concatenate([cos, cos], axis=1)
    sin64 = jnp.concatenate([-sin, sin], axis=1)
    one = jnp.ones_like(cos64)
    zero = jnp.zeros_like(cos64)
    cosq = jnp.tile(cos64, (1, 8))
    sinq = jnp.tile(sin64, (1, 8))
    coss = jnp.concatenate([cos64, one, cos64, one], axis=1)
    sins = jnp.concatenate([sin64, zero, sin64, zero], axis=1)
    return cosq, sinq, coss, sins


def _block_diag_ones(width, group):
    idx = np.arange(width) // group
    return jnp.asarray((idx[:, None] == idx[None, :]).astype(np.float32), dtype=BF16)


def _inproj(x2d, batch, seq, g_mix, w_in, g_q_dsa, g_k_dsa, g_q_mem, tm):
    n = x2d.shape[0]
    sizes = np.cumsum((0,) + IN_SIZES)
    col = {name: (int(sizes[i]), int(sizes[i + 1])) for i, name in enumerate(
        ("sq", "sk", "sv", "dq", "dk", "dv", "iq", "ik", "iw", "mq"))}
    order = ("sq", "sk", "sv", "dq", "mq", "iq", "dk", "dv", "ik", "iw")
    w = jnp.concatenate([w_in[:, col[k][0]:col[k][1]] for k in order]
                        + [jnp.zeros((D_MODEL, C_END - sum(IN_SIZES)), w_in.dtype)], axis=1).astype(BF16)
    cosq, sinq, coss, sins = (jnp.asarray(t) for t in _rope_tables(seq))
    gq = jnp.tile(g_q_dsa, 8)[None, :]
    gks = jnp.concatenate([g_k_dsa, jnp.ones((256 - HEAD_DIM,), F32)])[None, :]
    gm = jnp.tile(g_q_mem, MEM_HEADS)[None, :]
    spb = seq // tm
    row = lambda i: (i, 0)
    const = lambda i: (0, 0)
    pos = lambda i: (i % spb, 0)
    heads = lambda i: (i // spb, 0, i % spb, 0)
    head_shape = jax.ShapeDtypeStruct((batch, 8, seq, HEAD_DIM), BF16)
    head_spec = pl.BlockSpec((1, 8, tm, HEAD_DIM), heads)
    return pl.pallas_call(
        _inproj_kernel,
        grid=(n // tm,),
        in_specs=[
            pl.BlockSpec((tm, D_MODEL), row),
            pl.BlockSpec((1, D_MODEL), const),
            pl.BlockSpec((D_MODEL, C_END), const),
            pl.BlockSpec((512, D_MODEL), const),
            pl.BlockSpec((tm, 512), pos), pl.BlockSpec((tm, 512), pos),
            pl.BlockSpec((tm, 256), pos), pl.BlockSpec((tm, 256), pos),
            pl.BlockSpec((1, 512), const), pl.BlockSpec((1, 256), const), pl.BlockSpec((1, 512), const),
            pl.BlockSpec((512, 512), const), pl.BlockSpec((512, 512), const),
        ],
        out_specs=[head_spec, head_spec, pl.BlockSpec((1, 512, tm), lambda i: (i // spb, 0, i % spb)), head_spec,
                   pl.BlockSpec((tm, 256), row), pl.BlockSpec((tm, 256), row),
                   pl.BlockSpec((tm, 128), row), pl.BlockSpec((tm, 512), row),
                   pl.BlockSpec((1, 128, tm), lambda i: (i // spb, 0, i % spb))],
        out_shape=[head_shape, head_shape, jax.ShapeDtypeStruct((batch, 512, seq), BF16), head_shape,
                   jax.ShapeDtypeStruct((n, 256), BF16), jax.ShapeDtypeStruct((n, 256), BF16),
                   jax.ShapeDtypeStruct((n, 128), F32), jax.ShapeDtypeStruct((n, 512), BF16),
                   jax.ShapeDtypeStruct((batch, 128, seq), BF16)],
        compiler_params=_cparams(("parallel",)),
        name="inproj",
    )(x2d, g_mix[None, :], w, w_in[:, col["sv"][0]:col["sv"][1]].T.astype(BF16), cosq, sinq, coss, sins, gq, gks, gm,
      _block_diag_ones(512, HEAD_DIM), _block_diag_ones(512, MEM_HEAD_DIM))


def _sb_kernel(q_ref, k_ref, vt_ref, u_ref, o_ref, acc_ref, car_ref, *, tq):
    qi = pl.program_id(1)
    rows = lax.broadcasted_iota(I32, (tq, tq), 0)
    cols = lax.broadcasted_iota(I32, (tq, tq), 1)
    dif = rows - cols
    u = u_ref[...]
    acc_ref[...] = jnp.zeros_like(acc_ref)
    car_ref[...] = jnp.zeros_like(car_ref)

    def cond(c):
        kb, mx = c
        return jnp.logical_and(kb >= 0, mx > -SB_CUTOFF)

    def body(c):
        kb, _ = c
        ks = pl.multiple_of(kb * tq, tq)
        earlier = dif < (qi - kb) * tq
        heads = range(SB_HEADS)
        z = [_dot_nt(k_ref[0, hd, pl.ds(ks, tq), :], q_ref[0, hd]) for hd in heads]
        ls, lk, between = [], [], []
        for hd in heads:
            sp = jnp.maximum(z[hd], 0.0) + jnp.log(1.0 + jnp.exp(-jnp.abs(z[hd])))
            ls.append(z[hd] - sp)
            lk.append(jnp.where(earlier, -sp, 0.0))
            hi, lo = _split_bf16(lk[hd])
            between.append(_dot(u, hi) + _dot(u, lo))
        for hd in heads:
            rs = slice(hd * HEAD_DIM, (hd + 1) * HEAD_DIM)
            car = car_ref[hd:hd + 1, :]
            w = jnp.where(earlier, jnp.exp(ls[hd] + between[hd] + car), 0.0)
            acc_ref[rs, :] += _dot(vt_ref[0, rs, pl.ds(ks, tq)], w.astype(BF16))
            car_ref[hd:hd + 1, :] = car + (between[hd][0:1, :] + lk[hd][0:1, :])
        return kb - 1, jnp.max(car_ref[...])

    lax.while_loop(cond, body, (qi, jnp.float32(0.0)))
    o_ref[0] = acc_ref[...].T.astype(BF16)


def _strict_tri(n, lower):
    i = np.arange(n)
    m = (i[:, None] > i[None, :]) if lower else (i[:, None] < i[None, :])
    return jnp.asarray(m.astype(np.float32), dtype=BF16)


def _sb_attention(sq, sk, svt, tq):
    batch, _, seq, _ = sq.shape
    return pl.pallas_call(
        functools.partial(_sb_kernel, tq=tq),
        grid=(batch, seq // tq),
        in_specs=[pl.BlockSpec((1, SB_HEADS, tq, HEAD_DIM), lambda b, i: (b, 0, i, 0)),
                  pl.BlockSpec((1, SB_HEADS, seq, HEAD_DIM), lambda b, i: (b, 0, 0, 0)),
                  pl.BlockSpec((1, BRANCH_W, seq), lambda b, i: (b, 0, 0)),
                  pl.BlockSpec((tq, tq), lambda b, i: (0, 0))],
        out_specs=pl.BlockSpec((1, tq, BRANCH_W), lambda b, i: (b, i, 0)),
        out_shape=jax.ShapeDtypeStruct((batch, seq, BRANCH_W), BF16),
        scratch_shapes=[pltpu.VMEM((BRANCH_W, tq), F32), pltpu.VMEM((SB_HEADS, tq), F32)],
        compiler_params=_cparams(("parallel", "parallel")),
        name="sb_attention",
    )(sq, sk, svt, _strict_tri(tq, lower=False))


DSA_SEG = 512
DSA_KB = 512
DSA_GROUP_HEADS = 8


def _dsa_kernel(dq_ref, iq_ref, iw_ref, kvi_ref, vt_ref, tri_ref, o_ref,
                sc_ref, bias_ref, m_ref, acc_ref, *, tq, topk):
    qi = pl.program_id(1)
    qs = qi * tq
    nseg = qs // DSA_SEG + 1
    nblk = nseg * (DSA_SEG // DSA_KB)
    iq = iq_ref[0]
    w_t = iw_ref[0].T
    w_row = [w_t[IW_LANE + h:IW_LANE + h + 1, :] * (IDX_HEADS ** -0.5) for h in range(IDX_HEADS)]
    q_chunk = (qs + lax.broadcasted_iota(I32, (DSA_KB, tq), 1)) // CHUNK
    k_chunk = lax.broadcasted_iota(I32, (DSA_KB, tq), 0) // CHUNK

    def score_block(c, carry):
        cs = pl.multiple_of(c * DSA_KB, DSA_KB)
        ik = kvi_ref[0, pl.ds(cs, DSA_KB), 128:192]
        lg = [_dot_nt(ik, iq[:, h * HEAD_DIM:(h + 1) * HEAD_DIM]) for h in range(IDX_HEADS)]
        sc = jnp.zeros((DSA_KB, tq), F32)
        for h in range(IDX_HEADS):
            sc = sc + w_row[h] * jnp.maximum(lg[h], 0.0)
        admissible = (c * (DSA_KB // CHUNK) + k_chunk) <= q_chunk
        sc_ref[pl.ds(cs, DSA_KB), :] = jnp.where(admissible, sc, -jnp.inf)
        return carry

    lax.fori_loop(0, nblk, score_block, 0)

    def count(pred_fn):
        sub = DSA_SEG // 8

        def seg_body(s, a):
            base = pl.multiple_of(s * DSA_SEG, DSA_SEG)
            parts = [jnp.where(pred_fn(sc_ref[pl.ds(base + j * sub, sub), :]), 1.0, 0.0) for j in range(8)]
            return a + ((parts[0] + parts[1]) + (parts[2] + parts[3])) + ((parts[4] + parts[5]) + (parts[6] + parts[7]))
        a = lax.fori_loop(0, nseg, seg_body, jnp.zeros((sub, tq), F32))
        return jnp.sum(a, axis=0, keepdims=True)

    def key_to_float(key):
        return lax.bitcast_convert_type(jnp.where(key >= 0, key, key ^ 0x7FFFFFFF), F32)

    kf = jnp.float32(topk)
    n_rows = (nseg * DSA_SEG).astype(F32)
    t0 = jnp.where(count(lambda s: s >= 0.0) >= kf, 0, INT_MIN).astype(I32)

    def bit_step(i, t):
        cand = t + lax.shift_left(jnp.int32(1), 30 - i)
        cand_f = key_to_float(cand)
        cnt = jnp.where(cand <= KEY_NEG_INF, n_rows, count(lambda s: s >= cand_f))
        return jnp.where(cnt >= kf, cand, t)

    thr_key = lax.fori_loop(0, 31, bit_step, t0)
    thr = jnp.where(thr_key <= KEY_NEG_INF, -jnp.inf, key_to_float(thr_key))
    need = kf - count(lambda s: s > thr)
    tri = tri_ref[...]

    def select_block(c, prefix):
        cs = pl.multiple_of(c * DSA_KB, DSA_KB)
        sc = sc_ref[pl.ds(cs, DSA_KB), :]
        eqf = jnp.where(sc == thr, 1.0, 0.0)
        rank = _dot(tri, eqf.astype(BF16)) + prefix
        tie = jnp.where(rank < need, eqf, 0.0)
        sel = jnp.where(sc > thr, 1.0, tie)
        bias_ref[pl.ds(cs, DSA_KB), :] = jnp.where(sc > -jnp.inf, (sel - 1.0) * (-NEG_BIG), NEG_BIG)
        return prefix + jnp.sum(eqf, axis=0, keepdims=True)

    def select_block_no_ties(c, carry):
        cs = pl.multiple_of(c * DSA_KB, DSA_KB)
        bias_ref[pl.ds(cs, DSA_KB), :] = jnp.where(sc_ref[pl.ds(cs, DSA_KB), :] >= thr, 0.0, NEG_BIG)
        return carry

    surplus = jnp.max(count(lambda s: s >= thr)) > kf

    @pl.when(surplus)
    def _():
        lax.fori_loop(0, nblk, select_block, jnp.zeros((1, tq), F32))

    @pl.when(jnp.logical_not(surplus))
    def _():
        lax.fori_loop(0, nblk, select_block_no_ties, 0)

    q8 = dq_ref[0].reshape(DSA_HEADS * tq, HEAD_DIM)
    m_ref[...] = jnp.full_like(m_ref, NEG_BIG)
    acc_ref[...] = jnp.zeros_like(acc_ref)

    group = DSA_GROUP_HEADS * tq
    n_groups = DSA_HEADS * tq // group

    def attn_block(c, carry):
        cs = pl.multiple_of(c * DSA_KB, DSA_KB)
        k = kvi_ref[0, pl.ds(cs, DSA_KB), 0:HEAD_DIM]
        b = bias_ref[pl.ds(cs, DSA_KB), :]
        b2 = jnp.concatenate([b] * DSA_GROUP_HEADS, axis=1)
        vt = vt_ref[0, :, pl.ds(cs, DSA_KB)]
        s = [_dot_nt(k, q8[g * group:(g + 1) * group]) for g in range(n_groups)]
        for g in range(n_groups):
            cols = slice(g * group, (g + 1) * group)
            sg = s[g] + b2
            m_old = m_ref[:, cols]
            m_new = jnp.maximum(m_old, jnp.max(sg, axis=0, keepdims=True))
            p = jnp.exp(sg - m_new)
            alpha = jnp.exp(m_old - m_new)
            acc_ref[:, cols] = alpha * acc_ref[:, cols] + _dot(vt, p.astype(BF16))
            m_ref[:, cols] = m_new
        return carry

    lax.fori_loop(0, nblk, attn_block, 0)
    for hd in range(DSA_HEADS):
        a = acc_ref[:, hd * tq:(hd + 1) * tq]
        o = (a / a[0:1, :]).T
        o_ref[0, :, hd * HEAD_DIM:(hd + 1) * HEAD_DIM] = o[:, HEAD_DIM:].astype(BF16)


def _dsa_attention(dq, iq, iw, kvi, vt, tq, topk):
    batch, _, seq, _ = dq.shape
    return pl.pallas_call(
        functools.partial(_dsa_kernel, tq=tq, topk=topk),
        grid=(batch, seq // tq),
        in_specs=[
            pl.BlockSpec((1, DSA_HEADS, tq, HEAD_DIM), lambda b, i: (b, 0, i, 0)),
            pl.BlockSpec((1, tq, 256), lambda b, i: (b, i, 0)),
            pl.BlockSpec((1, tq, 128), lambda b, i: (b, i, 0)),
            pl.BlockSpec((1, seq, 256), lambda b, i: (b, 0, 0)),
            pl.BlockSpec((1, 128, seq), lambda b, i: (b, 0, 0)),
            pl.BlockSpec((DSA_KB, DSA_KB), lambda b, i: (0, 0)),
        ],
        out_specs=pl.BlockSpec((1, tq, BRANCH_W), lambda b, i: (b, i, 0)),
        out_shape=jax.ShapeDtypeStruct((batch, seq, BRANCH_W), BF16),
        scratch_shapes=[pltpu.VMEM((seq, tq), F32), pltpu.VMEM((seq, tq), F32),
                        pltpu.VMEM((1, DSA_HEADS * tq), F32), pltpu.VMEM((128, DSA_HEADS * tq), F32)],
        compiler_params=_cparams(("parallel", "parallel")),
        name="dsa_attention",
    )(dq, iq.reshape(batch, seq, 256), iw.reshape(batch, seq, 128), kvi.reshape(batch, seq, 256), vt,
      _strict_tri(DSA_KB, lower=True))


def _memkv_kernel(m_ref, g_ref, w_ref, gk_ref, bd_ref, mk_ref, mv_ref):
    x = m_ref[...]
    ms = jnp.mean(x * x, axis=-1, keepdims=True)
    h = (x * lax.rsqrt(ms + EPS) * g_ref[...]).astype(BF16)
    mw = MEM_HEADS * MEM_HEAD_DIM
    k = _dot(h, w_ref[:, :mw])
    msk = _dot_split(k * k, bd_ref[...]) * (1.0 / MEM_HEAD_DIM)
    mk_ref[...] = (k * lax.rsqrt(msk + EPS) * gk_ref[...]).astype(BF16)
    mv_ref[...] = _dot(h, w_ref[:, mw:]).astype(BF16)


def _mem_kv(mem2d, g_mem, w_mem_kv, g_k_mem, tm):
    n = mem2d.shape[0]
    mw = MEM_HEADS * MEM_HEAD_DIM
    row = lambda i: (i, 0)
    const = lambda i: (0, 0)
    return pl.pallas_call(
        _memkv_kernel,
        grid=(n // tm,),
        in_specs=[pl.BlockSpec((tm, D_MODEL), row), pl.BlockSpec((1, D_MODEL), const),
                  pl.BlockSpec((D_MODEL, 2 * mw), const), pl.BlockSpec((1, mw), const),
                  pl.BlockSpec((mw, mw), const)],
        out_specs=[pl.BlockSpec((tm, mw), row), pl.BlockSpec((tm, mw), row)],
        out_shape=[jax.ShapeDtypeStruct((n, mw), BF16), jax.ShapeDtypeStruct((n, mw), BF16)],
        compiler_params=_cparams(("parallel",)),
        name="mem_kv",
    )(mem2d, g_mem[None, :], w_mem_kv.astype(BF16), jnp.tile(g_k_mem, MEM_HEADS)[None, :],
      _block_diag_ones(mw, MEM_HEAD_DIM))


def _memattn_kernel(q_ref, k_ref, v_ref, o_ref):
    for hd in range(MEM_HEADS):
        sl = slice(hd * MEM_HEAD_DIM, (hd + 1) * MEM_HEAD_DIM)
        s = _dot_nt(q_ref[0, :, sl], k_ref[0, :, sl]) * (MEM_HEAD_DIM ** -0.5)
        p = jnp.exp(s - jnp.max(s, axis=-1, keepdims=True))
        o = _dot(p.astype(BF16), v_ref[0, :, sl]) / jnp.sum(p, axis=-1, keepdims=True)
        o_ref[0, :, sl] = o.astype(BF16)


def _mem_attention(mq, mk, mv, tq):
    batch, seq, mw = mq.shape
    mlen = mk.shape[1]
    kspec = pl.BlockSpec((1, mlen, mw), lambda b, i: (b, 0, 0))
    return pl.pallas_call(
        _memattn_kernel,
        grid=(batch, seq // tq),
        in_specs=[pl.BlockSpec((1, tq, mw), lambda b, i: (b, i, 0)), kspec, kspec],
        out_specs=pl.BlockSpec((1, tq, mw), lambda b, i: (b, i, 0)),
        out_shape=jax.ShapeDtypeStruct((batch, seq, mw), BF16),
        compiler_params=_cparams(("parallel", "parallel")),
        name="mem_attention",
    )(mq, mk, mv)


def _merge_kernel(x_ref, g_ref, osb_ref, odsa_ref, omem_ref, wg_ref, bg_ref, wb_ref, wo_ref,
                  gf_ref, wrh_ref, wrl_ref, br_ref,
                  x2_ref, h2_ref, idx_ref, gate_ref):
    x = x_ref[...]
    ms = jnp.mean(x * x, axis=-1, keepdims=True)
    h = (x * lax.rsqrt(ms + EPS) * g_ref[...]).astype(BF16)
    merged = None
    for n, o_ref in enumerate((osb_ref, odsa_ref, omem_ref)):
        gate = jax.nn.sigmoid(_dot(h, wg_ref[n]) + bg_ref[n])
        term = gate * _dot(o_ref[...], wb_ref[n])
        merged = term if merged is None else merged + term
    x2 = x + _dot(merged.astype(BF16), wo_ref[...])
    x2_ref[...] = x2

    ms2 = jnp.mean(x2 * x2, axis=-1, keepdims=True)
    h2 = x2 * lax.rsqrt(ms2 + EPS) * gf_ref[...]
    _store_token_tiles(h2_ref, h2)
    hi, lo = _split_bf16(h2)
    logits = _dot(hi, wrh_ref[...]) + _dot(hi, wrl_ref[...]) + _dot(lo, wrh_ref[...]) + br_ref[...]

    lane = lax.broadcasted_iota(I32, logits.shape, 1)
    vals = logits
    top_v, top_i = [], []
    for _ in range(TOP_K):
        mx = jnp.max(vals, axis=-1, keepdims=True)
        ix = jnp.min(jnp.where(vals == mx, lane, N_EXPERTS), axis=-1, keepdims=True)
        top_v.append(mx)
        top_i.append(ix)
        vals = jnp.where(lane == ix, -jnp.inf, vals)
    ex = [jnp.exp(v - top_v[0]) for v in top_v]
    den = ex[0] + ex[1] + ex[2] + ex[3]
    for k in range(TOP_K):
        idx_ref[:, k:k + 1] = top_i[k]
        gate_ref[:, k:k + 1] = ex[k] / den


def _merge_route(x2d, g_mix, o_sb, o_dsa, o_mem, w_gate, b_gate, w_branch, w_out, g_ffn, w_router, b_router, tm):
    n = x2d.shape[0]
    row = lambda i: (i, 0)
    c2 = lambda i: (0, 0)
    c3 = lambda i: (0, 0, 0)
    wr_hi = w_router.astype(BF16)
    wr_lo = (w_router - wr_hi.astype(F32)).astype(BF16)
    return pl.pallas_call(
        _merge_kernel,
        grid=(n // tm,),
        in_specs=[pl.BlockSpec((tm, D_MODEL), row), pl.BlockSpec((1, D_MODEL), c2),
                  pl.BlockSpec((tm, BRANCH_W), row), pl.BlockSpec((tm, BRANCH_W), row),
                  pl.BlockSpec((tm, BRANCH_W), row),
                  pl.BlockSpec((3, D_MODEL, D_MODEL), c3), pl.BlockSpec((3, 1, D_MODEL), c3),
                  pl.BlockSpec((3, BRANCH_W, D_MODEL), c3), pl.BlockSpec((D_MODEL, D_MODEL), c2),
                  pl.BlockSpec((1, D_MODEL), c2), pl.BlockSpec((D_MODEL, N_EXPERTS), c2),
                  pl.BlockSpec((D_MODEL, N_EXPERTS), c2), pl.BlockSpec((1, N_EXPERTS), c2)],
        out_specs=[pl.BlockSpec((tm, D_MODEL), row), pl.BlockSpec((tm * TOKEN_TILE, LANES), row),
                   pl.BlockSpec((tm, TOP_K), row), pl.BlockSpec((tm, TOP_K), row)],
        out_shape=[jax.ShapeDtypeStruct((n, D_MODEL), F32), jax.ShapeDtypeStruct((n * TOKEN_TILE, LANES), F32),
                   jax.ShapeDtypeStruct((n, TOP_K), I32), jax.ShapeDtypeStruct((n, TOP_K), F32)],
        compiler_params=_cparams(("parallel",)),
        name="merge_route",
    )(x2d, g_mix[None, :], o_sb, o_dsa, o_mem, w_gate.astype(BF16), b_gate[:, None, :],
      w_branch.astype(BF16), w_out.astype(BF16), g_ffn[None, :], wr_hi, wr_lo, b_router[None, :])


def _rank_kernel(idx_ref, l_ref, rank_ref, cnt_ref, carry_ref):
    @pl.when(pl.program_id(0) == 0)
    def _():
        carry_ref[...] = jnp.zeros_like(carry_ref)

    idx = idx_ref[...]
    tm = idx.shape[0]
    lane = lax.broadcasted_iota(I32, (tm, LANES), 1)
    hits = [lane == idx[:, k:k + 1] for k in range(TOP_K)]
    onehot = sum(jnp.where(hk, 1.0, 0.0) for hk in hits)
    before = _dot(l_ref[...], onehot.astype(BF16)) + carry_ref[...]
    for k in range(TOP_K):
        rank_ref[:, k:k + 1] = jnp.sum(jnp.where(hits[k], before, 0.0), axis=-1, keepdims=True).astype(I32)
    carry_ref[...] += jnp.sum(onehot, axis=0, keepdims=True)
    cnt_ref[...] = carry_ref[...]


def _expert_rank(top_idx, tm):
    n = top_idx.shape[0]
    return pl.pallas_call(
        _rank_kernel,
        grid=(n // tm,),
        in_specs=[pl.BlockSpec((tm, TOP_K), lambda i: (i, 0)), pl.BlockSpec((tm, tm), lambda i: (0, 0))],
        out_specs=[pl.BlockSpec((tm, TOP_K), lambda i: (i, 0)), pl.BlockSpec((1, LANES), lambda i: (0, 0))],
        out_shape=[jax.ShapeDtypeStruct((n, TOP_K), I32), jax.ShapeDtypeStruct((1, LANES), F32)],
        scratch_shapes=[pltpu.VMEM((1, LANES), F32)],
        compiler_params=_cparams(("arbitrary",)),
        name="expert_rank",
    )(top_idx, _strict_tri(tm, lower=True))


DISPATCH_TOKENS = 256


def _dispatch_kernel(dest_ref, h_ref, xs_in_ref, xs_ref, sem):
    del xs_in_ref
    n_copies = DISPATCH_TOKENS * TOP_K

    def start(t, c):
        src = pl.multiple_of(t * TOKEN_TILE, TOKEN_TILE)
        for k in range(TOP_K):
            dst = pl.multiple_of(dest_ref[0, 0, t * TOP_K + k] * TOKEN_TILE, TOKEN_TILE)
            pltpu.make_async_copy(h_ref.at[pl.ds(src, TOKEN_TILE)], xs_ref.at[pl.ds(dst, TOKEN_TILE)], sem).start()
        return c

    lax.fori_loop(0, DISPATCH_TOKENS, start, 0, unroll=2)
    total = n_copies * TOKEN_TILE
    pltpu.make_async_copy(xs_ref.at[pl.ds(0, total)], xs_ref.at[pl.ds(0, total)], sem).wait()


def _dispatch(dest, h2t, n_slots):
    n = h2t.shape[0] // TOKEN_TILE
    per = DISPATCH_TOKENS * TOP_K
    return pl.pallas_call(
        _dispatch_kernel,
        grid=(n // DISPATCH_TOKENS,),
        in_specs=[pl.BlockSpec((1, 1, per), lambda i: (i, 0, 0), memory_space=pltpu.SMEM),
                  pl.BlockSpec((DISPATCH_TOKENS * TOKEN_TILE, LANES), lambda i: (i, 0)),
                  pl.BlockSpec(memory_space=pl.ANY)],
        out_specs=pl.BlockSpec(memory_space=pl.ANY),
        out_shape=jax.ShapeDtypeStruct((n_slots * TOKEN_TILE, LANES), F32),
        scratch_shapes=[pltpu.SemaphoreType.DMA(())],
        input_output_aliases={2: 0},
        compiler_params=_cparams(("arbitrary",)),
        name="moe_dispatch",
    )(dest.reshape(n // DISPATCH_TOKENS, 1, per), h2t, jnp.zeros((n_slots * TOKEN_TILE, LANES), F32))


def _expert_kernel(blk_e_ref, nused_ref, x_ref, w1_ref, b1_ref, w2_ref, b2_ref, y_ref, xb_ref):
    del blk_e_ref

    @pl.when(pl.program_id(0) < nused_ref[0])
    def _():
        for s in range(TOKEN_TILE):
            xb_ref[:, s * LANES:(s + 1) * LANES] = _load_token_tiles(x_ref, 0, MOE_BLOCK, s).astype(BF16)
        hb = _dot(xb_ref[...], w1_ref[0]) + b1_ref[0]
        g = jnp.minimum(hb[:, :D_EXPERT], SWIGLU_LIMIT)
        u = jnp.clip(hb[:, D_EXPERT:], -SWIGLU_LIMIT, SWIGLU_LIMIT)
        act = (u + 1.0) * (g * jax.nn.sigmoid(SWIGLU_ALPHA * g))
        _store_token_tiles(y_ref, _dot(act.astype(BF16), w2_ref[0]) + b2_ref[0])

    @pl.when(pl.program_id(0) >= nused_ref[0])
    def _():
        y_ref[...] = jnp.zeros_like(y_ref)


def _experts(blk_e, n_used, xs, w_e_in, b_e_in, w_e_out, b_e_out):
    n_slots = xs.shape[0] // TOKEN_TILE
    nblk = n_slots // MOE_BLOCK
    slot_block = pl.BlockSpec((MOE_BLOCK * TOKEN_TILE, LANES), lambda i, be, nu: (i, 0))
    grid_spec = pltpu.PrefetchScalarGridSpec(
        num_scalar_prefetch=2,
        grid=(nblk,),
        in_specs=[slot_block,
                  pl.BlockSpec((1, D_MODEL, 2 * D_EXPERT), lambda i, be, nu: (be[i], 0, 0)),
                  pl.BlockSpec((1, 1, 2 * D_EXPERT), lambda i, be, nu: (be[i], 0, 0)),
                  pl.BlockSpec((1, D_EXPERT, D_MODEL), lambda i, be, nu: (be[i], 0, 0)),
                  pl.BlockSpec((1, 1, D_MODEL), lambda i, be, nu: (be[i], 0, 0))],
        out_specs=slot_block,
        scratch_shapes=[pltpu.VMEM((MOE_BLOCK, D_MODEL), BF16)],
    )
    return pl.pallas_call(
        _expert_kernel,
        grid_spec=grid_spec,
        out_shape=jax.ShapeDtypeStruct((n_slots * TOKEN_TILE, LANES), F32),
        compiler_params=_cparams(("arbitrary",)),
        name="moe_experts",
    )(blk_e, n_used, xs, w_e_in.astype(BF16), b_e_in[:, None, :], w_e_out.astype(BF16), b_e_out[:, None, :])


def _combine_kernel(dest_ref, x_ref, gate_ref, y_ref, o_ref, buf_ref, sem):
    n_copies = DISPATCH_TOKENS * TOP_K

    def start(t, c):
        for k in range(TOP_K):
            src = pl.multiple_of(dest_ref[0, 0, t * TOP_K + k] * TOKEN_TILE, TOKEN_TILE)
            dst = pl.multiple_of((k * DISPATCH_TOKENS + t) * TOKEN_TILE, TOKEN_TILE)
            pltpu.make_async_copy(y_ref.at[pl.ds(src, TOKEN_TILE)], buf_ref.at[pl.ds(dst, TOKEN_TILE)], sem).start()
        return c

    lax.fori_loop(0, DISPATCH_TOKENS, start, 0, unroll=2)
    pltpu.make_async_copy(y_ref.at[pl.ds(0, n_copies * TOKEN_TILE)], buf_ref, sem).wait()
    gate = gate_ref[...]
    gates = [jnp.broadcast_to(gate[:, k:k + 1], (DISPATCH_TOKENS, LANES)) for k in range(TOP_K)]
    for s in range(TOKEN_TILE):
        out = x_ref[:, s * LANES:(s + 1) * LANES]
        for k in range(TOP_K):
            out = out + gates[k] * _load_token_tiles(buf_ref, k * DISPATCH_TOKENS, DISPATCH_TOKENS, s)
        o_ref[:, s * LANES:(s + 1) * LANES] = out


def _combine(dest, x2, gate, ys):
    n = x2.shape[0]
    per = DISPATCH_TOKENS * TOP_K
    return pl.pallas_call(
        _combine_kernel,
        grid=(n // DISPATCH_TOKENS,),
        in_specs=[pl.BlockSpec((1, 1, per), lambda i: (i, 0, 0), memory_space=pltpu.SMEM),
                  pl.BlockSpec((DISPATCH_TOKENS, D_MODEL), lambda i: (i, 0)),
                  pl.BlockSpec((DISPATCH_TOKENS, TOP_K), lambda i: (i, 0)),
                  pl.BlockSpec(memory_space=pl.ANY)],
        out_specs=pl.BlockSpec((DISPATCH_TOKENS, D_MODEL), lambda i: (i, 0)),
        out_shape=jax.ShapeDtypeStruct((n, D_MODEL), F32),
        scratch_shapes=[pltpu.VMEM((TOP_K * DISPATCH_TOKENS * TOKEN_TILE, LANES), F32),
                        pltpu.SemaphoreType.DMA(())],
        compiler_params=_cparams(("arbitrary",)),
        name="moe_combine",
    )(dest.reshape(n // DISPATCH_TOKENS, 1, per), x2, gate, ys)


def _moe(x2, h2, top_idx, gate, w_e_in, b_e_in, w_e_out, b_e_out):
    n = x2.shape[0]
    rank, counts = _expert_rank(top_idx, 512)
    counts = counts[0, :N_EXPERTS].astype(I32)
    padded = (counts + MOE_BLOCK - 1) // MOE_BLOCK * MOE_BLOCK
    pend = jnp.cumsum(padded)
    pstart = pend - padded
    nblk = -(-(n * TOP_K) // MOE_BLOCK) + N_EXPERTS
    blk_start = jnp.arange(nblk, dtype=I32) * MOE_BLOCK
    blk_e = jnp.minimum(jnp.sum((pend[None, :] <= blk_start[:, None]).astype(I32), axis=1), N_EXPERTS - 1)
    n_used = (pend[-1:] // MOE_BLOCK).astype(I32)
    onehot = top_idx[:, :, None] == jnp.arange(N_EXPERTS, dtype=I32)[None, None, :]
    dest = rank + jnp.sum(jnp.where(onehot, pstart[None, None, :], 0), axis=-1)
    xs = _dispatch(dest, h2, nblk * MOE_BLOCK)
    ys = _experts(blk_e, n_used, xs, w_e_in, b_e_in, w_e_out, b_e_out)
    return _combine(dest, x2, gate, ys)


def _layer(x, mem, g_mix, w_in, g_q_dsa, g_k_dsa, g_q_mem, g_k_mem, g_mem, w_mem_kv, w_gate, b_gate,
           w_branch, w_out, g_ffn, w_router, b_router, w_e_in, b_e_in, w_e_out, b_e_out):
    batch, seq, _ = x.shape
    n = batch * seq
    topk = min(DSA_TOPK_MAX, seq // 4)
    x2d = x.reshape(n, D_MODEL)
    tm = min(512, seq)
    sq, sk, sv, dq, iq, kvi, iw, mq, vt = _inproj(x2d, batch, seq, g_mix, w_in, g_q_dsa, g_k_dsa, g_q_mem, tm)
    o_sb = _sb_attention(sq, sk, sv, min(256, seq))
    o_dsa = _dsa_attention(dq, iq, iw, kvi, vt, 128, topk)
    mlen = mem.shape[1]
    mk, mv = _mem_kv(mem.reshape(batch * mlen, D_MODEL), g_mem, w_mem_kv, g_k_mem, min(512, batch * mlen))
    mw = MEM_HEADS * MEM_HEAD_DIM
    o_mem = _mem_attention(mq.reshape(batch, seq, mw), mk.reshape(batch, mlen, mw), mv.reshape(batch, mlen, mw), tm)
    x2, h2, top_idx, gate = _merge_route(
        x2d, g_mix, o_sb.reshape(n, BRANCH_W), o_dsa.reshape(n, BRANCH_W), o_mem.reshape(n, mw),
        w_gate, b_gate, w_branch, w_out, g_ffn, w_router, b_router, tm)
    out = _moe(x2, h2, top_idx, gate, w_e_in, b_e_in, w_e_out, b_e_out)
    return out.reshape(batch, seq, D_MODEL)


def kernel(x, mem, g_mix, w_in, g_q_dsa, g_k_dsa, g_q_mem, g_k_mem, g_mem, w_mem_kv, w_gate, b_gate, w_branch, w_out, g_ffn, w_router, b_router, w_e_in, b_e_in, w_e_out, b_e_out):
    for l in range(g_mix.shape[0]):
        x = _layer(x, mem, g_mix[l], w_in[l], g_q_dsa[l], g_k_dsa[l], g_q_mem[l], g_k_mem[l], g_mem[l],
                   w_mem_kv[l], w_gate[l], b_gate[l], w_branch[l], w_out[l], g_ffn[l], w_router[l],
                   b_router[l], w_e_in[l], b_e_in[l], w_e_out[l], b_e_out[l])
    return x
```

```python
import functools

import numpy as np
import jax
import jax.numpy as jnp
from jax import lax
from jax.experimental import pallas as pl
from jax.experimental.pallas import tpu as pltpu

F32 = jnp.float32
BF16 = jnp.bfloat16
I32 = jnp.int32

D_MODEL = 1024
CHUNK = 64
SB_HEADS = 8
DSA_HEADS = 8
HEAD_DIM = 64
IDX_HEADS = 4
DSA_TOPK_MAX = 256
MEM_HEADS = 4
MEM_HEAD_DIM = 128
N_EXPERTS = 32
TOP_K = 4
D_EXPERT = D_MODEL
SWIGLU_LIMIT = 7.0
SWIGLU_ALPHA = 1.702
ROPE_THETA = 10000.0
EPS = 1e-6
MOE_BLOCK = 512

BRANCH_W = 512
IN_SIZES = (512, 512, 512, 512, 64, 64, 256, 64, 4, 512)
C_SQ, C_SK, C_SV, C_DQ, C_MQ, C_IQ, C_SMALL, C_END = 0, 512, 1024, 1536, 2048, 2560, 2816, 3072
IW_LANE = 64

LANES = 128
NEG_BIG = -1e30
SB_CUTOFF = 110.0
KEY_NEG_INF = int(np.array(-np.inf, np.float32).view(np.int32)) ^ 0x7FFFFFFF
INT_MIN = -(2 ** 31)

VMEM_LIMIT = 56 * 1024 * 1024


def _cparams(sem):
    return pltpu.CompilerParams(dimension_semantics=sem, vmem_limit_bytes=VMEM_LIMIT)


def _dot(a, b):
    return jnp.dot(a, b, preferred_element_type=F32)


def _dot_nt(a, b):
    return lax.dot_general(a, b, (((1,), (1,)), ((), ())), preferred_element_type=F32)


def _split_bf16(x):
    hi = x.astype(BF16)
    lo = (x - hi.astype(F32)).astype(BF16)
    return hi, lo


def _dot_split(x, m_bf16):
    hi, lo = _split_bf16(x)
    return _dot(hi, m_bf16) + _dot(lo, m_bf16)


TOKEN_TILE = D_MODEL // LANES


def _store_token_tiles(ref, y):
    rows = y.shape[0]
    for s in range(TOKEN_TILE):
        ref[pl.ds(s, rows, stride=TOKEN_TILE), :] = y[:, s * LANES:(s + 1) * LANES]


def _load_token_tiles(ref, start_row, rows, s):
    return ref[pl.ds(start_row * TOKEN_TILE + s, rows, stride=TOKEN_TILE), :]


def _rot_half_unsigned(y):
    w = y.shape[1]
    lane = lax.broadcasted_iota(I32, y.shape, 1)
    return jnp.where((lane & 32) == 0, pltpu.roll(y, w - 32, 1), pltpu.roll(y, 32, 1))


def _inproj_kernel(x_ref, g_ref, w_ref, wsvt_ref, cos_ref, sin_ref, coss_ref, sins_ref, gq_ref, gks_ref,
                   gm_ref, bd64_ref, bd128_ref,
                   sq_ref, sk_ref, sv_ref, dq_ref, iq_ref, kvi_ref, iw_ref, mq_ref, vt_ref):
    x = x_ref[...]
    ms = jnp.mean(x * x, axis=-1, keepdims=True)
    h = (x * lax.rsqrt(ms + EPS) * g_ref[...]).astype(BF16)

    def seg(a, b):
        return _dot(h, w_ref[:, a:b])

    def put_heads(ref, y):
        for hd in range(y.shape[1] // HEAD_DIM):
            ref[0, hd] = y[:, hd * HEAD_DIM:(hd + 1) * HEAD_DIM].astype(BF16)

    put_heads(sq_ref, seg(C_SQ, C_SK) * (HEAD_DIM ** -0.5))
    put_heads(sk_ref, seg(C_SK, C_SV))
    sv_ref[0] = _dot_nt(wsvt_ref[...], h).astype(BF16)

    y = seg(C_DQ, C_MQ)
    msq = _dot_split(y * y, bd64_ref[...]) * (1.0 / HEAD_DIM)
    y = y * lax.rsqrt(msq + EPS) * gq_ref[...]
    y = y * cos_ref[...] + _rot_half_unsigned(y) * sin_ref[...]
    put_heads(dq_ref, y * (HEAD_DIM ** -0.5))

    y = seg(C_IQ, C_SMALL)
    y = y * cos_ref[:, :256] + _rot_half_unsigned(y) * sin_ref[:, :256]
    iq_ref[...] = (y * (HEAD_DIM ** -0.5)).astype(BF16)

    y = seg(C_SMALL, C_END)
    lane = lax.broadcasted_iota(I32, y.shape, 1)
    is_k = lane < HEAD_DIM
    msk = jnp.sum(jnp.where(is_k, y * y, 0.0), axis=-1, keepdims=True) * (1.0 / HEAD_DIM)
    y = y * jnp.where(is_k, lax.rsqrt(msk + EPS) * gks_ref[...], 1.0)
    y = y * coss_ref[...] + _rot_half_unsigned(y) * sins_ref[...]
    kvi_ref[...] = y.astype(BF16)
    iw_ref[...] = y[:, 128:256]
    kv_t = y[:, 0:128].T
    row_t = lax.broadcasted_iota(I32, kv_t.shape, 0)
    vt_ref[0] = jnp.where(row_t < HEAD_DIM, 1.0, kv_t).astype(BF16)

    y = seg(C_MQ, C_IQ)
    msm = _dot_split(y * y, bd128_ref[...]) * (1.0 / MEM_HEAD_DIM)
    mq_ref[...] = (y * lax.rsqrt(msm + EPS) * gm_ref[...]).astype(BF16)


def _rope_tables(seq):
    half = HEAD_DIM // 2
    inv = ROPE_THETA ** (-jnp.arange(half, dtype=F32) / half)
    ang = jnp.arange(seq).astype(F32)[:, None] * inv[None, :]
    cos = jnp.cos(ang)
    sin = jnp.sin(ang)
    cos64 = jnp.concatenate([cos, cos], axis=1)
    sin64 = jnp.concatenate([-sin, sin], axis=1)
    one = jnp.ones_like(cos64)
    zero = jnp.zeros_like(cos64)
    cosq = jnp.tile(cos64, (1, 8))
    sinq = jnp.tile(sin64, (1, 8))
    coss = jnp.concatenate([cos64, one, cos64, one], axis=1)
    sins = jnp.concatenate([sin64, zero, sin64, zero], axis=1)
    return cosq, sinq, coss, sins


def _block_diag_ones(width, group):
    idx = np.arange(width) // group
    return jnp.asarray((idx[:, None] == idx[None, :]).astype(np.float32), dtype=BF16)


def _inproj(x2d, batch, seq, g_mix, w_in, g_q_dsa, g_k_dsa, g_q_mem, tm):
    n = x2d.shape[0]
    sizes = np.cumsum((0,) + IN_SIZES)
    col = {name: (int(sizes[i]), int(sizes[i + 1])) for i, name in enumerate(
        ("sq", "sk", "sv", "dq", "dk", "dv", "iq", "ik", "iw", "mq"))}
    order = ("sq", "sk", "sv", "dq", "mq", "iq", "dk", "dv", "ik", "iw")
    w = jnp.concatenate([w_in[:, col[k][0]:col[k][1]] for k in order]
                        + [jnp.zeros((D_MODEL, C_END - sum(IN_SIZES)), w_in.dtype)], axis=1).astype(BF16)
    cosq, sinq, coss, sins = (jnp.asarray(t) for t in _rope_tables(seq))
    gq = jnp.tile(g_q_dsa, 8)[None, :]
    gks = jnp.concatenate([g_k_dsa, jnp.ones((256 - HEAD_DIM,), F32)])[None, :]
    gm = jnp.tile(g_q_mem, MEM_HEADS)[None, :]
    spb = seq // tm
    row = lambda i: (i, 0)
    const = lambda i: (0, 0)
    pos = lambda i: (i % spb, 0)
    heads = lambda i: (i // spb, 0, i % spb, 0)
    head_shape = jax.ShapeDtypeStruct((batch, 8, seq, HEAD_DIM), BF16)
    head_spec = pl.BlockSpec((1, 8, tm, HEAD_DIM), heads)
    return pl.pallas_call(
        _inproj_kernel,
        grid=(n // tm,),
        in_specs=[
            pl.BlockSpec((tm, D_MODEL), row),
            pl.BlockSpec((1, D_MODEL), const),
            pl.BlockSpec((D_MODEL, C_END), const),
            pl.BlockSpec((512, D_MODEL), const),
            pl.BlockSpec((tm, 512), pos), pl.BlockSpec((tm, 512), pos),
            pl.BlockSpec((tm, 256), pos), pl.BlockSpec((tm, 256), pos),
            pl.BlockSpec((1, 512), const), pl.BlockSpec((1, 256), const), pl.BlockSpec((1, 512), const),
            pl.BlockSpec((512, 512), const), pl.BlockSpec((512, 512), const),
        ],
        out_specs=[head_spec, head_spec, pl.BlockSpec((1, 512, tm), lambda i: (i // spb, 0, i % spb)), head_spec,
                   pl.BlockSpec((tm, 256), row), pl.BlockSpec((tm, 256), row),
                   pl.BlockSpec((tm, 128), row), pl.BlockSpec((tm, 512), row),
                   pl.BlockSpec((1, 128, tm), lambda i: (i // spb, 0, i % spb))],
        out_shape=[head_shape, head_shape, jax.ShapeDtypeStruct((batch, 512, seq), BF16), head_shape,
                   jax.ShapeDtypeStruct((n, 256), BF16), jax.ShapeDtypeStruct((n, 256), BF16),
                   jax.ShapeDtypeStruct((n, 128), F32), jax.ShapeDtypeStruct((n, 512), BF16),
                   jax.ShapeDtypeStruct((batch, 128, seq), BF16)],
        compiler_params=_cparams(("parallel",)),
        name="inproj",
    )(x2d, g_mix[None, :], w, w_in[:, col["sv"][0]:col["sv"][1]].T.astype(BF16), cosq, sinq, coss, sins, gq, gks, gm,
      _block_diag_ones(512, HEAD_DIM), _block_diag_ones(512, MEM_HEAD_DIM))


def _sb_kernel(q_ref, k_ref, vt_ref, u_ref, o_ref, acc_ref, car_ref, *, tq):
    qi = pl.program_id(1)
    rows = lax.broadcasted_iota(I32, (tq, tq), 0)
    cols = lax.broadcasted_iota(I32, (tq, tq), 1)
    dif = rows - cols
    u = u_ref[...]
    acc_ref[...] = jnp.zeros_like(acc_ref)
    car_ref[...] = jnp.zeros_like(car_ref)

    def cond(c):
        kb, mx = c
        return jnp.logical_and(kb >= 0, mx > -SB_CUTOFF)

    def body(c):
        kb, _ = c
        ks = pl.multiple_of(kb * tq, tq)
        earlier = dif < (qi - kb) * tq
        heads = range(SB_HEADS)
        z = [_dot_nt(k_ref[0, hd, pl.ds(ks, tq), :], q_ref[0, hd]) for hd in heads]
        ls, lk, between = [], [], []
        for hd in heads:
            sp = jnp.maximum(z[hd], 0.0) + jnp.log(1.0 + jnp.exp(-jnp.abs(z[hd])))
            ls.append(z[hd] - sp)
            lk.append(jnp.where(earlier, -sp, 0.0))
            hi, lo = _split_bf16(lk[hd])
            between.append(_dot(u, hi) + _dot(u, lo))
        for hd in heads:
            rs = slice(hd * HEAD_DIM, (hd + 1) * HEAD_DIM)
            car = car_ref[hd:hd + 1, :]
            w = jnp.where(earlier, jnp.exp(ls[hd] + between[hd] + car), 0.0)
            acc_ref[rs, :] += _dot(vt_ref[0, rs, pl.ds(ks, tq)], w.astype(BF16))
            car_ref[hd:hd + 1, :] = car + (between[hd][0:1, :] + lk[hd][0:1, :])
        return kb - 1, jnp.max(car_ref[...])

    lax.while_loop(cond, body, (qi, jnp.float32(0.0)))
    o_ref[0] = acc_ref[...].T.astype(BF16)


def _strict_tri(n, lower):
    i = np.arange(n)
    m = (i[:, None] > i[None, :]) if lower else (i[:, None] < i[None, :])
    return jnp.asarray(m.astype(np.float32), dtype=BF16)


def _sb_attention(sq, sk, svt, tq):
    batch, _, seq, _ = sq.shape
    return pl.pallas_call(
        functools.partial(_sb_kernel, tq=tq),
        grid=(batch, seq // tq),
        in_specs=[pl.BlockSpec((1, SB_HEADS, tq, HEAD_DIM), lambda b, i: (b, 0, i, 0)),
                  pl.BlockSpec((1, SB_HEADS, seq, HEAD_DIM), lambda b, i: (b, 0, 0, 0)),
                  pl.BlockSpec((1, BRANCH_W, seq), lambda b, i: (b, 0, 0)),
                  pl.BlockSpec((tq, tq), lambda b, i: (0, 0))],
        out_specs=pl.BlockSpec((1, tq, BRANCH_W), lambda b, i: (b, i, 0)),
        out_shape=jax.ShapeDtypeStruct((batch, seq, BRANCH_W), BF16),
        scratch_shapes=[pltpu.VMEM((BRANCH_W, tq), F32), pltpu.VMEM((SB_HEADS, tq), F32)],
        compiler_params=_cparams(("parallel", "parallel")),
        name="sb_attention",
    )(sq, sk, svt, _strict_tri(tq, lower=False))


DSA_SEG = 512
DSA_KB = 512
DSA_GROUP_HEADS = 8


def _tree_sum(parts):
    while len(parts) > 1:
        parts = [parts[i] + parts[i + 1] for i in range(0, len(parts) - 1, 2)] + ([parts[-1]] if len(parts) % 2 else [])
    return parts[0]


def _dsa_kernel(dq_ref, iq_ref, iw_ref, kvi_ref, vt_ref, tri_ref, o_ref,
                sc_ref, bias_ref, s_ref, *, tq, topk, nseg_max):
    nseg = (pl.program_id(1) * tq) // DSA_SEG + 1
    for ns in range(1, nseg_max + 1):
        @pl.when(nseg == ns)
        def _(ns=ns):
            _dsa_body(dq_ref, iq_ref, iw_ref, kvi_ref, vt_ref, tri_ref, o_ref, sc_ref, bias_ref, s_ref,
                      tq=tq, topk=topk, nseg=ns)


def _dsa_body(dq_ref, iq_ref, iw_ref, kvi_ref, vt_ref, tri_ref, o_ref, sc_ref, bias_ref, s_ref, *, tq, topk, nseg):
    qs = pl.program_id(1) * tq
    blocks = [slice(c * DSA_KB, (c + 1) * DSA_KB) for c in range(nseg)]
    iq = iq_ref[0]
    w_t = iw_ref[0].T
    w_row = [w_t[IW_LANE + h:IW_LANE + h + 1, :] * (IDX_HEADS ** -0.5) for h in range(IDX_HEADS)]
    q_chunk = (qs + lax.broadcasted_iota(I32, (DSA_KB, tq), 1)) // CHUNK
    k_chunk = lax.broadcasted_iota(I32, (DSA_KB, tq), 0) // CHUNK

    for c, blk in enumerate(blocks):
        ik = kvi_ref[0, blk, 128:192]
        lg = [_dot_nt(ik, iq[:, h * HEAD_DIM:(h + 1) * HEAD_DIM]) for h in range(IDX_HEADS)]
        sc = jnp.zeros((DSA_KB, tq), F32)
        for h in range(IDX_HEADS):
            sc = sc + w_row[h] * jnp.maximum(lg[h], 0.0)
        admissible = (c * (DSA_KB // CHUNK) + k_chunk) <= q_chunk
        sc_ref[blk, :] = jnp.where(admissible, sc, -jnp.inf)

    def count(pred_fn):
        sub, lanes_of_sums = 32, 4
        acc = [jnp.zeros((sub, tq), F32)] * lanes_of_sums
        for j in range(nseg * DSA_SEG // sub):
            acc[j % lanes_of_sums] = acc[j % lanes_of_sums] + jnp.where(
                pred_fn(sc_ref[j * sub:(j + 1) * sub, :]), 1.0, 0.0)
        return jnp.sum(_tree_sum(acc), axis=0, keepdims=True)

    def key_to_float(key):
        return lax.bitcast_convert_type(jnp.where(key >= 0, key, key ^ 0x7FFFFFFF), F32)

    kf = jnp.float32(topk)
    n_rows = jnp.float32(nseg * DSA_SEG)
    t0 = jnp.where(count(lambda s: s >= 0.0) >= kf, 0, INT_MIN).astype(I32)

    def bit_step(i, t):
        cand = t + lax.shift_left(jnp.int32(1), 30 - i)
        cand_f = key_to_float(cand)
        cnt = jnp.where(cand <= KEY_NEG_INF, n_rows, count(lambda s: s >= cand_f))
        return jnp.where(cnt >= kf, cand, t)

    thr_key = lax.fori_loop(0, 31, bit_step, t0)
    thr = jnp.where(thr_key <= KEY_NEG_INF, -jnp.inf, key_to_float(thr_key))
    need = kf - count(lambda s: s > thr)
    tri = tri_ref[...]

    surplus = jnp.max(count(lambda s: s >= thr)) > kf

    @pl.when(surplus)
    def _():
        prefix = jnp.zeros((1, tq), F32)
        for blk in blocks:
            sc = sc_ref[blk, :]
            eqf = jnp.where(sc == thr, 1.0, 0.0)
            rank = _dot(tri, eqf.astype(BF16)) + prefix
            tie = jnp.where(rank < need, eqf, 0.0)
            sel = jnp.where(sc > thr, 1.0, tie)
            bias_ref[blk, :] = jnp.where(sc > -jnp.inf, (sel - 1.0) * (-NEG_BIG), NEG_BIG)
            prefix = prefix + jnp.sum(eqf, axis=0, keepdims=True)

    @pl.when(jnp.logical_not(surplus))
    def _():
        for blk in blocks:
            bias_ref[blk, :] = jnp.where(sc_ref[blk, :] >= thr, 0.0, NEG_BIG)

    q8 = dq_ref[0].reshape(DSA_HEADS * tq, HEAD_DIM)
    m = jnp.full((1, DSA_HEADS * tq), NEG_BIG, F32)
    for blk in blocks:
        b = bias_ref[blk, :]
        s = _dot_nt(kvi_ref[0, blk, 0:HEAD_DIM], q8) + jnp.concatenate([b] * DSA_HEADS, axis=1)
        s_ref[blk, :] = s
        m = jnp.maximum(m, jnp.max(s, axis=0, keepdims=True))
    acc = jnp.zeros((128, DSA_HEADS * tq), F32)
    for blk in blocks:
        p = jnp.exp(s_ref[blk, :] - m)
        acc = acc + _dot(vt_ref[0, :, blk], p.astype(BF16))
    for hd in range(DSA_HEADS):
        a = acc[:, hd * tq:(hd + 1) * tq]
        o = (a / a[0:1, :]).T
        o_ref[0, :, hd * HEAD_DIM:(hd + 1) * HEAD_DIM] = o[:, HEAD_DIM:].astype(BF16)


def _dsa_attention(dq, iq, iw, kvi, vt, tq, topk):
    batch, _, seq, _ = dq.shape
    return pl.pallas_call(
        functools.partial(_dsa_kernel, tq=tq, topk=topk, nseg_max=seq // DSA_SEG),
        grid=(batch, seq // tq),
        in_specs=[
            pl.BlockSpec((1, DSA_HEADS, tq, HEAD_DIM), lambda b, i: (b, 0, i, 0)),
            pl.BlockSpec((1, tq, 256), lambda b, i: (b, i, 0)),
            pl.BlockSpec((1, tq, 128), lambda b, i: (b, i, 0)),
            pl.BlockSpec((1, seq, 256), lambda b, i: (b, 0, 0)),
            pl.BlockSpec((1, 128, seq), lambda b, i: (b, 0, 0)),
            pl.BlockSpec((DSA_KB, DSA_KB), lambda b, i: (0, 0)),
        ],
        out_specs=pl.BlockSpec((1, tq, BRANCH_W), lambda b, i: (b, i, 0)),
        out_shape=jax.ShapeDtypeStruct((batch, seq, BRANCH_W), BF16),
        scratch_shapes=[pltpu.VMEM((seq, tq), F32), pltpu.VMEM((seq, tq), F32),
                        pltpu.VMEM((seq, DSA_HEADS * tq), F32)],
        compiler_params=_cparams(("parallel", "parallel")),
        name="dsa_attention",
    )(dq, iq.reshape(batch, seq, 256), iw.reshape(batch, seq, 128), kvi.reshape(batch, seq, 256), vt,
      _strict_tri(DSA_KB, lower=True))


def _memkv_kernel(m_ref, g_ref, w_ref, gk_ref, bd_ref, mk_ref, mv_ref):
    x = m_ref[...]
    ms = jnp.mean(x * x, axis=-1, keepdims=True)
    h = (x * lax.rsqrt(ms + EPS) * g_ref[...]).astype(BF16)
    mw = MEM_HEADS * MEM_HEAD_DIM
    k = _dot(h, w_ref[:, :mw])
    msk = _dot_split(k * k, bd_ref[...]) * (1.0 / MEM_HEAD_DIM)
    mk_ref[...] = (k * lax.rsqrt(msk + EPS) * gk_ref[...]).astype(BF16)
    mv_ref[...] = _dot(h, w_ref[:, mw:]).astype(BF16)


def _mem_kv(mem2d, g_mem, w_mem_kv, g_k_mem, tm):
    n = mem2d.shape[0]
    mw = MEM_HEADS * MEM_HEAD_DIM
    row = lambda i: (i, 0)
    const = lambda i: (0, 0)
    return pl.pallas_call(
        _memkv_kernel,
        grid=(n // tm,),
        in_specs=[pl.BlockSpec((tm, D_MODEL), row), pl.BlockSpec((1, D_MODEL), const),
                  pl.BlockSpec((D_MODEL, 2 * mw), const), pl.BlockSpec((1, mw), const),
                  pl.BlockSpec((mw, mw), const)],
        out_specs=[pl.BlockSpec((tm, mw), row), pl.BlockSpec((tm, mw), row)],
        out_shape=[jax.ShapeDtypeStruct((n, mw), BF16), jax.ShapeDtypeStruct((n, mw), BF16)],
        compiler_params=_cparams(("parallel",)),
        name="mem_kv",
    )(mem2d, g_mem[None, :], w_mem_kv.astype(BF16), jnp.tile(g_k_mem, MEM_HEADS)[None, :],
      _block_diag_ones(mw, MEM_HEAD_DIM))


def _memattn_kernel(q_ref, k_ref, v_ref, o_ref):
    for hd in range(MEM_HEADS):
        sl = slice(hd * MEM_HEAD_DIM, (hd + 1) * MEM_HEAD_DIM)
        s = _dot_nt(q_ref[0, :, sl], k_ref[0, :, sl]) * (MEM_HEAD_DIM ** -0.5)
        p = jnp.exp(s - jnp.max(s, axis=-1, keepdims=True))
        o = _dot(p.astype(BF16), v_ref[0, :, sl]) / jnp.sum(p, axis=-1, keepdims=True)
        o_ref[0, :, sl] = o.astype(BF16)


def _mem_attention(mq, mk, mv, tq):
    batch, seq, mw = mq.shape
    mlen = mk.shape[1]
    kspec = pl.BlockSpec((1, mlen, mw), lambda b, i: (b, 0, 0))
    return pl.pallas_call(
        _memattn_kernel,
        grid=(batch, seq // tq),
        in_specs=[pl.BlockSpec((1, tq, mw), lambda b, i: (b, i, 0)), kspec, kspec],
        out_specs=pl.BlockSpec((1, tq, mw), lambda b, i: (b, i, 0)),
        out_shape=jax.ShapeDtypeStruct((batch, seq, mw), BF16),
        compiler_params=_cparams(("parallel", "parallel")),
        name="mem_attention",
    )(mq, mk, mv)


def _merge_kernel(x_ref, g_ref, osb_ref, odsa_ref, omem_ref, wg_ref, bg_ref, wb_ref, wo_ref,
                  gf_ref, wrh_ref, wrl_ref, br_ref,
                  x2_ref, h2_ref, idx_ref, gate_ref):
    x = x_ref[...]
    ms = jnp.mean(x * x, axis=-1, keepdims=True)
    h = (x * lax.rsqrt(ms + EPS) * g_ref[...]).astype(BF16)
    merged = None
    for n, o_ref in enumerate((osb_ref, odsa_ref, omem_ref)):
        gate = jax.nn.sigmoid(_dot(h, wg_ref[n]) + bg_ref[n])
        term = gate * _dot(o_ref[...], wb_ref[n])
        merged = term if merged is None else merged + term
    x2 = x + _dot(merged.astype(BF16), wo_ref[...])
    x2_ref[...] = x2

    ms2 = jnp.mean(x2 * x2, axis=-1, keepdims=True)
    h2 = x2 * lax.rsqrt(ms2 + EPS) * gf_ref[...]
    _store_token_tiles(h2_ref, h2)
    hi, lo = _split_bf16(h2)
    logits = _dot(hi, wrh_ref[...]) + _dot(hi, wrl_ref[...]) + _dot(lo, wrh_ref[...]) + br_ref[...]

    lane = lax.broadcasted_iota(I32, logits.shape, 1)
    vals = logits
    top_v, top_i = [], []
    for _ in range(TOP_K):
        mx = jnp.max(vals, axis=-1, keepdims=True)
        ix = jnp.min(jnp.where(vals == mx, lane, N_EXPERTS), axis=-1, keepdims=True)
        top_v.append(mx)
        top_i.append(ix)
        vals = jnp.where(lane == ix, -jnp.inf, vals)
    ex = [jnp.exp(v - top_v[0]) for v in top_v]
    den = ex[0] + ex[1] + ex[2] + ex[3]
    for k in range(TOP_K):
        idx_ref[:, k:k + 1] = top_i[k]
        gate_ref[:, k:k + 1] = ex[k] / den


def _merge_route(x2d, g_mix, o_sb, o_dsa, o_mem, w_gate, b_gate, w_branch, w_out, g_ffn, w_router, b_router, tm):
    n = x2d.shape[0]
    row = lambda i: (i, 0)
    c2 = lambda i: (0, 0)
    c3 = lambda i: (0, 0, 0)
    wr_hi = w_router.astype(BF16)
    wr_lo = (w_router - wr_hi.astype(F32)).astype(BF16)
    return pl.pallas_call(
        _merge_kernel,
        grid=(n // tm,),
        in_specs=[pl.BlockSpec((tm, D_MODEL), row), pl.BlockSpec((1, D_MODEL), c2),
                  pl.BlockSpec((tm, BRANCH_W), row), pl.BlockSpec((tm, BRANCH_W), row),
                  pl.BlockSpec((tm, BRANCH_W), row),
                  pl.BlockSpec((3, D_MODEL, D_MODEL), c3), pl.BlockSpec((3, 1, D_MODEL), c3),
                  pl.BlockSpec((3, BRANCH_W, D_MODEL), c3), pl.BlockSpec((D_MODEL, D_MODEL), c2),
                  pl.BlockSpec((1, D_MODEL), c2), pl.BlockSpec((D_MODEL, N_EXPERTS), c2),
                  pl.BlockSpec((D_MODEL, N_EXPERTS), c2), pl.BlockSpec((1, N_EXPERTS), c2)],
        out_specs=[pl.BlockSpec((tm, D_MODEL), row), pl.BlockSpec((tm * TOKEN_TILE, LANES), row),
                   pl.BlockSpec((tm, TOP_K), row), pl.BlockSpec((tm, TOP_K), row)],
        out_shape=[jax.ShapeDtypeStruct((n, D_MODEL), F32), jax.ShapeDtypeStruct((n * TOKEN_TILE, LANES), F32),
                   jax.ShapeDtypeStruct((n, TOP_K), I32), jax.ShapeDtypeStruct((n, TOP_K), F32)],
        compiler_params=_cparams(("parallel",)),
        name="merge_route",
    )(x2d, g_mix[None, :], o_sb, o_dsa, o_mem, w_gate.astype(BF16), b_gate[:, None, :],
      w_branch.astype(BF16), w_out.astype(BF16), g_ffn[None, :], wr_hi, wr_lo, b_router[None, :])


def _rank_kernel(idx_ref, l_ref, rank_ref, cnt_ref, carry_ref):
    @pl.when(pl.program_id(0) == 0)
    def _():
        carry_ref[...] = jnp.zeros_like(carry_ref)

    idx = idx_ref[...]
    tm = idx.shape[0]
    lane = lax.broadcasted_iota(I32, (tm, LANES), 1)
    hits = [lane == idx[:, k:k + 1] for k in range(TOP_K)]
    onehot = sum(jnp.where(hk, 1.0, 0.0) for hk in hits)
    before = _dot(l_ref[...], onehot.astype(BF16)) + carry_ref[...]
    for k in range(TOP_K):
        rank_ref[:, k:k + 1] = jnp.sum(jnp.where(hits[k], before, 0.0), axis=-1, keepdims=True).astype(I32)
    carry_ref[...] += jnp.sum(onehot, axis=0, keepdims=True)
    cnt_ref[...] = carry_ref[...]


def _expert_rank(top_idx, tm):
    n = top_idx.shape[0]
    return pl.pallas_call(
        _rank_kernel,
        grid=(n // tm,),
        in_specs=[pl.BlockSpec((tm, TOP_K), lambda i: (i, 0)), pl.BlockSpec((tm, tm), lambda i: (0, 0))],
        out_specs=[pl.BlockSpec((tm, TOP_K), lambda i: (i, 0)), pl.BlockSpec((1, LANES), lambda i: (0, 0))],
        out_shape=[jax.ShapeDtypeStruct((n, TOP_K), I32), jax.ShapeDtypeStruct((1, LANES), F32)],
        scratch_shapes=[pltpu.VMEM((1, LANES), F32)],
        compiler_params=_cparams(("arbitrary",)),
        name="expert_rank",
    )(top_idx, _strict_tri(tm, lower=True))


DISPATCH_TOKENS = 256


def _dispatch_kernel(dest_ref, pad_ref, h_ref, xs_ref, zero_ref, sem, *, n_pad):
    n_copies = DISPATCH_TOKENS * TOP_K + n_pad
    zero_ref[...] = jnp.zeros_like(zero_ref)

    def fill(j, c):
        for r in range(2):
            dst = pl.multiple_of(pad_ref[0, 0, 2 * j + r] * TOKEN_TILE, TOKEN_TILE)
            pltpu.make_async_copy(zero_ref, xs_ref.at[pl.ds(dst, TOKEN_TILE)], sem).start(priority=r)
        return c

    lax.fori_loop(0, n_pad // 2, fill, 0, unroll=2)

    def start(t, c):
        src = pl.multiple_of(t * TOKEN_TILE, TOKEN_TILE)
        for k in range(TOP_K):
            dst = pl.multiple_of(dest_ref[0, 0, t * TOP_K + k] * TOKEN_TILE, TOKEN_TILE)
            pltpu.make_async_copy(h_ref.at[pl.ds(src, TOKEN_TILE)], xs_ref.at[pl.ds(dst, TOKEN_TILE)],
                                  sem).start(priority=k % 2)
        return c

    lax.fori_loop(0, DISPATCH_TOKENS, start, 0, unroll=2)
    total = n_copies * TOKEN_TILE
    pltpu.make_async_copy(xs_ref.at[pl.ds(0, total)], xs_ref.at[pl.ds(0, total)], sem).wait()


def _dispatch(dest, pad_slots, h2t, n_slots):
    n = h2t.shape[0] // TOKEN_TILE
    steps = n // DISPATCH_TOKENS
    per = DISPATCH_TOKENS * TOP_K
    n_pad = pad_slots.shape[0] // steps
    assert n_pad * steps == pad_slots.shape[0] and n_pad % 2 == 0
    return pl.pallas_call(
        functools.partial(_dispatch_kernel, n_pad=n_pad),
        grid=(steps,),
        in_specs=[pl.BlockSpec((1, 1, per), lambda i: (i, 0, 0), memory_space=pltpu.SMEM),
                  pl.BlockSpec((1, 1, n_pad), lambda i: (i, 0, 0), memory_space=pltpu.SMEM),
                  pl.BlockSpec((DISPATCH_TOKENS * TOKEN_TILE, LANES), lambda i: (i, 0))],
        out_specs=pl.BlockSpec(memory_space=pl.ANY),
        out_shape=jax.ShapeDtypeStruct((n_slots * TOKEN_TILE, LANES), F32),
        scratch_shapes=[pltpu.VMEM((TOKEN_TILE, LANES), F32), pltpu.SemaphoreType.DMA(())],
        compiler_params=_cparams(("arbitrary",)),
        name="moe_dispatch",
    )(dest.reshape(steps, 1, per), pad_slots.reshape(steps, 1, n_pad), h2t)


def _expert_kernel(blk_e_ref, nused_ref, x_ref, w1_ref, b1_ref, w2_ref, b2_ref, y_ref, xb_ref):
    del blk_e_ref

    @pl.when(pl.program_id(0) < nused_ref[0])
    def _():
        for s in range(TOKEN_TILE):
            xb_ref[:, s * LANES:(s + 1) * LANES] = _load_token_tiles(x_ref, 0, MOE_BLOCK, s).astype(BF16)
        hb = _dot(xb_ref[...], w1_ref[0]) + b1_ref[0]
        g = jnp.minimum(hb[:, :D_EXPERT], SWIGLU_LIMIT)
        u = jnp.clip(hb[:, D_EXPERT:], -SWIGLU_LIMIT, SWIGLU_LIMIT)
        act = (u + 1.0) * (g * jax.nn.sigmoid(SWIGLU_ALPHA * g))
        _store_token_tiles(y_ref, _dot(act.astype(BF16), w2_ref[0]) + b2_ref[0])

    @pl.when(pl.program_id(0) >= nused_ref[0])
    def _():
        y_ref[...] = jnp.zeros_like(y_ref)


def _experts(blk_e, n_used, xs, w_e_in, b_e_in, w_e_out, b_e_out):
    n_slots = xs.shape[0] // TOKEN_TILE
    nblk = n_slots // MOE_BLOCK
    slot_block = pl.BlockSpec((MOE_BLOCK * TOKEN_TILE, LANES), lambda i, be, nu: (i, 0))
    grid_spec = pltpu.PrefetchScalarGridSpec(
        num_scalar_prefetch=2,
        grid=(nblk,),
        in_specs=[slot_block,
                  pl.BlockSpec((1, D_MODEL, 2 * D_EXPERT), lambda i, be, nu: (be[i], 0, 0)),
                  pl.BlockSpec((1, 1, 2 * D_EXPERT), lambda i, be, nu: (be[i], 0, 0)),
                  pl.BlockSpec((1, D_EXPERT, D_MODEL), lambda i, be, nu: (be[i], 0, 0)),
                  pl.BlockSpec((1, 1, D_MODEL), lambda i, be, nu: (be[i], 0, 0))],
        out_specs=slot_block,
        scratch_shapes=[pltpu.VMEM((MOE_BLOCK, D_MODEL), BF16)],
    )
    return pl.pallas_call(
        _expert_kernel,
        grid_spec=grid_spec,
        out_shape=jax.ShapeDtypeStruct((n_slots * TOKEN_TILE, LANES), F32),
        compiler_params=_cparams(("arbitrary",)),
        name="moe_experts",
    )(blk_e, n_used, xs, w_e_in.astype(BF16), b_e_in[:, None, :], w_e_out.astype(BF16), b_e_out[:, None, :])


def _combine_kernel(dest_ref, x_ref, gate_ref, y_ref, o_ref, buf_ref, sem):
    n_copies = DISPATCH_TOKENS * TOP_K

    def start(t, c):
        for k in range(TOP_K):
            src = pl.multiple_of(dest_ref[0, 0, t * TOP_K + k] * TOKEN_TILE, TOKEN_TILE)
            dst = pl.multiple_of((k * DISPATCH_TOKENS + t) * TOKEN_TILE, TOKEN_TILE)
            pltpu.make_async_copy(y_ref.at[pl.ds(src, TOKEN_TILE)], buf_ref.at[pl.ds(dst, TOKEN_TILE)],
                                  sem).start(priority=k % 2)
        return c

    lax.fori_loop(0, DISPATCH_TOKENS, start, 0, unroll=2)
    pltpu.make_async_copy(y_ref.at[pl.ds(0, n_copies * TOKEN_TILE)], buf_ref, sem).wait()
    gate = gate_ref[...]
    gates = [jnp.broadcast_to(gate[:, k:k + 1], (DISPATCH_TOKENS, LANES)) for k in range(TOP_K)]
    for s in range(TOKEN_TILE):
        out = x_ref[:, s * LANES:(s + 1) * LANES]
        for k in range(TOP_K):
            out = out + gates[k] * _load_token_tiles(buf_ref, k * DISPATCH_TOKENS, DISPATCH_TOKENS, s)
        o_ref[:, s * LANES:(s + 1) * LANES] = out


def _combine(dest, x2, gate, ys):
    n = x2.shape[0]
    per = DISPATCH_TOKENS * TOP_K
    return pl.pallas_call(
        _combine_kernel,
        grid=(n // DISPATCH_TOKENS,),
        in_specs=[pl.BlockSpec((1, 1, per), lambda i: (i, 0, 0), memory_space=pltpu.SMEM),
                  pl.BlockSpec((DISPATCH_TOKENS, D_MODEL), lambda i: (i, 0)),
                  pl.BlockSpec((DISPATCH_TOKENS, TOP_K), lambda i: (i, 0)),
                  pl.BlockSpec(memory_space=pl.ANY)],
        out_specs=pl.BlockSpec((DISPATCH_TOKENS, D_MODEL), lambda i: (i, 0)),
        out_shape=jax.ShapeDtypeStruct((n, D_MODEL), F32),
        scratch_shapes=[pltpu.VMEM((TOP_K * DISPATCH_TOKENS * TOKEN_TILE, LANES), F32),
                        pltpu.SemaphoreType.DMA(())],
        compiler_params=_cparams(("arbitrary",)),
        name="moe_combine",
    )(dest.reshape(n // DISPATCH_TOKENS, 1, per), x2, gate, ys)


def _moe(x2, h2, top_idx, gate, w_e_in, b_e_in, w_e_out, b_e_out):
    n = x2.shape[0]
    rank, counts = _expert_rank(top_idx, 512)
    counts = counts[0, :N_EXPERTS].astype(I32)
    padded = (counts + MOE_BLOCK - 1) // MOE_BLOCK * MOE_BLOCK
    pend = jnp.cumsum(padded)
    pstart = pend - padded
    nblk = -(-(n * TOP_K) // MOE_BLOCK) + N_EXPERTS
    blk_start = jnp.arange(nblk, dtype=I32) * MOE_BLOCK
    blk_e = jnp.minimum(jnp.sum((pend[None, :] <= blk_start[:, None]).astype(I32), axis=1), N_EXPERTS - 1)
    n_used = (pend[-1:] // MOE_BLOCK).astype(I32)
    onehot = top_idx[:, :, None] == jnp.arange(N_EXPERTS, dtype=I32)[None, None, :]
    dest = rank + jnp.sum(jnp.where(onehot, pstart[None, None, :], 0), axis=-1)
    n_slots = nblk * MOE_BLOCK
    pad_len = padded - counts
    pad_end = jnp.cumsum(pad_len)
    base = jnp.concatenate([pstart + counts - (pad_end - pad_len), pend[-1:] - pad_end[-1:]])
    j = jnp.arange(n_slots - n * TOP_K, dtype=I32)
    group = jnp.sum((pad_end[None, :] <= j[:, None]).astype(I32), axis=1)
    group_hot = group[:, None] == jnp.arange(N_EXPERTS + 1, dtype=I32)[None, :]
    pad_slots = j + jnp.sum(jnp.where(group_hot, base[None, :], 0), axis=1)
    xs = _dispatch(dest, pad_slots, h2, n_slots)
    ys = _experts(blk_e, n_used, xs, w_e_in, b_e_in, w_e_out, b_e_out)
    return _combine(dest, x2, gate, ys)


def _layer(x, mem, g_mix, w_in, g_q_dsa, g_k_dsa, g_q_mem, g_k_mem, g_mem, w_mem_kv, w_gate, b_gate,
           w_branch, w_out, g_ffn, w_router, b_router, w_e_in, b_e_in, w_e_out, b_e_out):
    batch, seq, _ = x.shape
    n = batch * seq
    topk = min(DSA_TOPK_MAX, seq // 4)
    x2d = x.reshape(n, D_MODEL)
    tm = min(512, seq)
    sq, sk, sv, dq, iq, kvi, iw, mq, vt = _inproj(x2d, batch, seq, g_mix, w_in, g_q_dsa, g_k_dsa, g_q_mem, tm)
    o_sb = _sb_attention(sq, sk, sv, min(256, seq))
    o_dsa = _dsa_attention(dq, iq, iw, kvi, vt, 128, topk)
    mlen = mem.shape[1]
    mk, mv = _mem_kv(mem.reshape(batch * mlen, D_MODEL), g_mem, w_mem_kv, g_k_mem, min(512, batch * mlen))
    mw = MEM_HEADS * MEM_HEAD_DIM
    o_mem = _mem_attention(mq.reshape(batch, seq, mw), mk.reshape(batch, mlen, mw), mv.reshape(batch, mlen, mw), tm)
    x2, h2, top_idx, gate = _merge_route(
        x2d, g_mix, o_sb.reshape(n, BRANCH_W), o_dsa.reshape(n, BRANCH_W), o_mem.reshape(n, mw),
        w_gate, b_gate, w_branch, w_out, g_ffn, w_router, b_router, tm)
    out = _moe(x2, h2, top_idx, gate, w_e_in, b_e_in, w_e_out, b_e_out)
    return out.reshape(batch, seq, D_MODEL)


def kernel(x, mem, g_mix, w_in, g_q_dsa, g_k_dsa, g_q_mem, g_k_mem, g_mem, w_mem_kv, w_gate, b_gate, w_branch, w_out, g_ffn, w_router, b_router, w_e_in, b_e_in, w_e_out, b_e_out):
    for l in range(g_mix.shape[0]):
        x = _layer(x, mem, g_mix[l], w_in[l], g_q_dsa[l], g_k_dsa[l], g_q_mem[l], g_k_mem[l], g_mem[l],
                   w_mem_kv[l], w_gate[l], b_gate[l], w_branch[l], w_out[l], g_ffn[l], w_router[l],
                   b_router[l], w_e_in[l], b_e_in[l], w_e_out[l], b_e_out[l])
    return x
```

```python
import functools

import numpy as np
import jax
import jax.numpy as jnp
from jax import lax
from jax.experimental import pallas as pl
from jax.experimental.pallas import tpu as pltpu

F32 = jnp.float32
BF16 = jnp.bfloat16
I32 = jnp.int32

D_MODEL = 1024
CHUNK = 64
SB_HEADS = 8
DSA_HEADS = 8
HEAD_DIM = 64
IDX_HEADS = 4
DSA_TOPK_MAX = 256
MEM_HEADS = 4
MEM_HEAD_DIM = 128
N_EXPERTS = 32
TOP_K = 4
D_EXPERT = D_MODEL
SWIGLU_LIMIT = 7.0
SWIGLU_ALPHA = 1.702
ROPE_THETA = 10000.0
EPS = 1e-6
MOE_BLOCK = 512

BRANCH_W = 512
IN_SIZES = (512, 512, 512, 512, 64, 64, 256, 64, 4, 512)
C_SQ, C_SK, C_SV, C_DQ, C_MQ, C_IQ, C_SMALL, C_END = 0, 512, 1024, 1536, 2048, 2560, 2816, 3072
IW_LANE = 64

LANES = 128
NEG_BIG = -1e30
SB_CUTOFF = 110.0
KEY_NEG_INF = int(np.array(-np.inf, np.float32).view(np.int32)) ^ 0x7FFFFFFF
INT_MIN = -(2 ** 31)

VMEM_LIMIT = 56 * 1024 * 1024


def _cparams(sem):
    return pltpu.CompilerParams(dimension_semantics=sem, vmem_limit_bytes=VMEM_LIMIT)


def _dot(a, b):
    return jnp.dot(a, b, preferred_element_type=F32)


def _dot_nt(a, b):
    return lax.dot_general(a, b, (((1,), (1,)), ((), ())), preferred_element_type=F32)


def _split_bf16(x):
    hi = x.astype(BF16)
    lo = (x - hi.astype(F32)).astype(BF16)
    return hi, lo


def _dot_split(x, m_bf16):
    hi, lo = _split_bf16(x)
    return _dot(hi, m_bf16) + _dot(lo, m_bf16)


TOKEN_TILE = D_MODEL // LANES


def _store_token_tiles(ref, y):
    rows = y.shape[0]
    for s in range(TOKEN_TILE):
        ref[pl.ds(s, rows, stride=TOKEN_TILE), :] = y[:, s * LANES:(s + 1) * LANES]


def _load_token_tiles(ref, start_row, rows, s):
    return ref[pl.ds(start_row * TOKEN_TILE + s, rows, stride=TOKEN_TILE), :]


def _rot_half_unsigned(y):
    w = y.shape[1]
    lane = lax.broadcasted_iota(I32, y.shape, 1)
    return jnp.where((lane & 32) == 0, pltpu.roll(y, w - 32, 1), pltpu.roll(y, 32, 1))


def _inproj_kernel(x_ref, g_ref, w_ref, wsvt_ref, cos_ref, sin_ref, coss_ref, sins_ref, gq_ref, gks_ref,
                   gm_ref, bd64_ref, bd128_ref,
                   sq_ref, sk_ref, sv_ref, dq_ref, iq_ref, kvi_ref, iw_ref, mq_ref, vt_ref):
    x = x_ref[...]
    ms = jnp.mean(x * x, axis=-1, keepdims=True)
    h = (x * lax.rsqrt(ms + EPS) * g_ref[...]).astype(BF16)

    def seg(a, b):
        return _dot(h, w_ref[:, a:b])

    def put_heads(ref, y):
        for hd in range(y.shape[1] // HEAD_DIM):
            ref[0, hd] = y[:, hd * HEAD_DIM:(hd + 1) * HEAD_DIM].astype(BF16)

    put_heads(sq_ref, seg(C_SQ, C_SK) * (HEAD_DIM ** -0.5))
    put_heads(sk_ref, seg(C_SK, C_SV))
    sv_ref[0] = _dot_nt(wsvt_ref[...], h).astype(BF16)

    y = seg(C_DQ, C_MQ)
    msq = _dot_split(y * y, bd64_ref[...]) * (1.0 / HEAD_DIM)
    y = y * lax.rsqrt(msq + EPS) * gq_ref[...]
    y = y * cos_ref[...] + _rot_half_unsigned(y) * sin_ref[...]
    put_heads(dq_ref, y * (HEAD_DIM ** -0.5))

    y = seg(C_IQ, C_SMALL)
    y = y * cos_ref[:, :256] + _rot_half_unsigned(y) * sin_ref[:, :256]
    iq_ref[...] = (y * (HEAD_DIM ** -0.5)).astype(BF16)

    y = seg(C_SMALL, C_END)
    lane = lax.broadcasted_iota(I32, y.shape, 1)
    is_k = lane < HEAD_DIM
    msk = jnp.sum(jnp.where(is_k, y * y, 0.0), axis=-1, keepdims=True) * (1.0 / HEAD_DIM)
    y = y * jnp.where(is_k, lax.rsqrt(msk + EPS) * gks_ref[...], 1.0)
    y = y * coss_ref[...] + _rot_half_unsigned(y) * sins_ref[...]
    kvi_ref[...] = y.astype(BF16)
    iw_ref[...] = y[:, 128:256]
    kv_t = y[:, 0:128].T
    row_t = lax.broadcasted_iota(I32, kv_t.shape, 0)
    vt_ref[0] = jnp.where(row_t < HEAD_DIM, 1.0, kv_t).astype(BF16)

    y = seg(C_MQ, C_IQ)
    msm = _dot_split(y * y, bd128_ref[...]) * (1.0 / MEM_HEAD_DIM)
    mq_ref[...] = (y * lax.rsqrt(msm + EPS) * gm_ref[...]).astype(BF16)


def _rope_tables(seq):
    half = HEAD_DIM // 2
    inv = ROPE_THETA ** (-jnp.arange(half, dtype=F32) / half)
    ang = jnp.arange(seq).astype(F32)[:, None] * inv[None, :]
    cos = jnp.cos(ang)
    sin = jnp.sin(ang)
    cos64 = jnp.concatenate([cos, cos], axis=1)
    sin64 = jnp.concatenate([-sin, sin], axis=1)
    one = jnp.ones_like(cos64)
    zero = jnp.zeros_like(cos64)
    cosq = jnp.tile(cos64, (1, 8))
    sinq = jnp.tile(sin64, (1, 8))
    coss = jnp.concatenate([cos64, one, cos64, one], axis=1)
    sins = jnp.concatenate([sin64, zero, sin64, zero], axis=1)
    return cosq, sinq, coss, sins


def _block_diag_ones(width, group):
    idx = np.arange(width) // group
    return jnp.asarray((idx[:, None] == idx[None, :]).astype(np.float32), dtype=BF16)


def _inproj(x2d, batch, seq, g_mix, w_in, g_q_dsa, g_k_dsa, g_q_mem, tm):
    n = x2d.shape[0]
    sizes = np.cumsum((0,) + IN_SIZES)
    col = {name: (int(sizes[i]), int(sizes[i + 1])) for i, name in enumerate(
        ("sq", "sk", "sv", "dq", "dk", "dv", "iq", "ik", "iw", "mq"))}
    order = ("sq", "sk", "sv", "dq", "mq", "iq", "dk", "dv", "ik", "iw")
    w = jnp.concatenate([w_in[:, col[k][0]:col[k][1]] for k in order]
                        + [jnp.zeros((D_MODEL, C_END - sum(IN_SIZES)), w_in.dtype)], axis=1).astype(BF16)
    cosq, sinq, coss, sins = (jnp.asarray(t) for t in _rope_tables(seq))
    gq = jnp.tile(g_q_dsa, 8)[None, :]
    gks = jnp.concatenate([g_k_dsa, jnp.ones((256 - HEAD_DIM,), F32)])[None, :]
    gm = jnp.tile(g_q_mem, MEM_HEADS)[None, :]
    spb = seq // tm
    row = lambda i: (i, 0)
    const = lambda i: (0, 0)
    pos = lambda i: (i % spb, 0)
    heads = lambda i: (i // spb, 0, i % spb, 0)
    head_shape = jax.ShapeDtypeStruct((batch, 8, seq, HEAD_DIM), BF16)
    head_spec = pl.BlockSpec((1, 8, tm, HEAD_DIM), heads)
    return pl.pallas_call(
        _inproj_kernel,
        grid=(n // tm,),
        in_specs=[
            pl.BlockSpec((tm, D_MODEL), row),
            pl.BlockSpec((1, D_MODEL), const),
            pl.BlockSpec((D_MODEL, C_END), const),
            pl.BlockSpec((512, D_MODEL), const),
            pl.BlockSpec((tm, 512), pos), pl.BlockSpec((tm, 512), pos),
            pl.BlockSpec((tm, 256), pos), pl.BlockSpec((tm, 256), pos),
            pl.BlockSpec((1, 512), const), pl.BlockSpec((1, 256), const), pl.BlockSpec((1, 512), const),
            pl.BlockSpec((512, 512), const), pl.BlockSpec((512, 512), const),
        ],
        out_specs=[head_spec, head_spec, pl.BlockSpec((1, 512, tm), lambda i: (i // spb, 0, i % spb)), head_spec,
                   pl.BlockSpec((tm, 256), row), pl.BlockSpec((tm, 256), row),
                   pl.BlockSpec((tm, 128), row), pl.BlockSpec((tm, 512), row),
                   pl.BlockSpec((1, 128, tm), lambda i: (i // spb, 0, i % spb))],
        out_shape=[head_shape, head_shape, jax.ShapeDtypeStruct((batch, 512, seq), BF16), head_shape,
                   jax.ShapeDtypeStruct((n, 256), BF16), jax.ShapeDtypeStruct((n, 256), BF16),
                   jax.ShapeDtypeStruct((n, 128), F32), jax.ShapeDtypeStruct((n, 512), BF16),
                   jax.ShapeDtypeStruct((batch, 128, seq), BF16)],
        compiler_params=_cparams(("parallel",)),
        name="inproj",
    )(x2d, g_mix[None, :], w, w_in[:, col["sv"][0]:col["sv"][1]].T.astype(BF16), cosq, sinq, coss, sins, gq, gks, gm,
      _block_diag_ones(512, HEAD_DIM), _block_diag_ones(512, MEM_HEAD_DIM))


def _sb_kernel(q_ref, k_ref, vt_ref, u_ref, o_ref, acc_ref, car_ref, *, tq):
    qi = pl.program_id(1)
    rows = lax.broadcasted_iota(I32, (tq, tq), 0)
    cols = lax.broadcasted_iota(I32, (tq, tq), 1)
    dif = rows - cols
    u = u_ref[...]
    acc_ref[...] = jnp.zeros_like(acc_ref)
    car_ref[...] = jnp.zeros_like(car_ref)

    def cond(c):
        kb, mx = c
        return jnp.logical_and(kb >= 0, mx > -SB_CUTOFF)

    def body(c):
        kb, _ = c
        ks = pl.multiple_of(kb * tq, tq)
        earlier = dif < (qi - kb) * tq
        neg_mask = jnp.where(earlier, -1.0, 0.0).astype(BF16)
        heads = range(SB_HEADS)
        z = [_dot_nt(k_ref[0, hd, pl.ds(ks, tq), :], q_ref[0, hd]) for hd in heads]
        ls, lk, between = [], [], []
        for hd in heads:
            zb = z[hd].astype(BF16)
            sp = jnp.maximum(zb, 0.0) + jnp.log(1.0 + jnp.exp(-jnp.abs(zb)))
            ls.append(z[hd] - sp.astype(F32))
            lk.append(sp * neg_mask)
            between.append(_dot(u, lk[hd]))
        for hd in heads:
            rs = slice(hd * HEAD_DIM, (hd + 1) * HEAD_DIM)
            car = car_ref[hd:hd + 1, :]
            w = jnp.where(earlier, jnp.exp(ls[hd] + between[hd] + car), 0.0)
            acc_ref[rs, :] += _dot(vt_ref[0, rs, pl.ds(ks, tq)], w.astype(BF16))
            car_ref[hd:hd + 1, :] = car + (between[hd][0:1, :] + lk[hd][0:1, :].astype(F32))
        return kb - 1, jnp.max(car_ref[...])

    lax.while_loop(cond, body, (qi, jnp.float32(0.0)))
    o_ref[0] = acc_ref[...].T.astype(BF16)


def _strict_tri(n, lower):
    i = np.arange(n)
    m = (i[:, None] > i[None, :]) if lower else (i[:, None] < i[None, :])
    return jnp.asarray(m.astype(np.float32), dtype=BF16)


def _sb_attention(sq, sk, svt, tq):
    batch, _, seq, _ = sq.shape
    return pl.pallas_call(
        functools.partial(_sb_kernel, tq=tq),
        grid=(batch, seq // tq),
        in_specs=[pl.BlockSpec((1, SB_HEADS, tq, HEAD_DIM), lambda b, i: (b, 0, i, 0)),
                  pl.BlockSpec((1, SB_HEADS, seq, HEAD_DIM), lambda b, i: (b, 0, 0, 0)),
                  pl.BlockSpec((1, BRANCH_W, seq), lambda b, i: (b, 0, 0)),
                  pl.BlockSpec((tq, tq), lambda b, i: (0, 0))],
        out_specs=pl.BlockSpec((1, tq, BRANCH_W), lambda b, i: (b, i, 0)),
        out_shape=jax.ShapeDtypeStruct((batch, seq, BRANCH_W), BF16),
        scratch_shapes=[pltpu.VMEM((BRANCH_W, tq), F32), pltpu.VMEM((SB_HEADS, tq), F32)],
        compiler_params=_cparams(("parallel", "parallel")),
        name="sb_attention",
    )(sq, sk, svt, _strict_tri(tq, lower=False))


DSA_SEG = 512
DSA_KB = 512
DSA_GROUP_HEADS = 8


def _tree_sum(parts):
    while len(parts) > 1:
        parts = [parts[i] + parts[i + 1] for i in range(0, len(parts) - 1, 2)] + ([parts[-1]] if len(parts) % 2 else [])
    return parts[0]


def _dsa_kernel(dq_ref, iq_ref, iw_ref, kvi_ref, vt_ref, tri_ref, o_ref,
                sc_ref, bias_ref, s_ref, *, tq, topk, nseg_max):
    nseg = (pl.program_id(1) * tq) // DSA_SEG + 1
    for ns in range(1, nseg_max + 1):
        @pl.when(nseg == ns)
        def _(ns=ns):
            _dsa_body(dq_ref, iq_ref, iw_ref, kvi_ref, vt_ref, tri_ref, o_ref, sc_ref, bias_ref, s_ref,
                      tq=tq, topk=topk, nseg=ns)


def _dsa_body(dq_ref, iq_ref, iw_ref, kvi_ref, vt_ref, tri_ref, o_ref, sc_ref, bias_ref, s_ref, *, tq, topk, nseg):
    qs = pl.program_id(1) * tq
    blocks = [slice(c * DSA_KB, (c + 1) * DSA_KB) for c in range(nseg)]
    iq = iq_ref[0]
    w_t = iw_ref[0].T
    w_row = [w_t[IW_LANE + h:IW_LANE + h + 1, :] * (IDX_HEADS ** -0.5) for h in range(IDX_HEADS)]
    q_chunk = (qs + lax.broadcasted_iota(I32, (DSA_KB, tq), 1)) // CHUNK
    k_chunk = lax.broadcasted_iota(I32, (DSA_KB, tq), 0) // CHUNK

    for c, blk in enumerate(blocks):
        ik = kvi_ref[0, blk, 128:192]
        lg = [_dot_nt(ik, iq[:, h * HEAD_DIM:(h + 1) * HEAD_DIM]) for h in range(IDX_HEADS)]
        sc = jnp.zeros((DSA_KB, tq), F32)
        for h in range(IDX_HEADS):
            sc = sc + w_row[h] * jnp.maximum(lg[h], 0.0)
        admissible = (c * (DSA_KB // CHUNK) + k_chunk) <= q_chunk
        sc_ref[blk, :] = jnp.where(admissible, sc, -jnp.inf)

    def count(pred_fn):
        sub, lanes_of_sums = 32, 4
        acc = [jnp.zeros((sub, tq), F32)] * lanes_of_sums
        for j in range(nseg * DSA_SEG // sub):
            acc[j % lanes_of_sums] = acc[j % lanes_of_sums] + jnp.where(
                pred_fn(sc_ref[j * sub:(j + 1) * sub, :]), 1.0, 0.0)
        return jnp.sum(_tree_sum(acc), axis=0, keepdims=True)

    def key_to_float(key):
        return lax.bitcast_convert_type(jnp.where(key >= 0, key, key ^ 0x7FFFFFFF), F32)

    kf = jnp.float32(topk)
    n_rows = jnp.float32(nseg * DSA_SEG)
    t0 = jnp.where(count(lambda s: s >= 0.0) >= kf, 0, INT_MIN).astype(I32)

    def bit_step(i, t):
        cand = t + lax.shift_left(jnp.int32(1), 30 - i)
        cand_f = key_to_float(cand)
        cnt = jnp.where(cand <= KEY_NEG_INF, n_rows, count(lambda s: s >= cand_f))
        return jnp.where(cnt >= kf, cand, t)

    thr_key = lax.fori_loop(0, 31, bit_step, t0)
    thr = jnp.where(thr_key <= KEY_NEG_INF, -jnp.inf, key_to_float(thr_key))
    need = kf - count(lambda s: s > thr)
    tri = tri_ref[...]

    surplus = jnp.max(count(lambda s: s >= thr)) > kf

    @pl.when(surplus)
    def _():
        prefix = jnp.zeros((1, tq), F32)
        for blk in blocks:
            sc = sc_ref[blk, :]
            eqf = jnp.where(sc == thr, 1.0, 0.0)
            rank = _dot(tri, eqf.astype(BF16)) + prefix
            tie = jnp.where(rank < need, eqf, 0.0)
            sel = jnp.where(sc > thr, 1.0, tie)
            bias_ref[blk, :] = jnp.where(sc > -jnp.inf, (sel - 1.0) * (-NEG_BIG), NEG_BIG)
            prefix = prefix + jnp.sum(eqf, axis=0, keepdims=True)

    @pl.when(jnp.logical_not(surplus))
    def _():
        for blk in blocks:
            bias_ref[blk, :] = jnp.where(sc_ref[blk, :] >= thr, 0.0, NEG_BIG)

    q8 = dq_ref[0].reshape(DSA_HEADS * tq, HEAD_DIM)
    m = jnp.full((1, DSA_HEADS * tq), NEG_BIG, F32)
    for blk in blocks:
        b = bias_ref[blk, :]
        s = _dot_nt(kvi_ref[0, blk, 0:HEAD_DIM], q8) + jnp.concatenate([b] * DSA_HEADS, axis=1)
        s_ref[blk, :] = s
        m = jnp.maximum(m, jnp.max(s, axis=0, keepdims=True))
    acc = jnp.zeros((128, DSA_HEADS * tq), F32)
    for blk in blocks:
        p = jnp.exp((s_ref[blk, :] - m).astype(BF16))
        acc = acc + _dot(vt_ref[0, :, blk], p)
    for hd in range(DSA_HEADS):
        a = acc[:, hd * tq:(hd + 1) * tq]
        o = (a / a[0:1, :]).T
        o_ref[0, :, hd * HEAD_DIM:(hd + 1) * HEAD_DIM] = o[:, HEAD_DIM:].astype(BF16)


def _dsa_attention(dq, iq, iw, kvi, vt, tq, topk):
    batch, _, seq, _ = dq.shape
    return pl.pallas_call(
        functools.partial(_dsa_kernel, tq=tq, topk=topk, nseg_max=seq // DSA_SEG),
        grid=(batch, seq // tq),
        in_specs=[
            pl.BlockSpec((1, DSA_HEADS, tq, HEAD_DIM), lambda b, i: (b, 0, i, 0)),
            pl.BlockSpec((1, tq, 256), lambda b, i: (b, i, 0)),
            pl.BlockSpec((1, tq, 128), lambda b, i: (b, i, 0)),
            pl.BlockSpec((1, seq, 256), lambda b, i: (b, 0, 0)),
            pl.BlockSpec((1, 128, seq), lambda b, i: (b, 0, 0)),
            pl.BlockSpec((DSA_KB, DSA_KB), lambda b, i: (0, 0)),
        ],
        out_specs=pl.BlockSpec((1, tq, BRANCH_W), lambda b, i: (b, i, 0)),
        out_shape=jax.ShapeDtypeStruct((batch, seq, BRANCH_W), BF16),
        scratch_shapes=[pltpu.VMEM((seq, tq), F32), pltpu.VMEM((seq, tq), F32),
                        pltpu.VMEM((seq, DSA_HEADS * tq), F32)],
        compiler_params=_cparams(("parallel", "parallel")),
        name="dsa_attention",
    )(dq, iq.reshape(batch, seq, 256), iw.reshape(batch, seq, 128), kvi.reshape(batch, seq, 256), vt,
      _strict_tri(DSA_KB, lower=True))


def _memkv_kernel(m_ref, g_ref, w_ref, gk_ref, bd_ref, mk_ref, mv_ref):
    x = m_ref[...]
    ms = jnp.mean(x * x, axis=-1, keepdims=True)
    h = (x * lax.rsqrt(ms + EPS) * g_ref[...]).astype(BF16)
    mw = MEM_HEADS * MEM_HEAD_DIM
    k = _dot(h, w_ref[:, :mw])
    msk = _dot_split(k * k, bd_ref[...]) * (1.0 / MEM_HEAD_DIM)
    mk_ref[...] = (k * lax.rsqrt(msk + EPS) * gk_ref[...]).astype(BF16)
    mv_ref[...] = _dot(h, w_ref[:, mw:]).astype(BF16)


def _mem_kv(mem2d, g_mem, w_mem_kv, g_k_mem, tm):
    n = mem2d.shape[0]
    mw = MEM_HEADS * MEM_HEAD_DIM
    row = lambda i: (i, 0)
    const = lambda i: (0, 0)
    return pl.pallas_call(
        _memkv_kernel,
        grid=(n // tm,),
        in_specs=[pl.BlockSpec((tm, D_MODEL), row), pl.BlockSpec((1, D_MODEL), const),
                  pl.BlockSpec((D_MODEL, 2 * mw), const), pl.BlockSpec((1, mw), const),
                  pl.BlockSpec((mw, mw), const)],
        out_specs=[pl.BlockSpec((tm, mw), row), pl.BlockSpec((tm, mw), row)],
        out_shape=[jax.ShapeDtypeStruct((n, mw), BF16), jax.ShapeDtypeStruct((n, mw), BF16)],
        compiler_params=_cparams(("parallel",)),
        name="mem_kv",
    )(mem2d, g_mem[None, :], w_mem_kv.astype(BF16), jnp.tile(g_k_mem, MEM_HEADS)[None, :],
      _block_diag_ones(mw, MEM_HEAD_DIM))


def _memattn_kernel(q_ref, k_ref, v_ref, o_ref):
    for hd in range(MEM_HEADS):
        sl = slice(hd * MEM_HEAD_DIM, (hd + 1) * MEM_HEAD_DIM)
        s = _dot_nt(q_ref[0, :, sl], k_ref[0, :, sl]) * (MEM_HEAD_DIM ** -0.5)
        p = jnp.exp(s - jnp.max(s, axis=-1, keepdims=True))
        o = _dot(p.astype(BF16), v_ref[0, :, sl]) / jnp.sum(p, axis=-1, keepdims=True)
        o_ref[0, :, sl] = o.astype(BF16)


def _mem_attention(mq, mk, mv, tq):
    batch, seq, mw = mq.shape
    mlen = mk.shape[1]
    kspec = pl.BlockSpec((1, mlen, mw), lambda b, i: (b, 0, 0))
    return pl.pallas_call(
        _memattn_kernel,
        grid=(batch, seq // tq),
        in_specs=[pl.BlockSpec((1, tq, mw), lambda b, i: (b, i, 0)), kspec, kspec],
        out_specs=pl.BlockSpec((1, tq, mw), lambda b, i: (b, i, 0)),
        out_shape=jax.ShapeDtypeStruct((batch, seq, mw), BF16),
        compiler_params=_cparams(("parallel", "parallel")),
        name="mem_attention",
    )(mq, mk, mv)


def _merge_kernel(x_ref, g_ref, osb_ref, odsa_ref, omem_ref, wg_ref, bg_ref, wb_ref, wo_ref,
                  gf_ref, wrh_ref, wrl_ref, br_ref,
                  x2_ref, h2_ref, idx_ref, gate_ref):
    x = x_ref[...]
    ms = jnp.mean(x * x, axis=-1, keepdims=True)
    h = (x * lax.rsqrt(ms + EPS) * g_ref[...]).astype(BF16)
    merged = None
    for n, o_ref in enumerate((osb_ref, odsa_ref, omem_ref)):
        gate = jax.nn.sigmoid(_dot(h, wg_ref[n]) + bg_ref[n])
        term = gate * _dot(o_ref[...], wb_ref[n])
        merged = term if merged is None else merged + term
    x2 = x + _dot(merged.astype(BF16), wo_ref[...])
    x2_ref[...] = x2

    ms2 = jnp.mean(x2 * x2, axis=-1, keepdims=True)
    h2 = x2 * lax.rsqrt(ms2 + EPS) * gf_ref[...]
    _store_token_tiles(h2_ref, h2)
    hi, lo = _split_bf16(h2)
    logits = _dot(hi, wrh_ref[...]) + _dot(hi, wrl_ref[...]) + _dot(lo, wrh_ref[...]) + br_ref[...]

    lane = lax.broadcasted_iota(I32, logits.shape, 1)
    vals = logits
    top_v, top_i = [], []
    for _ in range(TOP_K):
        mx = jnp.max(vals, axis=-1, keepdims=True)
        ix = jnp.min(jnp.where(vals == mx, lane, N_EXPERTS), axis=-1, keepdims=True)
        top_v.append(mx)
        top_i.append(ix)
        vals = jnp.where(lane == ix, -jnp.inf, vals)
    ex = [jnp.exp(v - top_v[0]) for v in top_v]
    den = ex[0] + ex[1] + ex[2] + ex[3]
    for k in range(TOP_K):
        idx_ref[:, k:k + 1] = top_i[k]
        gate_ref[:, k:k + 1] = ex[k] / den


def _merge_route(x2d, g_mix, o_sb, o_dsa, o_mem, w_gate, b_gate, w_branch, w_out, g_ffn, w_router, b_router, tm):
    n = x2d.shape[0]
    row = lambda i: (i, 0)
    c2 = lambda i: (0, 0)
    c3 = lambda i: (0, 0, 0)
    wr_hi = w_router.astype(BF16)
    wr_lo = (w_router - wr_hi.astype(F32)).astype(BF16)
    return pl.pallas_call(
        _merge_kernel,
        grid=(n // tm,),
        in_specs=[pl.BlockSpec((tm, D_MODEL), row), pl.BlockSpec((1, D_MODEL), c2),
                  pl.BlockSpec((tm, BRANCH_W), row), pl.BlockSpec((tm, BRANCH_W), row),
                  pl.BlockSpec((tm, BRANCH_W), row),
                  pl.BlockSpec((3, D_MODEL, D_MODEL), c3), pl.BlockSpec((3, 1, D_MODEL), c3),
                  pl.BlockSpec((3, BRANCH_W, D_MODEL), c3), pl.BlockSpec((D_MODEL, D_MODEL), c2),
                  pl.BlockSpec((1, D_MODEL), c2), pl.BlockSpec((D_MODEL, N_EXPERTS), c2),
                  pl.BlockSpec((D_MODEL, N_EXPERTS), c2), pl.BlockSpec((1, N_EXPERTS), c2)],
        out_specs=[pl.BlockSpec((tm, D_MODEL), row), pl.BlockSpec((tm * TOKEN_TILE, LANES), row),
                   pl.BlockSpec((tm, TOP_K), row), pl.BlockSpec((tm, TOP_K), row)],
        out_shape=[jax.ShapeDtypeStruct((n, D_MODEL), F32), jax.ShapeDtypeStruct((n * TOKEN_TILE, LANES), F32),
                   jax.ShapeDtypeStruct((n, TOP_K), I32), jax.ShapeDtypeStruct((n, TOP_K), F32)],
        compiler_params=_cparams(("parallel",)),
        name="merge_route",
    )(x2d, g_mix[None, :], o_sb, o_dsa, o_mem, w_gate.astype(BF16), b_gate[:, None, :],
      w_branch.astype(BF16), w_out.astype(BF16), g_ffn[None, :], wr_hi, wr_lo, b_router[None, :])


def _rank_kernel(idx_ref, l_ref, rank_ref, cnt_ref, carry_ref):
    @pl.when(pl.program_id(0) == 0)
    def _():
        carry_ref[...] = jnp.zeros_like(carry_ref)

    idx = idx_ref[...]
    tm = idx.shape[0]
    lane = lax.broadcasted_iota(I32, (tm, LANES), 1)
    hits = [lane == idx[:, k:k + 1] for k in range(TOP_K)]
    onehot = sum(jnp.where(hk, 1.0, 0.0) for hk in hits)
    before = _dot(l_ref[...], onehot.astype(BF16)) + carry_ref[...]
    for k in range(TOP_K):
        rank_ref[:, k:k + 1] = jnp.sum(jnp.where(hits[k], before, 0.0), axis=-1, keepdims=True).astype(I32)
    carry_ref[...] += jnp.sum(onehot, axis=0, keepdims=True)
    cnt_ref[...] = carry_ref[...]


def _expert_rank(top_idx, tm):
    n = top_idx.shape[0]
    return pl.pallas_call(
        _rank_kernel,
        grid=(n // tm,),
        in_specs=[pl.BlockSpec((tm, TOP_K), lambda i: (i, 0)), pl.BlockSpec((tm, tm), lambda i: (0, 0))],
        out_specs=[pl.BlockSpec((tm, TOP_K), lambda i: (i, 0)), pl.BlockSpec((1, LANES), lambda i: (0, 0))],
        out_shape=[jax.ShapeDtypeStruct((n, TOP_K), I32), jax.ShapeDtypeStruct((1, LANES), F32)],
        scratch_shapes=[pltpu.VMEM((1, LANES), F32)],
        compiler_params=_cparams(("arbitrary",)),
        name="expert_rank",
    )(top_idx, _strict_tri(tm, lower=True))


DISPATCH_TOKENS = 256


def _dispatch_kernel(dest_ref, pad_ref, h_ref, xs_ref, zero_ref, sem, *, n_pad):
    n_copies = DISPATCH_TOKENS * TOP_K + n_pad
    zero_ref[...] = jnp.zeros_like(zero_ref)

    def fill(j, c):
        for r in range(2):
            dst = pl.multiple_of(pad_ref[0, 0, 2 * j + r] * TOKEN_TILE, TOKEN_TILE)
            pltpu.make_async_copy(zero_ref, xs_ref.at[pl.ds(dst, TOKEN_TILE)], sem).start(priority=r)
        return c

    lax.fori_loop(0, n_pad // 2, fill, 0, unroll=2)

    def start(t, c):
        src = pl.multiple_of(t * TOKEN_TILE, TOKEN_TILE)
        for k in range(TOP_K):
            dst = pl.multiple_of(dest_ref[0, 0, t * TOP_K + k] * TOKEN_TILE, TOKEN_TILE)
            pltpu.make_async_copy(h_ref.at[pl.ds(src, TOKEN_TILE)], xs_ref.at[pl.ds(dst, TOKEN_TILE)],
                                  sem).start(priority=k % 2)
        return c

    lax.fori_loop(0, DISPATCH_TOKENS, start, 0, unroll=2)
    total = n_copies * TOKEN_TILE
    pltpu.make_async_copy(xs_ref.at[pl.ds(0, total)], xs_ref.at[pl.ds(0, total)], sem).wait()


def _dispatch(dest, pad_slots, h2t, n_slots):
    n = h2t.shape[0] // TOKEN_TILE
    steps = n // DISPATCH_TOKENS
    per = DISPATCH_TOKENS * TOP_K
    n_pad = pad_slots.shape[0] // steps
    assert n_pad * steps == pad_slots.shape[0] and n_pad % 2 == 0
    return pl.pallas_call(
        functools.partial(_dispatch_kernel, n_pad=n_pad),
        grid=(steps,),
        in_specs=[pl.BlockSpec((1, 1, per), lambda i: (i, 0, 0), memory_space=pltpu.SMEM),
                  pl.BlockSpec((1, 1, n_pad), lambda i: (i, 0, 0), memory_space=pltpu.SMEM),
                  pl.BlockSpec((DISPATCH_TOKENS * TOKEN_TILE, LANES), lambda i: (i, 0))],
        out_specs=pl.BlockSpec(memory_space=pl.ANY),
        out_shape=jax.ShapeDtypeStruct((n_slots * TOKEN_TILE, LANES), F32),
        scratch_shapes=[pltpu.VMEM((TOKEN_TILE, LANES), F32), pltpu.SemaphoreType.DMA(())],
        compiler_params=_cparams(("arbitrary",)),
        name="moe_dispatch",
    )(dest.reshape(steps, 1, per), pad_slots.reshape(steps, 1, n_pad), h2t)


def _expert_kernel(blk_e_ref, nused_ref, x_ref, w1_ref, b1_ref, w2_ref, b2_ref, y_ref, xb_ref, w1b_ref, w2b_ref):
    i = pl.program_id(0)
    new_expert = jnp.logical_or(i == 0, blk_e_ref[i] != blk_e_ref[jnp.maximum(i - 1, 0)])

    @pl.when(jnp.logical_and(i < nused_ref[0], new_expert))
    def _():
        w1b_ref[...] = w1_ref[0].astype(BF16)
        w2b_ref[...] = w2_ref[0].astype(BF16)

    @pl.when(i < nused_ref[0])
    def _():
        for s in range(TOKEN_TILE):
            xb_ref[:, s * LANES:(s + 1) * LANES] = _load_token_tiles(x_ref, 0, MOE_BLOCK, s).astype(BF16)
        hb = _dot(xb_ref[...], w1b_ref[...]) + b1_ref[0]
        g = jnp.minimum(hb[:, :D_EXPERT], SWIGLU_LIMIT)
        u = jnp.clip(hb[:, D_EXPERT:], -SWIGLU_LIMIT, SWIGLU_LIMIT)
        act = (u + 1.0) * (g * jax.nn.sigmoid(SWIGLU_ALPHA * g))
        _store_token_tiles(y_ref, _dot(act.astype(BF16), w2b_ref[...]) + b2_ref[0])

    @pl.when(pl.program_id(0) >= nused_ref[0])
    def _():
        y_ref[...] = jnp.zeros_like(y_ref)


def _experts(blk_e, n_used, xs, w_e_in, b_e_in, w_e_out, b_e_out):
    n_slots = xs.shape[0] // TOKEN_TILE
    nblk = n_slots // MOE_BLOCK
    slot_block = pl.BlockSpec((MOE_BLOCK * TOKEN_TILE, LANES), lambda i, be, nu: (i, 0))
    grid_spec = pltpu.PrefetchScalarGridSpec(
        num_scalar_prefetch=2,
        grid=(nblk,),
        in_specs=[slot_block,
                  pl.BlockSpec((1, D_MODEL, 2 * D_EXPERT), lambda i, be, nu: (be[i], 0, 0)),
                  pl.BlockSpec((1, 1, 2 * D_EXPERT), lambda i, be, nu: (be[i], 0, 0)),
                  pl.BlockSpec((1, D_EXPERT, D_MODEL), lambda i, be, nu: (be[i], 0, 0)),
                  pl.BlockSpec((1, 1, D_MODEL), lambda i, be, nu: (be[i], 0, 0))],
        out_specs=slot_block,
        scratch_shapes=[pltpu.VMEM((MOE_BLOCK, D_MODEL), BF16), pltpu.VMEM((D_MODEL, 2 * D_EXPERT), BF16),
                        pltpu.VMEM((D_EXPERT, D_MODEL), BF16)],
    )
    return pl.pallas_call(
        _expert_kernel,
        grid_spec=grid_spec,
        out_shape=jax.ShapeDtypeStruct((n_slots * TOKEN_TILE, LANES), F32),
        compiler_params=_cparams(("arbitrary",)),
        name="moe_experts",
    )(blk_e, n_used, xs, w_e_in, b_e_in[:, None, :], w_e_out, b_e_out[:, None, :])


def _combine_kernel(dest_ref, x_ref, gate_ref, y_ref, o_ref, buf_ref, sem):
    n_copies = DISPATCH_TOKENS * TOP_K

    def start(t, c):
        for k in range(TOP_K):
            src = pl.multiple_of(dest_ref[0, 0, t * TOP_K + k] * TOKEN_TILE, TOKEN_TILE)
            dst = pl.multiple_of((k * DISPATCH_TOKENS + t) * TOKEN_TILE, TOKEN_TILE)
            pltpu.make_async_copy(y_ref.at[pl.ds(src, TOKEN_TILE)], buf_ref.at[pl.ds(dst, TOKEN_TILE)],
                                  sem).start(priority=k % 2)
        return c

    lax.fori_loop(0, DISPATCH_TOKENS, start, 0, unroll=2)
    pltpu.make_async_copy(y_ref.at[pl.ds(0, n_copies * TOKEN_TILE)], buf_ref, sem).wait()
    rows = 32
    for r0 in range(0, DISPATCH_TOKENS, rows):
        gate = gate_ref[r0:r0 + rows, :]
        gates = [jnp.broadcast_to(gate[:, k:k + 1], (rows, LANES)) for k in range(TOP_K)]
        for s in range(TOKEN_TILE):
            out = x_ref[r0:r0 + rows, s * LANES:(s + 1) * LANES]
            for k in range(TOP_K):
                out = out + gates[k] * _load_token_tiles(buf_ref, k * DISPATCH_TOKENS + r0, rows, s)
            o_ref[r0:r0 + rows, s * LANES:(s + 1) * LANES] = out


def _combine(dest, x2, gate, ys):
    n = x2.shape[0]
    per = DISPATCH_TOKENS * TOP_K
    return pl.pallas_call(
        _combine_kernel,
        grid=(n // DISPATCH_TOKENS,),
        in_specs=[pl.BlockSpec((1, 1, per), lambda i: (i, 0, 0), memory_space=pltpu.SMEM),
                  pl.BlockSpec((DISPATCH_TOKENS, D_MODEL), lambda i: (i, 0)),
                  pl.BlockSpec((DISPATCH_TOKENS, TOP_K), lambda i: (i, 0)),
                  pl.BlockSpec(memory_space=pl.ANY)],
        out_specs=pl.BlockSpec((DISPATCH_TOKENS, D_MODEL), lambda i: (i, 0)),
        out_shape=jax.ShapeDtypeStruct((n, D_MODEL), F32),
        scratch_shapes=[pltpu.VMEM((TOP_K * DISPATCH_TOKENS * TOKEN_TILE, LANES), F32),
                        pltpu.SemaphoreType.DMA(())],
        compiler_params=_cparams(("arbitrary",)),
        name="moe_combine",
    )(dest.reshape(n // DISPATCH_TOKENS, 1, per), x2, gate, ys)


def _moe(x2, h2, top_idx, gate, w_e_in, b_e_in, w_e_out, b_e_out):
    n = x2.shape[0]
    rank, counts = _expert_rank(top_idx, 512)
    counts = counts[0, :N_EXPERTS].astype(I32)
    padded = (counts + MOE_BLOCK - 1) // MOE_BLOCK * MOE_BLOCK
    pend = jnp.cumsum(padded)
    pstart = pend - padded
    nblk = -(-(n * TOP_K) // MOE_BLOCK) + N_EXPERTS
    blk_start = jnp.arange(nblk, dtype=I32) * MOE_BLOCK
    blk_e = jnp.minimum(jnp.sum((pend[None, :] <= blk_start[:, None]).astype(I32), axis=1), N_EXPERTS - 1)
    n_used = (pend[-1:] // MOE_BLOCK).astype(I32)
    onehot = top_idx[:, :, None] == jnp.arange(N_EXPERTS, dtype=I32)[None, None, :]
    dest = rank + jnp.sum(jnp.where(onehot, pstart[None, None, :], 0), axis=-1)
    n_slots = nblk * MOE_BLOCK
    pad_len = padded - counts
    pad_end = jnp.cumsum(pad_len)
    base = jnp.concatenate([pstart + counts - (pad_end - pad_len), pend[-1:] - pad_end[-1:]])
    j = jnp.arange(n_slots - n * TOP_K, dtype=I32)
    group = jnp.sum((pad_end[None, :] <= j[:, None]).astype(I32), axis=1)
    group_hot = group[:, None] == jnp.arange(N_EXPERTS + 1, dtype=I32)[None, :]
    pad_slots = j + jnp.sum(jnp.where(group_hot, base[None, :], 0), axis=1)
    xs = _dispatch(dest, pad_slots, h2, n_slots)
    ys = _experts(blk_e, n_used, xs, w_e_in, b_e_in, w_e_out, b_e_out)
    return _combine(dest, x2, gate, ys)


def _layer(x, mem, g_mix, w_in, g_q_dsa, g_k_dsa, g_q_mem, g_k_mem, g_mem, w_mem_kv, w_gate, b_gate,
           w_branch, w_out, g_ffn, w_router, b_router, w_e_in, b_e_in, w_e_out, b_e_out):
    batch, seq, _ = x.shape
    n = batch * seq
    topk = min(DSA_TOPK_MAX, seq // 4)
    x2d = x.reshape(n, D_MODEL)
    tm = min(512, seq)
    sq, sk, sv, dq, iq, kvi, iw, mq, vt = _inproj(x2d, batch, seq, g_mix, w_in, g_q_dsa, g_k_dsa, g_q_mem, tm)
    o_sb = _sb_attention(sq, sk, sv, min(256, seq))
    o_dsa = _dsa_attention(dq, iq, iw, kvi, vt, 128, topk)
    mlen = mem.shape[1]
    mk, mv = _mem_kv(mem.reshape(batch * mlen, D_MODEL), g_mem, w_mem_kv, g_k_mem, min(512, batch * mlen))
    mw = MEM_HEADS * MEM_HEAD_DIM
    o_mem = _mem_attention(mq.reshape(batch, seq, mw), mk.reshape(batch, mlen, mw), mv.reshape(batch, mlen, mw), tm)
    x2, h2, top_idx, gate = _merge_route(
        x2d, g_mix, o_sb.reshape(n, BRANCH_W), o_dsa.reshape(n, BRANCH_W), o_mem.reshape(n, mw),
        w_gate, b_gate, w_branch, w_out, g_ffn, w_router, b_router, tm)
    out = _moe(x2, h2, top_idx, gate, w_e_in, b_e_in, w_e_out, b_e_out)
    return out.reshape(batch, seq, D_MODEL)


def kernel(x, mem, g_mix, w_in, g_q_dsa, g_k_dsa, g_q_mem, g_k_mem, g_mem, w_mem_kv, w_gate, b_gate, w_branch, w_out, g_ffn, w_router, b_router, w_e_in, b_e_in, w_e_out, b_e_out):
    for l in range(g_mix.shape[0]):
        x = _layer(x, mem, g_mix[l], w_in[l], g_q_dsa[l], g_k_dsa[l], g_q_mem[l], g_k_mem[l], g_mem[l],
                   w_mem_kv[l], w_gate[l], b_gate[l], w_branch[l], w_out[l], g_ffn[l], w_router[l],
                   b_router[l], w_e_in[l], b_e_in[l], w_e_out[l], b_e_out[l])
    return x
```

```python
import functools

import numpy as np
import jax
import jax.numpy as jnp
from jax import lax
from jax.experimental import pallas as pl
from jax.experimental.pallas import tpu as pltpu

F32 = jnp.float32
BF16 = jnp.bfloat16
I32 = jnp.int32

D_MODEL = 1024
CHUNK = 64
SB_HEADS = 8
DSA_HEADS = 8
HEAD_DIM = 64
IDX_HEADS = 4
DSA_TOPK_MAX = 256
MEM_HEADS = 4
MEM_HEAD_DIM = 128
N_EXPERTS = 32
TOP_K = 4
D_EXPERT = D_MODEL
SWIGLU_LIMIT = 7.0
SWIGLU_ALPHA = 1.702
ROPE_THETA = 10000.0
EPS = 1e-6
MOE_BLOCK = 512

BRANCH_W = 512
IN_SIZES = (512, 512, 512, 512, 64, 64, 256, 64, 4, 512)
C_SQ, C_SK, C_SV, C_DQ, C_MQ, C_IQ, C_SMALL, C_END = 0, 512, 1024, 1536, 2048, 2560, 2816, 3072
IW_LANE = 64

LANES = 128
NEG_BIG = -1e30
SB_CUTOFF = 110.0
KEY_NEG_INF = int(np.array(-np.inf, np.float32).view(np.int32)) ^ 0x7FFFFFFF
INT_MIN = -(2 ** 31)

VMEM_LIMIT = 56 * 1024 * 1024
DENSE_ROWS = 1024


def _cparams(sem):
    return pltpu.CompilerParams(dimension_semantics=sem, vmem_limit_bytes=VMEM_LIMIT)


def _dot(a, b):
    return jnp.dot(a, b, preferred_element_type=F32)


def _dot_nt(a, b):
    return lax.dot_general(a, b, (((1,), (1,)), ((), ())), preferred_element_type=F32)


def _split_bf16(x):
    hi = x.astype(BF16)
    lo = (x - hi.astype(F32)).astype(BF16)
    return hi, lo


def _dot_split(x, m_bf16):
    hi, lo = _split_bf16(x)
    return _dot(hi, m_bf16) + _dot(lo, m_bf16)


TOKEN_TILE = D_MODEL // LANES


def _store_token_tiles(ref, y):
    rows = y.shape[0]
    for s in range(TOKEN_TILE):
        ref[pl.ds(s, rows, stride=TOKEN_TILE), :] = y[:, s * LANES:(s + 1) * LANES]


def _load_token_tiles(ref, start_row, rows, s):
    return ref[pl.ds(start_row * TOKEN_TILE + s, rows, stride=TOKEN_TILE), :]


def _rot_half_unsigned(y):
    w = y.shape[1]
    lane = lax.broadcasted_iota(I32, y.shape, 1)
    return jnp.where((lane & 32) == 0, pltpu.roll(y, w - 32, 1), pltpu.roll(y, 32, 1))


def _inproj_kernel(x_ref, g_ref, w_ref, wsvt_ref, cos_ref, sin_ref, coss_ref, sins_ref, gq_ref, gks_ref,
                   gm_ref, bd64_ref, bd128_ref,
                   sq_ref, sk_ref, sv_ref, dq_ref, iq_ref, kvi_ref, iw_ref, mq_ref, vt_ref):
    x = x_ref[...]
    ms = jnp.mean(x * x, axis=-1, keepdims=True)
    h = (x * lax.rsqrt(ms + EPS) * g_ref[...]).astype(BF16)

    def seg(a, b):
        return _dot(h, w_ref[:, a:b])

    def put_heads(ref, y):
        for hd in range(y.shape[1] // HEAD_DIM):
            ref[0, hd] = y[:, hd * HEAD_DIM:(hd + 1) * HEAD_DIM].astype(BF16)

    put_heads(sq_ref, seg(C_SQ, C_SK) * (HEAD_DIM ** -0.5))
    put_heads(sk_ref, seg(C_SK, C_SV))
    sv_ref[0] = _dot_nt(wsvt_ref[...], h).astype(BF16)

    y = seg(C_DQ, C_MQ)
    msq = _dot_split(y * y, bd64_ref[...]) * (1.0 / HEAD_DIM)
    y = y * lax.rsqrt(msq + EPS) * gq_ref[...]
    y = y * cos_ref[...] + _rot_half_unsigned(y) * sin_ref[...]
    put_heads(dq_ref, y * (HEAD_DIM ** -0.5))

    y = seg(C_IQ, C_SMALL)
    y = y * cos_ref[:, :256] + _rot_half_unsigned(y) * sin_ref[:, :256]
    iq_ref[...] = (y * (HEAD_DIM ** -0.5)).astype(BF16)

    y = seg(C_SMALL, C_END)
    lane = lax.broadcasted_iota(I32, y.shape, 1)
    is_k = lane < HEAD_DIM
    msk = jnp.sum(jnp.where(is_k, y * y, 0.0), axis=-1, keepdims=True) * (1.0 / HEAD_DIM)
    y = y * jnp.where(is_k, lax.rsqrt(msk + EPS) * gks_ref[...], 1.0)
    y = y * coss_ref[...] + _rot_half_unsigned(y) * sins_ref[...]
    kvi_ref[...] = y.astype(BF16)
    iw_ref[...] = y[:, 128:256]
    kv_t = y[:, 0:128].T
    row_t = lax.broadcasted_iota(I32, kv_t.shape, 0)
    vt_ref[0] = jnp.where(row_t < HEAD_DIM, 1.0, kv_t).astype(BF16)

    y = seg(C_MQ, C_IQ)
    msm = _dot_split(y * y, bd128_ref[...]) * (1.0 / MEM_HEAD_DIM)
    mq_ref[...] = (y * lax.rsqrt(msm + EPS) * gm_ref[...]).astype(BF16)


def _rope_tables(seq):
    half = HEAD_DIM // 2
    inv = ROPE_THETA ** (-jnp.arange(half, dtype=F32) / half)
    ang = jnp.arange(seq).astype(F32)[:, None] * inv[None, :]
    cos = jnp.cos(ang)
    sin = jnp.sin(ang)
    cos64 = jnp.concatenate([cos, cos], axis=1)
    sin64 = jnp.concatenate([-sin, sin], axis=1)
    one = jnp.ones_like(cos64)
    zero = jnp.zeros_like(cos64)
    cosq = jnp.tile(cos64, (1, 8))
    sinq = jnp.tile(sin64, (1, 8))
    coss = jnp.concatenate([cos64, one, cos64, one], axis=1)
    sins = jnp.concatenate([sin64, zero, sin64, zero], axis=1)
    return cosq, sinq, coss, sins


def _block_diag_ones(width, group):
    idx = np.arange(width) // group
    return jnp.asarray((idx[:, None] == idx[None, :]).astype(np.float32), dtype=BF16)


def _inproj(x2d, batch, seq, g_mix, w_in, g_q_dsa, g_k_dsa, g_q_mem, tm):
    n = x2d.shape[0]
    sizes = np.cumsum((0,) + IN_SIZES)
    col = {name: (int(sizes[i]), int(sizes[i + 1])) for i, name in enumerate(
        ("sq", "sk", "sv", "dq", "dk", "dv", "iq", "ik", "iw", "mq"))}
    order = ("sq", "sk", "sv", "dq", "mq", "iq", "dk", "dv", "ik", "iw")
    w = jnp.concatenate([w_in[:, col[k][0]:col[k][1]] for k in order]
                        + [jnp.zeros((D_MODEL, C_END - sum(IN_SIZES)), w_in.dtype)], axis=1).astype(BF16)
    cosq, sinq, coss, sins = (jnp.asarray(t) for t in _rope_tables(seq))
    gq = jnp.tile(g_q_dsa, 8)[None, :]
    gks = jnp.concatenate([g_k_dsa, jnp.ones((256 - HEAD_DIM,), F32)])[None, :]
    gm = jnp.tile(g_q_mem, MEM_HEADS)[None, :]
    spb = seq // tm
    row = lambda i: (i, 0)
    const = lambda i: (0, 0)
    pos = lambda i: (i % spb, 0)
    heads = lambda i: (i // spb, 0, i % spb, 0)
    head_shape = jax.ShapeDtypeStruct((batch, 8, seq, HEAD_DIM), BF16)
    head_spec = pl.BlockSpec((1, 8, tm, HEAD_DIM), heads)
    return pl.pallas_call(
        _inproj_kernel,
        grid=(n // tm,),
        in_specs=[
            pl.BlockSpec((tm, D_MODEL), row),
            pl.BlockSpec((1, D_MODEL), const),
            pl.BlockSpec((D_MODEL, C_END), const, pipeline_mode=pl.Buffered(1)),
            pl.BlockSpec((512, D_MODEL), const, pipeline_mode=pl.Buffered(1)),
            pl.BlockSpec((tm, 512), pos), pl.BlockSpec((tm, 512), pos),
            pl.BlockSpec((tm, 256), pos), pl.BlockSpec((tm, 256), pos),
            pl.BlockSpec((1, 512), const), pl.BlockSpec((1, 256), const), pl.BlockSpec((1, 512), const),
            pl.BlockSpec((512, 512), const), pl.BlockSpec((512, 512), const),
        ],
        out_specs=[head_spec, head_spec, pl.BlockSpec((1, 512, tm), lambda i: (i // spb, 0, i % spb)), head_spec,
                   pl.BlockSpec((tm, 256), row), pl.BlockSpec((tm, 256), row),
                   pl.BlockSpec((tm, 128), row), pl.BlockSpec((tm, 512), row),
                   pl.BlockSpec((1, 128, tm), lambda i: (i // spb, 0, i % spb))],
        out_shape=[head_shape, head_shape, jax.ShapeDtypeStruct((batch, 512, seq), BF16), head_shape,
                   jax.ShapeDtypeStruct((n, 256), BF16), jax.ShapeDtypeStruct((n, 256), BF16),
                   jax.ShapeDtypeStruct((n, 128), F32), jax.ShapeDtypeStruct((n, 512), BF16),
                   jax.ShapeDtypeStruct((batch, 128, seq), BF16)],
        compiler_params=_cparams(("parallel",)),
        name="inproj",
    )(x2d, g_mix[None, :], w, w_in[:, col["sv"][0]:col["sv"][1]].T.astype(BF16), cosq, sinq, coss, sins, gq, gks, gm,
      _block_diag_ones(512, HEAD_DIM), _block_diag_ones(512, MEM_HEAD_DIM))


def _sb_kernel(q_ref, k_ref, vt_ref, u_ref, o_ref, acc_ref, car_ref, *, tq):
    qi = pl.program_id(1)
    rows = lax.broadcasted_iota(I32, (tq, tq), 0)
    cols = lax.broadcasted_iota(I32, (tq, tq), 1)
    dif = rows - cols
    u = u_ref[...]
    acc_ref[...] = jnp.zeros_like(acc_ref)
    car_ref[...] = jnp.zeros_like(car_ref)

    def cond(c):
        kb, mx = c
        return jnp.logical_and(kb >= 0, mx > -SB_CUTOFF)

    def body(c):
        kb, _ = c
        ks = pl.multiple_of(kb * tq, tq)
        earlier = dif < (qi - kb) * tq
        neg_mask = jnp.where(earlier, -1.0, 0.0).astype(BF16)
        heads = range(SB_HEADS)
        z = [_dot_nt(k_ref[0, hd, pl.ds(ks, tq), :], q_ref[0, hd]) for hd in heads]
        ls, lk, between = [], [], []
        for hd in heads:
            zb = z[hd].astype(BF16)
            sp = jnp.maximum(zb, 0.0) + jnp.log(1.0 + jnp.exp(-jnp.abs(zb)))
            ls.append(z[hd] - sp.astype(F32))
            lk.append(sp * neg_mask)
            between.append(_dot(u, lk[hd]))
        for hd in heads:
            rs = slice(hd * HEAD_DIM, (hd + 1) * HEAD_DIM)
            car = car_ref[hd:hd + 1, :]
            w = jnp.where(earlier, jnp.exp(ls[hd] + between[hd] + car), 0.0)
            acc_ref[rs, :] += _dot(vt_ref[0, rs, pl.ds(ks, tq)], w.astype(BF16))
            car_ref[hd:hd + 1, :] = car + (between[hd][0:1, :] + lk[hd][0:1, :].astype(F32))
        return kb - 1, jnp.max(car_ref[...])

    lax.while_loop(cond, body, (qi, jnp.float32(0.0)))
    o_ref[0] = acc_ref[...].T.astype(BF16)


def _strict_tri(n, lower):
    i = np.arange(n)
    m = (i[:, None] > i[None, :]) if lower else (i[:, None] < i[None, :])
    return jnp.asarray(m.astype(np.float32), dtype=BF16)


def _sb_attention(sq, sk, svt, tq):
    batch, _, seq, _ = sq.shape
    return pl.pallas_call(
        functools.partial(_sb_kernel, tq=tq),
        grid=(batch, seq // tq),
        in_specs=[pl.BlockSpec((1, SB_HEADS, tq, HEAD_DIM), lambda b, i: (b, 0, i, 0)),
                  pl.BlockSpec((1, SB_HEADS, seq, HEAD_DIM), lambda b, i: (b, 0, 0, 0)),
                  pl.BlockSpec((1, BRANCH_W, seq), lambda b, i: (b, 0, 0)),
                  pl.BlockSpec((tq, tq), lambda b, i: (0, 0))],
        out_specs=pl.BlockSpec((1, tq, BRANCH_W), lambda b, i: (b, i, 0)),
        out_shape=jax.ShapeDtypeStruct((batch, seq, BRANCH_W), BF16),
        scratch_shapes=[pltpu.VMEM((BRANCH_W, tq), F32), pltpu.VMEM((SB_HEADS, tq), F32)],
        compiler_params=_cparams(("parallel", "parallel")),
        name="sb_attention",
    )(sq, sk, svt, _strict_tri(tq, lower=False))


DSA_SEG = 512
DSA_KB = 512
DSA_GROUP_HEADS = 8


def _tree_sum(parts):
    while len(parts) > 1:
        parts = [parts[i] + parts[i + 1] for i in range(0, len(parts) - 1, 2)] + ([parts[-1]] if len(parts) % 2 else [])
    return parts[0]


def _dsa_kernel(dq_ref, iq_ref, iw_ref, kvi_ref, vt_ref, tri_ref, o_ref,
                sc_ref, bias_ref, s_ref, *, tq, topk, nseg_max):
    nseg = (pl.program_id(1) * tq) // DSA_SEG + 1
    for ns in range(1, nseg_max + 1):
        @pl.when(nseg == ns)
        def _(ns=ns):
            _dsa_body(dq_ref, iq_ref, iw_ref, kvi_ref, vt_ref, tri_ref, o_ref, sc_ref, bias_ref, s_ref,
                      tq=tq, topk=topk, nseg=ns)


def _dsa_body(dq_ref, iq_ref, iw_ref, kvi_ref, vt_ref, tri_ref, o_ref, sc_ref, bias_ref, s_ref, *, tq, topk, nseg):
    qs = pl.program_id(1) * tq
    blocks = [slice(c * DSA_KB, (c + 1) * DSA_KB) for c in range(nseg)]
    iq = iq_ref[0]
    w_t = iw_ref[0].T
    w_row = [w_t[IW_LANE + h:IW_LANE + h + 1, :] * (IDX_HEADS ** -0.5) for h in range(IDX_HEADS)]
    q_chunk = (qs + lax.broadcasted_iota(I32, (DSA_KB, tq), 1)) // CHUNK
    k_chunk = lax.broadcasted_iota(I32, (DSA_KB, tq), 0) // CHUNK

    for c, blk in enumerate(blocks):
        ik = kvi_ref[0, blk, 128:192]
        lg = [_dot_nt(ik, iq[:, h * HEAD_DIM:(h + 1) * HEAD_DIM]) for h in range(IDX_HEADS)]
        sc = jnp.zeros((DSA_KB, tq), F32)
        for h in range(IDX_HEADS):
            sc = sc + w_row[h] * jnp.maximum(lg[h], 0.0)
        admissible = (c * (DSA_KB // CHUNK) + k_chunk) <= q_chunk
        sc_ref[blk, :] = jnp.where(admissible, sc, -jnp.inf)

    def count(pred_fn):
        sub, lanes_of_sums = 32, 4
        acc = [jnp.zeros((sub, tq), F32)] * lanes_of_sums
        for j in range(nseg * DSA_SEG // sub):
            acc[j % lanes_of_sums] = acc[j % lanes_of_sums] + jnp.where(
                pred_fn(sc_ref[j * sub:(j + 1) * sub, :]), 1.0, 0.0)
        return jnp.sum(_tree_sum(acc), axis=0, keepdims=True)

    def key_to_float(key):
        return lax.bitcast_convert_type(jnp.where(key >= 0, key, key ^ 0x7FFFFFFF), F32)

    kf = jnp.float32(topk)
    n_rows = jnp.float32(nseg * DSA_SEG)
    t0 = jnp.where(count(lambda s: s >= 0.0) >= kf, 0, INT_MIN).astype(I32)

    def bit_step(i, t):
        cand = t + lax.shift_left(jnp.int32(1), 30 - i)
        cand_f = key_to_float(cand)
        cnt = jnp.where(cand <= KEY_NEG_INF, n_rows, count(lambda s: s >= cand_f))
        return jnp.where(cnt >= kf, cand, t)

    thr_key = lax.fori_loop(0, 31, bit_step, t0)
    thr = jnp.where(thr_key <= KEY_NEG_INF, -jnp.inf, key_to_float(thr_key))
    need = kf - count(lambda s: s > thr)
    tri = tri_ref[...]

    surplus = jnp.max(count(lambda s: s >= thr)) > kf

    @pl.when(surplus)
    def _():
        prefix = jnp.zeros((1, tq), F32)
        for blk in blocks:
            sc = sc_ref[blk, :]
            eqf = jnp.where(sc == thr, 1.0, 0.0)
            rank = _dot(tri, eqf.astype(BF16)) + prefix
            tie = jnp.where(rank < need, eqf, 0.0)
            sel = jnp.where(sc > thr, 1.0, tie)
            bias_ref[blk, :] = jnp.where(sc > -jnp.inf, (sel - 1.0) * (-NEG_BIG), NEG_BIG)
            prefix = prefix + jnp.sum(eqf, axis=0, keepdims=True)

    @pl.when(jnp.logical_not(surplus))
    def _():
        for blk in blocks:
            bias_ref[blk, :] = jnp.where(sc_ref[blk, :] >= thr, 0.0, NEG_BIG)

    q8 = dq_ref[0].reshape(DSA_HEADS * tq, HEAD_DIM)
    m = jnp.full((1, DSA_HEADS * tq), NEG_BIG, F32)
    for blk in blocks:
        b = bias_ref[blk, :]
        s = _dot_nt(kvi_ref[0, blk, 0:HEAD_DIM], q8) + jnp.concatenate([b] * DSA_HEADS, axis=1)
        s_ref[blk, :] = s
        m = jnp.maximum(m, jnp.max(s, axis=0, keepdims=True))
    acc = jnp.zeros((128, DSA_HEADS * tq), F32)
    for blk in blocks:
        p = jnp.exp(s_ref[blk, :] - m)
        acc = acc + _dot(vt_ref[0, :, blk], p.astype(BF16))
    for hd in range(DSA_HEADS):
        a = acc[:, hd * tq:(hd + 1) * tq]
        o = (a / a[0:1, :]).T
        o_ref[0, :, hd * HEAD_DIM:(hd + 1) * HEAD_DIM] = o[:, HEAD_DIM:].astype(BF16)


def _dsa_attention(dq, iq, iw, kvi, vt, tq, topk):
    batch, _, seq, _ = dq.shape
    return pl.pallas_call(
        functools.partial(_dsa_kernel, tq=tq, topk=topk, nseg_max=seq // DSA_SEG),
        grid=(batch, seq // tq),
        in_specs=[
            pl.BlockSpec((1, DSA_HEADS, tq, HEAD_DIM), lambda b, i: (b, 0, i, 0)),
            pl.BlockSpec((1, tq, 256), lambda b, i: (b, i, 0)),
            pl.BlockSpec((1, tq, 128), lambda b, i: (b, i, 0)),
            pl.BlockSpec((1, seq, 256), lambda b, i: (b, 0, 0)),
            pl.BlockSpec((1, 128, seq), lambda b, i: (b, 0, 0)),
            pl.BlockSpec((DSA_KB, DSA_KB), lambda b, i: (0, 0)),
        ],
        out_specs=pl.BlockSpec((1, tq, BRANCH_W), lambda b, i: (b, i, 0)),
        out_shape=jax.ShapeDtypeStruct((batch, seq, BRANCH_W), BF16),
        scratch_shapes=[pltpu.VMEM((seq, tq), F32), pltpu.VMEM((seq, tq), F32),
                        pltpu.VMEM((seq, DSA_HEADS * tq), F32)],
        compiler_params=_cparams(("parallel", "parallel")),
        name="dsa_attention",
    )(dq, iq.reshape(batch, seq, 256), iw.reshape(batch, seq, 128), kvi.reshape(batch, seq, 256), vt,
      _strict_tri(DSA_KB, lower=True))


def _memkv_kernel(m_ref, g_ref, w_ref, gk_ref, bd_ref, mk_ref, mv_ref):
    x = m_ref[...]
    ms = jnp.mean(x * x, axis=-1, keepdims=True)
    h = (x * lax.rsqrt(ms + EPS) * g_ref[...]).astype(BF16)
    mw = MEM_HEADS * MEM_HEAD_DIM
    k = _dot(h, w_ref[:, :mw])
    msk = _dot_split(k * k, bd_ref[...]) * (1.0 / MEM_HEAD_DIM)
    mk_ref[...] = (k * lax.rsqrt(msk + EPS) * gk_ref[...]).astype(BF16)
    mv_ref[...] = _dot(h, w_ref[:, mw:]).astype(BF16)


def _mem_kv(mem2d, g_mem, w_mem_kv, g_k_mem, tm):
    n = mem2d.shape[0]
    mw = MEM_HEADS * MEM_HEAD_DIM
    row = lambda i: (i, 0)
    const = lambda i: (0, 0)
    return pl.pallas_call(
        _memkv_kernel,
        grid=(n // tm,),
        in_specs=[pl.BlockSpec((tm, D_MODEL), row), pl.BlockSpec((1, D_MODEL), const),
                  pl.BlockSpec((D_MODEL, 2 * mw), const), pl.BlockSpec((1, mw), const),
                  pl.BlockSpec((mw, mw), const)],
        out_specs=[pl.BlockSpec((tm, mw), row), pl.BlockSpec((tm, mw), row)],
        out_shape=[jax.ShapeDtypeStruct((n, mw), BF16), jax.ShapeDtypeStruct((n, mw), BF16)],
        compiler_params=_cparams(("parallel",)),
        name="mem_kv",
    )(mem2d, g_mem[None, :], w_mem_kv.astype(BF16), jnp.tile(g_k_mem, MEM_HEADS)[None, :],
      _block_diag_ones(mw, MEM_HEAD_DIM))


def _memattn_kernel(q_ref, k_ref, v_ref, o_ref):
    for hd in range(MEM_HEADS):
        sl = slice(hd * MEM_HEAD_DIM, (hd + 1) * MEM_HEAD_DIM)
        s = _dot_nt(q_ref[0, :, sl], k_ref[0, :, sl]) * (MEM_HEAD_DIM ** -0.5)
        p = jnp.exp(s - jnp.max(s, axis=-1, keepdims=True))
        o = _dot(p.astype(BF16), v_ref[0, :, sl]) / jnp.sum(p, axis=-1, keepdims=True)
        o_ref[0, :, sl] = o.astype(BF16)


def _mem_attention(mq, mk, mv, tq):
    batch, seq, mw = mq.shape
    mlen = mk.shape[1]
    kspec = pl.BlockSpec((1, mlen, mw), lambda b, i: (b, 0, 0))
    return pl.pallas_call(
        _memattn_kernel,
        grid=(batch, seq // tq),
        in_specs=[pl.BlockSpec((1, tq, mw), lambda b, i: (b, i, 0)), kspec, kspec],
        out_specs=pl.BlockSpec((1, tq, mw), lambda b, i: (b, i, 0)),
        out_shape=jax.ShapeDtypeStruct((batch, seq, mw), BF16),
        compiler_params=_cparams(("parallel", "parallel")),
        name="mem_attention",
    )(mq, mk, mv)


def _merge_kernel(x_ref, g_ref, osb_ref, odsa_ref, omem_ref, wg_ref, bg_ref, wb_ref, wo_ref,
                  gf_ref, wrh_ref, wrl_ref, br_ref, tri_ref,
                  x2_ref, h2_ref, idx_ref, gate_ref, rank_ref, cnt_ref, carry_ref):
    x = x_ref[...]
    ms = jnp.mean(x * x, axis=-1, keepdims=True)
    h = (x * lax.rsqrt(ms + EPS) * g_ref[...]).astype(BF16)
    merged = None
    for n, o_ref in enumerate((osb_ref, odsa_ref, omem_ref)):
        gate = jax.nn.sigmoid(_dot(h, wg_ref[n]) + bg_ref[n])
        term = gate * _dot(o_ref[...], wb_ref[n])
        merged = term if merged is None else merged + term
    x2 = x + _dot(merged.astype(BF16), wo_ref[...])
    x2_ref[...] = x2

    ms2 = jnp.mean(x2 * x2, axis=-1, keepdims=True)
    h2 = x2 * lax.rsqrt(ms2 + EPS) * gf_ref[...]
    _store_token_tiles(h2_ref, h2)
    hi, lo = _split_bf16(h2)
    logits = _dot(hi, wrh_ref[...]) + _dot(hi, wrl_ref[...]) + _dot(lo, wrh_ref[...]) + br_ref[...]

    lane = lax.broadcasted_iota(I32, logits.shape, 1)
    vals = logits
    top_v, top_i = [], []
    for _ in range(TOP_K):
        mx = jnp.max(vals, axis=-1, keepdims=True)
        ix = jnp.min(jnp.where(vals == mx, lane, N_EXPERTS), axis=-1, keepdims=True)
        top_v.append(mx)
        top_i.append(ix)
        vals = jnp.where(lane == ix, -jnp.inf, vals)
    ex = [jnp.exp(v - top_v[0]) for v in top_v]
    den = ex[0] + ex[1] + ex[2] + ex[3]
    for k in range(TOP_K):
        idx_ref[:, k:k + 1] = top_i[k]
        gate_ref[:, k:k + 1] = ex[k] / den

    @pl.when(pl.program_id(0) == 0)
    def _():
        carry_ref[...] = jnp.zeros_like(carry_ref)

    lane_e = lax.broadcasted_iota(I32, (x.shape[0], LANES), 1)
    hits = [lane_e == top_i[k] for k in range(TOP_K)]
    onehot = sum(jnp.where(hk, 1.0, 0.0) for hk in hits)
    before = _dot(tri_ref[...], onehot.astype(BF16)) + carry_ref[...]
    for k in range(TOP_K):
        rank_ref[:, k:k + 1] = jnp.sum(jnp.where(hits[k], before, 0.0), axis=-1, keepdims=True).astype(I32)
    carry_ref[...] += jnp.sum(onehot, axis=0, keepdims=True)
    cnt_ref[...] = carry_ref[...]


def _merge_route(x2d, g_mix, o_sb, o_dsa, o_mem, w_gate, b_gate, w_branch, w_out, g_ffn, w_router, b_router, tm):
    n = x2d.shape[0]
    row = lambda i: (i, 0)
    c2 = lambda i: (0, 0)
    c3 = lambda i: (0, 0, 0)
    wr_hi = w_router.astype(BF16)
    wr_lo = (w_router - wr_hi.astype(F32)).astype(BF16)
    return pl.pallas_call(
        _merge_kernel,
        grid=(n // tm,),
        in_specs=[pl.BlockSpec((tm, D_MODEL), row), pl.BlockSpec((1, D_MODEL), c2),
                  pl.BlockSpec((tm, BRANCH_W), row), pl.BlockSpec((tm, BRANCH_W), row),
                  pl.BlockSpec((tm, BRANCH_W), row),
                  pl.BlockSpec((3, D_MODEL, D_MODEL), c3, pipeline_mode=pl.Buffered(1)),
                  pl.BlockSpec((3, 1, D_MODEL), c3),
                  pl.BlockSpec((3, BRANCH_W, D_MODEL), c3, pipeline_mode=pl.Buffered(1)),
                  pl.BlockSpec((D_MODEL, D_MODEL), c2, pipeline_mode=pl.Buffered(1)),
                  pl.BlockSpec((1, D_MODEL), c2), pl.BlockSpec((D_MODEL, N_EXPERTS), c2),
                  pl.BlockSpec((D_MODEL, N_EXPERTS), c2), pl.BlockSpec((1, N_EXPERTS), c2),
                  pl.BlockSpec((tm, tm), c2, pipeline_mode=pl.Buffered(1))],
        out_specs=[pl.BlockSpec((tm, D_MODEL), row), pl.BlockSpec((tm * TOKEN_TILE, LANES), row),
                   pl.BlockSpec((tm, TOP_K), row), pl.BlockSpec((tm, TOP_K), row),
                   pl.BlockSpec((tm, TOP_K), row), pl.BlockSpec((1, LANES), c2)],
        out_shape=[jax.ShapeDtypeStruct((n, D_MODEL), F32), jax.ShapeDtypeStruct((n * TOKEN_TILE, LANES), F32),
                   jax.ShapeDtypeStruct((n, TOP_K), I32), jax.ShapeDtypeStruct((n, TOP_K), F32),
                   jax.ShapeDtypeStruct((n, TOP_K), I32), jax.ShapeDtypeStruct((1, LANES), F32)],
        scratch_shapes=[pltpu.VMEM((1, LANES), F32)],
        compiler_params=_cparams(("arbitrary",)),
        name="merge_route",
    )(x2d, g_mix[None, :], o_sb, o_dsa, o_mem, w_gate.astype(BF16), b_gate[:, None, :],
      w_branch.astype(BF16), w_out.astype(BF16), g_ffn[None, :], wr_hi, wr_lo, b_router[None, :],
      _strict_tri(tm, lower=True))


DISPATCH_TOKENS = 256


def _dispatch_kernel(dest_ref, pad_ref, h_ref, xs_ref, zero_ref, sem, *, n_pad):
    n_copies = DISPATCH_TOKENS * TOP_K + n_pad
    zero_ref[...] = jnp.zeros_like(zero_ref)

    def fill(j, c):
        for r in range(2):
            dst = pl.multiple_of(pad_ref[0, 0, 2 * j + r] * TOKEN_TILE, TOKEN_TILE)
            pltpu.make_async_copy(zero_ref, xs_ref.at[pl.ds(dst, TOKEN_TILE)], sem).start(priority=r)
        return c

    lax.fori_loop(0, n_pad // 2, fill, 0, unroll=2)

    def start(t, c):
        src = pl.multiple_of(t * TOKEN_TILE, TOKEN_TILE)
        for k in range(TOP_K):
            dst = pl.multiple_of(dest_ref[0, 0, t * TOP_K + k] * TOKEN_TILE, TOKEN_TILE)
            pltpu.make_async_copy(h_ref.at[pl.ds(src, TOKEN_TILE)], xs_ref.at[pl.ds(dst, TOKEN_TILE)],
                                  sem).start(priority=k % 2)
        return c

    lax.fori_loop(0, DISPATCH_TOKENS, start, 0, unroll=2)
    total = n_copies * TOKEN_TILE
    pltpu.make_async_copy(xs_ref.at[pl.ds(0, total)], xs_ref.at[pl.ds(0, total)], sem).wait()


def _dispatch(dest, pad_slots, h2t, n_slots):
    n = h2t.shape[0] // TOKEN_TILE
    steps = n // DISPATCH_TOKENS
    per = DISPATCH_TOKENS * TOP_K
    n_pad = pad_slots.shape[0] // steps
    assert n_pad * steps == pad_slots.shape[0] and n_pad % 2 == 0
    return pl.pallas_call(
        functools.partial(_dispatch_kernel, n_pad=n_pad),
        grid=(steps,),
        in_specs=[pl.BlockSpec((1, 1, per), lambda i: (i, 0, 0), memory_space=pltpu.SMEM),
                  pl.BlockSpec((1, 1, n_pad), lambda i: (i, 0, 0), memory_space=pltpu.SMEM),
                  pl.BlockSpec((DISPATCH_TOKENS * TOKEN_TILE, LANES), lambda i: (i, 0))],
        out_specs=pl.BlockSpec(memory_space=pl.ANY),
        out_shape=jax.ShapeDtypeStruct((n_slots * TOKEN_TILE, LANES), F32),
        scratch_shapes=[pltpu.VMEM((TOKEN_TILE, LANES), F32), pltpu.SemaphoreType.DMA(())],
        compiler_params=_cparams(("arbitrary",)),
        name="moe_dispatch",
    )(dest.reshape(steps, 1, per), pad_slots.reshape(steps, 1, n_pad), h2t)


def _expert_kernel(blk_e_ref, nused_ref, x_ref, w1_ref, b1_ref, w2_ref, b2_ref, y_ref, xb_ref, w1b_ref, w2b_ref):
    i = pl.program_id(0)
    new_expert = jnp.logical_or(i == 0, blk_e_ref[i] != blk_e_ref[jnp.maximum(i - 1, 0)])

    @pl.when(jnp.logical_and(i < nused_ref[0], new_expert))
    def _():
        w1b_ref[...] = w1_ref[0].astype(BF16)
        w2b_ref[...] = w2_ref[0].astype(BF16)

    @pl.when(i < nused_ref[0])
    def _():
        for s in range(TOKEN_TILE):
            xb_ref[:, s * LANES:(s + 1) * LANES] = _load_token_tiles(x_ref, 0, MOE_BLOCK, s).astype(BF16)
        hb = _dot(xb_ref[...], w1b_ref[...]) + b1_ref[0]
        g = jnp.minimum(hb[:, :D_EXPERT], SWIGLU_LIMIT)
        u = jnp.clip(hb[:, D_EXPERT:], -SWIGLU_LIMIT, SWIGLU_LIMIT)
        act = (u + 1.0) * (g * jax.nn.sigmoid(SWIGLU_ALPHA * g))
        _store_token_tiles(y_ref, _dot(act.astype(BF16), w2b_ref[...]) + b2_ref[0])

    @pl.when(pl.program_id(0) >= nused_ref[0])
    def _():
        y_ref[...] = jnp.zeros_like(y_ref)


def _experts(blk_e, n_used, xs, w_e_in, b_e_in, w_e_out, b_e_out):
    n_slots = xs.shape[0] // TOKEN_TILE
    nblk = n_slots // MOE_BLOCK
    slot_block = pl.BlockSpec((MOE_BLOCK * TOKEN_TILE, LANES), lambda i, be, nu: (i, 0))
    grid_spec = pltpu.PrefetchScalarGridSpec(
        num_scalar_prefetch=2,
        grid=(nblk,),
        in_specs=[slot_block,
                  pl.BlockSpec((1, D_MODEL, 2 * D_EXPERT), lambda i, be, nu: (be[i], 0, 0)),
                  pl.BlockSpec((1, 1, 2 * D_EXPERT), lambda i, be, nu: (be[i], 0, 0)),
                  pl.BlockSpec((1, D_EXPERT, D_MODEL), lambda i, be, nu: (be[i], 0, 0)),
                  pl.BlockSpec((1, 1, D_MODEL), lambda i, be, nu: (be[i], 0, 0))],
        out_specs=slot_block,
        scratch_shapes=[pltpu.VMEM((MOE_BLOCK, D_MODEL), BF16), pltpu.VMEM((D_MODEL, 2 * D_EXPERT), BF16),
                        pltpu.VMEM((D_EXPERT, D_MODEL), BF16)],
    )
    return pl.pallas_call(
        _expert_kernel,
        grid_spec=grid_spec,
        out_shape=jax.ShapeDtypeStruct((n_slots * TOKEN_TILE, LANES), F32),
        compiler_params=_cparams(("arbitrary",)),
        name="moe_experts",
    )(blk_e, n_used, xs, w_e_in, b_e_in[:, None, :], w_e_out, b_e_out[:, None, :])


def _combine_kernel(dest_ref, dest_next_ref, x_ref, gate_ref, y_ref, o_ref, buf_ref, sem):
    n_copies = DISPATCH_TOKENS * TOP_K
    i = pl.program_id(0)
    slot = i % 2

    def start_gathers(idx_ref, buf, dma_sem):
        def start(t, c):
            for k in range(TOP_K):
                src = pl.multiple_of(idx_ref[0, 0, t * TOP_K + k] * TOKEN_TILE, TOKEN_TILE)
                dst = pl.multiple_of((k * DISPATCH_TOKENS + t) * TOKEN_TILE, TOKEN_TILE)
                pltpu.make_async_copy(y_ref.at[pl.ds(src, TOKEN_TILE)], buf.at[pl.ds(dst, TOKEN_TILE)],
                                      dma_sem).start(priority=k % 2)
            return c
        lax.fori_loop(0, DISPATCH_TOKENS, start, 0, unroll=2)

    @pl.when(i == 0)
    def _():
        start_gathers(dest_ref, buf_ref.at[0], sem.at[0])

    @pl.when(i + 1 < pl.num_programs(0))
    def _():
        start_gathers(dest_next_ref, buf_ref.at[1 - slot], sem.at[1 - slot])

    buf = buf_ref.at[slot]
    pltpu.make_async_copy(y_ref.at[pl.ds(0, n_copies * TOKEN_TILE)], buf, sem.at[slot]).wait()
    rows = 32
    for r0 in range(0, DISPATCH_TOKENS, rows):
        gate = gate_ref[r0:r0 + rows, :]
        gates = [jnp.broadcast_to(gate[:, k:k + 1], (rows, LANES)) for k in range(TOP_K)]
        for s in range(TOKEN_TILE):
            out = x_ref[r0:r0 + rows, s * LANES:(s + 1) * LANES]
            for k in range(TOP_K):
                out = out + gates[k] * _load_token_tiles(buf, k * DISPATCH_TOKENS + r0, rows, s)
            o_ref[r0:r0 + rows, s * LANES:(s + 1) * LANES] = out


def _combine(dest, x2, gate, ys):
    n = x2.shape[0]
    steps = n // DISPATCH_TOKENS
    per = DISPATCH_TOKENS * TOP_K
    dest3 = dest.reshape(steps, 1, per)
    return pl.pallas_call(
        _combine_kernel,
        grid=(steps,),
        in_specs=[pl.BlockSpec((1, 1, per), lambda i: (i, 0, 0), memory_space=pltpu.SMEM),
                  pl.BlockSpec((1, 1, per), lambda i: (jnp.minimum(i + 1, steps - 1), 0, 0), memory_space=pltpu.SMEM),
                  pl.BlockSpec((DISPATCH_TOKENS, D_MODEL), lambda i: (i, 0)),
                  pl.BlockSpec((DISPATCH_TOKENS, TOP_K), lambda i: (i, 0)),
                  pl.BlockSpec(memory_space=pl.ANY)],
        out_specs=pl.BlockSpec((DISPATCH_TOKENS, D_MODEL), lambda i: (i, 0)),
        out_shape=jax.ShapeDtypeStruct((n, D_MODEL), F32),
        scratch_shapes=[pltpu.VMEM((2, TOP_K * DISPATCH_TOKENS * TOKEN_TILE, LANES), F32),
                        pltpu.SemaphoreType.DMA((2,))],
        compiler_params=_cparams(("arbitrary",)),
        name="moe_combine",
    )(dest3, dest3, x2, gate, ys)


def _moe(x2, h2, top_idx, gate, rank, counts, w_e_in, b_e_in, w_e_out, b_e_out):
    n = x2.shape[0]
    counts = counts[0, :N_EXPERTS].astype(I32)
    padded = (counts + MOE_BLOCK - 1) // MOE_BLOCK * MOE_BLOCK
    pend = jnp.cumsum(padded)
    pstart = pend - padded
    nblk = -(-(n * TOP_K) // MOE_BLOCK) + N_EXPERTS
    blk_start = jnp.arange(nblk, dtype=I32) * MOE_BLOCK
    blk_e = jnp.minimum(jnp.sum((pend[None, :] <= blk_start[:, None]).astype(I32), axis=1), N_EXPERTS - 1)
    n_used = (pend[-1:] // MOE_BLOCK).astype(I32)
    onehot = top_idx[:, :, None] == jnp.arange(N_EXPERTS, dtype=I32)[None, None, :]
    dest = rank + jnp.sum(jnp.where(onehot, pstart[None, None, :], 0), axis=-1)
    n_slots = nblk * MOE_BLOCK
    pad_len = padded - counts
    pad_end = jnp.cumsum(pad_len)
    base = jnp.concatenate([pstart + counts - (pad_end - pad_len), pend[-1:] - pad_end[-1:]])
    j = jnp.arange(n_slots - n * TOP_K, dtype=I32)
    group = jnp.sum((pad_end[None, :] <= j[:, None]).astype(I32), axis=1)
    group_hot = group[:, None] == jnp.arange(N_EXPERTS + 1, dtype=I32)[None, :]
    pad_slots = j + jnp.sum(jnp.where(group_hot, base[None, :], 0), axis=1)
    xs = _dispatch(dest, pad_slots, h2, n_slots)
    ys = _experts(blk_e, n_used, xs, w_e_in, b_e_in, w_e_out, b_e_out)
    return _combine(dest, x2, gate, ys)


def _layer(x, mem, g_mix, w_in, g_q_dsa, g_k_dsa, g_q_mem, g_k_mem, g_mem, w_mem_kv, w_gate, b_gate,
           w_branch, w_out, g_ffn, w_router, b_router, w_e_in, b_e_in, w_e_out, b_e_out):
    batch, seq, _ = x.shape
    n = batch * seq
    topk = min(DSA_TOPK_MAX, seq // 4)
    x2d = x.reshape(n, D_MODEL)
    tm = min(512, seq)
    tm_dense = min(DENSE_ROWS, seq)
    sq, sk, sv, dq, iq, kvi, iw, mq, vt = _inproj(x2d, batch, seq, g_mix, w_in, g_q_dsa, g_k_dsa, g_q_mem, tm_dense)
    o_sb = _sb_attention(sq, sk, sv, min(256, seq))
    o_dsa = _dsa_attention(dq, iq, iw, kvi, vt, 128, topk)
    mlen = mem.shape[1]
    mk, mv = _mem_kv(mem.reshape(batch * mlen, D_MODEL), g_mem, w_mem_kv, g_k_mem, min(512, batch * mlen))
    mw = MEM_HEADS * MEM_HEAD_DIM
    o_mem = _mem_attention(mq.reshape(batch, seq, mw), mk.reshape(batch, mlen, mw), mv.reshape(batch, mlen, mw), tm)
    x2, h2, top_idx, gate, rank, counts = _merge_route(
        x2d, g_mix, o_sb.reshape(n, BRANCH_W), o_dsa.reshape(n, BRANCH_W), o_mem.reshape(n, mw),
        w_gate, b_gate, w_branch, w_out, g_ffn, w_router, b_router, tm_dense)
    out = _moe(x2, h2, top_idx, gate, rank, counts, w_e_in, b_e_in, w_e_out, b_e_out)
    return out.reshape(batch, seq, D_MODEL)


def kernel(x, mem, g_mix, w_in, g_q_dsa, g_k_dsa, g_q_mem, g_k_mem, g_mem, w_mem_kv, w_gate, b_gate, w_branch, w_out, g_ffn, w_router, b_router, w_e_in, b_e_in, w_e_out, b_e_out):
    for l in range(g_mix.shape[0]):
        x = _layer(x, mem, g_mix[l], w_in[l], g_q_dsa[l], g_k_dsa[l], g_q_mem[l], g_k_mem[l], g_mem[l],
                   w_mem_kv[l], w_gate[l], b_gate[l], w_branch[l], w_out[l], g_ffn[l], w_router[l],
                   b_router[l], w_e_in[l], b_e_in[l], w_e_out[l], b_e_out[l])
    return x
```

```python
import functools

import numpy as np
import jax
import jax.numpy as jnp
from jax import lax
from jax.experimental import pallas as pl
from jax.experimental.pallas import tpu as pltpu

F32 = jnp.float32
BF16 = jnp.bfloat16
I32 = jnp.int32

D_MODEL = 1024
CHUNK = 64
SB_HEADS = 8
DSA_HEADS = 8
HEAD_DIM = 64
IDX_HEADS = 4
DSA_TOPK_MAX = 256
MEM_HEADS = 4
MEM_HEAD_DIM = 128
N_EXPERTS = 32
TOP_K = 4
D_EXPERT = D_MODEL
SWIGLU_LIMIT = 7.0
SWIGLU_ALPHA = 1.702
ROPE_THETA = 10000.0
EPS = 1e-6
MOE_BLOCK = 512

BRANCH_W = 512
IN_SIZES = (512, 512, 512, 512, 64, 64, 256, 64, 4, 512)
C_SQ, C_SK, C_SV, C_DQ, C_MQ, C_IQ, C_SMALL, C_END = 0, 512, 1024, 1536, 2048, 2560, 2816, 3072
IW_LANE = 64

LANES = 128
NEG_BIG = -1e30
SB_CUTOFF = 110.0
KEY_NEG_INF = int(np.array(-np.inf, np.float32).view(np.int32)) ^ 0x7FFFFFFF
INT_MIN = -(2 ** 31)

VMEM_LIMIT = 56 * 1024 * 1024
DENSE_ROWS = 1024


def _cparams(sem):
    return pltpu.CompilerParams(dimension_semantics=sem, vmem_limit_bytes=VMEM_LIMIT)


def _dot(a, b):
    return jnp.dot(a, b, preferred_element_type=F32)


def _dot_nt(a, b):
    return lax.dot_general(a, b, (((1,), (1,)), ((), ())), preferred_element_type=F32)


def _split_bf16(x):
    hi = x.astype(BF16)
    lo = (x - hi.astype(F32)).astype(BF16)
    return hi, lo


def _dot_split(x, m_bf16):
    hi, lo = _split_bf16(x)
    return _dot(hi, m_bf16) + _dot(lo, m_bf16)


TOKEN_TILE = D_MODEL // LANES


def _store_token_tiles(ref, y):
    rows = y.shape[0]
    for s in range(TOKEN_TILE):
        ref[pl.ds(s, rows, stride=TOKEN_TILE), :] = y[:, s * LANES:(s + 1) * LANES]


def _load_token_tiles(ref, start_row, rows, s):
    return ref[pl.ds(start_row * TOKEN_TILE + s, rows, stride=TOKEN_TILE), :]


def _rot_half_unsigned(y):
    w = y.shape[1]
    lane = lax.broadcasted_iota(I32, y.shape, 1)
    return jnp.where((lane & 32) == 0, pltpu.roll(y, w - 32, 1), pltpu.roll(y, 32, 1))


def _inproj_kernel(x_ref, g_ref, w_ref, wsvt_ref, cos_ref, sin_ref, coss_ref, sins_ref, gq_ref, gks_ref,
                   gm_ref, bd64_ref, bd128_ref,
                   sq_ref, sk_ref, sv_ref, dq_ref, iq_ref, kvi_ref, iw_ref, mq_ref, vt_ref):
    x = x_ref[...]
    ms = jnp.mean(x * x, axis=-1, keepdims=True)
    h = (x * lax.rsqrt(ms + EPS) * g_ref[...]).astype(BF16)

    def seg(a, b):
        return _dot(h, w_ref[:, a:b])

    def put_heads(ref, y):
        for hd in range(y.shape[1] // HEAD_DIM):
            ref[0, hd] = y[:, hd * HEAD_DIM:(hd + 1) * HEAD_DIM].astype(BF16)

    put_heads(sq_ref, seg(C_SQ, C_SK) * (HEAD_DIM ** -0.5))
    put_heads(sk_ref, seg(C_SK, C_SV))
    sv_ref[0] = _dot_nt(wsvt_ref[...], h).astype(BF16)

    y = seg(C_DQ, C_MQ)
    msq = _dot_split(y * y, bd64_ref[...]) * (1.0 / HEAD_DIM)
    y = y * lax.rsqrt(msq + EPS) * gq_ref[...]
    y = y * cos_ref[...] + _rot_half_unsigned(y) * sin_ref[...]
    put_heads(dq_ref, y * (HEAD_DIM ** -0.5))

    y = seg(C_IQ, C_SMALL)
    y = y * cos_ref[:, :256] + _rot_half_unsigned(y) * sin_ref[:, :256]
    iq_ref[...] = (y * (HEAD_DIM ** -0.5)).astype(BF16)

    y = seg(C_SMALL, C_END)
    lane = lax.broadcasted_iota(I32, y.shape, 1)
    is_k = lane < HEAD_DIM
    msk = jnp.sum(jnp.where(is_k, y * y, 0.0), axis=-1, keepdims=True) * (1.0 / HEAD_DIM)
    y = y * jnp.where(is_k, lax.rsqrt(msk + EPS) * gks_ref[...], 1.0)
    y = y * coss_ref[...] + _rot_half_unsigned(y) * sins_ref[...]
    kvi_ref[...] = y.astype(BF16)
    iw_ref[...] = y[:, 128:256]
    kv_t = y[:, 0:128].T
    row_t = lax.broadcasted_iota(I32, kv_t.shape, 0)
    vt_ref[0] = jnp.where(row_t < HEAD_DIM, 1.0, kv_t).astype(BF16)

    y = seg(C_MQ, C_IQ)
    msm = _dot_split(y * y, bd128_ref[...]) * (1.0 / MEM_HEAD_DIM)
    mq_ref[...] = (y * lax.rsqrt(msm + EPS) * gm_ref[...]).astype(BF16)


def _rope_tables(seq):
    half = HEAD_DIM // 2
    inv = ROPE_THETA ** (-jnp.arange(half, dtype=F32) / half)
    ang = jnp.arange(seq).astype(F32)[:, None] * inv[None, :]
    cos = jnp.cos(ang)
    sin = jnp.sin(ang)
    cos64 = jnp.concatenate([cos, cos], axis=1)
    sin64 = jnp.concatenate([-sin, sin], axis=1)
    one = jnp.ones_like(cos64)
    zero = jnp.zeros_like(cos64)
    cosq = jnp.tile(cos64, (1, 8))
    sinq = jnp.tile(sin64, (1, 8))
    coss = jnp.concatenate([cos64, one, cos64, one], axis=1)
    sins = jnp.concatenate([sin64, zero, sin64, zero], axis=1)
    return cosq, sinq, coss, sins


def _block_diag_ones(width, group):
    idx = np.arange(width) // group
    return jnp.asarray((idx[:, None] == idx[None, :]).astype(np.float32), dtype=BF16)


def _inproj(x2d, batch, seq, g_mix, w_in, g_q_dsa, g_k_dsa, g_q_mem, tm):
    n = x2d.shape[0]
    sizes = np.cumsum((0,) + IN_SIZES)
    col = {name: (int(sizes[i]), int(sizes[i + 1])) for i, name in enumerate(
        ("sq", "sk", "sv", "dq", "dk", "dv", "iq", "ik", "iw", "mq"))}
    order = ("sq", "sk", "sv", "dq", "mq", "iq", "dk", "dv", "ik", "iw")
    w = jnp.concatenate([w_in[:, col[k][0]:col[k][1]] for k in order]
                        + [jnp.zeros((D_MODEL, C_END - sum(IN_SIZES)), w_in.dtype)], axis=1).astype(BF16)
    cosq, sinq, coss, sins = (jnp.asarray(t) for t in _rope_tables(seq))
    gq = jnp.tile(g_q_dsa, 8)[None, :]
    gks = jnp.concatenate([g_k_dsa, jnp.ones((256 - HEAD_DIM,), F32)])[None, :]
    gm = jnp.tile(g_q_mem, MEM_HEADS)[None, :]
    spb = seq // tm
    row = lambda i: (i, 0)
    const = lambda i: (0, 0)
    pos = lambda i: (i % spb, 0)
    heads = lambda i: (i // spb, 0, i % spb, 0)
    head_shape = jax.ShapeDtypeStruct((batch, 8, seq, HEAD_DIM), BF16)
    head_spec = pl.BlockSpec((1, 8, tm, HEAD_DIM), heads)
    return pl.pallas_call(
        _inproj_kernel,
        grid=(n // tm,),
        in_specs=[
            pl.BlockSpec((tm, D_MODEL), row),
            pl.BlockSpec((1, D_MODEL), const),
            pl.BlockSpec((D_MODEL, C_END), const, pipeline_mode=pl.Buffered(1)),
            pl.BlockSpec((512, D_MODEL), const, pipeline_mode=pl.Buffered(1)),
            pl.BlockSpec((tm, 512), pos), pl.BlockSpec((tm, 512), pos),
            pl.BlockSpec((tm, 256), pos), pl.BlockSpec((tm, 256), pos),
            pl.BlockSpec((1, 512), const), pl.BlockSpec((1, 256), const), pl.BlockSpec((1, 512), const),
            pl.BlockSpec((512, 512), const), pl.BlockSpec((512, 512), const),
        ],
        out_specs=[head_spec, head_spec, pl.BlockSpec((1, 512, tm), lambda i: (i // spb, 0, i % spb)), head_spec,
                   pl.BlockSpec((tm, 256), row), pl.BlockSpec((tm, 256), row),
                   pl.BlockSpec((tm, 128), row), pl.BlockSpec((tm, 512), row),
                   pl.BlockSpec((1, 128, tm), lambda i: (i // spb, 0, i % spb))],
        out_shape=[head_shape, head_shape, jax.ShapeDtypeStruct((batch, 512, seq), BF16), head_shape,
                   jax.ShapeDtypeStruct((n, 256), BF16), jax.ShapeDtypeStruct((n, 256), BF16),
                   jax.ShapeDtypeStruct((n, 128), F32), jax.ShapeDtypeStruct((n, 512), BF16),
                   jax.ShapeDtypeStruct((batch, 128, seq), BF16)],
        compiler_params=_cparams(("parallel",)),
        name="inproj",
    )(x2d, g_mix[None, :], w, w_in[:, col["sv"][0]:col["sv"][1]].T.astype(BF16), cosq, sinq, coss, sins, gq, gks, gm,
      _block_diag_ones(512, HEAD_DIM), _block_diag_ones(512, MEM_HEAD_DIM))


def _sb_kernel(q_ref, k_ref, vt_ref, u_ref, o_ref, acc_ref, car_ref, *, tq):
    qi = pl.program_id(1)
    rows = lax.broadcasted_iota(I32, (tq, tq), 0)
    cols = lax.broadcasted_iota(I32, (tq, tq), 1)
    dif = rows - cols
    u = u_ref[...]
    acc_ref[...] = jnp.zeros_like(acc_ref)
    car_ref[...] = jnp.zeros_like(car_ref)

    def cond(c):
        kb, mx = c
        return jnp.logical_and(kb >= 0, mx > -SB_CUTOFF)

    def body(c):
        kb, _ = c
        ks = pl.multiple_of(kb * tq, tq)
        earlier = dif < (qi - kb) * tq
        neg_mask = jnp.where(earlier, -1.0, 0.0).astype(BF16)
        heads = range(SB_HEADS)
        z = [_dot_nt(k_ref[0, hd, pl.ds(ks, tq), :], q_ref[0, hd]) for hd in heads]
        ls, lk, between = [], [], []
        for hd in heads:
            zb = z[hd].astype(BF16)
            sp = jnp.maximum(zb, 0.0) + jnp.log(1.0 + jnp.exp(-jnp.abs(zb)))
            ls.append(z[hd] - sp.astype(F32))
            lk.append(sp * neg_mask)
            between.append(_dot(u, lk[hd]))
        for hd in heads:
            rs = slice(hd * HEAD_DIM, (hd + 1) * HEAD_DIM)
            car = car_ref[hd:hd + 1, :]
            w = jnp.where(earlier, jnp.exp(ls[hd] + between[hd] + car), 0.0)
            acc_ref[rs, :] += _dot(vt_ref[0, rs, pl.ds(ks, tq)], w.astype(BF16))
            car_ref[hd:hd + 1, :] = car + (between[hd][0:1, :] + lk[hd][0:1, :].astype(F32))
        return kb - 1, jnp.max(car_ref[...])

    lax.while_loop(cond, body, (qi, jnp.float32(0.0)))
    o_ref[0] = acc_ref[...].T.astype(BF16)


def _strict_tri(n, lower):
    i = np.arange(n)
    m = (i[:, None] > i[None, :]) if lower else (i[:, None] < i[None, :])
    return jnp.asarray(m.astype(np.float32), dtype=BF16)


def _sb_attention(sq, sk, svt, tq):
    batch, _, seq, _ = sq.shape
    return pl.pallas_call(
        functools.partial(_sb_kernel, tq=tq),
        grid=(batch, seq // tq),
        in_specs=[pl.BlockSpec((1, SB_HEADS, tq, HEAD_DIM), lambda b, i: (b, 0, i, 0)),
                  pl.BlockSpec((1, SB_HEADS, seq, HEAD_DIM), lambda b, i: (b, 0, 0, 0)),
                  pl.BlockSpec((1, BRANCH_W, seq), lambda b, i: (b, 0, 0)),
                  pl.BlockSpec((tq, tq), lambda b, i: (0, 0))],
        out_specs=pl.BlockSpec((1, tq, BRANCH_W), lambda b, i: (b, i, 0)),
        out_shape=jax.ShapeDtypeStruct((batch, seq, BRANCH_W), BF16),
        scratch_shapes=[pltpu.VMEM((BRANCH_W, tq), F32), pltpu.VMEM((SB_HEADS, tq), F32)],
        compiler_params=_cparams(("parallel", "parallel")),
        name="sb_attention",
    )(sq, sk, svt, _strict_tri(tq, lower=False))


DSA_SEG = 256
DSA_KB = DSA_SEG


def _tree_sum(parts):
    while len(parts) > 1:
        parts = [parts[i] + parts[i + 1] for i in range(0, len(parts) - 1, 2)] + ([parts[-1]] if len(parts) % 2 else [])
    return parts[0]


def _dsa_kernel(dq_ref, iq_ref, iw_ref, kvi_ref, vt_ref, tri_ref, o_ref,
                sc_ref, bias_ref, s_ref, *, tq, topk, nseg_max):
    nseg = (pl.program_id(1) * tq) // DSA_SEG + 1
    for ns in range(1, nseg_max + 1):
        @pl.when(nseg == ns)
        def _(ns=ns):
            _dsa_body(dq_ref, iq_ref, iw_ref, kvi_ref, vt_ref, tri_ref, o_ref, sc_ref, bias_ref, s_ref,
                      tq=tq, topk=topk, nseg=ns)


def _dsa_body(dq_ref, iq_ref, iw_ref, kvi_ref, vt_ref, tri_ref, o_ref, sc_ref, bias_ref, s_ref, *, tq, topk, nseg):
    qs = pl.program_id(1) * tq
    blocks = [slice(c * DSA_KB, (c + 1) * DSA_KB) for c in range(nseg)]
    iq = iq_ref[0]
    w_t = iw_ref[0].T
    w_row = [w_t[IW_LANE + h:IW_LANE + h + 1, :] * (IDX_HEADS ** -0.5) for h in range(IDX_HEADS)]
    q_chunk = (qs + lax.broadcasted_iota(I32, (DSA_KB, tq), 1)) // CHUNK
    k_chunk = lax.broadcasted_iota(I32, (DSA_KB, tq), 0) // CHUNK

    for c, blk in enumerate(blocks):
        ik = kvi_ref[0, blk, 128:192]
        lg = [_dot_nt(ik, iq[:, h * HEAD_DIM:(h + 1) * HEAD_DIM]) for h in range(IDX_HEADS)]
        sc = jnp.zeros((DSA_KB, tq), F32)
        for h in range(IDX_HEADS):
            sc = sc + w_row[h] * jnp.maximum(lg[h], 0.0)
        admissible = (c * (DSA_KB // CHUNK) + k_chunk) <= q_chunk
        sc_ref[blk, :] = jnp.where(admissible, sc, -jnp.inf)

    def count(pred_fn):
        sub, lanes_of_sums = 32, 4
        acc = [jnp.zeros((sub, tq), F32)] * lanes_of_sums
        for j in range(nseg * DSA_SEG // sub):
            acc[j % lanes_of_sums] = acc[j % lanes_of_sums] + jnp.where(
                pred_fn(sc_ref[j * sub:(j + 1) * sub, :]), 1.0, 0.0)
        return jnp.sum(_tree_sum(acc), axis=0, keepdims=True)

    def key_to_float(key):
        return lax.bitcast_convert_type(jnp.where(key >= 0, key, key ^ 0x7FFFFFFF), F32)

    kf = jnp.float32(topk)
    n_rows = jnp.float32(nseg * DSA_SEG)
    t0 = jnp.where(count(lambda s: s >= 0.0) >= kf, 0, INT_MIN).astype(I32)

    def bit_step(i, t):
        cand = t + lax.shift_left(jnp.int32(1), 30 - i)
        cand_f = key_to_float(cand)
        cnt = jnp.where(cand <= KEY_NEG_INF, n_rows, count(lambda s: s >= cand_f))
        return jnp.where(cnt >= kf, cand, t)

    thr_key = lax.fori_loop(0, 31, bit_step, t0)
    thr = jnp.where(thr_key <= KEY_NEG_INF, -jnp.inf, key_to_float(thr_key))
    need = kf - count(lambda s: s > thr)
    tri = tri_ref[...]

    surplus = jnp.logical_or(jnp.max(count(lambda s: s >= thr)) > kf, jnp.min(thr) == -jnp.inf)

    @pl.when(surplus)
    def _():
        prefix = jnp.zeros((1, tq), F32)
        for blk in blocks:
            sc = sc_ref[blk, :]
            eqf = jnp.where(sc == thr, 1.0, 0.0)
            rank = _dot(tri, eqf.astype(BF16)) + prefix
            tie = jnp.where(rank < need, eqf, 0.0)
            sel = jnp.where(sc > thr, 1.0, tie)
            bias_ref[blk, :] = jnp.where(sc > -jnp.inf, (sel - 1.0) * (-NEG_BIG), NEG_BIG)
            prefix = prefix + jnp.sum(eqf, axis=0, keepdims=True)

    @pl.when(jnp.logical_not(surplus))
    def _():
        for blk in blocks:
            bias_ref[blk, :] = jnp.where(sc_ref[blk, :] >= thr, 0.0, NEG_BIG)

    q8 = dq_ref[0].reshape(DSA_HEADS * tq, HEAD_DIM)
    m = jnp.full((1, DSA_HEADS * tq), NEG_BIG, F32)
    for blk in blocks:
        b = bias_ref[blk, :]
        s = _dot_nt(kvi_ref[0, blk, 0:HEAD_DIM], q8) + jnp.concatenate([b] * DSA_HEADS, axis=1)
        s_ref[blk, :] = s
        m = jnp.maximum(m, jnp.max(s, axis=0, keepdims=True))
    acc = jnp.zeros((128, DSA_HEADS * tq), F32)
    for blk in blocks:
        p = jnp.exp(s_ref[blk, :] - m)
        acc = acc + _dot(vt_ref[0, :, blk], p.astype(BF16))
    for hd in range(DSA_HEADS):
        a = acc[:, hd * tq:(hd + 1) * tq]
        o = (a / a[0:1, :]).T
        o_ref[0, :, hd * HEAD_DIM:(hd + 1) * HEAD_DIM] = o[:, HEAD_DIM:].astype(BF16)


def _dsa_attention(dq, iq, iw, kvi, vt, tq, topk):
    batch, _, seq, _ = dq.shape
    return pl.pallas_call(
        functools.partial(_dsa_kernel, tq=tq, topk=topk, nseg_max=seq // DSA_SEG),
        grid=(batch, seq // tq),
        in_specs=[
            pl.BlockSpec((1, DSA_HEADS, tq, HEAD_DIM), lambda b, i: (b, 0, i, 0)),
            pl.BlockSpec((1, tq, 256), lambda b, i: (b, i, 0)),
            pl.BlockSpec((1, tq, 128), lambda b, i: (b, i, 0)),
            pl.BlockSpec((1, seq, 256), lambda b, i: (b, 0, 0)),
            pl.BlockSpec((1, 128, seq), lambda b, i: (b, 0, 0)),
            pl.BlockSpec((DSA_KB, DSA_KB), lambda b, i: (0, 0)),
        ],
        out_specs=pl.BlockSpec((1, tq, BRANCH_W), lambda b, i: (b, i, 0)),
        out_shape=jax.ShapeDtypeStruct((batch, seq, BRANCH_W), BF16),
        scratch_shapes=[pltpu.VMEM((seq, tq), F32), pltpu.VMEM((seq, tq), F32),
                        pltpu.VMEM((seq, DSA_HEADS * tq), F32)],
        compiler_params=_cparams(("parallel", "parallel")),
        name="dsa_attention",
    )(dq, iq.reshape(batch, seq, 256), iw.reshape(batch, seq, 128), kvi.reshape(batch, seq, 256), vt,
      _strict_tri(DSA_KB, lower=True))


def _memkv_kernel(m_ref, g_ref, w_ref, gk_ref, bd_ref, mk_ref, mv_ref):
    x = m_ref[...]
    ms = jnp.mean(x * x, axis=-1, keepdims=True)
    h = (x * lax.rsqrt(ms + EPS) * g_ref[...]).astype(BF16)
    mw = MEM_HEADS * MEM_HEAD_DIM
    k = _dot(h, w_ref[:, :mw])
    msk = _dot_split(k * k, bd_ref[...]) * (1.0 / MEM_HEAD_DIM)
    mk_ref[...] = (k * lax.rsqrt(msk + EPS) * gk_ref[...]).astype(BF16)
    mv_ref[...] = _dot(h, w_ref[:, mw:]).astype(BF16)


def _mem_kv(mem2d, g_mem, w_mem_kv, g_k_mem, tm):
    n = mem2d.shape[0]
    mw = MEM_HEADS * MEM_HEAD_DIM
    row = lambda i: (i, 0)
    const = lambda i: (0, 0)
    return pl.pallas_call(
        _memkv_kernel,
        grid=(n // tm,),
        in_specs=[pl.BlockSpec((tm, D_MODEL), row), pl.BlockSpec((1, D_MODEL), const),
                  pl.BlockSpec((D_MODEL, 2 * mw), const), pl.BlockSpec((1, mw), const),
                  pl.BlockSpec((mw, mw), const)],
        out_specs=[pl.BlockSpec((tm, mw), row), pl.BlockSpec((tm, mw), row)],
        out_shape=[jax.ShapeDtypeStruct((n, mw), BF16), jax.ShapeDtypeStruct((n, mw), BF16)],
        compiler_params=_cparams(("parallel",)),
        name="mem_kv",
    )(mem2d, g_mem[None, :], w_mem_kv.astype(BF16), jnp.tile(g_k_mem, MEM_HEADS)[None, :],
      _block_diag_ones(mw, MEM_HEAD_DIM))


def _memattn_kernel(q_ref, k_ref, v_ref, o_ref):
    for hd in range(MEM_HEADS):
        sl = slice(hd * MEM_HEAD_DIM, (hd + 1) * MEM_HEAD_DIM)
        s = _dot_nt(q_ref[0, :, sl], k_ref[0, :, sl]) * (MEM_HEAD_DIM ** -0.5)
        p = jnp.exp(s - jnp.max(s, axis=-1, keepdims=True))
        o = _dot(p.astype(BF16), v_ref[0, :, sl]) / jnp.sum(p, axis=-1, keepdims=True)
        o_ref[0, :, sl] = o.astype(BF16)


def _mem_attention(mq, mk, mv, tq):
    batch, seq, mw = mq.shape
    mlen = mk.shape[1]
    kspec = pl.BlockSpec((1, mlen, mw), lambda b, i: (b, 0, 0))
    return pl.pallas_call(
        _memattn_kernel,
        grid=(batch, seq // tq),
        in_specs=[pl.BlockSpec((1, tq, mw), lambda b, i: (b, i, 0)), kspec, kspec],
        out_specs=pl.BlockSpec((1, tq, mw), lambda b, i: (b, i, 0)),
        out_shape=jax.ShapeDtypeStruct((batch, seq, mw), BF16),
        compiler_params=_cparams(("parallel", "parallel")),
        name="mem_attention",
    )(mq, mk, mv)


def _merge_kernel(x_ref, g_ref, osb_ref, odsa_ref, omem_ref, wg_ref, bg_ref, wb_ref, wo_ref,
                  gf_ref, wrh_ref, wrl_ref, br_ref, tri_ref,
                  x2_ref, h2_ref, idx_ref, gate_ref, rank_ref, cnt_ref, carry_ref):
    x = x_ref[...]
    ms = jnp.mean(x * x, axis=-1, keepdims=True)
    h = (x * lax.rsqrt(ms + EPS) * g_ref[...]).astype(BF16)
    merged = None
    for n, o_ref in enumerate((osb_ref, odsa_ref, omem_ref)):
        gate = jax.nn.sigmoid(_dot(h, wg_ref[n]) + bg_ref[n])
        term = gate * _dot(o_ref[...], wb_ref[n])
        merged = term if merged is None else merged + term
    x2 = x + _dot(merged.astype(BF16), wo_ref[...])
    x2_ref[...] = x2

    ms2 = jnp.mean(x2 * x2, axis=-1, keepdims=True)
    h2 = x2 * lax.rsqrt(ms2 + EPS) * gf_ref[...]
    _store_token_tiles(h2_ref, h2)
    hi, lo = _split_bf16(h2)
    logits = _dot(hi, wrh_ref[...]) + _dot(hi, wrl_ref[...]) + _dot(lo, wrh_ref[...]) + br_ref[...]

    lane = lax.broadcasted_iota(I32, logits.shape, 1)
    vals = logits
    top_v, top_i = [], []
    for _ in range(TOP_K):
        mx = jnp.max(vals, axis=-1, keepdims=True)
        ix = jnp.min(jnp.where(vals == mx, lane, N_EXPERTS), axis=-1, keepdims=True)
        top_v.append(mx)
        top_i.append(ix)
        vals = jnp.where(lane == ix, -jnp.inf, vals)
    ex = [jnp.exp(v - top_v[0]) for v in top_v]
    den = ex[0] + ex[1] + ex[2] + ex[3]
    for k in range(TOP_K):
        idx_ref[:, k:k + 1] = top_i[k]
        gate_ref[:, k:k + 1] = ex[k] / den

    @pl.when(pl.program_id(0) == 0)
    def _():
        carry_ref[...] = jnp.zeros_like(carry_ref)

    lane_e = lax.broadcasted_iota(I32, (x.shape[0], LANES), 1)
    hits = [lane_e == top_i[k] for k in range(TOP_K)]
    onehot = sum(jnp.where(hk, 1.0, 0.0) for hk in hits)
    before = _dot(tri_ref[...], onehot.astype(BF16)) + carry_ref[...]
    for k in range(TOP_K):
        rank_ref[:, k:k + 1] = jnp.sum(jnp.where(hits[k], before, 0.0), axis=-1, keepdims=True).astype(I32)
    carry_ref[...] += jnp.sum(onehot, axis=0, keepdims=True)
    cnt_ref[...] = carry_ref[...]


def _merge_route(x2d, g_mix, o_sb, o_dsa, o_mem, w_gate, b_gate, w_branch, w_out, g_ffn, w_router, b_router, tm):
    n = x2d.shape[0]
    row = lambda i: (i, 0)
    c2 = lambda i: (0, 0)
    c3 = lambda i: (0, 0, 0)
    wr_hi = w_router.astype(BF16)
    wr_lo = (w_router - wr_hi.astype(F32)).astype(BF16)
    return pl.pallas_call(
        _merge_kernel,
        grid=(n // tm,),
        in_specs=[pl.BlockSpec((tm, D_MODEL), row), pl.BlockSpec((1, D_MODEL), c2),
                  pl.BlockSpec((tm, BRANCH_W), row), pl.BlockSpec((tm, BRANCH_W), row),
                  pl.BlockSpec((tm, BRANCH_W), row),
                  pl.BlockSpec((3, D_MODEL, D_MODEL), c3, pipeline_mode=pl.Buffered(1)),
                  pl.BlockSpec((3, 1, D_MODEL), c3),
                  pl.BlockSpec((3, BRANCH_W, D_MODEL), c3, pipeline_mode=pl.Buffered(1)),
                  pl.BlockSpec((D_MODEL, D_MODEL), c2, pipeline_mode=pl.Buffered(1)),
                  pl.BlockSpec((1, D_MODEL), c2), pl.BlockSpec((D_MODEL, N_EXPERTS), c2),
                  pl.BlockSpec((D_MODEL, N_EXPERTS), c2), pl.BlockSpec((1, N_EXPERTS), c2),
                  pl.BlockSpec((tm, tm), c2, pipeline_mode=pl.Buffered(1))],
        out_specs=[pl.BlockSpec((tm, D_MODEL), row), pl.BlockSpec((tm * TOKEN_TILE, LANES), row),
                   pl.BlockSpec((tm, TOP_K), row), pl.BlockSpec((tm, TOP_K), row),
                   pl.BlockSpec((tm, TOP_K), row), pl.BlockSpec((1, LANES), c2)],
        out_shape=[jax.ShapeDtypeStruct((n, D_MODEL), F32), jax.ShapeDtypeStruct((n * TOKEN_TILE, LANES), F32),
                   jax.ShapeDtypeStruct((n, TOP_K), I32), jax.ShapeDtypeStruct((n, TOP_K), F32),
                   jax.ShapeDtypeStruct((n, TOP_K), I32), jax.ShapeDtypeStruct((1, LANES), F32)],
        scratch_shapes=[pltpu.VMEM((1, LANES), F32)],
        compiler_params=_cparams(("arbitrary",)),
        name="merge_route",
    )(x2d, g_mix[None, :], o_sb, o_dsa, o_mem, w_gate.astype(BF16), b_gate[:, None, :],
      w_branch.astype(BF16), w_out.astype(BF16), g_ffn[None, :], wr_hi, wr_lo, b_router[None, :],
      _strict_tri(tm, lower=True))


DISPATCH_TOKENS = 256


def _dispatch_kernel(dest_ref, pad_ref, h_ref, xs_ref, zero_ref, sem, *, n_pad):
    n_copies = DISPATCH_TOKENS * TOP_K + n_pad
    zero_ref[...] = jnp.zeros_like(zero_ref)

    def fill(j, c):
        for r in range(2):
            dst = pl.multiple_of(pad_ref[0, 0, 2 * j + r] * TOKEN_TILE, TOKEN_TILE)
            pltpu.make_async_copy(zero_ref, xs_ref.at[pl.ds(dst, TOKEN_TILE)], sem).start(priority=r)
        return c

    lax.fori_loop(0, n_pad // 2, fill, 0, unroll=2)

    def start(t, c):
        src = pl.multiple_of(t * TOKEN_TILE, TOKEN_TILE)
        for k in range(TOP_K):
            dst = pl.multiple_of(dest_ref[0, 0, t * TOP_K + k] * TOKEN_TILE, TOKEN_TILE)
            pltpu.make_async_copy(h_ref.at[pl.ds(src, TOKEN_TILE)], xs_ref.at[pl.ds(dst, TOKEN_TILE)],
                                  sem).start(priority=k % 2)
        return c

    lax.fori_loop(0, DISPATCH_TOKENS, start, 0, unroll=2)
    total = n_copies * TOKEN_TILE
    pltpu.make_async_copy(xs_ref.at[pl.ds(0, total)], xs_ref.at[pl.ds(0, total)], sem).wait()


def _dispatch(dest, pad_slots, h2t, n_slots):
    n = h2t.shape[0] // TOKEN_TILE
    steps = n // DISPATCH_TOKENS
    per = DISPATCH_TOKENS * TOP_K
    n_pad = pad_slots.shape[0] // steps
    assert n_pad * steps == pad_slots.shape[0] and n_pad % 2 == 0
    return pl.pallas_call(
        functools.partial(_dispatch_kernel, n_pad=n_pad),
        grid=(steps,),
        in_specs=[pl.BlockSpec((1, 1, per), lambda i: (i, 0, 0), memory_space=pltpu.SMEM),
                  pl.BlockSpec((1, 1, n_pad), lambda i: (i, 0, 0), memory_space=pltpu.SMEM),
                  pl.BlockSpec((DISPATCH_TOKENS * TOKEN_TILE, LANES), lambda i: (i, 0))],
        out_specs=pl.BlockSpec(memory_space=pl.ANY),
        out_shape=jax.ShapeDtypeStruct((n_slots * TOKEN_TILE, LANES), F32),
        scratch_shapes=[pltpu.VMEM((TOKEN_TILE, LANES), F32), pltpu.SemaphoreType.DMA(())],
        compiler_params=_cparams(("arbitrary",)),
        name="moe_dispatch",
    )(dest.reshape(steps, 1, per), pad_slots.reshape(steps, 1, n_pad), h2t)


def _expert_kernel(blk_e_ref, nused_ref, x_ref, w1_ref, b1_ref, w2_ref, b2_ref, y_ref, xb_ref, w1b_ref, w2b_ref):
    i = pl.program_id(0)
    new_expert = jnp.logical_or(i == 0, blk_e_ref[i] != blk_e_ref[jnp.maximum(i - 1, 0)])

    @pl.when(jnp.logical_and(i < nused_ref[0], new_expert))
    def _():
        w1b_ref[...] = w1_ref[0].astype(BF16)
        w2b_ref[...] = w2_ref[0].astype(BF16)

    @pl.when(i < nused_ref[0])
    def _():
        for s in range(TOKEN_TILE):
            xb_ref[:, s * LANES:(s + 1) * LANES] = _load_token_tiles(x_ref, 0, MOE_BLOCK, s).astype(BF16)
        hb = _dot(xb_ref[...], w1b_ref[...]) + b1_ref[0]
        g = jnp.minimum(hb[:, :D_EXPERT], SWIGLU_LIMIT)
        u = jnp.clip(hb[:, D_EXPERT:], -SWIGLU_LIMIT, SWIGLU_LIMIT)
        act = (u + 1.0) * (g * jax.nn.sigmoid(SWIGLU_ALPHA * g))
        _store_token_tiles(y_ref, _dot(act.astype(BF16), w2b_ref[...]) + b2_ref[0])

    @pl.when(pl.program_id(0) >= nused_ref[0])
    def _():
        y_ref[...] = jnp.zeros_like(y_ref)


def _experts(blk_e, n_used, xs, w_e_in, b_e_in, w_e_out, b_e_out):
    n_slots = xs.shape[0] // TOKEN_TILE
    nblk = n_slots // MOE_BLOCK
    slot_block = pl.BlockSpec((MOE_BLOCK * TOKEN_TILE, LANES), lambda i, be, nu: (i, 0))
    grid_spec = pltpu.PrefetchScalarGridSpec(
        num_scalar_prefetch=2,
        grid=(nblk,),
        in_specs=[slot_block,
                  pl.BlockSpec((1, D_MODEL, 2 * D_EXPERT), lambda i, be, nu: (be[i], 0, 0)),
                  pl.BlockSpec((1, 1, 2 * D_EXPERT), lambda i, be, nu: (be[i], 0, 0)),
                  pl.BlockSpec((1, D_EXPERT, D_MODEL), lambda i, be, nu: (be[i], 0, 0)),
                  pl.BlockSpec((1, 1, D_MODEL), lambda i, be, nu: (be[i], 0, 0))],
        out_specs=slot_block,
        scratch_shapes=[pltpu.VMEM((MOE_BLOCK, D_MODEL), BF16), pltpu.VMEM((D_MODEL, 2 * D_EXPERT), BF16),
                        pltpu.VMEM((D_EXPERT, D_MODEL), BF16)],
    )
    return pl.pallas_call(
        _expert_kernel,
        grid_spec=grid_spec,
        out_shape=jax.ShapeDtypeStruct((n_slots * TOKEN_TILE, LANES), F32),
        compiler_params=_cparams(("arbitrary",)),
        name="moe_experts",
    )(blk_e, n_used, xs, w_e_in, b_e_in[:, None, :], w_e_out, b_e_out[:, None, :])


def _combine_kernel(dest_ref, dest_next_ref, x_ref, gate_ref, y_ref, o_ref, buf_ref, sem):
    n_copies = DISPATCH_TOKENS * TOP_K
    i = pl.program_id(0)
    slot = i % 2

    def start_gathers(idx_ref, buf, dma_sem):
        def start(t, c):
            for k in range(TOP_K):
                src = pl.multiple_of(idx_ref[0, 0, t * TOP_K + k] * TOKEN_TILE, TOKEN_TILE)
                dst = pl.multiple_of((k * DISPATCH_TOKENS + t) * TOKEN_TILE, TOKEN_TILE)
                pltpu.make_async_copy(y_ref.at[pl.ds(src, TOKEN_TILE)], buf.at[pl.ds(dst, TOKEN_TILE)],
                                      dma_sem).start(priority=k % 2)
            return c
        lax.fori_loop(0, DISPATCH_TOKENS, start, 0, unroll=2)

    @pl.when(i == 0)
    def _():
        start_gathers(dest_ref, buf_ref.at[0], sem.at[0])

    @pl.when(i + 1 < pl.num_programs(0))
    def _():
        start_gathers(dest_next_ref, buf_ref.at[1 - slot], sem.at[1 - slot])

    buf = buf_ref.at[slot]
    pltpu.make_async_copy(y_ref.at[pl.ds(0, n_copies * TOKEN_TILE)], buf, sem.at[slot]).wait()
    rows = 32
    for r0 in range(0, DISPATCH_TOKENS, rows):
        gate = gate_ref[r0:r0 + rows, :]
        gates = [jnp.broadcast_to(gate[:, k:k + 1], (rows, LANES)) for k in range(TOP_K)]
        for s in range(TOKEN_TILE):
            out = x_ref[r0:r0 + rows, s * LANES:(s + 1) * LANES]
            for k in range(TOP_K):
                out = out + gates[k] * _load_token_tiles(buf, k * DISPATCH_TOKENS + r0, rows, s)
            o_ref[r0:r0 + rows, s * LANES:(s + 1) * LANES] = out


def _combine(dest, x2, gate, ys):
    n = x2.shape[0]
    steps = n // DISPATCH_TOKENS
    per = DISPATCH_TOKENS * TOP_K
    dest3 = dest.reshape(steps, 1, per)
    return pl.pallas_call(
        _combine_kernel,
        grid=(steps,),
        in_specs=[pl.BlockSpec((1, 1, per), lambda i: (i, 0, 0), memory_space=pltpu.SMEM),
                  pl.BlockSpec((1, 1, per), lambda i: (jnp.minimum(i + 1, steps - 1), 0, 0), memory_space=pltpu.SMEM),
                  pl.BlockSpec((DISPATCH_TOKENS, D_MODEL), lambda i: (i, 0)),
                  pl.BlockSpec((DISPATCH_TOKENS, TOP_K), lambda i: (i, 0)),
                  pl.BlockSpec(memory_space=pl.ANY)],
        out_specs=pl.BlockSpec((DISPATCH_TOKENS, D_MODEL), lambda i: (i, 0)),
        out_shape=jax.ShapeDtypeStruct((n, D_MODEL), F32),
        scratch_shapes=[pltpu.VMEM((2, TOP_K * DISPATCH_TOKENS * TOKEN_TILE, LANES), F32),
                        pltpu.SemaphoreType.DMA((2,))],
        compiler_params=_cparams(("arbitrary",)),
        name="moe_combine",
    )(dest3, dest3, x2, gate, ys)


def _moe(x2, h2, top_idx, gate, rank, counts, w_e_in, b_e_in, w_e_out, b_e_out):
    n = x2.shape[0]
    counts = counts[0, :N_EXPERTS].astype(I32)
    padded = (counts + MOE_BLOCK - 1) // MOE_BLOCK * MOE_BLOCK
    pend = jnp.cumsum(padded)
    pstart = pend - padded
    nblk = -(-(n * TOP_K) // MOE_BLOCK) + N_EXPERTS
    blk_start = jnp.arange(nblk, dtype=I32) * MOE_BLOCK
    blk_e = jnp.minimum(jnp.sum((pend[None, :] <= blk_start[:, None]).astype(I32), axis=1), N_EXPERTS - 1)
    n_used = (pend[-1:] // MOE_BLOCK).astype(I32)
    onehot = top_idx[:, :, None] == jnp.arange(N_EXPERTS, dtype=I32)[None, None, :]
    dest = rank + jnp.sum(jnp.where(onehot, pstart[None, None, :], 0), axis=-1)
    n_slots = nblk * MOE_BLOCK
    pad_len = padded - counts
    pad_end = jnp.cumsum(pad_len)
    base = jnp.concatenate([pstart + counts - (pad_end - pad_len), pend[-1:] - pad_end[-1:]])
    j = jnp.arange(n_slots - n * TOP_K, dtype=I32)
    group = jnp.sum((pad_end[None, :] <= j[:, None]).astype(I32), axis=1)
    group_hot = group[:, None] == jnp.arange(N_EXPERTS + 1, dtype=I32)[None, :]
    pad_slots = j + jnp.sum(jnp.where(group_hot, base[None, :], 0), axis=1)
    xs = _dispatch(dest, pad_slots, h2, n_slots)
    ys = _experts(blk_e, n_used, xs, w_e_in, b_e_in, w_e_out, b_e_out)
    return _combine(dest, x2, gate, ys)


def _layer(x, mem, g_mix, w_in, g_q_dsa, g_k_dsa, g_q_mem, g_k_mem, g_mem, w_mem_kv, w_gate, b_gate,
           w_branch, w_out, g_ffn, w_router, b_router, w_e_in, b_e_in, w_e_out, b_e_out):
    batch, seq, _ = x.shape
    n = batch * seq
    topk = min(DSA_TOPK_MAX, seq // 4)
    x2d = x.reshape(n, D_MODEL)
    tm = min(512, seq)
    tm_dense = min(DENSE_ROWS, seq)
    sq, sk, sv, dq, iq, kvi, iw, mq, vt = _inproj(x2d, batch, seq, g_mix, w_in, g_q_dsa, g_k_dsa, g_q_mem, tm_dense)
    o_sb = _sb_attention(sq, sk, sv, min(256, seq))
    o_dsa = _dsa_attention(dq, iq, iw, kvi, vt, 128, topk)
    mlen = mem.shape[1]
    mk, mv = _mem_kv(mem.reshape(batch * mlen, D_MODEL), g_mem, w_mem_kv, g_k_mem, min(512, batch * mlen))
    mw = MEM_HEADS * MEM_HEAD_DIM
    o_mem = _mem_attention(mq.reshape(batch, seq, mw), mk.reshape(batch, mlen, mw), mv.reshape(batch, mlen, mw), tm)
    x2, h2, top_idx, gate, rank, counts = _merge_route(
        x2d, g_mix, o_sb.reshape(n, BRANCH_W), o_dsa.reshape(n, BRANCH_W), o_mem.reshape(n, mw),
        w_gate, b_gate, w_branch, w_out, g_ffn, w_router, b_router, tm_dense)
    out = _moe(x2, h2, top_idx, gate, rank, counts, w_e_in, b_e_in, w_e_out, b_e_out)
    return out.reshape(batch, seq, D_MODEL)


def kernel(x, mem, g_mix, w_in, g_q_dsa, g_k_dsa, g_q_mem, g_k_mem, g_mem, w_mem_kv, w_gate, b_gate, w_branch, w_out, g_ffn, w_router, b_router, w_e_in, b_e_in, w_e_out, b_e_out):
    for l in range(g_mix.shape[0]):
        x = _layer(x, mem, g_mix[l], w_in[l], g_q_dsa[l], g_k_dsa[l], g_q_mem[l], g_k_mem[l], g_mem[l],
                   w_mem_kv[l], w_gate[l], b_gate[l], w_branch[l], w_out[l], g_ffn[l], w_router[l],
                   b_router[l], w_e_in[l], b_e_in[l], w_e_out[l], b_e_out[l])
    return x
```

```python
import functools

import numpy as np
import jax
import jax.numpy as jnp
from jax import lax
from jax.experimental import pallas as pl
from jax.experimental.pallas import tpu as pltpu

F32 = jnp.float32
BF16 = jnp.bfloat16
I32 = jnp.int32

D_MODEL = 1024
CHUNK = 64
SB_HEADS = 8
DSA_HEADS = 8
HEAD_DIM = 64
IDX_HEADS = 4
DSA_TOPK_MAX = 256
MEM_HEADS = 4
MEM_HEAD_DIM = 128
N_EXPERTS = 32
TOP_K = 4
D_EXPERT = D_MODEL
SWIGLU_LIMIT = 7.0
SWIGLU_ALPHA = 1.702
ROPE_THETA = 10000.0
EPS = 1e-6
MOE_BLOCK = 512

BRANCH_W = 512
IN_SIZES = (512, 512, 512, 512, 64, 64, 256, 64, 4, 512)
C_SQ, C_SK, C_SV, C_DQ, C_MQ, C_IQ, C_SMALL, C_END = 0, 512, 1024, 1536, 2048, 2560, 2816, 3072
IW_LANE = 64

LANES = 128
NEG_BIG = -1e30
SB_CUTOFF = 110.0
KEY_NEG_INF = int(np.array(-np.inf, np.float32).view(np.int32)) ^ 0x7FFFFFFF
INT_MIN = -(2 ** 31)

VMEM_LIMIT = 56 * 1024 * 1024
DENSE_ROWS = 1024
COL_BLOCK = 256


def _cparams(sem):
    return pltpu.CompilerParams(dimension_semantics=sem, vmem_limit_bytes=VMEM_LIMIT)


def _dot(a, b):
    return jnp.dot(a, b, preferred_element_type=F32)


def _dot_nt(a, b):
    return lax.dot_general(a, b, (((1,), (1,)), ((), ())), preferred_element_type=F32)


def _split_bf16(x):
    hi = x.astype(BF16)
    lo = (x - hi.astype(F32)).astype(BF16)
    return hi, lo


def _dot_split(x, m_bf16):
    hi, lo = _split_bf16(x)
    return _dot(hi, m_bf16) + _dot(lo, m_bf16)


TOKEN_TILE = D_MODEL // LANES


def _store_token_tiles(ref, y, first_chunk=0):
    rows = y.shape[0]
    for c in range(y.shape[1] // LANES):
        ref[pl.ds(first_chunk + c, rows, stride=TOKEN_TILE), :] = y[:, c * LANES:(c + 1) * LANES]


def _load_token_tiles(ref, start_row, rows, s):
    return ref[pl.ds(start_row * TOKEN_TILE + s, rows, stride=TOKEN_TILE), :]


def _rot_half_unsigned(y):
    w = y.shape[1]
    lane = lax.broadcasted_iota(I32, y.shape, 1)
    return jnp.where((lane & 32) == 0, pltpu.roll(y, w - 32, 1), pltpu.roll(y, 32, 1))


def _inproj_kernel(x_ref, g_ref, w_ref, wsvt_ref, cos_ref, sin_ref, coss_ref, sins_ref, gq_ref, gks_ref,
                   gm_ref, bd64_ref, bd128_ref,
                   sq_ref, sk_ref, sv_ref, dq_ref, iq_ref, kvi_ref, iw_ref, mq_ref, vt_ref):
    x = x_ref[...]
    ms = jnp.mean(x * x, axis=-1, keepdims=True)
    h = (x * lax.rsqrt(ms + EPS) * g_ref[...]).astype(BF16)

    def seg(a, b):
        return _dot(h, w_ref[:, a:b])

    def put_heads(ref, y):
        for hd in range(y.shape[1] // HEAD_DIM):
            ref[0, hd] = y[:, hd * HEAD_DIM:(hd + 1) * HEAD_DIM].astype(BF16)

    put_heads(sq_ref, seg(C_SQ, C_SK) * (HEAD_DIM ** -0.5))
    put_heads(sk_ref, seg(C_SK, C_SV))
    sv_ref[0] = _dot_nt(wsvt_ref[...], h).astype(BF16)

    y = seg(C_DQ, C_MQ)
    msq = _dot_split(y * y, bd64_ref[...]) * (1.0 / HEAD_DIM)
    y = y * lax.rsqrt(msq + EPS) * gq_ref[...]
    y = y * cos_ref[...] + _rot_half_unsigned(y) * sin_ref[...]
    put_heads(dq_ref, y * (HEAD_DIM ** -0.5))

    y = seg(C_IQ, C_SMALL)
    y = y * cos_ref[:, :256] + _rot_half_unsigned(y) * sin_ref[:, :256]
    iq_ref[...] = (y * (HEAD_DIM ** -0.5)).astype(BF16)

    y = seg(C_SMALL, C_END)
    lane = lax.broadcasted_iota(I32, y.shape, 1)
    is_k = lane < HEAD_DIM
    msk = jnp.sum(jnp.where(is_k, y * y, 0.0), axis=-1, keepdims=True) * (1.0 / HEAD_DIM)
    y = y * jnp.where(is_k, lax.rsqrt(msk + EPS) * gks_ref[...], 1.0)
    y = y * coss_ref[...] + _rot_half_unsigned(y) * sins_ref[...]
    kvi_ref[...] = y.astype(BF16)
    iw_ref[...] = y[:, 128:256]
    kv_t = y[:, 0:128].T
    row_t = lax.broadcasted_iota(I32, kv_t.shape, 0)
    vt_ref[0] = jnp.where(row_t < HEAD_DIM, 1.0, kv_t).astype(BF16)

    y = seg(C_MQ, C_IQ)
    msm = _dot_split(y * y, bd128_ref[...]) * (1.0 / MEM_HEAD_DIM)
    mq_ref[...] = (y * lax.rsqrt(msm + EPS) * gm_ref[...]).astype(BF16)


def _rope_tables(seq):
    half = HEAD_DIM // 2
    inv = ROPE_THETA ** (-jnp.arange(half, dtype=F32) / half)
    ang = jnp.arange(seq).astype(F32)[:, None] * inv[None, :]
    cos = jnp.cos(ang)
    sin = jnp.sin(ang)
    cos64 = jnp.concatenate([cos, cos], axis=1)
    sin64 = jnp.concatenate([-sin, sin], axis=1)
    one = jnp.ones_like(cos64)
    zero = jnp.zeros_like(cos64)
    cosq = jnp.tile(cos64, (1, 8))
    sinq = jnp.tile(sin64, (1, 8))
    coss = jnp.concatenate([cos64, one, cos64, one], axis=1)
    sins = jnp.concatenate([sin64, zero, sin64, zero], axis=1)
    return cosq, sinq, coss, sins


def _block_diag_ones(width, group):
    idx = np.arange(width) // group
    return jnp.asarray((idx[:, None] == idx[None, :]).astype(np.float32), dtype=BF16)


def _inproj(x2d, batch, seq, g_mix, w_in, g_q_dsa, g_k_dsa, g_q_mem, tm):
    n = x2d.shape[0]
    sizes = np.cumsum((0,) + IN_SIZES)
    col = {name: (int(sizes[i]), int(sizes[i + 1])) for i, name in enumerate(
        ("sq", "sk", "sv", "dq", "dk", "dv", "iq", "ik", "iw", "mq"))}
    order = ("sq", "sk", "sv", "dq", "mq", "iq", "dk", "dv", "ik", "iw")
    w = jnp.concatenate([w_in[:, col[k][0]:col[k][1]] for k in order]
                        + [jnp.zeros((D_MODEL, C_END - sum(IN_SIZES)), w_in.dtype)], axis=1).astype(BF16)
    cosq, sinq, coss, sins = (jnp.asarray(t) for t in _rope_tables(seq))
    gq = jnp.tile(g_q_dsa, 8)[None, :]
    gks = jnp.concatenate([g_k_dsa, jnp.ones((256 - HEAD_DIM,), F32)])[None, :]
    gm = jnp.tile(g_q_mem, MEM_HEADS)[None, :]
    spb = seq // tm
    row = lambda i: (i, 0)
    const = lambda i: (0, 0)
    pos = lambda i: (i % spb, 0)
    heads = lambda i: (i // spb, 0, i % spb, 0)
    head_shape = jax.ShapeDtypeStruct((batch, 8, seq, HEAD_DIM), BF16)
    head_spec = pl.BlockSpec((1, 8, tm, HEAD_DIM), heads)
    return pl.pallas_call(
        _inproj_kernel,
        grid=(n // tm,),
        in_specs=[
            pl.BlockSpec((tm, D_MODEL), row),
            pl.BlockSpec((1, D_MODEL), const),
            pl.BlockSpec((D_MODEL, C_END), const, pipeline_mode=pl.Buffered(1)),
            pl.BlockSpec((512, D_MODEL), const, pipeline_mode=pl.Buffered(1)),
            pl.BlockSpec((tm, 512), pos), pl.BlockSpec((tm, 512), pos),
            pl.BlockSpec((tm, 256), pos), pl.BlockSpec((tm, 256), pos),
            pl.BlockSpec((1, 512), const), pl.BlockSpec((1, 256), const), pl.BlockSpec((1, 512), const),
            pl.BlockSpec((512, 512), const), pl.BlockSpec((512, 512), const),
        ],
        out_specs=[head_spec, head_spec, pl.BlockSpec((1, 512, tm), lambda i: (i // spb, 0, i % spb)), head_spec,
                   pl.BlockSpec((tm, 256), row), pl.BlockSpec((tm, 256), row),
                   pl.BlockSpec((tm, 128), row), pl.BlockSpec((tm, 512), row),
                   pl.BlockSpec((1, 128, tm), lambda i: (i // spb, 0, i % spb))],
        out_shape=[head_shape, head_shape, jax.ShapeDtypeStruct((batch, 512, seq), BF16), head_shape,
                   jax.ShapeDtypeStruct((n, 256), BF16), jax.ShapeDtypeStruct((n, 256), BF16),
                   jax.ShapeDtypeStruct((n, 128), F32), jax.ShapeDtypeStruct((n, 512), BF16),
                   jax.ShapeDtypeStruct((batch, 128, seq), BF16)],
        compiler_params=_cparams(("parallel",)),
        name="inproj",
    )(x2d, g_mix[None, :], w, w_in[:, col["sv"][0]:col["sv"][1]].T.astype(BF16), cosq, sinq, coss, sins, gq, gks, gm,
      _block_diag_ones(512, HEAD_DIM), _block_diag_ones(512, MEM_HEAD_DIM))


def _sb_kernel(q_ref, k_ref, vt_ref, u_ref, o_ref, acc_ref, car_ref, *, tq):
    qi = pl.program_id(1)
    rows = lax.broadcasted_iota(I32, (tq, tq), 0)
    cols = lax.broadcasted_iota(I32, (tq, tq), 1)
    dif = rows - cols
    u = u_ref[...]
    acc_ref[...] = jnp.zeros_like(acc_ref)
    car_ref[...] = jnp.zeros_like(car_ref)

    def cond(c):
        kb, mx = c
        return jnp.logical_and(kb >= 0, mx > -SB_CUTOFF)

    def body(c):
        kb, _ = c
        ks = pl.multiple_of(kb * tq, tq)
        earlier = dif < (qi - kb) * tq
        neg_mask = jnp.where(earlier, -1.0, 0.0).astype(BF16)
        heads = range(SB_HEADS)
        z = [_dot_nt(k_ref[0, hd, pl.ds(ks, tq), :], q_ref[0, hd]) for hd in heads]
        ls, lk, between = [], [], []
        for hd in heads:
            zb = z[hd].astype(BF16)
            sp = jnp.maximum(zb, 0.0) + jnp.log(1.0 + jnp.exp(-jnp.abs(zb)))
            ls.append(z[hd] - sp.astype(F32))
            lk.append(sp * neg_mask)
            between.append(_dot(u, lk[hd]))
        for hd in heads:
            rs = slice(hd * HEAD_DIM, (hd + 1) * HEAD_DIM)
            car = car_ref[hd:hd + 1, :]
            w = jnp.where(earlier, jnp.exp(ls[hd] + between[hd] + car), 0.0)
            acc_ref[rs, :] += _dot(vt_ref[0, rs, pl.ds(ks, tq)], w.astype(BF16))
            car_ref[hd:hd + 1, :] = car + (between[hd][0:1, :] + lk[hd][0:1, :].astype(F32))
        return kb - 1, jnp.max(car_ref[...])

    lax.while_loop(cond, body, (qi, jnp.float32(0.0)))
    o_ref[0] = acc_ref[...].T.astype(BF16)


def _strict_tri(n, lower):
    i = np.arange(n)
    m = (i[:, None] > i[None, :]) if lower else (i[:, None] < i[None, :])
    return jnp.asarray(m.astype(np.float32), dtype=BF16)


def _sb_attention(sq, sk, svt, tq):
    batch, _, seq, _ = sq.shape
    return pl.pallas_call(
        functools.partial(_sb_kernel, tq=tq),
        grid=(batch, seq // tq),
        in_specs=[pl.BlockSpec((1, SB_HEADS, tq, HEAD_DIM), lambda b, i: (b, 0, i, 0)),
                  pl.BlockSpec((1, SB_HEADS, seq, HEAD_DIM), lambda b, i: (b, 0, 0, 0)),
                  pl.BlockSpec((1, BRANCH_W, seq), lambda b, i: (b, 0, 0)),
                  pl.BlockSpec((tq, tq), lambda b, i: (0, 0))],
        out_specs=pl.BlockSpec((1, tq, BRANCH_W), lambda b, i: (b, i, 0)),
        out_shape=jax.ShapeDtypeStruct((batch, seq, BRANCH_W), BF16),
        scratch_shapes=[pltpu.VMEM((BRANCH_W, tq), F32), pltpu.VMEM((SB_HEADS, tq), F32)],
        compiler_params=_cparams(("parallel", "parallel")),
        name="sb_attention",
    )(sq, sk, svt, _strict_tri(tq, lower=False))


DSA_SEG = 256
DSA_KB = DSA_SEG


def _tree_sum(parts):
    while len(parts) > 1:
        parts = [parts[i] + parts[i + 1] for i in range(0, len(parts) - 1, 2)] + ([parts[-1]] if len(parts) % 2 else [])
    return parts[0]


def _dsa_kernel(dq_ref, iq_ref, iw_ref, kvi_ref, vt_ref, tri_ref, o_ref,
                sc_ref, bias_ref, s_ref, *, tq, topk, nseg_max):
    nseg = (pl.program_id(1) * tq) // DSA_SEG + 1
    for ns in range(1, nseg_max + 1):
        @pl.when(nseg == ns)
        def _(ns=ns):
            _dsa_body(dq_ref, iq_ref, iw_ref, kvi_ref, vt_ref, tri_ref, o_ref, sc_ref, bias_ref, s_ref,
                      tq=tq, topk=topk, nseg=ns)


def _dsa_body(dq_ref, iq_ref, iw_ref, kvi_ref, vt_ref, tri_ref, o_ref, sc_ref, bias_ref, s_ref, *, tq, topk, nseg):
    qs = pl.program_id(1) * tq
    blocks = [slice(c * DSA_KB, (c + 1) * DSA_KB) for c in range(nseg)]
    iq = iq_ref[0]
    w_t = iw_ref[0].T
    w_row = [w_t[IW_LANE + h:IW_LANE + h + 1, :] * (IDX_HEADS ** -0.5) for h in range(IDX_HEADS)]
    q_chunk = (qs + lax.broadcasted_iota(I32, (DSA_KB, tq), 1)) // CHUNK
    k_chunk = lax.broadcasted_iota(I32, (DSA_KB, tq), 0) // CHUNK

    for c, blk in enumerate(blocks):
        ik = kvi_ref[0, blk, 128:192]
        lg = [_dot_nt(ik, iq[:, h * HEAD_DIM:(h + 1) * HEAD_DIM]) for h in range(IDX_HEADS)]
        sc = jnp.zeros((DSA_KB, tq), F32)
        for h in range(IDX_HEADS):
            sc = sc + w_row[h] * jnp.maximum(lg[h], 0.0)
        admissible = (c * (DSA_KB // CHUNK) + k_chunk) <= q_chunk
        sc_ref[blk, :] = jnp.where(admissible, sc, -jnp.inf)

    def count(pred_fn):
        sub, lanes_of_sums = 32, 4
        acc = [jnp.zeros((sub, tq), F32)] * lanes_of_sums
        for j in range(nseg * DSA_SEG // sub):
            acc[j % lanes_of_sums] = acc[j % lanes_of_sums] + jnp.where(
                pred_fn(sc_ref[j * sub:(j + 1) * sub, :]), 1.0, 0.0)
        return jnp.sum(_tree_sum(acc), axis=0, keepdims=True)

    def key_to_float(key):
        return lax.bitcast_convert_type(jnp.where(key >= 0, key, key ^ 0x7FFFFFFF), F32)

    kf = jnp.float32(topk)
    n_rows = jnp.float32(nseg * DSA_SEG)
    cnt0 = count(lambda s: s >= 0.0)
    t0 = jnp.where(cnt0 >= kf, 0, INT_MIN).astype(I32)
    cnt_t0 = jnp.where(cnt0 >= kf, cnt0, n_rows)

    def bit_step(i, carry):
        t, cnt_t = carry
        cand = t + lax.shift_left(jnp.int32(1), 30 - i)
        cand_f = key_to_float(cand)
        cnt = jnp.where(cand <= KEY_NEG_INF, n_rows, count(lambda s: s >= cand_f))
        take = cnt >= kf
        return jnp.where(take, cand, t), jnp.where(take, cnt, cnt_t)

    thr_key, cnt_thr = lax.fori_loop(0, 31, bit_step, (t0, cnt_t0))
    thr = jnp.where(thr_key <= KEY_NEG_INF, -jnp.inf, key_to_float(thr_key))
    tri = tri_ref[...]

    surplus = jnp.logical_or(jnp.max(cnt_thr) > kf, jnp.min(thr) == -jnp.inf)

    @pl.when(surplus)
    def _():
        need = kf - count(lambda s: s > thr)
        prefix = jnp.zeros((1, tq), F32)
        for blk in blocks:
            sc = sc_ref[blk, :]
            eqf = jnp.where(sc == thr, 1.0, 0.0)
            rank = _dot(tri, eqf.astype(BF16)) + prefix
            tie = jnp.where(rank < need, eqf, 0.0)
            sel = jnp.where(sc > thr, 1.0, tie)
            bias_ref[blk, :] = jnp.where(sc > -jnp.inf, (sel - 1.0) * (-NEG_BIG), NEG_BIG)
            prefix = prefix + jnp.sum(eqf, axis=0, keepdims=True)

    @pl.when(jnp.logical_not(surplus))
    def _():
        for blk in blocks:
            bias_ref[blk, :] = jnp.where(sc_ref[blk, :] >= thr, 0.0, NEG_BIG)

    q8 = dq_ref[0].reshape(DSA_HEADS * tq, HEAD_DIM)
    m = jnp.full((1, DSA_HEADS * tq), NEG_BIG, F32)
    for blk in blocks:
        b = bias_ref[blk, :]
        s = _dot_nt(kvi_ref[0, blk, 0:HEAD_DIM], q8) + jnp.concatenate([b] * DSA_HEADS, axis=1)
        s_ref[blk, :] = s
        m = jnp.maximum(m, jnp.max(s, axis=0, keepdims=True))
    acc = jnp.zeros((128, DSA_HEADS * tq), F32)
    for blk in blocks:
        p = jnp.exp(s_ref[blk, :] - m)
        acc = acc + _dot(vt_ref[0, :, blk], p.astype(BF16))
    for hd in range(DSA_HEADS):
        a = acc[:, hd * tq:(hd + 1) * tq]
        o = (a / a[0:1, :]).T
        o_ref[0, :, hd * HEAD_DIM:(hd + 1) * HEAD_DIM] = o[:, HEAD_DIM:].astype(BF16)


def _dsa_attention(dq, iq, iw, kvi, vt, tq, topk):
    batch, _, seq, _ = dq.shape
    return pl.pallas_call(
        functools.partial(_dsa_kernel, tq=tq, topk=topk, nseg_max=seq // DSA_SEG),
        grid=(batch, seq // tq),
        in_specs=[
            pl.BlockSpec((1, DSA_HEADS, tq, HEAD_DIM), lambda b, i: (b, 0, i, 0)),
            pl.BlockSpec((1, tq, 256), lambda b, i: (b, i, 0)),
            pl.BlockSpec((1, tq, 128), lambda b, i: (b, i, 0)),
            pl.BlockSpec((1, seq, 256), lambda b, i: (b, 0, 0)),
            pl.BlockSpec((1, 128, seq), lambda b, i: (b, 0, 0)),
            pl.BlockSpec((DSA_KB, DSA_KB), lambda b, i: (0, 0)),
        ],
        out_specs=pl.BlockSpec((1, tq, BRANCH_W), lambda b, i: (b, i, 0)),
        out_shape=jax.ShapeDtypeStruct((batch, seq, BRANCH_W), BF16),
        scratch_shapes=[pltpu.VMEM((seq, tq), F32), pltpu.VMEM((seq, tq), F32),
                        pltpu.VMEM((seq, DSA_HEADS * tq), F32)],
        compiler_params=_cparams(("parallel", "parallel")),
        name="dsa_attention",
    )(dq, iq.reshape(batch, seq, 256), iw.reshape(batch, seq, 128), kvi.reshape(batch, seq, 256), vt,
      _strict_tri(DSA_KB, lower=True))


def _memkv_kernel(m_ref, g_ref, w_ref, gk_ref, bd_ref, mk_ref, mv_ref):
    x = m_ref[...]
    ms = jnp.mean(x * x, axis=-1, keepdims=True)
    h = (x * lax.rsqrt(ms + EPS) * g_ref[...]).astype(BF16)
    mw = MEM_HEADS * MEM_HEAD_DIM
    k = _dot(h, w_ref[:, :mw])
    msk = _dot_split(k * k, bd_ref[...]) * (1.0 / MEM_HEAD_DIM)
    mk_ref[...] = (k * lax.rsqrt(msk + EPS) * gk_ref[...]).astype(BF16)
    mv_ref[...] = _dot(h, w_ref[:, mw:]).astype(BF16)


def _mem_kv(mem2d, g_mem, w_mem_kv, g_k_mem, tm):
    n = mem2d.shape[0]
    mw = MEM_HEADS * MEM_HEAD_DIM
    row = lambda i: (i, 0)
    const = lambda i: (0, 0)
    return pl.pallas_call(
        _memkv_kernel,
        grid=(n // tm,),
        in_specs=[pl.BlockSpec((tm, D_MODEL), row), pl.BlockSpec((1, D_MODEL), const),
                  pl.BlockSpec((D_MODEL, 2 * mw), const), pl.BlockSpec((1, mw), const),
                  pl.BlockSpec((mw, mw), const)],
        out_specs=[pl.BlockSpec((tm, mw), row), pl.BlockSpec((tm, mw), row)],
        out_shape=[jax.ShapeDtypeStruct((n, mw), BF16), jax.ShapeDtypeStruct((n, mw), BF16)],
        compiler_params=_cparams(("parallel",)),
        name="mem_kv",
    )(mem2d, g_mem[None, :], w_mem_kv.astype(BF16), jnp.tile(g_k_mem, MEM_HEADS)[None, :],
      _block_diag_ones(mw, MEM_HEAD_DIM))


def _memattn_kernel(q_ref, k_ref, v_ref, o_ref):
    for hd in range(MEM_HEADS):
        sl = slice(hd * MEM_HEAD_DIM, (hd + 1) * MEM_HEAD_DIM)
        s = _dot_nt(q_ref[0, :, sl], k_ref[0, :, sl]) * (MEM_HEAD_DIM ** -0.5)
        p = jnp.exp(s - jnp.max(s, axis=-1, keepdims=True))
        o = _dot(p.astype(BF16), v_ref[0, :, sl]) / jnp.sum(p, axis=-1, keepdims=True)
        o_ref[0, :, sl] = o.astype(BF16)


def _mem_attention(mq, mk, mv, tq):
    batch, seq, mw = mq.shape
    mlen = mk.shape[1]
    kspec = pl.BlockSpec((1, mlen, mw), lambda b, i: (b, 0, 0))
    return pl.pallas_call(
        _memattn_kernel,
        grid=(batch, seq // tq),
        in_specs=[pl.BlockSpec((1, tq, mw), lambda b, i: (b, i, 0)), kspec, kspec],
        out_specs=pl.BlockSpec((1, tq, mw), lambda b, i: (b, i, 0)),
        out_shape=jax.ShapeDtypeStruct((batch, seq, mw), BF16),
        compiler_params=_cparams(("parallel", "parallel")),
        name="mem_attention",
    )(mq, mk, mv)


def _merge_kernel(x_ref, g_ref, osb_ref, odsa_ref, omem_ref, wg_ref, bg_ref, wb_ref, wo_ref,
                  gf_ref, wrh_ref, wrl_ref, br_ref, tri_ref,
                  x2_ref, h2_ref, idx_ref, gate_ref, rank_ref, cnt_ref, carry_ref):
    x = x_ref[...]
    ms = jnp.mean(x * x, axis=-1, keepdims=True)
    h = (x * lax.rsqrt(ms + EPS) * g_ref[...]).astype(BF16)
    merged = None
    for n, o_ref in enumerate((osb_ref, odsa_ref, omem_ref)):
        gate = jax.nn.sigmoid(_dot(h, wg_ref[n]) + bg_ref[n])
        term = gate * _dot(o_ref[...], wb_ref[n])
        merged = term if merged is None else merged + term
    x2 = x + _dot(merged.astype(BF16), wo_ref[...])
    x2_ref[...] = x2

    ms2 = jnp.mean(x2 * x2, axis=-1, keepdims=True)
    h2 = x2 * lax.rsqrt(ms2 + EPS) * gf_ref[...]
    _store_token_tiles(h2_ref, h2)
    hi, lo = _split_bf16(h2)
    logits = _dot(hi, wrh_ref[...]) + _dot(hi, wrl_ref[...]) + _dot(lo, wrh_ref[...]) + br_ref[...]

    lane = lax.broadcasted_iota(I32, logits.shape, 1)
    vals = logits
    top_v, top_i = [], []
    for _ in range(TOP_K):
        mx = jnp.max(vals, axis=-1, keepdims=True)
        ix = jnp.min(jnp.where(vals == mx, lane, N_EXPERTS), axis=-1, keepdims=True)
        top_v.append(mx)
        top_i.append(ix)
        vals = jnp.where(lane == ix, -jnp.inf, vals)
    ex = [jnp.exp(v - top_v[0]) for v in top_v]
    den = ex[0] + ex[1] + ex[2] + ex[3]
    for k in range(TOP_K):
        idx_ref[:, k:k + 1] = top_i[k]
        gate_ref[:, k:k + 1] = ex[k] / den

    @pl.when(pl.program_id(0) == 0)
    def _():
        carry_ref[...] = jnp.zeros_like(carry_ref)

    lane_e = lax.broadcasted_iota(I32, (x.shape[0], LANES), 1)
    hits = [lane_e == top_i[k] for k in range(TOP_K)]
    onehot = sum(jnp.where(hk, 1.0, 0.0) for hk in hits)
    before = _dot(tri_ref[...], onehot.astype(BF16)) + carry_ref[...]
    for k in range(TOP_K):
        rank_ref[:, k:k + 1] = jnp.sum(jnp.where(hits[k], before, 0.0), axis=-1, keepdims=True).astype(I32)
    carry_ref[...] += jnp.sum(onehot, axis=0, keepdims=True)
    cnt_ref[...] = carry_ref[...]


def _merge_route(x2d, g_mix, o_sb, o_dsa, o_mem, w_gate, b_gate, w_branch, w_out, g_ffn, w_router, b_router, tm):
    n = x2d.shape[0]
    row = lambda i: (i, 0)
    c2 = lambda i: (0, 0)
    c3 = lambda i: (0, 0, 0)
    wr_hi = w_router.astype(BF16)
    wr_lo = (w_router - wr_hi.astype(F32)).astype(BF16)
    return pl.pallas_call(
        _merge_kernel,
        grid=(n // tm,),
        in_specs=[pl.BlockSpec((tm, D_MODEL), row), pl.BlockSpec((1, D_MODEL), c2),
                  pl.BlockSpec((tm, BRANCH_W), row), pl.BlockSpec((tm, BRANCH_W), row),
                  pl.BlockSpec((tm, BRANCH_W), row),
                  pl.BlockSpec((3, D_MODEL, D_MODEL), c3, pipeline_mode=pl.Buffered(1)),
                  pl.BlockSpec((3, 1, D_MODEL), c3),
                  pl.BlockSpec((3, BRANCH_W, D_MODEL), c3, pipeline_mode=pl.Buffered(1)),
                  pl.BlockSpec((D_MODEL, D_MODEL), c2, pipeline_mode=pl.Buffered(1)),
                  pl.BlockSpec((1, D_MODEL), c2), pl.BlockSpec((D_MODEL, N_EXPERTS), c2),
                  pl.BlockSpec((D_MODEL, N_EXPERTS), c2), pl.BlockSpec((1, N_EXPERTS), c2),
                  pl.BlockSpec((tm, tm), c2, pipeline_mode=pl.Buffered(1))],
        out_specs=[pl.BlockSpec((tm, D_MODEL), row), pl.BlockSpec((tm * TOKEN_TILE, LANES), row),
                   pl.BlockSpec((tm, TOP_K), row), pl.BlockSpec((tm, TOP_K), row),
                   pl.BlockSpec((tm, TOP_K), row), pl.BlockSpec((1, LANES), c2)],
        out_shape=[jax.ShapeDtypeStruct((n, D_MODEL), F32), jax.ShapeDtypeStruct((n * TOKEN_TILE, LANES), F32),
                   jax.ShapeDtypeStruct((n, TOP_K), I32), jax.ShapeDtypeStruct((n, TOP_K), F32),
                   jax.ShapeDtypeStruct((n, TOP_K), I32), jax.ShapeDtypeStruct((1, LANES), F32)],
        scratch_shapes=[pltpu.VMEM((1, LANES), F32)],
        compiler_params=_cparams(("arbitrary",)),
        name="merge_route",
    )(x2d, g_mix[None, :], o_sb, o_dsa, o_mem, w_gate.astype(BF16), b_gate[:, None, :],
      w_branch.astype(BF16), w_out.astype(BF16), g_ffn[None, :], wr_hi, wr_lo, b_router[None, :],
      _strict_tri(tm, lower=True))


DISPATCH_TOKENS = 256


def _dispatch_kernel(dest_ref, pad_ref, h_ref, xs_ref, zero_ref, sem, *, n_pad):
    n_copies = DISPATCH_TOKENS * TOP_K + n_pad
    zero_ref[...] = jnp.zeros_like(zero_ref)

    def fill(j, c):
        for r in range(2):
            dst = pl.multiple_of(pad_ref[0, 0, 2 * j + r] * TOKEN_TILE, TOKEN_TILE)
            pltpu.make_async_copy(zero_ref, xs_ref.at[pl.ds(dst, TOKEN_TILE)], sem).start(priority=r)
        return c

    lax.fori_loop(0, n_pad // 2, fill, 0, unroll=2)

    def start(t, c):
        src = pl.multiple_of(t * TOKEN_TILE, TOKEN_TILE)
        for k in range(TOP_K):
            dst = pl.multiple_of(dest_ref[0, 0, t * TOP_K + k] * TOKEN_TILE, TOKEN_TILE)
            pltpu.make_async_copy(h_ref.at[pl.ds(src, TOKEN_TILE)], xs_ref.at[pl.ds(dst, TOKEN_TILE)],
                                  sem).start(priority=k % 2)
        return c

    lax.fori_loop(0, DISPATCH_TOKENS, start, 0, unroll=2)
    total = n_copies * TOKEN_TILE
    pltpu.make_async_copy(xs_ref.at[pl.ds(0, total)], xs_ref.at[pl.ds(0, total)], sem).wait()


def _dispatch(dest, pad_slots, h2t, n_slots):
    n = h2t.shape[0] // TOKEN_TILE
    steps = n // DISPATCH_TOKENS
    per = DISPATCH_TOKENS * TOP_K
    n_pad = pad_slots.shape[0] // steps
    assert n_pad * steps == pad_slots.shape[0] and n_pad % 2 == 0
    return pl.pallas_call(
        functools.partial(_dispatch_kernel, n_pad=n_pad),
        grid=(steps,),
        in_specs=[pl.BlockSpec((1, 1, per), lambda i: (i, 0, 0), memory_space=pltpu.SMEM),
                  pl.BlockSpec((1, 1, n_pad), lambda i: (i, 0, 0), memory_space=pltpu.SMEM),
                  pl.BlockSpec((DISPATCH_TOKENS * TOKEN_TILE, LANES), lambda i: (i, 0))],
        out_specs=pl.BlockSpec(memory_space=pl.ANY),
        out_shape=jax.ShapeDtypeStruct((n_slots * TOKEN_TILE, LANES), F32),
        scratch_shapes=[pltpu.VMEM((TOKEN_TILE, LANES), F32), pltpu.SemaphoreType.DMA(())],
        compiler_params=_cparams(("arbitrary",)),
        name="moe_dispatch",
    )(dest.reshape(steps, 1, per), pad_slots.reshape(steps, 1, n_pad), h2t)


def _expert_kernel(blk_e_ref, nused_ref, x_ref, w1_ref, b1_ref, w2_ref, b2_ref, y_ref,
                   xb_ref, act_ref, w1b_ref, w2b_ref):
    i = pl.program_id(0)
    new_expert = jnp.logical_or(i == 0, blk_e_ref[i] != blk_e_ref[jnp.maximum(i - 1, 0)])

    @pl.when(jnp.logical_and(i < nused_ref[0], new_expert))
    def _():
        w1b_ref[...] = w1_ref[0].astype(BF16)
        w2b_ref[...] = w2_ref[0].astype(BF16)

    @pl.when(i < nused_ref[0])
    def _():
        for s in range(TOKEN_TILE):
            xb_ref[:, s * LANES:(s + 1) * LANES] = _load_token_tiles(x_ref, 0, MOE_BLOCK, s).astype(BF16)
        xb = xb_ref[...]
        for c0 in range(0, D_EXPERT, COL_BLOCK):
            cg = slice(c0, c0 + COL_BLOCK)
            cu = slice(D_EXPERT + c0, D_EXPERT + c0 + COL_BLOCK)
            g = jnp.minimum(_dot(xb, w1b_ref[:, cg]) + b1_ref[0, :, cg], SWIGLU_LIMIT)
            u = jnp.clip(_dot(xb, w1b_ref[:, cu]) + b1_ref[0, :, cu], -SWIGLU_LIMIT, SWIGLU_LIMIT)
            act_ref[:, cg] = ((u + 1.0) * (g * jax.nn.sigmoid(SWIGLU_ALPHA * g))).astype(BF16)
        act = act_ref[...]
        for c0 in range(0, D_MODEL, COL_BLOCK):
            cols = slice(c0, c0 + COL_BLOCK)
            _store_token_tiles(y_ref, _dot(act, w2b_ref[:, cols]) + b2_ref[0, :, cols], first_chunk=c0 // LANES)

    @pl.when(pl.program_id(0) >= nused_ref[0])
    def _():
        y_ref[...] = jnp.zeros_like(y_ref)


def _experts(blk_e, n_used, xs, w_e_in, b_e_in, w_e_out, b_e_out):
    n_slots = xs.shape[0] // TOKEN_TILE
    nblk = n_slots // MOE_BLOCK
    slot_block = pl.BlockSpec((MOE_BLOCK * TOKEN_TILE, LANES), lambda i, be, nu: (i, 0))
    grid_spec = pltpu.PrefetchScalarGridSpec(
        num_scalar_prefetch=2,
        grid=(nblk,),
        in_specs=[slot_block,
                  pl.BlockSpec((1, D_MODEL, 2 * D_EXPERT), lambda i, be, nu: (be[i], 0, 0)),
                  pl.BlockSpec((1, 1, 2 * D_EXPERT), lambda i, be, nu: (be[i], 0, 0)),
                  pl.BlockSpec((1, D_EXPERT, D_MODEL), lambda i, be, nu: (be[i], 0, 0)),
                  pl.BlockSpec((1, 1, D_MODEL), lambda i, be, nu: (be[i], 0, 0))],
        out_specs=slot_block,
        scratch_shapes=[pltpu.VMEM((MOE_BLOCK, D_MODEL), BF16), pltpu.VMEM((MOE_BLOCK, D_EXPERT), BF16),
                        pltpu.VMEM((D_MODEL, 2 * D_EXPERT), BF16),
                        pltpu.VMEM((D_EXPERT, D_MODEL), BF16)],
    )
    return pl.pallas_call(
        _expert_kernel,
        grid_spec=grid_spec,
        out_shape=jax.ShapeDtypeStruct((n_slots * TOKEN_TILE, LANES), F32),
        compiler_params=_cparams(("arbitrary",)),
        name="moe_experts",
    )(blk_e, n_used, xs, w_e_in, b_e_in[:, None, :], w_e_out, b_e_out[:, None, :])


def _combine_kernel(dest_ref, dest_next_ref, x_ref, gate_ref, y_ref, o_ref, buf_ref, sem):
    n_copies = DISPATCH_TOKENS * TOP_K
    i = pl.program_id(0)
    slot = i % 2

    def start_gathers(idx_ref, buf, dma_sem):
        def start(t, c):
            for k in range(TOP_K):
                src = pl.multiple_of(idx_ref[0, 0, t * TOP_K + k] * TOKEN_TILE, TOKEN_TILE)
                dst = pl.multiple_of((k * DISPATCH_TOKENS + t) * TOKEN_TILE, TOKEN_TILE)
                pltpu.make_async_copy(y_ref.at[pl.ds(src, TOKEN_TILE)], buf.at[pl.ds(dst, TOKEN_TILE)],
                                      dma_sem).start(priority=k % 2)
            return c
        lax.fori_loop(0, DISPATCH_TOKENS, start, 0, unroll=2)

    @pl.when(i == 0)
    def _():
        start_gathers(dest_ref, buf_ref.at[0], sem.at[0])

    @pl.when(i + 1 < pl.num_programs(0))
    def _():
        start_gathers(dest_next_ref, buf_ref.at[1 - slot], sem.at[1 - slot])

    buf = buf_ref.at[slot]
    pltpu.make_async_copy(y_ref.at[pl.ds(0, n_copies * TOKEN_TILE)], buf, sem.at[slot]).wait()
    rows = 32
    for r0 in range(0, DISPATCH_TOKENS, rows):
        gate = gate_ref[r0:r0 + rows, :]
        gates = [jnp.broadcast_to(gate[:, k:k + 1], (rows, LANES)) for k in range(TOP_K)]
        for s in range(TOKEN_TILE):
            out = x_ref[r0:r0 + rows, s * LANES:(s + 1) * LANES]
            for k in range(TOP_K):
                out = out + gates[k] * _load_token_tiles(buf, k * DISPATCH_TOKENS + r0, rows, s)
            o_ref[r0:r0 + rows, s * LANES:(s + 1) * LANES] = out


def _combine(dest, x2, gate, ys):
    n = x2.shape[0]
    steps = n // DISPATCH_TOKENS
    per = DISPATCH_TOKENS * TOP_K
    dest3 = dest.reshape(steps, 1, per)
    return pl.pallas_call(
        _combine_kernel,
        grid=(steps,),
        in_specs=[pl.BlockSpec((1, 1, per), lambda i: (i, 0, 0), memory_space=pltpu.SMEM),
                  pl.BlockSpec((1, 1, per), lambda i: (jnp.minimum(i + 1, steps - 1), 0, 0), memory_space=pltpu.SMEM),
                  pl.BlockSpec((DISPATCH_TOKENS, D_MODEL), lambda i: (i, 0)),
                  pl.BlockSpec((DISPATCH_TOKENS, TOP_K), lambda i: (i, 0)),
                  pl.BlockSpec(memory_space=pl.ANY)],
        out_specs=pl.BlockSpec((DISPATCH_TOKENS, D_MODEL), lambda i: (i, 0)),
        out_shape=jax.ShapeDtypeStruct((n, D_MODEL), F32),
        scratch_shapes=[pltpu.VMEM((2, TOP_K * DISPATCH_TOKENS * TOKEN_TILE, LANES), F32),
                        pltpu.SemaphoreType.DMA((2,))],
        compiler_params=_cparams(("arbitrary",)),
        name="moe_combine",
    )(dest3, dest3, x2, gate, ys)


def _moe(x2, h2, top_idx, gate, rank, counts, w_e_in, b_e_in, w_e_out, b_e_out):
    n = x2.shape[0]
    counts = counts[0, :N_EXPERTS].astype(I32)
    padded = (counts + MOE_BLOCK - 1) // MOE_BLOCK * MOE_BLOCK
    pend = jnp.cumsum(padded)
    pstart = pend - padded
    nblk = -(-(n * TOP_K) // MOE_BLOCK) + N_EXPERTS
    blk_start = jnp.arange(nblk, dtype=I32) * MOE_BLOCK
    blk_e = jnp.minimum(jnp.sum((pend[None, :] <= blk_start[:, None]).astype(I32), axis=1), N_EXPERTS - 1)
    n_used = (pend[-1:] // MOE_BLOCK).astype(I32)
    onehot = top_idx[:, :, None] == jnp.arange(N_EXPERTS, dtype=I32)[None, None, :]
    dest = rank + jnp.sum(jnp.where(onehot, pstart[None, None, :], 0), axis=-1)
    n_slots = nblk * MOE_BLOCK
    pad_len = padded - counts
    pad_end = jnp.cumsum(pad_len)
    base = jnp.concatenate([pstart + counts - (pad_end - pad_len), pend[-1:] - pad_end[-1:]])
    j = jnp.arange(n_slots - n * TOP_K, dtype=I32)
    group = jnp.sum((pad_end[None, :] <= j[:, None]).astype(I32), axis=1)
    group_hot = group[:, None] == jnp.arange(N_EXPERTS + 1, dtype=I32)[None, :]
    pad_slots = j + jnp.sum(jnp.where(group_hot, base[None, :], 0), axis=1)
    xs = _dispatch(dest, pad_slots, h2, n_slots)
    ys = _experts(blk_e, n_used, xs, w_e_in, b_e_in, w_e_out, b_e_out)
    return _combine(dest, x2, gate, ys)


def _layer(x, mem, g_mix, w_in, g_q_dsa, g_k_dsa, g_q_mem, g_k_mem, g_mem, w_mem_kv, w_gate, b_gate,
           w_branch, w_out, g_ffn, w_router, b_router, w_e_in, b_e_in, w_e_out, b_e_out):
    batch, seq, _ = x.shape
    n = batch * seq
    topk = min(DSA_TOPK_MAX, seq // 4)
    x2d = x.reshape(n, D_MODEL)
    tm_dense = min(DENSE_ROWS, seq)
    sq, sk, sv, dq, iq, kvi, iw, mq, vt = _inproj(x2d, batch, seq, g_mix, w_in, g_q_dsa, g_k_dsa, g_q_mem, tm_dense)
    o_sb = _sb_attention(sq, sk, sv, min(256, seq))
    o_dsa = _dsa_attention(dq, iq, iw, kvi, vt, 128, topk)
    mlen = mem.shape[1]
    mk, mv = _mem_kv(mem.reshape(batch * mlen, D_MODEL), g_mem, w_mem_kv, g_k_mem, min(512, batch * mlen))
    mw = MEM_HEADS * MEM_HEAD_DIM
    o_mem = _mem_attention(mq.reshape(batch, seq, mw), mk.reshape(batch, mlen, mw), mv.reshape(batch, mlen, mw),
                           tm_dense)
    x2, h2, top_idx, gate, rank, counts = _merge_route(
        x2d, g_mix, o_sb.reshape(n, BRANCH_W), o_dsa.reshape(n, BRANCH_W), o_mem.reshape(n, mw),
        w_gate, b_gate, w_branch, w_out, g_ffn, w_router, b_router, tm_dense)
    out = _moe(x2, h2, top_idx, gate, rank, counts, w_e_in, b_e_in, w_e_out, b_e_out)
    return out.reshape(batch, seq, D_MODEL)


def kernel(x, mem, g_mix, w_in, g_q_dsa, g_k_dsa, g_q_mem, g_k_mem, g_mem, w_mem_kv, w_gate, b_gate, w_branch, w_out, g_ffn, w_router, b_router, w_e_in, b_e_in, w_e_out, b_e_out):
    for l in range(g_mix.shape[0]):
        x = _layer(x, mem, g_mix[l], w_in[l], g_q_dsa[l], g_k_dsa[l], g_q_mem[l], g_k_mem[l], g_mem[l],
                   w_mem_kv[l], w_gate[l], b_gate[l], w_branch[l], w_out[l], g_ffn[l], w_router[l],
                   b_router[l], w_e_in[l], b_e_in[l], w_e_out[l], b_e_out[l])
    return x
```

```python
import functools

import numpy as np
import jax
import jax.numpy as jnp
from jax import lax
from jax.experimental import pallas as pl
from jax.experimental.pallas import tpu as pltpu

F32 = jnp.float32
BF16 = jnp.bfloat16
I32 = jnp.int32

D_MODEL = 1024
CHUNK = 64
SB_HEADS = 8
DSA_HEADS = 8
HEAD_DIM = 64
IDX_HEADS = 4
DSA_TOPK_MAX = 256
MEM_HEADS = 4
MEM_HEAD_DIM = 128
N_EXPERTS = 32
TOP_K = 4
D_EXPERT = D_MODEL
SWIGLU_LIMIT = 7.0
SWIGLU_ALPHA = 1.702
ROPE_THETA = 10000.0
EPS = 1e-6
MOE_BLOCK = 512

BRANCH_W = 512
IN_SIZES = (512, 512, 512, 512, 64, 64, 256, 64, 4, 512)
C_SQ, C_SK, C_SV, C_DQ, C_MQ, C_IQ, C_SMALL, C_END = 0, 512, 1024, 1536, 2048, 2560, 2816, 3072
IW_LANE = 64

LANES = 128
NEG_BIG = -1e30
SB_CUTOFF = 110.0
KEY_NEG_INF = int(np.array(-np.inf, np.float32).view(np.int32)) ^ 0x7FFFFFFF
INT_MIN = -(2 ** 31)

VMEM_LIMIT = 56 * 1024 * 1024
DENSE_ROWS = 1024


def _cparams(sem):
    return pltpu.CompilerParams(dimension_semantics=sem, vmem_limit_bytes=VMEM_LIMIT)


def _dot(a, b):
    return jnp.dot(a, b, preferred_element_type=F32)


def _dot_nt(a, b):
    return lax.dot_general(a, b, (((1,), (1,)), ((), ())), preferred_element_type=F32)


def _split_bf16(x):
    hi = x.astype(BF16)
    lo = (x - hi.astype(F32)).astype(BF16)
    return hi, lo


def _dot_split(x, m_bf16):
    hi, lo = _split_bf16(x)
    return _dot(hi, m_bf16) + _dot(lo, m_bf16)


TOKEN_TILE = D_MODEL // LANES


def _store_token_tiles(ref, y):
    rows = y.shape[0]
    for s in range(TOKEN_TILE):
        ref[pl.ds(s, rows, stride=TOKEN_TILE), :] = y[:, s * LANES:(s + 1) * LANES]


def _load_token_tiles(ref, start_row, rows, s):
    return ref[pl.ds(start_row * TOKEN_TILE + s, rows, stride=TOKEN_TILE), :]


def _rot_half_unsigned(y):
    w = y.shape[1]
    lane = lax.broadcasted_iota(I32, y.shape, 1)
    return jnp.where((lane & 32) == 0, pltpu.roll(y, w - 32, 1), pltpu.roll(y, 32, 1))


def _inproj_kernel(x_ref, g_ref, w_ref, wsvt_ref, cos_ref, sin_ref, coss_ref, sins_ref, gq_ref, gks_ref,
                   gm_ref, bd64_ref, bd128_ref,
                   sq_ref, sk_ref, sv_ref, dq_ref, iq_ref, kvi_ref, iw_ref, mq_ref, vt_ref):
    x = x_ref[...]
    ms = jnp.mean(x * x, axis=-1, keepdims=True)
    h = (x * lax.rsqrt(ms + EPS) * g_ref[...]).astype(BF16)

    def seg(a, b):
        return _dot(h, w_ref[:, a:b])

    def put_heads(ref, y):
        for hd in range(y.shape[1] // HEAD_DIM):
            ref[0, hd] = y[:, hd * HEAD_DIM:(hd + 1) * HEAD_DIM].astype(BF16)

    put_heads(sq_ref, seg(C_SQ, C_SK) * (HEAD_DIM ** -0.5))
    put_heads(sk_ref, seg(C_SK, C_SV))
    sv_ref[0] = _dot_nt(wsvt_ref[...], h).astype(BF16)

    y = seg(C_DQ, C_MQ)
    msq = _dot_split(y * y, bd64_ref[...]) * (1.0 / HEAD_DIM)
    y = y * lax.rsqrt(msq + EPS) * gq_ref[...]
    y = y * cos_ref[...] + _rot_half_unsigned(y) * sin_ref[...]
    put_heads(dq_ref, y * (HEAD_DIM ** -0.5))

    y = seg(C_IQ, C_SMALL)
    y = y * cos_ref[:, :256] + _rot_half_unsigned(y) * sin_ref[:, :256]
    iq_ref[...] = (y * (HEAD_DIM ** -0.5)).astype(BF16)

    y = seg(C_SMALL, C_END)
    lane = lax.broadcasted_iota(I32, y.shape, 1)
    is_k = lane < HEAD_DIM
    msk = jnp.sum(jnp.where(is_k, y * y, 0.0), axis=-1, keepdims=True) * (1.0 / HEAD_DIM)
    y = y * jnp.where(is_k, lax.rsqrt(msk + EPS) * gks_ref[...], 1.0)
    y = y * coss_ref[...] + _rot_half_unsigned(y) * sins_ref[...]
    kvi_ref[...] = y.astype(BF16)
    iw_ref[...] = y[:, 128:256]
    kv_t = y[:, 0:128].T
    row_t = lax.broadcasted_iota(I32, kv_t.shape, 0)
    vt_ref[0] = jnp.where(row_t < HEAD_DIM, 1.0, kv_t).astype(BF16)

    y = seg(C_MQ, C_IQ)
    msm = _dot_split(y * y, bd128_ref[...]) * (1.0 / MEM_HEAD_DIM)
    mq_ref[...] = (y * lax.rsqrt(msm + EPS) * gm_ref[...]).astype(BF16)


def _rope_tables(seq):
    half = HEAD_DIM // 2
    inv = ROPE_THETA ** (-jnp.arange(half, dtype=F32) / half)
    ang = jnp.arange(seq).astype(F32)[:, None] * inv[None, :]
    cos = jnp.cos(ang)
    sin = jnp.sin(ang)
    cos64 = jnp.concatenate([cos, cos], axis=1)
    sin64 = jnp.concatenate([-sin, sin], axis=1)
    one = jnp.ones_like(cos64)
    zero = jnp.zeros_like(cos64)
    cosq = jnp.tile(cos64, (1, 8))
    sinq = jnp.tile(sin64, (1, 8))
    coss = jnp.concatenate([cos64, one, cos64, one], axis=1)
    sins = jnp.concatenate([sin64, zero, sin64, zero], axis=1)
    return cosq, sinq, coss, sins


def _block_diag_ones(width, group):
    idx = np.arange(width) // group
    return jnp.asarray((idx[:, None] == idx[None, :]).astype(np.float32), dtype=BF16)


def _inproj(x2d, batch, seq, g_mix, w_in, g_q_dsa, g_k_dsa, g_q_mem, tm):
    n = x2d.shape[0]
    sizes = np.cumsum((0,) + IN_SIZES)
    col = {name: (int(sizes[i]), int(sizes[i + 1])) for i, name in enumerate(
        ("sq", "sk", "sv", "dq", "dk", "dv", "iq", "ik", "iw", "mq"))}
    order = ("sq", "sk", "sv", "dq", "mq", "iq", "dk", "dv", "ik", "iw")
    w = jnp.concatenate([w_in[:, col[k][0]:col[k][1]] for k in order]
                        + [jnp.zeros((D_MODEL, C_END - sum(IN_SIZES)), w_in.dtype)], axis=1).astype(BF16)
    cosq, sinq, coss, sins = (jnp.asarray(t) for t in _rope_tables(seq))
    gq = jnp.tile(g_q_dsa, 8)[None, :]
    gks = jnp.concatenate([g_k_dsa, jnp.ones((256 - HEAD_DIM,), F32)])[None, :]
    gm = jnp.tile(g_q_mem, MEM_HEADS)[None, :]
    spb = seq // tm
    row = lambda i: (i, 0)
    const = lambda i: (0, 0)
    pos = lambda i: (i % spb, 0)
    heads = lambda i: (i // spb, 0, i % spb, 0)
    head_shape = jax.ShapeDtypeStruct((batch, 8, seq, HEAD_DIM), BF16)
    head_spec = pl.BlockSpec((1, 8, tm, HEAD_DIM), heads)
    return pl.pallas_call(
        _inproj_kernel,
        grid=(n // tm,),
        in_specs=[
            pl.BlockSpec((tm, D_MODEL), row),
            pl.BlockSpec((1, D_MODEL), const),
            pl.BlockSpec((D_MODEL, C_END), const, pipeline_mode=pl.Buffered(1)),
            pl.BlockSpec((512, D_MODEL), const, pipeline_mode=pl.Buffered(1)),
            pl.BlockSpec((tm, 512), pos), pl.BlockSpec((tm, 512), pos),
            pl.BlockSpec((tm, 256), pos), pl.BlockSpec((tm, 256), pos),
            pl.BlockSpec((1, 512), const), pl.BlockSpec((1, 256), const), pl.BlockSpec((1, 512), const),
            pl.BlockSpec((512, 512), const), pl.BlockSpec((512, 512), const),
        ],
        out_specs=[head_spec, head_spec, pl.BlockSpec((1, 512, tm), lambda i: (i // spb, 0, i % spb)), head_spec,
                   pl.BlockSpec((tm, 256), row), pl.BlockSpec((tm, 256), row),
                   pl.BlockSpec((tm, 128), row), pl.BlockSpec((tm, 512), row),
                   pl.BlockSpec((1, 128, tm), lambda i: (i // spb, 0, i % spb))],
        out_shape=[head_shape, head_shape, jax.ShapeDtypeStruct((batch, 512, seq), BF16), head_shape,
                   jax.ShapeDtypeStruct((n, 256), BF16), jax.ShapeDtypeStruct((n, 256), BF16),
                   jax.ShapeDtypeStruct((n, 128), F32), jax.ShapeDtypeStruct((n, 512), BF16),
                   jax.ShapeDtypeStruct((batch, 128, seq), BF16)],
        compiler_params=_cparams(("parallel",)),
        name="inproj",
    )(x2d, g_mix[None, :], w, w_in[:, col["sv"][0]:col["sv"][1]].T.astype(BF16), cosq, sinq, coss, sins, gq, gks, gm,
      _block_diag_ones(512, HEAD_DIM), _block_diag_ones(512, MEM_HEAD_DIM))


def _sb_kernel(q_ref, k_ref, vt_ref, u_ref, o_ref, acc_ref, car_ref, *, tq):
    qi = pl.program_id(1)
    rows = lax.broadcasted_iota(I32, (tq, tq), 0)
    cols = lax.broadcasted_iota(I32, (tq, tq), 1)
    dif = rows - cols
    u = u_ref[...]
    acc_ref[...] = jnp.zeros_like(acc_ref)
    car_ref[...] = jnp.zeros_like(car_ref)

    def cond(c):
        kb, mx = c
        return jnp.logical_and(kb >= 0, mx > -SB_CUTOFF)

    def body(c):
        kb, _ = c
        ks = pl.multiple_of(kb * tq, tq)
        earlier = dif < (qi - kb) * tq
        neg_mask = jnp.where(earlier, -1.0, 0.0).astype(BF16)
        heads = range(SB_HEADS)
        z = [_dot_nt(k_ref[0, hd, pl.ds(ks, tq), :], q_ref[0, hd]) for hd in heads]
        ls, lk, between = [], [], []
        for hd in heads:
            zb = z[hd].astype(BF16)
            sp = jnp.maximum(zb, 0.0) + jnp.log(1.0 + jnp.exp(-jnp.abs(zb)))
            ls.append(z[hd] - sp.astype(F32))
            lk.append(sp * neg_mask)
            between.append(_dot(u, lk[hd]))
        for hd in heads:
            rs = slice(hd * HEAD_DIM, (hd + 1) * HEAD_DIM)
            car = car_ref[hd:hd + 1, :]
            w = jnp.where(earlier, jnp.exp(ls[hd] + between[hd] + car), 0.0)
            acc_ref[rs, :] += _dot(vt_ref[0, rs, pl.ds(ks, tq)], w.astype(BF16))
            car_ref[hd:hd + 1, :] = car + (between[hd][0:1, :] + lk[hd][0:1, :].astype(F32))
        return kb - 1, jnp.max(car_ref[...])

    lax.while_loop(cond, body, (qi, jnp.float32(0.0)))
    o_ref[0] = acc_ref[...].T.astype(BF16)


def _strict_tri(n, lower):
    i = np.arange(n)
    m = (i[:, None] > i[None, :]) if lower else (i[:, None] < i[None, :])
    return jnp.asarray(m.astype(np.float32), dtype=BF16)


def _sb_attention(sq, sk, svt, tq):
    batch, _, seq, _ = sq.shape
    return pl.pallas_call(
        functools.partial(_sb_kernel, tq=tq),
        grid=(batch, seq // tq),
        in_specs=[pl.BlockSpec((1, SB_HEADS, tq, HEAD_DIM), lambda b, i: (b, 0, i, 0)),
                  pl.BlockSpec((1, SB_HEADS, seq, HEAD_DIM), lambda b, i: (b, 0, 0, 0)),
                  pl.BlockSpec((1, BRANCH_W, seq), lambda b, i: (b, 0, 0)),
                  pl.BlockSpec((tq, tq), lambda b, i: (0, 0))],
        out_specs=pl.BlockSpec((1, tq, BRANCH_W), lambda b, i: (b, i, 0)),
        out_shape=jax.ShapeDtypeStruct((batch, seq, BRANCH_W), BF16),
        scratch_shapes=[pltpu.VMEM((BRANCH_W, tq), F32), pltpu.VMEM((SB_HEADS, tq), F32)],
        compiler_params=_cparams(("parallel", "parallel")),
        name="sb_attention",
    )(sq, sk, svt, _strict_tri(tq, lower=False))


DSA_SEG = 256
DSA_KB = DSA_SEG


def _tree_sum(parts):
    while len(parts) > 1:
        parts = [parts[i] + parts[i + 1] for i in range(0, len(parts) - 1, 2)] + ([parts[-1]] if len(parts) % 2 else [])
    return parts[0]


def _dsa_kernel(dq_ref, iq_ref, iw_ref, kvi_ref, vt_ref, tri_ref, o_ref,
                sc_ref, bias_ref, s_ref, *, tq, topk, nseg_max):
    nseg = (pl.program_id(1) * tq) // DSA_SEG + 1
    for ns in range(1, nseg_max + 1):
        @pl.when(nseg == ns)
        def _(ns=ns):
            _dsa_body(dq_ref, iq_ref, iw_ref, kvi_ref, vt_ref, tri_ref, o_ref, sc_ref, bias_ref, s_ref,
                      tq=tq, topk=topk, nseg=ns)


def _dsa_body(dq_ref, iq_ref, iw_ref, kvi_ref, vt_ref, tri_ref, o_ref, sc_ref, bias_ref, s_ref, *, tq, topk, nseg):
    qs = pl.program_id(1) * tq
    blocks = [slice(c * DSA_KB, (c + 1) * DSA_KB) for c in range(nseg)]
    iq = iq_ref[0]
    w_t = iw_ref[0].T
    w_row = [w_t[IW_LANE + h:IW_LANE + h + 1, :] * (IDX_HEADS ** -0.5) for h in range(IDX_HEADS)]
    q_chunk = (qs + lax.broadcasted_iota(I32, (DSA_KB, tq), 1)) // CHUNK
    k_chunk = lax.broadcasted_iota(I32, (DSA_KB, tq), 0) // CHUNK

    for c, blk in enumerate(blocks):
        ik = kvi_ref[0, blk, 128:192]
        lg = [_dot_nt(ik, iq[:, h * HEAD_DIM:(h + 1) * HEAD_DIM]) for h in range(IDX_HEADS)]
        sc = jnp.zeros((DSA_KB, tq), F32)
        for h in range(IDX_HEADS):
            sc = sc + w_row[h] * jnp.maximum(lg[h], 0.0)
        admissible = (c * (DSA_KB // CHUNK) + k_chunk) <= q_chunk
        sc_ref[blk, :] = jnp.where(admissible, sc, -jnp.inf)

    def count(pred_fn):
        sub, lanes_of_sums = 32, 4
        acc = [jnp.zeros((sub, tq), F32)] * lanes_of_sums
        for j in range(nseg * DSA_SEG // sub):
            acc[j % lanes_of_sums] = acc[j % lanes_of_sums] + jnp.where(
                pred_fn(sc_ref[j * sub:(j + 1) * sub, :]), 1.0, 0.0)
        return jnp.sum(_tree_sum(acc), axis=0, keepdims=True)

    def key_to_float(key):
        return lax.bitcast_convert_type(jnp.where(key >= 0, key, key ^ 0x7FFFFFFF), F32)

    kf = jnp.float32(topk)
    n_rows = jnp.float32(nseg * DSA_SEG)
    cnt0 = count(lambda s: s >= 0.0)
    t0 = jnp.where(cnt0 >= kf, 0, INT_MIN).astype(I32)
    cnt_t0 = jnp.where(cnt0 >= kf, cnt0, n_rows)

    def bit_step(i, carry):
        t, cnt_t = carry
        cand = t + lax.shift_left(jnp.int32(1), 30 - i)
        cand_f = key_to_float(cand)
        cnt = jnp.where(cand <= KEY_NEG_INF, n_rows, count(lambda s: s >= cand_f))
        take = cnt >= kf
        return jnp.where(take, cand, t), jnp.where(take, cnt, cnt_t)

    thr_key, cnt_thr = lax.fori_loop(0, 31, bit_step, (t0, cnt_t0))
    thr = jnp.where(thr_key <= KEY_NEG_INF, -jnp.inf, key_to_float(thr_key))
    tri = tri_ref[...]

    surplus = jnp.logical_or(jnp.max(cnt_thr) > kf, jnp.min(thr) == -jnp.inf)

    @pl.when(surplus)
    def _():
        need = kf - count(lambda s: s > thr)
        prefix = jnp.zeros((1, tq), F32)
        for blk in blocks:
            sc = sc_ref[blk, :]
            eqf = jnp.where(sc == thr, 1.0, 0.0)
            rank = _dot(tri, eqf.astype(BF16)) + prefix
            tie = jnp.where(rank < need, eqf, 0.0)
            sel = jnp.where(sc > thr, 1.0, tie)
            bias_ref[blk, :] = jnp.where(sc > -jnp.inf, (sel - 1.0) * (-NEG_BIG), NEG_BIG)
            prefix = prefix + jnp.sum(eqf, axis=0, keepdims=True)

    @pl.when(jnp.logical_not(surplus))
    def _():
        for blk in blocks:
            bias_ref[blk, :] = jnp.where(sc_ref[blk, :] >= thr, 0.0, NEG_BIG)

    q8 = dq_ref[0].reshape(DSA_HEADS * tq, HEAD_DIM)
    m = jnp.full((1, DSA_HEADS * tq), NEG_BIG, F32)
    for blk in blocks:
        b = bias_ref[blk, :]
        s = _dot_nt(kvi_ref[0, blk, 0:HEAD_DIM], q8) + jnp.concatenate([b] * DSA_HEADS, axis=1)
        s_ref[blk, :] = s
        m = jnp.maximum(m, jnp.max(s, axis=0, keepdims=True))
    acc = jnp.zeros((128, DSA_HEADS * tq), F32)
    for blk in blocks:
        p = jnp.exp(s_ref[blk, :] - m)
        acc = acc + _dot(vt_ref[0, :, blk], p.astype(BF16))
    for hd in range(DSA_HEADS):
        a = acc[:, hd * tq:(hd + 1) * tq]
        o = (a / a[0:1, :]).T
        o_ref[0, :, hd * HEAD_DIM:(hd + 1) * HEAD_DIM] = o[:, HEAD_DIM:].astype(BF16)


def _dsa_attention(dq, iq, iw, kvi, vt, tq, topk):
    batch, _, seq, _ = dq.shape
    return pl.pallas_call(
        functools.partial(_dsa_kernel, tq=tq, topk=topk, nseg_max=seq // DSA_SEG),
        grid=(batch, seq // tq),
        in_specs=[
            pl.BlockSpec((1, DSA_HEADS, tq, HEAD_DIM), lambda b, i: (b, 0, i, 0)),
            pl.BlockSpec((1, tq, 256), lambda b, i: (b, i, 0)),
            pl.BlockSpec((1, tq, 128), lambda b, i: (b, i, 0)),
            pl.BlockSpec((1, seq, 256), lambda b, i: (b, 0, 0)),
            pl.BlockSpec((1, 128, seq), lambda b, i: (b, 0, 0)),
            pl.BlockSpec((DSA_KB, DSA_KB), lambda b, i: (0, 0)),
        ],
        out_specs=pl.BlockSpec((1, tq, BRANCH_W), lambda b, i: (b, i, 0)),
        out_shape=jax.ShapeDtypeStruct((batch, seq, BRANCH_W), BF16),
        scratch_shapes=[pltpu.VMEM((seq, tq), F32), pltpu.VMEM((seq, tq), F32),
                        pltpu.VMEM((seq, DSA_HEADS * tq), F32)],
        compiler_params=_cparams(("parallel", "parallel")),
        name="dsa_attention",
    )(dq, iq.reshape(batch, seq, 256), iw.reshape(batch, seq, 128), kvi.reshape(batch, seq, 256), vt,
      _strict_tri(DSA_KB, lower=True))


def _memkv_kernel(m_ref, g_ref, w_ref, gk_ref, bd_ref, mk_ref, mv_ref):
    x = m_ref[...]
    ms = jnp.mean(x * x, axis=-1, keepdims=True)
    h = (x * lax.rsqrt(ms + EPS) * g_ref[...]).astype(BF16)
    mw = MEM_HEADS * MEM_HEAD_DIM
    k = _dot(h, w_ref[:, :mw])
    msk = _dot_split(k * k, bd_ref[...]) * (1.0 / MEM_HEAD_DIM)
    mk_ref[...] = (k * lax.rsqrt(msk + EPS) * gk_ref[...]).astype(BF16)
    mv_ref[...] = _dot(h, w_ref[:, mw:]).astype(BF16)


def _mem_kv(mem2d, g_mem, w_mem_kv, g_k_mem, tm):
    n = mem2d.shape[0]
    mw = MEM_HEADS * MEM_HEAD_DIM
    row = lambda i: (i, 0)
    const = lambda i: (0, 0)
    return pl.pallas_call(
        _memkv_kernel,
        grid=(n // tm,),
        in_specs=[pl.BlockSpec((tm, D_MODEL), row), pl.BlockSpec((1, D_MODEL), const),
                  pl.BlockSpec((D_MODEL, 2 * mw), const), pl.BlockSpec((1, mw), const),
                  pl.BlockSpec((mw, mw), const)],
        out_specs=[pl.BlockSpec((tm, mw), row), pl.BlockSpec((tm, mw), row)],
        out_shape=[jax.ShapeDtypeStruct((n, mw), BF16), jax.ShapeDtypeStruct((n, mw), BF16)],
        compiler_params=_cparams(("parallel",)),
        name="mem_kv",
    )(mem2d, g_mem[None, :], w_mem_kv.astype(BF16), jnp.tile(g_k_mem, MEM_HEADS)[None, :],
      _block_diag_ones(mw, MEM_HEAD_DIM))


def _memattn_kernel(q_ref, k_ref, v_ref, o_ref):
    for hd in range(MEM_HEADS):
        sl = slice(hd * MEM_HEAD_DIM, (hd + 1) * MEM_HEAD_DIM)
        s = _dot_nt(q_ref[0, :, sl], k_ref[0, :, sl]) * (MEM_HEAD_DIM ** -0.5)
        p = jnp.exp(s - jnp.max(s, axis=-1, keepdims=True))
        o = _dot(p.astype(BF16), v_ref[0, :, sl]) / jnp.sum(p, axis=-1, keepdims=True)
        o_ref[0, :, sl] = o.astype(BF16)


def _mem_attention(mq, mk, mv, tq):
    batch, seq, mw = mq.shape
    mlen = mk.shape[1]
    kspec = pl.BlockSpec((1, mlen, mw), lambda b, i: (b, 0, 0))
    return pl.pallas_call(
        _memattn_kernel,
        grid=(batch, seq // tq),
        in_specs=[pl.BlockSpec((1, tq, mw), lambda b, i: (b, i, 0)), kspec, kspec],
        out_specs=pl.BlockSpec((1, tq, mw), lambda b, i: (b, i, 0)),
        out_shape=jax.ShapeDtypeStruct((batch, seq, mw), BF16),
        compiler_params=_cparams(("parallel", "parallel")),
        name="mem_attention",
    )(mq, mk, mv)


def _merge_kernel(x_ref, g_ref, osb_ref, odsa_ref, omem_ref, wg_ref, bg_ref, wb_ref, wo_ref,
                  gf_ref, wrh_ref, wrl_ref, br_ref, tri_ref,
                  x2_ref, h2_ref, idx_ref, gate_ref, rank_ref, cnt_ref, carry_ref):
    x = x_ref[...]
    ms = jnp.mean(x * x, axis=-1, keepdims=True)
    h = (x * lax.rsqrt(ms + EPS) * g_ref[...]).astype(BF16)
    merged = None
    for n, o_ref in enumerate((osb_ref, odsa_ref, omem_ref)):
        gate = jax.nn.sigmoid(_dot(h, wg_ref[n]) + bg_ref[n])
        term = gate * _dot(o_ref[...], wb_ref[n])
        merged = term if merged is None else merged + term
    x2 = x + _dot(merged.astype(BF16), wo_ref[...])
    x2_ref[...] = x2

    ms2 = jnp.mean(x2 * x2, axis=-1, keepdims=True)
    h2 = x2 * lax.rsqrt(ms2 + EPS) * gf_ref[...]
    _store_token_tiles(h2_ref, h2)
    hi, lo = _split_bf16(h2)
    logits = _dot(hi, wrh_ref[...]) + _dot(hi, wrl_ref[...]) + _dot(lo, wrh_ref[...]) + br_ref[...]

    lane = lax.broadcasted_iota(I32, logits.shape, 1)
    vals = logits
    top_v, top_i = [], []
    for _ in range(TOP_K):
        mx = jnp.max(vals, axis=-1, keepdims=True)
        ix = jnp.min(jnp.where(vals == mx, lane, N_EXPERTS), axis=-1, keepdims=True)
        top_v.append(mx)
        top_i.append(ix)
        vals = jnp.where(lane == ix, -jnp.inf, vals)
    ex = [jnp.exp(v - top_v[0]) for v in top_v]
    den = ex[0] + ex[1] + ex[2] + ex[3]
    for k in range(TOP_K):
        idx_ref[:, k:k + 1] = top_i[k]
        gate_ref[:, k:k + 1] = ex[k] / den

    @pl.when(pl.program_id(0) == 0)
    def _():
        carry_ref[...] = jnp.zeros_like(carry_ref)

    lane_e = lax.broadcasted_iota(I32, (x.shape[0], LANES), 1)
    hits = [lane_e == top_i[k] for k in range(TOP_K)]
    onehot = sum(jnp.where(hk, 1.0, 0.0) for hk in hits)
    before = _dot(tri_ref[...], onehot.astype(BF16)) + carry_ref[...]
    for k in range(TOP_K):
        rank_ref[:, k:k + 1] = jnp.sum(jnp.where(hits[k], before, 0.0), axis=-1, keepdims=True).astype(I32)
    carry_ref[...] += jnp.sum(onehot, axis=0, keepdims=True)
    cnt_ref[...] = carry_ref[...]


def _merge_route(x2d, g_mix, o_sb, o_dsa, o_mem, w_gate, b_gate, w_branch, w_out, g_ffn, w_router, b_router, tm):
    n = x2d.shape[0]
    row = lambda i: (i, 0)
    c2 = lambda i: (0, 0)
    c3 = lambda i: (0, 0, 0)
    wr_hi = w_router.astype(BF16)
    wr_lo = (w_router - wr_hi.astype(F32)).astype(BF16)
    return pl.pallas_call(
        _merge_kernel,
        grid=(n // tm,),
        in_specs=[pl.BlockSpec((tm, D_MODEL), row), pl.BlockSpec((1, D_MODEL), c2),
                  pl.BlockSpec((tm, BRANCH_W), row), pl.BlockSpec((tm, BRANCH_W), row),
                  pl.BlockSpec((tm, BRANCH_W), row),
                  pl.BlockSpec((3, D_MODEL, D_MODEL), c3, pipeline_mode=pl.Buffered(1)),
                  pl.BlockSpec((3, 1, D_MODEL), c3),
                  pl.BlockSpec((3, BRANCH_W, D_MODEL), c3, pipeline_mode=pl.Buffered(1)),
                  pl.BlockSpec((D_MODEL, D_MODEL), c2, pipeline_mode=pl.Buffered(1)),
                  pl.BlockSpec((1, D_MODEL), c2), pl.BlockSpec((D_MODEL, N_EXPERTS), c2),
                  pl.BlockSpec((D_MODEL, N_EXPERTS), c2), pl.BlockSpec((1, N_EXPERTS), c2),
                  pl.BlockSpec((tm, tm), c2, pipeline_mode=pl.Buffered(1))],
        out_specs=[pl.BlockSpec((tm, D_MODEL), row), pl.BlockSpec((tm * TOKEN_TILE, LANES), row),
                   pl.BlockSpec((tm, TOP_K), row), pl.BlockSpec((tm, TOP_K), row),
                   pl.BlockSpec((tm, TOP_K), row), pl.BlockSpec((1, LANES), c2)],
        out_shape=[jax.ShapeDtypeStruct((n, D_MODEL), F32), jax.ShapeDtypeStruct((n * TOKEN_TILE, LANES), F32),
                   jax.ShapeDtypeStruct((n, TOP_K), I32), jax.ShapeDtypeStruct((n, TOP_K), F32),
                   jax.ShapeDtypeStruct((n, TOP_K), I32), jax.ShapeDtypeStruct((1, LANES), F32)],
        scratch_shapes=[pltpu.VMEM((1, LANES), F32)],
        compiler_params=_cparams(("arbitrary",)),
        name="merge_route",
    )(x2d, g_mix[None, :], o_sb, o_dsa, o_mem, w_gate.astype(BF16), b_gate[:, None, :],
      w_branch.astype(BF16), w_out.astype(BF16), g_ffn[None, :], wr_hi, wr_lo, b_router[None, :],
      _strict_tri(tm, lower=True))


DISPATCH_TOKENS = 512


def _dispatch_kernel(dest_ref, pad_ref, h_ref, xs_ref, zero_ref, sem, *, n_pad):
    n_copies = DISPATCH_TOKENS * TOP_K + n_pad
    zero_ref[...] = jnp.zeros_like(zero_ref)

    def fill(j, c):
        for r in range(2):
            dst = pl.multiple_of(pad_ref[0, 0, 2 * j + r] * TOKEN_TILE, TOKEN_TILE)
            pltpu.make_async_copy(zero_ref, xs_ref.at[pl.ds(dst, TOKEN_TILE)], sem).start(priority=r)
        return c

    lax.fori_loop(0, n_pad // 2, fill, 0, unroll=2)

    def start(t, c):
        src = pl.multiple_of(t * TOKEN_TILE, TOKEN_TILE)
        for k in range(TOP_K):
            dst = pl.multiple_of(dest_ref[0, 0, t * TOP_K + k] * TOKEN_TILE, TOKEN_TILE)
            pltpu.make_async_copy(h_ref.at[pl.ds(src, TOKEN_TILE)], xs_ref.at[pl.ds(dst, TOKEN_TILE)],
                                  sem).start(priority=k % 2)
        return c

    lax.fori_loop(0, DISPATCH_TOKENS, start, 0, unroll=2)
    total = n_copies * TOKEN_TILE
    pltpu.make_async_copy(xs_ref.at[pl.ds(0, total)], xs_ref.at[pl.ds(0, total)], sem).wait()


def _dispatch(dest, pad_slots, h2t, n_slots):
    n = h2t.shape[0] // TOKEN_TILE
    steps = n // DISPATCH_TOKENS
    per = DISPATCH_TOKENS * TOP_K
    n_pad = pad_slots.shape[0] // steps
    assert n_pad * steps == pad_slots.shape[0] and n_pad % 2 == 0
    return pl.pallas_call(
        functools.partial(_dispatch_kernel, n_pad=n_pad),
        grid=(steps,),
        in_specs=[pl.BlockSpec((1, 1, per), lambda i: (i, 0, 0), memory_space=pltpu.SMEM),
                  pl.BlockSpec((1, 1, n_pad), lambda i: (i, 0, 0), memory_space=pltpu.SMEM),
                  pl.BlockSpec((DISPATCH_TOKENS * TOKEN_TILE, LANES), lambda i: (i, 0))],
        out_specs=pl.BlockSpec(memory_space=pl.ANY),
        out_shape=jax.ShapeDtypeStruct((n_slots * TOKEN_TILE, LANES), F32),
        scratch_shapes=[pltpu.VMEM((TOKEN_TILE, LANES), F32), pltpu.SemaphoreType.DMA(())],
        compiler_params=_cparams(("arbitrary",)),
        name="moe_dispatch",
    )(dest.reshape(steps, 1, per), pad_slots.reshape(steps, 1, n_pad), h2t)


def _expert_kernel(blk_e_ref, nused_ref, x_ref, w1_ref, b1_ref, w2_ref, b2_ref, y_ref, xb_ref, w1b_ref, w2b_ref):
    i = pl.program_id(0)
    new_expert = jnp.logical_or(i == 0, blk_e_ref[i] != blk_e_ref[jnp.maximum(i - 1, 0)])

    @pl.when(jnp.logical_and(i < nused_ref[0], new_expert))
    def _():
        w1b_ref[...] = w1_ref[0].astype(BF16)
        w2b_ref[...] = w2_ref[0].astype(BF16)

    @pl.when(i < nused_ref[0])
    def _():
        for s in range(TOKEN_TILE):
            xb_ref[:, s * LANES:(s + 1) * LANES] = _load_token_tiles(x_ref, 0, MOE_BLOCK, s).astype(BF16)
        hb = _dot(xb_ref[...], w1b_ref[...]) + b1_ref[0]
        g = jnp.minimum(hb[:, :D_EXPERT], SWIGLU_LIMIT)
        u = jnp.clip(hb[:, D_EXPERT:], -SWIGLU_LIMIT, SWIGLU_LIMIT)
        act = (u + 1.0) * (g * jax.nn.sigmoid(SWIGLU_ALPHA * g))
        _store_token_tiles(y_ref, _dot(act.astype(BF16), w2b_ref[...]) + b2_ref[0])

    @pl.when(pl.program_id(0) >= nused_ref[0])
    def _():
        y_ref[...] = jnp.zeros_like(y_ref)


def _experts(blk_e, n_used, xs, w_e_in, b_e_in, w_e_out, b_e_out):
    n_slots = xs.shape[0] // TOKEN_TILE
    nblk = n_slots // MOE_BLOCK
    slot_block = pl.BlockSpec((MOE_BLOCK * TOKEN_TILE, LANES), lambda i, be, nu: (i, 0))
    grid_spec = pltpu.PrefetchScalarGridSpec(
        num_scalar_prefetch=2,
        grid=(nblk,),
        in_specs=[slot_block,
                  pl.BlockSpec((1, D_MODEL, 2 * D_EXPERT), lambda i, be, nu: (be[i], 0, 0)),
                  pl.BlockSpec((1, 1, 2 * D_EXPERT), lambda i, be, nu: (be[i], 0, 0)),
                  pl.BlockSpec((1, D_EXPERT, D_MODEL), lambda i, be, nu: (be[i], 0, 0)),
                  pl.BlockSpec((1, 1, D_MODEL), lambda i, be, nu: (be[i], 0, 0))],
        out_specs=slot_block,
        scratch_shapes=[pltpu.VMEM((MOE_BLOCK, D_MODEL), BF16), pltpu.VMEM((D_MODEL, 2 * D_EXPERT), BF16),
                        pltpu.VMEM((D_EXPERT, D_MODEL), BF16)],
    )
    return pl.pallas_call(
        _expert_kernel,
        grid_spec=grid_spec,
        out_shape=jax.ShapeDtypeStruct((n_slots * TOKEN_TILE, LANES), F32),
        compiler_params=_cparams(("arbitrary",)),
        name="moe_experts",
    )(blk_e, n_used, xs, w_e_in, b_e_in[:, None, :], w_e_out, b_e_out[:, None, :])


def _combine_kernel(dest_ref, dest_next_ref, x_ref, gate_ref, y_ref, o_ref, buf_ref, sem):
    n_copies = DISPATCH_TOKENS * TOP_K
    i = pl.program_id(0)
    slot = i % 2

    def start_gathers(idx_ref, buf, dma_sem):
        def start(t, c):
            for k in range(TOP_K):
                src = pl.multiple_of(idx_ref[0, 0, t * TOP_K + k] * TOKEN_TILE, TOKEN_TILE)
                dst = pl.multiple_of((k * DISPATCH_TOKENS + t) * TOKEN_TILE, TOKEN_TILE)
                pltpu.make_async_copy(y_ref.at[pl.ds(src, TOKEN_TILE)], buf.at[pl.ds(dst, TOKEN_TILE)],
                                      dma_sem).start(priority=k % 2)
            return c
        lax.fori_loop(0, DISPATCH_TOKENS, start, 0, unroll=2)

    @pl.when(i == 0)
    def _():
        start_gathers(dest_ref, buf_ref.at[0], sem.at[0])

    @pl.when(i + 1 < pl.num_programs(0))
    def _():
        start_gathers(dest_next_ref, buf_ref.at[1 - slot], sem.at[1 - slot])

    buf = buf_ref.at[slot]
    pltpu.make_async_copy(y_ref.at[pl.ds(0, n_copies * TOKEN_TILE)], buf, sem.at[slot]).wait()
    rows = 32
    for r0 in range(0, DISPATCH_TOKENS, rows):
        gate = gate_ref[r0:r0 + rows, :]
        gates = [jnp.broadcast_to(gate[:, k:k + 1], (rows, LANES)) for k in range(TOP_K)]
        for s in range(TOKEN_TILE):
            out = x_ref[r0:r0 + rows, s * LANES:(s + 1) * LANES]
            for k in range(TOP_K):
                out = out + gates[k] * _load_token_tiles(buf, k * DISPATCH_TOKENS + r0, rows, s)
            o_ref[r0:r0 + rows, s * LANES:(s + 1) * LANES] = out


def _combine(dest, x2, gate, ys):
    n = x2.shape[0]
    steps = n // DISPATCH_TOKENS
    per = DISPATCH_TOKENS * TOP_K
    dest3 = dest.reshape(steps, 1, per)
    return pl.pallas_call(
        _combine_kernel,
        grid=(steps,),
        in_specs=[pl.BlockSpec((1, 1, per), lambda i: (i, 0, 0), memory_space=pltpu.SMEM),
                  pl.BlockSpec((1, 1, per), lambda i: (jnp.minimum(i + 1, steps - 1), 0, 0), memory_space=pltpu.SMEM),
                  pl.BlockSpec((DISPATCH_TOKENS, D_MODEL), lambda i: (i, 0)),
                  pl.BlockSpec((DISPATCH_TOKENS, TOP_K), lambda i: (i, 0)),
                  pl.BlockSpec(memory_space=pl.ANY)],
        out_specs=pl.BlockSpec((DISPATCH_TOKENS, D_MODEL), lambda i: (i, 0)),
        out_shape=jax.ShapeDtypeStruct((n, D_MODEL), F32),
        scratch_shapes=[pltpu.VMEM((2, TOP_K * DISPATCH_TOKENS * TOKEN_TILE, LANES), F32),
                        pltpu.SemaphoreType.DMA((2,))],
        compiler_params=_cparams(("arbitrary",)),
        name="moe_combine",
    )(dest3, dest3, x2, gate, ys)


def _moe(x2, h2, top_idx, gate, rank, counts, w_e_in, b_e_in, w_e_out, b_e_out):
    n = x2.shape[0]
    counts = counts[0, :N_EXPERTS].astype(I32)
    padded = (counts + MOE_BLOCK - 1) // MOE_BLOCK * MOE_BLOCK
    pend = jnp.cumsum(padded)
    pstart = pend - padded
    nblk = -(-(n * TOP_K) // MOE_BLOCK) + N_EXPERTS
    blk_start = jnp.arange(nblk, dtype=I32) * MOE_BLOCK
    blk_e = jnp.minimum(jnp.sum((pend[None, :] <= blk_start[:, None]).astype(I32), axis=1), N_EXPERTS - 1)
    n_used = (pend[-1:] // MOE_BLOCK).astype(I32)
    onehot = top_idx[:, :, None] == jnp.arange(N_EXPERTS, dtype=I32)[None, None, :]
    dest = rank + jnp.sum(jnp.where(onehot, pstart[None, None, :], 0), axis=-1)
    n_slots = nblk * MOE_BLOCK
    pad_len = padded - counts
    pad_end = jnp.cumsum(pad_len)
    base = jnp.concatenate([pstart + counts - (pad_end - pad_len), pend[-1:] - pad_end[-1:]])
    j = jnp.arange(n_slots - n * TOP_K, dtype=I32)
    group = jnp.sum((pad_end[None, :] <= j[:, None]).astype(I32), axis=1)
    group_hot = group[:, None] == jnp.arange(N_EXPERTS + 1, dtype=I32)[None, :]
    pad_slots = j + jnp.sum(jnp.where(group_hot, base[None, :], 0), axis=1)
    xs = _dispatch(dest, pad_slots, h2, n_slots)
    ys = _experts(blk_e, n_used, xs, w_e_in, b_e_in, w_e_out, b_e_out)
    return _combine(dest, x2, gate, ys)


def _layer(x, mem, g_mix, w_in, g_q_dsa, g_k_dsa, g_q_mem, g_k_mem, g_mem, w_mem_kv, w_gate, b_gate,
           w_branch, w_out, g_ffn, w_router, b_router, w_e_in, b_e_in, w_e_out, b_e_out):
    batch, seq, _ = x.shape
    n = batch * seq
    topk = min(DSA_TOPK_MAX, seq // 4)
    x2d = x.reshape(n, D_MODEL)
    tm_dense = min(DENSE_ROWS, seq)
    sq, sk, sv, dq, iq, kvi, iw, mq, vt = _inproj(x2d, batch, seq, g_mix, w_in, g_q_dsa, g_k_dsa, g_q_mem, tm_dense)
    o_sb = _sb_attention(sq, sk, sv, min(256, seq))
    o_dsa = _dsa_attention(dq, iq, iw, kvi, vt, 128, topk)
    mlen = mem.shape[1]
    mk, mv = _mem_kv(mem.reshape(batch * mlen, D_MODEL), g_mem, w_mem_kv, g_k_mem, min(512, batch * mlen))
    mw = MEM_HEADS * MEM_HEAD_DIM
    o_mem = _mem_attention(mq.reshape(batch, seq, mw), mk.reshape(batch, mlen, mw), mv.reshape(batch, mlen, mw), seq)
    x2, h2, top_idx, gate, rank, counts = _merge_route(
        x2d, g_mix, o_sb.reshape(n, BRANCH_W), o_dsa.reshape(n, BRANCH_W), o_mem.reshape(n, mw),
        w_gate, b_gate, w_branch, w_out, g_ffn, w_router, b_router, tm_dense)
    out = _moe(x2, h2, top_idx, gate, rank, counts, w_e_in, b_e_in, w_e_out, b_e_out)
    return out.reshape(batch, seq, D_MODEL)


def kernel(x, mem, g_mix, w_in, g_q_dsa, g_k_dsa, g_q_mem, g_k_mem, g_mem, w_mem_kv, w_gate, b_gate, w_branch, w_out, g_ffn, w_router, b_router, w_e_in, b_e_in, w_e_out, b_e_out):
    for l in range(g_mix.shape[0]):
        x = _layer(x, mem, g_mix[l], w_in[l], g_q_dsa[l], g_k_dsa[l], g_q_mem[l], g_k_mem[l], g_mem[l],
                   w_mem_kv[l], w_gate[l], b_gate[l], w_branch[l], w_out[l], g_ffn[l], w_router[l],
                   b_router[l], w_e_in[l], b_e_in[l], w_e_out[l], b_e_out[l])
    return x
```

```python
import functools

import numpy as np
import jax
import jax.numpy as jnp
from jax import lax
from jax.experimental import pallas as pl
from jax.experimental.pallas import tpu as pltpu

F32 = jnp.float32
BF16 = jnp.bfloat16
I32 = jnp.int32

D_MODEL = 1024
CHUNK = 64
SB_HEADS = 8
DSA_HEADS = 8
HEAD_DIM = 64
IDX_HEADS = 4
DSA_TOPK_MAX = 256
MEM_HEADS = 4
MEM_HEAD_DIM = 128
N_EXPERTS = 32
TOP_K = 4
D_EXPERT = D_MODEL
SWIGLU_LIMIT = 7.0
SWIGLU_ALPHA = 1.702
ROPE_THETA = 10000.0
EPS = 1e-6
MOE_BLOCK = 512

BRANCH_W = 512
IN_SIZES = (512, 512, 512, 512, 64, 64, 256, 64, 4, 512)
C_SQ, C_SK, C_SV, C_DQ, C_MQ, C_IQ, C_SMALL, C_END = 0, 512, 1024, 1536, 2048, 2560, 2816, 3072
IW_LANE = 64

LANES = 128
NEG_BIG = -1e30
SB_CUTOFF = 110.0
KEY_NEG_INF = int(np.array(-np.inf, np.float32).view(np.int32)) ^ 0x7FFFFFFF
INT_MIN = -(2 ** 31)

VMEM_LIMIT = 56 * 1024 * 1024
DENSE_ROWS = 1024


def _cparams(sem):
    return pltpu.CompilerParams(dimension_semantics=sem, vmem_limit_bytes=VMEM_LIMIT)


def _dot(a, b):
    return jnp.dot(a, b, preferred_element_type=F32)


def _dot_nt(a, b):
    return lax.dot_general(a, b, (((1,), (1,)), ((), ())), preferred_element_type=F32)


def _split_bf16(x):
    hi = x.astype(BF16)
    lo = (x - hi.astype(F32)).astype(BF16)
    return hi, lo


def _dot_split(x, m_bf16):
    hi, lo = _split_bf16(x)
    return _dot(hi, m_bf16) + _dot(lo, m_bf16)


TOKEN_TILE = D_MODEL // LANES


def _store_token_tiles(ref, y):
    rows = y.shape[0]
    for s in range(TOKEN_TILE):
        ref[pl.ds(s, rows, stride=TOKEN_TILE), :] = y[:, s * LANES:(s + 1) * LANES]


def _load_token_tiles(ref, start_row, rows, s):
    return ref[pl.ds(start_row * TOKEN_TILE + s, rows, stride=TOKEN_TILE), :]


def _rot_half_unsigned(y):
    w = y.shape[1]
    lane = lax.broadcasted_iota(I32, y.shape, 1)
    return jnp.where((lane & 32) == 0, pltpu.roll(y, w - 32, 1), pltpu.roll(y, 32, 1))


def _inproj_kernel(x_ref, g_ref, w_ref, wsvt_ref, cos_ref, sin_ref, coss_ref, sins_ref, gq_ref, gks_ref,
                   gm_ref, bd64_ref, bd128_ref,
                   sq_ref, sk_ref, sv_ref, dq_ref, iq_ref, kvi_ref, iw_ref, mq_ref, vt_ref):
    x = x_ref[...]
    ms = jnp.mean(x * x, axis=-1, keepdims=True)
    h = (x * lax.rsqrt(ms + EPS) * g_ref[...]).astype(BF16)

    def seg(a, b):
        return _dot(h, w_ref[:, a:b])

    def put_heads(ref, y):
        for hd in range(y.shape[1] // HEAD_DIM):
            ref[0, hd] = y[:, hd * HEAD_DIM:(hd + 1) * HEAD_DIM].astype(BF16)

    put_heads(sq_ref, seg(C_SQ, C_SK) * (HEAD_DIM ** -0.5))
    put_heads(sk_ref, seg(C_SK, C_SV))
    sv_ref[0] = _dot_nt(wsvt_ref[...], h).astype(BF16)

    y = seg(C_DQ, C_MQ)
    msq = _dot_split(y * y, bd64_ref[...]) * (1.0 / HEAD_DIM)
    y = y * lax.rsqrt(msq + EPS) * gq_ref[...]
    y = y * cos_ref[...] + _rot_half_unsigned(y) * sin_ref[...]
    put_heads(dq_ref, y * (HEAD_DIM ** -0.5))

    y = seg(C_IQ, C_SMALL)
    y = y * cos_ref[:, :256] + _rot_half_unsigned(y) * sin_ref[:, :256]
    iq_ref[...] = (y * (HEAD_DIM ** -0.5)).astype(BF16)

    y = seg(C_SMALL, C_END)
    lane = lax.broadcasted_iota(I32, y.shape, 1)
    is_k = lane < HEAD_DIM
    msk = jnp.sum(jnp.where(is_k, y * y, 0.0), axis=-1, keepdims=True) * (1.0 / HEAD_DIM)
    y = y * jnp.where(is_k, lax.rsqrt(msk + EPS) * gks_ref[...], 1.0)
    y = y * coss_ref[...] + _rot_half_unsigned(y) * sins_ref[...]
    kvi_ref[...] = y.astype(BF16)
    iw_ref[...] = y[:, 128:256]
    kv_t = y[:, 0:128].T
    row_t = lax.broadcasted_iota(I32, kv_t.shape, 0)
    vt_ref[0] = jnp.where(row_t < HEAD_DIM, 1.0, kv_t).astype(BF16)

    y = seg(C_MQ, C_IQ)
    msm = _dot_split(y * y, bd128_ref[...]) * (1.0 / MEM_HEAD_DIM)
    mq_ref[...] = (y * lax.rsqrt(msm + EPS) * gm_ref[...]).astype(BF16)


def _rope_tables(seq):
    half = HEAD_DIM // 2
    inv = ROPE_THETA ** (-jnp.arange(half, dtype=F32) / half)
    ang = jnp.arange(seq).astype(F32)[:, None] * inv[None, :]
    cos = jnp.cos(ang)
    sin = jnp.sin(ang)
    cos64 = jnp.concatenate([cos, cos], axis=1)
    sin64 = jnp.concatenate([-sin, sin], axis=1)
    one = jnp.ones_like(cos64)
    zero = jnp.zeros_like(cos64)
    cosq = jnp.tile(cos64, (1, 8))
    sinq = jnp.tile(sin64, (1, 8))
    coss = jnp.concatenate([cos64, one, cos64, one], axis=1)
    sins = jnp.concatenate([sin64, zero, sin64, zero], axis=1)
    return cosq, sinq, coss, sins


def _block_diag_ones(width, group):
    idx = np.arange(width) // group
    return jnp.asarray((idx[:, None] == idx[None, :]).astype(np.float32), dtype=BF16)


def _inproj(x2d, batch, seq, g_mix, w_in, g_q_dsa, g_k_dsa, g_q_mem, tm):
    n = x2d.shape[0]
    sizes = np.cumsum((0,) + IN_SIZES)
    col = {name: (int(sizes[i]), int(sizes[i + 1])) for i, name in enumerate(
        ("sq", "sk", "sv", "dq", "dk", "dv", "iq", "ik", "iw", "mq"))}
    order = ("sq", "sk", "sv", "dq", "mq", "iq", "dk", "dv", "ik", "iw")
    w = jnp.concatenate([w_in[:, col[k][0]:col[k][1]] for k in order]
                        + [jnp.zeros((D_MODEL, C_END - sum(IN_SIZES)), w_in.dtype)], axis=1).astype(BF16)
    cosq, sinq, coss, sins = (jnp.asarray(t) for t in _rope_tables(seq))
    gq = jnp.tile(g_q_dsa, 8)[None, :]
    gks = jnp.concatenate([g_k_dsa, jnp.ones((256 - HEAD_DIM,), F32)])[None, :]
    gm = jnp.tile(g_q_mem, MEM_HEADS)[None, :]
    spb = seq // tm
    row = lambda i: (i, 0)
    const = lambda i: (0, 0)
    pos = lambda i: (i % spb, 0)
    heads = lambda i: (i // spb, 0, i % spb, 0)
    head_shape = jax.ShapeDtypeStruct((batch, 8, seq, HEAD_DIM), BF16)
    head_spec = pl.BlockSpec((1, 8, tm, HEAD_DIM), heads)
    return pl.pallas_call(
        _inproj_kernel,
        grid=(n // tm,),
        in_specs=[
            pl.BlockSpec((tm, D_MODEL), row),
            pl.BlockSpec((1, D_MODEL), const),
            pl.BlockSpec((D_MODEL, C_END), const, pipeline_mode=pl.Buffered(1)),
            pl.BlockSpec((512, D_MODEL), const, pipeline_mode=pl.Buffered(1)),
            pl.BlockSpec((tm, 512), pos), pl.BlockSpec((tm, 512), pos),
            pl.BlockSpec((tm, 256), pos), pl.BlockSpec((tm, 256), pos),
            pl.BlockSpec((1, 512), const), pl.BlockSpec((1, 256), const), pl.BlockSpec((1, 512), const),
            pl.BlockSpec((512, 512), const), pl.BlockSpec((512, 512), const),
        ],
        out_specs=[head_spec, head_spec, pl.BlockSpec((1, 512, tm), lambda i: (i // spb, 0, i % spb)), head_spec,
                   pl.BlockSpec((tm, 256), row), pl.BlockSpec((tm, 256), row),
                   pl.BlockSpec((tm, 128), row), pl.BlockSpec((tm, 512), row),
                   pl.BlockSpec((1, 128, tm), lambda i: (i // spb, 0, i % spb))],
        out_shape=[head_shape, head_shape, jax.ShapeDtypeStruct((batch, 512, seq), BF16), head_shape,
                   jax.ShapeDtypeStruct((n, 256), BF16), jax.ShapeDtypeStruct((n, 256), BF16),
                   jax.ShapeDtypeStruct((n, 128), F32), jax.ShapeDtypeStruct((n, 512), BF16),
                   jax.ShapeDtypeStruct((batch, 128, seq), BF16)],
        compiler_params=_cparams(("parallel",)),
        name="inproj",
    )(x2d, g_mix[None, :], w, w_in[:, col["sv"][0]:col["sv"][1]].T.astype(BF16), cosq, sinq, coss, sins, gq, gks, gm,
      _block_diag_ones(512, HEAD_DIM), _block_diag_ones(512, MEM_HEAD_DIM))


def _sb_kernel(q_ref, k_ref, vt_ref, u_ref, o_ref, acc_ref, car_ref, *, tq):
    qi = pl.program_id(1)
    rows = lax.broadcasted_iota(I32, (tq, tq), 0)
    cols = lax.broadcasted_iota(I32, (tq, tq), 1)
    dif = rows - cols
    u = u_ref[...]
    acc_ref[...] = jnp.zeros_like(acc_ref)
    car_ref[...] = jnp.zeros_like(car_ref)

    def cond(c):
        kb, mx = c
        return jnp.logical_and(kb >= 0, mx > -SB_CUTOFF)

    def body(c):
        kb, _ = c
        ks = pl.multiple_of(kb * tq, tq)
        earlier = dif < (qi - kb) * tq
        neg_mask = jnp.where(earlier, -1.0, 0.0).astype(BF16)
        heads = range(SB_HEADS)
        z = [_dot_nt(k_ref[0, hd, pl.ds(ks, tq), :], q_ref[0, hd]) for hd in heads]
        ls, lk, between = [], [], []
        for hd in heads:
            zb = z[hd].astype(BF16)
            sp = jnp.maximum(zb, 0.0) + jnp.log(1.0 + jnp.exp(-jnp.abs(zb)))
            ls.append(z[hd] - sp.astype(F32))
            lk.append(sp * neg_mask)
            between.append(_dot(u, lk[hd]))
        for hd in heads:
            rs = slice(hd * HEAD_DIM, (hd + 1) * HEAD_DIM)
            car = car_ref[hd:hd + 1, :]
            w = jnp.where(earlier, jnp.exp(ls[hd] + between[hd] + car), 0.0)
            acc_ref[rs, :] += _dot(vt_ref[0, rs, pl.ds(ks, tq)], w.astype(BF16))
            car_ref[hd:hd + 1, :] = car + (between[hd][0:1, :] + lk[hd][0:1, :].astype(F32))
        return kb - 1, jnp.max(car_ref[...])

    lax.while_loop(cond, body, (qi, jnp.float32(0.0)))
    o_ref[0] = acc_ref[...].T.astype(BF16)


def _strict_tri(n, lower):
    i = np.arange(n)
    m = (i[:, None] > i[None, :]) if lower else (i[:, None] < i[None, :])
    return jnp.asarray(m.astype(np.float32), dtype=BF16)


def _sb_attention(sq, sk, svt, tq):
    batch, _, seq, _ = sq.shape
    return pl.pallas_call(
        functools.partial(_sb_kernel, tq=tq),
        grid=(batch, seq // tq),
        in_specs=[pl.BlockSpec((1, SB_HEADS, tq, HEAD_DIM), lambda b, i: (b, 0, i, 0)),
                  pl.BlockSpec((1, SB_HEADS, seq, HEAD_DIM), lambda b, i: (b, 0, 0, 0)),
                  pl.BlockSpec((1, BRANCH_W, seq), lambda b, i: (b, 0, 0)),
                  pl.BlockSpec((tq, tq), lambda b, i: (0, 0))],
        out_specs=pl.BlockSpec((1, tq, BRANCH_W), lambda b, i: (b, i, 0)),
        out_shape=jax.ShapeDtypeStruct((batch, seq, BRANCH_W), BF16),
        scratch_shapes=[pltpu.VMEM((BRANCH_W, tq), F32), pltpu.VMEM((SB_HEADS, tq), F32)],
        compiler_params=_cparams(("parallel", "parallel")),
        name="sb_attention",
    )(sq, sk, svt, _strict_tri(tq, lower=False))


DSA_SEG = 256
DSA_KB = DSA_SEG


def _tree_sum(parts):
    while len(parts) > 1:
        parts = [parts[i] + parts[i + 1] for i in range(0, len(parts) - 1, 2)] + ([parts[-1]] if len(parts) % 2 else [])
    return parts[0]


def _dsa_kernel(dq_ref, iq_ref, iw_ref, kvi_ref, vt_ref, tri_ref, o_ref,
                sc_ref, bias_ref, s_ref, *, tq, topk, nseg_max):
    nseg = (pl.program_id(1) * tq) // DSA_SEG + 1
    for ns in range(1, nseg_max + 1):
        @pl.when(nseg == ns)
        def _(ns=ns):
            _dsa_body(dq_ref, iq_ref, iw_ref, kvi_ref, vt_ref, tri_ref, o_ref, sc_ref, bias_ref, s_ref,
                      tq=tq, topk=topk, nseg=ns)


def _dsa_body(dq_ref, iq_ref, iw_ref, kvi_ref, vt_ref, tri_ref, o_ref, sc_ref, bias_ref, s_ref, *, tq, topk, nseg):
    qs = pl.program_id(1) * tq
    blocks = [slice(c * DSA_KB, (c + 1) * DSA_KB) for c in range(nseg)]
    iq = iq_ref[0]
    w_t = iw_ref[0].T
    w_row = [w_t[IW_LANE + h:IW_LANE + h + 1, :] * (IDX_HEADS ** -0.5) for h in range(IDX_HEADS)]
    q_chunk = (qs + lax.broadcasted_iota(I32, (DSA_KB, tq), 1)) // CHUNK
    k_chunk = lax.broadcasted_iota(I32, (DSA_KB, tq), 0) // CHUNK

    for c, blk in enumerate(blocks):
        ik = kvi_ref[0, blk, 128:192]
        lg = [_dot_nt(ik, iq[:, h * HEAD_DIM:(h + 1) * HEAD_DIM]) for h in range(IDX_HEADS)]
        sc = jnp.zeros((DSA_KB, tq), F32)
        for h in range(IDX_HEADS):
            sc = sc + w_row[h] * jnp.maximum(lg[h], 0.0)
        admissible = (c * (DSA_KB // CHUNK) + k_chunk) <= q_chunk
        sc_ref[blk, :] = jnp.where(admissible, sc, -jnp.inf)

    def count(pred_fn):
        sub, lanes_of_sums = 32, 4
        acc = [jnp.zeros((sub, tq), F32)] * lanes_of_sums
        for j in range(nseg * DSA_SEG // sub):
            acc[j % lanes_of_sums] = acc[j % lanes_of_sums] + jnp.where(
                pred_fn(sc_ref[j * sub:(j + 1) * sub, :]), 1.0, 0.0)
        return jnp.sum(_tree_sum(acc), axis=0, keepdims=True)

    def key_to_float(key):
        return lax.bitcast_convert_type(jnp.where(key >= 0, key, key ^ 0x7FFFFFFF), F32)

    kf = jnp.float32(topk)
    n_rows = jnp.float32(nseg * DSA_SEG)
    cnt0 = count(lambda s: s >= 0.0)
    t0 = jnp.where(cnt0 >= kf, 0, INT_MIN).astype(I32)
    cnt_t0 = jnp.where(cnt0 >= kf, cnt0, n_rows)

    def bit_step(i, carry):
        t, cnt_t = carry
        cand = t + lax.shift_left(jnp.int32(1), 30 - i)
        cand_f = key_to_float(cand)
        cnt = jnp.where(cand <= KEY_NEG_INF, n_rows, count(lambda s: s >= cand_f))
        take = cnt >= kf
        return jnp.where(take, cand, t), jnp.where(take, cnt, cnt_t)

    thr_key, cnt_thr = lax.fori_loop(0, 31, bit_step, (t0, cnt_t0))
    thr = jnp.where(thr_key <= KEY_NEG_INF, -jnp.inf, key_to_float(thr_key))
    tri = tri_ref[...]

    surplus = jnp.logical_or(jnp.max(cnt_thr) > kf, jnp.min(thr) == -jnp.inf)

    @pl.when(surplus)
    def _():
        need = kf - count(lambda s: s > thr)
        prefix = jnp.zeros((1, tq), F32)
        for blk in blocks:
            sc = sc_ref[blk, :]
            eqf = jnp.where(sc == thr, 1.0, 0.0)
            rank = _dot(tri, eqf.astype(BF16)) + prefix
            tie = jnp.where(rank < need, eqf, 0.0)
            sel = jnp.where(sc > thr, 1.0, tie)
            bias_ref[blk, :] = jnp.where(sc > -jnp.inf, (sel - 1.0) * (-NEG_BIG), NEG_BIG)
            prefix = prefix + jnp.sum(eqf, axis=0, keepdims=True)

    @pl.when(jnp.logical_not(surplus))
    def _():
        for blk in blocks:
            bias_ref[blk, :] = jnp.where(sc_ref[blk, :] >= thr, 0.0, NEG_BIG)

    q8 = dq_ref[0].reshape(DSA_HEADS * tq, HEAD_DIM)
    m = jnp.full((1, DSA_HEADS * tq), NEG_BIG, F32)
    for blk in blocks:
        b = bias_ref[blk, :]
        s = _dot_nt(kvi_ref[0, blk, 0:HEAD_DIM], q8) + jnp.concatenate([b] * DSA_HEADS, axis=1)
        s_ref[blk, :] = s
        m = jnp.maximum(m, jnp.max(s, axis=0, keepdims=True))
    acc = jnp.zeros((128, DSA_HEADS * tq), F32)
    for blk in blocks:
        p = jnp.exp(s_ref[blk, :] - m)
        acc = acc + _dot(vt_ref[0, :, blk], p.astype(BF16))
    for hd in range(DSA_HEADS):
        a = acc[:, hd * tq:(hd + 1) * tq]
        o = (a / a[0:1, :]).T
        o_ref[0, :, hd * HEAD_DIM:(hd + 1) * HEAD_DIM] = o[:, HEAD_DIM:].astype(BF16)


def _dsa_attention(dq, iq, iw, kvi, vt, tq, topk):
    batch, _, seq, _ = dq.shape
    return pl.pallas_call(
        functools.partial(_dsa_kernel, tq=tq, topk=topk, nseg_max=seq // DSA_SEG),
        grid=(batch, seq // tq),
        in_specs=[
            pl.BlockSpec((1, DSA_HEADS, tq, HEAD_DIM), lambda b, i: (b, 0, i, 0)),
            pl.BlockSpec((1, tq, 256), lambda b, i: (b, i, 0)),
            pl.BlockSpec((1, tq, 128), lambda b, i: (b, i, 0)),
            pl.BlockSpec((1, seq, 256), lambda b, i: (b, 0, 0)),
            pl.BlockSpec((1, 128, seq), lambda b, i: (b, 0, 0)),
            pl.BlockSpec((DSA_KB, DSA_KB), lambda b, i: (0, 0)),
        ],
        out_specs=pl.BlockSpec((1, tq, BRANCH_W), lambda b, i: (b, i, 0)),
        out_shape=jax.ShapeDtypeStruct((batch, seq, BRANCH_W), BF16),
        scratch_shapes=[pltpu.VMEM((seq, tq), F32), pltpu.VMEM((seq, tq), F32),
                        pltpu.VMEM((seq, DSA_HEADS * tq), F32)],
        compiler_params=_cparams(("parallel", "parallel")),
        name="dsa_attention",
    )(dq, iq.reshape(batch, seq, 256), iw.reshape(batch, seq, 128), kvi.reshape(batch, seq, 256), vt,
      _strict_tri(DSA_KB, lower=True))


def _memkv_kernel(m_ref, g_ref, w_ref, gk_ref, bd_ref, mk_ref, mv_ref):
    x = m_ref[...]
    ms = jnp.mean(x * x, axis=-1, keepdims=True)
    h = (x * lax.rsqrt(ms + EPS) * g_ref[...]).astype(BF16)
    mw = MEM_HEADS * MEM_HEAD_DIM
    k = _dot(h, w_ref[:, :mw])
    msk = _dot_split(k * k, bd_ref[...]) * (1.0 / MEM_HEAD_DIM)
    mk_ref[...] = (k * lax.rsqrt(msk + EPS) * gk_ref[...]).astype(BF16)
    mv_ref[...] = _dot(h, w_ref[:, mw:]).astype(BF16)


def _mem_kv(mem2d, g_mem, w_mem_kv, g_k_mem, tm):
    n = mem2d.shape[0]
    mw = MEM_HEADS * MEM_HEAD_DIM
    row = lambda i: (i, 0)
    const = lambda i: (0, 0)
    return pl.pallas_call(
        _memkv_kernel,
        grid=(n // tm,),
        in_specs=[pl.BlockSpec((tm, D_MODEL), row), pl.BlockSpec((1, D_MODEL), const),
                  pl.BlockSpec((D_MODEL, 2 * mw), const), pl.BlockSpec((1, mw), const),
                  pl.BlockSpec((mw, mw), const)],
        out_specs=[pl.BlockSpec((tm, mw), row), pl.BlockSpec((tm, mw), row)],
        out_shape=[jax.ShapeDtypeStruct((n, mw), BF16), jax.ShapeDtypeStruct((n, mw), BF16)],
        compiler_params=_cparams(("parallel",)),
        name="mem_kv",
    )(mem2d, g_mem[None, :], w_mem_kv.astype(BF16), jnp.tile(g_k_mem, MEM_HEADS)[None, :],
      _block_diag_ones(mw, MEM_HEAD_DIM))


def _memattn_kernel(q_ref, k_ref, v_ref, o_ref):
    for hd in range(MEM_HEADS):
        sl = slice(hd * MEM_HEAD_DIM, (hd + 1) * MEM_HEAD_DIM)
        s = _dot_nt(q_ref[0, :, sl], k_ref[0, :, sl]) * (MEM_HEAD_DIM ** -0.5)
        p = jnp.exp(s - jnp.max(s, axis=-1, keepdims=True))
        o = _dot(p.astype(BF16), v_ref[0, :, sl]) / jnp.sum(p, axis=-1, keepdims=True)
        o_ref[0, :, sl] = o.astype(BF16)


def _mem_attention(mq, mk, mv, tq):
    batch, seq, mw = mq.shape
    mlen = mk.shape[1]
    kspec = pl.BlockSpec((1, mlen, mw), lambda b, i: (b, 0, 0))
    return pl.pallas_call(
        _memattn_kernel,
        grid=(batch, seq // tq),
        in_specs=[pl.BlockSpec((1, tq, mw), lambda b, i: (b, i, 0)), kspec, kspec],
        out_specs=pl.BlockSpec((1, tq, mw), lambda b, i: (b, i, 0)),
        out_shape=jax.ShapeDtypeStruct((batch, seq, mw), BF16),
        compiler_params=_cparams(("parallel", "parallel")),
        name="mem_attention",
    )(mq, mk, mv)


def _merge_kernel(x_ref, g_ref, osb_ref, odsa_ref, omem_ref, wg_ref, bg_ref, wb_ref, wo_ref,
                  gf_ref, wrh_ref, wrl_ref, br_ref, tri_ref,
                  x2_ref, h2_ref, idx_ref, gate_ref, rank_ref, cnt_ref, carry_ref):
    x = x_ref[...]
    ms = jnp.mean(x * x, axis=-1, keepdims=True)
    h = (x * lax.rsqrt(ms + EPS) * g_ref[...]).astype(BF16)
    merged = None
    for n, o_ref in enumerate((osb_ref, odsa_ref, omem_ref)):
        gate = jax.nn.sigmoid(_dot(h, wg_ref[n]) + bg_ref[n])
        term = gate * _dot(o_ref[...], wb_ref[n])
        merged = term if merged is None else merged + term
    x2 = x + _dot(merged.astype(BF16), wo_ref[...])
    x2_ref[...] = x2

    ms2 = jnp.mean(x2 * x2, axis=-1, keepdims=True)
    h2 = x2 * lax.rsqrt(ms2 + EPS) * gf_ref[...]
    _store_token_tiles(h2_ref, h2)
    hi, lo = _split_bf16(h2)
    logits = _dot(hi, wrh_ref[...]) + _dot(hi, wrl_ref[...]) + _dot(lo, wrh_ref[...]) + br_ref[...]

    lane = lax.broadcasted_iota(I32, logits.shape, 1)
    vals = logits
    top_v, top_i = [], []
    for _ in range(TOP_K):
        mx = jnp.max(vals, axis=-1, keepdims=True)
        ix = jnp.min(jnp.where(vals == mx, lane, N_EXPERTS), axis=-1, keepdims=True)
        top_v.append(mx)
        top_i.append(ix)
        vals = jnp.where(lane == ix, -jnp.inf, vals)
    ex = [jnp.exp(v - top_v[0]) for v in top_v]
    den = ex[0] + ex[1] + ex[2] + ex[3]
    for k in range(TOP_K):
        idx_ref[:, k:k + 1] = top_i[k]
        gate_ref[:, k:k + 1] = ex[k] / den

    @pl.when(pl.program_id(0) == 0)
    def _():
        carry_ref[...] = jnp.zeros_like(carry_ref)

    lane_e = lax.broadcasted_iota(I32, (x.shape[0], LANES), 1)
    hits = [lane_e == top_i[k] for k in range(TOP_K)]
    onehot = sum(jnp.where(hk, 1.0, 0.0) for hk in hits)
    before = _dot(tri_ref[...], onehot.astype(BF16)) + carry_ref[...]
    for k in range(TOP_K):
        rank_ref[:, k:k + 1] = jnp.sum(jnp.where(hits[k], before, 0.0), axis=-1, keepdims=True).astype(I32)
    carry_ref[...] += jnp.sum(onehot, axis=0, keepdims=True)
    cnt_ref[...] = carry_ref[...]


def _merge_route(x2d, g_mix, o_sb, o_dsa, o_mem, w_gate, b_gate, w_branch, w_out, g_ffn, w_router, b_router, tm):
    n = x2d.shape[0]
    row = lambda i: (i, 0)
    c2 = lambda i: (0, 0)
    c3 = lambda i: (0, 0, 0)
    wr_hi = w_router.astype(BF16)
    wr_lo = (w_router - wr_hi.astype(F32)).astype(BF16)
    return pl.pallas_call(
        _merge_kernel,
        grid=(n // tm,),
        in_specs=[pl.BlockSpec((tm, D_MODEL), row), pl.BlockSpec((1, D_MODEL), c2),
                  pl.BlockSpec((tm, BRANCH_W), row), pl.BlockSpec((tm, BRANCH_W), row),
                  pl.BlockSpec((tm, BRANCH_W), row),
                  pl.BlockSpec((3, D_MODEL, D_MODEL), c3, pipeline_mode=pl.Buffered(1)),
                  pl.BlockSpec((3, 1, D_MODEL), c3),
                  pl.BlockSpec((3, BRANCH_W, D_MODEL), c3, pipeline_mode=pl.Buffered(1)),
                  pl.BlockSpec((D_MODEL, D_MODEL), c2, pipeline_mode=pl.Buffered(1)),
                  pl.BlockSpec((1, D_MODEL), c2), pl.BlockSpec((D_MODEL, N_EXPERTS), c2),
                  pl.BlockSpec((D_MODEL, N_EXPERTS), c2), pl.BlockSpec((1, N_EXPERTS), c2),
                  pl.BlockSpec((tm, tm), c2, pipeline_mode=pl.Buffered(1))],
        out_specs=[pl.BlockSpec((tm, D_MODEL), row), pl.BlockSpec((tm * TOKEN_TILE, LANES), row),
                   pl.BlockSpec((tm, TOP_K), row), pl.BlockSpec((tm, TOP_K), row),
                   pl.BlockSpec((tm, TOP_K), row), pl.BlockSpec((1, LANES), c2)],
        out_shape=[jax.ShapeDtypeStruct((n, D_MODEL), F32), jax.ShapeDtypeStruct((n * TOKEN_TILE, LANES), F32),
                   jax.ShapeDtypeStruct((n, TOP_K), I32), jax.ShapeDtypeStruct((n, TOP_K), F32),
                   jax.ShapeDtypeStruct((n, TOP_K), I32), jax.ShapeDtypeStruct((1, LANES), F32)],
        scratch_shapes=[pltpu.VMEM((1, LANES), F32)],
        compiler_params=_cparams(("arbitrary",)),
        name="merge_route",
    )(x2d, g_mix[None, :], o_sb, o_dsa, o_mem, w_gate.astype(BF16), b_gate[:, None, :],
      w_branch.astype(BF16), w_out.astype(BF16), g_ffn[None, :], wr_hi, wr_lo, b_router[None, :],
      _strict_tri(tm, lower=True))


DISPATCH_TOKENS = 1024


def _dispatch_kernel(dest_ref, pad_ref, h_ref, xs_ref, zero_ref, sem, *, n_pad):
    n_copies = DISPATCH_TOKENS * TOP_K + n_pad
    zero_ref[...] = jnp.zeros_like(zero_ref)

    def fill(j, c):
        for r in range(2):
            dst = pl.multiple_of(pad_ref[0, 0, 2 * j + r] * TOKEN_TILE, TOKEN_TILE)
            pltpu.make_async_copy(zero_ref, xs_ref.at[pl.ds(dst, TOKEN_TILE)], sem).start(priority=r)
        return c

    lax.fori_loop(0, n_pad // 2, fill, 0, unroll=2)

    def start(t, c):
        src = pl.multiple_of(t * TOKEN_TILE, TOKEN_TILE)
        for k in range(TOP_K):
            dst = pl.multiple_of(dest_ref[0, 0, t * TOP_K + k] * TOKEN_TILE, TOKEN_TILE)
            pltpu.make_async_copy(h_ref.at[pl.ds(src, TOKEN_TILE)], xs_ref.at[pl.ds(dst, TOKEN_TILE)],
                                  sem).start(priority=k % 2)
        return c

    lax.fori_loop(0, DISPATCH_TOKENS, start, 0, unroll=2)
    total = n_copies * TOKEN_TILE
    pltpu.make_async_copy(xs_ref.at[pl.ds(0, total)], xs_ref.at[pl.ds(0, total)], sem).wait()


def _dispatch(dest, pad_slots, h2t, n_slots):
    n = h2t.shape[0] // TOKEN_TILE
    steps = n // DISPATCH_TOKENS
    per = DISPATCH_TOKENS * TOP_K
    n_pad = pad_slots.shape[0] // steps
    assert n_pad * steps == pad_slots.shape[0] and n_pad % 2 == 0
    return pl.pallas_call(
        functools.partial(_dispatch_kernel, n_pad=n_pad),
        grid=(steps,),
        in_specs=[pl.BlockSpec((1, 1, per), lambda i: (i, 0, 0), memory_space=pltpu.SMEM),
                  pl.BlockSpec((1, 1, n_pad), lambda i: (i, 0, 0), memory_space=pltpu.SMEM),
                  pl.BlockSpec((DISPATCH_TOKENS * TOKEN_TILE, LANES), lambda i: (i, 0))],
        out_specs=pl.BlockSpec(memory_space=pl.ANY),
        out_shape=jax.ShapeDtypeStruct((n_slots * TOKEN_TILE, LANES), F32),
        scratch_shapes=[pltpu.VMEM((TOKEN_TILE, LANES), F32), pltpu.SemaphoreType.DMA(())],
        compiler_params=_cparams(("arbitrary",)),
        name="moe_dispatch",
    )(dest.reshape(steps, 1, per), pad_slots.reshape(steps, 1, n_pad), h2t)


def _expert_kernel(blk_e_ref, nused_ref, x_ref, w1_ref, b1_ref, w2_ref, b2_ref, y_ref, xb_ref, w1b_ref, w2b_ref):
    i = pl.program_id(0)
    new_expert = jnp.logical_or(i == 0, blk_e_ref[i] != blk_e_ref[jnp.maximum(i - 1, 0)])

    @pl.when(jnp.logical_and(i < nused_ref[0], new_expert))
    def _():
        w1b_ref[...] = w1_ref[0].astype(BF16)
        w2b_ref[...] = w2_ref[0].astype(BF16)

    @pl.when(i < nused_ref[0])
    def _():
        for s in range(TOKEN_TILE):
            xb_ref[:, s * LANES:(s + 1) * LANES] = _load_token_tiles(x_ref, 0, MOE_BLOCK, s).astype(BF16)
        hb = _dot(xb_ref[...], w1b_ref[...]) + b1_ref[0]
        g = jnp.minimum(hb[:, :D_EXPERT], SWIGLU_LIMIT)
        u = jnp.clip(hb[:, D_EXPERT:], -SWIGLU_LIMIT, SWIGLU_LIMIT)
        act = (u + 1.0) * (g * jax.nn.sigmoid(SWIGLU_ALPHA * g))
        _store_token_tiles(y_ref, _dot(act.astype(BF16), w2b_ref[...]) + b2_ref[0])

    @pl.when(pl.program_id(0) >= nused_ref[0])
    def _():
        y_ref[...] = jnp.zeros_like(y_ref)


def _experts(blk_e, n_used, xs, w_e_in, b_e_in, w_e_out, b_e_out):
    n_slots = xs.shape[0] // TOKEN_TILE
    nblk = n_slots // MOE_BLOCK
    slot_block = pl.BlockSpec((MOE_BLOCK * TOKEN_TILE, LANES), lambda i, be, nu: (i, 0))
    grid_spec = pltpu.PrefetchScalarGridSpec(
        num_scalar_prefetch=2,
        grid=(nblk,),
        in_specs=[slot_block,
                  pl.BlockSpec((1, D_MODEL, 2 * D_EXPERT), lambda i, be, nu: (be[i], 0, 0)),
                  pl.BlockSpec((1, 1, 2 * D_EXPERT), lambda i, be, nu: (be[i], 0, 0)),
                  pl.BlockSpec((1, D_EXPERT, D_MODEL), lambda i, be, nu: (be[i], 0, 0)),
                  pl.BlockSpec((1, 1, D_MODEL), lambda i, be, nu: (be[i], 0, 0))],
        out_specs=slot_block,
        scratch_shapes=[pltpu.VMEM((MOE_BLOCK, D_MODEL), BF16), pltpu.VMEM((D_MODEL, 2 * D_EXPERT), BF16),
                        pltpu.VMEM((D_EXPERT, D_MODEL), BF16)],
    )
    return pl.pallas_call(
        _expert_kernel,
        grid_spec=grid_spec,
        out_shape=jax.ShapeDtypeStruct((n_slots * TOKEN_TILE, LANES), F32),
        compiler_params=_cparams(("arbitrary",)),
        name="moe_experts",
    )(blk_e, n_used, xs, w_e_in, b_e_in[:, None, :], w_e_out, b_e_out[:, None, :])


def _combine_kernel(dest_ref, dest_next_ref, x_ref, gate_ref, y_ref, o_ref, buf_ref, sem):
    n_copies = DISPATCH_TOKENS * TOP_K
    i = pl.program_id(0)
    slot = i % 2

    def start_gathers(idx_ref, buf, dma_sem):
        def start(t, c):
            for k in range(TOP_K):
                src = pl.multiple_of(idx_ref[0, 0, t * TOP_K + k] * TOKEN_TILE, TOKEN_TILE)
                dst = pl.multiple_of((k * DISPATCH_TOKENS + t) * TOKEN_TILE, TOKEN_TILE)
                pltpu.make_async_copy(y_ref.at[pl.ds(src, TOKEN_TILE)], buf.at[pl.ds(dst, TOKEN_TILE)],
                                      dma_sem).start(priority=k % 2)
            return c
        lax.fori_loop(0, DISPATCH_TOKENS, start, 0, unroll=2)

    @pl.when(i == 0)
    def _():
        start_gathers(dest_ref, buf_ref.at[0], sem.at[0])

    @pl.when(i + 1 < pl.num_programs(0))
    def _():
        start_gathers(dest_next_ref, buf_ref.at[1 - slot], sem.at[1 - slot])

    buf = buf_ref.at[slot]
    pltpu.make_async_copy(y_ref.at[pl.ds(0, n_copies * TOKEN_TILE)], buf, sem.at[slot]).wait()
    rows = 32
    for r0 in range(0, DISPATCH_TOKENS, rows):
        gate = gate_ref[r0:r0 + rows, :]
        gates = [jnp.broadcast_to(gate[:, k:k + 1], (rows, LANES)) for k in range(TOP_K)]
        for s in range(TOKEN_TILE):
            out = x_ref[r0:r0 + rows, s * LANES:(s + 1) * LANES]
            for k in range(TOP_K):
                out = out + gates[k] * _load_token_tiles(buf, k * DISPATCH_TOKENS + r0, rows, s)
            o_ref[r0:r0 + rows, s * LANES:(s + 1) * LANES] = out


def _combine(dest, x2, gate, ys):
    n = x2.shape[0]
    steps = n // DISPATCH_TOKENS
    per = DISPATCH_TOKENS * TOP_K
    dest3 = dest.reshape(steps, 1, per)
    return pl.pallas_call(
        _combine_kernel,
        grid=(steps,),
        in_specs=[pl.BlockSpec((1, 1, per), lambda i: (i, 0, 0), memory_space=pltpu.SMEM),
                  pl.BlockSpec((1, 1, per), lambda i: (jnp.minimum(i + 1, steps - 1), 0, 0), memory_space=pltpu.SMEM),
                  pl.BlockSpec((DISPATCH_TOKENS, D_MODEL), lambda i: (i, 0)),
                  pl.BlockSpec((DISPATCH_TOKENS, TOP_K), lambda i: (i, 0)),
                  pl.BlockSpec(memory_space=pl.ANY)],
        out_specs=pl.BlockSpec((DISPATCH_TOKENS, D_MODEL), lambda i: (i, 0)),
        out_shape=jax.ShapeDtypeStruct((n, D_MODEL), F32),
        scratch_shapes=[pltpu.VMEM((2, TOP_K * DISPATCH_TOKENS * TOKEN_TILE, LANES), F32),
                        pltpu.SemaphoreType.DMA((2,))],
        compiler_params=_cparams(("arbitrary",)),
        name="moe_combine",
    )(dest3, dest3, x2, gate, ys)


def _moe(x2, h2, top_idx, gate, rank, counts, w_e_in, b_e_in, w_e_out, b_e_out):
    n = x2.shape[0]
    counts = counts[0, :N_EXPERTS].astype(I32)
    padded = (counts + MOE_BLOCK - 1) // MOE_BLOCK * MOE_BLOCK
    pend = jnp.cumsum(padded)
    pstart = pend - padded
    nblk = -(-(n * TOP_K) // MOE_BLOCK) + N_EXPERTS
    blk_start = jnp.arange(nblk, dtype=I32) * MOE_BLOCK
    blk_e = jnp.minimum(jnp.sum((pend[None, :] <= blk_start[:, None]).astype(I32), axis=1), N_EXPERTS - 1)
    n_used = (pend[-1:] // MOE_BLOCK).astype(I32)
    onehot = top_idx[:, :, None] == jnp.arange(N_EXPERTS, dtype=I32)[None, None, :]
    dest = rank + jnp.sum(jnp.where(onehot, pstart[None, None, :], 0), axis=-1)
    n_slots = nblk * MOE_BLOCK
    pad_len = padded - counts
    pad_end = jnp.cumsum(pad_len)
    base = jnp.concatenate([pstart + counts - (pad_end - pad_len), pend[-1:] - pad_end[-1:]])
    j = jnp.arange(n_slots - n * TOP_K, dtype=I32)
    group = jnp.sum((pad_end[None, :] <= j[:, None]).astype(I32), axis=1)
    group_hot = group[:, None] == jnp.arange(N_EXPERTS + 1, dtype=I32)[None, :]
    pad_slots = j + jnp.sum(jnp.where(group_hot, base[None, :], 0), axis=1)
    xs = _dispatch(dest, pad_slots, h2, n_slots)
    ys = _experts(blk_e, n_used, xs, w_e_in, b_e_in, w_e_out, b_e_out)
    return _combine(dest, x2, gate, ys)


def _layer(x, mem, g_mix, w_in, g_q_dsa, g_k_dsa, g_q_mem, g_k_mem, g_mem, w_mem_kv, w_gate, b_gate,
           w_branch, w_out, g_ffn, w_router, b_router, w_e_in, b_e_in, w_e_out, b_e_out):
    batch, seq, _ = x.shape
    n = batch * seq
    topk = min(DSA_TOPK_MAX, seq // 4)
    x2d = x.reshape(n, D_MODEL)
    tm_dense = min(DENSE_ROWS, seq)
    sq, sk, sv, dq, iq, kvi, iw, mq, vt = _inproj(x2d, batch, seq, g_mix, w_in, g_q_dsa, g_k_dsa, g_q_mem, tm_dense)
    o_sb = _sb_attention(sq, sk, sv, min(256, seq))
    o_dsa = _dsa_attention(dq, iq, iw, kvi, vt, 128, topk)
    mlen = mem.shape[1]
    mk, mv = _mem_kv(mem.reshape(batch * mlen, D_MODEL), g_mem, w_mem_kv, g_k_mem, min(512, batch * mlen))
    mw = MEM_HEADS * MEM_HEAD_DIM
    o_mem = _mem_attention(mq.reshape(batch, seq, mw), mk.reshape(batch, mlen, mw), mv.reshape(batch, mlen, mw), seq)
    x2, h2, top_idx, gate, rank, counts = _merge_route(
        x2d, g_mix, o_sb.reshape(n, BRANCH_W), o_dsa.reshape(n, BRANCH_W), o_mem.reshape(n, mw),
        w_gate, b_gate, w_branch, w_out, g_ffn, w_router, b_router, tm_dense)
    out = _moe(x2, h2, top_idx, gate, rank, counts, w_e_in, b_e_in, w_e_out, b_e_out)
    return out.reshape(batch, seq, D_MODEL)


def kernel(x, mem, g_mix, w_in, g_q_dsa, g_k_dsa, g_q_mem, g_k_mem, g_mem, w_mem_kv, w_gate, b_gate, w_branch, w_out, g_ffn, w_router, b_router, w_e_in, b_e_in, w_e_out, b_e_out):
    for l in range(g_mix.shape[0]):
        x = _layer(x, mem, g_mix[l], w_in[l], g_q_dsa[l], g_k_dsa[l], g_q_mem[l], g_k_mem[l], g_mem[l],
                   w_mem_kv[l], w_gate[l], b_gate[l], w_branch[l], w_out[l], g_ffn[l], w_router[l],
                   b_router[l], w_e_in[l], b_e_in[l], w_e_out[l], b_e_out[l])
    return x
```

```python
import functools

import numpy as np
import jax
import jax.numpy as jnp
from jax import lax
from jax.experimental import pallas as pl
from jax.experimental.pallas import tpu as pltpu

F32 = jnp.float32
BF16 = jnp.bfloat16
I32 = jnp.int32

D_MODEL = 1024
CHUNK = 64
SB_HEADS = 8
DSA_HEADS = 8
HEAD_DIM = 64
IDX_HEADS = 4
DSA_TOPK_MAX = 256
MEM_HEADS = 4
MEM_HEAD_DIM = 128
N_EXPERTS = 32
TOP_K = 4
D_EXPERT = D_MODEL
SWIGLU_LIMIT = 7.0
SWIGLU_ALPHA = 1.702
ROPE_THETA = 10000.0
EPS = 1e-6
MOE_BLOCK = 512

BRANCH_W = 512
IN_SIZES = (512, 512, 512, 512, 64, 64, 256, 64, 4, 512)
C_SQ, C_SK, C_SV, C_DQ, C_MQ, C_IQ, C_SMALL, C_END = 0, 512, 1024, 1536, 2048, 2560, 2816, 3072
IW_LANE = 64

LANES = 128
NEG_BIG = -1e30
SB_CUTOFF = 110.0
KEY_NEG_INF = int(np.array(-np.inf, np.float32).view(np.int32)) ^ 0x7FFFFFFF
INT_MIN = -(2 ** 31)

VMEM_LIMIT = 56 * 1024 * 1024
DENSE_ROWS = 1024


def _cparams(sem):
    return pltpu.CompilerParams(dimension_semantics=sem, vmem_limit_bytes=VMEM_LIMIT)


def _dot(a, b):
    return jnp.dot(a, b, preferred_element_type=F32)


def _dot_nt(a, b):
    return lax.dot_general(a, b, (((1,), (1,)), ((), ())), preferred_element_type=F32)


def _split_bf16(x):
    hi = x.astype(BF16)
    lo = (x - hi.astype(F32)).astype(BF16)
    return hi, lo


def _dot_split(x, m_bf16):
    hi, lo = _split_bf16(x)
    return _dot(hi, m_bf16) + _dot(lo, m_bf16)


TOKEN_TILE = D_MODEL // LANES


def _store_token_tiles(ref, y):
    rows = y.shape[0]
    for s in range(TOKEN_TILE):
        ref[pl.ds(s, rows, stride=TOKEN_TILE), :] = y[:, s * LANES:(s + 1) * LANES]


def _load_token_tiles(ref, start_row, rows, s):
    return ref[pl.ds(start_row * TOKEN_TILE + s, rows, stride=TOKEN_TILE), :]


def _rot_half_unsigned(y):
    w = y.shape[1]
    lane = lax.broadcasted_iota(I32, y.shape, 1)
    return jnp.where((lane & 32) == 0, pltpu.roll(y, w - 32, 1), pltpu.roll(y, 32, 1))


def _inproj_kernel(x_ref, g_ref, w_ref, wsvt_ref, cos_ref, sin_ref, coss_ref, sins_ref, gq_ref, gks_ref,
                   gm_ref, bd64_ref, bd128_ref,
                   sq_ref, sk_ref, sv_ref, dq_ref, iq_ref, kvi_ref, iw_ref, mq_ref, vt_ref):
    x = x_ref[...]
    ms = jnp.mean(x * x, axis=-1, keepdims=True)
    h = (x * lax.rsqrt(ms + EPS) * g_ref[...]).astype(BF16)

    def seg(a, b):
        return _dot(h, w_ref[:, a:b])

    def put_heads(ref, y):
        for hd in range(y.shape[1] // HEAD_DIM):
            ref[0, hd] = y[:, hd * HEAD_DIM:(hd + 1) * HEAD_DIM].astype(BF16)

    put_heads(sq_ref, seg(C_SQ, C_SK) * (HEAD_DIM ** -0.5))
    put_heads(sk_ref, seg(C_SK, C_SV))
    sv_ref[0] = _dot_nt(wsvt_ref[...], h).astype(BF16)

    y = seg(C_DQ, C_MQ)
    msq = _dot_split(y * y, bd64_ref[...]) * (1.0 / HEAD_DIM)
    y = y * lax.rsqrt(msq + EPS) * gq_ref[...]
    y = y * cos_ref[...] + _rot_half_unsigned(y) * sin_ref[...]
    put_heads(dq_ref, y * (HEAD_DIM ** -0.5))

    y = seg(C_IQ, C_SMALL)
    y = y * cos_ref[:, :256] + _rot_half_unsigned(y) * sin_ref[:, :256]
    iq_ref[...] = (y * (HEAD_DIM ** -0.5)).astype(BF16)

    y = seg(C_SMALL, C_END)
    lane = lax.broadcasted_iota(I32, y.shape, 1)
    is_k = lane < HEAD_DIM
    msk = jnp.sum(jnp.where(is_k, y * y, 0.0), axis=-1, keepdims=True) * (1.0 / HEAD_DIM)
    y = y * jnp.where(is_k, lax.rsqrt(msk + EPS) * gks_ref[...], 1.0)
    y = y * coss_ref[...] + _rot_half_unsigned(y) * sins_ref[...]
    kvi_ref[...] = y.astype(BF16)
    iw_ref[...] = y[:, 128:256]
    kv_t = y[:, 0:128].T
    row_t = lax.broadcasted_iota(I32, kv_t.shape, 0)
    vt_ref[0] = jnp.where(row_t < HEAD_DIM, 1.0, kv_t).astype(BF16)

    y = seg(C_MQ, C_IQ)
    msm = _dot_split(y * y, bd128_ref[...]) * (1.0 / MEM_HEAD_DIM)
    mq_ref[...] = (y * lax.rsqrt(msm + EPS) * gm_ref[...]).astype(BF16)


def _rope_tables(seq):
    half = HEAD_DIM // 2
    inv = ROPE_THETA ** (-jnp.arange(half, dtype=F32) / half)
    ang = jnp.arange(seq).astype(F32)[:, None] * inv[None, :]
    cos = jnp.cos(ang)
    sin = jnp.sin(ang)
    cos64 = jnp.concatenate([cos, cos], axis=1)
    sin64 = jnp.concatenate([-sin, sin], axis=1)
    one = jnp.ones_like(cos64)
    zero = jnp.zeros_like(cos64)
    cosq = jnp.tile(cos64, (1, 8))
    sinq = jnp.tile(sin64, (1, 8))
    coss = jnp.concatenate([cos64, one, cos64, one], axis=1)
    sins = jnp.concatenate([sin64, zero, sin64, zero], axis=1)
    return cosq, sinq, coss, sins


def _block_diag_ones(width, group):
    idx = np.arange(width) // group
    return jnp.asarray((idx[:, None] == idx[None, :]).astype(np.float32), dtype=BF16)


def _inproj(x2d, batch, seq, g_mix, w_in, g_q_dsa, g_k_dsa, g_q_mem, tm):
    n = x2d.shape[0]
    sizes = np.cumsum((0,) + IN_SIZES)
    col = {name: (int(sizes[i]), int(sizes[i + 1])) for i, name in enumerate(
        ("sq", "sk", "sv", "dq", "dk", "dv", "iq", "ik", "iw", "mq"))}
    order = ("sq", "sk", "sv", "dq", "mq", "iq", "dk", "dv", "ik", "iw")
    w = jnp.concatenate([w_in[:, col[k][0]:col[k][1]] for k in order]
                        + [jnp.zeros((D_MODEL, C_END - sum(IN_SIZES)), w_in.dtype)], axis=1).astype(BF16)
    cosq, sinq, coss, sins = (jnp.asarray(t) for t in _rope_tables(seq))
    gq = jnp.tile(g_q_dsa, 8)[None, :]
    gks = jnp.concatenate([g_k_dsa, jnp.ones((256 - HEAD_DIM,), F32)])[None, :]
    gm = jnp.tile(g_q_mem, MEM_HEADS)[None, :]
    spb = seq // tm
    row = lambda i: (i, 0)
    const = lambda i: (0, 0)
    pos = lambda i: (i % spb, 0)
    heads = lambda i: (i // spb, 0, i % spb, 0)
    head_shape = jax.ShapeDtypeStruct((batch, 8, seq, HEAD_DIM), BF16)
    head_spec = pl.BlockSpec((1, 8, tm, HEAD_DIM), heads)
    return pl.pallas_call(
        _inproj_kernel,
        grid=(n // tm,),
        in_specs=[
            pl.BlockSpec((tm, D_MODEL), row),
            pl.BlockSpec((1, D_MODEL), const),
            pl.BlockSpec((D_MODEL, C_END), const, pipeline_mode=pl.Buffered(1)),
            pl.BlockSpec((512, D_MODEL), const, pipeline_mode=pl.Buffered(1)),
            pl.BlockSpec((tm, 512), pos), pl.BlockSpec((tm, 512), pos),
            pl.BlockSpec((tm, 256), pos), pl.BlockSpec((tm, 256), pos),
            pl.BlockSpec((1, 512), const), pl.BlockSpec((1, 256), const), pl.BlockSpec((1, 512), const),
            pl.BlockSpec((512, 512), const), pl.BlockSpec((512, 512), const),
        ],
        out_specs=[head_spec, head_spec, pl.BlockSpec((1, 512, tm), lambda i: (i // spb, 0, i % spb)), head_spec,
                   pl.BlockSpec((tm, 256), row), pl.BlockSpec((tm, 256), row),
                   pl.BlockSpec((tm, 128), row), pl.BlockSpec((tm, 512), row),
                   pl.BlockSpec((1, 128, tm), lambda i: (i // spb, 0, i % spb))],
        out_shape=[head_shape, head_shape, jax.ShapeDtypeStruct((batch, 512, seq), BF16), head_shape,
                   jax.ShapeDtypeStruct((n, 256), BF16), jax.ShapeDtypeStruct((n, 256), BF16),
                   jax.ShapeDtypeStruct((n, 128), F32), jax.ShapeDtypeStruct((n, 512), BF16),
                   jax.ShapeDtypeStruct((batch, 128, seq), BF16)],
        compiler_params=_cparams(("parallel",)),
        name="inproj",
    )(x2d, g_mix[None, :], w, w_in[:, col["sv"][0]:col["sv"][1]].T.astype(BF16), cosq, sinq, coss, sins, gq, gks, gm,
      _block_diag_ones(512, HEAD_DIM), _block_diag_ones(512, MEM_HEAD_DIM))


def _sb_kernel(q_ref, k_ref, vt_ref, u_ref, o_ref, acc_ref, car_ref, *, tq):
    qi = pl.program_id(1)
    rows = lax.broadcasted_iota(I32, (tq, tq), 0)
    cols = lax.broadcasted_iota(I32, (tq, tq), 1)
    dif = rows - cols
    u = u_ref[...]
    acc_ref[...] = jnp.zeros_like(acc_ref)
    car_ref[...] = jnp.zeros_like(car_ref)

    def cond(c):
        kb, mx = c
        return jnp.logical_and(kb >= 0, mx > -SB_CUTOFF)

    def body(c):
        kb, _ = c
        ks = pl.multiple_of(kb * tq, tq)
        earlier = dif < (qi - kb) * tq
        neg_mask = jnp.where(earlier, -1.0, 0.0).astype(BF16)
        heads = range(SB_HEADS)
        z = [_dot_nt(k_ref[0, hd, pl.ds(ks, tq), :], q_ref[0, hd]) for hd in heads]
        ls, lk, between = [], [], []
        for hd in heads:
            zb = z[hd].astype(BF16)
            sp = jnp.maximum(zb, 0.0) + jnp.log(1.0 + jnp.exp(-jnp.abs(zb)))
            ls.append(z[hd] - sp.astype(F32))
            lk.append(sp * neg_mask)
            between.append(_dot(u, lk[hd]))
        for hd in heads:
            rs = slice(hd * HEAD_DIM, (hd + 1) * HEAD_DIM)
            car = car_ref[hd:hd + 1, :]
            w = jnp.where(earlier, jnp.exp(ls[hd] + between[hd] + car), 0.0)
            acc_ref[rs, :] += _dot(vt_ref[0, rs, pl.ds(ks, tq)], w.astype(BF16))
            car_ref[hd:hd + 1, :] = car + (between[hd][0:1, :] + lk[hd][0:1, :].astype(F32))
        return kb - 1, jnp.max(car_ref[...])

    lax.while_loop(cond, body, (qi, jnp.float32(0.0)))
    o_ref[0] = acc_ref[...].T.astype(BF16)


def _strict_tri(n, lower):
    i = np.arange(n)
    m = (i[:, None] > i[None, :]) if lower else (i[:, None] < i[None, :])
    return jnp.asarray(m.astype(np.float32), dtype=BF16)


def _sb_attention(sq, sk, svt, tq):
    batch, _, seq, _ = sq.shape
    return pl.pallas_call(
        functools.partial(_sb_kernel, tq=tq),
        grid=(batch, seq // tq),
        in_specs=[pl.BlockSpec((1, SB_HEADS, tq, HEAD_DIM), lambda b, i: (b, 0, i, 0)),
                  pl.BlockSpec((1, SB_HEADS, seq, HEAD_DIM), lambda b, i: (b, 0, 0, 0)),
                  pl.BlockSpec((1, BRANCH_W, seq), lambda b, i: (b, 0, 0)),
                  pl.BlockSpec((tq, tq), lambda b, i: (0, 0))],
        out_specs=pl.BlockSpec((1, tq, BRANCH_W), lambda b, i: (b, i, 0)),
        out_shape=jax.ShapeDtypeStruct((batch, seq, BRANCH_W), BF16),
        scratch_shapes=[pltpu.VMEM((BRANCH_W, tq), F32), pltpu.VMEM((SB_HEADS, tq), F32)],
        compiler_params=_cparams(("parallel", "parallel")),
        name="sb_attention",
    )(sq, sk, svt, _strict_tri(tq, lower=False))


DSA_SEG = 256
DSA_KB = DSA_SEG


def _tree_sum(parts):
    while len(parts) > 1:
        parts = [parts[i] + parts[i + 1] for i in range(0, len(parts) - 1, 2)] + ([parts[-1]] if len(parts) % 2 else [])
    return parts[0]


def _dsa_kernel(dq_ref, iq_ref, iw_ref, kvi_ref, vt_ref, tri_ref, o_ref,
                sc_ref, bias_ref, s_ref, *, tq, topk, nseg_max):
    nseg = (pl.program_id(1) * tq) // DSA_SEG + 1
    for ns in range(1, nseg_max + 1):
        @pl.when(nseg == ns)
        def _(ns=ns):
            _dsa_body(dq_ref, iq_ref, iw_ref, kvi_ref, vt_ref, tri_ref, o_ref, sc_ref, bias_ref, s_ref,
                      tq=tq, topk=topk, nseg=ns)


def _dsa_body(dq_ref, iq_ref, iw_ref, kvi_ref, vt_ref, tri_ref, o_ref, sc_ref, bias_ref, s_ref, *, tq, topk, nseg):
    qs = pl.program_id(1) * tq
    blocks = [slice(c * DSA_KB, (c + 1) * DSA_KB) for c in range(nseg)]
    iq = iq_ref[0]
    w_t = iw_ref[0].T
    w_row = [w_t[IW_LANE + h:IW_LANE + h + 1, :] * (IDX_HEADS ** -0.5) for h in range(IDX_HEADS)]
    q_chunk = (qs + lax.broadcasted_iota(I32, (DSA_KB, tq), 1)) // CHUNK
    k_chunk = lax.broadcasted_iota(I32, (DSA_KB, tq), 0) // CHUNK

    for c, blk in enumerate(blocks):
        ik = kvi_ref[0, blk, 128:192]
        lg = [_dot_nt(ik, iq[:, h * HEAD_DIM:(h + 1) * HEAD_DIM]) for h in range(IDX_HEADS)]
        sc = jnp.zeros((DSA_KB, tq), F32)
        for h in range(IDX_HEADS):
            sc = sc + w_row[h] * jnp.maximum(lg[h], 0.0)
        admissible = (c * (DSA_KB // CHUNK) + k_chunk) <= q_chunk
        sc_ref[blk, :] = jnp.where(admissible, sc, -jnp.inf)

    def count(pred_fn):
        sub, lanes_of_sums = 32, 4
        acc = [jnp.zeros((sub, tq), F32)] * lanes_of_sums
        for j in range(nseg * DSA_SEG // sub):
            acc[j % lanes_of_sums] = acc[j % lanes_of_sums] + jnp.where(
                pred_fn(sc_ref[j * sub:(j + 1) * sub, :]), 1.0, 0.0)
        return jnp.sum(_tree_sum(acc), axis=0, keepdims=True)

    def key_to_float(key):
        return lax.bitcast_convert_type(jnp.where(key >= 0, key, key ^ 0x7FFFFFFF), F32)

    kf = jnp.float32(topk)
    n_rows = jnp.float32(nseg * DSA_SEG)
    cnt0 = count(lambda s: s >= 0.0)
    t0 = jnp.where(cnt0 >= kf, 0, INT_MIN).astype(I32)
    cnt_t0 = jnp.where(cnt0 >= kf, cnt0, n_rows)

    def bit_step(i, carry):
        t, cnt_t = carry
        cand = t + lax.shift_left(jnp.int32(1), 30 - i)
        cand_f = key_to_float(cand)
        cnt = jnp.where(cand <= KEY_NEG_INF, n_rows, count(lambda s: s >= cand_f))
        take = cnt >= kf
        return jnp.where(take, cand, t), jnp.where(take, cnt, cnt_t)

    thr_key, cnt_thr = lax.fori_loop(0, 31, bit_step, (t0, cnt_t0))
    thr = jnp.where(thr_key <= KEY_NEG_INF, -jnp.inf, key_to_float(thr_key))
    tri = tri_ref[...]

    surplus = jnp.logical_or(jnp.max(cnt_thr) > kf, jnp.min(thr) == -jnp.inf)

    @pl.when(surplus)
    def _():
        need = kf - count(lambda s: s > thr)
        prefix = jnp.zeros((1, tq), F32)
        for blk in blocks:
            sc = sc_ref[blk, :]
            eqf = jnp.where(sc == thr, 1.0, 0.0)
            rank = _dot(tri, eqf.astype(BF16)) + prefix
            tie = jnp.where(rank < need, eqf, 0.0)
            sel = jnp.where(sc > thr, 1.0, tie)
            bias_ref[blk, :] = jnp.where(sc > -jnp.inf, (sel - 1.0) * (-NEG_BIG), NEG_BIG)
            prefix = prefix + jnp.sum(eqf, axis=0, keepdims=True)

    @pl.when(jnp.logical_not(surplus))
    def _():
        for blk in blocks:
            bias_ref[blk, :] = jnp.where(sc_ref[blk, :] >= thr, 0.0, NEG_BIG)

    q8 = dq_ref[0].reshape(DSA_HEADS * tq, HEAD_DIM)
    m = jnp.full((1, DSA_HEADS * tq), NEG_BIG, F32)
    for blk in blocks:
        b = bias_ref[blk, :]
        s = _dot_nt(kvi_ref[0, blk, 0:HEAD_DIM], q8) + jnp.concatenate([b] * DSA_HEADS, axis=1)
        s_ref[blk, :] = s
        m = jnp.maximum(m, jnp.max(s, axis=0, keepdims=True))
    acc = jnp.zeros((128, DSA_HEADS * tq), F32)
    for blk in blocks:
        p = jnp.exp(s_ref[blk, :] - m)
        acc = acc + _dot(vt_ref[0, :, blk], p.astype(BF16))
    for hd in range(DSA_HEADS):
        a = acc[:, hd * tq:(hd + 1) * tq]
        o = (a / a[0:1, :]).T
        o_ref[0, :, hd * HEAD_DIM:(hd + 1) * HEAD_DIM] = o[:, HEAD_DIM:].astype(BF16)


def _dsa_attention(dq, iq, iw, kvi, vt, tq, topk):
    batch, _, seq, _ = dq.shape
    return pl.pallas_call(
        functools.partial(_dsa_kernel, tq=tq, topk=topk, nseg_max=seq // DSA_SEG),
        grid=(batch, seq // tq),
        in_specs=[
            pl.BlockSpec((1, DSA_HEADS, tq, HEAD_DIM), lambda b, i: (b, 0, i, 0)),
            pl.BlockSpec((1, tq, 256), lambda b, i: (b, i, 0)),
            pl.BlockSpec((1, tq, 128), lambda b, i: (b, i, 0)),
            pl.BlockSpec((1, seq, 256), lambda b, i: (b, 0, 0)),
            pl.BlockSpec((1, 128, seq), lambda b, i: (b, 0, 0)),
            pl.BlockSpec((DSA_KB, DSA_KB), lambda b, i: (0, 0)),
        ],
        out_specs=pl.BlockSpec((1, tq, BRANCH_W), lambda b, i: (b, i, 0)),
        out_shape=jax.ShapeDtypeStruct((batch, seq, BRANCH_W), BF16),
        scratch_shapes=[pltpu.VMEM((seq, tq), F32), pltpu.VMEM((seq, tq), F32),
                        pltpu.VMEM((seq, DSA_HEADS * tq), F32)],
        compiler_params=_cparams(("parallel", "parallel")),
        name="dsa_attention",
    )(dq, iq.reshape(batch, seq, 256), iw.reshape(batch, seq, 128), kvi.reshape(batch, seq, 256), vt,
      _strict_tri(DSA_KB, lower=True))


def _memkv_kernel(m_ref, g_ref, w_ref, gk_ref, bd_ref, mk_ref, mv_ref):
    x = m_ref[...]
    ms = jnp.mean(x * x, axis=-1, keepdims=True)
    h = (x * lax.rsqrt(ms + EPS) * g_ref[...]).astype(BF16)
    mw = MEM_HEADS * MEM_HEAD_DIM
    k = _dot(h, w_ref[:, :mw])
    msk = _dot_split(k * k, bd_ref[...]) * (1.0 / MEM_HEAD_DIM)
    mk_ref[...] = (k * lax.rsqrt(msk + EPS) * gk_ref[...]).astype(BF16)
    mv_ref[...] = _dot(h, w_ref[:, mw:]).astype(BF16)


def _mem_kv(mem2d, g_mem, w_mem_kv, g_k_mem, tm):
    n = mem2d.shape[0]
    mw = MEM_HEADS * MEM_HEAD_DIM
    row = lambda i: (i, 0)
    const = lambda i: (0, 0)
    return pl.pallas_call(
        _memkv_kernel,
        grid=(n // tm,),
        in_specs=[pl.BlockSpec((tm, D_MODEL), row), pl.BlockSpec((1, D_MODEL), const),
                  pl.BlockSpec((D_MODEL, 2 * mw), const), pl.BlockSpec((1, mw), const),
                  pl.BlockSpec((mw, mw), const)],
        out_specs=[pl.BlockSpec((tm, mw), row), pl.BlockSpec((tm, mw), row)],
        out_shape=[jax.ShapeDtypeStruct((n, mw), BF16), jax.ShapeDtypeStruct((n, mw), BF16)],
        compiler_params=_cparams(("parallel",)),
        name="mem_kv",
    )(mem2d, g_mem[None, :], w_mem_kv.astype(BF16), jnp.tile(g_k_mem, MEM_HEADS)[None, :],
      _block_diag_ones(mw, MEM_HEAD_DIM))


def _memattn_kernel(q_ref, k_ref, v_ref, o_ref):
    for hd in range(MEM_HEADS):
        sl = slice(hd * MEM_HEAD_DIM, (hd + 1) * MEM_HEAD_DIM)
        s = _dot_nt(q_ref[0, :, sl], k_ref[0, :, sl]) * (MEM_HEAD_DIM ** -0.5)
        p = jnp.exp(s - jnp.max(s, axis=-1, keepdims=True))
        o = _dot(p.astype(BF16), v_ref[0, :, sl]) / jnp.sum(p, axis=-1, keepdims=True)
        o_ref[0, :, sl] = o.astype(BF16)


def _mem_attention(mq, mk, mv, tq):
    batch, seq, mw = mq.shape
    mlen = mk.shape[1]
    kspec = pl.BlockSpec((1, mlen, mw), lambda b, i: (b, 0, 0))
    return pl.pallas_call(
        _memattn_kernel,
        grid=(batch, seq // tq),
        in_specs=[pl.BlockSpec((1, tq, mw), lambda b, i: (b, i, 0)), kspec, kspec],
        out_specs=pl.BlockSpec((1, tq, mw), lambda b, i: (b, i, 0)),
        out_shape=jax.ShapeDtypeStruct((batch, seq, mw), BF16),
        compiler_params=_cparams(("parallel", "parallel")),
        name="mem_attention",
    )(mq, mk, mv)


def _merge_kernel(x_ref, g_ref, osb_ref, odsa_ref, omem_ref, wg_ref, bg_ref, wb_ref, wo_ref,
                  gf_ref, wrh_ref, wrl_ref, br_ref, tri_ref,
                  x2_ref, h2_ref, idx_ref, gate_ref, rank_ref, cnt_ref, carry_ref):
    x = x_ref[...]
    ms = jnp.mean(x * x, axis=-1, keepdims=True)
    h = (x * lax.rsqrt(ms + EPS) * g_ref[...]).astype(BF16)
    merged = None
    for n, o_ref in enumerate((osb_ref, odsa_ref, omem_ref)):
        gate = jax.nn.sigmoid(_dot(h, wg_ref[n]) + bg_ref[n])
        term = gate * _dot(o_ref[...], wb_ref[n])
        merged = term if merged is None else merged + term
    x2 = x + _dot(merged.astype(BF16), wo_ref[...])
    x2_ref[...] = x2

    ms2 = jnp.mean(x2 * x2, axis=-1, keepdims=True)
    h2 = x2 * lax.rsqrt(ms2 + EPS) * gf_ref[...]
    _store_token_tiles(h2_ref, h2)
    hi, lo = _split_bf16(h2)
    logits = _dot(hi, wrh_ref[...]) + _dot(hi, wrl_ref[...]) + _dot(lo, wrh_ref[...]) + br_ref[...]

    lane = lax.broadcasted_iota(I32, logits.shape, 1)
    vals = logits
    top_v, top_i = [], []
    for _ in range(TOP_K):
        mx = jnp.max(vals, axis=-1, keepdims=True)
        ix = jnp.min(jnp.where(vals == mx, lane, N_EXPERTS), axis=-1, keepdims=True)
        top_v.append(mx)
        top_i.append(ix)
        vals = jnp.where(lane == ix, -jnp.inf, vals)
    ex = [jnp.exp(v - top_v[0]) for v in top_v]
    den = ex[0] + ex[1] + ex[2] + ex[3]
    for k in range(TOP_K):
        idx_ref[:, k:k + 1] = top_i[k]
        gate_ref[:, k:k + 1] = ex[k] / den

    @pl.when(pl.program_id(0) == 0)
    def _():
        carry_ref[...] = jnp.zeros_like(carry_ref)

    lane_e = lax.broadcasted_iota(I32, (x.shape[0], LANES), 1)
    hits = [lane_e == top_i[k] for k in range(TOP_K)]
    onehot = sum(jnp.where(hk, 1.0, 0.0) for hk in hits)
    before = _dot(tri_ref[...], onehot.astype(BF16)) + carry_ref[...]
    for k in range(TOP_K):
        rank_ref[:, k:k + 1] = jnp.sum(jnp.where(hits[k], before, 0.0), axis=-1, keepdims=True).astype(I32)
    carry_ref[...] += jnp.sum(onehot, axis=0, keepdims=True)
    cnt_ref[...] = carry_ref[...]


def _merge_route(x2d, g_mix, o_sb, o_dsa, o_mem, w_gate, b_gate, w_branch, w_out, g_ffn, w_router, b_router, tm):
    n = x2d.shape[0]
    row = lambda i: (i, 0)
    c2 = lambda i: (0, 0)
    c3 = lambda i: (0, 0, 0)
    wr_hi = w_router.astype(BF16)
    wr_lo = (w_router - wr_hi.astype(F32)).astype(BF16)
    return pl.pallas_call(
        _merge_kernel,
        grid=(n // tm,),
        in_specs=[pl.BlockSpec((tm, D_MODEL), row), pl.BlockSpec((1, D_MODEL), c2),
                  pl.BlockSpec((tm, BRANCH_W), row), pl.BlockSpec((tm, BRANCH_W), row),
                  pl.BlockSpec((tm, BRANCH_W), row),
                  pl.BlockSpec((3, D_MODEL, D_MODEL), c3, pipeline_mode=pl.Buffered(1)),
                  pl.BlockSpec((3, 1, D_MODEL), c3),
                  pl.BlockSpec((3, BRANCH_W, D_MODEL), c3, pipeline_mode=pl.Buffered(1)),
                  pl.BlockSpec((D_MODEL, D_MODEL), c2, pipeline_mode=pl.Buffered(1)),
                  pl.BlockSpec((1, D_MODEL), c2), pl.BlockSpec((D_MODEL, N_EXPERTS), c2),
                  pl.BlockSpec((D_MODEL, N_EXPERTS), c2), pl.BlockSpec((1, N_EXPERTS), c2),
                  pl.BlockSpec((tm, tm), c2, pipeline_mode=pl.Buffered(1))],
        out_specs=[pl.BlockSpec((tm, D_MODEL), row), pl.BlockSpec((tm * TOKEN_TILE, LANES), row),
                   pl.BlockSpec((tm, TOP_K), row), pl.BlockSpec((tm, TOP_K), row),
                   pl.BlockSpec((tm, TOP_K), row), pl.BlockSpec((1, LANES), c2)],
        out_shape=[jax.ShapeDtypeStruct((n, D_MODEL), F32), jax.ShapeDtypeStruct((n * TOKEN_TILE, LANES), F32),
                   jax.ShapeDtypeStruct((n, TOP_K), I32), jax.ShapeDtypeStruct((n, TOP_K), F32),
                   jax.ShapeDtypeStruct((n, TOP_K), I32), jax.ShapeDtypeStruct((1, LANES), F32)],
        scratch_shapes=[pltpu.VMEM((1, LANES), F32)],
        compiler_params=_cparams(("arbitrary",)),
        name="merge_route",
    )(x2d, g_mix[None, :], o_sb, o_dsa, o_mem, w_gate.astype(BF16), b_gate[:, None, :],
      w_branch.astype(BF16), w_out.astype(BF16), g_ffn[None, :], wr_hi, wr_lo, b_router[None, :],
      _strict_tri(tm, lower=True))


DISPATCH_TOKENS = 2048
COMBINE_TOKENS = 512


def _dispatch_kernel(dest_ref, pad_ref, h_ref, xs_ref, zero_ref, sem, *, n_tok, n_pad):
    n_copies = n_tok * TOP_K + n_pad
    zero_ref[...] = jnp.zeros_like(zero_ref)

    def fill(j, c):
        for r in range(2):
            dst = pl.multiple_of(pad_ref[0, 0, 2 * j + r] * TOKEN_TILE, TOKEN_TILE)
            pltpu.make_async_copy(zero_ref, xs_ref.at[pl.ds(dst, TOKEN_TILE)], sem).start(priority=r)
        return c

    lax.fori_loop(0, n_pad // 2, fill, 0, unroll=2)

    def start(t, c):
        src = pl.multiple_of(t * TOKEN_TILE, TOKEN_TILE)
        for k in range(TOP_K):
            dst = pl.multiple_of(dest_ref[0, 0, t * TOP_K + k] * TOKEN_TILE, TOKEN_TILE)
            pltpu.make_async_copy(h_ref.at[pl.ds(src, TOKEN_TILE)], xs_ref.at[pl.ds(dst, TOKEN_TILE)],
                                  sem).start(priority=k % 2)
        return c

    lax.fori_loop(0, n_tok, start, 0, unroll=2)
    total = n_copies * TOKEN_TILE
    pltpu.make_async_copy(xs_ref.at[pl.ds(0, total)], xs_ref.at[pl.ds(0, total)], sem).wait()


def _dispatch(dest, pad_slots, h2t, n_slots):
    n = h2t.shape[0] // TOKEN_TILE
    n_tok = min(DISPATCH_TOKENS, n)
    steps = n // n_tok
    per = n_tok * TOP_K
    n_pad = pad_slots.shape[0] // steps
    assert n_tok * steps == n and n_pad * steps == pad_slots.shape[0] and n_pad % 2 == 0
    return pl.pallas_call(
        functools.partial(_dispatch_kernel, n_tok=n_tok, n_pad=n_pad),
        grid=(steps,),
        in_specs=[pl.BlockSpec((1, 1, per), lambda i: (i, 0, 0), memory_space=pltpu.SMEM),
                  pl.BlockSpec((1, 1, n_pad), lambda i: (i, 0, 0), memory_space=pltpu.SMEM),
                  pl.BlockSpec((n_tok * TOKEN_TILE, LANES), lambda i: (i, 0))],
        out_specs=pl.BlockSpec(memory_space=pl.ANY),
        out_shape=jax.ShapeDtypeStruct((n_slots * TOKEN_TILE, LANES), F32),
        scratch_shapes=[pltpu.VMEM((TOKEN_TILE, LANES), F32), pltpu.SemaphoreType.DMA(())],
        compiler_params=_cparams(("arbitrary",)),
        name="moe_dispatch",
    )(dest.reshape(steps, 1, per), pad_slots.reshape(steps, 1, n_pad), h2t)


def _expert_kernel(blk_e_ref, nused_ref, x_ref, w1_ref, b1_ref, w2_ref, b2_ref, y_ref, xb_ref, w1b_ref, w2b_ref):
    i = pl.program_id(0)
    new_expert = jnp.logical_or(i == 0, blk_e_ref[i] != blk_e_ref[jnp.maximum(i - 1, 0)])

    @pl.when(jnp.logical_and(i < nused_ref[0], new_expert))
    def _():
        w1b_ref[...] = w1_ref[0].astype(BF16)
        w2b_ref[...] = w2_ref[0].astype(BF16)

    @pl.when(i < nused_ref[0])
    def _():
        for s in range(TOKEN_TILE):
            xb_ref[:, s * LANES:(s + 1) * LANES] = _load_token_tiles(x_ref, 0, MOE_BLOCK, s).astype(BF16)
        hb = _dot(xb_ref[...], w1b_ref[...]) + b1_ref[0]
        g = jnp.minimum(hb[:, :D_EXPERT], SWIGLU_LIMIT)
        u = jnp.clip(hb[:, D_EXPERT:], -SWIGLU_LIMIT, SWIGLU_LIMIT)
        act = (u + 1.0) * (g * jax.nn.sigmoid(SWIGLU_ALPHA * g))
        _store_token_tiles(y_ref, _dot(act.astype(BF16), w2b_ref[...]) + b2_ref[0])

    @pl.when(pl.program_id(0) >= nused_ref[0])
    def _():
        y_ref[...] = jnp.zeros_like(y_ref)


def _experts(blk_e, n_used, xs, w_e_in, b_e_in, w_e_out, b_e_out):
    n_slots = xs.shape[0] // TOKEN_TILE
    nblk = n_slots // MOE_BLOCK
    slot_block = pl.BlockSpec((MOE_BLOCK * TOKEN_TILE, LANES), lambda i, be, nu: (i, 0))
    grid_spec = pltpu.PrefetchScalarGridSpec(
        num_scalar_prefetch=2,
        grid=(nblk,),
        in_specs=[slot_block,
                  pl.BlockSpec((1, D_MODEL, 2 * D_EXPERT), lambda i, be, nu: (be[i], 0, 0)),
                  pl.BlockSpec((1, 1, 2 * D_EXPERT), lambda i, be, nu: (be[i], 0, 0)),
                  pl.BlockSpec((1, D_EXPERT, D_MODEL), lambda i, be, nu: (be[i], 0, 0)),
                  pl.BlockSpec((1, 1, D_MODEL), lambda i, be, nu: (be[i], 0, 0))],
        out_specs=slot_block,
        scratch_shapes=[pltpu.VMEM((MOE_BLOCK, D_MODEL), BF16), pltpu.VMEM((D_MODEL, 2 * D_EXPERT), BF16),
                        pltpu.VMEM((D_EXPERT, D_MODEL), BF16)],
    )
    return pl.pallas_call(
        _expert_kernel,
        grid_spec=grid_spec,
        out_shape=jax.ShapeDtypeStruct((n_slots * TOKEN_TILE, LANES), F32),
        compiler_params=_cparams(("arbitrary",)),
        name="moe_experts",
    )(blk_e, n_used, xs, w_e_in, b_e_in[:, None, :], w_e_out, b_e_out[:, None, :])


def _combine_kernel(dest_ref, dest_next_ref, x_ref, gate_ref, y_ref, o_ref, buf_ref, sem):
    n_copies = COMBINE_TOKENS * TOP_K
    i = pl.program_id(0)
    slot = i % 2

    def start_gathers(idx_ref, buf, dma_sem):
        def start(t, c):
            for k in range(TOP_K):
                src = pl.multiple_of(idx_ref[0, 0, t * TOP_K + k] * TOKEN_TILE, TOKEN_TILE)
                dst = pl.multiple_of((k * COMBINE_TOKENS + t) * TOKEN_TILE, TOKEN_TILE)
                pltpu.make_async_copy(y_ref.at[pl.ds(src, TOKEN_TILE)], buf.at[pl.ds(dst, TOKEN_TILE)],
                                      dma_sem).start(priority=k % 2)
            return c
        lax.fori_loop(0, COMBINE_TOKENS, start, 0, unroll=2)

    @pl.when(i == 0)
    def _():
        start_gathers(dest_ref, buf_ref.at[0], sem.at[0])

    @pl.when(i + 1 < pl.num_programs(0))
    def _():
        start_gathers(dest_next_ref, buf_ref.at[1 - slot], sem.at[1 - slot])

    buf = buf_ref.at[slot]
    pltpu.make_async_copy(y_ref.at[pl.ds(0, n_copies * TOKEN_TILE)], buf, sem.at[slot]).wait()
    rows = 32
    for r0 in range(0, COMBINE_TOKENS, rows):
        gate = gate_ref[r0:r0 + rows, :]
        gates = [jnp.broadcast_to(gate[:, k:k + 1], (rows, LANES)) for k in range(TOP_K)]
        for s in range(TOKEN_TILE):
            out = x_ref[r0:r0 + rows, s * LANES:(s + 1) * LANES]
            for k in range(TOP_K):
                out = out + gates[k] * _load_token_tiles(buf, k * COMBINE_TOKENS + r0, rows, s)
            o_ref[r0:r0 + rows, s * LANES:(s + 1) * LANES] = out


def _combine(dest, x2, gate, ys):
    n = x2.shape[0]
    steps = n // COMBINE_TOKENS
    per = COMBINE_TOKENS * TOP_K
    dest3 = dest.reshape(steps, 1, per)
    return pl.pallas_call(
        _combine_kernel,
        grid=(steps,),
        in_specs=[pl.BlockSpec((1, 1, per), lambda i: (i, 0, 0), memory_space=pltpu.SMEM),
                  pl.BlockSpec((1, 1, per), lambda i: (jnp.minimum(i + 1, steps - 1), 0, 0), memory_space=pltpu.SMEM),
                  pl.BlockSpec((COMBINE_TOKENS, D_MODEL), lambda i: (i, 0)),
                  pl.BlockSpec((COMBINE_TOKENS, TOP_K), lambda i: (i, 0)),
                  pl.BlockSpec(memory_space=pl.ANY)],
        out_specs=pl.BlockSpec((COMBINE_TOKENS, D_MODEL), lambda i: (i, 0)),
        out_shape=jax.ShapeDtypeStruct((n, D_MODEL), F32),
        scratch_shapes=[pltpu.VMEM((2, TOP_K * COMBINE_TOKENS * TOKEN_TILE, LANES), F32),
                        pltpu.SemaphoreType.DMA((2,))],
        compiler_params=_cparams(("arbitrary",)),
        name="moe_combine",
    )(dest3, dest3, x2, gate, ys)


def _moe(x2, h2, top_idx, gate, rank, counts, w_e_in, b_e_in, w_e_out, b_e_out):
    n = x2.shape[0]
    counts = counts[0, :N_EXPERTS].astype(I32)
    padded = (counts + MOE_BLOCK - 1) // MOE_BLOCK * MOE_BLOCK
    pend = jnp.cumsum(padded)
    pstart = pend - padded
    nblk = -(-(n * TOP_K) // MOE_BLOCK) + N_EXPERTS
    blk_start = jnp.arange(nblk, dtype=I32) * MOE_BLOCK
    blk_e = jnp.minimum(jnp.sum((pend[None, :] <= blk_start[:, None]).astype(I32), axis=1), N_EXPERTS - 1)
    n_used = (pend[-1:] // MOE_BLOCK).astype(I32)
    onehot = top_idx[:, :, None] == jnp.arange(N_EXPERTS, dtype=I32)[None, None, :]
    dest = rank + jnp.sum(jnp.where(onehot, pstart[None, None, :], 0), axis=-1)
    n_slots = nblk * MOE_BLOCK
    pad_len = padded - counts
    pad_end = jnp.cumsum(pad_len)
    base = jnp.concatenate([pstart + counts - (pad_end - pad_len), pend[-1:] - pad_end[-1:]])
    j = jnp.arange(n_slots - n * TOP_K, dtype=I32)
    group = jnp.sum((pad_end[None, :] <= j[:, None]).astype(I32), axis=1)
    group_hot = group[:, None] == jnp.arange(N_EXPERTS + 1, dtype=I32)[None, :]
    pad_slots = j + jnp.sum(jnp.where(group_hot, base[None, :], 0), axis=1)
    xs = _dispatch(dest, pad_slots, h2, n_slots)
    ys = _experts(blk_e, n_used, xs, w_e_in, b_e_in, w_e_out, b_e_out)
    return _combine(dest, x2, gate, ys)


def _layer(x, mem, g_mix, w_in, g_q_dsa, g_k_dsa, g_q_mem, g_k_mem, g_mem, w_mem_kv, w_gate, b_gate,
           w_branch, w_out, g_ffn, w_router, b_router, w_e_in, b_e_in, w_e_out, b_e_out):
    batch, seq, _ = x.shape
    n = batch * seq
    topk = min(DSA_TOPK_MAX, seq // 4)
    x2d = x.reshape(n, D_MODEL)
    tm_dense = min(DENSE_ROWS, seq)
    sq, sk, sv, dq, iq, kvi, iw, mq, vt = _inproj(x2d, batch, seq, g_mix, w_in, g_q_dsa, g_k_dsa, g_q_mem, tm_dense)
    o_sb = _sb_attention(sq, sk, sv, min(256, seq))
    o_dsa = _dsa_attention(dq, iq, iw, kvi, vt, 128, topk)
    mlen = mem.shape[1]
    mk, mv = _mem_kv(mem.reshape(batch * mlen, D_MODEL), g_mem, w_mem_kv, g_k_mem, min(512, batch * mlen))
    mw = MEM_HEADS * MEM_HEAD_DIM
    o_mem = _mem_attention(mq.reshape(batch, seq, mw), mk.reshape(batch, mlen, mw), mv.reshape(batch, mlen, mw), seq)
    x2, h2, top_idx, gate, rank, counts = _merge_route(
        x2d, g_mix, o_sb.reshape(n, BRANCH_W), o_dsa.reshape(n, BRANCH_W), o_mem.reshape(n, mw),
        w_gate, b_gate, w_branch, w_out, g_ffn, w_router, b_router, tm_dense)
    out = _moe(x2, h2, top_idx, gate, rank, counts, w_e_in, b_e_in, w_e_out, b_e_out)
    return out.reshape(batch, seq, D_MODEL)


def kernel(x, mem, g_mix, w_in, g_q_dsa, g_k_dsa, g_q_mem, g_k_mem, g_mem, w_mem_kv, w_gate, b_gate, w_branch, w_out, g_ffn, w_router, b_router, w_e_in, b_e_in, w_e_out, b_e_out):
    for l in range(g_mix.shape[0]):
        x = _layer(x, mem, g_mix[l], w_in[l], g_q_dsa[l], g_k_dsa[l], g_q_mem[l], g_k_mem[l], g_mem[l],
                   w_mem_kv[l], w_gate[l], b_gate[l], w_branch[l], w_out[l], g_ffn[l], w_router[l],
                   b_router[l], w_e_in[l], b_e_in[l], w_e_out[l], b_e_out[l])
    return x
```

```python
import functools

import numpy as np
import jax
import jax.numpy as jnp
from jax import lax
from jax.experimental import pallas as pl
from jax.experimental.pallas import tpu as pltpu

F32 = jnp.float32
BF16 = jnp.bfloat16
I32 = jnp.int32

D_MODEL = 1024
CHUNK = 64
SB_HEADS = 8
DSA_HEADS = 8
HEAD_DIM = 64
IDX_HEADS = 4
DSA_TOPK_MAX = 256
MEM_HEADS = 4
MEM_HEAD_DIM = 128
N_EXPERTS = 32
TOP_K = 4
D_EXPERT = D_MODEL
SWIGLU_LIMIT = 7.0
SWIGLU_ALPHA = 1.702
ROPE_THETA = 10000.0
EPS = 1e-6
MOE_BLOCK = 512

BRANCH_W = 512
IN_SIZES = (512, 512, 512, 512, 64, 64, 256, 64, 4, 512)
C_SQ, C_SK, C_SV, C_DQ, C_MQ, C_IQ, C_SMALL, C_END = 0, 512, 1024, 1536, 2048, 2560, 2816, 3072
IW_LANE = 64

LANES = 128
NEG_BIG = -1e30
SB_CUTOFF = 110.0
KEY_NEG_INF = int(np.array(-np.inf, np.float32).view(np.int32)) ^ 0x7FFFFFFF
INT_MIN = -(2 ** 31)

VMEM_LIMIT = 56 * 1024 * 1024
DENSE_ROWS = 1024


def _cparams(sem):
    return pltpu.CompilerParams(dimension_semantics=sem, vmem_limit_bytes=VMEM_LIMIT)


def _dot(a, b):
    return jnp.dot(a, b, preferred_element_type=F32)


def _dot_nt(a, b):
    return lax.dot_general(a, b, (((1,), (1,)), ((), ())), preferred_element_type=F32)


def _split_bf16(x):
    hi = x.astype(BF16)
    lo = (x - hi.astype(F32)).astype(BF16)
    return hi, lo


def _dot_split(x, m_bf16):
    hi, lo = _split_bf16(x)
    return _dot(hi, m_bf16) + _dot(lo, m_bf16)


TOKEN_TILE = D_MODEL // LANES


def _store_token_tiles(ref, y):
    rows = y.shape[0]
    for s in range(TOKEN_TILE):
        ref[pl.ds(s, rows, stride=TOKEN_TILE), :] = y[:, s * LANES:(s + 1) * LANES]


def _load_token_tiles(ref, start_row, rows, s):
    return ref[pl.ds(start_row * TOKEN_TILE + s, rows, stride=TOKEN_TILE), :]


def _rot_half_unsigned(y):
    w = y.shape[1]
    lane = lax.broadcasted_iota(I32, y.shape, 1)
    return jnp.where((lane & 32) == 0, pltpu.roll(y, w - 32, 1), pltpu.roll(y, 32, 1))


def _inproj_kernel(x_ref, g_ref, w_ref, wsvt_ref, cos_ref, sin_ref, coss_ref, sins_ref, gq_ref, gks_ref,
                   gm_ref, bd64_ref, bd128_ref,
                   sq_ref, sk_ref, sv_ref, dq_ref, iq_ref, kvi_ref, iw_ref, mq_ref, vt_ref):
    x = x_ref[...]
    ms = jnp.mean(x * x, axis=-1, keepdims=True)
    h = (x * lax.rsqrt(ms + EPS) * g_ref[...]).astype(BF16)

    def seg(a, b):
        return _dot(h, w_ref[:, a:b])

    def put_heads(ref, y):
        for hd in range(y.shape[1] // HEAD_DIM):
            ref[0, hd] = y[:, hd * HEAD_DIM:(hd + 1) * HEAD_DIM].astype(BF16)

    put_heads(sq_ref, seg(C_SQ, C_SK) * (HEAD_DIM ** -0.5))
    put_heads(sk_ref, seg(C_SK, C_SV))
    sv_ref[0] = _dot_nt(wsvt_ref[...], h).astype(BF16)

    y = seg(C_DQ, C_MQ)
    msq = _dot_split(y * y, bd64_ref[...]) * (1.0 / HEAD_DIM)
    y = y * lax.rsqrt(msq + EPS) * gq_ref[...]
    y = y * cos_ref[...] + _rot_half_unsigned(y) * sin_ref[...]
    put_heads(dq_ref, y * (HEAD_DIM ** -0.5))

    y = seg(C_IQ, C_SMALL)
    y = y * cos_ref[:, :256] + _rot_half_unsigned(y) * sin_ref[:, :256]
    iq_ref[...] = (y * (HEAD_DIM ** -0.5)).astype(BF16)

    y = seg(C_SMALL, C_END)
    lane = lax.broadcasted_iota(I32, y.shape, 1)
    is_k = lane < HEAD_DIM
    msk = jnp.sum(jnp.where(is_k, y * y, 0.0), axis=-1, keepdims=True) * (1.0 / HEAD_DIM)
    y = y * jnp.where(is_k, lax.rsqrt(msk + EPS) * gks_ref[...], 1.0)
    y = y * coss_ref[...] + _rot_half_unsigned(y) * sins_ref[...]
    kvi_ref[...] = y.astype(BF16)
    iw_ref[...] = y[:, 128:256]
    kv_t = y[:, 0:128].T
    row_t = lax.broadcasted_iota(I32, kv_t.shape, 0)
    vt_ref[0] = jnp.where(row_t < HEAD_DIM, 1.0, kv_t).astype(BF16)

    y = seg(C_MQ, C_IQ)
    msm = _dot_split(y * y, bd128_ref[...]) * (1.0 / MEM_HEAD_DIM)
    mq_ref[...] = (y * lax.rsqrt(msm + EPS) * gm_ref[...]).astype(BF16)


def _rope_tables(seq):
    half = HEAD_DIM // 2
    inv = ROPE_THETA ** (-jnp.arange(half, dtype=F32) / half)
    ang = jnp.arange(seq).astype(F32)[:, None] * inv[None, :]
    cos = jnp.cos(ang)
    sin = jnp.sin(ang)
    cos64 = jnp.concatenate([cos, cos], axis=1)
    sin64 = jnp.concatenate([-sin, sin], axis=1)
    one = jnp.ones_like(cos64)
    zero = jnp.zeros_like(cos64)
    cosq = jnp.tile(cos64, (1, 8))
    sinq = jnp.tile(sin64, (1, 8))
    coss = jnp.concatenate([cos64, one, cos64, one], axis=1)
    sins = jnp.concatenate([sin64, zero, sin64, zero], axis=1)
    return cosq, sinq, coss, sins


def _block_diag_ones(width, group):
    idx = np.arange(width) // group
    return jnp.asarray((idx[:, None] == idx[None, :]).astype(np.float32), dtype=BF16)


def _inproj(x2d, batch, seq, g_mix, w_in, g_q_dsa, g_k_dsa, g_q_mem, tm):
    n = x2d.shape[0]
    sizes = np.cumsum((0,) + IN_SIZES)
    col = {name: (int(sizes[i]), int(sizes[i + 1])) for i, name in enumerate(
        ("sq", "sk", "sv", "dq", "dk", "dv", "iq", "ik", "iw", "mq"))}
    order = ("sq", "sk", "sv", "dq", "mq", "iq", "dk", "dv", "ik", "iw")
    w = jnp.concatenate([w_in[:, col[k][0]:col[k][1]] for k in order]
                        + [jnp.zeros((D_MODEL, C_END - sum(IN_SIZES)), w_in.dtype)], axis=1).astype(BF16)
    cosq, sinq, coss, sins = (jnp.asarray(t) for t in _rope_tables(seq))
    gq = jnp.tile(g_q_dsa, 8)[None, :]
    gks = jnp.concatenate([g_k_dsa, jnp.ones((256 - HEAD_DIM,), F32)])[None, :]
    gm = jnp.tile(g_q_mem, MEM_HEADS)[None, :]
    spb = seq // tm
    row = lambda i: (i, 0)
    const = lambda i: (0, 0)
    pos = lambda i: (i % spb, 0)
    heads = lambda i: (i // spb, 0, i % spb, 0)
    head_shape = jax.ShapeDtypeStruct((batch, 8, seq, HEAD_DIM), BF16)
    head_spec = pl.BlockSpec((1, 8, tm, HEAD_DIM), heads)
    return pl.pallas_call(
        _inproj_kernel,
        grid=(n // tm,),
        in_specs=[
            pl.BlockSpec((tm, D_MODEL), row),
            pl.BlockSpec((1, D_MODEL), const),
            pl.BlockSpec((D_MODEL, C_END), const, pipeline_mode=pl.Buffered(1)),
            pl.BlockSpec((512, D_MODEL), const, pipeline_mode=pl.Buffered(1)),
            pl.BlockSpec((tm, 512), pos), pl.BlockSpec((tm, 512), pos),
            pl.BlockSpec((tm, 256), pos), pl.BlockSpec((tm, 256), pos),
            pl.BlockSpec((1, 512), const), pl.BlockSpec((1, 256), const), pl.BlockSpec((1, 512), const),
            pl.BlockSpec((512, 512), const), pl.BlockSpec((512, 512), const),
        ],
        out_specs=[head_spec, head_spec, pl.BlockSpec((1, 512, tm), lambda i: (i // spb, 0, i % spb)), head_spec,
                   pl.BlockSpec((tm, 256), row), pl.BlockSpec((tm, 256), row),
                   pl.BlockSpec((tm, 128), row), pl.BlockSpec((tm, 512), row),
                   pl.BlockSpec((1, 128, tm), lambda i: (i // spb, 0, i % spb))],
        out_shape=[head_shape, head_shape, jax.ShapeDtypeStruct((batch, 512, seq), BF16), head_shape,
                   jax.ShapeDtypeStruct((n, 256), BF16), jax.ShapeDtypeStruct((n, 256), BF16),
                   jax.ShapeDtypeStruct((n, 128), F32), jax.ShapeDtypeStruct((n, 512), BF16),
                   jax.ShapeDtypeStruct((batch, 128, seq), BF16)],
        compiler_params=_cparams(("parallel",)),
        name="inproj",
    )(x2d, g_mix[None, :], w, w_in[:, col["sv"][0]:col["sv"][1]].T.astype(BF16), cosq, sinq, coss, sins, gq, gks, gm,
      _block_diag_ones(512, HEAD_DIM), _block_diag_ones(512, MEM_HEAD_DIM))


def _sb_kernel(q_ref, k_ref, vt_ref, u_ref, o_ref, acc_ref, car_ref, *, tq):
    qi = pl.program_id(1)
    rows = lax.broadcasted_iota(I32, (tq, tq), 0)
    cols = lax.broadcasted_iota(I32, (tq, tq), 1)
    dif = rows - cols
    u = u_ref[...]
    acc_ref[...] = jnp.zeros_like(acc_ref)
    car_ref[...] = jnp.zeros_like(car_ref)

    def cond(c):
        kb, mx = c
        return jnp.logical_and(kb >= 0, mx > -SB_CUTOFF)

    def body(c):
        kb, _ = c
        ks = pl.multiple_of(kb * tq, tq)
        earlier = dif < (qi - kb) * tq
        neg_mask = jnp.where(earlier, -1.0, 0.0).astype(BF16)
        heads = range(SB_HEADS)
        z = [_dot_nt(k_ref[0, hd, pl.ds(ks, tq), :], q_ref[0, hd]) for hd in heads]
        ls, lk, between = [], [], []
        for hd in heads:
            zb = z[hd].astype(BF16)
            sp = jnp.maximum(zb, 0.0) + jnp.log(1.0 + jnp.exp(-jnp.abs(zb)))
            ls.append(z[hd] - sp.astype(F32))
            lk.append(sp * neg_mask)
            between.append(_dot(u, lk[hd]))
        for hd in heads:
            rs = slice(hd * HEAD_DIM, (hd + 1) * HEAD_DIM)
            car = car_ref[hd:hd + 1, :]
            w = jnp.where(earlier, jnp.exp(ls[hd] + between[hd] + car), 0.0)
            acc_ref[rs, :] += _dot(vt_ref[0, rs, pl.ds(ks, tq)], w.astype(BF16))
            car_ref[hd:hd + 1, :] = car + (between[hd][0:1, :] + lk[hd][0:1, :].astype(F32))
        return kb - 1, jnp.max(car_ref[...])

    lax.while_loop(cond, body, (qi, jnp.float32(0.0)))
    o_ref[0] = acc_ref[...].T.astype(BF16)


def _strict_tri(n, lower):
    i = np.arange(n)
    m = (i[:, None] > i[None, :]) if lower else (i[:, None] < i[None, :])
    return jnp.asarray(m.astype(np.float32), dtype=BF16)


def _sb_attention(sq, sk, svt, tq):
    batch, _, seq, _ = sq.shape
    return pl.pallas_call(
        functools.partial(_sb_kernel, tq=tq),
        grid=(batch, seq // tq),
        in_specs=[pl.BlockSpec((1, SB_HEADS, tq, HEAD_DIM), lambda b, i: (b, 0, i, 0)),
                  pl.BlockSpec((1, SB_HEADS, seq, HEAD_DIM), lambda b, i: (b, 0, 0, 0)),
                  pl.BlockSpec((1, BRANCH_W, seq), lambda b, i: (b, 0, 0)),
                  pl.BlockSpec((tq, tq), lambda b, i: (0, 0))],
        out_specs=pl.BlockSpec((1, tq, BRANCH_W), lambda b, i: (b, i, 0)),
        out_shape=jax.ShapeDtypeStruct((batch, seq, BRANCH_W), BF16),
        scratch_shapes=[pltpu.VMEM((BRANCH_W, tq), F32), pltpu.VMEM((SB_HEADS, tq), F32)],
        compiler_params=_cparams(("parallel", "parallel")),
        name="sb_attention",
    )(sq, sk, svt, _strict_tri(tq, lower=False))


DSA_SEG = 256
DSA_KB = DSA_SEG


def _tree_sum(parts):
    while len(parts) > 1:
        parts = [parts[i] + parts[i + 1] for i in range(0, len(parts) - 1, 2)] + ([parts[-1]] if len(parts) % 2 else [])
    return parts[0]


def _dsa_kernel(dq_ref, iq_ref, iw_ref, kvi_ref, vt_ref, tri_ref, o_ref,
                sc_ref, bias_ref, s_ref, *, tq, topk, nseg_max):
    nseg = (pl.program_id(1) * tq) // DSA_SEG + 1
    for ns in range(1, nseg_max + 1):
        @pl.when(nseg == ns)
        def _(ns=ns):
            _dsa_body(dq_ref, iq_ref, iw_ref, kvi_ref, vt_ref, tri_ref, o_ref, sc_ref, bias_ref, s_ref,
                      tq=tq, topk=topk, nseg=ns)


def _dsa_body(dq_ref, iq_ref, iw_ref, kvi_ref, vt_ref, tri_ref, o_ref, sc_ref, bias_ref, s_ref, *, tq, topk, nseg):
    qs = pl.program_id(1) * tq
    blocks = [slice(c * DSA_KB, (c + 1) * DSA_KB) for c in range(nseg)]
    iq = iq_ref[0]
    w_t = iw_ref[0].T
    w_row = [w_t[IW_LANE + h:IW_LANE + h + 1, :] * (IDX_HEADS ** -0.5) for h in range(IDX_HEADS)]
    q_chunk = (qs + lax.broadcasted_iota(I32, (DSA_KB, tq), 1)) // CHUNK
    k_chunk = lax.broadcasted_iota(I32, (DSA_KB, tq), 0) // CHUNK

    for c, blk in enumerate(blocks):
        ik = kvi_ref[0, blk, 128:192]
        lg = [_dot_nt(ik, iq[:, h * HEAD_DIM:(h + 1) * HEAD_DIM]) for h in range(IDX_HEADS)]
        sc = jnp.zeros((DSA_KB, tq), F32)
        for h in range(IDX_HEADS):
            sc = sc + w_row[h] * jnp.maximum(lg[h], 0.0)
        admissible = (c * (DSA_KB // CHUNK) + k_chunk) <= q_chunk
        sc_ref[blk, :] = jnp.where(admissible, sc, -jnp.inf)

    def count(pred_fn):
        sub, lanes_of_sums = 32, 4
        acc = [jnp.zeros((sub, tq), F32)] * lanes_of_sums
        for j in range(nseg * DSA_SEG // sub):
            acc[j % lanes_of_sums] = acc[j % lanes_of_sums] + jnp.where(
                pred_fn(sc_ref[j * sub:(j + 1) * sub, :]), 1.0, 0.0)
        return jnp.sum(_tree_sum(acc), axis=0, keepdims=True)

    def key_to_float(key):
        return lax.bitcast_convert_type(jnp.where(key >= 0, key, key ^ 0x7FFFFFFF), F32)

    kf = jnp.float32(topk)
    n_rows = jnp.float32(nseg * DSA_SEG)
    cnt0 = count(lambda s: s >= 0.0)
    t0 = jnp.where(cnt0 >= kf, 0, INT_MIN).astype(I32)
    cnt_t0 = jnp.where(cnt0 >= kf, cnt0, n_rows)

    def bit_step(i, carry):
        t, cnt_t = carry
        cand = t + lax.shift_left(jnp.int32(1), 30 - i)
        cand_f = key_to_float(cand)
        cnt = jnp.where(cand <= KEY_NEG_INF, n_rows, count(lambda s: s >= cand_f))
        take = cnt >= kf
        return jnp.where(take, cand, t), jnp.where(take, cnt, cnt_t)

    thr_key, cnt_thr = lax.fori_loop(0, 31, bit_step, (t0, cnt_t0))
    thr = jnp.where(thr_key <= KEY_NEG_INF, -jnp.inf, key_to_float(thr_key))
    tri = tri_ref[...]

    surplus = jnp.logical_or(jnp.max(cnt_thr) > kf, jnp.min(thr) == -jnp.inf)

    @pl.when(surplus)
    def _():
        need = kf - count(lambda s: s > thr)
        prefix = jnp.zeros((1, tq), F32)
        for blk in blocks:
            sc = sc_ref[blk, :]
            eqf = jnp.where(sc == thr, 1.0, 0.0)
            rank = _dot(tri, eqf.astype(BF16)) + prefix
            tie = jnp.where(rank < need, eqf, 0.0)
            sel = jnp.where(sc > thr, 1.0, tie)
            bias_ref[blk, :] = jnp.where(sc > -jnp.inf, (sel - 1.0) * (-NEG_BIG), NEG_BIG)
            prefix = prefix + jnp.sum(eqf, axis=0, keepdims=True)

    @pl.when(jnp.logical_not(surplus))
    def _():
        for blk in blocks:
            bias_ref[blk, :] = jnp.where(sc_ref[blk, :] >= thr, 0.0, NEG_BIG)

    q8 = dq_ref[0].reshape(DSA_HEADS * tq, HEAD_DIM)
    m = jnp.full((1, DSA_HEADS * tq), NEG_BIG, F32)
    for blk in blocks:
        b = bias_ref[blk, :]
        s = _dot_nt(kvi_ref[0, blk, 0:HEAD_DIM], q8) + jnp.concatenate([b] * DSA_HEADS, axis=1)
        s_ref[blk, :] = s
        m = jnp.maximum(m, jnp.max(s, axis=0, keepdims=True))
    acc = jnp.zeros((128, DSA_HEADS * tq), F32)
    for blk in blocks:
        p = jnp.exp(s_ref[blk, :] - m)
        acc = acc + _dot(vt_ref[0, :, blk], p.astype(BF16))
    for hd in range(DSA_HEADS):
        a = acc[:, hd * tq:(hd + 1) * tq]
        o = (a / a[0:1, :]).T
        o_ref[0, :, hd * HEAD_DIM:(hd + 1) * HEAD_DIM] = o[:, HEAD_DIM:].astype(BF16)


def _dsa_attention(dq, iq, iw, kvi, vt, tq, topk):
    batch, _, seq, _ = dq.shape
    return pl.pallas_call(
        functools.partial(_dsa_kernel, tq=tq, topk=topk, nseg_max=seq // DSA_SEG),
        grid=(batch, seq // tq),
        in_specs=[
            pl.BlockSpec((1, DSA_HEADS, tq, HEAD_DIM), lambda b, i: (b, 0, i, 0)),
            pl.BlockSpec((1, tq, 256), lambda b, i: (b, i, 0)),
            pl.BlockSpec((1, tq, 128), lambda b, i: (b, i, 0)),
            pl.BlockSpec((1, seq, 256), lambda b, i: (b, 0, 0)),
            pl.BlockSpec((1, 128, seq), lambda b, i: (b, 0, 0)),
            pl.BlockSpec((DSA_KB, DSA_KB), lambda b, i: (0, 0)),
        ],
        out_specs=pl.BlockSpec((1, tq, BRANCH_W), lambda b, i: (b, i, 0)),
        out_shape=jax.ShapeDtypeStruct((batch, seq, BRANCH_W), BF16),
        scratch_shapes=[pltpu.VMEM((seq, tq), F32), pltpu.VMEM((seq, tq), F32),
                        pltpu.VMEM((seq, DSA_HEADS * tq), F32)],
        compiler_params=_cparams(("parallel", "parallel")),
        name="dsa_attention",
    )(dq, iq.reshape(batch, seq, 256), iw.reshape(batch, seq, 128), kvi.reshape(batch, seq, 256), vt,
      _strict_tri(DSA_KB, lower=True))


def _memkv_kernel(m_ref, g_ref, w_ref, gk_ref, bd_ref, mk_ref, mv_ref):
    x = m_ref[...]
    ms = jnp.mean(x * x, axis=-1, keepdims=True)
    h = (x * lax.rsqrt(ms + EPS) * g_ref[...]).astype(BF16)
    mw = MEM_HEADS * MEM_HEAD_DIM
    k = _dot(h, w_ref[:, :mw])
    msk = _dot_split(k * k, bd_ref[...]) * (1.0 / MEM_HEAD_DIM)
    mk_ref[...] = (k * lax.rsqrt(msk + EPS) * gk_ref[...]).astype(BF16)
    mv_ref[...] = _dot(h, w_ref[:, mw:]).astype(BF16)


def _mem_kv(mem2d, g_mem, w_mem_kv, g_k_mem, tm):
    n = mem2d.shape[0]
    mw = MEM_HEADS * MEM_HEAD_DIM
    row = lambda i: (i, 0)
    const = lambda i: (0, 0)
    return pl.pallas_call(
        _memkv_kernel,
        grid=(n // tm,),
        in_specs=[pl.BlockSpec((tm, D_MODEL), row), pl.BlockSpec((1, D_MODEL), const),
                  pl.BlockSpec((D_MODEL, 2 * mw), const), pl.BlockSpec((1, mw), const),
                  pl.BlockSpec((mw, mw), const)],
        out_specs=[pl.BlockSpec((tm, mw), row), pl.BlockSpec((tm, mw), row)],
        out_shape=[jax.ShapeDtypeStruct((n, mw), BF16), jax.ShapeDtypeStruct((n, mw), BF16)],
        compiler_params=_cparams(("parallel",)),
        name="mem_kv",
    )(mem2d, g_mem[None, :], w_mem_kv.astype(BF16), jnp.tile(g_k_mem, MEM_HEADS)[None, :],
      _block_diag_ones(mw, MEM_HEAD_DIM))


def _memattn_kernel(q_ref, k_ref, v_ref, o_ref):
    for hd in range(MEM_HEADS):
        sl = slice(hd * MEM_HEAD_DIM, (hd + 1) * MEM_HEAD_DIM)
        s = _dot_nt(q_ref[0, :, sl], k_ref[0, :, sl]) * (MEM_HEAD_DIM ** -0.5)
        p = jnp.exp(s - jnp.max(s, axis=-1, keepdims=True))
        o = _dot(p.astype(BF16), v_ref[0, :, sl]) / jnp.sum(p, axis=-1, keepdims=True)
        o_ref[0, :, sl] = o.astype(BF16)


def _mem_attention(mq, mk, mv, tq):
    batch, seq, mw = mq.shape
    mlen = mk.shape[1]
    kspec = pl.BlockSpec((1, mlen, mw), lambda b, i: (b, 0, 0))
    return pl.pallas_call(
        _memattn_kernel,
        grid=(batch, seq // tq),
        in_specs=[pl.BlockSpec((1, tq, mw), lambda b, i: (b, i, 0)), kspec, kspec],
        out_specs=pl.BlockSpec((1, tq, mw), lambda b, i: (b, i, 0)),
        out_shape=jax.ShapeDtypeStruct((batch, seq, mw), BF16),
        compiler_params=_cparams(("parallel", "parallel")),
        name="mem_attention",
    )(mq, mk, mv)


def _merge_kernel(x_ref, g_ref, osb_ref, odsa_ref, omem_ref, wg_ref, bg_ref, wb_ref, wo_ref,
                  gf_ref, wrh_ref, wrl_ref, br_ref, tri_ref,
                  x2_ref, h2_ref, idx_ref, gate_ref, rank_ref, cnt_ref, carry_ref):
    x = x_ref[...]
    ms = jnp.mean(x * x, axis=-1, keepdims=True)
    h = (x * lax.rsqrt(ms + EPS) * g_ref[...]).astype(BF16)
    merged = None
    for n, o_ref in enumerate((osb_ref, odsa_ref, omem_ref)):
        gate = jax.nn.sigmoid(_dot(h, wg_ref[n]) + bg_ref[n])
        term = gate * _dot(o_ref[...], wb_ref[n])
        merged = term if merged is None else merged + term
    x2 = x + _dot(merged.astype(BF16), wo_ref[...])
    x2_ref[...] = x2

    ms2 = jnp.mean(x2 * x2, axis=-1, keepdims=True)
    h2 = x2 * lax.rsqrt(ms2 + EPS) * gf_ref[...]
    _store_token_tiles(h2_ref, h2)
    hi, lo = _split_bf16(h2)
    logits = _dot(hi, wrh_ref[...]) + _dot(hi, wrl_ref[...]) + _dot(lo, wrh_ref[...]) + br_ref[...]

    lane = lax.broadcasted_iota(I32, logits.shape, 1)
    vals = logits
    top_v, top_i = [], []
    for _ in range(TOP_K):
        mx = jnp.max(vals, axis=-1, keepdims=True)
        ix = jnp.min(jnp.where(vals == mx, lane, N_EXPERTS), axis=-1, keepdims=True)
        top_v.append(mx)
        top_i.append(ix)
        vals = jnp.where(lane == ix, -jnp.inf, vals)
    ex = [jnp.exp(v - top_v[0]) for v in top_v]
    den = ex[0] + ex[1] + ex[2] + ex[3]
    for k in range(TOP_K):
        idx_ref[:, k:k + 1] = top_i[k]
        gate_ref[:, k:k + 1] = ex[k] / den

    @pl.when(pl.program_id(0) == 0)
    def _():
        carry_ref[...] = jnp.zeros_like(carry_ref)

    lane_e = lax.broadcasted_iota(I32, (x.shape[0], LANES), 1)
    hits = [lane_e == top_i[k] for k in range(TOP_K)]
    onehot = sum(jnp.where(hk, 1.0, 0.0) for hk in hits)
    before = _dot(tri_ref[...], onehot.astype(BF16)) + carry_ref[...]
    for k in range(TOP_K):
        rank_ref[:, k:k + 1] = jnp.sum(jnp.where(hits[k], before, 0.0), axis=-1, keepdims=True).astype(I32)
    carry_ref[...] += jnp.sum(onehot, axis=0, keepdims=True)
    cnt_ref[...] = carry_ref[...]


def _merge_route(x2d, g_mix, o_sb, o_dsa, o_mem, w_gate, b_gate, w_branch, w_out, g_ffn, w_router, b_router, tm):
    n = x2d.shape[0]
    row = lambda i: (i, 0)
    c2 = lambda i: (0, 0)
    c3 = lambda i: (0, 0, 0)
    wr_hi = w_router.astype(BF16)
    wr_lo = (w_router - wr_hi.astype(F32)).astype(BF16)
    return pl.pallas_call(
        _merge_kernel,
        grid=(n // tm,),
        in_specs=[pl.BlockSpec((tm, D_MODEL), row), pl.BlockSpec((1, D_MODEL), c2),
                  pl.BlockSpec((tm, BRANCH_W), row), pl.BlockSpec((tm, BRANCH_W), row),
                  pl.BlockSpec((tm, BRANCH_W), row),
                  pl.BlockSpec((3, D_MODEL, D_MODEL), c3, pipeline_mode=pl.Buffered(1)),
                  pl.BlockSpec((3, 1, D_MODEL), c3),
                  pl.BlockSpec((3, BRANCH_W, D_MODEL), c3, pipeline_mode=pl.Buffered(1)),
                  pl.BlockSpec((D_MODEL, D_MODEL), c2, pipeline_mode=pl.Buffered(1)),
                  pl.BlockSpec((1, D_MODEL), c2), pl.BlockSpec((D_MODEL, N_EXPERTS), c2),
                  pl.BlockSpec((D_MODEL, N_EXPERTS), c2), pl.BlockSpec((1, N_EXPERTS), c2),
                  pl.BlockSpec((tm, tm), c2, pipeline_mode=pl.Buffered(1))],
        out_specs=[pl.BlockSpec((tm, D_MODEL), row), pl.BlockSpec((tm * TOKEN_TILE, LANES), row),
                   pl.BlockSpec((tm, TOP_K), row), pl.BlockSpec((tm, TOP_K), row),
                   pl.BlockSpec((tm, TOP_K), row), pl.BlockSpec((1, LANES), c2)],
        out_shape=[jax.ShapeDtypeStruct((n, D_MODEL), F32), jax.ShapeDtypeStruct((n * TOKEN_TILE, LANES), F32),
                   jax.ShapeDtypeStruct((n, TOP_K), I32), jax.ShapeDtypeStruct((n, TOP_K), F32),
                   jax.ShapeDtypeStruct((n, TOP_K), I32), jax.ShapeDtypeStruct((1, LANES), F32)],
        scratch_shapes=[pltpu.VMEM((1, LANES), F32)],
        compiler_params=_cparams(("arbitrary",)),
        name="merge_route",
    )(x2d, g_mix[None, :], o_sb, o_dsa, o_mem, w_gate.astype(BF16), b_gate[:, None, :],
      w_branch.astype(BF16), w_out.astype(BF16), g_ffn[None, :], wr_hi, wr_lo, b_router[None, :],
      _strict_tri(tm, lower=True))


DISPATCH_TOKENS = 2048
COMBINE_TOKENS = 512


def _dispatch_kernel(dest_ref, pad_ref, h_ref, xs_ref, zero_ref, sem, *, n_tok, n_pad):
    n_copies = n_tok * TOP_K + n_pad
    zero_ref[...] = jnp.zeros_like(zero_ref)

    def fill(j, c):
        for r in range(2):
            dst = pl.multiple_of(pad_ref[0, 0, 2 * j + r] * TOKEN_TILE, TOKEN_TILE)
            pltpu.make_async_copy(zero_ref, xs_ref.at[pl.ds(dst, TOKEN_TILE)], sem).start(priority=r)
        return c

    lax.fori_loop(0, n_pad // 2, fill, 0, unroll=2)

    def start(t, c):
        src = pl.multiple_of(t * TOKEN_TILE, TOKEN_TILE)
        for k in range(TOP_K):
            dst = pl.multiple_of(dest_ref[0, 0, t * TOP_K + k] * TOKEN_TILE, TOKEN_TILE)
            pltpu.make_async_copy(h_ref.at[pl.ds(src, TOKEN_TILE)], xs_ref.at[pl.ds(dst, TOKEN_TILE)],
                                  sem).start(priority=k % 2)
        return c

    lax.fori_loop(0, n_tok, start, 0, unroll=2)
    total = n_copies * TOKEN_TILE
    pltpu.make_async_copy(xs_ref.at[pl.ds(0, total)], xs_ref.at[pl.ds(0, total)], sem).wait()


def _dispatch(dest, pad_slots, h2t, n_slots):
    n = h2t.shape[0] // TOKEN_TILE
    n_tok = min(DISPATCH_TOKENS, n)
    steps = n // n_tok
    per = n_tok * TOP_K
    n_pad = pad_slots.shape[0] // steps
    assert n_tok * steps == n and n_pad * steps == pad_slots.shape[0] and n_pad % 2 == 0
    return pl.pallas_call(
        functools.partial(_dispatch_kernel, n_tok=n_tok, n_pad=n_pad),
        grid=(steps,),
        in_specs=[pl.BlockSpec((1, 1, per), lambda i: (i, 0, 0), memory_space=pltpu.SMEM),
                  pl.BlockSpec((1, 1, n_pad), lambda i: (i, 0, 0), memory_space=pltpu.SMEM),
                  pl.BlockSpec((n_tok * TOKEN_TILE, LANES), lambda i: (i, 0))],
        out_specs=pl.BlockSpec(memory_space=pl.ANY),
        out_shape=jax.ShapeDtypeStruct((n_slots * TOKEN_TILE, LANES), F32),
        scratch_shapes=[pltpu.VMEM((TOKEN_TILE, LANES), F32), pltpu.SemaphoreType.DMA(())],
        compiler_params=_cparams(("arbitrary",)),
        name="moe_dispatch",
    )(dest.reshape(steps, 1, per), pad_slots.reshape(steps, 1, n_pad), h2t)


def _expert_kernel(blk_e_ref, nused_ref, x_ref, w1_ref, b1_ref, w2_ref, b2_ref, y_ref, xb_ref, w1b_ref, w2b_ref):
    i = pl.program_id(0)
    new_expert = jnp.logical_or(i == 0, blk_e_ref[i] != blk_e_ref[jnp.maximum(i - 1, 0)])

    @pl.when(jnp.logical_and(i < nused_ref[0], new_expert))
    def _():
        w1b_ref[...] = w1_ref[0].astype(BF16)
        w2b_ref[...] = w2_ref[0].astype(BF16)

    @pl.when(i < nused_ref[0])
    def _():
        for s in range(TOKEN_TILE):
            xb_ref[:, s * LANES:(s + 1) * LANES] = _load_token_tiles(x_ref, 0, MOE_BLOCK, s).astype(BF16)
        hb = _dot(xb_ref[...], w1b_ref[...]) + b1_ref[0]
        g = jnp.minimum(hb[:, :D_EXPERT], SWIGLU_LIMIT)
        u = jnp.clip(hb[:, D_EXPERT:], -SWIGLU_LIMIT, SWIGLU_LIMIT)
        act = (u + 1.0) * (g * jax.nn.sigmoid(SWIGLU_ALPHA * g))
        _store_token_tiles(y_ref, _dot(act.astype(BF16), w2b_ref[...]) + b2_ref[0])

    @pl.when(pl.program_id(0) >= nused_ref[0])
    def _():
        y_ref[...] = jnp.zeros_like(y_ref)


def _experts(blk_e, n_used, xs, w_e_in, b_e_in, w_e_out, b_e_out):
    n_slots = xs.shape[0] // TOKEN_TILE
    nblk = n_slots // MOE_BLOCK
    slot_block = pl.BlockSpec((MOE_BLOCK * TOKEN_TILE, LANES), lambda i, be, nu: (i, 0))
    grid_spec = pltpu.PrefetchScalarGridSpec(
        num_scalar_prefetch=2,
        grid=(nblk,),
        in_specs=[slot_block,
                  pl.BlockSpec((1, D_MODEL, 2 * D_EXPERT), lambda i, be, nu: (be[i], 0, 0)),
                  pl.BlockSpec((1, 1, 2 * D_EXPERT), lambda i, be, nu: (be[i], 0, 0)),
                  pl.BlockSpec((1, D_EXPERT, D_MODEL), lambda i, be, nu: (be[i], 0, 0)),
                  pl.BlockSpec((1, 1, D_MODEL), lambda i, be, nu: (be[i], 0, 0))],
        out_specs=slot_block,
        scratch_shapes=[pltpu.VMEM((MOE_BLOCK, D_MODEL), BF16), pltpu.VMEM((D_MODEL, 2 * D_EXPERT), BF16),
                        pltpu.VMEM((D_EXPERT, D_MODEL), BF16)],
    )
    return pl.pallas_call(
        _expert_kernel,
        grid_spec=grid_spec,
        out_shape=jax.ShapeDtypeStruct((n_slots * TOKEN_TILE, LANES), F32),
        compiler_params=_cparams(("arbitrary",)),
        name="moe_experts",
    )(blk_e, n_used, xs, w_e_in, b_e_in[:, None, :], w_e_out, b_e_out[:, None, :])


def _combine_kernel(dest_ref, dest_next_ref, x_ref, gate_ref, y_ref, o_ref, buf_ref, sem):
    n_copies = COMBINE_TOKENS * TOP_K
    rows = 32
    i = pl.program_id(0)
    slot = i % 2
    buf, nxt = buf_ref.at[slot], buf_ref.at[1 - slot]

    def start_rows(idx_ref, dst_buf, dma_sem, t0):
        for t in range(rows):
            for k in range(TOP_K):
                src = pl.multiple_of(idx_ref[0, 0, (t0 + t) * TOP_K + k] * TOKEN_TILE, TOKEN_TILE)
                dst = pl.multiple_of((k * COMBINE_TOKENS + t0 + t) * TOKEN_TILE, TOKEN_TILE)
                pltpu.make_async_copy(y_ref.at[pl.ds(src, TOKEN_TILE)], dst_buf.at[pl.ds(dst, TOKEN_TILE)],
                                      dma_sem).start(priority=k % 2)

    def wait_rows(dst_buf, dma_sem):
        pltpu.make_async_copy(y_ref.at[pl.ds(0, n_copies * TOKEN_TILE)], dst_buf, dma_sem).wait()

    @pl.when(i == 0)
    def _():
        def first(c, z):
            start_rows(dest_ref, buf_ref.at[0], sem.at[0], c * rows)
            return z
        lax.fori_loop(0, COMBINE_TOKENS // rows, first, 0)

    wait_rows(buf, sem.at[slot])

    def sum_pass(c, z):
        r0 = pl.multiple_of(c * rows, rows)
        start_rows(dest_next_ref, nxt, sem.at[1 - slot], r0)
        gate = gate_ref[pl.ds(r0, rows), :]
        gates = [jnp.broadcast_to(gate[:, k:k + 1], (rows, LANES)) for k in range(TOP_K)]
        for s in range(TOKEN_TILE):
            out = x_ref[pl.ds(r0, rows), s * LANES:(s + 1) * LANES]
            for k in range(TOP_K):
                out = out + gates[k] * _load_token_tiles(buf, k * COMBINE_TOKENS + r0, rows, s)
            o_ref[pl.ds(r0, rows), s * LANES:(s + 1) * LANES] = out
        return z

    lax.fori_loop(0, COMBINE_TOKENS // rows, sum_pass, 0)

    @pl.when(i == pl.num_programs(0) - 1)
    def _():
        wait_rows(nxt, sem.at[1 - slot])


def _combine(dest, x2, gate, ys):
    n = x2.shape[0]
    steps = n // COMBINE_TOKENS
    per = COMBINE_TOKENS * TOP_K
    dest3 = dest.reshape(steps, 1, per)
    return pl.pallas_call(
        _combine_kernel,
        grid=(steps,),
        in_specs=[pl.BlockSpec((1, 1, per), lambda i: (i, 0, 0), memory_space=pltpu.SMEM),
                  pl.BlockSpec((1, 1, per), lambda i: (jnp.minimum(i + 1, steps - 1), 0, 0), memory_space=pltpu.SMEM),
                  pl.BlockSpec((COMBINE_TOKENS, D_MODEL), lambda i: (i, 0)),
                  pl.BlockSpec((COMBINE_TOKENS, TOP_K), lambda i: (i, 0)),
                  pl.BlockSpec(memory_space=pl.ANY)],
        out_specs=pl.BlockSpec((COMBINE_TOKENS, D_MODEL), lambda i: (i, 0)),
        out_shape=jax.ShapeDtypeStruct((n, D_MODEL), F32),
        scratch_shapes=[pltpu.VMEM((2, TOP_K * COMBINE_TOKENS * TOKEN_TILE, LANES), F32),
                        pltpu.SemaphoreType.DMA((2,))],
        compiler_params=_cparams(("arbitrary",)),
        name="moe_combine",
    )(dest3, dest3, x2, gate, ys)


def _moe(x2, h2, top_idx, gate, rank, counts, w_e_in, b_e_in, w_e_out, b_e_out):
    n = x2.shape[0]
    counts = counts[0, :N_EXPERTS].astype(I32)
    padded = (counts + MOE_BLOCK - 1) // MOE_BLOCK * MOE_BLOCK
    pend = jnp.cumsum(padded)
    pstart = pend - padded
    nblk = -(-(n * TOP_K) // MOE_BLOCK) + N_EXPERTS
    blk_start = jnp.arange(nblk, dtype=I32) * MOE_BLOCK
    blk_e = jnp.minimum(jnp.sum((pend[None, :] <= blk_start[:, None]).astype(I32), axis=1), N_EXPERTS - 1)
    n_used = (pend[-1:] // MOE_BLOCK).astype(I32)
    onehot = top_idx[:, :, None] == jnp.arange(N_EXPERTS, dtype=I32)[None, None, :]
    dest = rank + jnp.sum(jnp.where(onehot, pstart[None, None, :], 0), axis=-1)
    n_slots = nblk * MOE_BLOCK
    pad_len = padded - counts
    pad_end = jnp.cumsum(pad_len)
    base = jnp.concatenate([pstart + counts - (pad_end - pad_len), pend[-1:] - pad_end[-1:]])
    j = jnp.arange(n_slots - n * TOP_K, dtype=I32)
    group = jnp.sum((pad_end[None, :] <= j[:, None]).astype(I32), axis=1)
    group_hot = group[:, None] == jnp.arange(N_EXPERTS + 1, dtype=I32)[None, :]
    pad_slots = j + jnp.sum(jnp.where(group_hot, base[None, :], 0), axis=1)
    xs = _dispatch(dest, pad_slots, h2, n_slots)
    ys = _experts(blk_e, n_used, xs, w_e_in, b_e_in, w_e_out, b_e_out)
    return _combine(dest, x2, gate, ys)


def _layer(x, mem, g_mix, w_in, g_q_dsa, g_k_dsa, g_q_mem, g_k_mem, g_mem, w_mem_kv, w_gate, b_gate,
           w_branch, w_out, g_ffn, w_router, b_router, w_e_in, b_e_in, w_e_out, b_e_out):
    batch, seq, _ = x.shape
    n = batch * seq
    topk = min(DSA_TOPK_MAX, seq // 4)
    x2d = x.reshape(n, D_MODEL)
    tm_dense = min(DENSE_ROWS, seq)
    sq, sk, sv, dq, iq, kvi, iw, mq, vt = _inproj(x2d, batch, seq, g_mix, w_in, g_q_dsa, g_k_dsa, g_q_mem, tm_dense)
    o_sb = _sb_attention(sq, sk, sv, min(256, seq))
    o_dsa = _dsa_attention(dq, iq, iw, kvi, vt, 128, topk)
    mlen = mem.shape[1]
    mk, mv = _mem_kv(mem.reshape(batch * mlen, D_MODEL), g_mem, w_mem_kv, g_k_mem, min(512, batch * mlen))
    mw = MEM_HEADS * MEM_HEAD_DIM
    o_mem = _mem_attention(mq.reshape(batch, seq, mw), mk.reshape(batch, mlen, mw), mv.reshape(batch, mlen, mw), seq)
    x2, h2, top_idx, gate, rank, counts = _merge_route(
        x2d, g_mix, o_sb.reshape(n, BRANCH_W), o_dsa.reshape(n, BRANCH_W), o_mem.reshape(n, mw),
        w_gate, b_gate, w_branch, w_out, g_ffn, w_router, b_router, tm_dense)
    out = _moe(x2, h2, top_idx, gate, rank, counts, w_e_in, b_e_in, w_e_out, b_e_out)
    return out.reshape(batch, seq, D_MODEL)


def kernel(x, mem, g_mix, w_in, g_q_dsa, g_k_dsa, g_q_mem, g_k_mem, g_mem, w_mem_kv, w_gate, b_gate, w_branch, w_out, g_ffn, w_router, b_router, w_e_in, b_e_in, w_e_out, b_e_out):
    for l in range(g_mix.shape[0]):
        x = _layer(x, mem, g_mix[l], w_in[l], g_q_dsa[l], g_k_dsa[l], g_q_mem[l], g_k_mem[l], g_mem[l],
                   w_mem_kv[l], w_gate[l], b_gate[l], w_branch[l], w_out[l], g_ffn[l], w_router[l],
                   b_router[l], w_e_in[l], b_e_in[l], w_e_out[l], b_e_out[l])
    return x
```

```python
import functools

import numpy as np
import jax
import jax.numpy as jnp
from jax import lax
from jax.experimental import pallas as pl
from jax.experimental.pallas import tpu as pltpu

F32 = jnp.float32
BF16 = jnp.bfloat16
I32 = jnp.int32

D_MODEL = 1024
CHUNK = 64
SB_HEADS = 8
DSA_HEADS = 8
HEAD_DIM = 64
IDX_HEADS = 4
DSA_TOPK_MAX = 256
MEM_HEADS = 4
MEM_HEAD_DIM = 128
N_EXPERTS = 32
TOP_K = 4
D_EXPERT = D_MODEL
SWIGLU_LIMIT = 7.0
SWIGLU_ALPHA = 1.702
ROPE_THETA = 10000.0
EPS = 1e-6
MOE_BLOCK = 512

BRANCH_W = 512
IN_SIZES = (512, 512, 512, 512, 64, 64, 256, 64, 4, 512)
C_SQ, C_SK, C_SV, C_DQ, C_MQ, C_IQ, C_SMALL, C_END = 0, 512, 1024, 1536, 2048, 2560, 2816, 3072
IW_LANE = 64

LANES = 128
NEG_BIG = -1e30
SB_CUTOFF = 110.0
KEY_NEG_INF = int(np.array(-np.inf, np.float32).view(np.int32)) ^ 0x7FFFFFFF
INT_MIN = -(2 ** 31)

VMEM_LIMIT = 56 * 1024 * 1024
DENSE_ROWS = 1024


def _cparams(sem):
    return pltpu.CompilerParams(dimension_semantics=sem, vmem_limit_bytes=VMEM_LIMIT)


def _dot(a, b):
    return jnp.dot(a, b, preferred_element_type=F32)


def _dot_nt(a, b):
    return lax.dot_general(a, b, (((1,), (1,)), ((), ())), preferred_element_type=F32)


def _split_bf16(x):
    hi = x.astype(BF16)
    lo = (x - hi.astype(F32)).astype(BF16)
    return hi, lo


def _dot_split(x, m_bf16):
    hi, lo = _split_bf16(x)
    return _dot(hi, m_bf16) + _dot(lo, m_bf16)


TOKEN_TILE = D_MODEL // LANES


def _store_token_tiles(ref, y):
    rows = y.shape[0]
    for s in range(TOKEN_TILE):
        ref[pl.ds(s, rows, stride=TOKEN_TILE), :] = y[:, s * LANES:(s + 1) * LANES]


def _load_token_tiles(ref, start_row, rows, s):
    return ref[pl.ds(start_row * TOKEN_TILE + s, rows, stride=TOKEN_TILE), :]


def _rot_half_unsigned(y):
    w = y.shape[1]
    lane = lax.broadcasted_iota(I32, y.shape, 1)
    return jnp.where((lane & 32) == 0, pltpu.roll(y, w - 32, 1), pltpu.roll(y, 32, 1))


def _inproj_kernel(x_ref, g_ref, w_ref, wsvt_ref, cos_ref, sin_ref, coss_ref, sins_ref, gq_ref, gks_ref,
                   gm_ref, bd64_ref, bd128_ref,
                   sq_ref, sk_ref, sv_ref, dq_ref, iq_ref, kvi_ref, iw_ref, mq_ref, vt_ref):
    x = x_ref[...]
    ms = jnp.mean(x * x, axis=-1, keepdims=True)
    h = (x * lax.rsqrt(ms + EPS) * g_ref[...]).astype(BF16)

    def seg(a, b):
        return _dot(h, w_ref[:, a:b])

    def put_heads(ref, y):
        for hd in range(y.shape[1] // HEAD_DIM):
            ref[0, hd] = y[:, hd * HEAD_DIM:(hd + 1) * HEAD_DIM].astype(BF16)

    put_heads(sq_ref, seg(C_SQ, C_SK) * (HEAD_DIM ** -0.5))
    put_heads(sk_ref, seg(C_SK, C_SV))
    sv_ref[0] = _dot_nt(wsvt_ref[...], h).astype(BF16)

    y = seg(C_DQ, C_MQ)
    msq = _dot_split(y * y, bd64_ref[...]) * (1.0 / HEAD_DIM)
    y = y * lax.rsqrt(msq + EPS) * gq_ref[...]
    y = y * cos_ref[...] + _rot_half_unsigned(y) * sin_ref[...]
    put_heads(dq_ref, y * (HEAD_DIM ** -0.5))

    y = seg(C_IQ, C_SMALL)
    y = y * cos_ref[:, :256] + _rot_half_unsigned(y) * sin_ref[:, :256]
    iq_ref[...] = (y * (HEAD_DIM ** -0.5)).astype(BF16)

    y = seg(C_SMALL, C_END)
    lane = lax.broadcasted_iota(I32, y.shape, 1)
    is_k = lane < HEAD_DIM
    msk = jnp.sum(jnp.where(is_k, y * y, 0.0), axis=-1, keepdims=True) * (1.0 / HEAD_DIM)
    y = y * jnp.where(is_k, lax.rsqrt(msk + EPS) * gks_ref[...], 1.0)
    y = y * coss_ref[...] + _rot_half_unsigned(y) * sins_ref[...]
    kvi_ref[...] = y.astype(BF16)
    iw_ref[...] = y[:, 128:256]
    kv_t = y[:, 0:128].T
    row_t = lax.broadcasted_iota(I32, kv_t.shape, 0)
    vt_ref[0] = jnp.where(row_t < HEAD_DIM, 1.0, kv_t).astype(BF16)

    y = seg(C_MQ, C_IQ)
    msm = _dot_split(y * y, bd128_ref[...]) * (1.0 / MEM_HEAD_DIM)
    mq_ref[...] = (y * lax.rsqrt(msm + EPS) * gm_ref[...]).astype(BF16)


def _rope_tables(seq):
    half = HEAD_DIM // 2
    inv = ROPE_THETA ** (-jnp.arange(half, dtype=F32) / half)
    ang = jnp.arange(seq).astype(F32)[:, None] * inv[None, :]
    cos = jnp.cos(ang)
    sin = jnp.sin(ang)
    cos64 = jnp.concatenate([cos, cos], axis=1)
    sin64 = jnp.concatenate([-sin, sin], axis=1)
    one = jnp.ones_like(cos64)
    zero = jnp.zeros_like(cos64)
    cosq = jnp.tile(cos64, (1, 8))
    sinq = jnp.tile(sin64, (1, 8))
    coss = jnp.concatenate([cos64, one, cos64, one], axis=1)
    sins = jnp.concatenate([sin64, zero, sin64, zero], axis=1)
    return cosq, sinq, coss, sins


def _block_diag_ones(width, group):
    idx = np.arange(width) // group
    return jnp.asarray((idx[:, None] == idx[None, :]).astype(np.float32), dtype=BF16)


def _inproj(x2d, batch, seq, g_mix, w_in, g_q_dsa, g_k_dsa, g_q_mem, tm):
    n = x2d.shape[0]
    sizes = np.cumsum((0,) + IN_SIZES)
    col = {name: (int(sizes[i]), int(sizes[i + 1])) for i, name in enumerate(
        ("sq", "sk", "sv", "dq", "dk", "dv", "iq", "ik", "iw", "mq"))}
    order = ("sq", "sk", "sv", "dq", "mq", "iq", "dk", "dv", "ik", "iw")
    w = jnp.concatenate([w_in[:, col[k][0]:col[k][1]] for k in order]
                        + [jnp.zeros((D_MODEL, C_END - sum(IN_SIZES)), w_in.dtype)], axis=1).astype(BF16)
    cosq, sinq, coss, sins = (jnp.asarray(t) for t in _rope_tables(seq))
    gq = jnp.tile(g_q_dsa, 8)[None, :]
    gks = jnp.concatenate([g_k_dsa, jnp.ones((256 - HEAD_DIM,), F32)])[None, :]
    gm = jnp.tile(g_q_mem, MEM_HEADS)[None, :]
    spb = seq // tm
    row = lambda i: (i, 0)
    const = lambda i: (0, 0)
    pos = lambda i: (i % spb, 0)
    heads = lambda i: (i // spb, 0, i % spb, 0)
    head_shape = jax.ShapeDtypeStruct((batch, 8, seq, HEAD_DIM), BF16)
    head_spec = pl.BlockSpec((1, 8, tm, HEAD_DIM), heads)
    return pl.pallas_call(
        _inproj_kernel,
        grid=(n // tm,),
        in_specs=[
            pl.BlockSpec((tm, D_MODEL), row),
            pl.BlockSpec((1, D_MODEL), const),
            pl.BlockSpec((D_MODEL, C_END), const, pipeline_mode=pl.Buffered(1)),
            pl.BlockSpec((512, D_MODEL), const, pipeline_mode=pl.Buffered(1)),
            pl.BlockSpec((tm, 512), pos), pl.BlockSpec((tm, 512), pos),
            pl.BlockSpec((tm, 256), pos), pl.BlockSpec((tm, 256), pos),
            pl.BlockSpec((1, 512), const), pl.BlockSpec((1, 256), const), pl.BlockSpec((1, 512), const),
            pl.BlockSpec((512, 512), const), pl.BlockSpec((512, 512), const),
        ],
        out_specs=[head_spec, head_spec, pl.BlockSpec((1, 512, tm), lambda i: (i // spb, 0, i % spb)), head_spec,
                   pl.BlockSpec((tm, 256), row), pl.BlockSpec((tm, 256), row),
                   pl.BlockSpec((tm, 128), row), pl.BlockSpec((tm, 512), row),
                   pl.BlockSpec((1, 128, tm), lambda i: (i // spb, 0, i % spb))],
        out_shape=[head_shape, head_shape, jax.ShapeDtypeStruct((batch, 512, seq), BF16), head_shape,
                   jax.ShapeDtypeStruct((n, 256), BF16), jax.ShapeDtypeStruct((n, 256), BF16),
                   jax.ShapeDtypeStruct((n, 128), F32), jax.ShapeDtypeStruct((n, 512), BF16),
                   jax.ShapeDtypeStruct((batch, 128, seq), BF16)],
        compiler_params=_cparams(("parallel",)),
        name="inproj",
    )(x2d, g_mix[None, :], w, w_in[:, col["sv"][0]:col["sv"][1]].T.astype(BF16), cosq, sinq, coss, sins, gq, gks, gm,
      _block_diag_ones(512, HEAD_DIM), _block_diag_ones(512, MEM_HEAD_DIM))


def _sb_kernel(q_ref, k_ref, vt_ref, u_ref, o_ref, acc_ref, car_ref, *, tq):
    qi = pl.program_id(1)
    rows = lax.broadcasted_iota(I32, (tq, tq), 0)
    cols = lax.broadcasted_iota(I32, (tq, tq), 1)
    dif = rows - cols
    u = u_ref[...]
    acc_ref[...] = jnp.zeros_like(acc_ref)
    car_ref[...] = jnp.zeros_like(car_ref)

    def cond(c):
        kb, mx = c
        return jnp.logical_and(kb >= 0, mx > -SB_CUTOFF)

    def body(c):
        kb, _ = c
        ks = pl.multiple_of(kb * tq, tq)
        earlier = dif < (qi - kb) * tq
        neg_mask = jnp.where(earlier, -1.0, 0.0).astype(BF16)
        heads = range(SB_HEADS)
        z = [_dot_nt(k_ref[0, hd, pl.ds(ks, tq), :], q_ref[0, hd]) for hd in heads]
        ls, lk, between = [], [], []
        for hd in heads:
            zb = z[hd].astype(BF16)
            sp = jnp.maximum(zb, 0.0) + jnp.log(1.0 + jnp.exp(-jnp.abs(zb)))
            ls.append(z[hd] - sp.astype(F32))
            lk.append(sp * neg_mask)
            between.append(_dot(u, lk[hd]))
        for hd in heads:
            rs = slice(hd * HEAD_DIM, (hd + 1) * HEAD_DIM)
            car = car_ref[hd:hd + 1, :]
            w = jnp.where(earlier, jnp.exp(ls[hd] + between[hd] + car), 0.0)
            acc_ref[rs, :] += _dot(vt_ref[0, rs, pl.ds(ks, tq)], w.astype(BF16))
            car_ref[hd:hd + 1, :] = car + (between[hd][0:1, :] + lk[hd][0:1, :].astype(F32))
        return kb - 1, jnp.max(car_ref[...])

    lax.while_loop(cond, body, (qi, jnp.float32(0.0)))
    o_ref[0] = acc_ref[...].T.astype(BF16)


def _strict_tri(n, lower):
    i = np.arange(n)
    m = (i[:, None] > i[None, :]) if lower else (i[:, None] < i[None, :])
    return jnp.asarray(m.astype(np.float32), dtype=BF16)


def _sb_attention(sq, sk, svt, tq):
    batch, _, seq, _ = sq.shape
    return pl.pallas_call(
        functools.partial(_sb_kernel, tq=tq),
        grid=(batch, seq // tq),
        in_specs=[pl.BlockSpec((1, SB_HEADS, tq, HEAD_DIM), lambda b, i: (b, 0, i, 0)),
                  pl.BlockSpec((1, SB_HEADS, seq, HEAD_DIM), lambda b, i: (b, 0, 0, 0)),
                  pl.BlockSpec((1, BRANCH_W, seq), lambda b, i: (b, 0, 0)),
                  pl.BlockSpec((tq, tq), lambda b, i: (0, 0))],
        out_specs=pl.BlockSpec((1, tq, BRANCH_W), lambda b, i: (b, i, 0)),
        out_shape=jax.ShapeDtypeStruct((batch, seq, BRANCH_W), BF16),
        scratch_shapes=[pltpu.VMEM((BRANCH_W, tq), F32), pltpu.VMEM((SB_HEADS, tq), F32)],
        compiler_params=_cparams(("parallel", "parallel")),
        name="sb_attention",
    )(sq, sk, svt, _strict_tri(tq, lower=False))


DSA_SEG = 256
DSA_KB = DSA_SEG


def _tree_sum(parts):
    while len(parts) > 1:
        parts = [parts[i] + parts[i + 1] for i in range(0, len(parts) - 1, 2)] + ([parts[-1]] if len(parts) % 2 else [])
    return parts[0]


def _dsa_kernel(dq_ref, iq_ref, iw_ref, kvi_ref, vt_ref, tri_ref, o_ref,
                sc_ref, bias_ref, s_ref, *, tq, topk, nseg_max):
    nseg = (pl.program_id(1) * tq) // DSA_SEG + 1
    for ns in range(1, nseg_max + 1):
        @pl.when(nseg == ns)
        def _(ns=ns):
            _dsa_body(dq_ref, iq_ref, iw_ref, kvi_ref, vt_ref, tri_ref, o_ref, sc_ref, bias_ref, s_ref,
                      tq=tq, topk=topk, nseg=ns)


def _dsa_body(dq_ref, iq_ref, iw_ref, kvi_ref, vt_ref, tri_ref, o_ref, sc_ref, bias_ref, s_ref, *, tq, topk, nseg):
    qs = pl.program_id(1) * tq
    blocks = [slice(c * DSA_KB, (c + 1) * DSA_KB) for c in range(nseg)]
    iq = iq_ref[0]
    w_t = iw_ref[0].T
    w_row = [w_t[IW_LANE + h:IW_LANE + h + 1, :] * (IDX_HEADS ** -0.5) for h in range(IDX_HEADS)]
    q_chunk = (qs + lax.broadcasted_iota(I32, (DSA_KB, tq), 1)) // CHUNK
    k_chunk = lax.broadcasted_iota(I32, (DSA_KB, tq), 0) // CHUNK

    for c, blk in enumerate(blocks):
        ik = kvi_ref[0, blk, 128:192]
        lg = [_dot_nt(ik, iq[:, h * HEAD_DIM:(h + 1) * HEAD_DIM]) for h in range(IDX_HEADS)]
        sc = jnp.zeros((DSA_KB, tq), F32)
        for h in range(IDX_HEADS):
            sc = sc + w_row[h] * jnp.maximum(lg[h], 0.0)
        admissible = (c * (DSA_KB // CHUNK) + k_chunk) <= q_chunk
        sc_ref[blk, :] = jnp.where(admissible, sc, -jnp.inf)

    def count(pred_fn):
        sub, lanes_of_sums = 32, 4
        acc = [jnp.zeros((sub, tq), F32)] * lanes_of_sums
        for j in range(nseg * DSA_SEG // sub):
            acc[j % lanes_of_sums] = acc[j % lanes_of_sums] + jnp.where(
                pred_fn(sc_ref[j * sub:(j + 1) * sub, :]), 1.0, 0.0)
        return jnp.sum(_tree_sum(acc), axis=0, keepdims=True)

    def key_to_float(key):
        return lax.bitcast_convert_type(jnp.where(key >= 0, key, key ^ 0x7FFFFFFF), F32)

    kf = jnp.float32(topk)
    n_rows = jnp.float32(nseg * DSA_SEG)
    cnt0 = count(lambda s: s >= 0.0)
    t0 = jnp.where(cnt0 >= kf, 0, INT_MIN).astype(I32)
    cnt_t0 = jnp.where(cnt0 >= kf, cnt0, n_rows)

    def bit_step(i, carry):
        t, cnt_t = carry
        cand = t + lax.shift_left(jnp.int32(1), 30 - i)
        cand_f = key_to_float(cand)
        cnt = jnp.where(cand <= KEY_NEG_INF, n_rows, count(lambda s: s >= cand_f))
        take = cnt >= kf
        return jnp.where(take, cand, t), jnp.where(take, cnt, cnt_t)

    thr_key, cnt_thr = lax.fori_loop(0, 31, bit_step, (t0, cnt_t0))
    thr = jnp.where(thr_key <= KEY_NEG_INF, -jnp.inf, key_to_float(thr_key))
    tri = tri_ref[...]

    surplus = jnp.logical_or(jnp.max(cnt_thr) > kf, jnp.min(thr) == -jnp.inf)

    @pl.when(surplus)
    def _():
        need = kf - count(lambda s: s > thr)
        prefix = jnp.zeros((1, tq), F32)
        for blk in blocks:
            sc = sc_ref[blk, :]
            eqf = jnp.where(sc == thr, 1.0, 0.0)
            rank = _dot(tri, eqf.astype(BF16)) + prefix
            tie = jnp.where(rank < need, eqf, 0.0)
            sel = jnp.where(sc > thr, 1.0, tie)
            bias_ref[blk, :] = jnp.where(sc > -jnp.inf, (sel - 1.0) * (-NEG_BIG), NEG_BIG)
            prefix = prefix + jnp.sum(eqf, axis=0, keepdims=True)

    @pl.when(jnp.logical_not(surplus))
    def _():
        for blk in blocks:
            bias_ref[blk, :] = jnp.where(sc_ref[blk, :] >= thr, 0.0, NEG_BIG)

    q8 = dq_ref[0].reshape(DSA_HEADS * tq, HEAD_DIM)
    m = jnp.full((1, DSA_HEADS * tq), NEG_BIG, F32)
    for blk in blocks:
        b = bias_ref[blk, :]
        s = _dot_nt(kvi_ref[0, blk, 0:HEAD_DIM], q8) + jnp.concatenate([b] * DSA_HEADS, axis=1)
        s_ref[blk, :] = s
        m = jnp.maximum(m, jnp.max(s, axis=0, keepdims=True))
    acc = jnp.zeros((128, DSA_HEADS * tq), F32)
    for blk in blocks:
        p = jnp.exp(s_ref[blk, :] - m)
        acc = acc + _dot(vt_ref[0, :, blk], p.astype(BF16))
    for hd in range(DSA_HEADS):
        a = acc[:, hd * tq:(hd + 1) * tq]
        o = (a / a[0:1, :]).T
        o_ref[0, :, hd * HEAD_DIM:(hd + 1) * HEAD_DIM] = o[:, HEAD_DIM:].astype(BF16)


def _dsa_attention(dq, iq, iw, kvi, vt, tq, topk):
    batch, _, seq, _ = dq.shape
    return pl.pallas_call(
        functools.partial(_dsa_kernel, tq=tq, topk=topk, nseg_max=seq // DSA_SEG),
        grid=(batch, seq // tq),
        in_specs=[
            pl.BlockSpec((1, DSA_HEADS, tq, HEAD_DIM), lambda b, i: (b, 0, i, 0)),
            pl.BlockSpec((1, tq, 256), lambda b, i: (b, i, 0)),
            pl.BlockSpec((1, tq, 128), lambda b, i: (b, i, 0)),
            pl.BlockSpec((1, seq, 256), lambda b, i: (b, 0, 0)),
            pl.BlockSpec((1, 128, seq), lambda b, i: (b, 0, 0)),
            pl.BlockSpec((DSA_KB, DSA_KB), lambda b, i: (0, 0)),
        ],
        out_specs=pl.BlockSpec((1, tq, BRANCH_W), lambda b, i: (b, i, 0)),
        out_shape=jax.ShapeDtypeStruct((batch, seq, BRANCH_W), BF16),
        scratch_shapes=[pltpu.VMEM((seq, tq), F32), pltpu.VMEM((seq, tq), F32),
                        pltpu.VMEM((seq, DSA_HEADS * tq), F32)],
        compiler_params=_cparams(("parallel", "parallel")),
        name="dsa_attention",
    )(dq, iq.reshape(batch, seq, 256), iw.reshape(batch, seq, 128), kvi.reshape(batch, seq, 256), vt,
      _strict_tri(DSA_KB, lower=True))


def _memkv_kernel(m_ref, g_ref, w_ref, gk_ref, bd_ref, mk_ref, mv_ref):
    x = m_ref[...]
    ms = jnp.mean(x * x, axis=-1, keepdims=True)
    h = (x * lax.rsqrt(ms + EPS) * g_ref[...]).astype(BF16)
    mw = MEM_HEADS * MEM_HEAD_DIM
    k = _dot(h, w_ref[:, :mw])
    msk = _dot_split(k * k, bd_ref[...]) * (1.0 / MEM_HEAD_DIM)
    mk_ref[...] = (k * lax.rsqrt(msk + EPS) * gk_ref[...]).astype(BF16)
    mv_ref[...] = _dot(h, w_ref[:, mw:]).astype(BF16)


def _mem_kv(mem2d, g_mem, w_mem_kv, g_k_mem, tm):
    n = mem2d.shape[0]
    mw = MEM_HEADS * MEM_HEAD_DIM
    row = lambda i: (i, 0)
    const = lambda i: (0, 0)
    return pl.pallas_call(
        _memkv_kernel,
        grid=(n // tm,),
        in_specs=[pl.BlockSpec((tm, D_MODEL), row), pl.BlockSpec((1, D_MODEL), const),
                  pl.BlockSpec((D_MODEL, 2 * mw), const), pl.BlockSpec((1, mw), const),
                  pl.BlockSpec((mw, mw), const)],
        out_specs=[pl.BlockSpec((tm, mw), row), pl.BlockSpec((tm, mw), row)],
        out_shape=[jax.ShapeDtypeStruct((n, mw), BF16), jax.ShapeDtypeStruct((n, mw), BF16)],
        compiler_params=_cparams(("parallel",)),
        name="mem_kv",
    )(mem2d, g_mem[None, :], w_mem_kv.astype(BF16), jnp.tile(g_k_mem, MEM_HEADS)[None, :],
      _block_diag_ones(mw, MEM_HEAD_DIM))


def _memattn_kernel(q_ref, k_ref, v_ref, o_ref):
    for hd in range(MEM_HEADS):
        sl = slice(hd * MEM_HEAD_DIM, (hd + 1) * MEM_HEAD_DIM)
        s = _dot_nt(q_ref[0, :, sl], k_ref[0, :, sl]) * (MEM_HEAD_DIM ** -0.5)
        p = jnp.exp(s - jnp.max(s, axis=-1, keepdims=True))
        o = _dot(p.astype(BF16), v_ref[0, :, sl]) / jnp.sum(p, axis=-1, keepdims=True)
        o_ref[0, :, sl] = o.astype(BF16)


def _mem_attention(mq, mk, mv, tq):
    batch, seq, mw = mq.shape
    mlen = mk.shape[1]
    kspec = pl.BlockSpec((1, mlen, mw), lambda b, i: (b, 0, 0))
    return pl.pallas_call(
        _memattn_kernel,
        grid=(batch, seq // tq),
        in_specs=[pl.BlockSpec((1, tq, mw), lambda b, i: (b, i, 0)), kspec, kspec],
        out_specs=pl.BlockSpec((1, tq, mw), lambda b, i: (b, i, 0)),
        out_shape=jax.ShapeDtypeStruct((batch, seq, mw), BF16),
        compiler_params=_cparams(("parallel", "parallel")),
        name="mem_attention",
    )(mq, mk, mv)


def _merge_kernel(x_ref, g_ref, osb_ref, odsa_ref, omem_ref, wg_ref, bg_ref, wb_ref, wo_ref,
                  gf_ref, wrh_ref, wrl_ref, br_ref, tri_ref,
                  x2_ref, h2_ref, idx_ref, gate_ref, rank_ref, cnt_ref, carry_ref):
    x = x_ref[...]
    ms = jnp.mean(x * x, axis=-1, keepdims=True)
    h = (x * lax.rsqrt(ms + EPS) * g_ref[...]).astype(BF16)
    merged = None
    for n, o_ref in enumerate((osb_ref, odsa_ref, omem_ref)):
        gate = jax.nn.sigmoid(_dot(h, wg_ref[n]) + bg_ref[n])
        term = gate * _dot(o_ref[...], wb_ref[n])
        merged = term if merged is None else merged + term
    x2 = x + _dot(merged.astype(BF16), wo_ref[...])
    x2_ref[...] = x2

    ms2 = jnp.mean(x2 * x2, axis=-1, keepdims=True)
    h2 = x2 * lax.rsqrt(ms2 + EPS) * gf_ref[...]
    _store_token_tiles(h2_ref, h2)
    hi, lo = _split_bf16(h2)
    logits = _dot(hi, wrh_ref[...]) + _dot(hi, wrl_ref[...]) + _dot(lo, wrh_ref[...]) + br_ref[...]

    lane = lax.broadcasted_iota(I32, logits.shape, 1)
    vals = logits
    top_v, top_i = [], []
    for _ in range(TOP_K):
        mx = jnp.max(vals, axis=-1, keepdims=True)
        ix = jnp.min(jnp.where(vals == mx, lane, N_EXPERTS), axis=-1, keepdims=True)
        top_v.append(mx)
        top_i.append(ix)
        vals = jnp.where(lane == ix, -jnp.inf, vals)
    ex = [jnp.exp(v - top_v[0]) for v in top_v]
    den = ex[0] + ex[1] + ex[2] + ex[3]
    for k in range(TOP_K):
        idx_ref[:, k:k + 1] = top_i[k]
        gate_ref[:, k:k + 1] = ex[k] / den

    @pl.when(pl.program_id(0) == 0)
    def _():
        carry_ref[...] = jnp.zeros_like(carry_ref)

    lane_e = lax.broadcasted_iota(I32, (x.shape[0], LANES), 1)
    hits = [lane_e == top_i[k] for k in range(TOP_K)]
    onehot = sum(jnp.where(hk, 1.0, 0.0) for hk in hits)
    before = _dot(tri_ref[...], onehot.astype(BF16)) + carry_ref[...]
    for k in range(TOP_K):
        rank_ref[:, k:k + 1] = jnp.sum(jnp.where(hits[k], before, 0.0), axis=-1, keepdims=True).astype(I32)
    carry_ref[...] += jnp.sum(onehot, axis=0, keepdims=True)
    cnt_ref[...] = carry_ref[...]


def _merge_route(x2d, g_mix, o_sb, o_dsa, o_mem, w_gate, b_gate, w_branch, w_out, g_ffn, w_router, b_router, tm):
    n = x2d.shape[0]
    row = lambda i: (i, 0)
    c2 = lambda i: (0, 0)
    c3 = lambda i: (0, 0, 0)
    wr_hi = w_router.astype(BF16)
    wr_lo = (w_router - wr_hi.astype(F32)).astype(BF16)
    return pl.pallas_call(
        _merge_kernel,
        grid=(n // tm,),
        in_specs=[pl.BlockSpec((tm, D_MODEL), row), pl.BlockSpec((1, D_MODEL), c2),
                  pl.BlockSpec((tm, BRANCH_W), row), pl.BlockSpec((tm, BRANCH_W), row),
                  pl.BlockSpec((tm, BRANCH_W), row),
                  pl.BlockSpec((3, D_MODEL, D_MODEL), c3, pipeline_mode=pl.Buffered(1)),
                  pl.BlockSpec((3, 1, D_MODEL), c3),
                  pl.BlockSpec((3, BRANCH_W, D_MODEL), c3, pipeline_mode=pl.Buffered(1)),
                  pl.BlockSpec((D_MODEL, D_MODEL), c2, pipeline_mode=pl.Buffered(1)),
                  pl.BlockSpec((1, D_MODEL), c2), pl.BlockSpec((D_MODEL, N_EXPERTS), c2),
                  pl.BlockSpec((D_MODEL, N_EXPERTS), c2), pl.BlockSpec((1, N_EXPERTS), c2),
                  pl.BlockSpec((tm, tm), c2, pipeline_mode=pl.Buffered(1))],
        out_specs=[pl.BlockSpec((tm, D_MODEL), row), pl.BlockSpec((tm * TOKEN_TILE, LANES), row),
                   pl.BlockSpec((tm, TOP_K), row), pl.BlockSpec((tm, TOP_K), row),
                   pl.BlockSpec((tm, TOP_K), row), pl.BlockSpec((1, LANES), c2)],
        out_shape=[jax.ShapeDtypeStruct((n, D_MODEL), F32), jax.ShapeDtypeStruct((n * TOKEN_TILE, LANES), F32),
                   jax.ShapeDtypeStruct((n, TOP_K), I32), jax.ShapeDtypeStruct((n, TOP_K), F32),
                   jax.ShapeDtypeStruct((n, TOP_K), I32), jax.ShapeDtypeStruct((1, LANES), F32)],
        scratch_shapes=[pltpu.VMEM((1, LANES), F32)],
        compiler_params=_cparams(("arbitrary",)),
        name="merge_route",
    )(x2d, g_mix[None, :], o_sb, o_dsa, o_mem, w_gate.astype(BF16), b_gate[:, None, :],
      w_branch.astype(BF16), w_out.astype(BF16), g_ffn[None, :], wr_hi, wr_lo, b_router[None, :],
      _strict_tri(tm, lower=True))


DISPATCH_TOKENS = 2048
COMBINE_TOKENS = 512


def _dispatch_kernel(dest_ref, pad_ref, h_ref, xs_ref, zero_ref, sem, *, n_tok, n_pad):
    n_copies = n_tok * TOP_K + n_pad
    zero_ref[...] = jnp.zeros_like(zero_ref)

    def fill(j, c):
        for r in range(2):
            dst = pl.multiple_of(pad_ref[0, 0, 2 * j + r] * TOKEN_TILE, TOKEN_TILE)
            pltpu.make_async_copy(zero_ref, xs_ref.at[pl.ds(dst, TOKEN_TILE)], sem).start(priority=r)
        return c

    lax.fori_loop(0, n_pad // 2, fill, 0, unroll=2)

    def start(t, c):
        src = pl.multiple_of(t * TOKEN_TILE, TOKEN_TILE)
        for k in range(TOP_K):
            dst = pl.multiple_of(dest_ref[0, 0, t * TOP_K + k] * TOKEN_TILE, TOKEN_TILE)
            pltpu.make_async_copy(h_ref.at[pl.ds(src, TOKEN_TILE)], xs_ref.at[pl.ds(dst, TOKEN_TILE)],
                                  sem).start(priority=k % 2)
        return c

    lax.fori_loop(0, n_tok, start, 0, unroll=2)
    total = n_copies * TOKEN_TILE
    pltpu.make_async_copy(xs_ref.at[pl.ds(0, total)], xs_ref.at[pl.ds(0, total)], sem).wait()


def _dispatch(dest, pad_slots, h2t, n_slots):
    n = h2t.shape[0] // TOKEN_TILE
    n_tok = min(DISPATCH_TOKENS, n)
    steps = n // n_tok
    per = n_tok * TOP_K
    n_pad = pad_slots.shape[0] // steps
    assert n_tok * steps == n and n_pad * steps == pad_slots.shape[0] and n_pad % 2 == 0
    return pl.pallas_call(
        functools.partial(_dispatch_kernel, n_tok=n_tok, n_pad=n_pad),
        grid=(steps,),
        in_specs=[pl.BlockSpec((1, 1, per), lambda i: (i, 0, 0), memory_space=pltpu.SMEM),
                  pl.BlockSpec((1, 1, n_pad), lambda i: (i, 0, 0), memory_space=pltpu.SMEM),
                  pl.BlockSpec((n_tok * TOKEN_TILE, LANES), lambda i: (i, 0))],
        out_specs=pl.BlockSpec(memory_space=pl.ANY),
        out_shape=jax.ShapeDtypeStruct((n_slots * TOKEN_TILE, LANES), F32),
        scratch_shapes=[pltpu.VMEM((TOKEN_TILE, LANES), F32), pltpu.SemaphoreType.DMA(())],
        compiler_params=_cparams(("arbitrary",)),
        name="moe_dispatch",
    )(dest.reshape(steps, 1, per), pad_slots.reshape(steps, 1, n_pad), h2t)


def _expert_kernel(blk_e_ref, nused_ref, x_ref, w1_ref, b1_ref, w2_ref, b2_ref, y_ref, xb_ref, w1b_ref, w2b_ref):
    i = pl.program_id(0)
    new_expert = jnp.logical_or(i == 0, blk_e_ref[i] != blk_e_ref[jnp.maximum(i - 1, 0)])

    @pl.when(jnp.logical_and(i < nused_ref[0], new_expert))
    def _():
        w1b_ref[...] = w1_ref[0].astype(BF16)
        w2b_ref[...] = w2_ref[0].astype(BF16)

    @pl.when(i < nused_ref[0])
    def _():
        for s in range(TOKEN_TILE):
            xb_ref[:, s * LANES:(s + 1) * LANES] = _load_token_tiles(x_ref, 0, MOE_BLOCK, s).astype(BF16)
        hb = _dot(xb_ref[...], w1b_ref[...]) + b1_ref[0]
        g = jnp.minimum(hb[:, :D_EXPERT], SWIGLU_LIMIT)
        u = jnp.clip(hb[:, D_EXPERT:], -SWIGLU_LIMIT, SWIGLU_LIMIT)
        act = (u + 1.0) * (g * jax.nn.sigmoid(SWIGLU_ALPHA * g))
        _store_token_tiles(y_ref, _dot(act.astype(BF16), w2b_ref[...]) + b2_ref[0])

    @pl.when(pl.program_id(0) >= nused_ref[0])
    def _():
        y_ref[...] = jnp.zeros_like(y_ref)


def _experts(blk_e, n_used, xs, w_e_in, b_e_in, w_e_out, b_e_out):
    n_slots = xs.shape[0] // TOKEN_TILE
    nblk = n_slots // MOE_BLOCK
    slot_block = pl.BlockSpec((MOE_BLOCK * TOKEN_TILE, LANES), lambda i, be, nu: (i, 0))
    grid_spec = pltpu.PrefetchScalarGridSpec(
        num_scalar_prefetch=2,
        grid=(nblk,),
        in_specs=[slot_block,
                  pl.BlockSpec((1, D_MODEL, 2 * D_EXPERT), lambda i, be, nu: (be[i], 0, 0)),
                  pl.BlockSpec((1, 1, 2 * D_EXPERT), lambda i, be, nu: (be[i], 0, 0)),
                  pl.BlockSpec((1, D_EXPERT, D_MODEL), lambda i, be, nu: (be[i], 0, 0)),
                  pl.BlockSpec((1, 1, D_MODEL), lambda i, be, nu: (be[i], 0, 0))],
        out_specs=slot_block,
        scratch_shapes=[pltpu.VMEM((MOE_BLOCK, D_MODEL), BF16), pltpu.VMEM((D_MODEL, 2 * D_EXPERT), BF16),
                        pltpu.VMEM((D_EXPERT, D_MODEL), BF16)],
    )
    return pl.pallas_call(
        _expert_kernel,
        grid_spec=grid_spec,
        out_shape=jax.ShapeDtypeStruct((n_slots * TOKEN_TILE, LANES), F32),
        compiler_params=_cparams(("arbitrary",)),
        name="moe_experts",
    )(blk_e, n_used, xs, w_e_in, b_e_in[:, None, :], w_e_out, b_e_out[:, None, :])


def _combine_kernel(dest_ref, dest_one_ref, dest_ahead_ref, x_ref, gate_ref, y_ref, o_ref, buf_ref, sem):
    n_copies = COMBINE_TOKENS * TOP_K
    rows = 32
    i = pl.program_id(0)
    slot, slot1, slot2 = i % 3, (i + 1) % 3, (i + 2) % 3
    buf = buf_ref.at[slot]

    def start_rows(idx_ref, dst_buf, dma_sem, t0):
        for t in range(rows):
            for k in range(TOP_K):
                src = pl.multiple_of(idx_ref[0, 0, (t0 + t) * TOP_K + k] * TOKEN_TILE, TOKEN_TILE)
                dst = pl.multiple_of((k * COMBINE_TOKENS + t0 + t) * TOKEN_TILE, TOKEN_TILE)
                pltpu.make_async_copy(y_ref.at[pl.ds(src, TOKEN_TILE)], dst_buf.at[pl.ds(dst, TOKEN_TILE)],
                                      dma_sem).start(priority=k % 2)

    def wait_rows(dst_buf, dma_sem):
        pltpu.make_async_copy(y_ref.at[pl.ds(0, n_copies * TOKEN_TILE)], dst_buf, dma_sem).wait()

    @pl.when(i == 0)
    def _():
        def first(c, z):
            start_rows(dest_ref, buf_ref.at[0], sem.at[0], c * rows)
            start_rows(dest_one_ref, buf_ref.at[1], sem.at[1], c * rows)
            return z
        lax.fori_loop(0, COMBINE_TOKENS // rows, first, 0)

    wait_rows(buf, sem.at[slot])

    def sum_pass(c, z):
        r0 = pl.multiple_of(c * rows, rows)
        start_rows(dest_ahead_ref, buf_ref.at[slot2], sem.at[slot2], r0)
        gate = gate_ref[pl.ds(r0, rows), :]
        gates = [jnp.broadcast_to(gate[:, k:k + 1], (rows, LANES)) for k in range(TOP_K)]
        for s in range(TOKEN_TILE):
            out = x_ref[pl.ds(r0, rows), s * LANES:(s + 1) * LANES]
            for k in range(TOP_K):
                out = out + gates[k] * _load_token_tiles(buf, k * COMBINE_TOKENS + r0, rows, s)
            o_ref[pl.ds(r0, rows), s * LANES:(s + 1) * LANES] = out
        return z

    lax.fori_loop(0, COMBINE_TOKENS // rows, sum_pass, 0)

    @pl.when(i == pl.num_programs(0) - 1)
    def _():
        wait_rows(buf_ref.at[slot1], sem.at[slot1])
        wait_rows(buf_ref.at[slot2], sem.at[slot2])


def _combine(dest, x2, gate, ys):
    n = x2.shape[0]
    steps = n // COMBINE_TOKENS
    per = COMBINE_TOKENS * TOP_K
    dest3 = dest.reshape(steps, 1, per)
    return pl.pallas_call(
        _combine_kernel,
        grid=(steps,),
        in_specs=[pl.BlockSpec((1, 1, per), lambda i: (i, 0, 0), memory_space=pltpu.SMEM),
                  pl.BlockSpec((1, 1, per), lambda i: (min(1, steps - 1), 0, 0), memory_space=pltpu.SMEM),
                  pl.BlockSpec((1, 1, per), lambda i: (jnp.minimum(i + 2, steps - 1), 0, 0), memory_space=pltpu.SMEM),
                  pl.BlockSpec((COMBINE_TOKENS, D_MODEL), lambda i: (i, 0)),
                  pl.BlockSpec((COMBINE_TOKENS, TOP_K), lambda i: (i, 0)),
                  pl.BlockSpec(memory_space=pl.ANY)],
        out_specs=pl.BlockSpec((COMBINE_TOKENS, D_MODEL), lambda i: (i, 0)),
        out_shape=jax.ShapeDtypeStruct((n, D_MODEL), F32),
        scratch_shapes=[pltpu.VMEM((3, TOP_K * COMBINE_TOKENS * TOKEN_TILE, LANES), F32),
                        pltpu.SemaphoreType.DMA((3,))],
        compiler_params=_cparams(("arbitrary",)),
        name="moe_combine",
    )(dest3, dest3, dest3, x2, gate, ys)


def _moe(x2, h2, top_idx, gate, rank, counts, w_e_in, b_e_in, w_e_out, b_e_out):
    n = x2.shape[0]
    counts = counts[0, :N_EXPERTS].astype(I32)
    padded = (counts + MOE_BLOCK - 1) // MOE_BLOCK * MOE_BLOCK
    pend = jnp.cumsum(padded)
    pstart = pend - padded
    nblk = -(-(n * TOP_K) // MOE_BLOCK) + N_EXPERTS
    blk_start = jnp.arange(nblk, dtype=I32) * MOE_BLOCK
    blk_e = jnp.minimum(jnp.sum((pend[None, :] <= blk_start[:, None]).astype(I32), axis=1), N_EXPERTS - 1)
    n_used = (pend[-1:] // MOE_BLOCK).astype(I32)
    onehot = top_idx[:, :, None] == jnp.arange(N_EXPERTS, dtype=I32)[None, None, :]
    dest = rank + jnp.sum(jnp.where(onehot, pstart[None, None, :], 0), axis=-1)
    n_slots = nblk * MOE_BLOCK
    pad_len = padded - counts
    pad_end = jnp.cumsum(pad_len)
    base = jnp.concatenate([pstart + counts - (pad_end - pad_len), pend[-1:] - pad_end[-1:]])
    j = jnp.arange(n_slots - n * TOP_K, dtype=I32)
    group = jnp.sum((pad_end[None, :] <= j[:, None]).astype(I32), axis=1)
    group_hot = group[:, None] == jnp.arange(N_EXPERTS + 1, dtype=I32)[None, :]
    pad_slots = j + jnp.sum(jnp.where(group_hot, base[None, :], 0), axis=1)
    xs = _dispatch(dest, pad_slots, h2, n_slots)
    ys = _experts(blk_e, n_used, xs, w_e_in, b_e_in, w_e_out, b_e_out)
    return _combine(dest, x2, gate, ys)


def _layer(x, mem, g_mix, w_in, g_q_dsa, g_k_dsa, g_q_mem, g_k_mem, g_mem, w_mem_kv, w_gate, b_gate,
           w_branch, w_out, g_ffn, w_router, b_router, w_e_in, b_e_in, w_e_out, b_e_out):
    batch, seq, _ = x.shape
    n = batch * seq
    topk = min(DSA_TOPK_MAX, seq // 4)
    x2d = x.reshape(n, D_MODEL)
    tm_dense = min(DENSE_ROWS, seq)
    sq, sk, sv, dq, iq, kvi, iw, mq, vt = _inproj(x2d, batch, seq, g_mix, w_in, g_q_dsa, g_k_dsa, g_q_mem, tm_dense)
    o_sb = _sb_attention(sq, sk, sv, min(256, seq))
    o_dsa = _dsa_attention(dq, iq, iw, kvi, vt, 128, topk)
    mlen = mem.shape[1]
    mk, mv = _mem_kv(mem.reshape(batch * mlen, D_MODEL), g_mem, w_mem_kv, g_k_mem, min(512, batch * mlen))
    mw = MEM_HEADS * MEM_HEAD_DIM
    o_mem = _mem_attention(mq.reshape(batch, seq, mw), mk.reshape(batch, mlen, mw), mv.reshape(batch, mlen, mw), seq)
    x2, h2, top_idx, gate, rank, counts = _merge_route(
        x2d, g_mix, o_sb.reshape(n, BRANCH_W), o_dsa.reshape(n, BRANCH_W), o_mem.reshape(n, mw),
        w_gate, b_gate, w_branch, w_out, g_ffn, w_router, b_router, tm_dense)
    out = _moe(x2, h2, top_idx, gate, rank, counts, w_e_in, b_e_in, w_e_out, b_e_out)
    return out.reshape(batch, seq, D_MODEL)


def kernel(x, mem, g_mix, w_in, g_q_dsa, g_k_dsa, g_q_mem, g_k_mem, g_mem, w_mem_kv, w_gate, b_gate, w_branch, w_out, g_ffn, w_router, b_router, w_e_in, b_e_in, w_e_out, b_e_out):
    for l in range(g_mix.shape[0]):
        x = _layer(x, mem, g_mix[l], w_in[l], g_q_dsa[l], g_k_dsa[l], g_q_mem[l], g_k_mem[l], g_mem[l],
                   w_mem_kv[l], w_gate[l], b_gate[l], w_branch[l], w_out[l], g_ffn[l], w_router[l],
                   b_router[l], w_e_in[l], b_e_in[l], w_e_out[l], b_e_out[l])
    return x
```

```python
import functools

import numpy as np
import jax
import jax.numpy as jnp
from jax import lax
from jax.experimental import pallas as pl
from jax.experimental.pallas import tpu as pltpu

F32 = jnp.float32
BF16 = jnp.bfloat16
I32 = jnp.int32

D_MODEL = 1024
CHUNK = 64
SB_HEADS = 8
DSA_HEADS = 8
HEAD_DIM = 64
IDX_HEADS = 4
DSA_TOPK_MAX = 256
MEM_HEADS = 4
MEM_HEAD_DIM = 128
N_EXPERTS = 32
TOP_K = 4
D_EXPERT = D_MODEL
SWIGLU_LIMIT = 7.0
SWIGLU_ALPHA = 1.702
ROPE_THETA = 10000.0
EPS = 1e-6
MOE_BLOCK = 512

BRANCH_W = 512
IN_SIZES = (512, 512, 512, 512, 64, 64, 256, 64, 4, 512)
C_SQ, C_SK, C_SV, C_DQ, C_MQ, C_IQ, C_SMALL, C_END = 0, 512, 1024, 1536, 2048, 2560, 2816, 3072
IW_LANE = 64

LANES = 128
NEG_BIG = -1e30
SB_CUTOFF = 110.0
KEY_NEG_INF = int(np.array(-np.inf, np.float32).view(np.int32)) ^ 0x7FFFFFFF
INT_MIN = -(2 ** 31)

VMEM_LIMIT = 56 * 1024 * 1024
DENSE_ROWS = 1024


def _cparams(sem):
    return pltpu.CompilerParams(dimension_semantics=sem, vmem_limit_bytes=VMEM_LIMIT)


def _dot(a, b):
    return jnp.dot(a, b, preferred_element_type=F32)


def _dot_nt(a, b):
    return lax.dot_general(a, b, (((1,), (1,)), ((), ())), preferred_element_type=F32)


def _split_bf16(x):
    hi = x.astype(BF16)
    lo = (x - hi.astype(F32)).astype(BF16)
    return hi, lo


def _dot_split(x, m_bf16):
    hi, lo = _split_bf16(x)
    return _dot(hi, m_bf16) + _dot(lo, m_bf16)


TOKEN_TILE = D_MODEL // LANES


def _store_token_tiles(ref, y):
    rows = y.shape[0]
    for s in range(TOKEN_TILE):
        ref[pl.ds(s, rows, stride=TOKEN_TILE), :] = y[:, s * LANES:(s + 1) * LANES]


def _load_token_tiles(ref, start_row, rows, s):
    return ref[pl.ds(start_row * TOKEN_TILE + s, rows, stride=TOKEN_TILE), :]


def _rot_half_unsigned(y):
    w = y.shape[1]
    lane = lax.broadcasted_iota(I32, y.shape, 1)
    return jnp.where((lane & 32) == 0, pltpu.roll(y, w - 32, 1), pltpu.roll(y, 32, 1))


def _inproj_kernel(x_ref, g_ref, w_ref, wsvt_ref, cos_ref, sin_ref, coss_ref, sins_ref, gq_ref, gks_ref,
                   gm_ref, bd64_ref, bd128_ref,
                   sq_ref, sk_ref, sv_ref, dq_ref, iq_ref, kvi_ref, iw_ref, mq_ref, vt_ref):
    x = x_ref[...]
    ms = jnp.mean(x * x, axis=-1, keepdims=True)
    h = (x * lax.rsqrt(ms + EPS) * g_ref[...]).astype(BF16)

    def seg(a, b):
        return _dot(h, w_ref[:, a:b])

    def put_heads(ref, y):
        for hd in range(y.shape[1] // HEAD_DIM):
            ref[0, hd] = y[:, hd * HEAD_DIM:(hd + 1) * HEAD_DIM].astype(BF16)

    put_heads(sq_ref, seg(C_SQ, C_SK) * (HEAD_DIM ** -0.5))
    put_heads(sk_ref, seg(C_SK, C_SV))
    sv_ref[0] = _dot_nt(wsvt_ref[...], h).astype(BF16)

    y = seg(C_DQ, C_MQ)
    msq = _dot_split(y * y, bd64_ref[...]) * (1.0 / HEAD_DIM)
    y = y * lax.rsqrt(msq + EPS) * gq_ref[...]
    y = y * cos_ref[...] + _rot_half_unsigned(y) * sin_ref[...]
    put_heads(dq_ref, y * (HEAD_DIM ** -0.5))

    y = seg(C_IQ, C_SMALL)
    y = y * cos_ref[:, :256] + _rot_half_unsigned(y) * sin_ref[:, :256]
    iq_ref[...] = (y * (HEAD_DIM ** -0.5)).astype(BF16)

    y = seg(C_SMALL, C_END)
    lane = lax.broadcasted_iota(I32, y.shape, 1)
    is_k = lane < HEAD_DIM
    msk = jnp.sum(jnp.where(is_k, y * y, 0.0), axis=-1, keepdims=True) * (1.0 / HEAD_DIM)
    y = y * jnp.where(is_k, lax.rsqrt(msk + EPS) * gks_ref[...], 1.0)
    y = y * coss_ref[...] + _rot_half_unsigned(y) * sins_ref[...]
    kvi_ref[...] = y.astype(BF16)
    iw_ref[...] = y[:, 128:256]
    kv_t = y[:, 0:128].T
    row_t = lax.broadcasted_iota(I32, kv_t.shape, 0)
    vt_ref[0] = jnp.where(row_t < HEAD_DIM, 1.0, kv_t).astype(BF16)

    y = seg(C_MQ, C_IQ)
    msm = _dot_split(y * y, bd128_ref[...]) * (1.0 / MEM_HEAD_DIM)
    mq_ref[...] = (y * lax.rsqrt(msm + EPS) * gm_ref[...]).astype(BF16)


def _rope_tables(seq):
    half = HEAD_DIM // 2
    inv = ROPE_THETA ** (-jnp.arange(half, dtype=F32) / half)
    ang = jnp.arange(seq).astype(F32)[:, None] * inv[None, :]
    cos = jnp.cos(ang)
    sin = jnp.sin(ang)
    cos64 = jnp.concatenate([cos, cos], axis=1)
    sin64 = jnp.concatenate([-sin, sin], axis=1)
    one = jnp.ones_like(cos64)
    zero = jnp.zeros_like(cos64)
    cosq = jnp.tile(cos64, (1, 8))
    sinq = jnp.tile(sin64, (1, 8))
    coss = jnp.concatenate([cos64, one, cos64, one], axis=1)
    sins = jnp.concatenate([sin64, zero, sin64, zero], axis=1)
    return cosq, sinq, coss, sins


def _block_diag_ones(width, group):
    idx = np.arange(width) // group
    return jnp.asarray((idx[:, None] == idx[None, :]).astype(np.float32), dtype=BF16)


def _inproj(x2d, batch, seq, g_mix, w_in, g_q_dsa, g_k_dsa, g_q_mem, tm):
    n = x2d.shape[0]
    sizes = np.cumsum((0,) + IN_SIZES)
    col = {name: (int(sizes[i]), int(sizes[i + 1])) for i, name in enumerate(
        ("sq", "sk", "sv", "dq", "dk", "dv", "iq", "ik", "iw", "mq"))}
    order = ("sq", "sk", "sv", "dq", "mq", "iq", "dk", "dv", "ik", "iw")
    w = jnp.concatenate([w_in[:, col[k][0]:col[k][1]] for k in order]
                        + [jnp.zeros((D_MODEL, C_END - sum(IN_SIZES)), w_in.dtype)], axis=1).astype(BF16)
    cosq, sinq, coss, sins = (jnp.asarray(t) for t in _rope_tables(seq))
    gq = jnp.tile(g_q_dsa, 8)[None, :]
    gks = jnp.concatenate([g_k_dsa, jnp.ones((256 - HEAD_DIM,), F32)])[None, :]
    gm = jnp.tile(g_q_mem, MEM_HEADS)[None, :]
    spb = seq // tm
    row = lambda i: (i, 0)
    const = lambda i: (0, 0)
    pos = lambda i: (i % spb, 0)
    heads = lambda i: (i // spb, 0, i % spb, 0)
    head_shape = jax.ShapeDtypeStruct((batch, 8, seq, HEAD_DIM), BF16)
    head_spec = pl.BlockSpec((1, 8, tm, HEAD_DIM), heads)
    return pl.pallas_call(
        _inproj_kernel,
        grid=(n // tm,),
        in_specs=[
            pl.BlockSpec((tm, D_MODEL), row),
            pl.BlockSpec((1, D_MODEL), const),
            pl.BlockSpec((D_MODEL, C_END), const, pipeline_mode=pl.Buffered(1)),
            pl.BlockSpec((512, D_MODEL), const, pipeline_mode=pl.Buffered(1)),
            pl.BlockSpec((tm, 512), pos), pl.BlockSpec((tm, 512), pos),
            pl.BlockSpec((tm, 256), pos), pl.BlockSpec((tm, 256), pos),
            pl.BlockSpec((1, 512), const), pl.BlockSpec((1, 256), const), pl.BlockSpec((1, 512), const),
            pl.BlockSpec((512, 512), const), pl.BlockSpec((512, 512), const),
        ],
        out_specs=[head_spec, head_spec, pl.BlockSpec((1, 512, tm), lambda i: (i // spb, 0, i % spb)), head_spec,
                   pl.BlockSpec((tm, 256), row), pl.BlockSpec((tm, 256), row),
                   pl.BlockSpec((tm, 128), row), pl.BlockSpec((tm, 512), row),
                   pl.BlockSpec((1, 128, tm), lambda i: (i // spb, 0, i % spb))],
        out_shape=[head_shape, head_shape, jax.ShapeDtypeStruct((batch, 512, seq), BF16), head_shape,
                   jax.ShapeDtypeStruct((n, 256), BF16), jax.ShapeDtypeStruct((n, 256), BF16),
                   jax.ShapeDtypeStruct((n, 128), F32), jax.ShapeDtypeStruct((n, 512), BF16),
                   jax.ShapeDtypeStruct((batch, 128, seq), BF16)],
        compiler_params=_cparams(("parallel",)),
        name="inproj",
    )(x2d, g_mix[None, :], w, w_in[:, col["sv"][0]:col["sv"][1]].T.astype(BF16), cosq, sinq, coss, sins, gq, gks, gm,
      _block_diag_ones(512, HEAD_DIM), _block_diag_ones(512, MEM_HEAD_DIM))


def _sb_kernel(q_ref, k_ref, vt_ref, u_ref, o_ref, acc_ref, car_ref, *, tq):
    qi = pl.program_id(1)
    rows = lax.broadcasted_iota(I32, (tq, tq), 0)
    cols = lax.broadcasted_iota(I32, (tq, tq), 1)
    dif = rows - cols
    u = u_ref[...]
    acc_ref[...] = jnp.zeros_like(acc_ref)
    car_ref[...] = jnp.zeros_like(car_ref)

    def cond(c):
        kb, mx = c
        return jnp.logical_and(kb >= 0, mx > -SB_CUTOFF)

    def body(c):
        kb, _ = c
        ks = pl.multiple_of(kb * tq, tq)
        earlier = dif < (qi - kb) * tq
        neg_mask = jnp.where(earlier, -1.0, 0.0).astype(BF16)
        heads = range(SB_HEADS)
        z = [_dot_nt(k_ref[0, hd, pl.ds(ks, tq), :], q_ref[0, hd]) for hd in heads]
        ls, lk, between = [], [], []
        for hd in heads:
            zb = z[hd].astype(BF16)
            sp = jnp.maximum(zb, 0.0) + jnp.log(1.0 + jnp.exp(-jnp.abs(zb)))
            ls.append(z[hd] - sp.astype(F32))
            lk.append(sp * neg_mask)
            between.append(_dot(u, lk[hd]))
        for hd in heads:
            rs = slice(hd * HEAD_DIM, (hd + 1) * HEAD_DIM)
            car = car_ref[hd:hd + 1, :]
            w = jnp.where(earlier, jnp.exp(ls[hd] + between[hd] + car), 0.0)
            acc_ref[rs, :] += _dot(vt_ref[0, rs, pl.ds(ks, tq)], w.astype(BF16))
            car_ref[hd:hd + 1, :] = car + (between[hd][0:1, :] + lk[hd][0:1, :].astype(F32))
        return kb - 1, jnp.max(car_ref[...])

    lax.while_loop(cond, body, (qi, jnp.float32(0.0)))
    o_ref[0] = acc_ref[...].T.astype(BF16)


def _strict_tri(n, lower):
    i = np.arange(n)
    m = (i[:, None] > i[None, :]) if lower else (i[:, None] < i[None, :])
    return jnp.asarray(m.astype(np.float32), dtype=BF16)


def _sb_attention(sq, sk, svt, tq):
    batch, _, seq, _ = sq.shape
    return pl.pallas_call(
        functools.partial(_sb_kernel, tq=tq),
        grid=(batch, seq // tq),
        in_specs=[pl.BlockSpec((1, SB_HEADS, tq, HEAD_DIM), lambda b, i: (b, 0, i, 0)),
                  pl.BlockSpec((1, SB_HEADS, seq, HEAD_DIM), lambda b, i: (b, 0, 0, 0)),
                  pl.BlockSpec((1, BRANCH_W, seq), lambda b, i: (b, 0, 0)),
                  pl.BlockSpec((tq, tq), lambda b, i: (0, 0))],
        out_specs=pl.BlockSpec((1, tq, BRANCH_W), lambda b, i: (b, i, 0)),
        out_shape=jax.ShapeDtypeStruct((batch, seq, BRANCH_W), BF16),
        scratch_shapes=[pltpu.VMEM((BRANCH_W, tq), F32), pltpu.VMEM((SB_HEADS, tq), F32)],
        compiler_params=_cparams(("parallel", "parallel")),
        name="sb_attention",
    )(sq, sk, svt, _strict_tri(tq, lower=False))


DSA_SEG = 256
DSA_KB = DSA_SEG


def _tree_sum(parts):
    while len(parts) > 1:
        parts = [parts[i] + parts[i + 1] for i in range(0, len(parts) - 1, 2)] + ([parts[-1]] if len(parts) % 2 else [])
    return parts[0]


def _dsa_kernel(dq_ref, iq_ref, iw_ref, kvi_ref, vt_ref, tri_ref, o_ref,
                sc_ref, bias_ref, s_ref, *, tq, topk, nseg_max):
    nseg = pl.program_id(1) + 1

    def query_block(r, carry, ns):
        rows = pl.ds(pl.multiple_of(r * tq, tq), tq)
        _dsa_body(dq_ref.at[:, :, rows, :], iq_ref.at[:, rows, :], iw_ref.at[:, rows, :], kvi_ref, vt_ref, tri_ref,
                  o_ref.at[:, rows, :], sc_ref, bias_ref, s_ref,
                  qs=pl.program_id(1) * DSA_SEG + r * tq, tq=tq, topk=topk, nseg=ns)
        return carry

    for ns in range(1, nseg_max + 1):
        @pl.when(nseg == ns)
        def _(ns=ns):
            lax.fori_loop(0, DSA_SEG // tq, functools.partial(query_block, ns=ns), 0)


def _dsa_body(dq_ref, iq_ref, iw_ref, kvi_ref, vt_ref, tri_ref, o_ref, sc_ref, bias_ref, s_ref,
              *, qs, tq, topk, nseg):
    blocks = [slice(c * DSA_KB, (c + 1) * DSA_KB) for c in range(nseg)]
    iq = iq_ref[0]
    w_t = iw_ref[0].T
    w_row = [w_t[IW_LANE + h:IW_LANE + h + 1, :] * (IDX_HEADS ** -0.5) for h in range(IDX_HEADS)]
    q_chunk = (qs + lax.broadcasted_iota(I32, (DSA_KB, tq), 1)) // CHUNK
    k_chunk = lax.broadcasted_iota(I32, (DSA_KB, tq), 0) // CHUNK

    for c, blk in enumerate(blocks):
        ik = kvi_ref[0, blk, 128:192]
        lg = [_dot_nt(ik, iq[:, h * HEAD_DIM:(h + 1) * HEAD_DIM]) for h in range(IDX_HEADS)]
        sc = jnp.zeros((DSA_KB, tq), F32)
        for h in range(IDX_HEADS):
            sc = sc + w_row[h] * jnp.maximum(lg[h], 0.0)
        admissible = (c * (DSA_KB // CHUNK) + k_chunk) <= q_chunk
        sc_ref[blk, :] = jnp.where(admissible, sc, -jnp.inf)

    def count(pred_fn):
        sub, lanes_of_sums = 32, 4
        acc = [jnp.zeros((sub, tq), F32)] * lanes_of_sums
        for j in range(nseg * DSA_SEG // sub):
            acc[j % lanes_of_sums] = acc[j % lanes_of_sums] + jnp.where(
                pred_fn(sc_ref[j * sub:(j + 1) * sub, :]), 1.0, 0.0)
        return jnp.sum(_tree_sum(acc), axis=0, keepdims=True)

    def key_to_float(key):
        return lax.bitcast_convert_type(jnp.where(key >= 0, key, key ^ 0x7FFFFFFF), F32)

    kf = jnp.float32(topk)
    n_rows = jnp.float32(nseg * DSA_SEG)
    cnt0 = count(lambda s: s >= 0.0)
    t0 = jnp.where(cnt0 >= kf, 0, INT_MIN).astype(I32)
    cnt_t0 = jnp.where(cnt0 >= kf, cnt0, n_rows)

    def bit_step(i, carry):
        t, cnt_t = carry
        cand = t + lax.shift_left(jnp.int32(1), 30 - i)
        cand_f = key_to_float(cand)
        cnt = jnp.where(cand <= KEY_NEG_INF, n_rows, count(lambda s: s >= cand_f))
        take = cnt >= kf
        return jnp.where(take, cand, t), jnp.where(take, cnt, cnt_t)

    thr_key, cnt_thr = lax.fori_loop(0, 31, bit_step, (t0, cnt_t0))
    thr = jnp.where(thr_key <= KEY_NEG_INF, -jnp.inf, key_to_float(thr_key))
    tri = tri_ref[...]

    surplus = jnp.logical_or(jnp.max(cnt_thr) > kf, jnp.min(thr) == -jnp.inf)

    @pl.when(surplus)
    def _():
        need = kf - count(lambda s: s > thr)
        prefix = jnp.zeros((1, tq), F32)
        for blk in blocks:
            sc = sc_ref[blk, :]
            eqf = jnp.where(sc == thr, 1.0, 0.0)
            rank = _dot(tri, eqf.astype(BF16)) + prefix
            tie = jnp.where(rank < need, eqf, 0.0)
            sel = jnp.where(sc > thr, 1.0, tie)
            bias_ref[blk, :] = jnp.where(sc > -jnp.inf, (sel - 1.0) * (-NEG_BIG), NEG_BIG)
            prefix = prefix + jnp.sum(eqf, axis=0, keepdims=True)

    @pl.when(jnp.logical_not(surplus))
    def _():
        for blk in blocks:
            bias_ref[blk, :] = jnp.where(sc_ref[blk, :] >= thr, 0.0, NEG_BIG)

    q8 = dq_ref[0].reshape(DSA_HEADS * tq, HEAD_DIM)
    m = jnp.full((1, DSA_HEADS * tq), NEG_BIG, F32)
    for blk in blocks:
        b = bias_ref[blk, :]
        s = _dot_nt(kvi_ref[0, blk, 0:HEAD_DIM], q8) + jnp.concatenate([b] * DSA_HEADS, axis=1)
        s_ref[blk, :] = s
        m = jnp.maximum(m, jnp.max(s, axis=0, keepdims=True))
    acc = jnp.zeros((128, DSA_HEADS * tq), F32)
    for blk in blocks:
        p = jnp.exp(s_ref[blk, :] - m)
        acc = acc + _dot(vt_ref[0, :, blk], p.astype(BF16))
    for hd in range(DSA_HEADS):
        a = acc[:, hd * tq:(hd + 1) * tq]
        o = (a / a[0:1, :]).T
        o_ref[0, :, hd * HEAD_DIM:(hd + 1) * HEAD_DIM] = o[:, HEAD_DIM:].astype(BF16)


def _dsa_attention(dq, iq, iw, kvi, vt, tq, topk):
    batch, _, seq, _ = dq.shape
    return pl.pallas_call(
        functools.partial(_dsa_kernel, tq=tq, topk=topk, nseg_max=seq // DSA_SEG),
        grid=(batch, seq // DSA_SEG),
        in_specs=[
            pl.BlockSpec((1, DSA_HEADS, DSA_SEG, HEAD_DIM), lambda b, i: (b, 0, i, 0)),
            pl.BlockSpec((1, DSA_SEG, 256), lambda b, i: (b, i, 0)),
            pl.BlockSpec((1, DSA_SEG, 128), lambda b, i: (b, i, 0)),
            pl.BlockSpec((1, seq, 256), lambda b, i: (b, 0, 0)),
            pl.BlockSpec((1, 128, seq), lambda b, i: (b, 0, 0)),
            pl.BlockSpec((DSA_KB, DSA_KB), lambda b, i: (0, 0)),
        ],
        out_specs=pl.BlockSpec((1, DSA_SEG, BRANCH_W), lambda b, i: (b, i, 0)),
        out_shape=jax.ShapeDtypeStruct((batch, seq, BRANCH_W), BF16),
        scratch_shapes=[pltpu.VMEM((seq, tq), F32), pltpu.VMEM((seq, tq), F32),
                        pltpu.VMEM((seq, DSA_HEADS * tq), F32)],
        compiler_params=_cparams(("parallel", "parallel")),
        name="dsa_attention",
    )(dq, iq.reshape(batch, seq, 256), iw.reshape(batch, seq, 128), kvi.reshape(batch, seq, 256), vt,
      _strict_tri(DSA_KB, lower=True))


def _memkv_kernel(m_ref, g_ref, w_ref, gk_ref, bd_ref, mk_ref, mv_ref):
    x = m_ref[...]
    ms = jnp.mean(x * x, axis=-1, keepdims=True)
    h = (x * lax.rsqrt(ms + EPS) * g_ref[...]).astype(BF16)
    mw = MEM_HEADS * MEM_HEAD_DIM
    k = _dot(h, w_ref[:, :mw])
    msk = _dot_split(k * k, bd_ref[...]) * (1.0 / MEM_HEAD_DIM)
    mk_ref[...] = (k * lax.rsqrt(msk + EPS) * gk_ref[...]).astype(BF16)
    mv_ref[...] = _dot(h, w_ref[:, mw:]).astype(BF16)


def _mem_kv(mem2d, g_mem, w_mem_kv, g_k_mem, tm):
    n = mem2d.shape[0]
    mw = MEM_HEADS * MEM_HEAD_DIM
    row = lambda i: (i, 0)
    const = lambda i: (0, 0)
    return pl.pallas_call(
        _memkv_kernel,
        grid=(n // tm,),
        in_specs=[pl.BlockSpec((tm, D_MODEL), row), pl.BlockSpec((1, D_MODEL), const),
                  pl.BlockSpec((D_MODEL, 2 * mw), const), pl.BlockSpec((1, mw), const),
                  pl.BlockSpec((mw, mw), const)],
        out_specs=[pl.BlockSpec((tm, mw), row), pl.BlockSpec((tm, mw), row)],
        out_shape=[jax.ShapeDtypeStruct((n, mw), BF16), jax.ShapeDtypeStruct((n, mw), BF16)],
        compiler_params=_cparams(("parallel",)),
        name="mem_kv",
    )(mem2d, g_mem[None, :], w_mem_kv.astype(BF16), jnp.tile(g_k_mem, MEM_HEADS)[None, :],
      _block_diag_ones(mw, MEM_HEAD_DIM))


def _memattn_kernel(q_ref, k_ref, v_ref, o_ref):
    for hd in range(MEM_HEADS):
        sl = slice(hd * MEM_HEAD_DIM, (hd + 1) * MEM_HEAD_DIM)
        s = _dot_nt(q_ref[0, :, sl], k_ref[0, :, sl]) * (MEM_HEAD_DIM ** -0.5)
        p = jnp.exp(s - jnp.max(s, axis=-1, keepdims=True))
        o = _dot(p.astype(BF16), v_ref[0, :, sl]) / jnp.sum(p, axis=-1, keepdims=True)
        o_ref[0, :, sl] = o.astype(BF16)


def _mem_attention(mq, mk, mv, tq):
    batch, seq, mw = mq.shape
    mlen = mk.shape[1]
    kspec = pl.BlockSpec((1, mlen, mw), lambda b, i: (b, 0, 0))
    return pl.pallas_call(
        _memattn_kernel,
        grid=(batch, seq // tq),
        in_specs=[pl.BlockSpec((1, tq, mw), lambda b, i: (b, i, 0)), kspec, kspec],
        out_specs=pl.BlockSpec((1, tq, mw), lambda b, i: (b, i, 0)),
        out_shape=jax.ShapeDtypeStruct((batch, seq, mw), BF16),
        compiler_params=_cparams(("parallel", "parallel")),
        name="mem_attention",
    )(mq, mk, mv)


def _merge_kernel(x_ref, g_ref, osb_ref, odsa_ref, omem_ref, wg_ref, bg_ref, wb_ref, wo_ref,
                  gf_ref, wrh_ref, wrl_ref, br_ref, tri_ref,
                  x2_ref, h2_ref, idx_ref, gate_ref, rank_ref, cnt_ref, carry_ref):
    x = x_ref[...]
    ms = jnp.mean(x * x, axis=-1, keepdims=True)
    h = (x * lax.rsqrt(ms + EPS) * g_ref[...]).astype(BF16)
    merged = None
    for n, o_ref in enumerate((osb_ref, odsa_ref, omem_ref)):
        gate = jax.nn.sigmoid(_dot(h, wg_ref[n]) + bg_ref[n])
        term = gate * _dot(o_ref[...], wb_ref[n])
        merged = term if merged is None else merged + term
    x2 = x + _dot(merged.astype(BF16), wo_ref[...])
    x2_ref[...] = x2

    ms2 = jnp.mean(x2 * x2, axis=-1, keepdims=True)
    h2 = x2 * lax.rsqrt(ms2 + EPS) * gf_ref[...]
    _store_token_tiles(h2_ref, h2)
    hi, lo = _split_bf16(h2)
    logits = _dot(hi, wrh_ref[...]) + _dot(hi, wrl_ref[...]) + _dot(lo, wrh_ref[...]) + br_ref[...]

    lane = lax.broadcasted_iota(I32, logits.shape, 1)
    vals = logits
    top_v, top_i = [], []
    for _ in range(TOP_K):
        mx = jnp.max(vals, axis=-1, keepdims=True)
        ix = jnp.min(jnp.where(vals == mx, lane, N_EXPERTS), axis=-1, keepdims=True)
        top_v.append(mx)
        top_i.append(ix)
        vals = jnp.where(lane == ix, -jnp.inf, vals)
    ex = [jnp.exp(v - top_v[0]) for v in top_v]
    den = ex[0] + ex[1] + ex[2] + ex[3]
    for k in range(TOP_K):
        idx_ref[:, k:k + 1] = top_i[k]
        gate_ref[:, k:k + 1] = ex[k] / den

    @pl.when(pl.program_id(0) == 0)
    def _():
        carry_ref[...] = jnp.zeros_like(carry_ref)

    lane_e = lax.broadcasted_iota(I32, (x.shape[0], LANES), 1)
    hits = [lane_e == top_i[k] for k in range(TOP_K)]
    onehot = sum(jnp.where(hk, 1.0, 0.0) for hk in hits)
    before = _dot(tri_ref[...], onehot.astype(BF16)) + carry_ref[...]
    for k in range(TOP_K):
        rank_ref[:, k:k + 1] = jnp.sum(jnp.where(hits[k], before, 0.0), axis=-1, keepdims=True).astype(I32)
    carry_ref[...] += jnp.sum(onehot, axis=0, keepdims=True)
    cnt_ref[...] = carry_ref[...]


def _merge_route(x2d, g_mix, o_sb, o_dsa, o_mem, w_gate, b_gate, w_branch, w_out, g_ffn, w_router, b_router, tm):
    n = x2d.shape[0]
    row = lambda i: (i, 0)
    c2 = lambda i: (0, 0)
    c3 = lambda i: (0, 0, 0)
    wr_hi = w_router.astype(BF16)
    wr_lo = (w_router - wr_hi.astype(F32)).astype(BF16)
    return pl.pallas_call(
        _merge_kernel,
        grid=(n // tm,),
        in_specs=[pl.BlockSpec((tm, D_MODEL), row), pl.BlockSpec((1, D_MODEL), c2),
                  pl.BlockSpec((tm, BRANCH_W), row), pl.BlockSpec((tm, BRANCH_W), row),
                  pl.BlockSpec((tm, BRANCH_W), row),
                  pl.BlockSpec((3, D_MODEL, D_MODEL), c3, pipeline_mode=pl.Buffered(1)),
                  pl.BlockSpec((3, 1, D_MODEL), c3),
                  pl.BlockSpec((3, BRANCH_W, D_MODEL), c3, pipeline_mode=pl.Buffered(1)),
                  pl.BlockSpec((D_MODEL, D_MODEL), c2, pipeline_mode=pl.Buffered(1)),
                  pl.BlockSpec((1, D_MODEL), c2), pl.BlockSpec((D_MODEL, N_EXPERTS), c2),
                  pl.BlockSpec((D_MODEL, N_EXPERTS), c2), pl.BlockSpec((1, N_EXPERTS), c2),
                  pl.BlockSpec((tm, tm), c2, pipeline_mode=pl.Buffered(1))],
        out_specs=[pl.BlockSpec((tm, D_MODEL), row), pl.BlockSpec((tm * TOKEN_TILE, LANES), row),
                   pl.BlockSpec((tm, TOP_K), row), pl.BlockSpec((tm, TOP_K), row),
                   pl.BlockSpec((tm, TOP_K), row), pl.BlockSpec((1, LANES), c2)],
        out_shape=[jax.ShapeDtypeStruct((n, D_MODEL), F32), jax.ShapeDtypeStruct((n * TOKEN_TILE, LANES), F32),
                   jax.ShapeDtypeStruct((n, TOP_K), I32), jax.ShapeDtypeStruct((n, TOP_K), F32),
                   jax.ShapeDtypeStruct((n, TOP_K), I32), jax.ShapeDtypeStruct((1, LANES), F32)],
        scratch_shapes=[pltpu.VMEM((1, LANES), F32)],
        compiler_params=_cparams(("arbitrary",)),
        name="merge_route",
    )(x2d, g_mix[None, :], o_sb, o_dsa, o_mem, w_gate.astype(BF16), b_gate[:, None, :],
      w_branch.astype(BF16), w_out.astype(BF16), g_ffn[None, :], wr_hi, wr_lo, b_router[None, :],
      _strict_tri(tm, lower=True))


DISPATCH_TOKENS = 2048
COMBINE_TOKENS = 512


def _dispatch_kernel(dest_ref, pad_ref, h_ref, xs_ref, zero_ref, sem, *, n_tok, n_pad):
    n_copies = n_tok * TOP_K + n_pad
    zero_ref[...] = jnp.zeros_like(zero_ref)

    def fill(j, c):
        for r in range(2):
            dst = pl.multiple_of(pad_ref[0, 0, 2 * j + r] * TOKEN_TILE, TOKEN_TILE)
            pltpu.make_async_copy(zero_ref, xs_ref.at[pl.ds(dst, TOKEN_TILE)], sem).start(priority=r)
        return c

    lax.fori_loop(0, n_pad // 2, fill, 0, unroll=2)

    def start(t, c):
        src = pl.multiple_of(t * TOKEN_TILE, TOKEN_TILE)
        for k in range(TOP_K):
            dst = pl.multiple_of(dest_ref[0, 0, t * TOP_K + k] * TOKEN_TILE, TOKEN_TILE)
            pltpu.make_async_copy(h_ref.at[pl.ds(src, TOKEN_TILE)], xs_ref.at[pl.ds(dst, TOKEN_TILE)],
                                  sem).start(priority=k % 2)
        return c

    lax.fori_loop(0, n_tok, start, 0, unroll=2)
    total = n_copies * TOKEN_TILE
    pltpu.make_async_copy(xs_ref.at[pl.ds(0, total)], xs_ref.at[pl.ds(0, total)], sem).wait()


def _dispatch(dest, pad_slots, h2t, n_slots):
    n = h2t.shape[0] // TOKEN_TILE
    n_tok = min(DISPATCH_TOKENS, n)
    steps = n // n_tok
    per = n_tok * TOP_K
    n_pad = pad_slots.shape[0] // steps
    assert n_tok * steps == n and n_pad * steps == pad_slots.shape[0] and n_pad % 2 == 0
    return pl.pallas_call(
        functools.partial(_dispatch_kernel, n_tok=n_tok, n_pad=n_pad),
        grid=(steps,),
        in_specs=[pl.BlockSpec((1, 1, per), lambda i: (i, 0, 0), memory_space=pltpu.SMEM),
                  pl.BlockSpec((1, 1, n_pad), lambda i: (i, 0, 0), memory_space=pltpu.SMEM),
                  pl.BlockSpec((n_tok * TOKEN_TILE, LANES), lambda i: (i, 0))],
        out_specs=pl.BlockSpec(memory_space=pl.ANY),
        out_shape=jax.ShapeDtypeStruct((n_slots * TOKEN_TILE, LANES), F32),
        scratch_shapes=[pltpu.VMEM((TOKEN_TILE, LANES), F32), pltpu.SemaphoreType.DMA(())],
        compiler_params=_cparams(("arbitrary",)),
        name="moe_dispatch",
    )(dest.reshape(steps, 1, per), pad_slots.reshape(steps, 1, n_pad), h2t)


def _expert_kernel(blk_e_ref, nused_ref, x_ref, w1_ref, b1_ref, w2_ref, b2_ref, y_ref, xb_ref, w1b_ref, w2b_ref):
    i = pl.program_id(0)
    new_expert = jnp.logical_or(i == 0, blk_e_ref[i] != blk_e_ref[jnp.maximum(i - 1, 0)])

    @pl.when(jnp.logical_and(i < nused_ref[0], new_expert))
    def _():
        w1b_ref[...] = w1_ref[0].astype(BF16)
        w2b_ref[...] = w2_ref[0].astype(BF16)

    @pl.when(i < nused_ref[0])
    def _():
        for s in range(TOKEN_TILE):
            xb_ref[:, s * LANES:(s + 1) * LANES] = _load_token_tiles(x_ref, 0, MOE_BLOCK, s).astype(BF16)
        hb = _dot(xb_ref[...], w1b_ref[...]) + b1_ref[0]
        g = jnp.minimum(hb[:, :D_EXPERT], SWIGLU_LIMIT)
        u = jnp.clip(hb[:, D_EXPERT:], -SWIGLU_LIMIT, SWIGLU_LIMIT)
        act = (u + 1.0) * (g * jax.nn.sigmoid(SWIGLU_ALPHA * g))
        _store_token_tiles(y_ref, _dot(act.astype(BF16), w2b_ref[...]) + b2_ref[0])

    @pl.when(pl.program_id(0) >= nused_ref[0])
    def _():
        y_ref[...] = jnp.zeros_like(y_ref)


def _experts(blk_e, n_used, xs, w_e_in, b_e_in, w_e_out, b_e_out):
    n_slots = xs.shape[0] // TOKEN_TILE
    nblk = n_slots // MOE_BLOCK
    slot_block = pl.BlockSpec((MOE_BLOCK * TOKEN_TILE, LANES), lambda i, be, nu: (i, 0))
    grid_spec = pltpu.PrefetchScalarGridSpec(
        num_scalar_prefetch=2,
        grid=(nblk,),
        in_specs=[slot_block,
                  pl.BlockSpec((1, D_MODEL, 2 * D_EXPERT), lambda i, be, nu: (be[i], 0, 0)),
                  pl.BlockSpec((1, 1, 2 * D_EXPERT), lambda i, be, nu: (be[i], 0, 0)),
                  pl.BlockSpec((1, D_EXPERT, D_MODEL), lambda i, be, nu: (be[i], 0, 0)),
                  pl.BlockSpec((1, 1, D_MODEL), lambda i, be, nu: (be[i], 0, 0))],
        out_specs=slot_block,
        scratch_shapes=[pltpu.VMEM((MOE_BLOCK, D_MODEL), BF16), pltpu.VMEM((D_MODEL, 2 * D_EXPERT), BF16),
                        pltpu.VMEM((D_EXPERT, D_MODEL), BF16)],
    )
    return pl.pallas_call(
        _expert_kernel,
        grid_spec=grid_spec,
        out_shape=jax.ShapeDtypeStruct((n_slots * TOKEN_TILE, LANES), F32),
        compiler_params=_cparams(("arbitrary",)),
        name="moe_experts",
    )(blk_e, n_used, xs, w_e_in, b_e_in[:, None, :], w_e_out, b_e_out[:, None, :])


def _combine_kernel(dest_ref, dest_one_ref, dest_ahead_ref, x_ref, gate_ref, y_ref, o_ref, buf_ref, sem):
    n_copies = COMBINE_TOKENS * TOP_K
    rows = 32
    i = pl.program_id(0)
    slot, slot1, slot2 = i % 3, (i + 1) % 3, (i + 2) % 3
    buf = buf_ref.at[slot]

    def start_rows(idx_ref, dst_buf, dma_sem, t0):
        for t in range(rows):
            for k in range(TOP_K):
                src = pl.multiple_of(idx_ref[0, 0, (t0 + t) * TOP_K + k] * TOKEN_TILE, TOKEN_TILE)
                dst = pl.multiple_of((k * COMBINE_TOKENS + t0 + t) * TOKEN_TILE, TOKEN_TILE)
                pltpu.make_async_copy(y_ref.at[pl.ds(src, TOKEN_TILE)], dst_buf.at[pl.ds(dst, TOKEN_TILE)],
                                      dma_sem).start(priority=k % 2)

    def wait_rows(dst_buf, dma_sem):
        pltpu.make_async_copy(y_ref.at[pl.ds(0, n_copies * TOKEN_TILE)], dst_buf, dma_sem).wait()

    @pl.when(i == 0)
    def _():
        def first(c, z):
            start_rows(dest_ref, buf_ref.at[0], sem.at[0], c * rows)
            start_rows(dest_one_ref, buf_ref.at[1], sem.at[1], c * rows)
            return z
        lax.fori_loop(0, COMBINE_TOKENS // rows, first, 0)

    wait_rows(buf, sem.at[slot])

    def sum_pass(c, z):
        r0 = pl.multiple_of(c * rows, rows)
        start_rows(dest_ahead_ref, buf_ref.at[slot2], sem.at[slot2], r0)
        gate = gate_ref[pl.ds(r0, rows), :]
        gates = [jnp.broadcast_to(gate[:, k:k + 1], (rows, LANES)) for k in range(TOP_K)]
        for s in range(TOKEN_TILE):
            out = x_ref[pl.ds(r0, rows), s * LANES:(s + 1) * LANES]
            for k in range(TOP_K):
                out = out + gates[k] * _load_token_tiles(buf, k * COMBINE_TOKENS + r0, rows, s)
            o_ref[pl.ds(r0, rows), s * LANES:(s + 1) * LANES] = out
        return z

    lax.fori_loop(0, COMBINE_TOKENS // rows, sum_pass, 0)

    @pl.when(i == pl.num_programs(0) - 1)
    def _():
        wait_rows(buf_ref.at[slot1], sem.at[slot1])
        wait_rows(buf_ref.at[slot2], sem.at[slot2])


def _combine(dest, x2, gate, ys):
    n = x2.shape[0]
    steps = n // COMBINE_TOKENS
    per = COMBINE_TOKENS * TOP_K
    dest3 = dest.reshape(steps, 1, per)
    return pl.pallas_call(
        _combine_kernel,
        grid=(steps,),
        in_specs=[pl.BlockSpec((1, 1, per), lambda i: (i, 0, 0), memory_space=pltpu.SMEM),
                  pl.BlockSpec((1, 1, per), lambda i: (min(1, steps - 1), 0, 0), memory_space=pltpu.SMEM),
                  pl.BlockSpec((1, 1, per), lambda i: (jnp.minimum(i + 2, steps - 1), 0, 0), memory_space=pltpu.SMEM),
                  pl.BlockSpec((COMBINE_TOKENS, D_MODEL), lambda i: (i, 0)),
                  pl.BlockSpec((COMBINE_TOKENS, TOP_K), lambda i: (i, 0)),
                  pl.BlockSpec(memory_space=pl.ANY)],
        out_specs=pl.BlockSpec((COMBINE_TOKENS, D_MODEL), lambda i: (i, 0)),
        out_shape=jax.ShapeDtypeStruct((n, D_MODEL), F32),
        scratch_shapes=[pltpu.VMEM((3, TOP_K * COMBINE_TOKENS * TOKEN_TILE, LANES), F32),
                        pltpu.SemaphoreType.DMA((3,))],
        compiler_params=_cparams(("arbitrary",)),
        name="moe_combine",
    )(dest3, dest3, dest3, x2, gate, ys)


def _moe(x2, h2, top_idx, gate, rank, counts, w_e_in, b_e_in, w_e_out, b_e_out):
    n = x2.shape[0]
    counts = counts[0, :N_EXPERTS].astype(I32)
    padded = (counts + MOE_BLOCK - 1) // MOE_BLOCK * MOE_BLOCK
    pend = jnp.cumsum(padded)
    pstart = pend - padded
    nblk = -(-(n * TOP_K) // MOE_BLOCK) + N_EXPERTS
    blk_start = jnp.arange(nblk, dtype=I32) * MOE_BLOCK
    blk_e = jnp.minimum(jnp.sum((pend[None, :] <= blk_start[:, None]).astype(I32), axis=1), N_EXPERTS - 1)
    n_used = (pend[-1:] // MOE_BLOCK).astype(I32)
    onehot = top_idx[:, :, None] == jnp.arange(N_EXPERTS, dtype=I32)[None, None, :]
    dest = rank + jnp.sum(jnp.where(onehot, pstart[None, None, :], 0), axis=-1)
    n_slots = nblk * MOE_BLOCK
    pad_len = padded - counts
    pad_end = jnp.cumsum(pad_len)
    base = jnp.concatenate([pstart + counts - (pad_end - pad_len), pend[-1:] - pad_end[-1:]])
    j = jnp.arange(n_slots - n * TOP_K, dtype=I32)
    group = jnp.sum((pad_end[None, :] <= j[:, None]).astype(I32), axis=1)
    group_hot = group[:, None] == jnp.arange(N_EXPERTS + 1, dtype=I32)[None, :]
    pad_slots = j + jnp.sum(jnp.where(group_hot, base[None, :], 0), axis=1)
    xs = _dispatch(dest, pad_slots, h2, n_slots)
    ys = _experts(blk_e, n_used, xs, w_e_in, b_e_in, w_e_out, b_e_out)
    return _combine(dest, x2, gate, ys)


def _layer(x, mem, g_mix, w_in, g_q_dsa, g_k_dsa, g_q_mem, g_k_mem, g_mem, w_mem_kv, w_gate, b_gate,
           w_branch, w_out, g_ffn, w_router, b_router, w_e_in, b_e_in, w_e_out, b_e_out):
    batch, seq, _ = x.shape
    n = batch * seq
    topk = min(DSA_TOPK_MAX, seq // 4)
    x2d = x.reshape(n, D_MODEL)
    tm_dense = min(DENSE_ROWS, seq)
    sq, sk, sv, dq, iq, kvi, iw, mq, vt = _inproj(x2d, batch, seq, g_mix, w_in, g_q_dsa, g_k_dsa, g_q_mem, tm_dense)
    o_sb = _sb_attention(sq, sk, sv, min(256, seq))
    o_dsa = _dsa_attention(dq, iq, iw, kvi, vt, 128, topk)
    mlen = mem.shape[1]
    mk, mv = _mem_kv(mem.reshape(batch * mlen, D_MODEL), g_mem, w_mem_kv, g_k_mem, min(512, batch * mlen))
    mw = MEM_HEADS * MEM_HEAD_DIM
    o_mem = _mem_attention(mq.reshape(batch, seq, mw), mk.reshape(batch, mlen, mw), mv.reshape(batch, mlen, mw), seq)
    x2, h2, top_idx, gate, rank, counts = _merge_route(
        x2d, g_mix, o_sb.reshape(n, BRANCH_W), o_dsa.reshape(n, BRANCH_W), o_mem.reshape(n, mw),
        w_gate, b_gate, w_branch, w_out, g_ffn, w_router, b_router, tm_dense)
    out = _moe(x2, h2, top_idx, gate, rank, counts, w_e_in, b_e_in, w_e_out, b_e_out)
    return out.reshape(batch, seq, D_MODEL)


def kernel(x, mem, g_mix, w_in, g_q_dsa, g_k_dsa, g_q_mem, g_k_mem, g_mem, w_mem_kv, w_gate, b_gate, w_branch, w_out, g_ffn, w_router, b_router, w_e_in, b_e_in, w_e_out, b_e_out):
    for l in range(g_mix.shape[0]):
        x = _layer(x, mem, g_mix[l], w_in[l], g_q_dsa[l], g_k_dsa[l], g_q_mem[l], g_k_mem[l], g_mem[l],
                   w_mem_kv[l], w_gate[l], b_gate[l], w_branch[l], w_out[l], g_ffn[l], w_router[l],
                   b_router[l], w_e_in[l], b_e_in[l], w_e_out[l], b_e_out[l])
    return x
```

```python
import functools

import numpy as np
import jax
import jax.numpy as jnp
from jax import lax
from jax.experimental import pallas as pl
from jax.experimental.pallas import tpu as pltpu

F32 = jnp.float32
BF16 = jnp.bfloat16
I32 = jnp.int32

D_MODEL = 1024
CHUNK = 64
SB_HEADS = 8
DSA_HEADS = 8
HEAD_DIM = 64
IDX_HEADS = 4
DSA_TOPK_MAX = 256
MEM_HEADS = 4
MEM_HEAD_DIM = 128
N_EXPERTS = 32
TOP_K = 4
D_EXPERT = D_MODEL
SWIGLU_LIMIT = 7.0
SWIGLU_ALPHA = 1.702
ROPE_THETA = 10000.0
EPS = 1e-6
MOE_BLOCK = 512

BRANCH_W = 512
IN_SIZES = (512, 512, 512, 512, 64, 64, 256, 64, 4, 512)
C_SQ, C_SK, C_SV, C_DQ, C_MQ, C_IQ, C_SMALL, C_END = 0, 512, 1024, 1536, 2048, 2560, 2816, 3072
IW_LANE = 64

LANES = 128
NEG_BIG = -1e30
SB_CUTOFF = 110.0
KEY_NEG_INF = int(np.array(-np.inf, np.float32).view(np.int32)) ^ 0x7FFFFFFF
INT_MIN = -(2 ** 31)

VMEM_LIMIT = 56 * 1024 * 1024
DENSE_ROWS = 1024


def _cparams(sem):
    return pltpu.CompilerParams(dimension_semantics=sem, vmem_limit_bytes=VMEM_LIMIT)


def _dot(a, b):
    return jnp.dot(a, b, preferred_element_type=F32)


def _dot_nt(a, b):
    return lax.dot_general(a, b, (((1,), (1,)), ((), ())), preferred_element_type=F32)


def _split_bf16(x):
    hi = x.astype(BF16)
    lo = (x - hi.astype(F32)).astype(BF16)
    return hi, lo


def _dot_split(x, m_bf16):
    hi, lo = _split_bf16(x)
    return _dot(hi, m_bf16) + _dot(lo, m_bf16)


TOKEN_TILE = D_MODEL // LANES


def _store_token_tiles(ref, y):
    rows = y.shape[0]
    for s in range(TOKEN_TILE):
        ref[pl.ds(s, rows, stride=TOKEN_TILE), :] = y[:, s * LANES:(s + 1) * LANES]


def _load_token_tiles(ref, start_row, rows, s):
    return ref[pl.ds(start_row * TOKEN_TILE + s, rows, stride=TOKEN_TILE), :]


def _rot_half_unsigned(y):
    w = y.shape[1]
    lane = lax.broadcasted_iota(I32, y.shape, 1)
    return jnp.where((lane & 32) == 0, pltpu.roll(y, w - 32, 1), pltpu.roll(y, 32, 1))


def _inproj_kernel(x_ref, g_ref, w_ref, wsvt_ref, cos_ref, sin_ref, coss_ref, sins_ref, gq_ref, gks_ref,
                   gm_ref, bd64_ref, bd128_ref,
                   sq_ref, sk_ref, sv_ref, dq_ref, iq_ref, kvi_ref, iw_ref, mq_ref, vt_ref):
    x = x_ref[...]
    ms = jnp.mean(x * x, axis=-1, keepdims=True)
    h = (x * lax.rsqrt(ms + EPS) * g_ref[...]).astype(BF16)

    def seg(a, b):
        return _dot(h, w_ref[:, a:b])

    def put_heads(ref, y):
        for hd in range(y.shape[1] // HEAD_DIM):
            ref[0, hd] = y[:, hd * HEAD_DIM:(hd + 1) * HEAD_DIM].astype(BF16)

    put_heads(sq_ref, seg(C_SQ, C_SK) * (HEAD_DIM ** -0.5))
    put_heads(sk_ref, seg(C_SK, C_SV))
    sv_ref[0] = _dot_nt(wsvt_ref[...], h).astype(BF16)

    y = seg(C_DQ, C_MQ)
    msq = _dot_split(y * y, bd64_ref[...]) * (1.0 / HEAD_DIM)
    y = y * lax.rsqrt(msq + EPS) * gq_ref[...]
    y = y * cos_ref[...] + _rot_half_unsigned(y) * sin_ref[...]
    put_heads(dq_ref, y * (HEAD_DIM ** -0.5))

    y = seg(C_IQ, C_SMALL)
    y = y * cos_ref[:, :256] + _rot_half_unsigned(y) * sin_ref[:, :256]
    iq_ref[...] = (y * (HEAD_DIM ** -0.5)).astype(BF16)

    y = seg(C_SMALL, C_END)
    lane = lax.broadcasted_iota(I32, y.shape, 1)
    is_k = lane < HEAD_DIM
    msk = jnp.sum(jnp.where(is_k, y * y, 0.0), axis=-1, keepdims=True) * (1.0 / HEAD_DIM)
    y = y * jnp.where(is_k, lax.rsqrt(msk + EPS) * gks_ref[...], 1.0)
    y = y * coss_ref[...] + _rot_half_unsigned(y) * sins_ref[...]
    kvi_ref[...] = y.astype(BF16)
    iw_ref[...] = y[:, 128:256]
    kv_t = y[:, 0:128].T
    row_t = lax.broadcasted_iota(I32, kv_t.shape, 0)
    vt_ref[0] = jnp.where(row_t < HEAD_DIM, 1.0, kv_t).astype(BF16)

    y = seg(C_MQ, C_IQ)
    msm = _dot_split(y * y, bd128_ref[...]) * (1.0 / MEM_HEAD_DIM)
    mq_ref[...] = (y * lax.rsqrt(msm + EPS) * gm_ref[...]).astype(BF16)


def _rope_tables(seq):
    half = HEAD_DIM // 2
    inv = ROPE_THETA ** (-jnp.arange(half, dtype=F32) / half)
    ang = jnp.arange(seq).astype(F32)[:, None] * inv[None, :]
    cos = jnp.cos(ang)
    sin = jnp.sin(ang)
    cos64 = jnp.concatenate([cos, cos], axis=1)
    sin64 = jnp.concatenate([-sin, sin], axis=1)
    one = jnp.ones_like(cos64)
    zero = jnp.zeros_like(cos64)
    cosq = jnp.tile(cos64, (1, 8))
    sinq = jnp.tile(sin64, (1, 8))
    coss = jnp.concatenate([cos64, one, cos64, one], axis=1)
    sins = jnp.concatenate([sin64, zero, sin64, zero], axis=1)
    return cosq, sinq, coss, sins


def _block_diag_ones(width, group):
    idx = np.arange(width) // group
    return jnp.asarray((idx[:, None] == idx[None, :]).astype(np.float32), dtype=BF16)


def _inproj(x2d, batch, seq, g_mix, w_in, g_q_dsa, g_k_dsa, g_q_mem, tm):
    n = x2d.shape[0]
    sizes = np.cumsum((0,) + IN_SIZES)
    col = {name: (int(sizes[i]), int(sizes[i + 1])) for i, name in enumerate(
        ("sq", "sk", "sv", "dq", "dk", "dv", "iq", "ik", "iw", "mq"))}
    order = ("sq", "sk", "sv", "dq", "mq", "iq", "dk", "dv", "ik", "iw")
    w = jnp.concatenate([w_in[:, col[k][0]:col[k][1]] for k in order]
                        + [jnp.zeros((D_MODEL, C_END - sum(IN_SIZES)), w_in.dtype)], axis=1).astype(BF16)
    cosq, sinq, coss, sins = (jnp.asarray(t) for t in _rope_tables(seq))
    gq = jnp.tile(g_q_dsa, 8)[None, :]
    gks = jnp.concatenate([g_k_dsa, jnp.ones((256 - HEAD_DIM,), F32)])[None, :]
    gm = jnp.tile(g_q_mem, MEM_HEADS)[None, :]
    spb = seq // tm
    row = lambda i: (i, 0)
    const = lambda i: (0, 0)
    pos = lambda i: (i % spb, 0)
    heads = lambda i: (i // spb, 0, i % spb, 0)
    head_shape = jax.ShapeDtypeStruct((batch, 8, seq, HEAD_DIM), BF16)
    head_spec = pl.BlockSpec((1, 8, tm, HEAD_DIM), heads)
    return pl.pallas_call(
        _inproj_kernel,
        grid=(n // tm,),
        in_specs=[
            pl.BlockSpec((tm, D_MODEL), row),
            pl.BlockSpec((1, D_MODEL), const),
            pl.BlockSpec((D_MODEL, C_END), const, pipeline_mode=pl.Buffered(1)),
            pl.BlockSpec((512, D_MODEL), const, pipeline_mode=pl.Buffered(1)),
            pl.BlockSpec((tm, 512), pos), pl.BlockSpec((tm, 512), pos),
            pl.BlockSpec((tm, 256), pos), pl.BlockSpec((tm, 256), pos),
            pl.BlockSpec((1, 512), const), pl.BlockSpec((1, 256), const), pl.BlockSpec((1, 512), const),
            pl.BlockSpec((512, 512), const), pl.BlockSpec((512, 512), const),
        ],
        out_specs=[head_spec, head_spec, pl.BlockSpec((1, 512, tm), lambda i: (i // spb, 0, i % spb)), head_spec,
                   pl.BlockSpec((tm, 256), row), pl.BlockSpec((tm, 256), row),
                   pl.BlockSpec((tm, 128), row), pl.BlockSpec((tm, 512), row),
                   pl.BlockSpec((1, 128, tm), lambda i: (i // spb, 0, i % spb))],
        out_shape=[head_shape, head_shape, jax.ShapeDtypeStruct((batch, 512, seq), BF16), head_shape,
                   jax.ShapeDtypeStruct((n, 256), BF16), jax.ShapeDtypeStruct((n, 256), BF16),
                   jax.ShapeDtypeStruct((n, 128), F32), jax.ShapeDtypeStruct((n, 512), BF16),
                   jax.ShapeDtypeStruct((batch, 128, seq), BF16)],
        compiler_params=_cparams(("parallel",)),
        name="inproj",
    )(x2d, g_mix[None, :], w, w_in[:, col["sv"][0]:col["sv"][1]].T.astype(BF16), cosq, sinq, coss, sins, gq, gks, gm,
      _block_diag_ones(512, HEAD_DIM), _block_diag_ones(512, MEM_HEAD_DIM))


def _sb_kernel(q_ref, k_ref, vt_ref, u_ref, o_ref, acc_ref, car_ref, *, tq):
    qi = pl.program_id(1)
    rows = lax.broadcasted_iota(I32, (tq, tq), 0)
    cols = lax.broadcasted_iota(I32, (tq, tq), 1)
    dif = rows - cols
    u = u_ref[...]
    acc_ref[...] = jnp.zeros_like(acc_ref)
    car_ref[...] = jnp.zeros_like(car_ref)

    def cond(c):
        kb, mx = c
        return jnp.logical_and(kb >= 0, mx > -SB_CUTOFF)

    def body(c):
        kb, _ = c
        ks = pl.multiple_of(kb * tq, tq)
        earlier = dif < (qi - kb) * tq
        neg_mask = jnp.where(earlier, -1.0, 0.0).astype(BF16)
        heads = range(SB_HEADS)
        z = [_dot_nt(k_ref[0, hd, pl.ds(ks, tq), :], q_ref[0, hd]) for hd in heads]
        ls, lk, between = [], [], []
        for hd in heads:
            zb = z[hd].astype(BF16)
            sp = jnp.maximum(zb, 0.0) + jnp.log(1.0 + jnp.exp(-jnp.abs(zb)))
            ls.append(z[hd] - sp.astype(F32))
            lk.append(sp * neg_mask)
            between.append(_dot(u, lk[hd]))
        for hd in heads:
            rs = slice(hd * HEAD_DIM, (hd + 1) * HEAD_DIM)
            car = car_ref[hd:hd + 1, :]
            w = jnp.where(earlier, jnp.exp(ls[hd] + between[hd] + car), 0.0)
            acc_ref[rs, :] += _dot(vt_ref[0, rs, pl.ds(ks, tq)], w.astype(BF16))
            car_ref[hd:hd + 1, :] = car + (between[hd][0:1, :] + lk[hd][0:1, :].astype(F32))
        return kb - 1, jnp.max(car_ref[...])

    lax.while_loop(cond, body, (qi, jnp.float32(0.0)))
    o_ref[0] = acc_ref[...].T.astype(BF16)


def _strict_tri(n, lower):
    i = np.arange(n)
    m = (i[:, None] > i[None, :]) if lower else (i[:, None] < i[None, :])
    return jnp.asarray(m.astype(np.float32), dtype=BF16)


def _sb_attention(sq, sk, svt, tq):
    batch, _, seq, _ = sq.shape
    return pl.pallas_call(
        functools.partial(_sb_kernel, tq=tq),
        grid=(batch, seq // tq),
        in_specs=[pl.BlockSpec((1, SB_HEADS, tq, HEAD_DIM), lambda b, i: (b, 0, i, 0)),
                  pl.BlockSpec((1, SB_HEADS, seq, HEAD_DIM), lambda b, i: (b, 0, 0, 0)),
                  pl.BlockSpec((1, BRANCH_W, seq), lambda b, i: (b, 0, 0)),
                  pl.BlockSpec((tq, tq), lambda b, i: (0, 0))],
        out_specs=pl.BlockSpec((1, tq, BRANCH_W), lambda b, i: (b, i, 0)),
        out_shape=jax.ShapeDtypeStruct((batch, seq, BRANCH_W), BF16),
        scratch_shapes=[pltpu.VMEM((BRANCH_W, tq), F32), pltpu.VMEM((SB_HEADS, tq), F32)],
        compiler_params=_cparams(("parallel", "parallel")),
        name="sb_attention",
    )(sq, sk, svt, _strict_tri(tq, lower=False))


DSA_SEG = 256
DSA_KB = DSA_SEG


def _tree_sum(parts):
    while len(parts) > 1:
        parts = [parts[i] + parts[i + 1] for i in range(0, len(parts) - 1, 2)] + ([parts[-1]] if len(parts) % 2 else [])
    return parts[0]


def _dsa_kernel(dq_ref, iq_ref, iw_ref, kvi_ref, vt_ref, tri_ref, o_ref,
                sc_ref, bias_ref, s_ref, *, tq, topk, nseg_max):
    nseg = pl.program_id(1) + 1

    def query_block(r, carry, ns):
        rows = pl.ds(pl.multiple_of(r * tq, tq), tq)
        _dsa_body(dq_ref.at[:, :, rows, :], iq_ref.at[:, rows, :], iw_ref.at[:, rows, :], kvi_ref, vt_ref, tri_ref,
                  o_ref.at[:, rows, :], sc_ref, bias_ref, s_ref,
                  qs=pl.program_id(1) * DSA_SEG + r * tq, tq=tq, topk=topk, nseg=ns)
        return carry

    for ns in range(1, nseg_max + 1):
        @pl.when(nseg == ns)
        def _(ns=ns):
            lax.fori_loop(0, DSA_SEG // tq, functools.partial(query_block, ns=ns), 0)


def _dsa_body(dq_ref, iq_ref, iw_ref, kvi_ref, vt_ref, tri_ref, o_ref, sc_ref, bias_ref, s_ref,
              *, qs, tq, topk, nseg):
    blocks = [slice(c * DSA_KB, (c + 1) * DSA_KB) for c in range(nseg)]
    iq = iq_ref[0]
    w_t = iw_ref[0].T
    w_row = [w_t[IW_LANE + h:IW_LANE + h + 1, :] * (IDX_HEADS ** -0.5) for h in range(IDX_HEADS)]
    q_chunk = (qs + lax.broadcasted_iota(I32, (DSA_KB, tq), 1)) // CHUNK
    k_chunk = lax.broadcasted_iota(I32, (DSA_KB, tq), 0) // CHUNK

    for c, blk in enumerate(blocks):
        ik = kvi_ref[0, blk, 128:192]
        lg = [_dot_nt(ik, iq[:, h * HEAD_DIM:(h + 1) * HEAD_DIM]) for h in range(IDX_HEADS)]
        sc = jnp.zeros((DSA_KB, tq), F32)
        for h in range(IDX_HEADS):
            sc = sc + w_row[h] * jnp.maximum(lg[h], 0.0)
        admissible = (c * (DSA_KB // CHUNK) + k_chunk) <= q_chunk
        sc_ref[blk, :] = jnp.where(admissible, sc, -jnp.inf)

    def count(pred_fn):
        sub, lanes_of_sums = 32, 4
        acc = [jnp.zeros((sub, tq), F32)] * lanes_of_sums
        for j in range(nseg * DSA_SEG // sub):
            acc[j % lanes_of_sums] = acc[j % lanes_of_sums] + jnp.where(
                pred_fn(sc_ref[j * sub:(j + 1) * sub, :]), 1.0, 0.0)
        return jnp.sum(_tree_sum(acc), axis=0, keepdims=True)

    def key_to_float(key):
        return lax.bitcast_convert_type(jnp.where(key >= 0, key, key ^ 0x7FFFFFFF), F32)

    kf = jnp.float32(topk)
    n_rows = jnp.float32(nseg * DSA_SEG)
    cnt0 = count(lambda s: s >= 0.0)
    t0 = jnp.where(cnt0 >= kf, 0, INT_MIN).astype(I32)
    cnt_t0 = jnp.where(cnt0 >= kf, cnt0, n_rows)

    def bit_step(i, carry):
        t, cnt_t = carry
        cand = t + lax.shift_left(jnp.int32(1), 30 - i)
        cand_f = key_to_float(cand)
        cnt = jnp.where(cand <= KEY_NEG_INF, n_rows, count(lambda s: s >= cand_f))
        take = cnt >= kf
        return jnp.where(take, cand, t), jnp.where(take, cnt, cnt_t)

    thr_key, cnt_thr = lax.fori_loop(0, 31, bit_step, (t0, cnt_t0))
    thr = jnp.where(thr_key <= KEY_NEG_INF, -jnp.inf, key_to_float(thr_key))
    tri = tri_ref[...]

    surplus = jnp.logical_or(jnp.max(cnt_thr) > kf, jnp.min(thr) == -jnp.inf)

    @pl.when(surplus)
    def _():
        need = kf - count(lambda s: s > thr)
        prefix = jnp.zeros((1, tq), F32)
        for blk in blocks:
            sc = sc_ref[blk, :]
            eqf = jnp.where(sc == thr, 1.0, 0.0)
            rank = _dot(tri, eqf.astype(BF16)) + prefix
            tie = jnp.where(rank < need, eqf, 0.0)
            sel = jnp.where(sc > thr, 1.0, tie)
            bias_ref[blk, :] = jnp.where(sc > -jnp.inf, (sel - 1.0) * (-NEG_BIG), NEG_BIG)
            prefix = prefix + jnp.sum(eqf, axis=0, keepdims=True)

    @pl.when(jnp.logical_not(surplus))
    def _():
        for blk in blocks:
            bias_ref[blk, :] = jnp.where(sc_ref[blk, :] >= thr, 0.0, NEG_BIG)

    q8 = dq_ref[0].reshape(DSA_HEADS * tq, HEAD_DIM)
    m = jnp.full((1, DSA_HEADS * tq), NEG_BIG, F32)
    for blk in blocks:
        b = bias_ref[blk, :]
        s = _dot_nt(kvi_ref[0, blk, 0:HEAD_DIM], q8) + jnp.concatenate([b] * DSA_HEADS, axis=1)
        s_ref[blk, :] = s
        m = jnp.maximum(m, jnp.max(s, axis=0, keepdims=True))
    acc = jnp.zeros((128, DSA_HEADS * tq), F32)
    for blk in blocks:
        p = jnp.exp(s_ref[blk, :] - m)
        acc = acc + _dot(vt_ref[0, :, blk], p.astype(BF16))
    for hd in range(DSA_HEADS):
        a = acc[:, hd * tq:(hd + 1) * tq]
        o = (a / a[0:1, :]).T
        o_ref[0, :, hd * HEAD_DIM:(hd + 1) * HEAD_DIM] = o[:, HEAD_DIM:].astype(BF16)


def _dsa_attention(dq, iq, iw, kvi, vt, tq, topk):
    batch, _, seq, _ = dq.shape
    return pl.pallas_call(
        functools.partial(_dsa_kernel, tq=tq, topk=topk, nseg_max=seq // DSA_SEG),
        grid=(batch, seq // DSA_SEG),
        in_specs=[
            pl.BlockSpec((1, DSA_HEADS, DSA_SEG, HEAD_DIM), lambda b, i: (b, 0, i, 0)),
            pl.BlockSpec((1, DSA_SEG, 256), lambda b, i: (b, i, 0)),
            pl.BlockSpec((1, DSA_SEG, 128), lambda b, i: (b, i, 0)),
            pl.BlockSpec((1, seq, 256), lambda b, i: (b, 0, 0)),
            pl.BlockSpec((1, 128, seq), lambda b, i: (b, 0, 0)),
            pl.BlockSpec((DSA_KB, DSA_KB), lambda b, i: (0, 0)),
        ],
        out_specs=pl.BlockSpec((1, DSA_SEG, BRANCH_W), lambda b, i: (b, i, 0)),
        out_shape=jax.ShapeDtypeStruct((batch, seq, BRANCH_W), BF16),
        scratch_shapes=[pltpu.VMEM((seq, tq), F32), pltpu.VMEM((seq, tq), F32),
                        pltpu.VMEM((seq, DSA_HEADS * tq), F32)],
        compiler_params=_cparams(("parallel", "parallel")),
        name="dsa_attention",
    )(dq, iq.reshape(batch, seq, 256), iw.reshape(batch, seq, 128), kvi.reshape(batch, seq, 256), vt,
      _strict_tri(DSA_KB, lower=True))


def _memkv_kernel(m_ref, g_ref, w_ref, gk_ref, bd_ref, mk_ref, mv_ref):
    x = m_ref[...]
    ms = jnp.mean(x * x, axis=-1, keepdims=True)
    h = (x * lax.rsqrt(ms + EPS) * g_ref[...]).astype(BF16)
    mw = MEM_HEADS * MEM_HEAD_DIM
    k = _dot(h, w_ref[:, :mw])
    msk = _dot_split(k * k, bd_ref[...]) * (1.0 / MEM_HEAD_DIM)
    mk_ref[...] = (k * lax.rsqrt(msk + EPS) * gk_ref[...]).astype(BF16)
    mv_ref[...] = _dot(h, w_ref[:, mw:]).astype(BF16)


def _mem_kv(mem2d, g_mem, w_mem_kv, g_k_mem, tm):
    n = mem2d.shape[0]
    mw = MEM_HEADS * MEM_HEAD_DIM
    row = lambda i: (i, 0)
    const = lambda i: (0, 0)
    return pl.pallas_call(
        _memkv_kernel,
        grid=(n // tm,),
        in_specs=[pl.BlockSpec((tm, D_MODEL), row), pl.BlockSpec((1, D_MODEL), const),
                  pl.BlockSpec((D_MODEL, 2 * mw), const), pl.BlockSpec((1, mw), const),
                  pl.BlockSpec((mw, mw), const)],
        out_specs=[pl.BlockSpec((tm, mw), row), pl.BlockSpec((tm, mw), row)],
        out_shape=[jax.ShapeDtypeStruct((n, mw), BF16), jax.ShapeDtypeStruct((n, mw), BF16)],
        compiler_params=_cparams(("parallel",)),
        name="mem_kv",
    )(mem2d, g_mem[None, :], w_mem_kv.astype(BF16), jnp.tile(g_k_mem, MEM_HEADS)[None, :],
      _block_diag_ones(mw, MEM_HEAD_DIM))


def _memattn_kernel(q_ref, k_ref, v_ref, o_ref):
    for hd in range(MEM_HEADS):
        sl = slice(hd * MEM_HEAD_DIM, (hd + 1) * MEM_HEAD_DIM)
        s = _dot_nt(q_ref[0, :, sl], k_ref[0, :, sl]) * (MEM_HEAD_DIM ** -0.5)
        p = jnp.exp(s - jnp.max(s, axis=-1, keepdims=True))
        o = _dot(p.astype(BF16), v_ref[0, :, sl]) / jnp.sum(p, axis=-1, keepdims=True)
        o_ref[0, :, sl] = o.astype(BF16)


def _mem_attention(mq, mk, mv, tq):
    batch, seq, mw = mq.shape
    mlen = mk.shape[1]
    kspec = pl.BlockSpec((1, mlen, mw), lambda b, i: (b, 0, 0))
    return pl.pallas_call(
        _memattn_kernel,
        grid=(batch, seq // tq),
        in_specs=[pl.BlockSpec((1, tq, mw), lambda b, i: (b, i, 0)), kspec, kspec],
        out_specs=pl.BlockSpec((1, tq, mw), lambda b, i: (b, i, 0)),
        out_shape=jax.ShapeDtypeStruct((batch, seq, mw), BF16),
        compiler_params=_cparams(("parallel", "parallel")),
        name="mem_attention",
    )(mq, mk, mv)


def _merge_kernel(x_ref, g_ref, osb_ref, odsa_ref, omem_ref, wg_ref, bg_ref, wb_ref, wo_ref,
                  gf_ref, wrh_ref, wrl_ref, br_ref, tri_ref,
                  x2_ref, h2_ref, idx_ref, gate_ref, rank_ref, cnt_ref, carry_ref):
    x = x_ref[...]
    ms = jnp.mean(x * x, axis=-1, keepdims=True)
    h = (x * lax.rsqrt(ms + EPS) * g_ref[...]).astype(BF16)
    merged = None
    for n, o_ref in enumerate((osb_ref, odsa_ref, omem_ref)):
        gate = jax.nn.sigmoid(_dot(h, wg_ref[n]) + bg_ref[n])
        term = gate * _dot(o_ref[...], wb_ref[n])
        merged = term if merged is None else merged + term
    x2 = x + _dot(merged.astype(BF16), wo_ref[...])
    x2_ref[...] = x2

    ms2 = jnp.mean(x2 * x2, axis=-1, keepdims=True)
    h2 = x2 * lax.rsqrt(ms2 + EPS) * gf_ref[...]
    _store_token_tiles(h2_ref, h2)
    hi, lo = _split_bf16(h2)
    logits = _dot(hi, wrh_ref[...]) + _dot(hi, wrl_ref[...]) + _dot(lo, wrh_ref[...]) + br_ref[...]

    lane = lax.broadcasted_iota(I32, logits.shape, 1)
    vals = logits
    top_v, top_i = [], []
    for _ in range(TOP_K):
        mx = jnp.max(vals, axis=-1, keepdims=True)
        ix = jnp.min(jnp.where(vals == mx, lane, N_EXPERTS), axis=-1, keepdims=True)
        top_v.append(mx)
        top_i.append(ix)
        vals = jnp.where(lane == ix, -jnp.inf, vals)
    ex = [jnp.exp(v - top_v[0]) for v in top_v]
    den = ex[0] + ex[1] + ex[2] + ex[3]
    for k in range(TOP_K):
        idx_ref[:, k:k + 1] = top_i[k]
        gate_ref[:, k:k + 1] = ex[k] / den

    @pl.when(pl.program_id(0) == 0)
    def _():
        carry_ref[...] = jnp.zeros_like(carry_ref)

    lane_e = lax.broadcasted_iota(I32, (x.shape[0], LANES), 1)
    hits = [lane_e == top_i[k] for k in range(TOP_K)]
    onehot = sum(jnp.where(hk, 1.0, 0.0) for hk in hits)
    before = _dot(tri_ref[...], onehot.astype(BF16)) + carry_ref[...]
    for k in range(TOP_K):
        rank_ref[:, k:k + 1] = jnp.sum(jnp.where(hits[k], before, 0.0), axis=-1, keepdims=True).astype(I32)
    carry_ref[...] += jnp.sum(onehot, axis=0, keepdims=True)
    cnt_ref[...] = carry_ref[...]


def _merge_route(x2d, g_mix, o_sb, o_dsa, o_mem, w_gate, b_gate, w_branch, w_out, g_ffn, w_router, b_router, tm):
    n = x2d.shape[0]
    row = lambda i: (i, 0)
    c2 = lambda i: (0, 0)
    c3 = lambda i: (0, 0, 0)
    wr_hi = w_router.astype(BF16)
    wr_lo = (w_router - wr_hi.astype(F32)).astype(BF16)
    return pl.pallas_call(
        _merge_kernel,
        grid=(n // tm,),
        in_specs=[pl.BlockSpec((tm, D_MODEL), row), pl.BlockSpec((1, D_MODEL), c2),
                  pl.BlockSpec((tm, BRANCH_W), row), pl.BlockSpec((tm, BRANCH_W), row),
                  pl.BlockSpec((tm, BRANCH_W), row),
                  pl.BlockSpec((3, D_MODEL, D_MODEL), c3, pipeline_mode=pl.Buffered(1)),
                  pl.BlockSpec((3, 1, D_MODEL), c3),
                  pl.BlockSpec((3, BRANCH_W, D_MODEL), c3, pipeline_mode=pl.Buffered(1)),
                  pl.BlockSpec((D_MODEL, D_MODEL), c2, pipeline_mode=pl.Buffered(1)),
                  pl.BlockSpec((1, D_MODEL), c2), pl.BlockSpec((D_MODEL, N_EXPERTS), c2),
                  pl.BlockSpec((D_MODEL, N_EXPERTS), c2), pl.BlockSpec((1, N_EXPERTS), c2),
                  pl.BlockSpec((tm, tm), c2, pipeline_mode=pl.Buffered(1))],
        out_specs=[pl.BlockSpec((tm, D_MODEL), row), pl.BlockSpec((tm * TOKEN_TILE, LANES), row),
                   pl.BlockSpec((tm, TOP_K), row), pl.BlockSpec((tm, TOP_K), row),
                   pl.BlockSpec((tm, TOP_K), row), pl.BlockSpec((1, LANES), c2)],
        out_shape=[jax.ShapeDtypeStruct((n, D_MODEL), F32), jax.ShapeDtypeStruct((n * TOKEN_TILE, LANES), F32),
                   jax.ShapeDtypeStruct((n, TOP_K), I32), jax.ShapeDtypeStruct((n, TOP_K), F32),
                   jax.ShapeDtypeStruct((n, TOP_K), I32), jax.ShapeDtypeStruct((1, LANES), F32)],
        scratch_shapes=[pltpu.VMEM((1, LANES), F32)],
        compiler_params=_cparams(("arbitrary",)),
        name="merge_route",
    )(x2d, g_mix[None, :], o_sb, o_dsa, o_mem, w_gate.astype(BF16), b_gate[:, None, :],
      w_branch.astype(BF16), w_out.astype(BF16), g_ffn[None, :], wr_hi, wr_lo, b_router[None, :],
      _strict_tri(tm, lower=True))


DISPATCH_TOKENS = 4096
COMBINE_TOKENS = 512


def _dispatch_kernel(dest_ref, pad_ref, h_ref, xs_ref, zero_ref, sem, *, n_tok, n_pad):
    n_copies = n_tok * TOP_K + n_pad
    zero_ref[...] = jnp.zeros_like(zero_ref)

    def fill(j, c):
        for r in range(2):
            dst = pl.multiple_of(pad_ref[0, 0, 2 * j + r] * TOKEN_TILE, TOKEN_TILE)
            pltpu.make_async_copy(zero_ref, xs_ref.at[pl.ds(dst, TOKEN_TILE)], sem).start(priority=r)
        return c

    lax.fori_loop(0, n_pad // 2, fill, 0, unroll=2)

    def start(t, c):
        src = pl.multiple_of(t * TOKEN_TILE, TOKEN_TILE)
        for k in range(TOP_K):
            dst = pl.multiple_of(dest_ref[0, 0, t * TOP_K + k] * TOKEN_TILE, TOKEN_TILE)
            pltpu.make_async_copy(h_ref.at[pl.ds(src, TOKEN_TILE)], xs_ref.at[pl.ds(dst, TOKEN_TILE)],
                                  sem).start(priority=k % 2)
        return c

    lax.fori_loop(0, n_tok, start, 0, unroll=2)
    total = n_copies * TOKEN_TILE
    pltpu.make_async_copy(xs_ref.at[pl.ds(0, total)], xs_ref.at[pl.ds(0, total)], sem).wait()


def _dispatch(dest, pad_slots, h2t, n_slots):
    n = h2t.shape[0] // TOKEN_TILE
    n_tok = min(DISPATCH_TOKENS, n)
    steps = n // n_tok
    per = n_tok * TOP_K
    n_pad = pad_slots.shape[0] // steps
    assert n_tok * steps == n and n_pad * steps == pad_slots.shape[0] and n_pad % 2 == 0
    return pl.pallas_call(
        functools.partial(_dispatch_kernel, n_tok=n_tok, n_pad=n_pad),
        grid=(steps,),
        in_specs=[pl.BlockSpec((1, 1, per), lambda i: (i, 0, 0), memory_space=pltpu.SMEM),
                  pl.BlockSpec((1, 1, n_pad), lambda i: (i, 0, 0), memory_space=pltpu.SMEM),
                  pl.BlockSpec((n_tok * TOKEN_TILE, LANES), lambda i: (i, 0))],
        out_specs=pl.BlockSpec(memory_space=pl.ANY),
        out_shape=jax.ShapeDtypeStruct((n_slots * TOKEN_TILE, LANES), F32),
        scratch_shapes=[pltpu.VMEM((TOKEN_TILE, LANES), F32), pltpu.SemaphoreType.DMA(())],
        compiler_params=_cparams(("arbitrary",)),
        name="moe_dispatch",
    )(dest.reshape(steps, 1, per), pad_slots.reshape(steps, 1, n_pad), h2t)


def _expert_kernel(blk_e_ref, nused_ref, x_ref, w1_ref, b1_ref, w2_ref, b2_ref, y_ref, xb_ref, w1b_ref, w2b_ref):
    i = pl.program_id(0)
    new_expert = jnp.logical_or(i == 0, blk_e_ref[i] != blk_e_ref[jnp.maximum(i - 1, 0)])

    @pl.when(jnp.logical_and(i < nused_ref[0], new_expert))
    def _():
        w1b_ref[...] = w1_ref[0].astype(BF16)
        w2b_ref[...] = w2_ref[0].astype(BF16)

    @pl.when(i < nused_ref[0])
    def _():
        for s in range(TOKEN_TILE):
            xb_ref[:, s * LANES:(s + 1) * LANES] = _load_token_tiles(x_ref, 0, MOE_BLOCK, s).astype(BF16)
        hb = _dot(xb_ref[...], w1b_ref[...]) + b1_ref[0]
        g = jnp.minimum(hb[:, :D_EXPERT], SWIGLU_LIMIT)
        u = jnp.clip(hb[:, D_EXPERT:], -SWIGLU_LIMIT, SWIGLU_LIMIT)
        act = (u + 1.0) * (g * jax.nn.sigmoid(SWIGLU_ALPHA * g))
        _store_token_tiles(y_ref, _dot(act.astype(BF16), w2b_ref[...]) + b2_ref[0])

    @pl.when(pl.program_id(0) >= nused_ref[0])
    def _():
        y_ref[...] = jnp.zeros_like(y_ref)


def _experts(blk_e, n_used, xs, w_e_in, b_e_in, w_e_out, b_e_out):
    n_slots = xs.shape[0] // TOKEN_TILE
    nblk = n_slots // MOE_BLOCK
    slot_block = pl.BlockSpec((MOE_BLOCK * TOKEN_TILE, LANES), lambda i, be, nu: (i, 0))
    grid_spec = pltpu.PrefetchScalarGridSpec(
        num_scalar_prefetch=2,
        grid=(nblk,),
        in_specs=[slot_block,
                  pl.BlockSpec((1, D_MODEL, 2 * D_EXPERT), lambda i, be, nu: (be[i], 0, 0)),
                  pl.BlockSpec((1, 1, 2 * D_EXPERT), lambda i, be, nu: (be[i], 0, 0)),
                  pl.BlockSpec((1, D_EXPERT, D_MODEL), lambda i, be, nu: (be[i], 0, 0)),
                  pl.BlockSpec((1, 1, D_MODEL), lambda i, be, nu: (be[i], 0, 0))],
        out_specs=slot_block,
        scratch_shapes=[pltpu.VMEM((MOE_BLOCK, D_MODEL), BF16), pltpu.VMEM((D_MODEL, 2 * D_EXPERT), BF16),
                        pltpu.VMEM((D_EXPERT, D_MODEL), BF16)],
    )
    return pl.pallas_call(
        _expert_kernel,
        grid_spec=grid_spec,
        out_shape=jax.ShapeDtypeStruct((n_slots * TOKEN_TILE, LANES), F32),
        compiler_params=_cparams(("arbitrary",)),
        name="moe_experts",
    )(blk_e, n_used, xs, w_e_in, b_e_in[:, None, :], w_e_out, b_e_out[:, None, :])


def _combine_kernel(dest_ref, dest_one_ref, dest_ahead_ref, x_ref, gate_ref, y_ref, o_ref, buf_ref, sem):
    n_copies = COMBINE_TOKENS * TOP_K
    rows = 32
    i = pl.program_id(0)
    slot, slot1, slot2 = i % 3, (i + 1) % 3, (i + 2) % 3
    buf = buf_ref.at[slot]

    def start_rows(idx_ref, dst_buf, dma_sem, t0):
        for t in range(rows):
            for k in range(TOP_K):
                src = pl.multiple_of(idx_ref[0, 0, (t0 + t) * TOP_K + k] * TOKEN_TILE, TOKEN_TILE)
                dst = pl.multiple_of((k * COMBINE_TOKENS + t0 + t) * TOKEN_TILE, TOKEN_TILE)
                pltpu.make_async_copy(y_ref.at[pl.ds(src, TOKEN_TILE)], dst_buf.at[pl.ds(dst, TOKEN_TILE)],
                                      dma_sem).start(priority=k % 2)

    def wait_rows(dst_buf, dma_sem):
        pltpu.make_async_copy(y_ref.at[pl.ds(0, n_copies * TOKEN_TILE)], dst_buf, dma_sem).wait()

    @pl.when(i == 0)
    def _():
        def first(c, z):
            start_rows(dest_ref, buf_ref.at[0], sem.at[0], c * rows)
            start_rows(dest_one_ref, buf_ref.at[1], sem.at[1], c * rows)
            return z
        lax.fori_loop(0, COMBINE_TOKENS // rows, first, 0)

    wait_rows(buf, sem.at[slot])

    def sum_pass(c, z):
        r0 = pl.multiple_of(c * rows, rows)
        start_rows(dest_ahead_ref, buf_ref.at[slot2], sem.at[slot2], r0)
        gate = gate_ref[pl.ds(r0, rows), :]
        gates = [jnp.broadcast_to(gate[:, k:k + 1], (rows, LANES)) for k in range(TOP_K)]
        for s in range(TOKEN_TILE):
            out = x_ref[pl.ds(r0, rows), s * LANES:(s + 1) * LANES]
            for k in range(TOP_K):
                out = out + gates[k] * _load_token_tiles(buf, k * COMBINE_TOKENS + r0, rows, s)
            o_ref[pl.ds(r0, rows), s * LANES:(s + 1) * LANES] = out
        return z

    lax.fori_loop(0, COMBINE_TOKENS // rows, sum_pass, 0)

    @pl.when(i == pl.num_programs(0) - 1)
    def _():
        wait_rows(buf_ref.at[slot1], sem.at[slot1])
        wait_rows(buf_ref.at[slot2], sem.at[slot2])


def _combine(dest, x2, gate, ys):
    n = x2.shape[0]
    steps = n // COMBINE_TOKENS
    per = COMBINE_TOKENS * TOP_K
    dest3 = dest.reshape(steps, 1, per)
    return pl.pallas_call(
        _combine_kernel,
        grid=(steps,),
        in_specs=[pl.BlockSpec((1, 1, per), lambda i: (i, 0, 0), memory_space=pltpu.SMEM),
                  pl.BlockSpec((1, 1, per), lambda i: (min(1, steps - 1), 0, 0), memory_space=pltpu.SMEM),
                  pl.BlockSpec((1, 1, per), lambda i: (jnp.minimum(i + 2, steps - 1), 0, 0), memory_space=pltpu.SMEM),
                  pl.BlockSpec((COMBINE_TOKENS, D_MODEL), lambda i: (i, 0)),
                  pl.BlockSpec((COMBINE_TOKENS, TOP_K), lambda i: (i, 0)),
                  pl.BlockSpec(memory_space=pl.ANY)],
        out_specs=pl.BlockSpec((COMBINE_TOKENS, D_MODEL), lambda i: (i, 0)),
        out_shape=jax.ShapeDtypeStruct((n, D_MODEL), F32),
        scratch_shapes=[pltpu.VMEM((3, TOP_K * COMBINE_TOKENS * TOKEN_TILE, LANES), F32),
                        pltpu.SemaphoreType.DMA((3,))],
        compiler_params=_cparams(("arbitrary",)),
        name="moe_combine",
    )(dest3, dest3, dest3, x2, gate, ys)


def _moe(x2, h2, top_idx, gate, rank, counts, w_e_in, b_e_in, w_e_out, b_e_out):
    n = x2.shape[0]
    counts = counts[0, :N_EXPERTS].astype(I32)
    padded = (counts + MOE_BLOCK - 1) // MOE_BLOCK * MOE_BLOCK
    pend = jnp.cumsum(padded)
    pstart = pend - padded
    nblk = -(-(n * TOP_K) // MOE_BLOCK) + N_EXPERTS
    blk_start = jnp.arange(nblk, dtype=I32) * MOE_BLOCK
    blk_e = jnp.minimum(jnp.sum((pend[None, :] <= blk_start[:, None]).astype(I32), axis=1), N_EXPERTS - 1)
    n_used = (pend[-1:] // MOE_BLOCK).astype(I32)
    onehot = top_idx[:, :, None] == jnp.arange(N_EXPERTS, dtype=I32)[None, None, :]
    dest = rank + jnp.sum(jnp.where(onehot, pstart[None, None, :], 0), axis=-1)
    n_slots = nblk * MOE_BLOCK
    pad_len = padded - counts
    pad_end = jnp.cumsum(pad_len)
    base = jnp.concatenate([pstart + counts - (pad_end - pad_len), pend[-1:] - pad_end[-1:]])
    j = jnp.arange(n_slots - n * TOP_K, dtype=I32)
    group = jnp.sum((pad_end[None, :] <= j[:, None]).astype(I32), axis=1)
    group_hot = group[:, None] == jnp.arange(N_EXPERTS + 1, dtype=I32)[None, :]
    pad_slots = j + jnp.sum(jnp.where(group_hot, base[None, :], 0), axis=1)
    xs = _dispatch(dest, pad_slots, h2, n_slots)
    ys = _experts(blk_e, n_used, xs, w_e_in, b_e_in, w_e_out, b_e_out)
    return _combine(dest, x2, gate, ys)


def _layer(x, mem, g_mix, w_in, g_q_dsa, g_k_dsa, g_q_mem, g_k_mem, g_mem, w_mem_kv, w_gate, b_gate,
           w_branch, w_out, g_ffn, w_router, b_router, w_e_in, b_e_in, w_e_out, b_e_out):
    batch, seq, _ = x.shape
    n = batch * seq
    topk = min(DSA_TOPK_MAX, seq // 4)
    x2d = x.reshape(n, D_MODEL)
    tm_dense = min(DENSE_ROWS, seq)
    sq, sk, sv, dq, iq, kvi, iw, mq, vt = _inproj(x2d, batch, seq, g_mix, w_in, g_q_dsa, g_k_dsa, g_q_mem, tm_dense)
    o_sb = _sb_attention(sq, sk, sv, min(256, seq))
    o_dsa = _dsa_attention(dq, iq, iw, kvi, vt, 128, topk)
    mlen = mem.shape[1]
    mk, mv = _mem_kv(mem.reshape(batch * mlen, D_MODEL), g_mem, w_mem_kv, g_k_mem, min(512, batch * mlen))
    mw = MEM_HEADS * MEM_HEAD_DIM
    o_mem = _mem_attention(mq.reshape(batch, seq, mw), mk.reshape(batch, mlen, mw), mv.reshape(batch, mlen, mw), seq)
    x2, h2, top_idx, gate, rank, counts = _merge_route(
        x2d, g_mix, o_sb.reshape(n, BRANCH_W), o_dsa.reshape(n, BRANCH_W), o_mem.reshape(n, mw),
        w_gate, b_gate, w_branch, w_out, g_ffn, w_router, b_router, tm_dense)
    out = _moe(x2, h2, top_idx, gate, rank, counts, w_e_in, b_e_in, w_e_out, b_e_out)
    return out.reshape(batch, seq, D_MODEL)


def kernel(x, mem, g_mix, w_in, g_q_dsa, g_k_dsa, g_q_mem, g_k_mem, g_mem, w_mem_kv, w_gate, b_gate, w_branch, w_out, g_ffn, w_router, b_router, w_e_in, b_e_in, w_e_out, b_e_out):
    for l in range(g_mix.shape[0]):
        x = _layer(x, mem, g_mix[l], w_in[l], g_q_dsa[l], g_k_dsa[l], g_q_mem[l], g_k_mem[l], g_mem[l],
                   w_mem_kv[l], w_gate[l], b_gate[l], w_branch[l], w_out[l], g_ffn[l], w_router[l],
                   b_router[l], w_e_in[l], b_e_in[l], w_e_out[l], b_e_out[l])
    return x
```

```python
import functools

import numpy as np
import jax
import jax.numpy as jnp
from jax import lax
from jax.experimental import pallas as pl
from jax.experimental.pallas import tpu as pltpu

F32 = jnp.float32
BF16 = jnp.bfloat16
I32 = jnp.int32

D_MODEL = 1024
CHUNK = 64
SB_HEADS = 8
DSA_HEADS = 8
HEAD_DIM = 64
IDX_HEADS = 4
DSA_TOPK_MAX = 256
MEM_HEADS = 4
MEM_HEAD_DIM = 128
N_EXPERTS = 32
TOP_K = 4
D_EXPERT = D_MODEL
SWIGLU_LIMIT = 7.0
SWIGLU_ALPHA = 1.702
ROPE_THETA = 10000.0
EPS = 1e-6
MOE_BLOCK = 512

BRANCH_W = 512
IN_SIZES = (512, 512, 512, 512, 64, 64, 256, 64, 4, 512)
C_SQ, C_SK, C_DQ, C_MQ, C_IQ, C_SMALL, C_END = 0, 512, 1024, 1536, 2048, 2304, 2560
IW_LANE = 64

LANES = 128
NEG_BIG = -1e30
SB_CUTOFF = 110.0
KEY_NEG_INF = int(np.array(-np.inf, np.float32).view(np.int32)) ^ 0x7FFFFFFF
INT_MIN = -(2 ** 31)

VMEM_LIMIT = 56 * 1024 * 1024
DENSE_ROWS = 1024


def _cparams(sem):
    return pltpu.CompilerParams(dimension_semantics=sem, vmem_limit_bytes=VMEM_LIMIT)


def _dot(a, b):
    return jnp.dot(a, b, preferred_element_type=F32)


def _dot_nt(a, b):
    return lax.dot_general(a, b, (((1,), (1,)), ((), ())), preferred_element_type=F32)


def _split_bf16(x):
    hi = x.astype(BF16)
    lo = (x - hi.astype(F32)).astype(BF16)
    return hi, lo


def _dot_split(x, m_bf16):
    hi, lo = _split_bf16(x)
    return _dot(hi, m_bf16) + _dot(lo, m_bf16)


TOKEN_TILE = D_MODEL // LANES


def _store_token_tiles(ref, y):
    rows = y.shape[0]
    for s in range(TOKEN_TILE):
        ref[pl.ds(s, rows, stride=TOKEN_TILE), :] = y[:, s * LANES:(s + 1) * LANES]


def _load_token_tiles(ref, start_row, rows, s):
    return ref[pl.ds(start_row * TOKEN_TILE + s, rows, stride=TOKEN_TILE), :]


def _rot_half_unsigned(y):
    w = y.shape[1]
    lane = lax.broadcasted_iota(I32, y.shape, 1)
    return jnp.where((lane & 32) == 0, pltpu.roll(y, w - 32, 1), pltpu.roll(y, 32, 1))


def _inproj_kernel(x_ref, g_ref, w_ref, wsvt_ref, cos_ref, sin_ref, coss_ref, sins_ref, gq_ref, gks_ref,
                   gm_ref, bd64_ref, bd128_ref,
                   sq_ref, sk_ref, sv_ref, dq_ref, iq_ref, kvi_ref, iw_ref, mq_ref, vt_ref):
    x = x_ref[...]
    ms = jnp.mean(x * x, axis=-1, keepdims=True)
    h = (x * lax.rsqrt(ms + EPS) * g_ref[...]).astype(BF16)

    def seg(a, b):
        return _dot(h, w_ref[:, a:b])

    def put_heads(ref, y):
        for hd in range(y.shape[1] // HEAD_DIM):
            ref[0, hd] = y[:, hd * HEAD_DIM:(hd + 1) * HEAD_DIM].astype(BF16)

    put_heads(sq_ref, seg(C_SQ, C_SK) * (HEAD_DIM ** -0.5))
    put_heads(sk_ref, seg(C_SK, C_DQ))
    sv_ref[0] = _dot_nt(wsvt_ref[...], h).astype(BF16)

    y = seg(C_DQ, C_MQ)
    msq = _dot_split(y * y, bd64_ref[...]) * (1.0 / HEAD_DIM)
    y = y * lax.rsqrt(msq + EPS) * gq_ref[...]
    y = y * cos_ref[...] + _rot_half_unsigned(y) * sin_ref[...]
    put_heads(dq_ref, y * (HEAD_DIM ** -0.5))

    y = seg(C_IQ, C_SMALL)
    y = y * cos_ref[:, :256] + _rot_half_unsigned(y) * sin_ref[:, :256]
    iq_ref[...] = (y * (HEAD_DIM ** -0.5)).astype(BF16)

    y = seg(C_SMALL, C_END)
    lane = lax.broadcasted_iota(I32, y.shape, 1)
    is_k = lane < HEAD_DIM
    msk = jnp.sum(jnp.where(is_k, y * y, 0.0), axis=-1, keepdims=True) * (1.0 / HEAD_DIM)
    y = y * jnp.where(is_k, lax.rsqrt(msk + EPS) * gks_ref[...], 1.0)
    y = y * coss_ref[...] + _rot_half_unsigned(y) * sins_ref[...]
    kvi_ref[...] = y.astype(BF16)
    iw_ref[...] = y[:, 128:256]
    kv_t = y[:, 0:128].T
    row_t = lax.broadcasted_iota(I32, kv_t.shape, 0)
    vt_ref[0] = jnp.where(row_t < HEAD_DIM, 1.0, kv_t).astype(BF16)

    y = seg(C_MQ, C_IQ)
    msm = _dot_split(y * y, bd128_ref[...]) * (1.0 / MEM_HEAD_DIM)
    mq_ref[...] = (y * lax.rsqrt(msm + EPS) * gm_ref[...]).astype(BF16)


def _rope_tables(seq):
    half = HEAD_DIM // 2
    inv = ROPE_THETA ** (-jnp.arange(half, dtype=F32) / half)
    ang = jnp.arange(seq).astype(F32)[:, None] * inv[None, :]
    cos = jnp.cos(ang)
    sin = jnp.sin(ang)
    cos64 = jnp.concatenate([cos, cos], axis=1)
    sin64 = jnp.concatenate([-sin, sin], axis=1)
    one = jnp.ones_like(cos64)
    zero = jnp.zeros_like(cos64)
    cosq = jnp.tile(cos64, (1, 8))
    sinq = jnp.tile(sin64, (1, 8))
    coss = jnp.concatenate([cos64, one, cos64, one], axis=1)
    sins = jnp.concatenate([sin64, zero, sin64, zero], axis=1)
    return cosq, sinq, coss, sins


def _block_diag_ones(width, group):
    idx = np.arange(width) // group
    return jnp.asarray((idx[:, None] == idx[None, :]).astype(np.float32), dtype=BF16)


def _inproj(x2d, batch, seq, g_mix, w_in, g_q_dsa, g_k_dsa, g_q_mem, tm):
    n = x2d.shape[0]
    sizes = np.cumsum((0,) + IN_SIZES)
    col = {name: (int(sizes[i]), int(sizes[i + 1])) for i, name in enumerate(
        ("sq", "sk", "sv", "dq", "dk", "dv", "iq", "ik", "iw", "mq"))}
    order = ("sq", "sk", "dq", "mq", "iq", "dk", "dv", "ik", "iw")
    width = sum(col[k][1] - col[k][0] for k in order)
    w = jnp.concatenate([w_in[:, col[k][0]:col[k][1]] for k in order]
                        + [jnp.zeros((D_MODEL, C_END - width), w_in.dtype)], axis=1).astype(BF16)
    cosq, sinq, coss, sins = (jnp.asarray(t) for t in _rope_tables(seq))
    gq = jnp.tile(g_q_dsa, 8)[None, :]
    gks = jnp.concatenate([g_k_dsa, jnp.ones((256 - HEAD_DIM,), F32)])[None, :]
    gm = jnp.tile(g_q_mem, MEM_HEADS)[None, :]
    spb = seq // tm
    row = lambda i: (i, 0)
    const = lambda i: (0, 0)
    pos = lambda i: (i % spb, 0)
    heads = lambda i: (i // spb, 0, i % spb, 0)
    head_shape = jax.ShapeDtypeStruct((batch, 8, seq, HEAD_DIM), BF16)
    head_spec = pl.BlockSpec((1, 8, tm, HEAD_DIM), heads)
    return pl.pallas_call(
        _inproj_kernel,
        grid=(n // tm,),
        in_specs=[
            pl.BlockSpec((tm, D_MODEL), row),
            pl.BlockSpec((1, D_MODEL), const),
            pl.BlockSpec((D_MODEL, C_END), const, pipeline_mode=pl.Buffered(1)),
            pl.BlockSpec((512, D_MODEL), const, pipeline_mode=pl.Buffered(1)),
            pl.BlockSpec((tm, 512), pos), pl.BlockSpec((tm, 512), pos),
            pl.BlockSpec((tm, 256), pos), pl.BlockSpec((tm, 256), pos),
            pl.BlockSpec((1, 512), const), pl.BlockSpec((1, 256), const), pl.BlockSpec((1, 512), const),
            pl.BlockSpec((512, 512), const), pl.BlockSpec((512, 512), const),
        ],
        out_specs=[head_spec, head_spec, pl.BlockSpec((1, 512, tm), lambda i: (i // spb, 0, i % spb)), head_spec,
                   pl.BlockSpec((tm, 256), row), pl.BlockSpec((tm, 256), row),
                   pl.BlockSpec((tm, 128), row), pl.BlockSpec((tm, 512), row),
                   pl.BlockSpec((1, 128, tm), lambda i: (i // spb, 0, i % spb))],
        out_shape=[head_shape, head_shape, jax.ShapeDtypeStruct((batch, 512, seq), BF16), head_shape,
                   jax.ShapeDtypeStruct((n, 256), BF16), jax.ShapeDtypeStruct((n, 256), BF16),
                   jax.ShapeDtypeStruct((n, 128), F32), jax.ShapeDtypeStruct((n, 512), BF16),
                   jax.ShapeDtypeStruct((batch, 128, seq), BF16)],
        compiler_params=_cparams(("parallel",)),
        name="inproj",
    )(x2d, g_mix[None, :], w, w_in[:, col["sv"][0]:col["sv"][1]].T.astype(BF16), cosq, sinq, coss, sins, gq, gks, gm,
      _block_diag_ones(512, HEAD_DIM), _block_diag_ones(512, MEM_HEAD_DIM))


def _sb_kernel(q_ref, k_ref, vt_ref, u_ref, o_ref, acc_ref, car_ref, *, tq):
    qi = pl.program_id(1)
    rows = lax.broadcasted_iota(I32, (tq, tq), 0)
    cols = lax.broadcasted_iota(I32, (tq, tq), 1)
    dif = rows - cols
    u = u_ref[...]
    acc_ref[...] = jnp.zeros_like(acc_ref)
    car_ref[...] = jnp.zeros_like(car_ref)

    def cond(c):
        kb, mx = c
        return jnp.logical_and(kb >= 0, mx > -SB_CUTOFF)

    def body(c):
        kb, _ = c
        ks = pl.multiple_of(kb * tq, tq)
        earlier = dif < (qi - kb) * tq
        neg_mask = jnp.where(earlier, -1.0, 0.0).astype(BF16)
        heads = range(SB_HEADS)
        z = [_dot_nt(k_ref[0, hd, pl.ds(ks, tq), :], q_ref[0, hd]) for hd in heads]
        ls, lk, between = [], [], []
        for hd in heads:
            zb = z[hd].astype(BF16)
            sp = jnp.maximum(zb, 0.0) + jnp.log(1.0 + jnp.exp(-jnp.abs(zb)))
            ls.append(z[hd] - sp.astype(F32))
            lk.append(sp * neg_mask)
            between.append(_dot(u, lk[hd]))
        for hd in heads:
            rs = slice(hd * HEAD_DIM, (hd + 1) * HEAD_DIM)
            car = car_ref[hd:hd + 1, :]
            w = jnp.where(earlier, jnp.exp(ls[hd] + between[hd] + car), 0.0)
            acc_ref[rs, :] += _dot(vt_ref[0, rs, pl.ds(ks, tq)], w.astype(BF16))
            car_ref[hd:hd + 1, :] = car + (between[hd][0:1, :] + lk[hd][0:1, :].astype(F32))
        return kb - 1, jnp.max(car_ref[...])

    lax.while_loop(cond, body, (qi, jnp.float32(0.0)))
    o_ref[0] = acc_ref[...].T.astype(BF16)


def _strict_tri(n, lower):
    i = np.arange(n)
    m = (i[:, None] > i[None, :]) if lower else (i[:, None] < i[None, :])
    return jnp.asarray(m.astype(np.float32), dtype=BF16)


def _sb_attention(sq, sk, svt, tq):
    batch, _, seq, _ = sq.shape
    return pl.pallas_call(
        functools.partial(_sb_kernel, tq=tq),
        grid=(batch, seq // tq),
        in_specs=[pl.BlockSpec((1, SB_HEADS, tq, HEAD_DIM), lambda b, i: (b, 0, i, 0)),
                  pl.BlockSpec((1, SB_HEADS, seq, HEAD_DIM), lambda b, i: (b, 0, 0, 0)),
                  pl.BlockSpec((1, BRANCH_W, seq), lambda b, i: (b, 0, 0)),
                  pl.BlockSpec((tq, tq), lambda b, i: (0, 0))],
        out_specs=pl.BlockSpec((1, tq, BRANCH_W), lambda b, i: (b, i, 0)),
        out_shape=jax.ShapeDtypeStruct((batch, seq, BRANCH_W), BF16),
        scratch_shapes=[pltpu.VMEM((BRANCH_W, tq), F32), pltpu.VMEM((SB_HEADS, tq), F32)],
        compiler_params=_cparams(("parallel", "parallel")),
        name="sb_attention",
    )(sq, sk, svt, _strict_tri(tq, lower=False))


DSA_SEG = 256
DSA_KB = DSA_SEG


def _tree_sum(parts):
    while len(parts) > 1:
        parts = [parts[i] + parts[i + 1] for i in range(0, len(parts) - 1, 2)] + ([parts[-1]] if len(parts) % 2 else [])
    return parts[0]


def _dsa_kernel(dq_ref, iq_ref, iw_ref, kvi_ref, vt_ref, tri_ref, o_ref,
                sc_ref, bias_ref, s_ref, *, tq, topk, nseg_max):
    nseg = pl.program_id(1) + 1

    def query_block(r, carry, ns):
        rows = pl.ds(pl.multiple_of(r * tq, tq), tq)
        _dsa_body(dq_ref.at[:, :, rows, :], iq_ref.at[:, rows, :], iw_ref.at[:, rows, :], kvi_ref, vt_ref, tri_ref,
                  o_ref.at[:, rows, :], sc_ref, bias_ref, s_ref,
                  qs=pl.program_id(1) * DSA_SEG + r * tq, tq=tq, topk=topk, nseg=ns)
        return carry

    for ns in range(1, nseg_max + 1):
        @pl.when(nseg == ns)
        def _(ns=ns):
            lax.fori_loop(0, DSA_SEG // tq, functools.partial(query_block, ns=ns), 0)


def _dsa_body(dq_ref, iq_ref, iw_ref, kvi_ref, vt_ref, tri_ref, o_ref, sc_ref, bias_ref, s_ref,
              *, qs, tq, topk, nseg):
    blocks = [slice(c * DSA_KB, (c + 1) * DSA_KB) for c in range(nseg)]
    iq = iq_ref[0]
    w_t = iw_ref[0].T
    w_row = [w_t[IW_LANE + h:IW_LANE + h + 1, :] * (IDX_HEADS ** -0.5) for h in range(IDX_HEADS)]
    q_chunk = (qs + lax.broadcasted_iota(I32, (DSA_KB, tq), 1)) // CHUNK
    k_chunk = lax.broadcasted_iota(I32, (DSA_KB, tq), 0) // CHUNK

    for c, blk in enumerate(blocks):
        ik = kvi_ref[0, blk, 128:192]
        lg = [_dot_nt(ik, iq[:, h * HEAD_DIM:(h + 1) * HEAD_DIM]) for h in range(IDX_HEADS)]
        sc = jnp.zeros((DSA_KB, tq), F32)
        for h in range(IDX_HEADS):
            sc = sc + w_row[h] * jnp.maximum(lg[h], 0.0)
        admissible = (c * (DSA_KB // CHUNK) + k_chunk) <= q_chunk
        sc_ref[blk, :] = jnp.where(admissible, sc, -jnp.inf)

    def count(pred_fn):
        sub, n_sums = 32, 4
        acc = [jnp.zeros((sub, tq), F32)] * n_sums
        for j in range(nseg * DSA_SEG // sub):
            acc[j % n_sums] = acc[j % n_sums] + jnp.where(pred_fn(sc_ref[j * sub:(j + 1) * sub, :]), 1.0, 0.0)
        return jnp.sum(_tree_sum(acc), axis=0, keepdims=True)

    def key_to_float(key):
        return lax.bitcast_convert_type(jnp.where(key >= 0, key, key ^ 0x7FFFFFFF), F32)

    kf = jnp.float32(topk)
    n_rows = jnp.float32(nseg * DSA_SEG)
    cnt0 = count(lambda s: s >= 0.0)
    t0 = jnp.where(cnt0 >= kf, 0, INT_MIN).astype(I32)
    cnt_t0 = jnp.where(cnt0 >= kf, cnt0, n_rows)

    def bit_step(i, carry):
        t, cnt_t = carry
        cand = t + lax.shift_left(jnp.int32(1), 30 - i)
        cand_f = key_to_float(cand)
        cnt = jnp.where(cand <= KEY_NEG_INF, n_rows, count(lambda s: s >= cand_f))
        take = cnt >= kf
        return jnp.where(take, cand, t), jnp.where(take, cnt, cnt_t)

    thr_key, cnt_thr = lax.fori_loop(0, 31, bit_step, (t0, cnt_t0))
    thr = jnp.where(thr_key <= KEY_NEG_INF, -jnp.inf, key_to_float(thr_key))
    tri = tri_ref[...]

    surplus = jnp.logical_or(jnp.max(cnt_thr) > kf, jnp.min(thr) == -jnp.inf)

    @pl.when(surplus)
    def _():
        need = kf - count(lambda s: s > thr)
        prefix = jnp.zeros((1, tq), F32)
        for blk in blocks:
            sc = sc_ref[blk, :]
            eqf = jnp.where(sc == thr, 1.0, 0.0)
            rank = _dot(tri, eqf.astype(BF16)) + prefix
            tie = jnp.where(rank < need, eqf, 0.0)
            sel = jnp.where(sc > thr, 1.0, tie)
            bias_ref[blk, :] = jnp.where(sc > -jnp.inf, (sel - 1.0) * (-NEG_BIG), NEG_BIG)
            prefix = prefix + jnp.sum(eqf, axis=0, keepdims=True)

    @pl.when(jnp.logical_not(surplus))
    def _():
        for blk in blocks:
            bias_ref[blk, :] = jnp.where(sc_ref[blk, :] >= thr, 0.0, NEG_BIG)

    q8 = dq_ref[0].reshape(DSA_HEADS * tq, HEAD_DIM)
    m = jnp.full((1, DSA_HEADS * tq), NEG_BIG, F32)
    for blk in blocks:
        b = bias_ref[blk, :]
        s = _dot_nt(kvi_ref[0, blk, 0:HEAD_DIM], q8) + jnp.concatenate([b] * DSA_HEADS, axis=1)
        s_ref[blk, :] = s
        m = jnp.maximum(m, jnp.max(s, axis=0, keepdims=True))
    acc = jnp.zeros((128, DSA_HEADS * tq), F32)
    for blk in blocks:
        p = jnp.exp(s_ref[blk, :] - m)
        acc = acc + _dot(vt_ref[0, :, blk], p.astype(BF16))
    for hd in range(DSA_HEADS):
        a = acc[:, hd * tq:(hd + 1) * tq]
        o = (a / a[0:1, :]).T
        o_ref[0, :, hd * HEAD_DIM:(hd + 1) * HEAD_DIM] = o[:, HEAD_DIM:].astype(BF16)


def _dsa_attention(dq, iq, iw, kvi, vt, tq, topk):
    batch, _, seq, _ = dq.shape
    return pl.pallas_call(
        functools.partial(_dsa_kernel, tq=tq, topk=topk, nseg_max=seq // DSA_SEG),
        grid=(batch, seq // DSA_SEG),
        in_specs=[
            pl.BlockSpec((1, DSA_HEADS, DSA_SEG, HEAD_DIM), lambda b, i: (b, 0, i, 0)),
            pl.BlockSpec((1, DSA_SEG, 256), lambda b, i: (b, i, 0)),
            pl.BlockSpec((1, DSA_SEG, 128), lambda b, i: (b, i, 0)),
            pl.BlockSpec((1, seq, 256), lambda b, i: (b, 0, 0)),
            pl.BlockSpec((1, 128, seq), lambda b, i: (b, 0, 0)),
            pl.BlockSpec((DSA_KB, DSA_KB), lambda b, i: (0, 0)),
        ],
        out_specs=pl.BlockSpec((1, DSA_SEG, BRANCH_W), lambda b, i: (b, i, 0)),
        out_shape=jax.ShapeDtypeStruct((batch, seq, BRANCH_W), BF16),
        scratch_shapes=[pltpu.VMEM((seq, tq), F32), pltpu.VMEM((seq, tq), F32),
                        pltpu.VMEM((seq, DSA_HEADS * tq), F32)],
        compiler_params=_cparams(("parallel", "parallel")),
        name="dsa_attention",
    )(dq, iq.reshape(batch, seq, 256), iw.reshape(batch, seq, 128), kvi.reshape(batch, seq, 256), vt,
      _strict_tri(DSA_KB, lower=True))


def _memkv_kernel(m_ref, g_ref, w_ref, gk_ref, bd_ref, mk_ref, mv_ref):
    x = m_ref[...]
    ms = jnp.mean(x * x, axis=-1, keepdims=True)
    h = (x * lax.rsqrt(ms + EPS) * g_ref[...]).astype(BF16)
    mw = MEM_HEADS * MEM_HEAD_DIM
    k = _dot(h, w_ref[:, :mw])
    msk = _dot_split(k * k, bd_ref[...]) * (1.0 / MEM_HEAD_DIM)
    mk_ref[...] = (k * lax.rsqrt(msk + EPS) * gk_ref[...]).astype(BF16)
    mv_ref[...] = _dot(h, w_ref[:, mw:]).astype(BF16)


def _mem_kv(mem2d, g_mem, w_mem_kv, g_k_mem, tm):
    n = mem2d.shape[0]
    mw = MEM_HEADS * MEM_HEAD_DIM
    row = lambda i: (i, 0)
    const = lambda i: (0, 0)
    return pl.pallas_call(
        _memkv_kernel,
        grid=(n // tm,),
        in_specs=[pl.BlockSpec((tm, D_MODEL), row), pl.BlockSpec((1, D_MODEL), const),
                  pl.BlockSpec((D_MODEL, 2 * mw), const), pl.BlockSpec((1, mw), const),
                  pl.BlockSpec((mw, mw), const)],
        out_specs=[pl.BlockSpec((tm, mw), row), pl.BlockSpec((tm, mw), row)],
        out_shape=[jax.ShapeDtypeStruct((n, mw), BF16), jax.ShapeDtypeStruct((n, mw), BF16)],
        compiler_params=_cparams(("parallel",)),
        name="mem_kv",
    )(mem2d, g_mem[None, :], w_mem_kv.astype(BF16), jnp.tile(g_k_mem, MEM_HEADS)[None, :],
      _block_diag_ones(mw, MEM_HEAD_DIM))


def _memattn_kernel(q_ref, k_ref, v_ref, o_ref):
    for hd in range(MEM_HEADS):
        sl = slice(hd * MEM_HEAD_DIM, (hd + 1) * MEM_HEAD_DIM)
        s = _dot_nt(q_ref[0, :, sl], k_ref[0, :, sl]) * (MEM_HEAD_DIM ** -0.5)
        p = jnp.exp(s - jnp.max(s, axis=-1, keepdims=True))
        o = _dot(p.astype(BF16), v_ref[0, :, sl]) / jnp.sum(p, axis=-1, keepdims=True)
        o_ref[0, :, sl] = o.astype(BF16)


def _mem_attention(mq, mk, mv, tq):
    batch, seq, mw = mq.shape
    mlen = mk.shape[1]
    kspec = pl.BlockSpec((1, mlen, mw), lambda b, i: (b, 0, 0))
    return pl.pallas_call(
        _memattn_kernel,
        grid=(batch, seq // tq),
        in_specs=[pl.BlockSpec((1, tq, mw), lambda b, i: (b, i, 0)), kspec, kspec],
        out_specs=pl.BlockSpec((1, tq, mw), lambda b, i: (b, i, 0)),
        out_shape=jax.ShapeDtypeStruct((batch, seq, mw), BF16),
        compiler_params=_cparams(("parallel", "parallel")),
        name="mem_attention",
    )(mq, mk, mv)


def _merge_kernel(x_ref, g_ref, osb_ref, odsa_ref, omem_ref, wg_ref, bg_ref, wb_ref, wo_ref,
                  gf_ref, wrh_ref, wrl_ref, br_ref, tri_ref,
                  x2_ref, h2_ref, idx_ref, gate_ref, rank_ref, cnt_ref, carry_ref):
    x = x_ref[...]
    ms = jnp.mean(x * x, axis=-1, keepdims=True)
    h = (x * lax.rsqrt(ms + EPS) * g_ref[...]).astype(BF16)
    merged = None
    for n, o_ref in enumerate((osb_ref, odsa_ref, omem_ref)):
        gate = jax.nn.sigmoid(_dot(h, wg_ref[n]) + bg_ref[n])
        term = gate * _dot(o_ref[...], wb_ref[n])
        merged = term if merged is None else merged + term
    x2 = x + _dot(merged.astype(BF16), wo_ref[...])
    x2_ref[...] = x2

    ms2 = jnp.mean(x2 * x2, axis=-1, keepdims=True)
    h2 = x2 * lax.rsqrt(ms2 + EPS) * gf_ref[...]
    _store_token_tiles(h2_ref, h2)
    hi, lo = _split_bf16(h2)
    logits = _dot(hi, wrh_ref[...]) + _dot(hi, wrl_ref[...]) + _dot(lo, wrh_ref[...]) + br_ref[...]

    lane = lax.broadcasted_iota(I32, logits.shape, 1)
    vals = logits
    top_v, top_i = [], []
    for _ in range(TOP_K):
        mx = jnp.max(vals, axis=-1, keepdims=True)
        ix = jnp.min(jnp.where(vals == mx, lane, N_EXPERTS), axis=-1, keepdims=True)
        top_v.append(mx)
        top_i.append(ix)
        vals = jnp.where(lane == ix, -jnp.inf, vals)
    ex = [jnp.exp(v - top_v[0]) for v in top_v]
    den = ex[0] + ex[1] + ex[2] + ex[3]
    for k in range(TOP_K):
        idx_ref[:, k:k + 1] = top_i[k]
        gate_ref[:, k:k + 1] = ex[k] / den

    @pl.when(pl.program_id(0) == 0)
    def _():
        carry_ref[...] = jnp.zeros_like(carry_ref)

    lane_e = lax.broadcasted_iota(I32, (x.shape[0], LANES), 1)
    hits = [lane_e == top_i[k] for k in range(TOP_K)]
    onehot = sum(jnp.where(hk, 1.0, 0.0) for hk in hits)
    before = _dot(tri_ref[...], onehot.astype(BF16)) + carry_ref[...]
    for k in range(TOP_K):
        rank_ref[:, k:k + 1] = jnp.sum(jnp.where(hits[k], before, 0.0), axis=-1, keepdims=True).astype(I32)
    carry_ref[...] += jnp.sum(onehot, axis=0, keepdims=True)
    cnt_ref[...] = carry_ref[...]


def _merge_route(x2d, g_mix, o_sb, o_dsa, o_mem, w_gate, b_gate, w_branch, w_out, g_ffn, w_router, b_router, tm):
    n = x2d.shape[0]
    row = lambda i: (i, 0)
    c2 = lambda i: (0, 0)
    c3 = lambda i: (0, 0, 0)
    wr_hi = w_router.astype(BF16)
    wr_lo = (w_router - wr_hi.astype(F32)).astype(BF16)
    return pl.pallas_call(
        _merge_kernel,
        grid=(n // tm,),
        in_specs=[pl.BlockSpec((tm, D_MODEL), row), pl.BlockSpec((1, D_MODEL), c2),
                  pl.BlockSpec((tm, BRANCH_W), row), pl.BlockSpec((tm, BRANCH_W), row),
                  pl.BlockSpec((tm, BRANCH_W), row),
                  pl.BlockSpec((3, D_MODEL, D_MODEL), c3, pipeline_mode=pl.Buffered(1)),
                  pl.BlockSpec((3, 1, D_MODEL), c3),
                  pl.BlockSpec((3, BRANCH_W, D_MODEL), c3, pipeline_mode=pl.Buffered(1)),
                  pl.BlockSpec((D_MODEL, D_MODEL), c2, pipeline_mode=pl.Buffered(1)),
                  pl.BlockSpec((1, D_MODEL), c2), pl.BlockSpec((D_MODEL, N_EXPERTS), c2),
                  pl.BlockSpec((D_MODEL, N_EXPERTS), c2), pl.BlockSpec((1, N_EXPERTS), c2),
                  pl.BlockSpec((tm, tm), c2, pipeline_mode=pl.Buffered(1))],
        out_specs=[pl.BlockSpec((tm, D_MODEL), row), pl.BlockSpec((tm * TOKEN_TILE, LANES), row),
                   pl.BlockSpec((tm, TOP_K), row), pl.BlockSpec((tm, TOP_K), row),
                   pl.BlockSpec((tm, TOP_K), row), pl.BlockSpec((1, LANES), c2)],
        out_shape=[jax.ShapeDtypeStruct((n, D_MODEL), F32), jax.ShapeDtypeStruct((n * TOKEN_TILE, LANES), F32),
                   jax.ShapeDtypeStruct((n, TOP_K), I32), jax.ShapeDtypeStruct((n, TOP_K), F32),
                   jax.ShapeDtypeStruct((n, TOP_K), I32), jax.ShapeDtypeStruct((1, LANES), F32)],
        scratch_shapes=[pltpu.VMEM((1, LANES), F32)],
        compiler_params=_cparams(("arbitrary",)),
        name="merge_route",
    )(x2d, g_mix[None, :], o_sb, o_dsa, o_mem, w_gate.astype(BF16), b_gate[:, None, :],
      w_branch.astype(BF16), w_out.astype(BF16), g_ffn[None, :], wr_hi, wr_lo, b_router[None, :],
      _strict_tri(tm, lower=True))


DISPATCH_TOKENS = 4096
COMBINE_TOKENS = 512


def _dispatch_kernel(dest_ref, pad_ref, h_ref, xs_ref, zero_ref, sem, *, n_tok, n_pad):
    n_copies = n_tok * TOP_K + n_pad
    zero_ref[...] = jnp.zeros_like(zero_ref)

    def fill(j, c):
        for r in range(2):
            dst = pl.multiple_of(pad_ref[0, 0, 2 * j + r] * TOKEN_TILE, TOKEN_TILE)
            pltpu.make_async_copy(zero_ref, xs_ref.at[pl.ds(dst, TOKEN_TILE)], sem).start(priority=r)
        return c

    lax.fori_loop(0, n_pad // 2, fill, 0, unroll=2)

    def start(t, c):
        src = pl.multiple_of(t * TOKEN_TILE, TOKEN_TILE)
        for k in range(TOP_K):
            dst = pl.multiple_of(dest_ref[0, 0, t * TOP_K + k] * TOKEN_TILE, TOKEN_TILE)
            pltpu.make_async_copy(h_ref.at[pl.ds(src, TOKEN_TILE)], xs_ref.at[pl.ds(dst, TOKEN_TILE)],
                                  sem).start(priority=k % 2)
        return c

    lax.fori_loop(0, n_tok, start, 0, unroll=2)
    total = n_copies * TOKEN_TILE
    pltpu.make_async_copy(xs_ref.at[pl.ds(0, total)], xs_ref.at[pl.ds(0, total)], sem).wait()


def _dispatch(dest, pad_slots, h2t, n_slots):
    n = h2t.shape[0] // TOKEN_TILE
    n_tok = min(DISPATCH_TOKENS, n)
    steps = n // n_tok
    per = n_tok * TOP_K
    n_pad = pad_slots.shape[0] // steps
    assert n_tok * steps == n and n_pad * steps == pad_slots.shape[0] and n_pad % 2 == 0
    return pl.pallas_call(
        functools.partial(_dispatch_kernel, n_tok=n_tok, n_pad=n_pad),
        grid=(steps,),
        in_specs=[pl.BlockSpec((1, 1, per), lambda i: (i, 0, 0), memory_space=pltpu.SMEM),
                  pl.BlockSpec((1, 1, n_pad), lambda i: (i, 0, 0), memory_space=pltpu.SMEM),
                  pl.BlockSpec((n_tok * TOKEN_TILE, LANES), lambda i: (i, 0))],
        out_specs=pl.BlockSpec(memory_space=pl.ANY),
        out_shape=jax.ShapeDtypeStruct((n_slots * TOKEN_TILE, LANES), F32),
        scratch_shapes=[pltpu.VMEM((TOKEN_TILE, LANES), F32), pltpu.SemaphoreType.DMA(())],
        compiler_params=_cparams(("arbitrary",)),
        name="moe_dispatch",
    )(dest.reshape(steps, 1, per), pad_slots.reshape(steps, 1, n_pad), h2t)


def _expert_kernel(blk_e_ref, nused_ref, x_ref, w1_ref, b1_ref, w2_ref, b2_ref, y_ref, xb_ref, w1b_ref, w2b_ref):
    i = pl.program_id(0)
    new_expert = jnp.logical_or(i == 0, blk_e_ref[i] != blk_e_ref[jnp.maximum(i - 1, 0)])

    @pl.when(jnp.logical_and(i < nused_ref[0], new_expert))
    def _():
        w1b_ref[...] = w1_ref[0].astype(BF16)
        w2b_ref[...] = w2_ref[0].astype(BF16)

    @pl.when(i < nused_ref[0])
    def _():
        for s in range(TOKEN_TILE):
            xb_ref[:, s * LANES:(s + 1) * LANES] = _load_token_tiles(x_ref, 0, MOE_BLOCK, s).astype(BF16)
        hb = _dot(xb_ref[...], w1b_ref[...]) + b1_ref[0]
        g = jnp.minimum(hb[:, :D_EXPERT], SWIGLU_LIMIT)
        u = jnp.clip(hb[:, D_EXPERT:], -SWIGLU_LIMIT, SWIGLU_LIMIT)
        act = (u + 1.0) * (g * jax.nn.sigmoid(SWIGLU_ALPHA * g))
        _store_token_tiles(y_ref, _dot(act.astype(BF16), w2b_ref[...]) + b2_ref[0])

    @pl.when(pl.program_id(0) >= nused_ref[0])
    def _():
        y_ref[...] = jnp.zeros_like(y_ref)


def _experts(blk_e, n_used, xs, w_e_in, b_e_in, w_e_out, b_e_out):
    n_slots = xs.shape[0] // TOKEN_TILE
    nblk = n_slots // MOE_BLOCK
    slot_block = pl.BlockSpec((MOE_BLOCK * TOKEN_TILE, LANES), lambda i, be, nu: (i, 0))
    grid_spec = pltpu.PrefetchScalarGridSpec(
        num_scalar_prefetch=2,
        grid=(nblk,),
        in_specs=[slot_block,
                  pl.BlockSpec((1, D_MODEL, 2 * D_EXPERT), lambda i, be, nu: (be[i], 0, 0)),
                  pl.BlockSpec((1, 1, 2 * D_EXPERT), lambda i, be, nu: (be[i], 0, 0)),
                  pl.BlockSpec((1, D_EXPERT, D_MODEL), lambda i, be, nu: (be[i], 0, 0)),
                  pl.BlockSpec((1, 1, D_MODEL), lambda i, be, nu: (be[i], 0, 0))],
        out_specs=slot_block,
        scratch_shapes=[pltpu.VMEM((MOE_BLOCK, D_MODEL), BF16), pltpu.VMEM((D_MODEL, 2 * D_EXPERT), BF16),
                        pltpu.VMEM((D_EXPERT, D_MODEL), BF16)],
    )
    return pl.pallas_call(
        _expert_kernel,
        grid_spec=grid_spec,
        out_shape=jax.ShapeDtypeStruct((n_slots * TOKEN_TILE, LANES), F32),
        compiler_params=_cparams(("arbitrary",)),
        name="moe_experts",
    )(blk_e, n_used, xs, w_e_in, b_e_in[:, None, :], w_e_out, b_e_out[:, None, :])


def _combine_kernel(dest_ref, dest_one_ref, dest_ahead_ref, x_ref, gate_ref, y_ref, o_ref, buf_ref, sem):
    n_copies = COMBINE_TOKENS * TOP_K
    rows = 32
    i = pl.program_id(0)
    slot, slot1, slot2 = i % 3, (i + 1) % 3, (i + 2) % 3
    buf = buf_ref.at[slot]

    def start_rows(idx_ref, dst_buf, dma_sem, t0):
        for t in range(rows):
            for k in range(TOP_K):
                src = pl.multiple_of(idx_ref[0, 0, (t0 + t) * TOP_K + k] * TOKEN_TILE, TOKEN_TILE)
                dst = pl.multiple_of((k * COMBINE_TOKENS + t0 + t) * TOKEN_TILE, TOKEN_TILE)
                pltpu.make_async_copy(y_ref.at[pl.ds(src, TOKEN_TILE)], dst_buf.at[pl.ds(dst, TOKEN_TILE)],
                                      dma_sem).start(priority=k % 2)

    def wait_rows(dst_buf, dma_sem):
        pltpu.make_async_copy(y_ref.at[pl.ds(0, n_copies * TOKEN_TILE)], dst_buf, dma_sem).wait()

    @pl.when(i == 0)
    def _():
        def first(c, z):
            start_rows(dest_ref, buf_ref.at[0], sem.at[0], c * rows)
            start_rows(dest_one_ref, buf_ref.at[1], sem.at[1], c * rows)
            return z
        lax.fori_loop(0, COMBINE_TOKENS // rows, first, 0)

    wait_rows(buf, sem.at[slot])

    def sum_pass(c, z):
        r0 = pl.multiple_of(c * rows, rows)
        start_rows(dest_ahead_ref, buf_ref.at[slot2], sem.at[slot2], r0)
        gate = gate_ref[pl.ds(r0, rows), :]
        gates = [jnp.broadcast_to(gate[:, k:k + 1], (rows, LANES)) for k in range(TOP_K)]
        for s in range(TOKEN_TILE):
            out = x_ref[pl.ds(r0, rows), s * LANES:(s + 1) * LANES]
            for k in range(TOP_K):
                out = out + gates[k] * _load_token_tiles(buf, k * COMBINE_TOKENS + r0, rows, s)
            o_ref[pl.ds(r0, rows), s * LANES:(s + 1) * LANES] = out
        return z

    lax.fori_loop(0, COMBINE_TOKENS // rows, sum_pass, 0)

    @pl.when(i == pl.num_programs(0) - 1)
    def _():
        wait_rows(buf_ref.at[slot1], sem.at[slot1])
        wait_rows(buf_ref.at[slot2], sem.at[slot2])


def _combine(dest, x2, gate, ys):
    n = x2.shape[0]
    steps = n // COMBINE_TOKENS
    per = COMBINE_TOKENS * TOP_K
    dest3 = dest.reshape(steps, 1, per)
    return pl.pallas_call(
        _combine_kernel,
        grid=(steps,),
        in_specs=[pl.BlockSpec((1, 1, per), lambda i: (i, 0, 0), memory_space=pltpu.SMEM),
                  pl.BlockSpec((1, 1, per), lambda i: (min(1, steps - 1), 0, 0), memory_space=pltpu.SMEM),
                  pl.BlockSpec((1, 1, per), lambda i: (jnp.minimum(i + 2, steps - 1), 0, 0), memory_space=pltpu.SMEM),
                  pl.BlockSpec((COMBINE_TOKENS, D_MODEL), lambda i: (i, 0)),
                  pl.BlockSpec((COMBINE_TOKENS, TOP_K), lambda i: (i, 0)),
                  pl.BlockSpec(memory_space=pl.ANY)],
        out_specs=pl.BlockSpec((COMBINE_TOKENS, D_MODEL), lambda i: (i, 0)),
        out_shape=jax.ShapeDtypeStruct((n, D_MODEL), F32),
        scratch_shapes=[pltpu.VMEM((3, TOP_K * COMBINE_TOKENS * TOKEN_TILE, LANES), F32),
                        pltpu.SemaphoreType.DMA((3,))],
        compiler_params=_cparams(("arbitrary",)),
        name="moe_combine",
    )(dest3, dest3, dest3, x2, gate, ys)


def _moe(x2, h2, top_idx, gate, rank, counts, w_e_in, b_e_in, w_e_out, b_e_out):
    n = x2.shape[0]
    counts = counts[0, :N_EXPERTS].astype(I32)
    padded = (counts + MOE_BLOCK - 1) // MOE_BLOCK * MOE_BLOCK
    pend = jnp.cumsum(padded)
    pstart = pend - padded
    nblk = -(-(n * TOP_K) // MOE_BLOCK) + N_EXPERTS
    blk_start = jnp.arange(nblk, dtype=I32) * MOE_BLOCK
    blk_e = jnp.minimum(jnp.sum((pend[None, :] <= blk_start[:, None]).astype(I32), axis=1), N_EXPERTS - 1)
    n_used = (pend[-1:] // MOE_BLOCK).astype(I32)
    onehot = top_idx[:, :, None] == jnp.arange(N_EXPERTS, dtype=I32)[None, None, :]
    dest = rank + jnp.sum(jnp.where(onehot, pstart[None, None, :], 0), axis=-1)
    n_slots = nblk * MOE_BLOCK
    pad_len = padded - counts
    pad_end = jnp.cumsum(pad_len)
    base = jnp.concatenate([pstart + counts - (pad_end - pad_len), pend[-1:] - pad_end[-1:]])
    j = jnp.arange(n_slots - n * TOP_K, dtype=I32)
    group = jnp.sum((pad_end[None, :] <= j[:, None]).astype(I32), axis=1)
    group_hot = group[:, None] == jnp.arange(N_EXPERTS + 1, dtype=I32)[None, :]
    pad_slots = j + jnp.sum(jnp.where(group_hot, base[None, :], 0), axis=1)
    xs = _dispatch(dest, pad_slots, h2, n_slots)
    ys = _experts(blk_e, n_used, xs, w_e_in, b_e_in, w_e_out, b_e_out)
    return _combine(dest, x2, gate, ys)


def _layer(x, mem, g_mix, w_in, g_q_dsa, g_k_dsa, g_q_mem, g_k_mem, g_mem, w_mem_kv, w_gate, b_gate,
           w_branch, w_out, g_ffn, w_router, b_router, w_e_in, b_e_in, w_e_out, b_e_out):
    batch, seq, _ = x.shape
    n = batch * seq
    topk = min(DSA_TOPK_MAX, seq // 4)
    x2d = x.reshape(n, D_MODEL)
    tm_dense = min(DENSE_ROWS, seq)
    sq, sk, sv, dq, iq, kvi, iw, mq, vt = _inproj(x2d, batch, seq, g_mix, w_in, g_q_dsa, g_k_dsa, g_q_mem, tm_dense)
    o_sb = _sb_attention(sq, sk, sv, min(256, seq))
    o_dsa = _dsa_attention(dq, iq, iw, kvi, vt, 128, topk)
    mlen = mem.shape[1]
    mk, mv = _mem_kv(mem.reshape(batch * mlen, D_MODEL), g_mem, w_mem_kv, g_k_mem, min(512, batch * mlen))
    mw = MEM_HEADS * MEM_HEAD_DIM
    o_mem = _mem_attention(mq.reshape(batch, seq, mw), mk.reshape(batch, mlen, mw), mv.reshape(batch, mlen, mw), seq)
    x2, h2, top_idx, gate, rank, counts = _merge_route(
        x2d, g_mix, o_sb.reshape(n, BRANCH_W), o_dsa.reshape(n, BRANCH_W), o_mem.reshape(n, mw),
        w_gate, b_gate, w_branch, w_out, g_ffn, w_router, b_router, tm_dense)
    out = _moe(x2, h2, top_idx, gate, rank, counts, w_e_in, b_e_in, w_e_out, b_e_out)
    return out.reshape(batch, seq, D_MODEL)


def kernel(x, mem, g_mix, w_in, g_q_dsa, g_k_dsa, g_q_mem, g_k_mem, g_mem, w_mem_kv, w_gate, b_gate, w_branch, w_out, g_ffn, w_router, b_router, w_e_in, b_e_in, w_e_out, b_e_out):
    for l in range(g_mix.shape[0]):
        x = _layer(x, mem, g_mix[l], w_in[l], g_q_dsa[l], g_k_dsa[l], g_q_mem[l], g_k_mem[l], g_mem[l],
                   w_mem_kv[l], w_gate[l], b_gate[l], w_branch[l], w_out[l], g_ffn[l], w_router[l],
                   b_router[l], w_e_in[l], b_e_in[l], w_e_out[l], b_e_out[l])
    return x
```

```python
import functools

import numpy as np
import jax
import jax.numpy as jnp
from jax import lax
from jax.experimental import pallas as pl
from jax.experimental.pallas import tpu as pltpu

F32 = jnp.float32
BF16 = jnp.bfloat16
I32 = jnp.int32

D_MODEL = 1024
CHUNK = 64
SB_HEADS = 8
DSA_HEADS = 8
HEAD_DIM = 64
IDX_HEADS = 4
DSA_TOPK_MAX = 256
MEM_HEADS = 4
MEM_HEAD_DIM = 128
N_EXPERTS = 32
TOP_K = 4
D_EXPERT = D_MODEL
SWIGLU_LIMIT = 7.0
SWIGLU_ALPHA = 1.702
ROPE_THETA = 10000.0
EPS = 1e-6
MOE_BLOCK = 512

BRANCH_W = 512
IN_SIZES = (512, 512, 512, 512, 64, 64, 256, 64, 4, 512)
C_SQ, C_SK, C_DQ, C_MQ, C_IQ, C_SMALL, C_END = 0, 512, 1024, 1536, 2048, 2304, 2560
IW_LANE = 64

LANES = 128
NEG_BIG = -1e30
SB_CUTOFF = 110.0
KEY_NEG_INF = int(np.array(-np.inf, np.float32).view(np.int32)) ^ 0x7FFFFFFF
INT_MIN = -(2 ** 31)

VMEM_LIMIT = 56 * 1024 * 1024
DENSE_ROWS = 1024


def _cparams(sem):
    return pltpu.CompilerParams(dimension_semantics=sem, vmem_limit_bytes=VMEM_LIMIT)


def _dot(a, b):
    return jnp.dot(a, b, preferred_element_type=F32)


def _dot_nt(a, b):
    return lax.dot_general(a, b, (((1,), (1,)), ((), ())), preferred_element_type=F32)


def _split_bf16(x):
    hi = x.astype(BF16)
    lo = (x - hi.astype(F32)).astype(BF16)
    return hi, lo


def _dot_split(x, m_bf16):
    hi, lo = _split_bf16(x)
    return _dot(hi, m_bf16) + _dot(lo, m_bf16)


TOKEN_TILE = D_MODEL // LANES


def _store_token_tiles(ref, y):
    rows = y.shape[0]
    for s in range(TOKEN_TILE):
        ref[pl.ds(s, rows, stride=TOKEN_TILE), :] = y[:, s * LANES:(s + 1) * LANES]


def _load_token_tiles(ref, start_row, rows, s):
    return ref[pl.ds(start_row * TOKEN_TILE + s, rows, stride=TOKEN_TILE), :]


def _rot_half_unsigned(y):
    w = y.shape[1]
    lane = lax.broadcasted_iota(I32, y.shape, 1)
    return jnp.where((lane & 32) == 0, pltpu.roll(y, w - 32, 1), pltpu.roll(y, 32, 1))


def _inproj_kernel(x_ref, g_ref, w_ref, wsvt_ref, cos_ref, sin_ref, coss_ref, sins_ref, gq_ref, gks_ref,
                   gm_ref, bd64_ref, bd128_ref,
                   sq_ref, sk_ref, sv_ref, dq_ref, iq_ref, kvi_ref, iw_ref, mq_ref, vt_ref):
    x = x_ref[...]
    ms = jnp.mean(x * x, axis=-1, keepdims=True)
    h = (x * lax.rsqrt(ms + EPS) * g_ref[...]).astype(BF16)

    def seg(a, b):
        return _dot(h, w_ref[:, a:b])

    def put_heads(ref, y):
        for hd in range(y.shape[1] // HEAD_DIM):
            ref[0, hd] = y[:, hd * HEAD_DIM:(hd + 1) * HEAD_DIM].astype(BF16)

    put_heads(sq_ref, seg(C_SQ, C_SK) * (HEAD_DIM ** -0.5))
    put_heads(sk_ref, seg(C_SK, C_DQ))
    sv_ref[0] = _dot_nt(wsvt_ref[...], h).astype(BF16)

    y = seg(C_DQ, C_MQ)
    msq = _dot_split(y * y, bd64_ref[...]) * (1.0 / HEAD_DIM)
    y = y * lax.rsqrt(msq + EPS) * gq_ref[...]
    y = y * cos_ref[...] + _rot_half_unsigned(y) * sin_ref[...]
    put_heads(dq_ref, y * (HEAD_DIM ** -0.5))

    y = seg(C_IQ, C_SMALL)
    y = y * cos_ref[:, :256] + _rot_half_unsigned(y) * sin_ref[:, :256]
    iq_ref[...] = (y * (HEAD_DIM ** -0.5)).astype(BF16)

    y = seg(C_SMALL, C_END)
    lane = lax.broadcasted_iota(I32, y.shape, 1)
    is_k = lane < HEAD_DIM
    msk = jnp.sum(jnp.where(is_k, y * y, 0.0), axis=-1, keepdims=True) * (1.0 / HEAD_DIM)
    y = y * jnp.where(is_k, lax.rsqrt(msk + EPS) * gks_ref[...], 1.0)
    y = y * coss_ref[...] + _rot_half_unsigned(y) * sins_ref[...]
    kvi_ref[...] = y.astype(BF16)
    iw_ref[...] = y[:, 128:256]
    kv_t = y[:, 0:128].T
    row_t = lax.broadcasted_iota(I32, kv_t.shape, 0)
    vt_ref[0] = jnp.where(row_t < HEAD_DIM, 1.0, kv_t).astype(BF16)

    y = seg(C_MQ, C_IQ)
    msm = _dot_split(y * y, bd128_ref[...]) * (1.0 / MEM_HEAD_DIM)
    mq_ref[...] = (y * lax.rsqrt(msm + EPS) * gm_ref[...]).astype(BF16)


def _rope_tables(seq):
    half = HEAD_DIM // 2
    inv = ROPE_THETA ** (-jnp.arange(half, dtype=F32) / half)
    ang = jnp.arange(seq).astype(F32)[:, None] * inv[None, :]
    cos = jnp.cos(ang)
    sin = jnp.sin(ang)
    cos64 = jnp.concatenate([cos, cos], axis=1)
    sin64 = jnp.concatenate([-sin, sin], axis=1)
    one = jnp.ones_like(cos64)
    zero = jnp.zeros_like(cos64)
    cosq = jnp.tile(cos64, (1, 8))
    sinq = jnp.tile(sin64, (1, 8))
    coss = jnp.concatenate([cos64, one, cos64, one], axis=1)
    sins = jnp.concatenate([sin64, zero, sin64, zero], axis=1)
    return cosq, sinq, coss, sins


def _block_diag_ones(width, group):
    idx = np.arange(width) // group
    return jnp.asarray((idx[:, None] == idx[None, :]).astype(np.float32), dtype=BF16)


def _inproj(x2d, batch, seq, g_mix, w_in, g_q_dsa, g_k_dsa, g_q_mem, tm):
    n = x2d.shape[0]
    sizes = np.cumsum((0,) + IN_SIZES)
    col = {name: (int(sizes[i]), int(sizes[i + 1])) for i, name in enumerate(
        ("sq", "sk", "sv", "dq", "dk", "dv", "iq", "ik", "iw", "mq"))}
    order = ("sq", "sk", "dq", "mq", "iq", "dk", "dv", "ik", "iw")
    width = sum(col[k][1] - col[k][0] for k in order)
    w = jnp.concatenate([w_in[:, col[k][0]:col[k][1]] for k in order]
                        + [jnp.zeros((D_MODEL, C_END - width), w_in.dtype)], axis=1).astype(BF16)
    cosq, sinq, coss, sins = (jnp.asarray(t) for t in _rope_tables(seq))
    gq = jnp.tile(g_q_dsa, 8)[None, :]
    gks = jnp.concatenate([g_k_dsa, jnp.ones((256 - HEAD_DIM,), F32)])[None, :]
    gm = jnp.tile(g_q_mem, MEM_HEADS)[None, :]
    spb = seq // tm
    row = lambda i: (i, 0)
    const = lambda i: (0, 0)
    pos = lambda i: (i % spb, 0)
    heads = lambda i: (i // spb, 0, i % spb, 0)
    head_shape = jax.ShapeDtypeStruct((batch, 8, seq, HEAD_DIM), BF16)
    head_spec = pl.BlockSpec((1, 8, tm, HEAD_DIM), heads)
    return pl.pallas_call(
        _inproj_kernel,
        grid=(n // tm,),
        in_specs=[
            pl.BlockSpec((tm, D_MODEL), row),
            pl.BlockSpec((1, D_MODEL), const),
            pl.BlockSpec((D_MODEL, C_END), const, pipeline_mode=pl.Buffered(1)),
            pl.BlockSpec((512, D_MODEL), const, pipeline_mode=pl.Buffered(1)),
            pl.BlockSpec((tm, 512), pos), pl.BlockSpec((tm, 512), pos),
            pl.BlockSpec((tm, 256), pos), pl.BlockSpec((tm, 256), pos),
            pl.BlockSpec((1, 512), const), pl.BlockSpec((1, 256), const), pl.BlockSpec((1, 512), const),
            pl.BlockSpec((512, 512), const), pl.BlockSpec((512, 512), const),
        ],
        out_specs=[head_spec, head_spec, pl.BlockSpec((1, 512, tm), lambda i: (i // spb, 0, i % spb)), head_spec,
                   pl.BlockSpec((tm, 256), row), pl.BlockSpec((tm, 256), row),
                   pl.BlockSpec((tm, 128), row), pl.BlockSpec((tm, 512), row),
                   pl.BlockSpec((1, 128, tm), lambda i: (i // spb, 0, i % spb))],
        out_shape=[head_shape, head_shape, jax.ShapeDtypeStruct((batch, 512, seq), BF16), head_shape,
                   jax.ShapeDtypeStruct((n, 256), BF16), jax.ShapeDtypeStruct((n, 256), BF16),
                   jax.ShapeDtypeStruct((n, 128), F32), jax.ShapeDtypeStruct((n, 512), BF16),
                   jax.ShapeDtypeStruct((batch, 128, seq), BF16)],
        compiler_params=_cparams(("parallel",)),
        name="inproj",
    )(x2d, g_mix[None, :], w, w_in[:, col["sv"][0]:col["sv"][1]].T.astype(BF16), cosq, sinq, coss, sins, gq, gks, gm,
      _block_diag_ones(512, HEAD_DIM), _block_diag_ones(512, MEM_HEAD_DIM))


def _sb_kernel(q_ref, k_ref, vt_ref, u_ref, o_ref, acc_ref, car_ref, *, tq, blocks_per_step):
    def query_block(r, carry):
        rows = pl.ds(pl.multiple_of(r * tq, tq), tq)
        _sb_sweep(q_ref.at[:, :, rows, :], k_ref, vt_ref, u_ref, o_ref.at[:, rows, :], acc_ref, car_ref,
                  qi=pl.program_id(1) * blocks_per_step + r, tq=tq)
        return carry

    lax.fori_loop(0, blocks_per_step, query_block, 0)


def _sb_sweep(q_ref, k_ref, vt_ref, u_ref, o_ref, acc_ref, car_ref, *, qi, tq):
    rows = lax.broadcasted_iota(I32, (tq, tq), 0)
    cols = lax.broadcasted_iota(I32, (tq, tq), 1)
    dif = rows - cols
    u = u_ref[...]
    acc_ref[...] = jnp.zeros_like(acc_ref)
    car_ref[...] = jnp.zeros_like(car_ref)

    def cond(c):
        kb, mx = c
        return jnp.logical_and(kb >= 0, mx > -SB_CUTOFF)

    def body(c):
        kb, _ = c
        ks = pl.multiple_of(kb * tq, tq)
        earlier = dif < (qi - kb) * tq
        neg_mask = jnp.where(earlier, -1.0, 0.0).astype(BF16)
        heads = range(SB_HEADS)
        z = [_dot_nt(k_ref[0, hd, pl.ds(ks, tq), :], q_ref[0, hd]) for hd in heads]
        ls, lk, between = [], [], []
        for hd in heads:
            zb = z[hd].astype(BF16)
            sp = jnp.maximum(zb, 0.0) + jnp.log(1.0 + jnp.exp(-jnp.abs(zb)))
            ls.append(z[hd] - sp.astype(F32))
            lk.append(sp * neg_mask)
            between.append(_dot(u, lk[hd]))
        for hd in heads:
            rs = slice(hd * HEAD_DIM, (hd + 1) * HEAD_DIM)
            car = car_ref[hd:hd + 1, :]
            w = jnp.where(earlier, jnp.exp(ls[hd] + between[hd] + car), 0.0)
            acc_ref[rs, :] += _dot(vt_ref[0, rs, pl.ds(ks, tq)], w.astype(BF16))
            car_ref[hd:hd + 1, :] = car + (between[hd][0:1, :] + lk[hd][0:1, :].astype(F32))
        return kb - 1, jnp.max(car_ref[...])

    lax.while_loop(cond, body, (qi, jnp.float32(0.0)))
    o_ref[0] = acc_ref[...].T.astype(BF16)


def _strict_tri(n, lower):
    i = np.arange(n)
    m = (i[:, None] > i[None, :]) if lower else (i[:, None] < i[None, :])
    return jnp.asarray(m.astype(np.float32), dtype=BF16)


SB_BLOCKS_PER_STEP = 4


def _sb_attention(sq, sk, svt, tq):
    batch, _, seq, _ = sq.shape
    per_step = min(SB_BLOCKS_PER_STEP, seq // tq)
    return pl.pallas_call(
        functools.partial(_sb_kernel, tq=tq, blocks_per_step=per_step),
        grid=(batch, seq // (tq * per_step)),
        in_specs=[pl.BlockSpec((1, SB_HEADS, tq * per_step, HEAD_DIM), lambda b, i: (b, 0, i, 0)),
                  pl.BlockSpec((1, SB_HEADS, seq, HEAD_DIM), lambda b, i: (b, 0, 0, 0)),
                  pl.BlockSpec((1, BRANCH_W, seq), lambda b, i: (b, 0, 0)),
                  pl.BlockSpec((tq, tq), lambda b, i: (0, 0))],
        out_specs=pl.BlockSpec((1, tq * per_step, BRANCH_W), lambda b, i: (b, i, 0)),
        out_shape=jax.ShapeDtypeStruct((batch, seq, BRANCH_W), BF16),
        scratch_shapes=[pltpu.VMEM((BRANCH_W, tq), F32), pltpu.VMEM((SB_HEADS, tq), F32)],
        compiler_params=_cparams(("parallel", "parallel")),
        name="sb_attention",
    )(sq, sk, svt, _strict_tri(tq, lower=False))


DSA_SEG = 256
DSA_KB = DSA_SEG


def _tree_sum(parts):
    while len(parts) > 1:
        parts = [parts[i] + parts[i + 1] for i in range(0, len(parts) - 1, 2)] + ([parts[-1]] if len(parts) % 2 else [])
    return parts[0]


def _dsa_kernel(dq_ref, iq_ref, iw_ref, kvi_ref, vt_ref, tri_ref, o_ref,
                sc_ref, bias_ref, s_ref, *, tq, topk, nseg_max):
    nseg = pl.program_id(1) + 1

    def query_block(r, carry, ns):
        rows = pl.ds(pl.multiple_of(r * tq, tq), tq)
        _dsa_body(dq_ref.at[:, :, rows, :], iq_ref.at[:, rows, :], iw_ref.at[:, rows, :], kvi_ref, vt_ref, tri_ref,
                  o_ref.at[:, rows, :], sc_ref, bias_ref, s_ref,
                  qs=pl.program_id(1) * DSA_SEG + r * tq, tq=tq, topk=topk, nseg=ns)
        return carry

    for ns in range(1, nseg_max + 1):
        @pl.when(nseg == ns)
        def _(ns=ns):
            lax.fori_loop(0, DSA_SEG // tq, functools.partial(query_block, ns=ns), 0)


def _dsa_body(dq_ref, iq_ref, iw_ref, kvi_ref, vt_ref, tri_ref, o_ref, sc_ref, bias_ref, s_ref,
              *, qs, tq, topk, nseg):
    blocks = [slice(c * DSA_KB, (c + 1) * DSA_KB) for c in range(nseg)]
    iq = iq_ref[0]
    w_t = iw_ref[0].T
    w_row = [w_t[IW_LANE + h:IW_LANE + h + 1, :] * (IDX_HEADS ** -0.5) for h in range(IDX_HEADS)]
    q_chunk = (qs + lax.broadcasted_iota(I32, (DSA_KB, tq), 1)) // CHUNK
    k_chunk = lax.broadcasted_iota(I32, (DSA_KB, tq), 0) // CHUNK

    for c, blk in enumerate(blocks):
        ik = kvi_ref[0, blk, 128:192]
        lg = [_dot_nt(ik, iq[:, h * HEAD_DIM:(h + 1) * HEAD_DIM]) for h in range(IDX_HEADS)]
        sc = jnp.zeros((DSA_KB, tq), F32)
        for h in range(IDX_HEADS):
            sc = sc + w_row[h] * jnp.maximum(lg[h], 0.0)
        admissible = (c * (DSA_KB // CHUNK) + k_chunk) <= q_chunk
        sc_ref[blk, :] = jnp.where(admissible, sc, -jnp.inf)

    def count(pred_fn):
        sub, n_sums = 32, 4
        acc = [jnp.zeros((sub, tq), F32)] * n_sums
        for j in range(nseg * DSA_SEG // sub):
            acc[j % n_sums] = acc[j % n_sums] + jnp.where(pred_fn(sc_ref[j * sub:(j + 1) * sub, :]), 1.0, 0.0)
        return jnp.sum(_tree_sum(acc), axis=0, keepdims=True)

    def key_to_float(key):
        return lax.bitcast_convert_type(jnp.where(key >= 0, key, key ^ 0x7FFFFFFF), F32)

    kf = jnp.float32(topk)
    n_rows = jnp.float32(nseg * DSA_SEG)
    cnt0 = count(lambda s: s >= 0.0)
    t0 = jnp.where(cnt0 >= kf, 0, INT_MIN).astype(I32)
    cnt_t0 = jnp.where(cnt0 >= kf, cnt0, n_rows)

    def bit_step(i, carry):
        t, cnt_t = carry
        cand = t + lax.shift_left(jnp.int32(1), 30 - i)
        cand_f = key_to_float(cand)
        cnt = jnp.where(cand <= KEY_NEG_INF, n_rows, count(lambda s: s >= cand_f))
        take = cnt >= kf
        return jnp.where(take, cand, t), jnp.where(take, cnt, cnt_t)

    thr_key, cnt_thr = lax.fori_loop(0, 31, bit_step, (t0, cnt_t0))
    thr = jnp.where(thr_key <= KEY_NEG_INF, -jnp.inf, key_to_float(thr_key))
    tri = tri_ref[...]

    surplus = jnp.logical_or(jnp.max(cnt_thr) > kf, jnp.min(thr) == -jnp.inf)

    @pl.when(surplus)
    def _():
        need = kf - count(lambda s: s > thr)
        prefix = jnp.zeros((1, tq), F32)
        for blk in blocks:
            sc = sc_ref[blk, :]
            eqf = jnp.where(sc == thr, 1.0, 0.0)
            rank = _dot(tri, eqf.astype(BF16)) + prefix
            tie = jnp.where(rank < need, eqf, 0.0)
            sel = jnp.where(sc > thr, 1.0, tie)
            bias_ref[blk, :] = jnp.where(sc > -jnp.inf, (sel - 1.0) * (-NEG_BIG), NEG_BIG)
            prefix = prefix + jnp.sum(eqf, axis=0, keepdims=True)

    @pl.when(jnp.logical_not(surplus))
    def _():
        for blk in blocks:
            bias_ref[blk, :] = jnp.where(sc_ref[blk, :] >= thr, 0.0, NEG_BIG)

    q8 = dq_ref[0].reshape(DSA_HEADS * tq, HEAD_DIM)
    m = jnp.full((1, DSA_HEADS * tq), NEG_BIG, F32)
    for blk in blocks:
        b = bias_ref[blk, :]
        s = _dot_nt(kvi_ref[0, blk, 0:HEAD_DIM], q8) + jnp.concatenate([b] * DSA_HEADS, axis=1)
        s_ref[blk, :] = s
        m = jnp.maximum(m, jnp.max(s, axis=0, keepdims=True))
    acc = jnp.zeros((128, DSA_HEADS * tq), F32)
    for blk in blocks:
        p = jnp.exp(s_ref[blk, :] - m)
        acc = acc + _dot(vt_ref[0, :, blk], p.astype(BF16))
    for hd in range(DSA_HEADS):
        a = acc[:, hd * tq:(hd + 1) * tq]
        o = (a / a[0:1, :]).T
        o_ref[0, :, hd * HEAD_DIM:(hd + 1) * HEAD_DIM] = o[:, HEAD_DIM:].astype(BF16)


def _dsa_attention(dq, iq, iw, kvi, vt, tq, topk):
    batch, _, seq, _ = dq.shape
    return pl.pallas_call(
        functools.partial(_dsa_kernel, tq=tq, topk=topk, nseg_max=seq // DSA_SEG),
        grid=(batch, seq // DSA_SEG),
        in_specs=[
            pl.BlockSpec((1, DSA_HEADS, DSA_SEG, HEAD_DIM), lambda b, i: (b, 0, i, 0)),
            pl.BlockSpec((1, DSA_SEG, 256), lambda b, i: (b, i, 0)),
            pl.BlockSpec((1, DSA_SEG, 128), lambda b, i: (b, i, 0)),
            pl.BlockSpec((1, seq, 256), lambda b, i: (b, 0, 0)),
            pl.BlockSpec((1, 128, seq), lambda b, i: (b, 0, 0)),
            pl.BlockSpec((DSA_KB, DSA_KB), lambda b, i: (0, 0)),
        ],
        out_specs=pl.BlockSpec((1, DSA_SEG, BRANCH_W), lambda b, i: (b, i, 0)),
        out_shape=jax.ShapeDtypeStruct((batch, seq, BRANCH_W), BF16),
        scratch_shapes=[pltpu.VMEM((seq, tq), F32), pltpu.VMEM((seq, tq), F32),
                        pltpu.VMEM((seq, DSA_HEADS * tq), F32)],
        compiler_params=_cparams(("parallel", "parallel")),
        name="dsa_attention",
    )(dq, iq.reshape(batch, seq, 256), iw.reshape(batch, seq, 128), kvi.reshape(batch, seq, 256), vt,
      _strict_tri(DSA_KB, lower=True))


def _memkv_kernel(m_ref, g_ref, w_ref, gk_ref, bd_ref, mk_ref, mv_ref):
    x = m_ref[...]
    ms = jnp.mean(x * x, axis=-1, keepdims=True)
    h = (x * lax.rsqrt(ms + EPS) * g_ref[...]).astype(BF16)
    mw = MEM_HEADS * MEM_HEAD_DIM
    k = _dot(h, w_ref[:, :mw])
    msk = _dot_split(k * k, bd_ref[...]) * (1.0 / MEM_HEAD_DIM)
    mk_ref[...] = (k * lax.rsqrt(msk + EPS) * gk_ref[...]).astype(BF16)
    mv_ref[...] = _dot(h, w_ref[:, mw:]).astype(BF16)


def _mem_kv(mem2d, g_mem, w_mem_kv, g_k_mem, tm):
    n = mem2d.shape[0]
    mw = MEM_HEADS * MEM_HEAD_DIM
    row = lambda i: (i, 0)
    const = lambda i: (0, 0)
    return pl.pallas_call(
        _memkv_kernel,
        grid=(n // tm,),
        in_specs=[pl.BlockSpec((tm, D_MODEL), row), pl.BlockSpec((1, D_MODEL), const),
                  pl.BlockSpec((D_MODEL, 2 * mw), const), pl.BlockSpec((1, mw), const),
                  pl.BlockSpec((mw, mw), const)],
        out_specs=[pl.BlockSpec((tm, mw), row), pl.BlockSpec((tm, mw), row)],
        out_shape=[jax.ShapeDtypeStruct((n, mw), BF16), jax.ShapeDtypeStruct((n, mw), BF16)],
        compiler_params=_cparams(("parallel",)),
        name="mem_kv",
    )(mem2d, g_mem[None, :], w_mem_kv.astype(BF16), jnp.tile(g_k_mem, MEM_HEADS)[None, :],
      _block_diag_ones(mw, MEM_HEAD_DIM))


def _memattn_kernel(q_ref, k_ref, v_ref, o_ref):
    for hd in range(MEM_HEADS):
        sl = slice(hd * MEM_HEAD_DIM, (hd + 1) * MEM_HEAD_DIM)
        s = _dot_nt(q_ref[0, :, sl], k_ref[0, :, sl]) * (MEM_HEAD_DIM ** -0.5)
        p = jnp.exp(s - jnp.max(s, axis=-1, keepdims=True))
        o = _dot(p.astype(BF16), v_ref[0, :, sl]) / jnp.sum(p, axis=-1, keepdims=True)
        o_ref[0, :, sl] = o.astype(BF16)


def _mem_attention(mq, mk, mv, tq):
    batch, seq, mw = mq.shape
    mlen = mk.shape[1]
    kspec = pl.BlockSpec((1, mlen, mw), lambda b, i: (b, 0, 0))
    return pl.pallas_call(
        _memattn_kernel,
        grid=(batch, seq // tq),
        in_specs=[pl.BlockSpec((1, tq, mw), lambda b, i: (b, i, 0)), kspec, kspec],
        out_specs=pl.BlockSpec((1, tq, mw), lambda b, i: (b, i, 0)),
        out_shape=jax.ShapeDtypeStruct((batch, seq, mw), BF16),
        compiler_params=_cparams(("parallel", "parallel")),
        name="mem_attention",
    )(mq, mk, mv)


def _merge_kernel(x_ref, g_ref, osb_ref, odsa_ref, omem_ref, wg_ref, bg_ref, wb_ref, wo_ref,
                  gf_ref, wrh_ref, wrl_ref, br_ref, tri_ref,
                  x2_ref, h2_ref, idx_ref, gate_ref, rank_ref, cnt_ref, carry_ref):
    x = x_ref[...]
    ms = jnp.mean(x * x, axis=-1, keepdims=True)
    h = (x * lax.rsqrt(ms + EPS) * g_ref[...]).astype(BF16)
    merged = None
    for n, o_ref in enumerate((osb_ref, odsa_ref, omem_ref)):
        gate = jax.nn.sigmoid(_dot(h, wg_ref[n]) + bg_ref[n])
        term = gate * _dot(o_ref[...], wb_ref[n])
        merged = term if merged is None else merged + term
    x2 = x + _dot(merged.astype(BF16), wo_ref[...])
    x2_ref[...] = x2

    ms2 = jnp.mean(x2 * x2, axis=-1, keepdims=True)
    h2 = x2 * lax.rsqrt(ms2 + EPS) * gf_ref[...]
    _store_token_tiles(h2_ref, h2)
    hi, lo = _split_bf16(h2)
    logits = _dot(hi, wrh_ref[...]) + _dot(hi, wrl_ref[...]) + _dot(lo, wrh_ref[...]) + br_ref[...]

    lane = lax.broadcasted_iota(I32, logits.shape, 1)
    vals = logits
    top_v, top_i = [], []
    for _ in range(TOP_K):
        mx = jnp.max(vals, axis=-1, keepdims=True)
        ix = jnp.min(jnp.where(vals == mx, lane, N_EXPERTS), axis=-1, keepdims=True)
        top_v.append(mx)
        top_i.append(ix)
        vals = jnp.where(lane == ix, -jnp.inf, vals)
    ex = [jnp.exp(v - top_v[0]) for v in top_v]
    den = ex[0] + ex[1] + ex[2] + ex[3]
    for k in range(TOP_K):
        idx_ref[:, k:k + 1] = top_i[k]
        gate_ref[:, k:k + 1] = ex[k] / den

    @pl.when(pl.program_id(0) == 0)
    def _():
        carry_ref[...] = jnp.zeros_like(carry_ref)

    lane_e = lax.broadcasted_iota(I32, (x.shape[0], LANES), 1)
    hits = [lane_e == top_i[k] for k in range(TOP_K)]
    onehot = sum(jnp.where(hk, 1.0, 0.0) for hk in hits)
    before = _dot(tri_ref[...], onehot.astype(BF16)) + carry_ref[...]
    for k in range(TOP_K):
        rank_ref[:, k:k + 1] = jnp.sum(jnp.where(hits[k], before, 0.0), axis=-1, keepdims=True).astype(I32)
    carry_ref[...] += jnp.sum(onehot, axis=0, keepdims=True)
    cnt_ref[...] = carry_ref[...]


def _merge_route(x2d, g_mix, o_sb, o_dsa, o_mem, w_gate, b_gate, w_branch, w_out, g_ffn, w_router, b_router, tm):
    n = x2d.shape[0]
    row = lambda i: (i, 0)
    c2 = lambda i: (0, 0)
    c3 = lambda i: (0, 0, 0)
    wr_hi = w_router.astype(BF16)
    wr_lo = (w_router - wr_hi.astype(F32)).astype(BF16)
    return pl.pallas_call(
        _merge_kernel,
        grid=(n // tm,),
        in_specs=[pl.BlockSpec((tm, D_MODEL), row), pl.BlockSpec((1, D_MODEL), c2),
                  pl.BlockSpec((tm, BRANCH_W), row), pl.BlockSpec((tm, BRANCH_W), row),
                  pl.BlockSpec((tm, BRANCH_W), row),
                  pl.BlockSpec((3, D_MODEL, D_MODEL), c3, pipeline_mode=pl.Buffered(1)),
                  pl.BlockSpec((3, 1, D_MODEL), c3),
                  pl.BlockSpec((3, BRANCH_W, D_MODEL), c3, pipeline_mode=pl.Buffered(1)),
                  pl.BlockSpec((D_MODEL, D_MODEL), c2, pipeline_mode=pl.Buffered(1)),
                  pl.BlockSpec((1, D_MODEL), c2), pl.BlockSpec((D_MODEL, N_EXPERTS), c2),
                  pl.BlockSpec((D_MODEL, N_EXPERTS), c2), pl.BlockSpec((1, N_EXPERTS), c2),
                  pl.BlockSpec((tm, tm), c2, pipeline_mode=pl.Buffered(1))],
        out_specs=[pl.BlockSpec((tm, D_MODEL), row), pl.BlockSpec((tm * TOKEN_TILE, LANES), row),
                   pl.BlockSpec((tm, TOP_K), row), pl.BlockSpec((tm, TOP_K), row),
                   pl.BlockSpec((tm, TOP_K), row), pl.BlockSpec((1, LANES), c2)],
        out_shape=[jax.ShapeDtypeStruct((n, D_MODEL), F32), jax.ShapeDtypeStruct((n * TOKEN_TILE, LANES), F32),
                   jax.ShapeDtypeStruct((n, TOP_K), I32), jax.ShapeDtypeStruct((n, TOP_K), F32),
                   jax.ShapeDtypeStruct((n, TOP_K), I32), jax.ShapeDtypeStruct((1, LANES), F32)],
        scratch_shapes=[pltpu.VMEM((1, LANES), F32)],
        compiler_params=_cparams(("arbitrary",)),
        name="merge_route",
    )(x2d, g_mix[None, :], o_sb, o_dsa, o_mem, w_gate.astype(BF16), b_gate[:, None, :],
      w_branch.astype(BF16), w_out.astype(BF16), g_ffn[None, :], wr_hi, wr_lo, b_router[None, :],
      _strict_tri(tm, lower=True))


DISPATCH_TOKENS = 4096
COMBINE_TOKENS = 512


def _dispatch_kernel(dest_ref, pad_ref, h_ref, xs_ref, zero_ref, sem, *, n_tok, n_pad):
    n_copies = n_tok * TOP_K + n_pad
    zero_ref[...] = jnp.zeros_like(zero_ref)

    def fill(j, c):
        for r in range(2):
            dst = pl.multiple_of(pad_ref[0, 0, 2 * j + r] * TOKEN_TILE, TOKEN_TILE)
            pltpu.make_async_copy(zero_ref, xs_ref.at[pl.ds(dst, TOKEN_TILE)], sem).start(priority=r)
        return c

    lax.fori_loop(0, n_pad // 2, fill, 0, unroll=2)

    def start(t, c):
        src = pl.multiple_of(t * TOKEN_TILE, TOKEN_TILE)
        for k in range(TOP_K):
            dst = pl.multiple_of(dest_ref[0, 0, t * TOP_K + k] * TOKEN_TILE, TOKEN_TILE)
            pltpu.make_async_copy(h_ref.at[pl.ds(src, TOKEN_TILE)], xs_ref.at[pl.ds(dst, TOKEN_TILE)],
                                  sem).start(priority=k % 2)
        return c

    lax.fori_loop(0, n_tok, start, 0, unroll=2)
    total = n_copies * TOKEN_TILE
    pltpu.make_async_copy(xs_ref.at[pl.ds(0, total)], xs_ref.at[pl.ds(0, total)], sem).wait()


def _dispatch(dest, pad_slots, h2t, n_slots):
    n = h2t.shape[0] // TOKEN_TILE
    n_tok = min(DISPATCH_TOKENS, n)
    steps = n // n_tok
    per = n_tok * TOP_K
    n_pad = pad_slots.shape[0] // steps
    assert n_tok * steps == n and n_pad * steps == pad_slots.shape[0] and n_pad % 2 == 0
    return pl.pallas_call(
        functools.partial(_dispatch_kernel, n_tok=n_tok, n_pad=n_pad),
        grid=(steps,),
        in_specs=[pl.BlockSpec((1, 1, per), lambda i: (i, 0, 0), memory_space=pltpu.SMEM),
                  pl.BlockSpec((1, 1, n_pad), lambda i: (i, 0, 0), memory_space=pltpu.SMEM),
                  pl.BlockSpec((n_tok * TOKEN_TILE, LANES), lambda i: (i, 0))],
        out_specs=pl.BlockSpec(memory_space=pl.ANY),
        out_shape=jax.ShapeDtypeStruct((n_slots * TOKEN_TILE, LANES), F32),
        scratch_shapes=[pltpu.VMEM((TOKEN_TILE, LANES), F32), pltpu.SemaphoreType.DMA(())],
        compiler_params=_cparams(("arbitrary",)),
        name="moe_dispatch",
    )(dest.reshape(steps, 1, per), pad_slots.reshape(steps, 1, n_pad), h2t)


def _expert_kernel(blk_e_ref, nused_ref, x_ref, w1_ref, b1_ref, w2_ref, b2_ref, y_ref, xb_ref, w1b_ref, w2b_ref):
    i = pl.program_id(0)
    new_expert = jnp.logical_or(i == 0, blk_e_ref[i] != blk_e_ref[jnp.maximum(i - 1, 0)])

    @pl.when(jnp.logical_and(i < nused_ref[0], new_expert))
    def _():
        w1b_ref[...] = w1_ref[0].astype(BF16)
        w2b_ref[...] = w2_ref[0].astype(BF16)

    @pl.when(i < nused_ref[0])
    def _():
        for s in range(TOKEN_TILE):
            xb_ref[:, s * LANES:(s + 1) * LANES] = _load_token_tiles(x_ref, 0, MOE_BLOCK, s).astype(BF16)
        hb = _dot(xb_ref[...], w1b_ref[...]) + b1_ref[0]
        g = jnp.minimum(hb[:, :D_EXPERT], SWIGLU_LIMIT)
        u = jnp.clip(hb[:, D_EXPERT:], -SWIGLU_LIMIT, SWIGLU_LIMIT)
        act = (u + 1.0) * (g * jax.nn.sigmoid(SWIGLU_ALPHA * g))
        _store_token_tiles(y_ref, _dot(act.astype(BF16), w2b_ref[...]) + b2_ref[0])

    @pl.when(pl.program_id(0) >= nused_ref[0])
    def _():
        y_ref[...] = jnp.zeros_like(y_ref)


def _experts(blk_e, n_used, xs, w_e_in, b_e_in, w_e_out, b_e_out):
    n_slots = xs.shape[0] // TOKEN_TILE
    nblk = n_slots // MOE_BLOCK
    slot_block = pl.BlockSpec((MOE_BLOCK * TOKEN_TILE, LANES), lambda i, be, nu: (i, 0))
    grid_spec = pltpu.PrefetchScalarGridSpec(
        num_scalar_prefetch=2,
        grid=(nblk,),
        in_specs=[slot_block,
                  pl.BlockSpec((1, D_MODEL, 2 * D_EXPERT), lambda i, be, nu: (be[i], 0, 0)),
                  pl.BlockSpec((1, 1, 2 * D_EXPERT), lambda i, be, nu: (be[i], 0, 0)),
                  pl.BlockSpec((1, D_EXPERT, D_MODEL), lambda i, be, nu: (be[i], 0, 0)),
                  pl.BlockSpec((1, 1, D_MODEL), lambda i, be, nu: (be[i], 0, 0))],
        out_specs=slot_block,
        scratch_shapes=[pltpu.VMEM((MOE_BLOCK, D_MODEL), BF16), pltpu.VMEM((D_MODEL, 2 * D_EXPERT), BF16),
                        pltpu.VMEM((D_EXPERT, D_MODEL), BF16)],
    )
    return pl.pallas_call(
        _expert_kernel,
        grid_spec=grid_spec,
        out_shape=jax.ShapeDtypeStruct((n_slots * TOKEN_TILE, LANES), F32),
        compiler_params=_cparams(("arbitrary",)),
        name="moe_experts",
    )(blk_e, n_used, xs, w_e_in, b_e_in[:, None, :], w_e_out, b_e_out[:, None, :])


def _combine_kernel(dest_ref, dest_one_ref, dest_ahead_ref, x_ref, gate_ref, y_ref, o_ref, buf_ref, sem):
    n_copies = COMBINE_TOKENS * TOP_K
    rows = 32
    i = pl.program_id(0)
    slot, slot1, slot2 = i % 3, (i + 1) % 3, (i + 2) % 3
    buf = buf_ref.at[slot]

    def start_rows(idx_ref, dst_buf, dma_sem, t0):
        for t in range(rows):
            for k in range(TOP_K):
                src = pl.multiple_of(idx_ref[0, 0, (t0 + t) * TOP_K + k] * TOKEN_TILE, TOKEN_TILE)
                dst = pl.multiple_of((k * COMBINE_TOKENS + t0 + t) * TOKEN_TILE, TOKEN_TILE)
                pltpu.make_async_copy(y_ref.at[pl.ds(src, TOKEN_TILE)], dst_buf.at[pl.ds(dst, TOKEN_TILE)],
                                      dma_sem).start(priority=k % 2)

    def wait_rows(dst_buf, dma_sem):
        pltpu.make_async_copy(y_ref.at[pl.ds(0, n_copies * TOKEN_TILE)], dst_buf, dma_sem).wait()

    @pl.when(i == 0)
    def _():
        def first(c, z):
            start_rows(dest_ref, buf_ref.at[0], sem.at[0], c * rows)
            start_rows(dest_one_ref, buf_ref.at[1], sem.at[1], c * rows)
            return z
        lax.fori_loop(0, COMBINE_TOKENS // rows, first, 0)

    wait_rows(buf, sem.at[slot])

    def sum_pass(c, z):
        r0 = pl.multiple_of(c * rows, rows)
        start_rows(dest_ahead_ref, buf_ref.at[slot2], sem.at[slot2], r0)
        gate = gate_ref[pl.ds(r0, rows), :]
        gates = [jnp.broadcast_to(gate[:, k:k + 1], (rows, LANES)) for k in range(TOP_K)]
        for s in range(TOKEN_TILE):
            out = x_ref[pl.ds(r0, rows), s * LANES:(s + 1) * LANES]
            for k in range(TOP_K):
                out = out + gates[k] * _load_token_tiles(buf, k * COMBINE_TOKENS + r0, rows, s)
            o_ref[pl.ds(r0, rows), s * LANES:(s + 1) * LANES] = out
        return z

    lax.fori_loop(0, COMBINE_TOKENS // rows, sum_pass, 0)

    @pl.when(i == pl.num_programs(0) - 1)
    def _():
        wait_rows(buf_ref.at[slot1], sem.at[slot1])
        wait_rows(buf_ref.at[slot2], sem.at[slot2])


def _combine(dest, x2, gate, ys):
    n = x2.shape[0]
    steps = n // COMBINE_TOKENS
    per = COMBINE_TOKENS * TOP_K
    dest3 = dest.reshape(steps, 1, per)
    return pl.pallas_call(
        _combine_kernel,
        grid=(steps,),
        in_specs=[pl.BlockSpec((1, 1, per), lambda i: (i, 0, 0), memory_space=pltpu.SMEM),
                  pl.BlockSpec((1, 1, per), lambda i: (min(1, steps - 1), 0, 0), memory_space=pltpu.SMEM),
                  pl.BlockSpec((1, 1, per), lambda i: (jnp.minimum(i + 2, steps - 1), 0, 0), memory_space=pltpu.SMEM),
                  pl.BlockSpec((COMBINE_TOKENS, D_MODEL), lambda i: (i, 0)),
                  pl.BlockSpec((COMBINE_TOKENS, TOP_K), lambda i: (i, 0)),
                  pl.BlockSpec(memory_space=pl.ANY)],
        out_specs=pl.BlockSpec((COMBINE_TOKENS, D_MODEL), lambda i: (i, 0)),
        out_shape=jax.ShapeDtypeStruct((n, D_MODEL), F32),
        scratch_shapes=[pltpu.VMEM((3, TOP_K * COMBINE_TOKENS * TOKEN_TILE, LANES), F32),
                        pltpu.SemaphoreType.DMA((3,))],
        compiler_params=_cparams(("arbitrary",)),
        name="moe_combine",
    )(dest3, dest3, dest3, x2, gate, ys)


def _moe(x2, h2, top_idx, gate, rank, counts, w_e_in, b_e_in, w_e_out, b_e_out):
    n = x2.shape[0]
    counts = counts[0, :N_EXPERTS].astype(I32)
    padded = (counts + MOE_BLOCK - 1) // MOE_BLOCK * MOE_BLOCK
    pend = jnp.cumsum(padded)
    pstart = pend - padded
    nblk = -(-(n * TOP_K) // MOE_BLOCK) + N_EXPERTS
    blk_start = jnp.arange(nblk, dtype=I32) * MOE_BLOCK
    blk_e = jnp.minimum(jnp.sum((pend[None, :] <= blk_start[:, None]).astype(I32), axis=1), N_EXPERTS - 1)
    n_used = (pend[-1:] // MOE_BLOCK).astype(I32)
    onehot = top_idx[:, :, None] == jnp.arange(N_EXPERTS, dtype=I32)[None, None, :]
    dest = rank + jnp.sum(jnp.where(onehot, pstart[None, None, :], 0), axis=-1)
    n_slots = nblk * MOE_BLOCK
    pad_len = padded - counts
    pad_end = jnp.cumsum(pad_len)
    base = jnp.concatenate([pstart + counts - (pad_end - pad_len), pend[-1:] - pad_end[-1:]])
    j = jnp.arange(n_slots - n * TOP_K, dtype=I32)
    group = jnp.sum((pad_end[None, :] <= j[:, None]).astype(I32), axis=1)
    group_hot = group[:, None] == jnp.arange(N_EXPERTS + 1, dtype=I32)[None, :]
    pad_slots = j + jnp.sum(jnp.where(group_hot, base[None, :], 0), axis=1)
    xs = _dispatch(dest, pad_slots, h2, n_slots)
    ys = _experts(blk_e, n_used, xs, w_e_in, b_e_in, w_e_out, b_e_out)
    return _combine(dest, x2, gate, ys)


def _layer(x, mem, g_mix, w_in, g_q_dsa, g_k_dsa, g_q_mem, g_k_mem, g_mem, w_mem_kv, w_gate, b_gate,
           w_branch, w_out, g_ffn, w_router, b_router, w_e_in, b_e_in, w_e_out, b_e_out):
    batch, seq, _ = x.shape
    n = batch * seq
    topk = min(DSA_TOPK_MAX, seq // 4)
    x2d = x.reshape(n, D_MODEL)
    tm_dense = min(DENSE_ROWS, seq)
    sq, sk, sv, dq, iq, kvi, iw, mq, vt = _inproj(x2d, batch, seq, g_mix, w_in, g_q_dsa, g_k_dsa, g_q_mem, tm_dense)
    o_sb = _sb_attention(sq, sk, sv, min(256, seq))
    o_dsa = _dsa_attention(dq, iq, iw, kvi, vt, 128, topk)
    mlen = mem.shape[1]
    mk, mv = _mem_kv(mem.reshape(batch * mlen, D_MODEL), g_mem, w_mem_kv, g_k_mem, min(512, batch * mlen))
    mw = MEM_HEADS * MEM_HEAD_DIM
    o_mem = _mem_attention(mq.reshape(batch, seq, mw), mk.reshape(batch, mlen, mw), mv.reshape(batch, mlen, mw), seq)
    x2, h2, top_idx, gate, rank, counts = _merge_route(
        x2d, g_mix, o_sb.reshape(n, BRANCH_W), o_dsa.reshape(n, BRANCH_W), o_mem.reshape(n, mw),
        w_gate, b_gate, w_branch, w_out, g_ffn, w_router, b_router, tm_dense)
    out = _moe(x2, h2, top_idx, gate, rank, counts, w_e_in, b_e_in, w_e_out, b_e_out)
    return out.reshape(batch, seq, D_MODEL)


def kernel(x, mem, g_mix, w_in, g_q_dsa, g_k_dsa, g_q_mem, g_k_mem, g_mem, w_mem_kv, w_gate, b_gate, w_branch, w_out, g_ffn, w_router, b_router, w_e_in, b_e_in, w_e_out, b_e_out):
    for l in range(g_mix.shape[0]):
        x = _layer(x, mem, g_mix[l], w_in[l], g_q_dsa[l], g_k_dsa[l], g_q_mem[l], g_k_mem[l], g_mem[l],
                   w_mem_kv[l], w_gate[l], b_gate[l], w_branch[l], w_out[l], g_ffn[l], w_router[l],
                   b_router[l], w_e_in[l], b_e_in[l], w_e_out[l], b_e_out[l])
    return x
```

```python
import functools

import numpy as np
import jax
import jax.numpy as jnp
from jax import lax
from jax.experimental import pallas as pl
from jax.experimental.pallas import tpu as pltpu

F32 = jnp.float32
BF16 = jnp.bfloat16
I32 = jnp.int32

D_MODEL = 1024
CHUNK = 64
SB_HEADS = 8
DSA_HEADS = 8
HEAD_DIM = 64
IDX_HEADS = 4
DSA_TOPK_MAX = 256
MEM_HEADS = 4
MEM_HEAD_DIM = 128
N_EXPERTS = 32
TOP_K = 4
D_EXPERT = D_MODEL
SWIGLU_LIMIT = 7.0
SWIGLU_ALPHA = 1.702
ROPE_THETA = 10000.0
EPS = 1e-6
MOE_BLOCK = 512

BRANCH_W = 512
IN_SIZES = (512, 512, 512, 512, 64, 64, 256, 64, 4, 512)
C_SQ, C_SK, C_DQ, C_MQ, C_IQ, C_SMALL, C_END = 0, 512, 1024, 1536, 2048, 2304, 2560
IW_LANE = 64

LANES = 128
NEG_BIG = -1e30
SB_CUTOFF = 110.0
KEY_NEG_INF = int(np.array(-np.inf, np.float32).view(np.int32)) ^ 0x7FFFFFFF
INT_MIN = -(2 ** 31)

VMEM_LIMIT = 56 * 1024 * 1024
DENSE_ROWS = 1024


def _cparams(sem):
    return pltpu.CompilerParams(dimension_semantics=sem, vmem_limit_bytes=VMEM_LIMIT)


def _dot(a, b):
    return jnp.dot(a, b, preferred_element_type=F32)


def _dot_nt(a, b):
    return lax.dot_general(a, b, (((1,), (1,)), ((), ())), preferred_element_type=F32)


def _split_bf16(x):
    hi = x.astype(BF16)
    lo = (x - hi.astype(F32)).astype(BF16)
    return hi, lo


def _dot_split(x, m_bf16):
    hi, lo = _split_bf16(x)
    return _dot(hi, m_bf16) + _dot(lo, m_bf16)


TOKEN_TILE = D_MODEL // LANES


def _store_token_tiles(ref, y):
    rows = y.shape[0]
    for s in range(TOKEN_TILE):
        ref[pl.ds(s, rows, stride=TOKEN_TILE), :] = y[:, s * LANES:(s + 1) * LANES]


def _load_token_tiles(ref, start_row, rows, s):
    return ref[pl.ds(start_row * TOKEN_TILE + s, rows, stride=TOKEN_TILE), :]


def _rot_half_unsigned(y):
    w = y.shape[1]
    lane = lax.broadcasted_iota(I32, y.shape, 1)
    return jnp.where((lane & 32) == 0, pltpu.roll(y, w - 32, 1), pltpu.roll(y, 32, 1))


def _inproj_kernel(x_ref, g_ref, w_ref, wsvt_ref, cos_ref, sin_ref, coss_ref, sins_ref, gq_ref, gks_ref,
                   gm_ref, bd64_ref, bd128_ref,
                   sq_ref, sk_ref, sv_ref, dq_ref, iq_ref, kvi_ref, iw_ref, mq_ref, vt_ref):
    x = x_ref[...]
    ms = jnp.mean(x * x, axis=-1, keepdims=True)
    h = (x * lax.rsqrt(ms + EPS) * g_ref[...]).astype(BF16)

    def seg(a, b):
        return _dot(h, w_ref[:, a:b])

    def put_heads(ref, y):
        for hd in range(y.shape[1] // HEAD_DIM):
            ref[0, hd] = y[:, hd * HEAD_DIM:(hd + 1) * HEAD_DIM].astype(BF16)

    put_heads(sq_ref, seg(C_SQ, C_SK) * (HEAD_DIM ** -0.5))
    put_heads(sk_ref, seg(C_SK, C_DQ))
    sv_ref[0] = _dot_nt(wsvt_ref[...], h).astype(BF16)

    y = seg(C_DQ, C_MQ)
    msq = _dot_split(y * y, bd64_ref[...]) * (1.0 / HEAD_DIM)
    y = y * lax.rsqrt(msq + EPS) * gq_ref[...]
    y = y * cos_ref[...] + _rot_half_unsigned(y) * sin_ref[...]
    put_heads(dq_ref, y * (HEAD_DIM ** -0.5))

    y = seg(C_IQ, C_SMALL)
    y = y * cos_ref[:, :256] + _rot_half_unsigned(y) * sin_ref[:, :256]
    iq_ref[...] = (y * (HEAD_DIM ** -0.5)).astype(BF16)

    y = seg(C_SMALL, C_END)
    lane = lax.broadcasted_iota(I32, y.shape, 1)
    is_k = lane < HEAD_DIM
    msk = jnp.sum(jnp.where(is_k, y * y, 0.0), axis=-1, keepdims=True) * (1.0 / HEAD_DIM)
    y = y * jnp.where(is_k, lax.rsqrt(msk + EPS) * gks_ref[...], 1.0)
    y = y * coss_ref[...] + _rot_half_unsigned(y) * sins_ref[...]
    kvi_ref[...] = y.astype(BF16)
    iw_ref[...] = y[:, 128:256]
    kv_t = y[:, 0:128].T
    row_t = lax.broadcasted_iota(I32, kv_t.shape, 0)
    vt_ref[0] = jnp.where(row_t < HEAD_DIM, 1.0, kv_t).astype(BF16)

    y = seg(C_MQ, C_IQ)
    msm = _dot_split(y * y, bd128_ref[...]) * (1.0 / MEM_HEAD_DIM)
    mq_ref[...] = (y * lax.rsqrt(msm + EPS) * gm_ref[...]).astype(BF16)


def _rope_tables(seq):
    half = HEAD_DIM // 2
    inv = ROPE_THETA ** (-jnp.arange(half, dtype=F32) / half)
    ang = jnp.arange(seq).astype(F32)[:, None] * inv[None, :]
    cos = jnp.cos(ang)
    sin = jnp.sin(ang)
    cos64 = jnp.concatenate([cos, cos], axis=1)
    sin64 = jnp.concatenate([-sin, sin], axis=1)
    one = jnp.ones_like(cos64)
    zero = jnp.zeros_like(cos64)
    cosq = jnp.tile(cos64, (1, 8))
    sinq = jnp.tile(sin64, (1, 8))
    coss = jnp.concatenate([cos64, one, cos64, one], axis=1)
    sins = jnp.concatenate([sin64, zero, sin64, zero], axis=1)
    return cosq, sinq, coss, sins


def _block_diag_ones(width, group):
    idx = np.arange(width) // group
    return jnp.asarray((idx[:, None] == idx[None, :]).astype(np.float32), dtype=BF16)


def _inproj(x2d, batch, seq, g_mix, w_in, g_q_dsa, g_k_dsa, g_q_mem, tm):
    n = x2d.shape[0]
    sizes = np.cumsum((0,) + IN_SIZES)
    col = {name: (int(sizes[i]), int(sizes[i + 1])) for i, name in enumerate(
        ("sq", "sk", "sv", "dq", "dk", "dv", "iq", "ik", "iw", "mq"))}
    order = ("sq", "sk", "dq", "mq", "iq", "dk", "dv", "ik", "iw")
    width = sum(col[k][1] - col[k][0] for k in order)
    w = jnp.concatenate([w_in[:, col[k][0]:col[k][1]] for k in order]
                        + [jnp.zeros((D_MODEL, C_END - width), w_in.dtype)], axis=1).astype(BF16)
    cosq, sinq, coss, sins = (jnp.asarray(t) for t in _rope_tables(seq))
    gq = jnp.tile(g_q_dsa, 8)[None, :]
    gks = jnp.concatenate([g_k_dsa, jnp.ones((256 - HEAD_DIM,), F32)])[None, :]
    gm = jnp.tile(g_q_mem, MEM_HEADS)[None, :]
    spb = seq // tm
    row = lambda i: (i, 0)
    const = lambda i: (0, 0)
    pos = lambda i: (i % spb, 0)
    heads = lambda i: (i // spb, 0, i % spb, 0)
    head_shape = jax.ShapeDtypeStruct((batch, 8, seq, HEAD_DIM), BF16)
    head_spec = pl.BlockSpec((1, 8, tm, HEAD_DIM), heads)
    return pl.pallas_call(
        _inproj_kernel,
        grid=(n // tm,),
        in_specs=[
            pl.BlockSpec((tm, D_MODEL), row),
            pl.BlockSpec((1, D_MODEL), const),
            pl.BlockSpec((D_MODEL, C_END), const, pipeline_mode=pl.Buffered(1)),
            pl.BlockSpec((512, D_MODEL), const, pipeline_mode=pl.Buffered(1)),
            pl.BlockSpec((tm, 512), pos), pl.BlockSpec((tm, 512), pos),
            pl.BlockSpec((tm, 256), pos), pl.BlockSpec((tm, 256), pos),
            pl.BlockSpec((1, 512), const), pl.BlockSpec((1, 256), const), pl.BlockSpec((1, 512), const),
            pl.BlockSpec((512, 512), const), pl.BlockSpec((512, 512), const),
        ],
        out_specs=[head_spec, head_spec, pl.BlockSpec((1, 512, tm), lambda i: (i // spb, 0, i % spb)), head_spec,
                   pl.BlockSpec((tm, 256), row), pl.BlockSpec((tm, 256), row),
                   pl.BlockSpec((tm, 128), row), pl.BlockSpec((tm, 512), row),
                   pl.BlockSpec((1, 128, tm), lambda i: (i // spb, 0, i % spb))],
        out_shape=[head_shape, head_shape, jax.ShapeDtypeStruct((batch, 512, seq), BF16), head_shape,
                   jax.ShapeDtypeStruct((n, 256), BF16), jax.ShapeDtypeStruct((n, 256), BF16),
                   jax.ShapeDtypeStruct((n, 128), F32), jax.ShapeDtypeStruct((n, 512), BF16),
                   jax.ShapeDtypeStruct((batch, 128, seq), BF16)],
        compiler_params=_cparams(("parallel",)),
        name="inproj",
    )(x2d, g_mix[None, :], w, w_in[:, col["sv"][0]:col["sv"][1]].T.astype(BF16), cosq, sinq, coss, sins, gq, gks, gm,
      _block_diag_ones(512, HEAD_DIM), _block_diag_ones(512, MEM_HEAD_DIM))


def _sb_kernel(q_ref, k_ref, vt_ref, u_ref, o_ref, acc_ref, car_ref, *, tq, blocks_per_step):
    def query_block(r, carry):
        rows = pl.ds(pl.multiple_of(r * tq, tq), tq)
        _sb_sweep(q_ref.at[:, :, rows, :], k_ref, vt_ref, u_ref, o_ref.at[:, rows, :], acc_ref, car_ref,
                  qi=pl.program_id(1) * blocks_per_step + r, tq=tq)
        return carry

    lax.fori_loop(0, blocks_per_step, query_block, 0)


def _sb_sweep(q_ref, k_ref, vt_ref, u_ref, o_ref, acc_ref, car_ref, *, qi, tq):
    rows = lax.broadcasted_iota(I32, (tq, tq), 0)
    cols = lax.broadcasted_iota(I32, (tq, tq), 1)
    dif = rows - cols
    u = u_ref[...]
    acc_ref[...] = jnp.zeros_like(acc_ref)
    car_ref[...] = jnp.zeros_like(car_ref)

    def cond(c):
        kb, mx = c
        return jnp.logical_and(kb >= 0, mx > -SB_CUTOFF)

    def body(c):
        kb, _ = c
        ks = pl.multiple_of(kb * tq, tq)
        earlier = dif < (qi - kb) * tq
        neg_mask = jnp.where(earlier, -1.0, 0.0).astype(BF16)
        heads = range(SB_HEADS)
        z = [_dot_nt(k_ref[0, hd, pl.ds(ks, tq), :], q_ref[0, hd]) for hd in heads]
        ls, lk, between = [], [], []
        for hd in heads:
            zb = z[hd].astype(BF16)
            sp = jnp.maximum(zb, 0.0) + jnp.log(1.0 + jnp.exp(-jnp.abs(zb)))
            ls.append(z[hd] - sp.astype(F32))
            lk.append(sp * neg_mask)
            between.append(_dot(u, lk[hd]))
        for hd in heads:
            rs = slice(hd * HEAD_DIM, (hd + 1) * HEAD_DIM)
            car = car_ref[hd:hd + 1, :]
            w = jnp.where(earlier, jnp.exp(ls[hd] + between[hd] + car), 0.0)
            acc_ref[rs, :] += _dot(vt_ref[0, rs, pl.ds(ks, tq)], w.astype(BF16))
            car_ref[hd:hd + 1, :] = car + (between[hd][0:1, :] + lk[hd][0:1, :].astype(F32))
        return kb - 1, jnp.max(car_ref[...])

    lax.while_loop(cond, body, (qi, jnp.float32(0.0)))
    o_ref[0] = acc_ref[...].T.astype(BF16)


def _strict_tri(n, lower):
    i = np.arange(n)
    m = (i[:, None] > i[None, :]) if lower else (i[:, None] < i[None, :])
    return jnp.asarray(m.astype(np.float32), dtype=BF16)


SB_BLOCKS_PER_STEP = 4


def _sb_attention(sq, sk, svt, tq):
    batch, _, seq, _ = sq.shape
    per_step = min(SB_BLOCKS_PER_STEP, seq // tq)
    return pl.pallas_call(
        functools.partial(_sb_kernel, tq=tq, blocks_per_step=per_step),
        grid=(batch, seq // (tq * per_step)),
        in_specs=[pl.BlockSpec((1, SB_HEADS, tq * per_step, HEAD_DIM), lambda b, i: (b, 0, i, 0)),
                  pl.BlockSpec((1, SB_HEADS, seq, HEAD_DIM), lambda b, i: (b, 0, 0, 0)),
                  pl.BlockSpec((1, BRANCH_W, seq), lambda b, i: (b, 0, 0)),
                  pl.BlockSpec((tq, tq), lambda b, i: (0, 0))],
        out_specs=pl.BlockSpec((1, tq * per_step, BRANCH_W), lambda b, i: (b, i, 0)),
        out_shape=jax.ShapeDtypeStruct((batch, seq, BRANCH_W), BF16),
        scratch_shapes=[pltpu.VMEM((BRANCH_W, tq), F32), pltpu.VMEM((SB_HEADS, tq), F32)],
        compiler_params=_cparams(("parallel", "parallel")),
        name="sb_attention",
    )(sq, sk, svt, _strict_tri(tq, lower=False))


DSA_SEG = 256
DSA_KB = DSA_SEG


def _tree_sum(parts):
    while len(parts) > 1:
        parts = [parts[i] + parts[i + 1] for i in range(0, len(parts) - 1, 2)] + ([parts[-1]] if len(parts) % 2 else [])
    return parts[0]


def _dsa_kernel(dq_ref, iq_ref, iw_ref, kvi_ref, vt_ref, tri_ref, o_ref,
                sc_ref, bias_ref, s_ref, *, tq, topk, nseg_max):
    nseg = pl.program_id(1) + 1

    def query_block(r, carry, ns):
        rows = pl.ds(pl.multiple_of(r * tq, tq), tq)
        _dsa_body(dq_ref.at[:, :, rows, :], iq_ref.at[:, rows, :], iw_ref.at[:, rows, :], kvi_ref, vt_ref, tri_ref,
                  o_ref.at[:, rows, :], sc_ref, bias_ref, s_ref,
                  qs=pl.program_id(1) * DSA_SEG + r * tq, tq=tq, topk=topk, nseg=ns)
        return carry

    for ns in range(1, nseg_max + 1):
        @pl.when(nseg == ns)
        def _(ns=ns):
            lax.fori_loop(0, DSA_SEG // tq, functools.partial(query_block, ns=ns), 0)


def _dsa_body(dq_ref, iq_ref, iw_ref, kvi_ref, vt_ref, tri_ref, o_ref, sc_ref, bias_ref, s_ref,
              *, qs, tq, topk, nseg):
    blocks = [slice(c * DSA_KB, (c + 1) * DSA_KB) for c in range(nseg)]
    iq = iq_ref[0]
    w_t = iw_ref[0].T
    w_row = [w_t[IW_LANE + h:IW_LANE + h + 1, :] * (IDX_HEADS ** -0.5) for h in range(IDX_HEADS)]
    q_chunk = (qs + lax.broadcasted_iota(I32, (DSA_KB, tq), 1)) // CHUNK
    k_chunk = lax.broadcasted_iota(I32, (DSA_KB, tq), 0) // CHUNK

    for c, blk in enumerate(blocks):
        ik = kvi_ref[0, blk, 128:192]
        lg = [_dot_nt(ik, iq[:, h * HEAD_DIM:(h + 1) * HEAD_DIM]) for h in range(IDX_HEADS)]
        sc = jnp.zeros((DSA_KB, tq), F32)
        for h in range(IDX_HEADS):
            sc = sc + w_row[h] * jnp.maximum(lg[h], 0.0)
        admissible = (c * (DSA_KB // CHUNK) + k_chunk) <= q_chunk
        sc_ref[blk, :] = jnp.where(admissible, sc, -jnp.inf)

    def count(pred_fn):
        sub, n_sums = 32, 4
        acc = [jnp.zeros((sub, tq), F32)] * n_sums
        for j in range(nseg * DSA_SEG // sub):
            acc[j % n_sums] = acc[j % n_sums] + jnp.where(pred_fn(sc_ref[j * sub:(j + 1) * sub, :]), 1.0, 0.0)
        return jnp.sum(_tree_sum(acc), axis=0, keepdims=True)

    def key_to_float(key):
        return lax.bitcast_convert_type(jnp.where(key >= 0, key, key ^ 0x7FFFFFFF), F32)

    kf = jnp.float32(topk)
    n_rows = jnp.float32(nseg * DSA_SEG)
    cnt0 = count(lambda s: s >= 0.0)
    t0 = jnp.where(cnt0 >= kf, 0, INT_MIN).astype(I32)
    cnt_t0 = jnp.where(cnt0 >= kf, cnt0, n_rows)

    def bit_step(i, carry):
        t, cnt_t = carry
        cand = t + lax.shift_left(jnp.int32(1), 30 - i)
        cand_f = key_to_float(cand)
        cnt = jnp.where(cand <= KEY_NEG_INF, n_rows, count(lambda s: s >= cand_f))
        take = cnt >= kf
        return jnp.where(take, cand, t), jnp.where(take, cnt, cnt_t)

    thr_key, cnt_thr = lax.fori_loop(0, 31, bit_step, (t0, cnt_t0))
    thr = jnp.where(thr_key <= KEY_NEG_INF, -jnp.inf, key_to_float(thr_key))
    tri = tri_ref[...]

    surplus = jnp.logical_or(jnp.max(cnt_thr) > kf, jnp.min(thr) == -jnp.inf)

    @pl.when(surplus)
    def _():
        need = kf - count(lambda s: s > thr)
        prefix = jnp.zeros((1, tq), F32)
        for blk in blocks:
            sc = sc_ref[blk, :]
            eqf = jnp.where(sc == thr, 1.0, 0.0)
            rank = _dot(tri, eqf.astype(BF16)) + prefix
            tie = jnp.where(rank < need, eqf, 0.0)
            sel = jnp.where(sc > thr, 1.0, tie)
            bias_ref[blk, :] = jnp.where(sc > -jnp.inf, (sel - 1.0) * (-NEG_BIG), NEG_BIG)
            prefix = prefix + jnp.sum(eqf, axis=0, keepdims=True)

    @pl.when(jnp.logical_not(surplus))
    def _():
        for blk in blocks:
            bias_ref[blk, :] = jnp.where(sc_ref[blk, :] >= thr, 0.0, NEG_BIG)

    q8 = dq_ref[0].reshape(DSA_HEADS * tq, HEAD_DIM)
    m = jnp.full((1, DSA_HEADS * tq), NEG_BIG, F32)
    for blk in blocks:
        b = bias_ref[blk, :]
        s = _dot_nt(kvi_ref[0, blk, 0:HEAD_DIM], q8) + jnp.concatenate([b] * DSA_HEADS, axis=1)
        s_ref[blk, :] = s
        m = jnp.maximum(m, jnp.max(s, axis=0, keepdims=True))
    acc = jnp.zeros((128, DSA_HEADS * tq), F32)
    for blk in blocks:
        p = jnp.exp(s_ref[blk, :] - m)
        acc = acc + _dot(vt_ref[0, :, blk], p.astype(BF16))
    for hd in range(DSA_HEADS):
        a = acc[:, hd * tq:(hd + 1) * tq]
        o = (a / a[0:1, :]).T
        o_ref[0, :, hd * HEAD_DIM:(hd + 1) * HEAD_DIM] = o[:, HEAD_DIM:].astype(BF16)


def _dsa_attention(dq, iq, iw, kvi, vt, tq, topk):
    batch, _, seq, _ = dq.shape
    return pl.pallas_call(
        functools.partial(_dsa_kernel, tq=tq, topk=topk, nseg_max=seq // DSA_SEG),
        grid=(batch, seq // DSA_SEG),
        in_specs=[
            pl.BlockSpec((1, DSA_HEADS, DSA_SEG, HEAD_DIM), lambda b, i: (b, 0, i, 0)),
            pl.BlockSpec((1, DSA_SEG, 256), lambda b, i: (b, i, 0)),
            pl.BlockSpec((1, DSA_SEG, 128), lambda b, i: (b, i, 0)),
            pl.BlockSpec((1, seq, 256), lambda b, i: (b, 0, 0)),
            pl.BlockSpec((1, 128, seq), lambda b, i: (b, 0, 0)),
            pl.BlockSpec((DSA_KB, DSA_KB), lambda b, i: (0, 0)),
        ],
        out_specs=pl.BlockSpec((1, DSA_SEG, BRANCH_W), lambda b, i: (b, i, 0)),
        out_shape=jax.ShapeDtypeStruct((batch, seq, BRANCH_W), BF16),
        scratch_shapes=[pltpu.VMEM((seq, tq), F32), pltpu.VMEM((seq, tq), F32),
                        pltpu.VMEM((seq, DSA_HEADS * tq), F32)],
        compiler_params=_cparams(("parallel", "parallel")),
        name="dsa_attention",
    )(dq, iq.reshape(batch, seq, 256), iw.reshape(batch, seq, 128), kvi.reshape(batch, seq, 256), vt,
      _strict_tri(DSA_KB, lower=True))


def _memkv_kernel(m_ref, g_ref, w_ref, gk_ref, bd_ref, mk_ref, mv_ref):
    x = m_ref[...]
    ms = jnp.mean(x * x, axis=-1, keepdims=True)
    h = (x * lax.rsqrt(ms + EPS) * g_ref[...]).astype(BF16)
    mw = MEM_HEADS * MEM_HEAD_DIM
    k = _dot(h, w_ref[:, :mw])
    msk = _dot_split(k * k, bd_ref[...]) * (1.0 / MEM_HEAD_DIM)
    mk_ref[...] = (k * lax.rsqrt(msk + EPS) * gk_ref[...]).astype(BF16)
    mv_ref[...] = _dot(h, w_ref[:, mw:]).astype(BF16)


def _mem_kv(mem2d, g_mem, w_mem_kv, g_k_mem, tm):
    n = mem2d.shape[0]
    mw = MEM_HEADS * MEM_HEAD_DIM
    row = lambda i: (i, 0)
    const = lambda i: (0, 0)
    return pl.pallas_call(
        _memkv_kernel,
        grid=(n // tm,),
        in_specs=[pl.BlockSpec((tm, D_MODEL), row), pl.BlockSpec((1, D_MODEL), const),
                  pl.BlockSpec((D_MODEL, 2 * mw), const), pl.BlockSpec((1, mw), const),
                  pl.BlockSpec((mw, mw), const)],
        out_specs=[pl.BlockSpec((tm, mw), row), pl.BlockSpec((tm, mw), row)],
        out_shape=[jax.ShapeDtypeStruct((n, mw), BF16), jax.ShapeDtypeStruct((n, mw), BF16)],
        compiler_params=_cparams(("parallel",)),
        name="mem_kv",
    )(mem2d, g_mem[None, :], w_mem_kv.astype(BF16), jnp.tile(g_k_mem, MEM_HEADS)[None, :],
      _block_diag_ones(mw, MEM_HEAD_DIM))


def _memattn_kernel(q_ref, k_ref, v_ref, o_ref):
    for hd in range(MEM_HEADS):
        sl = slice(hd * MEM_HEAD_DIM, (hd + 1) * MEM_HEAD_DIM)
        s = _dot_nt(q_ref[0, :, sl], k_ref[0, :, sl]) * (MEM_HEAD_DIM ** -0.5)
        p = jnp.exp(s - jnp.max(s, axis=-1, keepdims=True))
        o = _dot(p.astype(BF16), v_ref[0, :, sl]) / jnp.sum(p, axis=-1, keepdims=True)
        o_ref[0, :, sl] = o.astype(BF16)


def _mem_attention(mq, mk, mv, tq):
    batch, seq, mw = mq.shape
    mlen = mk.shape[1]
    kspec = pl.BlockSpec((1, mlen, mw), lambda b, i: (b, 0, 0))
    return pl.pallas_call(
        _memattn_kernel,
        grid=(batch, seq // tq),
        in_specs=[pl.BlockSpec((1, tq, mw), lambda b, i: (b, i, 0)), kspec, kspec],
        out_specs=pl.BlockSpec((1, tq, mw), lambda b, i: (b, i, 0)),
        out_shape=jax.ShapeDtypeStruct((batch, seq, mw), BF16),
        compiler_params=_cparams(("parallel", "parallel")),
        name="mem_attention",
    )(mq, mk, mv)


def _merge_kernel(x_ref, g_ref, osb_ref, odsa_ref, omem_ref, wg_ref, bg_ref, wb_ref, wo_ref,
                  gf_ref, wrh_ref, wrl_ref, br_ref, tri_ref,
                  x2_ref, h2_ref, idx_ref, gate_ref, rank_ref, cnt_ref, carry_ref):
    x = x_ref[...]
    ms = jnp.mean(x * x, axis=-1, keepdims=True)
    h = (x * lax.rsqrt(ms + EPS) * g_ref[...]).astype(BF16)
    merged = None
    for n, o_ref in enumerate((osb_ref, odsa_ref, omem_ref)):
        gate = jax.nn.sigmoid(_dot(h, wg_ref[n]) + bg_ref[n])
        term = gate * _dot(o_ref[...], wb_ref[n])
        merged = term if merged is None else merged + term
    x2 = x + _dot(merged.astype(BF16), wo_ref[...])
    x2_ref[...] = x2

    ms2 = jnp.mean(x2 * x2, axis=-1, keepdims=True)
    h2 = x2 * lax.rsqrt(ms2 + EPS) * gf_ref[...]
    _store_token_tiles(h2_ref, h2)
    hi, lo = _split_bf16(h2)
    logits = _dot(hi, wrh_ref[...]) + _dot(hi, wrl_ref[...]) + _dot(lo, wrh_ref[...]) + br_ref[...]

    lane = lax.broadcasted_iota(I32, logits.shape, 1)
    vals = logits
    top_v, top_i = [], []
    for _ in range(TOP_K):
        mx = jnp.max(vals, axis=-1, keepdims=True)
        ix = jnp.min(jnp.where(vals == mx, lane, N_EXPERTS), axis=-1, keepdims=True)
        top_v.append(mx)
        top_i.append(ix)
        vals = jnp.where(lane == ix, -jnp.inf, vals)
    ex = [jnp.exp(v - top_v[0]) for v in top_v]
    den = ex[0] + ex[1] + ex[2] + ex[3]
    for k in range(TOP_K):
        idx_ref[:, k:k + 1] = top_i[k]
        gate_ref[:, k:k + 1] = ex[k] / den

    @pl.when(pl.program_id(0) == 0)
    def _():
        carry_ref[...] = jnp.zeros_like(carry_ref)

    lane_e = lax.broadcasted_iota(I32, (x.shape[0], LANES), 1)
    hits = [lane_e == top_i[k] for k in range(TOP_K)]
    onehot = sum(jnp.where(hk, 1.0, 0.0) for hk in hits)
    before = _dot(tri_ref[...], onehot.astype(BF16)) + carry_ref[...]
    for k in range(TOP_K):
        rank_ref[:, k:k + 1] = jnp.sum(jnp.where(hits[k], before, 0.0), axis=-1, keepdims=True).astype(I32)
    carry_ref[...] += jnp.sum(onehot, axis=0, keepdims=True)
    cnt_ref[...] = carry_ref[...]


def _merge_route(x2d, g_mix, o_sb, o_dsa, o_mem, w_gate, b_gate, w_branch, w_out, g_ffn, w_router, b_router, tm):
    n = x2d.shape[0]
    row = lambda i: (i, 0)
    c2 = lambda i: (0, 0)
    c3 = lambda i: (0, 0, 0)
    wr_hi = w_router.astype(BF16)
    wr_lo = (w_router - wr_hi.astype(F32)).astype(BF16)
    return pl.pallas_call(
        _merge_kernel,
        grid=(n // tm,),
        in_specs=[pl.BlockSpec((tm, D_MODEL), row), pl.BlockSpec((1, D_MODEL), c2),
                  pl.BlockSpec((tm, BRANCH_W), row), pl.BlockSpec((tm, BRANCH_W), row),
                  pl.BlockSpec((tm, BRANCH_W), row),
                  pl.BlockSpec((3, D_MODEL, D_MODEL), c3, pipeline_mode=pl.Buffered(1)),
                  pl.BlockSpec((3, 1, D_MODEL), c3),
                  pl.BlockSpec((3, BRANCH_W, D_MODEL), c3, pipeline_mode=pl.Buffered(1)),
                  pl.BlockSpec((D_MODEL, D_MODEL), c2, pipeline_mode=pl.Buffered(1)),
                  pl.BlockSpec((1, D_MODEL), c2), pl.BlockSpec((D_MODEL, N_EXPERTS), c2),
                  pl.BlockSpec((D_MODEL, N_EXPERTS), c2), pl.BlockSpec((1, N_EXPERTS), c2),
                  pl.BlockSpec((tm, tm), c2, pipeline_mode=pl.Buffered(1))],
        out_specs=[pl.BlockSpec((tm, D_MODEL), row), pl.BlockSpec((tm * TOKEN_TILE, LANES), row),
                   pl.BlockSpec((tm, TOP_K), row), pl.BlockSpec((tm, TOP_K), row),
                   pl.BlockSpec((tm, TOP_K), row), pl.BlockSpec((1, LANES), c2)],
        out_shape=[jax.ShapeDtypeStruct((n, D_MODEL), F32), jax.ShapeDtypeStruct((n * TOKEN_TILE, LANES), F32),
                   jax.ShapeDtypeStruct((n, TOP_K), I32), jax.ShapeDtypeStruct((n, TOP_K), F32),
                   jax.ShapeDtypeStruct((n, TOP_K), I32), jax.ShapeDtypeStruct((1, LANES), F32)],
        scratch_shapes=[pltpu.VMEM((1, LANES), F32)],
        compiler_params=_cparams(("arbitrary",)),
        name="merge_route",
    )(x2d, g_mix[None, :], o_sb, o_dsa, o_mem, w_gate.astype(BF16), b_gate[:, None, :],
      w_branch.astype(BF16), w_out.astype(BF16), g_ffn[None, :], wr_hi, wr_lo, b_router[None, :],
      _strict_tri(tm, lower=True))


DISPATCH_TOKENS = 4096
COMBINE_TOKENS = 512


def _dispatch_kernel(dest_ref, pad_ref, h_ref, xs_ref, zero_ref, sem, *, n_tok, n_pad):
    n_copies = n_tok * TOP_K + n_pad
    zero_ref[...] = jnp.zeros_like(zero_ref)

    def fill(j, c):
        for r in range(2):
            dst = pl.multiple_of(pad_ref[0, 0, 2 * j + r] * TOKEN_TILE, TOKEN_TILE)
            pltpu.make_async_copy(zero_ref, xs_ref.at[pl.ds(dst, TOKEN_TILE)], sem).start(priority=r)
        return c

    lax.fori_loop(0, n_pad // 2, fill, 0, unroll=2)

    def start(t, c):
        src = pl.multiple_of(t * TOKEN_TILE, TOKEN_TILE)
        for k in range(TOP_K):
            dst = pl.multiple_of(dest_ref[0, 0, t * TOP_K + k] * TOKEN_TILE, TOKEN_TILE)
            pltpu.make_async_copy(h_ref.at[pl.ds(src, TOKEN_TILE)], xs_ref.at[pl.ds(dst, TOKEN_TILE)],
                                  sem).start(priority=k % 2)
        return c

    lax.fori_loop(0, n_tok, start, 0, unroll=2)
    total = n_copies * TOKEN_TILE
    pltpu.make_async_copy(xs_ref.at[pl.ds(0, total)], xs_ref.at[pl.ds(0, total)], sem).wait()


def _dispatch(dest, pad_slots, h2t, n_slots):
    n = h2t.shape[0] // TOKEN_TILE
    n_tok = min(DISPATCH_TOKENS, n)
    steps = n // n_tok
    per = n_tok * TOP_K
    n_pad = pad_slots.shape[0] // steps
    assert n_tok * steps == n and n_pad * steps == pad_slots.shape[0] and n_pad % 2 == 0
    return pl.pallas_call(
        functools.partial(_dispatch_kernel, n_tok=n_tok, n_pad=n_pad),
        grid=(steps,),
        in_specs=[pl.BlockSpec((1, 1, per), lambda i: (i, 0, 0), memory_space=pltpu.SMEM),
                  pl.BlockSpec((1, 1, n_pad), lambda i: (i, 0, 0), memory_space=pltpu.SMEM),
                  pl.BlockSpec((n_tok * TOKEN_TILE, LANES), lambda i: (i, 0))],
        out_specs=pl.BlockSpec(memory_space=pl.ANY),
        out_shape=jax.ShapeDtypeStruct((n_slots * TOKEN_TILE, LANES), F32),
        scratch_shapes=[pltpu.VMEM((TOKEN_TILE, LANES), F32), pltpu.SemaphoreType.DMA(())],
        compiler_params=_cparams(("arbitrary",)),
        name="moe_dispatch",
    )(dest.reshape(steps, 1, per), pad_slots.reshape(steps, 1, n_pad), h2t)


def _expert_kernel(blk_e_ref, nused_ref, x_ref, w1_ref, b1_ref, w2_ref, b2_ref, y_ref, xb_ref, w1b_ref, w2b_ref):
    i = pl.program_id(0)
    new_expert = jnp.logical_or(i == 0, blk_e_ref[i] != blk_e_ref[jnp.maximum(i - 1, 0)])

    @pl.when(jnp.logical_and(i < nused_ref[0], new_expert))
    def _():
        w1b_ref[...] = w1_ref[0].astype(BF16)
        w2b_ref[...] = w2_ref[0].astype(BF16)

    @pl.when(i < nused_ref[0])
    def _():
        for s in range(TOKEN_TILE):
            xb_ref[:, s * LANES:(s + 1) * LANES] = _load_token_tiles(x_ref, 0, MOE_BLOCK, s).astype(BF16)
        hb = _dot(xb_ref[...], w1b_ref[...]) + b1_ref[0]
        g = jnp.minimum(hb[:, :D_EXPERT], SWIGLU_LIMIT)
        u = jnp.clip(hb[:, D_EXPERT:], -SWIGLU_LIMIT, SWIGLU_LIMIT)
        act = (u + 1.0) * (g * jax.nn.sigmoid(SWIGLU_ALPHA * g))
        _store_token_tiles(y_ref, _dot(act.astype(BF16), w2b_ref[...]) + b2_ref[0])

    @pl.when(pl.program_id(0) >= nused_ref[0])
    def _():
        y_ref[...] = jnp.zeros_like(y_ref)


def _experts(blk_e, n_used, xs, w_e_in, b_e_in, w_e_out, b_e_out):
    n_slots = xs.shape[0] // TOKEN_TILE
    nblk = n_slots // MOE_BLOCK
    slot_block = pl.BlockSpec((MOE_BLOCK * TOKEN_TILE, LANES), lambda i, be, nu: (i, 0))
    grid_spec = pltpu.PrefetchScalarGridSpec(
        num_scalar_prefetch=2,
        grid=(nblk,),
        in_specs=[slot_block,
                  pl.BlockSpec((1, D_MODEL, 2 * D_EXPERT), lambda i, be, nu: (be[i], 0, 0)),
                  pl.BlockSpec((1, 1, 2 * D_EXPERT), lambda i, be, nu: (be[i], 0, 0)),
                  pl.BlockSpec((1, D_EXPERT, D_MODEL), lambda i, be, nu: (be[i], 0, 0)),
                  pl.BlockSpec((1, 1, D_MODEL), lambda i, be, nu: (be[i], 0, 0))],
        out_specs=slot_block,
        scratch_shapes=[pltpu.VMEM((MOE_BLOCK, D_MODEL), BF16), pltpu.VMEM((D_MODEL, 2 * D_EXPERT), BF16),
                        pltpu.VMEM((D_EXPERT, D_MODEL), BF16)],
    )
    return pl.pallas_call(
        _expert_kernel,
        grid_spec=grid_spec,
        out_shape=jax.ShapeDtypeStruct((n_slots * TOKEN_TILE, LANES), F32),
        compiler_params=_cparams(("arbitrary",)),
        name="moe_experts",
    )(blk_e, n_used, xs, w_e_in, b_e_in[:, None, :], w_e_out, b_e_out[:, None, :])


def _combine_kernel(dest_ref, dest_one_ref, dest_ahead_ref, x_ref, gate_ref, y_ref, o_ref, buf_ref, sem):
    n_copies = COMBINE_TOKENS * TOP_K
    rows = 32
    i = pl.program_id(0)
    slot, slot1, slot2 = i % 3, (i + 1) % 3, (i + 2) % 3
    buf = buf_ref.at[slot]

    def start_rows(idx_ref, dst_buf, dma_sem, t0):
        for t in range(rows):
            for k in range(TOP_K):
                src = pl.multiple_of(idx_ref[0, 0, (t0 + t) * TOP_K + k] * TOKEN_TILE, TOKEN_TILE)
                dst = pl.multiple_of((k * COMBINE_TOKENS + t0 + t) * TOKEN_TILE, TOKEN_TILE)
                pltpu.make_async_copy(y_ref.at[pl.ds(src, TOKEN_TILE)], dst_buf.at[pl.ds(dst, TOKEN_TILE)],
                                      dma_sem).start(priority=k % 2)

    def wait_rows(dst_buf, dma_sem):
        pltpu.make_async_copy(y_ref.at[pl.ds(0, n_copies * TOKEN_TILE)], dst_buf, dma_sem).wait()

    @pl.when(i == 0)
    def _():
        def first(c, z):
            start_rows(dest_ref, buf_ref.at[0], sem.at[0], c * rows)
            start_rows(dest_one_ref, buf_ref.at[1], sem.at[1], c * rows)
            return z
        lax.fori_loop(0, COMBINE_TOKENS // rows, first, 0)

    wait_rows(buf, sem.at[slot])

    def sum_pass(c, z):
        r0 = pl.multiple_of(c * rows, rows)
        start_rows(dest_ahead_ref, buf_ref.at[slot2], sem.at[slot2], r0)
        gate = gate_ref[pl.ds(r0, rows), :]
        gates = [jnp.broadcast_to(gate[:, k:k + 1], (rows, LANES)) for k in range(TOP_K)]
        for s in range(TOKEN_TILE):
            out = x_ref[pl.ds(r0, rows), s * LANES:(s + 1) * LANES]
            for k in range(TOP_K):
                out = out + gates[k] * _load_token_tiles(buf, k * COMBINE_TOKENS + r0, rows, s)
            o_ref[pl.ds(r0, rows), s * LANES:(s + 1) * LANES] = out
        return z

    lax.fori_loop(0, COMBINE_TOKENS // rows, sum_pass, 0)

    @pl.when(i == pl.num_programs(0) - 1)
    def _():
        wait_rows(buf_ref.at[slot1], sem.at[slot1])
        wait_rows(buf_ref.at[slot2], sem.at[slot2])


def _combine(dest, x2, gate, ys):
    n = x2.shape[0]
    steps = n // COMBINE_TOKENS
    per = COMBINE_TOKENS * TOP_K
    dest3 = dest.reshape(steps, 1, per)
    return pl.pallas_call(
        _combine_kernel,
        grid=(steps,),
        in_specs=[pl.BlockSpec((1, 1, per), lambda i: (i, 0, 0), memory_space=pltpu.SMEM),
                  pl.BlockSpec((1, 1, per), lambda i: (min(1, steps - 1), 0, 0), memory_space=pltpu.SMEM),
                  pl.BlockSpec((1, 1, per), lambda i: (jnp.minimum(i + 2, steps - 1), 0, 0), memory_space=pltpu.SMEM),
                  pl.BlockSpec((COMBINE_TOKENS, D_MODEL), lambda i: (i, 0)),
                  pl.BlockSpec((COMBINE_TOKENS, TOP_K), lambda i: (i, 0)),
                  pl.BlockSpec(memory_space=pl.ANY)],
        out_specs=pl.BlockSpec((COMBINE_TOKENS, D_MODEL), lambda i: (i, 0)),
        out_shape=jax.ShapeDtypeStruct((n, D_MODEL), F32),
        scratch_shapes=[pltpu.VMEM((3, TOP_K * COMBINE_TOKENS * TOKEN_TILE, LANES), F32),
                        pltpu.SemaphoreType.DMA((3,))],
        compiler_params=_cparams(("arbitrary",)),
        name="moe_combine",
    )(dest3, dest3, dest3, x2, gate, ys)


def _moe(x2, h2, top_idx, gate, rank, counts, w_e_in, b_e_in, w_e_out, b_e_out):
    n = x2.shape[0]
    counts = counts[0, :N_EXPERTS].astype(I32)
    padded = (counts + MOE_BLOCK - 1) // MOE_BLOCK * MOE_BLOCK
    pend = jnp.cumsum(padded)
    pstart = pend - padded
    nblk = -(-(n * TOP_K) // MOE_BLOCK) + N_EXPERTS
    blk_start = jnp.arange(nblk, dtype=I32) * MOE_BLOCK
    blk_e = jnp.minimum(jnp.sum((pend[None, :] <= blk_start[:, None]).astype(I32), axis=1), N_EXPERTS - 1)
    n_used = (pend[-1:] // MOE_BLOCK).astype(I32)
    onehot = top_idx[:, :, None] == jnp.arange(N_EXPERTS, dtype=I32)[None, None, :]
    dest = rank + jnp.sum(jnp.where(onehot, pstart[None, None, :], 0), axis=-1)
    n_slots = nblk * MOE_BLOCK
    pad_len = padded - counts
    pad_end = jnp.cumsum(pad_len)
    base = jnp.concatenate([pstart + counts - (pad_end - pad_len), pend[-1:] - pad_end[-1:]])
    j = jnp.arange(n_slots - n * TOP_K, dtype=I32)
    group = jnp.sum((pad_end[None, :] <= j[:, None]).astype(I32), axis=1)
    group_hot = group[:, None] == jnp.arange(N_EXPERTS + 1, dtype=I32)[None, :]
    pad_slots = j + jnp.sum(jnp.where(group_hot, base[None, :], 0), axis=1)
    xs = _dispatch(dest, pad_slots, h2, n_slots)
    ys = _experts(blk_e, n_used, xs, w_e_in, b_e_in, w_e_out, b_e_out)
    return _combine(dest, x2, gate, ys)


def _layer(x, mem, g_mix, w_in, g_q_dsa, g_k_dsa, g_q_mem, g_k_mem, g_mem, w_mem_kv, w_gate, b_gate,
           w_branch, w_out, g_ffn, w_router, b_router, w_e_in, b_e_in, w_e_out, b_e_out):
    batch, seq, _ = x.shape
    n = batch * seq
    topk = min(DSA_TOPK_MAX, seq // 4)
    x2d = x.reshape(n, D_MODEL)
    tm_dense = min(DENSE_ROWS, seq)
    sq, sk, sv, dq, iq, kvi, iw, mq, vt = _inproj(x2d, batch, seq, g_mix, w_in, g_q_dsa, g_k_dsa, g_q_mem, tm_dense)
    o_sb = _sb_attention(sq, sk, sv, min(256, seq))
    o_dsa = _dsa_attention(dq, iq, iw, kvi, vt, DSA_SEG, topk)
    mlen = mem.shape[1]
    mk, mv = _mem_kv(mem.reshape(batch * mlen, D_MODEL), g_mem, w_mem_kv, g_k_mem, min(512, batch * mlen))
    mw = MEM_HEADS * MEM_HEAD_DIM
    o_mem = _mem_attention(mq.reshape(batch, seq, mw), mk.reshape(batch, mlen, mw), mv.reshape(batch, mlen, mw), seq)
    x2, h2, top_idx, gate, rank, counts = _merge_route(
        x2d, g_mix, o_sb.reshape(n, BRANCH_W), o_dsa.reshape(n, BRANCH_W), o_mem.reshape(n, mw),
        w_gate, b_gate, w_branch, w_out, g_ffn, w_router, b_router, tm_dense)
    out = _moe(x2, h2, top_idx, gate, rank, counts, w_e_in, b_e_in, w_e_out, b_e_out)
    return out.reshape(batch, seq, D_MODEL)


def kernel(x, mem, g_mix, w_in, g_q_dsa, g_k_dsa, g_q_mem, g_k_mem, g_mem, w_mem_kv, w_gate, b_gate, w_branch, w_out, g_ffn, w_router, b_router, w_e_in, b_e_in, w_e_out, b_e_out):
    for l in range(g_mix.shape[0]):
        x = _layer(x, mem, g_mix[l], w_in[l], g_q_dsa[l], g_k_dsa[l], g_q_mem[l], g_k_mem[l], g_mem[l],
                   w_mem_kv[l], w_gate[l], b_gate[l], w_branch[l], w_out[l], g_ffn[l], w_router[l],
                   b_router[l], w_e_in[l], b_e_in[l], w_e_out[l], b_e_out[l])
    return x
```

```python
import functools

import numpy as np
import jax
import jax.numpy as jnp
from jax import lax
from jax.experimental import pallas as pl
from jax.experimental.pallas import tpu as pltpu

F32 = jnp.float32
BF16 = jnp.bfloat16
I32 = jnp.int32

D_MODEL = 1024
CHUNK = 64
SB_HEADS = 8
DSA_HEADS = 8
HEAD_DIM = 64
IDX_HEADS = 4
DSA_TOPK_MAX = 256
MEM_HEADS = 4
MEM_HEAD_DIM = 128
N_EXPERTS = 32
TOP_K = 4
D_EXPERT = D_MODEL
SWIGLU_LIMIT = 7.0
SWIGLU_ALPHA = 1.702
ROPE_THETA = 10000.0
EPS = 1e-6
MOE_BLOCK = 512

BRANCH_W = 512
IN_SIZES = (512, 512, 512, 512, 64, 64, 256, 64, 4, 512)
C_SQ, C_SK, C_DQ, C_MQ, C_IQ, C_SMALL, C_END = 0, 512, 1024, 1536, 2048, 2304, 2560
IW_LANE = 64

LANES = 128
NEG_BIG = -1e30
SB_CUTOFF = 110.0
KEY_NEG_INF = int(np.array(-np.inf, np.float32).view(np.int32)) ^ 0x7FFFFFFF
INT_MIN = -(2 ** 31)

VMEM_LIMIT = 56 * 1024 * 1024
DENSE_ROWS = 1024


def _cparams(sem):
    return pltpu.CompilerParams(dimension_semantics=sem, vmem_limit_bytes=VMEM_LIMIT)


def _dot(a, b):
    return jnp.dot(a, b, preferred_element_type=F32)


def _dot_nt(a, b):
    return lax.dot_general(a, b, (((1,), (1,)), ((), ())), preferred_element_type=F32)


def _split_bf16(x):
    hi = x.astype(BF16)
    lo = (x - hi.astype(F32)).astype(BF16)
    return hi, lo


def _dot_split(x, m_bf16):
    hi, lo = _split_bf16(x)
    return _dot(hi, m_bf16) + _dot(lo, m_bf16)


TOKEN_TILE = D_MODEL // LANES


def _store_token_tiles(ref, y):
    rows = y.shape[0]
    for s in range(TOKEN_TILE):
        ref[pl.ds(s, rows, stride=TOKEN_TILE), :] = y[:, s * LANES:(s + 1) * LANES]


def _load_token_tiles(ref, start_row, rows, s):
    return ref[pl.ds(start_row * TOKEN_TILE + s, rows, stride=TOKEN_TILE), :]


def _rot_half_unsigned(y):
    w = y.shape[1]
    lane = lax.broadcasted_iota(I32, y.shape, 1)
    return jnp.where((lane & 32) == 0, pltpu.roll(y, w - 32, 1), pltpu.roll(y, 32, 1))


def _inproj_kernel(x_ref, g_ref, w_ref, wsvt_ref, cos_ref, sin_ref, coss_ref, sins_ref, gq_ref, gks_ref,
                   gm_ref, bd64_ref, bd128_ref,
                   sq_ref, sk_ref, sv_ref, dq_ref, iq_ref, kvi_ref, iw_ref, mq_ref, vt_ref):
    x = x_ref[...]
    ms = jnp.mean(x * x, axis=-1, keepdims=True)
    h = (x * lax.rsqrt(ms + EPS) * g_ref[...]).astype(BF16)

    def seg(a, b):
        return _dot(h, w_ref[:, a:b])

    def put_heads(ref, y):
        for hd in range(y.shape[1] // HEAD_DIM):
            ref[0, hd] = y[:, hd * HEAD_DIM:(hd + 1) * HEAD_DIM].astype(BF16)

    put_heads(sq_ref, seg(C_SQ, C_SK) * (HEAD_DIM ** -0.5))
    put_heads(sk_ref, seg(C_SK, C_DQ))
    sv_ref[0] = _dot_nt(wsvt_ref[...], h).astype(BF16)

    y = seg(C_DQ, C_MQ)
    msq = _dot_split(y * y, bd64_ref[...]) * (1.0 / HEAD_DIM)
    y = y * lax.rsqrt(msq + EPS) * gq_ref[...]
    y = y * cos_ref[...] + _rot_half_unsigned(y) * sin_ref[...]
    put_heads(dq_ref, y * (HEAD_DIM ** -0.5))

    y = seg(C_IQ, C_SMALL)
    y = y * cos_ref[:, :256] + _rot_half_unsigned(y) * sin_ref[:, :256]
    iq_ref[...] = (y * (HEAD_DIM ** -0.5)).astype(BF16)

    y = seg(C_SMALL, C_END)
    lane = lax.broadcasted_iota(I32, y.shape, 1)
    is_k = lane < HEAD_DIM
    msk = jnp.sum(jnp.where(is_k, y * y, 0.0), axis=-1, keepdims=True) * (1.0 / HEAD_DIM)
    y = y * jnp.where(is_k, lax.rsqrt(msk + EPS) * gks_ref[...], 1.0)
    y = y * coss_ref[...] + _rot_half_unsigned(y) * sins_ref[...]
    kvi_ref[...] = y.astype(BF16)
    iw_ref[...] = y[:, 128:256]
    kv_t = y[:, 0:128].T
    row_t = lax.broadcasted_iota(I32, kv_t.shape, 0)
    vt_ref[0] = jnp.where(row_t < HEAD_DIM, 1.0, kv_t).astype(BF16)

    y = seg(C_MQ, C_IQ)
    msm = _dot_split(y * y, bd128_ref[...]) * (1.0 / MEM_HEAD_DIM)
    mq_ref[...] = (y * lax.rsqrt(msm + EPS) * gm_ref[...]).astype(BF16)


def _rope_tables(seq):
    half = HEAD_DIM // 2
    inv = ROPE_THETA ** (-jnp.arange(half, dtype=F32) / half)
    ang = jnp.arange(seq).astype(F32)[:, None] * inv[None, :]
    cos = jnp.cos(ang)
    sin = jnp.sin(ang)
    cos64 = jnp.concatenate([cos, cos], axis=1)
    sin64 = jnp.concatenate([-sin, sin], axis=1)
    one = jnp.ones_like(cos64)
    zero = jnp.zeros_like(cos64)
    cosq = jnp.tile(cos64, (1, 8))
    sinq = jnp.tile(sin64, (1, 8))
    coss = jnp.concatenate([cos64, one, cos64, one], axis=1)
    sins = jnp.concatenate([sin64, zero, sin64, zero], axis=1)
    return cosq, sinq, coss, sins


def _block_diag_ones(width, group):
    idx = np.arange(width) // group
    return jnp.asarray((idx[:, None] == idx[None, :]).astype(np.float32), dtype=BF16)


def _inproj(x2d, batch, seq, g_mix, w_in, g_q_dsa, g_k_dsa, g_q_mem, tm):
    n = x2d.shape[0]
    sizes = np.cumsum((0,) + IN_SIZES)
    col = {name: (int(sizes[i]), int(sizes[i + 1])) for i, name in enumerate(
        ("sq", "sk", "sv", "dq", "dk", "dv", "iq", "ik", "iw", "mq"))}
    order = ("sq", "sk", "dq", "mq", "iq", "dk", "dv", "ik", "iw")
    width = sum(col[k][1] - col[k][0] for k in order)
    w = jnp.concatenate([w_in[:, col[k][0]:col[k][1]] for k in order]
                        + [jnp.zeros((D_MODEL, C_END - width), w_in.dtype)], axis=1).astype(BF16)
    cosq, sinq, coss, sins = (jnp.asarray(t) for t in _rope_tables(seq))
    gq = jnp.tile(g_q_dsa, 8)[None, :]
    gks = jnp.concatenate([g_k_dsa, jnp.ones((256 - HEAD_DIM,), F32)])[None, :]
    gm = jnp.tile(g_q_mem, MEM_HEADS)[None, :]
    spb = seq // tm
    row = lambda i: (i, 0)
    const = lambda i: (0, 0)
    pos = lambda i: (i % spb, 0)
    heads = lambda i: (i // spb, 0, i % spb, 0)
    head_shape = jax.ShapeDtypeStruct((batch, 8, seq, HEAD_DIM), BF16)
    head_spec = pl.BlockSpec((1, 8, tm, HEAD_DIM), heads)
    return pl.pallas_call(
        _inproj_kernel,
        grid=(n // tm,),
        in_specs=[
            pl.BlockSpec((tm, D_MODEL), row),
            pl.BlockSpec((1, D_MODEL), const),
            pl.BlockSpec((D_MODEL, C_END), const, pipeline_mode=pl.Buffered(1)),
            pl.BlockSpec((512, D_MODEL), const, pipeline_mode=pl.Buffered(1)),
            pl.BlockSpec((tm, 512), pos), pl.BlockSpec((tm, 512), pos),
            pl.BlockSpec((tm, 256), pos), pl.BlockSpec((tm, 256), pos),
            pl.BlockSpec((1, 512), const), pl.BlockSpec((1, 256), const), pl.BlockSpec((1, 512), const),
            pl.BlockSpec((512, 512), const), pl.BlockSpec((512, 512), const),
        ],
        out_specs=[head_spec, head_spec, pl.BlockSpec((1, 512, tm), lambda i: (i // spb, 0, i % spb)), head_spec,
                   pl.BlockSpec((tm, 256), row), pl.BlockSpec((tm, 256), row),
                   pl.BlockSpec((tm, 128), row), pl.BlockSpec((tm, 512), row),
                   pl.BlockSpec((1, 128, tm), lambda i: (i // spb, 0, i % spb))],
        out_shape=[head_shape, head_shape, jax.ShapeDtypeStruct((batch, 512, seq), BF16), head_shape,
                   jax.ShapeDtypeStruct((n, 256), BF16), jax.ShapeDtypeStruct((n, 256), BF16),
                   jax.ShapeDtypeStruct((n, 128), F32), jax.ShapeDtypeStruct((n, 512), BF16),
                   jax.ShapeDtypeStruct((batch, 128, seq), BF16)],
        compiler_params=_cparams(("parallel",)),
        name="inproj",
    )(x2d, g_mix[None, :], w, w_in[:, col["sv"][0]:col["sv"][1]].T.astype(BF16), cosq, sinq, coss, sins, gq, gks, gm,
      _block_diag_ones(512, HEAD_DIM), _block_diag_ones(512, MEM_HEAD_DIM))


def _sb_kernel(q_ref, k_ref, vt_ref, u_ref, o_ref, acc_ref, car_ref, *, tq, blocks_per_step):
    def query_block(r, carry):
        rows = pl.ds(pl.multiple_of(r * tq, tq), tq)
        _sb_sweep(q_ref.at[:, :, rows, :], k_ref, vt_ref, u_ref, o_ref.at[:, rows, :], acc_ref, car_ref,
                  qi=pl.program_id(1) * blocks_per_step + r, tq=tq)
        return carry

    lax.fori_loop(0, blocks_per_step, query_block, 0)


def _sb_sweep(q_ref, k_ref, vt_ref, u_ref, o_ref, acc_ref, car_ref, *, qi, tq):
    rows = lax.broadcasted_iota(I32, (tq, tq), 0)
    cols = lax.broadcasted_iota(I32, (tq, tq), 1)
    dif = rows - cols
    u = u_ref[...]
    acc_ref[...] = jnp.zeros_like(acc_ref)
    car_ref[...] = jnp.zeros_like(car_ref)

    def cond(c):
        kb, mx = c
        return jnp.logical_and(kb >= 0, mx > -SB_CUTOFF)

    def body(c):
        kb, _ = c
        ks = pl.multiple_of(kb * tq, tq)
        earlier = dif < (qi - kb) * tq
        neg_mask = jnp.where(earlier, -1.0, 0.0).astype(BF16)
        heads = range(SB_HEADS)
        z = [_dot_nt(k_ref[0, hd, pl.ds(ks, tq), :], q_ref[0, hd]) for hd in heads]
        ls, lk, between = [], [], []
        for hd in heads:
            zb = z[hd].astype(BF16)
            sp = jnp.maximum(zb, 0.0) + jnp.log(1.0 + jnp.exp(-jnp.abs(zb)))
            ls.append(z[hd] - sp.astype(F32))
            lk.append(sp * neg_mask)
            between.append(_dot(u, lk[hd]))
        for hd in heads:
            rs = slice(hd * HEAD_DIM, (hd + 1) * HEAD_DIM)
            car = car_ref[hd:hd + 1, :]
            w = jnp.where(earlier, jnp.exp(ls[hd] + between[hd] + car), 0.0)
            acc_ref[rs, :] += _dot(vt_ref[0, rs, pl.ds(ks, tq)], w.astype(BF16))
            car_ref[hd:hd + 1, :] = car + (between[hd][0:1, :] + lk[hd][0:1, :].astype(F32))
        return kb - 1, jnp.max(car_ref[...])

    lax.while_loop(cond, body, (qi, jnp.float32(0.0)))
    o_ref[0] = acc_ref[...].T.astype(BF16)


def _strict_tri(n, lower):
    i = np.arange(n)
    m = (i[:, None] > i[None, :]) if lower else (i[:, None] < i[None, :])
    return jnp.asarray(m.astype(np.float32), dtype=BF16)


SB_BLOCKS_PER_STEP = 8


def _sb_attention(sq, sk, svt, tq):
    batch, _, seq, _ = sq.shape
    per_step = min(SB_BLOCKS_PER_STEP, seq // tq)
    return pl.pallas_call(
        functools.partial(_sb_kernel, tq=tq, blocks_per_step=per_step),
        grid=(batch, seq // (tq * per_step)),
        in_specs=[pl.BlockSpec((1, SB_HEADS, tq * per_step, HEAD_DIM), lambda b, i: (b, 0, i, 0)),
                  pl.BlockSpec((1, SB_HEADS, seq, HEAD_DIM), lambda b, i: (b, 0, 0, 0)),
                  pl.BlockSpec((1, BRANCH_W, seq), lambda b, i: (b, 0, 0)),
                  pl.BlockSpec((tq, tq), lambda b, i: (0, 0))],
        out_specs=pl.BlockSpec((1, tq * per_step, BRANCH_W), lambda b, i: (b, i, 0)),
        out_shape=jax.ShapeDtypeStruct((batch, seq, BRANCH_W), BF16),
        scratch_shapes=[pltpu.VMEM((BRANCH_W, tq), F32), pltpu.VMEM((SB_HEADS, tq), F32)],
        compiler_params=_cparams(("parallel", "parallel")),
        name="sb_attention",
    )(sq, sk, svt, _strict_tri(tq, lower=False))


DSA_SEG = 256
DSA_KB = DSA_SEG


def _tree_sum(parts):
    while len(parts) > 1:
        parts = [parts[i] + parts[i + 1] for i in range(0, len(parts) - 1, 2)] + ([parts[-1]] if len(parts) % 2 else [])
    return parts[0]


def _dsa_kernel(dq_ref, iq_ref, iw_ref, kvi_ref, vt_ref, tri_ref, o_ref,
                sc_ref, bias_ref, s_ref, *, tq, topk, nseg_max):
    nseg = pl.program_id(1) + 1

    def query_block(r, carry, ns):
        rows = pl.ds(pl.multiple_of(r * tq, tq), tq)
        _dsa_body(dq_ref.at[:, :, rows, :], iq_ref.at[:, rows, :], iw_ref.at[:, rows, :], kvi_ref, vt_ref, tri_ref,
                  o_ref.at[:, rows, :], sc_ref, bias_ref, s_ref,
                  qs=pl.program_id(1) * DSA_SEG + r * tq, tq=tq, topk=topk, nseg=ns)
        return carry

    for ns in range(1, nseg_max + 1):
        @pl.when(nseg == ns)
        def _(ns=ns):
            lax.fori_loop(0, DSA_SEG // tq, functools.partial(query_block, ns=ns), 0)


def _dsa_body(dq_ref, iq_ref, iw_ref, kvi_ref, vt_ref, tri_ref, o_ref, sc_ref, bias_ref, s_ref,
              *, qs, tq, topk, nseg):
    blocks = [slice(c * DSA_KB, (c + 1) * DSA_KB) for c in range(nseg)]
    iq = iq_ref[0]
    w_t = iw_ref[0].T
    w_row = [w_t[IW_LANE + h:IW_LANE + h + 1, :] * (IDX_HEADS ** -0.5) for h in range(IDX_HEADS)]
    q_chunk = (qs + lax.broadcasted_iota(I32, (DSA_KB, tq), 1)) // CHUNK
    k_chunk = lax.broadcasted_iota(I32, (DSA_KB, tq), 0) // CHUNK

    for c, blk in enumerate(blocks):
        ik = kvi_ref[0, blk, 128:192]
        lg = [_dot_nt(ik, iq[:, h * HEAD_DIM:(h + 1) * HEAD_DIM]) for h in range(IDX_HEADS)]
        sc = jnp.zeros((DSA_KB, tq), F32)
        for h in range(IDX_HEADS):
            sc = sc + w_row[h] * jnp.maximum(lg[h], 0.0)
        admissible = (c * (DSA_KB // CHUNK) + k_chunk) <= q_chunk
        sc_ref[blk, :] = jnp.where(admissible, sc, -jnp.inf)

    def count(pred_fn):
        sub, n_sums = 32, 4
        acc = [jnp.zeros((sub, tq), F32)] * n_sums
        for j in range(nseg * DSA_SEG // sub):
            acc[j % n_sums] = acc[j % n_sums] + jnp.where(pred_fn(sc_ref[j * sub:(j + 1) * sub, :]), 1.0, 0.0)
        return jnp.sum(_tree_sum(acc), axis=0, keepdims=True)

    def key_to_float(key):
        return lax.bitcast_convert_type(jnp.where(key >= 0, key, key ^ 0x7FFFFFFF), F32)

    kf = jnp.float32(topk)
    n_rows = jnp.float32(nseg * DSA_SEG)
    cnt0 = count(lambda s: s >= 0.0)
    t0 = jnp.where(cnt0 >= kf, 0, INT_MIN).astype(I32)
    cnt_t0 = jnp.where(cnt0 >= kf, cnt0, n_rows)

    def bit_step(i, carry):
        t, cnt_t = carry
        cand = t + lax.shift_left(jnp.int32(1), 30 - i)
        cand_f = key_to_float(cand)
        cnt = jnp.where(cand <= KEY_NEG_INF, n_rows, count(lambda s: s >= cand_f))
        take = cnt >= kf
        return jnp.where(take, cand, t), jnp.where(take, cnt, cnt_t)

    thr_key, cnt_thr = lax.fori_loop(0, 31, bit_step, (t0, cnt_t0))
    thr = jnp.where(thr_key <= KEY_NEG_INF, -jnp.inf, key_to_float(thr_key))
    tri = tri_ref[...]

    surplus = jnp.logical_or(jnp.max(cnt_thr) > kf, jnp.min(thr) == -jnp.inf)

    @pl.when(surplus)
    def _():
        need = kf - count(lambda s: s > thr)
        prefix = jnp.zeros((1, tq), F32)
        for blk in blocks:
            sc = sc_ref[blk, :]
            eqf = jnp.where(sc == thr, 1.0, 0.0)
            rank = _dot(tri, eqf.astype(BF16)) + prefix
            tie = jnp.where(rank < need, eqf, 0.0)
            sel = jnp.where(sc > thr, 1.0, tie)
            bias_ref[blk, :] = jnp.where(sc > -jnp.inf, (sel - 1.0) * (-NEG_BIG), NEG_BIG)
            prefix = prefix + jnp.sum(eqf, axis=0, keepdims=True)

    @pl.when(jnp.logical_not(surplus))
    def _():
        for blk in blocks:
            bias_ref[blk, :] = jnp.where(sc_ref[blk, :] >= thr, 0.0, NEG_BIG)

    q8 = dq_ref[0].reshape(DSA_HEADS * tq, HEAD_DIM)
    m = jnp.full((1, DSA_HEADS * tq), NEG_BIG, F32)
    for blk in blocks:
        b = bias_ref[blk, :]
        s = _dot_nt(kvi_ref[0, blk, 0:HEAD_DIM], q8) + jnp.concatenate([b] * DSA_HEADS, axis=1)
        s_ref[blk, :] = s
        m = jnp.maximum(m, jnp.max(s, axis=0, keepdims=True))
    acc = jnp.zeros((128, DSA_HEADS * tq), F32)
    for blk in blocks:
        p = jnp.exp(s_ref[blk, :] - m)
        acc = acc + _dot(vt_ref[0, :, blk], p.astype(BF16))
    for hd in range(DSA_HEADS):
        a = acc[:, hd * tq:(hd + 1) * tq]
        o = (a / a[0:1, :]).T
        o_ref[0, :, hd * HEAD_DIM:(hd + 1) * HEAD_DIM] = o[:, HEAD_DIM:].astype(BF16)


def _dsa_attention(dq, iq, iw, kvi, vt, tq, topk):
    batch, _, seq, _ = dq.shape
    return pl.pallas_call(
        functools.partial(_dsa_kernel, tq=tq, topk=topk, nseg_max=seq // DSA_SEG),
        grid=(batch, seq // DSA_SEG),
        in_specs=[
            pl.BlockSpec((1, DSA_HEADS, DSA_SEG, HEAD_DIM), lambda b, i: (b, 0, i, 0)),
            pl.BlockSpec((1, DSA_SEG, 256), lambda b, i: (b, i, 0)),
            pl.BlockSpec((1, DSA_SEG, 128), lambda b, i: (b, i, 0)),
            pl.BlockSpec((1, seq, 256), lambda b, i: (b, 0, 0)),
            pl.BlockSpec((1, 128, seq), lambda b, i: (b, 0, 0)),
            pl.BlockSpec((DSA_KB, DSA_KB), lambda b, i: (0, 0)),
        ],
        out_specs=pl.BlockSpec((1, DSA_SEG, BRANCH_W), lambda b, i: (b, i, 0)),
        out_shape=jax.ShapeDtypeStruct((batch, seq, BRANCH_W), BF16),
        scratch_shapes=[pltpu.VMEM((seq, tq), F32), pltpu.VMEM((seq, tq), F32),
                        pltpu.VMEM((seq, DSA_HEADS * tq), F32)],
        compiler_params=_cparams(("parallel", "parallel")),
        name="dsa_attention",
    )(dq, iq.reshape(batch, seq, 256), iw.reshape(batch, seq, 128), kvi.reshape(batch, seq, 256), vt,
      _strict_tri(DSA_KB, lower=True))


def _memkv_kernel(m_ref, g_ref, w_ref, gk_ref, bd_ref, mk_ref, mv_ref):
    x = m_ref[...]
    ms = jnp.mean(x * x, axis=-1, keepdims=True)
    h = (x * lax.rsqrt(ms + EPS) * g_ref[...]).astype(BF16)
    mw = MEM_HEADS * MEM_HEAD_DIM
    k = _dot(h, w_ref[:, :mw])
    msk = _dot_split(k * k, bd_ref[...]) * (1.0 / MEM_HEAD_DIM)
    mk_ref[...] = (k * lax.rsqrt(msk + EPS) * gk_ref[...]).astype(BF16)
    mv_ref[...] = _dot(h, w_ref[:, mw:]).astype(BF16)


def _mem_kv(mem2d, g_mem, w_mem_kv, g_k_mem, tm):
    n = mem2d.shape[0]
    mw = MEM_HEADS * MEM_HEAD_DIM
    row = lambda i: (i, 0)
    const = lambda i: (0, 0)
    return pl.pallas_call(
        _memkv_kernel,
        grid=(n // tm,),
        in_specs=[pl.BlockSpec((tm, D_MODEL), row), pl.BlockSpec((1, D_MODEL), const),
                  pl.BlockSpec((D_MODEL, 2 * mw), const), pl.BlockSpec((1, mw), const),
                  pl.BlockSpec((mw, mw), const)],
        out_specs=[pl.BlockSpec((tm, mw), row), pl.BlockSpec((tm, mw), row)],
        out_shape=[jax.ShapeDtypeStruct((n, mw), BF16), jax.ShapeDtypeStruct((n, mw), BF16)],
        compiler_params=_cparams(("parallel",)),
        name="mem_kv",
    )(mem2d, g_mem[None, :], w_mem_kv.astype(BF16), jnp.tile(g_k_mem, MEM_HEADS)[None, :],
      _block_diag_ones(mw, MEM_HEAD_DIM))


def _memattn_kernel(q_ref, k_ref, v_ref, o_ref):
    for hd in range(MEM_HEADS):
        sl = slice(hd * MEM_HEAD_DIM, (hd + 1) * MEM_HEAD_DIM)
        s = _dot_nt(q_ref[0, :, sl], k_ref[0, :, sl]) * (MEM_HEAD_DIM ** -0.5)
        p = jnp.exp(s - jnp.max(s, axis=-1, keepdims=True))
        o = _dot(p.astype(BF16), v_ref[0, :, sl]) / jnp.sum(p, axis=-1, keepdims=True)
        o_ref[0, :, sl] = o.astype(BF16)


def _mem_attention(mq, mk, mv, tq):
    batch, seq, mw = mq.shape
    mlen = mk.shape[1]
    kspec = pl.BlockSpec((1, mlen, mw), lambda b, i: (b, 0, 0))
    return pl.pallas_call(
        _memattn_kernel,
        grid=(batch, seq // tq),
        in_specs=[pl.BlockSpec((1, tq, mw), lambda b, i: (b, i, 0)), kspec, kspec],
        out_specs=pl.BlockSpec((1, tq, mw), lambda b, i: (b, i, 0)),
        out_shape=jax.ShapeDtypeStruct((batch, seq, mw), BF16),
        compiler_params=_cparams(("parallel", "parallel")),
        name="mem_attention",
    )(mq, mk, mv)


def _merge_kernel(x_ref, g_ref, osb_ref, odsa_ref, omem_ref, wg_ref, bg_ref, wb_ref, wo_ref,
                  gf_ref, wrh_ref, wrl_ref, br_ref, tri_ref,
                  x2_ref, h2_ref, idx_ref, gate_ref, rank_ref, cnt_ref, carry_ref):
    x = x_ref[...]
    ms = jnp.mean(x * x, axis=-1, keepdims=True)
    h = (x * lax.rsqrt(ms + EPS) * g_ref[...]).astype(BF16)
    merged = None
    for n, o_ref in enumerate((osb_ref, odsa_ref, omem_ref)):
        gate = jax.nn.sigmoid(_dot(h, wg_ref[n]) + bg_ref[n])
        term = gate * _dot(o_ref[...], wb_ref[n])
        merged = term if merged is None else merged + term
    x2 = x + _dot(merged.astype(BF16), wo_ref[...])
    x2_ref[...] = x2

    ms2 = jnp.mean(x2 * x2, axis=-1, keepdims=True)
    h2 = x2 * lax.rsqrt(ms2 + EPS) * gf_ref[...]
    _store_token_tiles(h2_ref, h2)
    hi, lo = _split_bf16(h2)
    logits = _dot(hi, wrh_ref[...]) + _dot(hi, wrl_ref[...]) + _dot(lo, wrh_ref[...]) + br_ref[...]

    lane = lax.broadcasted_iota(I32, logits.shape, 1)
    vals = logits
    top_v, top_i = [], []
    for _ in range(TOP_K):
        mx = jnp.max(vals, axis=-1, keepdims=True)
        ix = jnp.min(jnp.where(vals == mx, lane, N_EXPERTS), axis=-1, keepdims=True)
        top_v.append(mx)
        top_i.append(ix)
        vals = jnp.where(lane == ix, -jnp.inf, vals)
    ex = [jnp.exp(v - top_v[0]) for v in top_v]
    den = ex[0] + ex[1] + ex[2] + ex[3]
    for k in range(TOP_K):
        idx_ref[:, k:k + 1] = top_i[k]
        gate_ref[:, k:k + 1] = ex[k] / den

    @pl.when(pl.program_id(0) == 0)
    def _():
        carry_ref[...] = jnp.zeros_like(carry_ref)

    lane_e = lax.broadcasted_iota(I32, (x.shape[0], LANES), 1)
    hits = [lane_e == top_i[k] for k in range(TOP_K)]
    onehot = sum(jnp.where(hk, 1.0, 0.0) for hk in hits)
    before = _dot(tri_ref[...], onehot.astype(BF16)) + carry_ref[...]
    for k in range(TOP_K):
        rank_ref[:, k:k + 1] = jnp.sum(jnp.where(hits[k], before, 0.0), axis=-1, keepdims=True).astype(I32)
    carry_ref[...] += jnp.sum(onehot, axis=0, keepdims=True)
    cnt_ref[...] = carry_ref[...]


def _merge_route(x2d, g_mix, o_sb, o_dsa, o_mem, w_gate, b_gate, w_branch, w_out, g_ffn, w_router, b_router, tm):
    n = x2d.shape[0]
    row = lambda i: (i, 0)
    c2 = lambda i: (0, 0)
    c3 = lambda i: (0, 0, 0)
    wr_hi = w_router.astype(BF16)
    wr_lo = (w_router - wr_hi.astype(F32)).astype(BF16)
    return pl.pallas_call(
        _merge_kernel,
        grid=(n // tm,),
        in_specs=[pl.BlockSpec((tm, D_MODEL), row), pl.BlockSpec((1, D_MODEL), c2),
                  pl.BlockSpec((tm, BRANCH_W), row), pl.BlockSpec((tm, BRANCH_W), row),
                  pl.BlockSpec((tm, BRANCH_W), row),
                  pl.BlockSpec((3, D_MODEL, D_MODEL), c3, pipeline_mode=pl.Buffered(1)),
                  pl.BlockSpec((3, 1, D_MODEL), c3),
                  pl.BlockSpec((3, BRANCH_W, D_MODEL), c3, pipeline_mode=pl.Buffered(1)),
                  pl.BlockSpec((D_MODEL, D_MODEL), c2, pipeline_mode=pl.Buffered(1)),
                  pl.BlockSpec((1, D_MODEL), c2), pl.BlockSpec((D_MODEL, N_EXPERTS), c2),
                  pl.BlockSpec((D_MODEL, N_EXPERTS), c2), pl.BlockSpec((1, N_EXPERTS), c2),
                  pl.BlockSpec((tm, tm), c2, pipeline_mode=pl.Buffered(1))],
        out_specs=[pl.BlockSpec((tm, D_MODEL), row), pl.BlockSpec((tm * TOKEN_TILE, LANES), row),
                   pl.BlockSpec((tm, TOP_K), row), pl.BlockSpec((tm, TOP_K), row),
                   pl.BlockSpec((tm, TOP_K), row), pl.BlockSpec((1, LANES), c2)],
        out_shape=[jax.ShapeDtypeStruct((n, D_MODEL), F32), jax.ShapeDtypeStruct((n * TOKEN_TILE, LANES), F32),
                   jax.ShapeDtypeStruct((n, TOP_K), I32), jax.ShapeDtypeStruct((n, TOP_K), F32),
                   jax.ShapeDtypeStruct((n, TOP_K), I32), jax.ShapeDtypeStruct((1, LANES), F32)],
        scratch_shapes=[pltpu.VMEM((1, LANES), F32)],
        compiler_params=_cparams(("arbitrary",)),
        name="merge_route",
    )(x2d, g_mix[None, :], o_sb, o_dsa, o_mem, w_gate.astype(BF16), b_gate[:, None, :],
      w_branch.astype(BF16), w_out.astype(BF16), g_ffn[None, :], wr_hi, wr_lo, b_router[None, :],
      _strict_tri(tm, lower=True))


DISPATCH_TOKENS = 4096
COMBINE_TOKENS = 512


def _dispatch_kernel(dest_ref, pad_ref, h_ref, xs_ref, zero_ref, sem, *, n_tok, n_pad):
    n_copies = n_tok * TOP_K + n_pad
    zero_ref[...] = jnp.zeros_like(zero_ref)

    def fill(j, c):
        for r in range(2):
            dst = pl.multiple_of(pad_ref[0, 0, 2 * j + r] * TOKEN_TILE, TOKEN_TILE)
            pltpu.make_async_copy(zero_ref, xs_ref.at[pl.ds(dst, TOKEN_TILE)], sem).start(priority=r)
        return c

    lax.fori_loop(0, n_pad // 2, fill, 0, unroll=2)

    def start(t, c):
        src = pl.multiple_of(t * TOKEN_TILE, TOKEN_TILE)
        for k in range(TOP_K):
            dst = pl.multiple_of(dest_ref[0, 0, t * TOP_K + k] * TOKEN_TILE, TOKEN_TILE)
            pltpu.make_async_copy(h_ref.at[pl.ds(src, TOKEN_TILE)], xs_ref.at[pl.ds(dst, TOKEN_TILE)],
                                  sem).start(priority=k % 2)
        return c

    lax.fori_loop(0, n_tok, start, 0, unroll=2)
    total = n_copies * TOKEN_TILE
    pltpu.make_async_copy(xs_ref.at[pl.ds(0, total)], xs_ref.at[pl.ds(0, total)], sem).wait()


def _dispatch(dest, pad_slots, h2t, n_slots):
    n = h2t.shape[0] // TOKEN_TILE
    n_tok = min(DISPATCH_TOKENS, n)
    steps = n // n_tok
    per = n_tok * TOP_K
    n_pad = pad_slots.shape[0] // steps
    assert n_tok * steps == n and n_pad * steps == pad_slots.shape[0] and n_pad % 2 == 0
    return pl.pallas_call(
        functools.partial(_dispatch_kernel, n_tok=n_tok, n_pad=n_pad),
        grid=(steps,),
        in_specs=[pl.BlockSpec((1, 1, per), lambda i: (i, 0, 0), memory_space=pltpu.SMEM),
                  pl.BlockSpec((1, 1, n_pad), lambda i: (i, 0, 0), memory_space=pltpu.SMEM),
                  pl.BlockSpec((n_tok * TOKEN_TILE, LANES), lambda i: (i, 0))],
        out_specs=pl.BlockSpec(memory_space=pl.ANY),
        out_shape=jax.ShapeDtypeStruct((n_slots * TOKEN_TILE, LANES), F32),
        scratch_shapes=[pltpu.VMEM((TOKEN_TILE, LANES), F32), pltpu.SemaphoreType.DMA(())],
        compiler_params=_cparams(("arbitrary",)),
        name="moe_dispatch",
    )(dest.reshape(steps, 1, per), pad_slots.reshape(steps, 1, n_pad), h2t)


def _expert_kernel(blk_e_ref, nused_ref, x_ref, w1_ref, b1_ref, w2_ref, b2_ref, y_ref, xb_ref, w1b_ref, w2b_ref):
    i = pl.program_id(0)
    new_expert = jnp.logical_or(i == 0, blk_e_ref[i] != blk_e_ref[jnp.maximum(i - 1, 0)])

    @pl.when(jnp.logical_and(i < nused_ref[0], new_expert))
    def _():
        w1b_ref[...] = w1_ref[0].astype(BF16)
        w2b_ref[...] = w2_ref[0].astype(BF16)

    @pl.when(i < nused_ref[0])
    def _():
        for s in range(TOKEN_TILE):
            xb_ref[:, s * LANES:(s + 1) * LANES] = _load_token_tiles(x_ref, 0, MOE_BLOCK, s).astype(BF16)
        hb = _dot(xb_ref[...], w1b_ref[...]) + b1_ref[0]
        g = jnp.minimum(hb[:, :D_EXPERT], SWIGLU_LIMIT)
        u = jnp.clip(hb[:, D_EXPERT:], -SWIGLU_LIMIT, SWIGLU_LIMIT)
        act = (u + 1.0) * (g * jax.nn.sigmoid(SWIGLU_ALPHA * g))
        _store_token_tiles(y_ref, _dot(act.astype(BF16), w2b_ref[...]) + b2_ref[0])

    @pl.when(pl.program_id(0) >= nused_ref[0])
    def _():
        y_ref[...] = jnp.zeros_like(y_ref)


def _experts(blk_e, n_used, xs, w_e_in, b_e_in, w_e_out, b_e_out):
    n_slots = xs.shape[0] // TOKEN_TILE
    nblk = n_slots // MOE_BLOCK
    slot_block = pl.BlockSpec((MOE_BLOCK * TOKEN_TILE, LANES), lambda i, be, nu: (i, 0))
    grid_spec = pltpu.PrefetchScalarGridSpec(
        num_scalar_prefetch=2,
        grid=(nblk,),
        in_specs=[slot_block,
                  pl.BlockSpec((1, D_MODEL, 2 * D_EXPERT), lambda i, be, nu: (be[i], 0, 0)),
                  pl.BlockSpec((1, 1, 2 * D_EXPERT), lambda i, be, nu: (be[i], 0, 0)),
                  pl.BlockSpec((1, D_EXPERT, D_MODEL), lambda i, be, nu: (be[i], 0, 0)),
                  pl.BlockSpec((1, 1, D_MODEL), lambda i, be, nu: (be[i], 0, 0))],
        out_specs=slot_block,
        scratch_shapes=[pltpu.VMEM((MOE_BLOCK, D_MODEL), BF16), pltpu.VMEM((D_MODEL, 2 * D_EXPERT), BF16),
                        pltpu.VMEM((D_EXPERT, D_MODEL), BF16)],
    )
    return pl.pallas_call(
        _expert_kernel,
        grid_spec=grid_spec,
        out_shape=jax.ShapeDtypeStruct((n_slots * TOKEN_TILE, LANES), F32),
        compiler_params=_cparams(("arbitrary",)),
        name="moe_experts",
    )(blk_e, n_used, xs, w_e_in, b_e_in[:, None, :], w_e_out, b_e_out[:, None, :])


def _combine_kernel(dest_ref, dest_one_ref, dest_ahead_ref, x_ref, gate_ref, y_ref, o_ref, buf_ref, sem):
    n_copies = COMBINE_TOKENS * TOP_K
    rows = 32
    i = pl.program_id(0)
    slot, slot1, slot2 = i % 3, (i + 1) % 3, (i + 2) % 3
    buf = buf_ref.at[slot]

    def start_rows(idx_ref, dst_buf, dma_sem, t0):
        for t in range(rows):
            for k in range(TOP_K):
                src = pl.multiple_of(idx_ref[0, 0, (t0 + t) * TOP_K + k] * TOKEN_TILE, TOKEN_TILE)
                dst = pl.multiple_of((k * COMBINE_TOKENS + t0 + t) * TOKEN_TILE, TOKEN_TILE)
                pltpu.make_async_copy(y_ref.at[pl.ds(src, TOKEN_TILE)], dst_buf.at[pl.ds(dst, TOKEN_TILE)],
                                      dma_sem).start(priority=k % 2)

    def wait_rows(dst_buf, dma_sem):
        pltpu.make_async_copy(y_ref.at[pl.ds(0, n_copies * TOKEN_TILE)], dst_buf, dma_sem).wait()

    @pl.when(i == 0)
    def _():
        def first(c, z):
            start_rows(dest_ref, buf_ref.at[0], sem.at[0], c * rows)
            start_rows(dest_one_ref, buf_ref.at[1], sem.at[1], c * rows)
            return z
        lax.fori_loop(0, COMBINE_TOKENS // rows, first, 0)

    wait_rows(buf, sem.at[slot])

    def sum_pass(c, z):
        r0 = pl.multiple_of(c * rows, rows)
        start_rows(dest_ahead_ref, buf_ref.at[slot2], sem.at[slot2], r0)
        gate = gate_ref[pl.ds(r0, rows), :]
        gates = [jnp.broadcast_to(gate[:, k:k + 1], (rows, LANES)) for k in range(TOP_K)]
        for s in range(TOKEN_TILE):
            out = x_ref[pl.ds(r0, rows), s * LANES:(s + 1) * LANES]
            for k in range(TOP_K):
                out = out + gates[k] * _load_token_tiles(buf, k * COMBINE_TOKENS + r0, rows, s)
            o_ref[pl.ds(r0, rows), s * LANES:(s + 1) * LANES] = out
        return z

    lax.fori_loop(0, COMBINE_TOKENS // rows, sum_pass, 0)

    @pl.when(i == pl.num_programs(0) - 1)
    def _():
        wait_rows(buf_ref.at[slot1], sem.at[slot1])
        wait_rows(buf_ref.at[slot2], sem.at[slot2])


def _combine(dest, x2, gate, ys):
    n = x2.shape[0]
    steps = n // COMBINE_TOKENS
    per = COMBINE_TOKENS * TOP_K
    dest3 = dest.reshape(steps, 1, per)
    return pl.pallas_call(
        _combine_kernel,
        grid=(steps,),
        in_specs=[pl.BlockSpec((1, 1, per), lambda i: (i, 0, 0), memory_space=pltpu.SMEM),
                  pl.BlockSpec((1, 1, per), lambda i: (min(1, steps - 1), 0, 0), memory_space=pltpu.SMEM),
                  pl.BlockSpec((1, 1, per), lambda i: (jnp.minimum(i + 2, steps - 1), 0, 0), memory_space=pltpu.SMEM),
                  pl.BlockSpec((COMBINE_TOKENS, D_MODEL), lambda i: (i, 0)),
                  pl.BlockSpec((COMBINE_TOKENS, TOP_K), lambda i: (i, 0)),
                  pl.BlockSpec(memory_space=pl.ANY)],
        out_specs=pl.BlockSpec((COMBINE_TOKENS, D_MODEL), lambda i: (i, 0)),
        out_shape=jax.ShapeDtypeStruct((n, D_MODEL), F32),
        scratch_shapes=[pltpu.VMEM((3, TOP_K * COMBINE_TOKENS * TOKEN_TILE, LANES), F32),
                        pltpu.SemaphoreType.DMA((3,))],
        compiler_params=_cparams(("arbitrary",)),
        name="moe_combine",
    )(dest3, dest3, dest3, x2, gate, ys)


def _moe(x2, h2, top_idx, gate, rank, counts, w_e_in, b_e_in, w_e_out, b_e_out):
    n = x2.shape[0]
    counts = counts[0, :N_EXPERTS].astype(I32)
    padded = (counts + MOE_BLOCK - 1) // MOE_BLOCK * MOE_BLOCK
    pend = jnp.cumsum(padded)
    pstart = pend - padded
    nblk = -(-(n * TOP_K) // MOE_BLOCK) + N_EXPERTS
    blk_start = jnp.arange(nblk, dtype=I32) * MOE_BLOCK
    blk_e = jnp.minimum(jnp.sum((pend[None, :] <= blk_start[:, None]).astype(I32), axis=1), N_EXPERTS - 1)
    n_used = (pend[-1:] // MOE_BLOCK).astype(I32)
    onehot = top_idx[:, :, None] == jnp.arange(N_EXPERTS, dtype=I32)[None, None, :]
    dest = rank + jnp.sum(jnp.where(onehot, pstart[None, None, :], 0), axis=-1)
    n_slots = nblk * MOE_BLOCK
    pad_len = padded - counts
    pad_end = jnp.cumsum(pad_len)
    base = jnp.concatenate([pstart + counts - (pad_end - pad_len), pend[-1:] - pad_end[-1:]])
    j = jnp.arange(n_slots - n * TOP_K, dtype=I32)
    group = jnp.sum((pad_end[None, :] <= j[:, None]).astype(I32), axis=1)
    group_hot = group[:, None] == jnp.arange(N_EXPERTS + 1, dtype=I32)[None, :]
    pad_slots = j + jnp.sum(jnp.where(group_hot, base[None, :], 0), axis=1)
    xs = _dispatch(dest, pad_slots, h2, n_slots)
    ys = _experts(blk_e, n_used, xs, w_e_in, b_e_in, w_e_out, b_e_out)
    return _combine(dest, x2, gate, ys)


def _layer(x, mem, g_mix, w_in, g_q_dsa, g_k_dsa, g_q_mem, g_k_mem, g_mem, w_mem_kv, w_gate, b_gate,
           w_branch, w_out, g_ffn, w_router, b_router, w_e_in, b_e_in, w_e_out, b_e_out):
    batch, seq, _ = x.shape
    n = batch * seq
    topk = min(DSA_TOPK_MAX, seq // 4)
    x2d = x.reshape(n, D_MODEL)
    tm_dense = min(DENSE_ROWS, seq)
    sq, sk, sv, dq, iq, kvi, iw, mq, vt = _inproj(x2d, batch, seq, g_mix, w_in, g_q_dsa, g_k_dsa, g_q_mem, tm_dense)
    o_sb = _sb_attention(sq, sk, sv, min(256, seq))
    o_dsa = _dsa_attention(dq, iq, iw, kvi, vt, 128, topk)
    mlen = mem.shape[1]
    mk, mv = _mem_kv(mem.reshape(batch * mlen, D_MODEL), g_mem, w_mem_kv, g_k_mem, min(512, batch * mlen))
    mw = MEM_HEADS * MEM_HEAD_DIM
    o_mem = _mem_attention(mq.reshape(batch, seq, mw), mk.reshape(batch, mlen, mw), mv.reshape(batch, mlen, mw), seq)
    x2, h2, top_idx, gate, rank, counts = _merge_route(
        x2d, g_mix, o_sb.reshape(n, BRANCH_W), o_dsa.reshape(n, BRANCH_W), o_mem.reshape(n, mw),
        w_gate, b_gate, w_branch, w_out, g_ffn, w_router, b_router, tm_dense)
    out = _moe(x2, h2, top_idx, gate, rank, counts, w_e_in, b_e_in, w_e_out, b_e_out)
    return out.reshape(batch, seq, D_MODEL)


def kernel(x, mem, g_mix, w_in, g_q_dsa, g_k_dsa, g_q_mem, g_k_mem, g_mem, w_mem_kv, w_gate, b_gate, w_branch, w_out, g_ffn, w_router, b_router, w_e_in, b_e_in, w_e_out, b_e_out):
    for l in range(g_mix.shape[0]):
        x = _layer(x, mem, g_mix[l], w_in[l], g_q_dsa[l], g_k_dsa[l], g_q_mem[l], g_k_mem[l], g_mem[l],
                   w_mem_kv[l], w_gate[l], b_gate[l], w_branch[l], w_out[l], g_ffn[l], w_router[l],
                   b_router[l], w_e_in[l], b_e_in[l], w_e_out[l], b_e_out[l])
    return x
```

```python
import functools

import numpy as np
import jax
import jax.numpy as jnp
from jax import lax
from jax.experimental import pallas as pl
from jax.experimental.pallas import tpu as pltpu

F32 = jnp.float32
BF16 = jnp.bfloat16
I32 = jnp.int32

D_MODEL = 1024
CHUNK = 64
SB_HEADS = 8
DSA_HEADS = 8
HEAD_DIM = 64
IDX_HEADS = 4
DSA_TOPK_MAX = 256
MEM_HEADS = 4
MEM_HEAD_DIM = 128
N_EXPERTS = 32
TOP_K = 4
D_EXPERT = D_MODEL
SWIGLU_LIMIT = 7.0
SWIGLU_ALPHA = 1.702
ROPE_THETA = 10000.0
EPS = 1e-6
MOE_BLOCK = 512

BRANCH_W = 512
IN_SIZES = (512, 512, 512, 512, 64, 64, 256, 64, 4, 512)
C_SQ, C_SK, C_DQ, C_MQ, C_IQ, C_SMALL, C_END = 0, 512, 1024, 1536, 2048, 2304, 2560
IW_LANE = 64

LANES = 128
NEG_BIG = -1e30
SB_CUTOFF = 110.0
KEY_NEG_INF = int(np.array(-np.inf, np.float32).view(np.int32)) ^ 0x7FFFFFFF
INT_MIN = -(2 ** 31)

VMEM_LIMIT = 56 * 1024 * 1024
DENSE_ROWS = 1024


def _cparams(sem):
    return pltpu.CompilerParams(dimension_semantics=sem, vmem_limit_bytes=VMEM_LIMIT)


def _dot(a, b):
    return jnp.dot(a, b, preferred_element_type=F32)


def _dot_nt(a, b):
    return lax.dot_general(a, b, (((1,), (1,)), ((), ())), preferred_element_type=F32)


def _split_bf16(x):
    hi = x.astype(BF16)
    lo = (x - hi.astype(F32)).astype(BF16)
    return hi, lo


def _dot_split(x, m_bf16):
    hi, lo = _split_bf16(x)
    return _dot(hi, m_bf16) + _dot(lo, m_bf16)


TOKEN_TILE = D_MODEL // LANES


def _store_token_tiles(ref, y):
    rows = y.shape[0]
    for s in range(TOKEN_TILE):
        ref[pl.ds(s, rows, stride=TOKEN_TILE), :] = y[:, s * LANES:(s + 1) * LANES]


def _load_token_tiles(ref, start_row, rows, s):
    return ref[pl.ds(start_row * TOKEN_TILE + s, rows, stride=TOKEN_TILE), :]


def _rot_half_unsigned(y):
    w = y.shape[1]
    lane = lax.broadcasted_iota(I32, y.shape, 1)
    return jnp.where((lane & 32) == 0, pltpu.roll(y, w - 32, 1), pltpu.roll(y, 32, 1))


def _inproj_kernel(x_ref, g_ref, w_ref, wsvt_ref, cos_ref, sin_ref, coss_ref, sins_ref, gq_ref, gks_ref,
                   gm_ref, bd64_ref, bd128_ref,
                   sq_ref, sk_ref, sv_ref, dq_ref, iq_ref, kvi_ref, iw_ref, mq_ref, vt_ref):
    x = x_ref[...]
    ms = jnp.mean(x * x, axis=-1, keepdims=True)
    h = (x * lax.rsqrt(ms + EPS) * g_ref[...]).astype(BF16)

    def seg(a, b):
        return _dot(h, w_ref[:, a:b])

    def put_heads(ref, y):
        for hd in range(y.shape[1] // HEAD_DIM):
            ref[0, hd] = y[:, hd * HEAD_DIM:(hd + 1) * HEAD_DIM].astype(BF16)

    put_heads(sq_ref, seg(C_SQ, C_SK) * (HEAD_DIM ** -0.5))
    put_heads(sk_ref, seg(C_SK, C_DQ))
    sv_ref[0] = _dot_nt(wsvt_ref[...], h).astype(BF16)

    y = seg(C_DQ, C_MQ)
    msq = _dot_split(y * y, bd64_ref[...]) * (1.0 / HEAD_DIM)
    y = y * lax.rsqrt(msq + EPS) * gq_ref[...]
    y = y * cos_ref[...] + _rot_half_unsigned(y) * sin_ref[...]
    put_heads(dq_ref, y * (HEAD_DIM ** -0.5))

    y = seg(C_IQ, C_SMALL)
    y = y * cos_ref[:, :256] + _rot_half_unsigned(y) * sin_ref[:, :256]
    iq_ref[...] = (y * (HEAD_DIM ** -0.5)).astype(BF16)

    y = seg(C_SMALL, C_END)
    lane = lax.broadcasted_iota(I32, y.shape, 1)
    is_k = lane < HEAD_DIM
    msk = jnp.sum(jnp.where(is_k, y * y, 0.0), axis=-1, keepdims=True) * (1.0 / HEAD_DIM)
    y = y * jnp.where(is_k, lax.rsqrt(msk + EPS) * gks_ref[...], 1.0)
    y = y * coss_ref[...] + _rot_half_unsigned(y) * sins_ref[...]
    kvi_ref[...] = y.astype(BF16)
    iw_ref[...] = y[:, 128:256]
    kv_t = y[:, 0:128].T
    row_t = lax.broadcasted_iota(I32, kv_t.shape, 0)
    vt_ref[0] = jnp.where(row_t < HEAD_DIM, 1.0, kv_t).astype(BF16)

    y = seg(C_MQ, C_IQ)
    msm = _dot_split(y * y, bd128_ref[...]) * (1.0 / MEM_HEAD_DIM)
    mq_ref[...] = (y * lax.rsqrt(msm + EPS) * gm_ref[...]).astype(BF16)


def _rope_tables(seq):
    half = HEAD_DIM // 2
    inv = ROPE_THETA ** (-jnp.arange(half, dtype=F32) / half)
    ang = jnp.arange(seq).astype(F32)[:, None] * inv[None, :]
    cos = jnp.cos(ang)
    sin = jnp.sin(ang)
    cos64 = jnp.concatenate([cos, cos], axis=1)
    sin64 = jnp.concatenate([-sin, sin], axis=1)
    one = jnp.ones_like(cos64)
    zero = jnp.zeros_like(cos64)
    cosq = jnp.tile(cos64, (1, 8))
    sinq = jnp.tile(sin64, (1, 8))
    coss = jnp.concatenate([cos64, one, cos64, one], axis=1)
    sins = jnp.concatenate([sin64, zero, sin64, zero], axis=1)
    return cosq, sinq, coss, sins


def _block_diag_ones(width, group):
    idx = np.arange(width) // group
    return jnp.asarray((idx[:, None] == idx[None, :]).astype(np.float32), dtype=BF16)


def _inproj(x2d, batch, seq, g_mix, w_in, g_q_dsa, g_k_dsa, g_q_mem, tm):
    n = x2d.shape[0]
    sizes = np.cumsum((0,) + IN_SIZES)
    col = {name: (int(sizes[i]), int(sizes[i + 1])) for i, name in enumerate(
        ("sq", "sk", "sv", "dq", "dk", "dv", "iq", "ik", "iw", "mq"))}
    order = ("sq", "sk", "dq", "mq", "iq", "dk", "dv", "ik", "iw")
    width = sum(col[k][1] - col[k][0] for k in order)
    w = jnp.concatenate([w_in[:, col[k][0]:col[k][1]] for k in order]
                        + [jnp.zeros((D_MODEL, C_END - width), w_in.dtype)], axis=1).astype(BF16)
    cosq, sinq, coss, sins = (jnp.asarray(t) for t in _rope_tables(seq))
    gq = jnp.tile(g_q_dsa, 8)[None, :]
    gks = jnp.concatenate([g_k_dsa, jnp.ones((256 - HEAD_DIM,), F32)])[None, :]
    gm = jnp.tile(g_q_mem, MEM_HEADS)[None, :]
    spb = seq // tm
    row = lambda i: (i, 0)
    const = lambda i: (0, 0)
    pos = lambda i: (i % spb, 0)
    heads = lambda i: (i // spb, 0, i % spb, 0)
    head_shape = jax.ShapeDtypeStruct((batch, 8, seq, HEAD_DIM), BF16)
    head_spec = pl.BlockSpec((1, 8, tm, HEAD_DIM), heads)
    return pl.pallas_call(
        _inproj_kernel,
        grid=(n // tm,),
        in_specs=[
            pl.BlockSpec((tm, D_MODEL), row),
            pl.BlockSpec((1, D_MODEL), const),
            pl.BlockSpec((D_MODEL, C_END), const, pipeline_mode=pl.Buffered(1)),
            pl.BlockSpec((512, D_MODEL), const, pipeline_mode=pl.Buffered(1)),
            pl.BlockSpec((tm, 512), pos), pl.BlockSpec((tm, 512), pos),
            pl.BlockSpec((tm, 256), pos), pl.BlockSpec((tm, 256), pos),
            pl.BlockSpec((1, 512), const), pl.BlockSpec((1, 256), const), pl.BlockSpec((1, 512), const),
            pl.BlockSpec((512, 512), const), pl.BlockSpec((512, 512), const),
        ],
        out_specs=[head_spec, head_spec, pl.BlockSpec((1, 512, tm), lambda i: (i // spb, 0, i % spb)), head_spec,
                   pl.BlockSpec((tm, 256), row), pl.BlockSpec((tm, 256), row),
                   pl.BlockSpec((tm, 128), row), pl.BlockSpec((tm, 512), row),
                   pl.BlockSpec((1, 128, tm), lambda i: (i // spb, 0, i % spb))],
        out_shape=[head_shape, head_shape, jax.ShapeDtypeStruct((batch, 512, seq), BF16), head_shape,
                   jax.ShapeDtypeStruct((n, 256), BF16), jax.ShapeDtypeStruct((n, 256), BF16),
                   jax.ShapeDtypeStruct((n, 128), F32), jax.ShapeDtypeStruct((n, 512), BF16),
                   jax.ShapeDtypeStruct((batch, 128, seq), BF16)],
        compiler_params=_cparams(("parallel",)),
        name="inproj",
    )(x2d, g_mix[None, :], w, w_in[:, col["sv"][0]:col["sv"][1]].T.astype(BF16), cosq, sinq, coss, sins, gq, gks, gm,
      _block_diag_ones(512, HEAD_DIM), _block_diag_ones(512, MEM_HEAD_DIM))


def _sb_kernel(q_ref, k_ref, vt_ref, u_ref, o_ref, acc_ref, car_ref, *, tq, blocks_per_step):
    def query_block(r, carry):
        rows = pl.ds(pl.multiple_of(r * tq, tq), tq)
        _sb_sweep(q_ref.at[:, :, rows, :], k_ref, vt_ref, u_ref, o_ref.at[:, rows, :], acc_ref, car_ref,
                  qi=pl.program_id(1) * blocks_per_step + r, tq=tq)
        return carry

    lax.fori_loop(0, blocks_per_step, query_block, 0)


def _sb_sweep(q_ref, k_ref, vt_ref, u_ref, o_ref, acc_ref, car_ref, *, qi, tq):
    rows = lax.broadcasted_iota(I32, (tq, tq), 0)
    cols = lax.broadcasted_iota(I32, (tq, tq), 1)
    dif = rows - cols
    u = u_ref[...]
    acc_ref[...] = jnp.zeros_like(acc_ref)
    car_ref[...] = jnp.zeros_like(car_ref)

    def cond(c):
        kb, mx = c
        return jnp.logical_and(kb >= 0, mx > -SB_CUTOFF)

    def body(c):
        kb, _ = c
        ks = pl.multiple_of(kb * tq, tq)
        earlier = dif < (qi - kb) * tq
        neg_mask = jnp.where(earlier, -1.0, 0.0).astype(BF16)
        heads = range(SB_HEADS)
        z = [_dot_nt(k_ref[0, hd, pl.ds(ks, tq), :], q_ref[0, hd]) for hd in heads]
        ls, lk, between = [], [], []
        for hd in heads:
            zb = z[hd].astype(BF16)
            sp = jnp.maximum(zb, 0.0) + jnp.log(1.0 + jnp.exp(-jnp.abs(zb)))
            ls.append(z[hd] - sp.astype(F32))
            lk.append(sp * neg_mask)
            between.append(_dot(u, lk[hd]))
        for hd in heads:
            rs = slice(hd * HEAD_DIM, (hd + 1) * HEAD_DIM)
            car = car_ref[hd:hd + 1, :]
            w = jnp.where(earlier, jnp.exp(ls[hd] + between[hd] + car), 0.0)
            acc_ref[rs, :] += _dot(vt_ref[0, rs, pl.ds(ks, tq)], w.astype(BF16))
            car_ref[hd:hd + 1, :] = car + (between[hd][0:1, :] + lk[hd][0:1, :].astype(F32))
        return kb - 1, jnp.max(car_ref[...])

    lax.while_loop(cond, body, (qi, jnp.float32(0.0)))
    o_ref[0] = acc_ref[...].T.astype(BF16)


def _strict_tri(n, lower):
    i = np.arange(n)
    m = (i[:, None] > i[None, :]) if lower else (i[:, None] < i[None, :])
    return jnp.asarray(m.astype(np.float32), dtype=BF16)


SB_BLOCKS_PER_STEP = 4


def _sb_attention(sq, sk, svt, tq):
    batch, _, seq, _ = sq.shape
    per_step = min(SB_BLOCKS_PER_STEP, seq // tq)
    return pl.pallas_call(
        functools.partial(_sb_kernel, tq=tq, blocks_per_step=per_step),
        grid=(batch, seq // (tq * per_step)),
        in_specs=[pl.BlockSpec((1, SB_HEADS, tq * per_step, HEAD_DIM), lambda b, i: (b, 0, i, 0)),
                  pl.BlockSpec((1, SB_HEADS, seq, HEAD_DIM), lambda b, i: (b, 0, 0, 0)),
                  pl.BlockSpec((1, BRANCH_W, seq), lambda b, i: (b, 0, 0)),
                  pl.BlockSpec((tq, tq), lambda b, i: (0, 0))],
        out_specs=pl.BlockSpec((1, tq * per_step, BRANCH_W), lambda b, i: (b, i, 0)),
        out_shape=jax.ShapeDtypeStruct((batch, seq, BRANCH_W), BF16),
        scratch_shapes=[pltpu.VMEM((BRANCH_W, tq), F32), pltpu.VMEM((SB_HEADS, tq), F32)],
        compiler_params=_cparams(("parallel", "parallel")),
        name="sb_attention",
    )(sq, sk, svt, _strict_tri(tq, lower=False))


DSA_SEG = 256
DSA_KB = DSA_SEG


def _tree_sum(parts):
    while len(parts) > 1:
        parts = [parts[i] + parts[i + 1] for i in range(0, len(parts) - 1, 2)] + ([parts[-1]] if len(parts) % 2 else [])
    return parts[0]


def _dsa_kernel(dq_ref, iq_ref, iw_ref, kvi_ref, vt_ref, tri_ref, o_ref,
                sc_ref, bias_ref, s_ref, *, tq, topk, nseg_max):
    nseg = pl.program_id(1) + 1

    def query_block(r, carry, ns):
        rows = pl.ds(pl.multiple_of(r * tq, tq), tq)
        _dsa_body(dq_ref.at[:, :, rows, :], iq_ref.at[:, rows, :], iw_ref.at[:, rows, :], kvi_ref, vt_ref, tri_ref,
                  o_ref.at[:, rows, :], sc_ref, bias_ref, s_ref,
                  qs=pl.program_id(1) * DSA_SEG + r * tq, tq=tq, topk=topk, nseg=ns)
        return carry

    for ns in range(1, nseg_max + 1):
        @pl.when(nseg == ns)
        def _(ns=ns):
            lax.fori_loop(0, DSA_SEG // tq, functools.partial(query_block, ns=ns), 0)


def _dsa_body(dq_ref, iq_ref, iw_ref, kvi_ref, vt_ref, tri_ref, o_ref, sc_ref, bias_ref, s_ref,
              *, qs, tq, topk, nseg):
    blocks = [slice(c * DSA_KB, (c + 1) * DSA_KB) for c in range(nseg)]
    iq = iq_ref[0]
    w_t = iw_ref[0].T
    w_row = [w_t[IW_LANE + h:IW_LANE + h + 1, :] * (IDX_HEADS ** -0.5) for h in range(IDX_HEADS)]
    q_chunk = (qs + lax.broadcasted_iota(I32, (DSA_KB, tq), 1)) // CHUNK
    k_chunk = lax.broadcasted_iota(I32, (DSA_KB, tq), 0) // CHUNK

    for c, blk in enumerate(blocks):
        ik = kvi_ref[0, blk, 128:192]
        lg = [_dot_nt(ik, iq[:, h * HEAD_DIM:(h + 1) * HEAD_DIM]) for h in range(IDX_HEADS)]
        sc = jnp.zeros((DSA_KB, tq), F32)
        for h in range(IDX_HEADS):
            sc = sc + w_row[h] * jnp.maximum(lg[h], 0.0)
        admissible = (c * (DSA_KB // CHUNK) + k_chunk) <= q_chunk
        sc_ref[blk, :] = jnp.where(admissible, sc, -jnp.inf)

    def count(pred_fn):
        sub, n_sums = 32, 4
        acc = [jnp.zeros((sub, tq), F32)] * n_sums
        for j in range(nseg * DSA_SEG // sub):
            acc[j % n_sums] = acc[j % n_sums] + jnp.where(pred_fn(sc_ref[j * sub:(j + 1) * sub, :]), 1.0, 0.0)
        return jnp.sum(_tree_sum(acc), axis=0, keepdims=True)

    def key_to_float(key):
        return lax.bitcast_convert_type(jnp.where(key >= 0, key, key ^ 0x7FFFFFFF), F32)

    kf = jnp.float32(topk)
    n_rows = jnp.float32(nseg * DSA_SEG)
    cnt0 = count(lambda s: s >= 0.0)
    t0 = jnp.where(cnt0 >= kf, 0, INT_MIN).astype(I32)
    cnt_t0 = jnp.where(cnt0 >= kf, cnt0, n_rows)

    def bit_step(i, carry):
        t, cnt_t = carry
        cand = t + lax.shift_left(jnp.int32(1), 30 - i)
        cand_f = key_to_float(cand)
        cnt = jnp.where(cand <= KEY_NEG_INF, n_rows, count(lambda s: s >= cand_f))
        take = cnt >= kf
        return jnp.where(take, cand, t), jnp.where(take, cnt, cnt_t)

    thr_key, cnt_thr = lax.fori_loop(0, 31, bit_step, (t0, cnt_t0))
    thr = jnp.where(thr_key <= KEY_NEG_INF, -jnp.inf, key_to_float(thr_key))
    tri = tri_ref[...]

    surplus = jnp.logical_or(jnp.max(cnt_thr) > kf, jnp.min(thr) == -jnp.inf)

    @pl.when(surplus)
    def _():
        need = kf - count(lambda s: s > thr)
        prefix = jnp.zeros((1, tq), F32)
        for blk in blocks:
            sc = sc_ref[blk, :]
            eqf = jnp.where(sc == thr, 1.0, 0.0)
            rank = _dot(tri, eqf.astype(BF16)) + prefix
            tie = jnp.where(rank < need, eqf, 0.0)
            sel = jnp.where(sc > thr, 1.0, tie)
            bias_ref[blk, :] = jnp.where(sc > -jnp.inf, (sel - 1.0) * (-NEG_BIG), NEG_BIG)
            prefix = prefix + jnp.sum(eqf, axis=0, keepdims=True)

    @pl.when(jnp.logical_not(surplus))
    def _():
        for blk in blocks:
            bias_ref[blk, :] = jnp.where(sc_ref[blk, :] >= thr, 0.0, NEG_BIG)

    q8 = dq_ref[0].reshape(DSA_HEADS * tq, HEAD_DIM)
    m = jnp.full((1, DSA_HEADS * tq), NEG_BIG, F32)
    for blk in blocks:
        b = bias_ref[blk, :]
        s = _dot_nt(kvi_ref[0, blk, 0:HEAD_DIM], q8) + jnp.concatenate([b] * DSA_HEADS, axis=1)
        s_ref[blk, :] = s
        m = jnp.maximum(m, jnp.max(s, axis=0, keepdims=True))
    acc = jnp.zeros((128, DSA_HEADS * tq), F32)
    for blk in blocks:
        p = jnp.exp(s_ref[blk, :] - m)
        acc = acc + _dot(vt_ref[0, :, blk], p.astype(BF16))
    for hd in range(DSA_HEADS):
        a = acc[:, hd * tq:(hd + 1) * tq]
        o = (a / a[0:1, :]).T
        o_ref[0, :, hd * HEAD_DIM:(hd + 1) * HEAD_DIM] = o[:, HEAD_DIM:].astype(BF16)


def _dsa_attention(dq, iq, iw, kvi, vt, tq, topk):
    batch, _, seq, _ = dq.shape
    return pl.pallas_call(
        functools.partial(_dsa_kernel, tq=tq, topk=topk, nseg_max=seq // DSA_SEG),
        grid=(batch, seq // DSA_SEG),
        in_specs=[
            pl.BlockSpec((1, DSA_HEADS, DSA_SEG, HEAD_DIM), lambda b, i: (b, 0, i, 0)),
            pl.BlockSpec((1, DSA_SEG, 256), lambda b, i: (b, i, 0)),
            pl.BlockSpec((1, DSA_SEG, 128), lambda b, i: (b, i, 0)),
            pl.BlockSpec((1, seq, 256), lambda b, i: (b, 0, 0)),
            pl.BlockSpec((1, 128, seq), lambda b, i: (b, 0, 0)),
            pl.BlockSpec((DSA_KB, DSA_KB), lambda b, i: (0, 0)),
        ],
        out_specs=pl.BlockSpec((1, DSA_SEG, BRANCH_W), lambda b, i: (b, i, 0)),
        out_shape=jax.ShapeDtypeStruct((batch, seq, BRANCH_W), BF16),
        scratch_shapes=[pltpu.VMEM((seq, tq), F32), pltpu.VMEM((seq, tq), F32),
                        pltpu.VMEM((seq, DSA_HEADS * tq), F32)],
        compiler_params=_cparams(("parallel", "parallel")),
        name="dsa_attention",
    )(dq, iq.reshape(batch, seq, 256), iw.reshape(batch, seq, 128), kvi.reshape(batch, seq, 256), vt,
      _strict_tri(DSA_KB, lower=True))


def _memkv_kernel(m_ref, g_ref, w_ref, gk_ref, bd_ref, mk_ref, mv_ref):
    x = m_ref[...]
    ms = jnp.mean(x * x, axis=-1, keepdims=True)
    h = (x * lax.rsqrt(ms + EPS) * g_ref[...]).astype(BF16)
    mw = MEM_HEADS * MEM_HEAD_DIM
    k = _dot(h, w_ref[:, :mw])
    msk = _dot_split(k * k, bd_ref[...]) * (1.0 / MEM_HEAD_DIM)
    mk_ref[...] = (k * lax.rsqrt(msk + EPS) * gk_ref[...]).astype(BF16)
    mv_ref[...] = _dot(h, w_ref[:, mw:]).astype(BF16)


def _mem_kv(mem2d, g_mem, w_mem_kv, g_k_mem, tm):
    n = mem2d.shape[0]
    mw = MEM_HEADS * MEM_HEAD_DIM
    row = lambda i: (i, 0)
    const = lambda i: (0, 0)
    return pl.pallas_call(
        _memkv_kernel,
        grid=(n // tm,),
        in_specs=[pl.BlockSpec((tm, D_MODEL), row), pl.BlockSpec((1, D_MODEL), const),
                  pl.BlockSpec((D_MODEL, 2 * mw), const), pl.BlockSpec((1, mw), const),
                  pl.BlockSpec((mw, mw), const)],
        out_specs=[pl.BlockSpec((tm, mw), row), pl.BlockSpec((tm, mw), row)],
        out_shape=[jax.ShapeDtypeStruct((n, mw), BF16), jax.ShapeDtypeStruct((n, mw), BF16)],
        compiler_params=_cparams(("parallel",)),
        name="mem_kv",
    )(mem2d, g_mem[None, :], w_mem_kv.astype(BF16), jnp.tile(g_k_mem, MEM_HEADS)[None, :],
      _block_diag_ones(mw, MEM_HEAD_DIM))


def _memattn_kernel(q_ref, k_ref, v_ref, o_ref):
    for hd in range(MEM_HEADS):
        sl = slice(hd * MEM_HEAD_DIM, (hd + 1) * MEM_HEAD_DIM)
        s = _dot_nt(q_ref[0, :, sl], k_ref[0, :, sl]) * (MEM_HEAD_DIM ** -0.5)
        p = jnp.exp(s - jnp.max(s, axis=-1, keepdims=True))
        o = _dot(p.astype(BF16), v_ref[0, :, sl]) / jnp.sum(p, axis=-1, keepdims=True)
        o_ref[0, :, sl] = o.astype(BF16)


def _mem_attention(mq, mk, mv, tq):
    batch, seq, mw = mq.shape
    mlen = mk.shape[1]
    kspec = pl.BlockSpec((1, mlen, mw), lambda b, i: (b, 0, 0))
    return pl.pallas_call(
        _memattn_kernel,
        grid=(batch, seq // tq),
        in_specs=[pl.BlockSpec((1, tq, mw), lambda b, i: (b, i, 0)), kspec, kspec],
        out_specs=pl.BlockSpec((1, tq, mw), lambda b, i: (b, i, 0)),
        out_shape=jax.ShapeDtypeStruct((batch, seq, mw), BF16),
        compiler_params=_cparams(("parallel", "parallel")),
        name="mem_attention",
    )(mq, mk, mv)


def _merge_kernel(x_ref, g_ref, osb_ref, odsa_ref, omem_ref, wg_ref, bg_ref, wb_ref, wo_ref,
                  gf_ref, wrh_ref, wrl_ref, br_ref, tri_ref,
                  x2_ref, h2_ref, idx_ref, gate_ref, rank_ref, cnt_ref, carry_ref):
    x = x_ref[...]
    ms = jnp.mean(x * x, axis=-1, keepdims=True)
    h = (x * lax.rsqrt(ms + EPS) * g_ref[...]).astype(BF16)
    merged = None
    for n, o_ref in enumerate((osb_ref, odsa_ref, omem_ref)):
        gate = jax.nn.sigmoid(_dot(h, wg_ref[n]) + bg_ref[n])
        term = gate * _dot(o_ref[...], wb_ref[n])
        merged = term if merged is None else merged + term
    x2 = x + _dot(merged.astype(BF16), wo_ref[...])
    x2_ref[...] = x2

    ms2 = jnp.mean(x2 * x2, axis=-1, keepdims=True)
    h2 = x2 * lax.rsqrt(ms2 + EPS) * gf_ref[...]
    _store_token_tiles(h2_ref, h2)
    hi, lo = _split_bf16(h2)
    logits = _dot(hi, wrh_ref[...]) + _dot(hi, wrl_ref[...]) + _dot(lo, wrh_ref[...]) + br_ref[...]

    lane = lax.broadcasted_iota(I32, logits.shape, 1)
    vals = logits
    top_v, top_i = [], []
    for _ in range(TOP_K):
        mx = jnp.max(vals, axis=-1, keepdims=True)
        ix = jnp.min(jnp.where(vals == mx, lane, N_EXPERTS), axis=-1, keepdims=True)
        top_v.append(mx)
        top_i.append(ix)
        vals = jnp.where(lane == ix, -jnp.inf, vals)
    ex = [jnp.exp(v - top_v[0]) for v in top_v]
    den = ex[0] + ex[1] + ex[2] + ex[3]
    for k in range(TOP_K):
        idx_ref[:, k:k + 1] = top_i[k]
        gate_ref[:, k:k + 1] = ex[k] / den

    @pl.when(pl.program_id(0) == 0)
    def _():
        carry_ref[...] = jnp.zeros_like(carry_ref)

    lane_e = lax.broadcasted_iota(I32, (x.shape[0], LANES), 1)
    hits = [lane_e == top_i[k] for k in range(TOP_K)]
    onehot = sum(jnp.where(hk, 1.0, 0.0) for hk in hits)
    before = _dot(tri_ref[...], onehot.astype(BF16)) + carry_ref[...]
    for k in range(TOP_K):
        rank_ref[:, k:k + 1] = jnp.sum(jnp.where(hits[k], before, 0.0), axis=-1, keepdims=True).astype(I32)
    carry_ref[...] += jnp.sum(onehot, axis=0, keepdims=True)
    cnt_ref[...] = carry_ref[...]


def _merge_route(x2d, g_mix, o_sb, o_dsa, o_mem, w_gate, b_gate, w_branch, w_out, g_ffn, w_router, b_router, tm):
    n = x2d.shape[0]
    row = lambda i: (i, 0)
    c2 = lambda i: (0, 0)
    c3 = lambda i: (0, 0, 0)
    wr_hi = w_router.astype(BF16)
    wr_lo = (w_router - wr_hi.astype(F32)).astype(BF16)
    return pl.pallas_call(
        _merge_kernel,
        grid=(n // tm,),
        in_specs=[pl.BlockSpec((tm, D_MODEL), row), pl.BlockSpec((1, D_MODEL), c2),
                  pl.BlockSpec((tm, BRANCH_W), row), pl.BlockSpec((tm, BRANCH_W), row),
                  pl.BlockSpec((tm, BRANCH_W), row),
                  pl.BlockSpec((3, D_MODEL, D_MODEL), c3, pipeline_mode=pl.Buffered(1)),
                  pl.BlockSpec((3, 1, D_MODEL), c3),
                  pl.BlockSpec((3, BRANCH_W, D_MODEL), c3, pipeline_mode=pl.Buffered(1)),
                  pl.BlockSpec((D_MODEL, D_MODEL), c2, pipeline_mode=pl.Buffered(1)),
                  pl.BlockSpec((1, D_MODEL), c2), pl.BlockSpec((D_MODEL, N_EXPERTS), c2),
                  pl.BlockSpec((D_MODEL, N_EXPERTS), c2), pl.BlockSpec((1, N_EXPERTS), c2),
                  pl.BlockSpec((tm, tm), c2, pipeline_mode=pl.Buffered(1))],
        out_specs=[pl.BlockSpec((tm, D_MODEL), row), pl.BlockSpec((tm * TOKEN_TILE, LANES), row),
                   pl.BlockSpec((tm, TOP_K), row), pl.BlockSpec((tm, TOP_K), row),
                   pl.BlockSpec((tm, TOP_K), row), pl.BlockSpec((1, LANES), c2)],
        out_shape=[jax.ShapeDtypeStruct((n, D_MODEL), F32), jax.ShapeDtypeStruct((n * TOKEN_TILE, LANES), F32),
                   jax.ShapeDtypeStruct((n, TOP_K), I32), jax.ShapeDtypeStruct((n, TOP_K), F32),
                   jax.ShapeDtypeStruct((n, TOP_K), I32), jax.ShapeDtypeStruct((1, LANES), F32)],
        scratch_shapes=[pltpu.VMEM((1, LANES), F32)],
        compiler_params=_cparams(("arbitrary",)),
        name="merge_route",
    )(x2d, g_mix[None, :], o_sb, o_dsa, o_mem, w_gate.astype(BF16), b_gate[:, None, :],
      w_branch.astype(BF16), w_out.astype(BF16), g_ffn[None, :], wr_hi, wr_lo, b_router[None, :],
      _strict_tri(tm, lower=True))


DISPATCH_TOKENS = 4096
COMBINE_TOKENS = 512


def _dispatch_kernel(dest_ref, pad_ref, h_ref, xs_ref, zero_ref, sem, *, n_tok, n_pad):
    n_copies = n_tok * TOP_K + n_pad
    zero_ref[...] = jnp.zeros_like(zero_ref)

    def fill(j, c):
        for r in range(2):
            dst = pl.multiple_of(pad_ref[0, 0, 2 * j + r] * TOKEN_TILE, TOKEN_TILE)
            pltpu.make_async_copy(zero_ref, xs_ref.at[pl.ds(dst, TOKEN_TILE)], sem).start(priority=r)
        return c

    lax.fori_loop(0, n_pad // 2, fill, 0, unroll=2)

    def start(t, c):
        src = pl.multiple_of(t * TOKEN_TILE, TOKEN_TILE)
        for k in range(TOP_K):
            dst = pl.multiple_of(dest_ref[0, 0, t * TOP_K + k] * TOKEN_TILE, TOKEN_TILE)
            pltpu.make_async_copy(h_ref.at[pl.ds(src, TOKEN_TILE)], xs_ref.at[pl.ds(dst, TOKEN_TILE)],
                                  sem).start(priority=k % 2)
        return c

    lax.fori_loop(0, n_tok, start, 0, unroll=2)
    total = n_copies * TOKEN_TILE
    pltpu.make_async_copy(xs_ref.at[pl.ds(0, total)], xs_ref.at[pl.ds(0, total)], sem).wait()


def _dispatch(dest, pad_slots, h2t, n_slots):
    n = h2t.shape[0] // TOKEN_TILE
    n_tok = min(DISPATCH_TOKENS, n)
    steps = n // n_tok
    per = n_tok * TOP_K
    n_pad = pad_slots.shape[0] // steps
    assert n_tok * steps == n and n_pad * steps == pad_slots.shape[0] and n_pad % 2 == 0
    return pl.pallas_call(
        functools.partial(_dispatch_kernel, n_tok=n_tok, n_pad=n_pad),
        grid=(steps,),
        in_specs=[pl.BlockSpec((1, 1, per), lambda i: (i, 0, 0), memory_space=pltpu.SMEM),
                  pl.BlockSpec((1, 1, n_pad), lambda i: (i, 0, 0), memory_space=pltpu.SMEM),
                  pl.BlockSpec((n_tok * TOKEN_TILE, LANES), lambda i: (i, 0))],
        out_specs=pl.BlockSpec(memory_space=pl.ANY),
        out_shape=jax.ShapeDtypeStruct((n_slots * TOKEN_TILE, LANES), F32),
        scratch_shapes=[pltpu.VMEM((TOKEN_TILE, LANES), F32), pltpu.SemaphoreType.DMA(())],
        compiler_params=_cparams(("arbitrary",)),
        name="moe_dispatch",
    )(dest.reshape(steps, 1, per), pad_slots.reshape(steps, 1, n_pad), h2t)


def _expert_kernel(blk_e_ref, nused_ref, first_ref, ring_ref, next_e_ref, x_ref, w1_hbm, b1_ref, w2_hbm, b2_ref, y_ref,
                   xb_ref, w1b_ref, w2b_ref, w1f_ref, w2f_ref, sem):
    i = pl.program_id(0)

    def weight_copies(e, slot):
        return (pltpu.make_async_copy(w1_hbm.at[e], w1f_ref.at[slot], sem.at[0, slot]),
                pltpu.make_async_copy(w2_hbm.at[e], w2f_ref.at[slot], sem.at[1, slot]))

    @pl.when(first_ref[i] == 1)
    def _():
        slot = ring_ref[i]

        @pl.when(i == 0)
        def _():
            for c in weight_copies(blk_e_ref[i], slot):
                c.start()

        @pl.when(next_e_ref[i] >= 0)
        def _():
            for c in weight_copies(next_e_ref[i], 1 - slot):
                c.start()

        for c in weight_copies(blk_e_ref[i], slot):
            c.wait()
        w1b_ref[...] = w1f_ref[slot].astype(BF16)
        w2b_ref[...] = w2f_ref[slot].astype(BF16)

    @pl.when(i < nused_ref[0])
    def _():
        for s in range(TOKEN_TILE):
            xb_ref[:, s * LANES:(s + 1) * LANES] = _load_token_tiles(x_ref, 0, MOE_BLOCK, s).astype(BF16)
        hb = _dot(xb_ref[...], w1b_ref[...]) + b1_ref[0]
        g = jnp.minimum(hb[:, :D_EXPERT], SWIGLU_LIMIT)
        u = jnp.clip(hb[:, D_EXPERT:], -SWIGLU_LIMIT, SWIGLU_LIMIT)
        act = (u + 1.0) * (g * jax.nn.sigmoid(SWIGLU_ALPHA * g))
        _store_token_tiles(y_ref, _dot(act.astype(BF16), w2b_ref[...]) + b2_ref[0])

    @pl.when(pl.program_id(0) >= nused_ref[0])
    def _():
        y_ref[...] = jnp.zeros_like(y_ref)


def _experts(blk_e, n_used, xs, w_e_in, b_e_in, w_e_out, b_e_out):
    n_slots = xs.shape[0] // TOKEN_TILE
    nblk = n_slots // MOE_BLOCK
    idx = jnp.arange(nblk, dtype=I32)
    used = idx < n_used[0]
    first = used & ((idx == 0) | (blk_e != jnp.concatenate([blk_e[:1], blk_e[:-1]])))
    ring = (jnp.cumsum(first.astype(I32)) - 1) % 2
    later_first = first[None, :] & (idx[None, :] > idx[:, None])
    next_idx = jnp.min(jnp.where(later_first, idx[None, :], nblk), axis=1)
    next_e = jnp.where(next_idx < nblk, blk_e[jnp.minimum(next_idx, nblk - 1)], -1).astype(I32)
    n_pre = 5
    slot_block = pl.BlockSpec((MOE_BLOCK * TOKEN_TILE, LANES), lambda i, be, *_: (i, 0))
    grid_spec = pltpu.PrefetchScalarGridSpec(
        num_scalar_prefetch=n_pre,
        grid=(nblk,),
        in_specs=[slot_block,
                  pl.BlockSpec(memory_space=pl.ANY),
                  pl.BlockSpec((1, 1, 2 * D_EXPERT), lambda i, be, *_: (be[i], 0, 0)),
                  pl.BlockSpec(memory_space=pl.ANY),
                  pl.BlockSpec((1, 1, D_MODEL), lambda i, be, *_: (be[i], 0, 0))],
        out_specs=slot_block,
        scratch_shapes=[pltpu.VMEM((MOE_BLOCK, D_MODEL), BF16), pltpu.VMEM((D_MODEL, 2 * D_EXPERT), BF16),
                        pltpu.VMEM((D_EXPERT, D_MODEL), BF16),
                        pltpu.VMEM((2, D_MODEL, 2 * D_EXPERT), F32), pltpu.VMEM((2, D_EXPERT, D_MODEL), F32),
                        pltpu.SemaphoreType.DMA((2, 2))],
    )
    return pl.pallas_call(
        _expert_kernel,
        grid_spec=grid_spec,
        out_shape=jax.ShapeDtypeStruct((n_slots * TOKEN_TILE, LANES), F32),
        compiler_params=_cparams(("arbitrary",)),
        name="moe_experts",
    )(blk_e, n_used, first.astype(I32), ring.astype(I32), next_e,
      xs, w_e_in, b_e_in[:, None, :], w_e_out, b_e_out[:, None, :])


def _combine_kernel(dest_ref, dest_one_ref, dest_ahead_ref, x_ref, gate_ref, y_ref, o_ref, buf_ref, sem):
    n_copies = COMBINE_TOKENS * TOP_K
    rows = 32
    i = pl.program_id(0)
    slot, slot1, slot2 = i % 3, (i + 1) % 3, (i + 2) % 3
    buf = buf_ref.at[slot]

    def start_rows(idx_ref, dst_buf, dma_sem, t0):
        for t in range(rows):
            for k in range(TOP_K):
                src = pl.multiple_of(idx_ref[0, 0, (t0 + t) * TOP_K + k] * TOKEN_TILE, TOKEN_TILE)
                dst = pl.multiple_of((k * COMBINE_TOKENS + t0 + t) * TOKEN_TILE, TOKEN_TILE)
                pltpu.make_async_copy(y_ref.at[pl.ds(src, TOKEN_TILE)], dst_buf.at[pl.ds(dst, TOKEN_TILE)],
                                      dma_sem).start(priority=k % 2)

    def wait_rows(dst_buf, dma_sem):
        pltpu.make_async_copy(y_ref.at[pl.ds(0, n_copies * TOKEN_TILE)], dst_buf, dma_sem).wait()

    @pl.when(i == 0)
    def _():
        def first(c, z):
            start_rows(dest_ref, buf_ref.at[0], sem.at[0], c * rows)
            start_rows(dest_one_ref, buf_ref.at[1], sem.at[1], c * rows)
            return z
        lax.fori_loop(0, COMBINE_TOKENS // rows, first, 0)

    wait_rows(buf, sem.at[slot])

    def sum_pass(c, z):
        r0 = pl.multiple_of(c * rows, rows)
        start_rows(dest_ahead_ref, buf_ref.at[slot2], sem.at[slot2], r0)
        gate = gate_ref[pl.ds(r0, rows), :]
        gates = [jnp.broadcast_to(gate[:, k:k + 1], (rows, LANES)) for k in range(TOP_K)]
        for s in range(TOKEN_TILE):
            out = x_ref[pl.ds(r0, rows), s * LANES:(s + 1) * LANES]
            for k in range(TOP_K):
                out = out + gates[k] * _load_token_tiles(buf, k * COMBINE_TOKENS + r0, rows, s)
            o_ref[pl.ds(r0, rows), s * LANES:(s + 1) * LANES] = out
        return z

    lax.fori_loop(0, COMBINE_TOKENS // rows, sum_pass, 0)

    @pl.when(i == pl.num_programs(0) - 1)
    def _():
        wait_rows(buf_ref.at[slot1], sem.at[slot1])
        wait_rows(buf_ref.at[slot2], sem.at[slot2])


def _combine(dest, x2, gate, ys):
    n = x2.shape[0]
    steps = n // COMBINE_TOKENS
    per = COMBINE_TOKENS * TOP_K
    dest3 = dest.reshape(steps, 1, per)
    return pl.pallas_call(
        _combine_kernel,
        grid=(steps,),
        in_specs=[pl.BlockSpec((1, 1, per), lambda i: (i, 0, 0), memory_space=pltpu.SMEM),
                  pl.BlockSpec((1, 1, per), lambda i: (min(1, steps - 1), 0, 0), memory_space=pltpu.SMEM),
                  pl.BlockSpec((1, 1, per), lambda i: (jnp.minimum(i + 2, steps - 1), 0, 0), memory_space=pltpu.SMEM),
                  pl.BlockSpec((COMBINE_TOKENS, D_MODEL), lambda i: (i, 0)),
                  pl.BlockSpec((COMBINE_TOKENS, TOP_K), lambda i: (i, 0)),
                  pl.BlockSpec(memory_space=pl.ANY)],
        out_specs=pl.BlockSpec((COMBINE_TOKENS, D_MODEL), lambda i: (i, 0)),
        out_shape=jax.ShapeDtypeStruct((n, D_MODEL), F32),
        scratch_shapes=[pltpu.VMEM((3, TOP_K * COMBINE_TOKENS * TOKEN_TILE, LANES), F32),
                        pltpu.SemaphoreType.DMA((3,))],
        compiler_params=_cparams(("arbitrary",)),
        name="moe_combine",
    )(dest3, dest3, dest3, x2, gate, ys)


def _moe(x2, h2, top_idx, gate, rank, counts, w_e_in, b_e_in, w_e_out, b_e_out):
    n = x2.shape[0]
    counts = counts[0, :N_EXPERTS].astype(I32)
    padded = (counts + MOE_BLOCK - 1) // MOE_BLOCK * MOE_BLOCK
    pend = jnp.cumsum(padded)
    pstart = pend - padded
    nblk = -(-(n * TOP_K) // MOE_BLOCK) + N_EXPERTS
    blk_start = jnp.arange(nblk, dtype=I32) * MOE_BLOCK
    blk_e = jnp.minimum(jnp.sum((pend[None, :] <= blk_start[:, None]).astype(I32), axis=1), N_EXPERTS - 1)
    n_used = (pend[-1:] // MOE_BLOCK).astype(I32)
    onehot = top_idx[:, :, None] == jnp.arange(N_EXPERTS, dtype=I32)[None, None, :]
    dest = rank + jnp.sum(jnp.where(onehot, pstart[None, None, :], 0), axis=-1)
    n_slots = nblk * MOE_BLOCK
    pad_len = padded - counts
    pad_end = jnp.cumsum(pad_len)
    base = jnp.concatenate([pstart + counts - (pad_end - pad_len), pend[-1:] - pad_end[-1:]])
    j = jnp.arange(n_slots - n * TOP_K, dtype=I32)
    group = jnp.sum((pad_end[None, :] <= j[:, None]).astype(I32), axis=1)
    group_hot = group[:, None] == jnp.arange(N_EXPERTS + 1, dtype=I32)[None, :]
    pad_slots = j + jnp.sum(jnp.where(group_hot, base[None, :], 0), axis=1)
    xs = _dispatch(dest, pad_slots, h2, n_slots)
    ys = _experts(blk_e, n_used, xs, w_e_in, b_e_in, w_e_out, b_e_out)
    return _combine(dest, x2, gate, ys)


def _layer(x, mem, g_mix, w_in, g_q_dsa, g_k_dsa, g_q_mem, g_k_mem, g_mem, w_mem_kv, w_gate, b_gate,
           w_branch, w_out, g_ffn, w_router, b_router, w_e_in, b_e_in, w_e_out, b_e_out):
    batch, seq, _ = x.shape
    n = batch * seq
    topk = min(DSA_TOPK_MAX, seq // 4)
    x2d = x.reshape(n, D_MODEL)
    tm_dense = min(DENSE_ROWS, seq)
    sq, sk, sv, dq, iq, kvi, iw, mq, vt = _inproj(x2d, batch, seq, g_mix, w_in, g_q_dsa, g_k_dsa, g_q_mem, tm_dense)
    o_sb = _sb_attention(sq, sk, sv, min(256, seq))
    o_dsa = _dsa_attention(dq, iq, iw, kvi, vt, 128, topk)
    mlen = mem.shape[1]
    mk, mv = _mem_kv(mem.reshape(batch * mlen, D_MODEL), g_mem, w_mem_kv, g_k_mem, min(512, batch * mlen))
    mw = MEM_HEADS * MEM_HEAD_DIM
    o_mem = _mem_attention(mq.reshape(batch, seq, mw), mk.reshape(batch, mlen, mw), mv.reshape(batch, mlen, mw), seq)
    x2, h2, top_idx, gate, rank, counts = _merge_route(
        x2d, g_mix, o_sb.reshape(n, BRANCH_W), o_dsa.reshape(n, BRANCH_W), o_mem.reshape(n, mw),
        w_gate, b_gate, w_branch, w_out, g_ffn, w_router, b_router, tm_dense)
    out = _moe(x2, h2, top_idx, gate, rank, counts, w_e_in, b_e_in, w_e_out, b_e_out)
    return out.reshape(batch, seq, D_MODEL)


def kernel(x, mem, g_mix, w_in, g_q_dsa, g_k_dsa, g_q_mem, g_k_mem, g_mem, w_mem_kv, w_gate, b_gate, w_branch, w_out, g_ffn, w_router, b_router, w_e_in, b_e_in, w_e_out, b_e_out):
    for l in range(g_mix.shape[0]):
        x = _layer(x, mem, g_mix[l], w_in[l], g_q_dsa[l], g_k_dsa[l], g_q_mem[l], g_k_mem[l], g_mem[l],
                   w_mem_kv[l], w_gate[l], b_gate[l], w_branch[l], w_out[l], g_ffn[l], w_router[l],
                   b_router[l], w_e_in[l], b_e_in[l], w_e_out[l], b_e_out[l])
    return x
```
